```python
import math
import jax
import jax.numpy as jnp
from jax import lax

D_MODEL = 1024
BATCH = 16
SEQ = 256
DEPTH = 2
DEC_BATCH = 2
DEC_SEQ = 2048
PAST_LEN = 256

F32 = jnp.float32
GRID_W = 64
BRANCH = D_MODEL // 4
MIX_WIDTH = 4 * BRANCH
EPS = 1e-6
ROPE_BASE = 10000.0
Q_BLOCK = 128
DA_HEADS = 4
DA_V = BRANCH // DA_HEADS
DA_QK = DA_V // 2
S5_CH = 16
S5_GROUPS = BRANCH // S5_CH
S5_STATE = 64
S5_DT_MIN = 1e-3
S5_DT_MAX = 1e-1
HG_HEADS = 4
HG_DK = BRANCH // HG_HEADS
HG_DV = BRANCH // HG_HEADS
HG_CHUNK = 64
MLA_HEADS = 4
MLA_NOPE = 64
MLA_ROPE = 32
MLA_V = BRANCH // MLA_HEADS
MLA_Q_RANK = 192
MLA_KV_RANK = 128

IN_SIZES = (
    DA_HEADS * 2 * DA_QK,
    DA_HEADS * 2 * DA_QK,
    DA_HEADS * DA_V,
    BRANCH,
    BRANCH,
    BRANCH,
    HG_HEADS * HG_DK,
    HG_HEADS * HG_DK,
    HG_HEADS * HG_DK,
    HG_HEADS * HG_DV,
    BRANCH,
    MLA_Q_RANK,
    MLA_KV_RANK,
    MLA_ROPE,
    BRANCH,
)
IN_WIDTH = sum(IN_SIZES)

kernel_name = "hybrid_diff_s5_hgrn2_mla_prefix_step"


def rms_norm(x):
    xf = x.astype(F32)
    return xf * lax.rsqrt(jnp.mean(xf * xf, axis=-1, keepdims=True) + EPS)


def split_columns(z):
    out, start = [], 0
    for n in IN_SIZES:
        out.append(z[..., start:start + n])
        start += n
    return out


def rope_2d(x, row, col):
    half = x.shape[-1] // 2
    nf = half // 2
    inv_freq = ROPE_BASE ** (-jnp.arange(nf, dtype=F32) / nf)
    bshape = (1, x.shape[1]) + (1,) * (x.ndim - 3) + (nf,)
    xf = x.astype(F32)

    def rot(xp, pos):
        ang = (pos.astype(F32)[:, None] * inv_freq).reshape(bshape)
        cos, sin = jnp.cos(ang), jnp.sin(ang)
        x1, x2 = xp[..., :nf], xp[..., nf:]
        return jnp.concatenate([x1 * cos - x2 * sin, x1 * sin + x2 * cos], axis=-1)

    return jnp.concatenate([rot(xf[..., :half], row), rot(xf[..., half:], col)], axis=-1).astype(x.dtype)


def over_query_blocks(fn, q):
    B, L = q.shape[:2]
    nb = L // Q_BLOCK
    qb = jnp.moveaxis(q.reshape((B, nb, Q_BLOCK) + q.shape[2:]), 1, 0)
    o = jnp.moveaxis(lax.map(fn, qb), 0, 1)
    return o.reshape((B, L) + o.shape[3:])


def blocked_attention(q, k, v, scale):
    def one(qb):
        s = jnp.einsum("bqhd,bkhd->bhqk", qb, k).astype(F32) * scale
        p = jax.nn.softmax(s, axis=-1).astype(v.dtype)
        return jnp.einsum("bhqk,bkhd->bqhd", p, v)
    return over_query_blocks(one, q)


def diff_attn_branch(q_in, k_in, v_in, gate, lam_vec, norm_g, lam_init, pos, ctx):
    B, L, _ = q_in.shape
    dt = q_in.dtype
    q = q_in.reshape(B, L, DA_HEADS, 2, DA_QK)
    k = k_in.reshape(B, L, DA_HEADS, 2, DA_QK)
    v = v_in.reshape(B, L, DA_HEADS, DA_V)
    if ctx is None:
        q_r, k_all, v_all = q, k, v
    else:
        ck, cv = ctx
        q_r = rope_2d(q, *pos)
        ck = ck.astype(dt).reshape(ck.shape[:3] + (2, DA_QK))
        k_all = jnp.concatenate([rope_2d(k, *pos), ck], axis=1)
        v_all = jnp.concatenate([v, cv.astype(dt)], axis=1)
    lv = lam_vec.astype(F32)
    lam = jnp.exp(jnp.sum(lv[0] * lv[1])) - jnp.exp(jnp.sum(lv[2] * lv[3])) + lam_init
    sm_scale = DA_QK ** -0.5

    def block(qb):
        s = jnp.einsum("bqhcd,bkhcd->bchqk", qb, k_all).astype(F32) * sm_scale
        p = jax.nn.softmax(s, axis=-1)
        a = (p[:, 0] - lam * p[:, 1]).astype(v_all.dtype)
        return jnp.einsum("bhqk,bkhd->bqhd", a, v_all)

    o = over_query_blocks(block, q_r)
    o = rms_norm(o) * norm_g.astype(F32) * (1.0 - lam_init)
    out = o.reshape(B, L, BRANCH) * jax.nn.silu(gate.astype(F32))
    return out.astype(dt), (k.reshape(B, L, DA_HEADS, 2 * DA_QK), v)


def s5_discretize(a_re, a_im, log_dt, b_re, b_im):
    a_re, a_im = a_re.astype(F32), a_im.astype(F32)
    step = jnp.exp(log_dt.astype(F32))[:, None]
    mag = jnp.exp(a_re * step)
    ab_re, ab_im = mag * jnp.cos(a_im * step), mag * jnp.sin(a_im * step)
    den = a_re * a_re + a_im * a_im
    f_re = ((ab_re - 1.0) * a_re + ab_im * a_im) / den
    f_im = (ab_im * a_re - (ab_re - 1.0) * a_im) / den
    b_re, b_im = b_re.astype(F32), b_im.astype(F32)
    bb_re = f_re[..., None] * b_re - f_im[..., None] * b_im
    bb_im = f_re[..., None] * b_im + f_im[..., None] * b_re
    return ab_re, ab_im, bb_re, bb_im


def s5_scan(u, ab_re, ab_im, bb_re, bb_im, h0, reverse):
    bu_re = jnp.einsum("gph,blgh->blgp", bb_re, u)
    bu_im = jnp.einsum("gph,blgh->blgp", bb_im, u)
    a_re = jnp.broadcast_to(ab_re, bu_re.shape)
    a_im = jnp.broadcast_to(ab_im, bu_re.shape)

    def combine(e1, e2):
        a1r, a1i, b1r, b1i = e1
        a2r, a2i, b2r, b2i = e2
        return (a2r * a1r - a2i * a1i, a2r * a1i + a2i * a1r,
                a2r * b1r - a2i * b1i + b2r, a2r * b1i + a2i * b1r + b2i)

    ar, ai, hr, hi = lax.associative_scan(combine, (a_re, a_im, bu_re, bu_im), axis=1, reverse=reverse)
    if h0 is not None:
        h0r, h0i = h0[0][:, None], h0[1][:, None]
        hr, hi = hr + ar * h0r - ai * h0i, hi + ar * h0i + ai * h0r
    return hr, hi


def s5_branch(u, gate, P, l, h0):
    B, L, _ = u.shape
    uf = u.astype(F32)
    ug = uf.reshape(B, L, S5_GROUPS, S5_CH)
    y = uf * P["s5_d"][l].astype(F32)
    finals = []
    for d in range(2):
        ab_re, ab_im, bb_re, bb_im = s5_discretize(P["s5_a_re"][l, d], P["s5_a_im"][l, d], P["s5_log_dt"][l, d],
                                                   P["s5_b_re"][l, d], P["s5_b_im"][l, d])
        init = None if h0 is None else (h0[:, d, ..., 0].astype(F32), h0[:, d, ..., 1].astype(F32))
        hr, hi = s5_scan(ug, ab_re, ab_im, bb_re, bb_im, init, reverse=(d == 1))
        c_re, c_im = P["s5_c_re"][l, d].astype(F32), P["s5_c_im"][l, d].astype(F32)
        y = y + (jnp.einsum("ghp,blgp->blgh", c_re, hr)
                 - jnp.einsum("ghp,blgp->blgh", c_im, hi)).reshape(B, L, BRANCH)
        if h0 is None:
            t = L - 1 if d == 0 else 0
            finals.append(jnp.stack([hr[:, t], hi[:, t]], axis=-1))
    glu = jax.nn.gelu(y) @ P["s5_w_glu"][l].astype(F32)
    out = glu[..., :BRANCH] * jax.nn.sigmoid(glu[..., BRANCH:]) * jax.nn.silu(gate.astype(F32))
    state = jnp.stack(finals, axis=1) if h0 is None else None
    return out.astype(u.dtype), state


def hgrn_chunkwise(q, k, v, g, s0):
    B, L, H, _ = q.shape
    n = L // HG_CHUNK

    def chunks(t):
        return jnp.moveaxis(t.reshape(B, n, HG_CHUNK, H, t.shape[-1]), 1, 0)

    causal = jnp.tril(jnp.ones((HG_CHUNK, HG_CHUNK), dtype=bool))[None, :, :, None, None]

    def step(S, inp):
        qc, kc, vc, gc = inp
        b = jnp.cumsum(gc, axis=1)
        decay = jnp.exp(jnp.where(causal, b[:, :, None] - b[:, None, :], -jnp.inf))
        scores = jnp.einsum("bthd,btshd,bshd->bhts", qc, decay, kc)
        o = (jnp.einsum("bhts,bshv->bthv", scores, vc)
             + jnp.einsum("bthd,bhdv->bthv", qc * jnp.exp(b), S))
        b_last = b[:, -1]
        S = S * jnp.exp(b_last)[..., None] + jnp.einsum("bshd,bshv->bhdv", kc * jnp.exp(b_last[:, None] - b), vc)
        return S, o

    S, o = lax.scan(step, s0, (chunks(q), chunks(k), chunks(v), chunks(g)))
    return jnp.moveaxis(o, 0, 1).reshape(B, L, H, v.shape[-1]), S


def hgrn_branch(q, ff, fb, iv, gate, lb_l, norm_g, s0):
    B, L, _ = q.shape
    shp = (B, L, HG_HEADS, HG_DK)
    qf = q.astype(F32).reshape(shp)
    vf = iv.astype(F32).reshape(B, L, HG_HEADS, HG_DV)
    o, finals = None, []
    for d, zf in enumerate((ff, fb)):
        lb = lb_l[d].reshape(HG_HEADS, HG_DK)
        z = zf.astype(F32).reshape(shp)
        log_f = jnp.log(lb + (1.0 - lb) * jax.nn.sigmoid(z))
        k = (1.0 - lb) * jax.nn.sigmoid(-z)
        init = jnp.zeros((B, HG_HEADS, HG_DK, HG_DV), F32) if s0 is None else s0[:, d].astype(F32)
        if d == 0:
            od, Sd = hgrn_chunkwise(qf, k, vf, log_f, init)
        else:
            od, Sd = hgrn_chunkwise(jnp.flip(qf, 1), jnp.flip(k, 1), jnp.flip(vf, 1), jnp.flip(log_f, 1), init)
            od = jnp.flip(od, 1)
        o = od if o is None else o + od
        finals.append(Sd)
    o = rms_norm(o) * norm_g.astype(F32)
    out = o.reshape(B, L, BRANCH) * jax.nn.silu(gate.astype(F32))
    state = jnp.stack(finals, axis=1) if s0 is None else None
    return out.astype(q.dtype), state


def mla_branch(cq, ckv, kr, gate, P, l, pos, ctx):
    B, L, _ = cq.shape
    dt = cq.dtype
    q = (rms_norm(cq) * P["mla_q_norm"][l].astype(F32)).astype(dt) @ P["mla_w_uq"][l]
    q = q.reshape(B, L, MLA_HEADS, MLA_NOPE + MLA_ROPE)
    ckv_n = (rms_norm(ckv) * P["mla_kv_norm"][l].astype(F32)).astype(dt)
    if ctx is None:
        ckv_all, kr_all = ckv_n, kr
    else:
        q = jnp.concatenate([q[..., :MLA_NOPE], rope_2d(q[..., MLA_NOPE:], *pos)], axis=-1)
        kr_lat = rope_2d(kr[:, :, None, :], *pos)[:, :, 0]
        ckv_all = jnp.concatenate([ckv_n, ctx[0].astype(dt)], axis=1)
        kr_all = jnp.concatenate([kr_lat, ctx[1].astype(dt)], axis=1)
    Lk = ckv_all.shape[1]
    kv = (ckv_all @ P["mla_w_ukv"][l]).reshape(B, Lk, MLA_HEADS, MLA_NOPE + MLA_V)
    k = jnp.concatenate([kv[..., :MLA_NOPE],
                         jnp.broadcast_to(kr_all[:, :, None, :], (B, Lk, MLA_HEADS, MLA_ROPE))], axis=-1)
    o = blocked_attention(q, k, kv[..., MLA_NOPE:], (MLA_NOPE + MLA_ROPE) ** -0.5)
    out = o.reshape(B, L, BRANCH).astype(F32) * jax.nn.silu(gate.astype(F32))
    return out.astype(dt), (ckv_n, kr)


def trunk_layer(x, mod, P, l, pos, ctx):
    dt = x.dtype
    shift, scale, gate = jnp.split(mod.astype(F32), 3, axis=-1)
    h = (rms_norm(x) * (1.0 + scale) + shift).astype(dt)
    (da_q, da_k, da_v, da_g, s5_u, s5_g, hg_q, hg_ff, hg_fb, hg_i, hg_g,
     mla_cq, mla_ckv, mla_kr, mla_g) = split_columns(h @ P["w_in"][l])
    latent = ctx is not None
    lam_init = 0.8 - 0.6 * math.exp(-0.3 * l)
    a_out, a_ctx = diff_attn_branch(da_q, da_k, da_v, da_g, P["da_lambda"][l], P["da_norm"][l], lam_init,
                                    pos, ctx[0:2] if latent else None)
    b_out, b_ctx = s5_branch(s5_u, s5_g, P, l, ctx[2] if latent else None)
    c_out, c_ctx = hgrn_branch(hg_q, hg_ff, hg_fb, hg_i, hg_g, P["hg_lb"][l], P["hg_norm"][l],
                               ctx[3] if latent else None)
    d_out, d_ctx = mla_branch(mla_cq, mla_ckv, mla_kr, mla_g, P, l, pos, ctx[4:6] if latent else None)
    mixed = jnp.concatenate([a_out, b_out, c_out, d_out], axis=-1)
    x = (x.astype(F32) + gate * (mixed @ P["w_out"][l]).astype(F32)).astype(dt)
    if latent:
        return x, None
    return x, (a_ctx[0], a_ctx[1], b_ctx, c_ctx, d_ctx[0], d_ctx[1])


def setup_inputs(seed: int = 0) -> dict:
    key = jax.random.key(seed)
    keys = iter(jax.random.split(key, 48))

    def nrm(shape, s=1.0):
        return jax.random.normal(next(keys), shape, F32) * s

    def gain(shape):
        return 1.0 + nrm(shape, 0.02)

    L, G, N, H = DEPTH, S5_GROUPS, S5_STATE, S5_CH
    s5_n = jnp.arange(N, dtype=F32)
    return {
        "x_prompt": nrm((BATCH, SEQ, D_MODEL)),
        "x_sample": nrm((DEC_BATCH, DEC_SEQ, D_MODEL)),
        "cache_diff_k": nrm((DEC_BATCH, DEPTH, PAST_LEN, DA_HEADS, 2 * DA_QK)),
        "cache_diff_v": nrm((DEC_BATCH, DEPTH, PAST_LEN, DA_HEADS, DA_V)),
        "state_s5": nrm((DEC_BATCH, DEPTH, 2, G, N, 2), 0.3),
        "state_hgrn": nrm((DEC_BATCH, DEPTH, 2, HG_HEADS, HG_DK, HG_DV), 0.5),
        "cache_mla_ckv": nrm((DEC_BATCH, DEPTH, PAST_LEN, MLA_KV_RANK)),
        "cache_mla_krope": nrm((DEC_BATCH, DEPTH, PAST_LEN, MLA_ROPE)),
        "c": nrm((DEC_BATCH, D_MODEL)),
        "c_ctx": nrm((D_MODEL,)),
        "w_mod": nrm((L, D_MODEL, 3 * D_MODEL), 0.5 * D_MODEL ** -0.5),
        "b_mod": nrm((L, 3 * D_MODEL), 0.02),
        "w_in": nrm((L, D_MODEL, IN_WIDTH), D_MODEL ** -0.5),
        "w_out": nrm((L, MIX_WIDTH, D_MODEL), MIX_WIDTH ** -0.5),
        "da_lambda": nrm((L, 4, DA_QK), 0.1),
        "da_norm": gain((L, DA_V)),
        "s5_a_re": -0.5 + nrm((L, 2, G, N), 0.01),
        "s5_a_im": math.pi * s5_n + nrm((L, 2, G, N), 0.01),
        "s5_log_dt": jax.random.uniform(next(keys), (L, 2, G), F32, math.log(S5_DT_MIN), math.log(S5_DT_MAX)),
        "s5_b_re": nrm((L, 2, G, N, H), H ** -0.5),
        "s5_b_im": nrm((L, 2, G, N, H), H ** -0.5),
        "s5_c_re": nrm((L, 2, G, H, N), N ** -0.5),
        "s5_c_im": nrm((L, 2, G, H, N), N ** -0.5),
        "s5_d": nrm((L, BRANCH)),
        "s5_w_glu": nrm((L, BRANCH, 2 * BRANCH), BRANCH ** -0.5),
        "hg_lb": nrm((L, 2, HG_HEADS * HG_DK), 0.1),
        "hg_norm": gain((L, HG_DV)),
        "mla_q_norm": gain((L, MLA_Q_RANK)),
        "mla_w_uq": nrm((L, MLA_Q_RANK, MLA_HEADS * (MLA_NOPE + MLA_ROPE)), MLA_Q_RANK ** -0.5),
        "mla_kv_norm": gain((L, MLA_KV_RANK)),
        "mla_w_ukv": nrm((L, MLA_KV_RANK, MLA_HEADS * (MLA_NOPE + MLA_V)), MLA_KV_RANK ** -0.5),
        "final_norm": gain((D_MODEL,)),
    }


def reference(x_prompt, x_sample, cache_diff_k, cache_diff_v, state_s5, state_hgrn, cache_mla_ckv,
              cache_mla_krope, c, c_ctx, w_mod, b_mod, w_in, w_out, da_lambda, da_norm, s5_a_re, s5_a_im,
              s5_log_dt, s5_b_re, s5_b_im, s5_c_re, s5_c_im, s5_d, s5_w_glu, hg_lb, hg_norm, mla_q_norm,
              mla_w_uq, mla_kv_norm, mla_w_ukv, final_norm):
    lb_w = jax.nn.softmax(hg_lb.astype(F32), axis=0)
    lb_all = jnp.cumsum(lb_w, axis=0) - lb_w[0:1]
    P = {
        "w_in": w_in, "w_out": w_out, "da_lambda": da_lambda, "da_norm": da_norm,
        "s5_a_re": s5_a_re, "s5_a_im": s5_a_im, "s5_log_dt": s5_log_dt, "s5_b_re": s5_b_re,
        "s5_b_im": s5_b_im, "s5_c_re": s5_c_re, "s5_c_im": s5_c_im, "s5_d": s5_d, "s5_w_glu": s5_w_glu,
        "hg_lb": lb_all, "hg_norm": hg_norm, "mla_q_norm": mla_q_norm, "mla_w_uq": mla_w_uq,
        "mla_kv_norm": mla_kv_norm, "mla_w_ukv": mla_w_ukv,
    }

    x = x_prompt
    ctx_layers = []
    for l in range(DEPTH):
        mod = (jax.nn.silu(c_ctx.astype(F32)) @ w_mod[l].astype(F32) + b_mod[l].astype(F32))[None, None]
        x, st = trunk_layer(x, mod, P, l, None, None)
        ctx_layers.append(st)
    y_prompt = (rms_norm(x) * final_norm.astype(F32)).astype(x_prompt.dtype)
    new_diff_k = jnp.stack([s[0] for s in ctx_layers], axis=1)
    new_diff_v = jnp.stack([s[1] for s in ctx_layers], axis=1)
    new_s5 = jnp.stack([s[2] for s in ctx_layers], axis=1)
    new_hgrn = jnp.stack([s[3] for s in ctx_layers], axis=1)
    new_mla_ckv = jnp.stack([s[4] for s in ctx_layers], axis=1)
    new_mla_krope = jnp.stack([s[5] for s in ctx_layers], axis=1)

    n_rows = x_sample.shape[1] // GRID_W
    row = jnp.repeat(jnp.arange(n_rows, dtype=jnp.int32), GRID_W)
    col = jnp.tile(jnp.arange(GRID_W, dtype=jnp.int32), n_rows)
    x = x_sample
    for l in range(DEPTH):
        mod = (jax.nn.silu(c.astype(F32)) @ w_mod[l].astype(F32) + b_mod[l].astype(F32))[:, None]
        ctx = (cache_diff_k[:, l], cache_diff_v[:, l], state_s5[:, l], state_hgrn[:, l],
               cache_mla_ckv[:, l], cache_mla_krope[:, l])
        x, _ = trunk_layer(x, mod, P, l, (row, col), ctx)
    y_sample = (rms_norm(x) * final_norm.astype(F32)).astype(x_sample.dtype)

    return (y_prompt, y_sample, new_diff_k, new_diff_v, new_s5, new_hgrn, new_mla_ckv, new_mla_krope)
```

```python
import functools
import math

import jax
import jax.numpy as jnp
from jax import lax
from jax.experimental import pallas as pl
from jax.experimental.pallas import tpu as pltpu

F32 = jnp.float32
BF16 = jnp.bfloat16

D_MODEL = 1024
DEPTH = 2
N_CTX_SEQ = 16
CTX_LEN = 256
N_LAT_SEQ = 2
LAT_LEN = 2048
PAST_LEN = 256
GRID_W = 64
N_CTX = N_CTX_SEQ * CTX_LEN
N_LAT = N_LAT_SEQ * LAT_LEN
N_TOK = N_CTX + N_LAT
BRANCH = 256
EPS = 1e-6
ROPE_BASE = 10000.0
ROW_TILE = 256
LAT_TILES = LAT_LEN // ROW_TILE
N_TILES = N_TOK // ROW_TILE
CTX_TILES = N_CTX // ROW_TILE
VMEM_LIMIT = 48 * 1024 * 1024

DA_HEADS = 4
DA_QK = 32
MLA_HEADS = 4
MLA_NOPE = 64
MLA_ROPE = 32
MLA_Q_RANK = 192
MLA_KV_RANK = 128
S5_GROUPS = 16
S5_CH = 16
S5_STATE = 64
S5_CHUNK = 16
S5_LAT_PAD = 8
HG_HEADS = 4
HG_DK = 64

W_A = 1536
W_B = 512
W_C = 1536
W_D = 896
W_ALL = W_A + W_B + W_C + W_D


def _cparams(*sem):
    return pltpu.CompilerParams(dimension_semantics=sem, vmem_limit_bytes=VMEM_LIMIT)


def _tile_seq(i):
    return jnp.where(i < CTX_TILES, 0, 1 + (i - CTX_TILES) // LAT_TILES)


def _silu(x):
    return x * (1.0 / (1.0 + jnp.exp(-x)))


def _dot(a, b):
    return jnp.dot(a, b, preferred_element_type=F32)


def _dot_nt(a, b):
    return lax.dot_general(a, b, (((1,), (1,)), ((), ())), preferred_element_type=F32)


def _dot_tn(a, b):
    return lax.dot_general(a, b, (((0,), (0,)), ((), ())), preferred_element_type=F32)


def _mod_body(c_ref, w_ref, b_ref, o_ref):
    c = _silu(c_ref[...]).astype(BF16)
    o_ref[0] = _dot(c, w_ref[0].astype(BF16)) + b_ref[0]


def _modulation(c_rows, w_mod, b_mod):
    tn = 768
    return pl.pallas_call(
        _mod_body,
        grid=(DEPTH, 3 * D_MODEL // tn),
        in_specs=[pl.BlockSpec((8, D_MODEL), lambda l, j: (0, 0)),
                  pl.BlockSpec((1, D_MODEL, tn), lambda l, j: (l, 0, j)),
                  pl.BlockSpec((1, 1, tn), lambda l, j: (l, 0, j))],
        out_specs=pl.BlockSpec((1, 8, tn), lambda l, j: (l, 0, j)),
        out_shape=jax.ShapeDtypeStruct((DEPTH, 8, 3 * D_MODEL), F32),
        compiler_params=_cparams("parallel", "parallel"),
    )(c_rows, w_mod, b_mod.reshape(DEPTH, 1, 3 * D_MODEL))


def _in_proj_body(x_ref, mod_ref, w_ref, oa, ob, oc, od):
    x = x_ref[...]
    xn = x * lax.rsqrt(jnp.mean(x * x, axis=-1, keepdims=True) + EPS)
    mod = mod_ref[0]
    h = (xn * (1.0 + mod[1:2]) + mod[0:1]).astype(BF16)
    off = 0
    for o in (oa, ob, oc, od):
        w = o.shape[-1]
        o[...] = _dot(h, w_ref[:, off:off + w])
        off += w


def _in_proj(x, mod, w_big):
    widths = (W_A, W_B, W_C, W_D)
    return pl.pallas_call(
        _in_proj_body,
        grid=(N_TILES,),
        in_specs=[pl.BlockSpec((ROW_TILE, D_MODEL), lambda i: (i, 0)),
                  pl.BlockSpec((1, 3, D_MODEL), lambda i: (_tile_seq(i), 0, 0)),
                  pl.BlockSpec((D_MODEL, W_ALL), lambda i: (0, 0))],
        out_specs=[pl.BlockSpec((ROW_TILE, w), lambda i: (i, 0)) for w in widths],
        out_shape=[jax.ShapeDtypeStruct((N_TOK, w), F32) for w in widths],
        compiler_params=_cparams("parallel"),
    )(x, mod, w_big)


def _out_proj_body(a_ref, b_ref, c_ref, d_ref, x_ref, mod_ref, w_ref, fn_ref, o_ref, *, final):
    acc = None
    for j, r in enumerate((a_ref, b_ref, c_ref, d_ref)):
        t = _dot(r[...].astype(BF16), w_ref[j * BRANCH:(j + 1) * BRANCH, :])
        acc = t if acc is None else acc + t
    x = x_ref[...] + mod_ref[0][2:3] * acc
    if final:
        x = x * lax.rsqrt(jnp.mean(x * x, axis=-1, keepdims=True) + EPS) * fn_ref[...]
    o_ref[...] = x


def _out_proj(a, b, c, d, x, mod, w_out, final_norm, final):
    br = pl.BlockSpec((ROW_TILE, BRANCH), lambda i: (i, 0))
    return pl.pallas_call(
        functools.partial(_out_proj_body, final=final),
        grid=(N_TILES,),
        in_specs=[br, br, br, br,
                  pl.BlockSpec((ROW_TILE, D_MODEL), lambda i: (i, 0)),
                  pl.BlockSpec((1, 3, D_MODEL), lambda i: (_tile_seq(i), 0, 0)),
                  pl.BlockSpec((D_MODEL, D_MODEL), lambda i: (0, 0)),
                  pl.BlockSpec((1, D_MODEL), lambda i: (0, 0))],
        out_specs=pl.BlockSpec((ROW_TILE, D_MODEL), lambda i: (i, 0)),
        out_shape=jax.ShapeDtypeStruct((N_TOK, D_MODEL), F32),
        compiler_params=_cparams("parallel"),
    )(a, b, c, d, x, mod, w_out, final_norm)


def _softmax_rows(s):
    e = jnp.exp(s - jnp.max(s, axis=-1, keepdims=True))
    return e * (1.0 / jnp.sum(e, axis=-1, keepdims=True))


def _da_kv_body(k_ref, kr_ref, v_ref, cos_ref, sin_ref, ck_ref, cv_ref, ko_ref, vo_ref):
    j = pl.program_id(1)

    @pl.when(j < LAT_TILES)
    def _():
        ko_ref[0] = (k_ref[...] * cos_ref[...] + kr_ref[...] * sin_ref[...]).astype(BF16)
        vo_ref[0] = v_ref[...].astype(BF16)

    @pl.when(j == LAT_TILES)
    def _():
        ko_ref[0] = ck_ref[0].astype(BF16)
        vo_ref[0] = cv_ref[0].astype(BF16)


def _da_latent_kv(z_a, cos, sin, cache_k, cache_v):
    def rows(col):
        return pl.BlockSpec(
            (ROW_TILE, BRANCH),
            lambda b, j: (CTX_TILES + b * LAT_TILES + jnp.minimum(j, LAT_TILES - 1), col))
    tab = pl.BlockSpec((ROW_TILE, BRANCH), lambda b, j: (jnp.minimum(j, LAT_TILES - 1), 0))
    cache = pl.BlockSpec((1, PAST_LEN, BRANCH), lambda b, j: (b, 0, 0))
    out = pl.BlockSpec((1, ROW_TILE, BRANCH), lambda b, j: (b, j, 0))
    shp = jax.ShapeDtypeStruct((N_LAT_SEQ, LAT_LEN + PAST_LEN, BRANCH), BF16)
    return pl.pallas_call(
        _da_kv_body,
        grid=(N_LAT_SEQ, LAT_TILES + 1),
        in_specs=[rows(1), rows(5), rows(2), tab, tab, cache, cache],
        out_specs=[out, out],
        out_shape=[shp, shp],
        compiler_params=_cparams("parallel", "parallel"),
    )(z_a, z_a, z_a, cos, sin, cache_k, cache_v)


def _da_attn_body(lam_ref, ng_ref, q_ref, *rest, rope, lam_init):
    if rope:
        qr_ref, cos_ref, sin_ref, k_ref, v_ref, g_ref, o_ref = rest
        q = q_ref[...] * cos_ref[...] + qr_ref[...] * sin_ref[...]
        k = k_ref[0]
        v = v_ref[0]
    else:
        k_ref, v_ref, g_ref, o_ref = rest
        q = q_ref[...]
        k = k_ref[...].astype(BF16)
        v = v_ref[...].astype(BF16)
    q = q * (DA_QK ** -0.5)
    lv = lam_ref[...]
    lam = (jnp.exp(jnp.sum(lv[0:1] * lv[1:2], axis=-1, keepdims=True))
           - jnp.exp(jnp.sum(lv[2:3] * lv[3:4], axis=-1, keepdims=True)) + lam_init)
    lane = lax.broadcasted_iota(jnp.int32, (1, BRANCH), 1)
    acc = jnp.zeros(q.shape, F32)
    for h in range(DA_HEADS):
        q1 = jnp.where(lane // DA_QK == 2 * h, q, 0.0).astype(BF16)
        q2 = jnp.where(lane // DA_QK == 2 * h + 1, q, 0.0).astype(BF16)
        p1 = _softmax_rows(_dot_nt(q1, k))
        p2 = _softmax_rows(_dot_nt(q2, k))
        a = (p1 - lam * p2).astype(BF16)
        acc = jnp.where(lane // (2 * DA_QK) == h, _dot(a, v), acc)
    sq = acc * acc
    ms = jnp.zeros(q.shape, F32)
    for h in range(DA_HEADS):
        hm = lane // (2 * DA_QK) == h
        ms = jnp.where(hm, jnp.sum(jnp.where(hm, sq, 0.0), axis=-1, keepdims=True), ms)
    o = acc * lax.rsqrt(ms * (1.0 / (2 * DA_QK)) + EPS) * (ng_ref[...] * (1.0 - lam_init))
    o_ref[...] = o * _silu(g_ref[...])


def _da_attention(z_a, lam_vec, norm_g, lam_init, cos, sin, kv_lat):
    ng = jnp.tile(norm_g.reshape(1, 2 * DA_QK), (1, DA_HEADS))
    small = [pl.BlockSpec((4, DA_QK), lambda *_: (0, 0)), pl.BlockSpec((1, BRANCH), lambda *_: (0, 0))]

    def col(c):
        return pl.BlockSpec((ROW_TILE, BRANCH), lambda i: (i, c))
    ctx = pl.pallas_call(
        functools.partial(_da_attn_body, rope=False, lam_init=lam_init),
        grid=(CTX_TILES,),
        in_specs=small + [col(0), col(1), col(2), col(3)],
        out_specs=pl.BlockSpec((ROW_TILE, BRANCH), lambda i: (i, 0)),
        out_shape=jax.ShapeDtypeStruct((N_TOK, BRANCH), F32),
        compiler_params=_cparams("parallel"),
    )(lam_vec, ng, z_a, z_a, z_a, z_a)

    def lcol(c):
        return pl.BlockSpec((ROW_TILE, BRANCH), lambda b, j: (CTX_TILES + b * LAT_TILES + j, c))
    tab = pl.BlockSpec((ROW_TILE, BRANCH), lambda b, j: (j, 0))
    kvs = pl.BlockSpec((1, LAT_LEN + PAST_LEN, BRANCH), lambda b, j: (b, 0, 0))
    return pl.pallas_call(
        functools.partial(_da_attn_body_aliased, rope=True, lam_init=lam_init),
        grid=(N_LAT_SEQ, LAT_TILES),
        in_specs=[pl.BlockSpec(memory_space=pl.ANY)] + small + [lcol(0), lcol(4), tab, tab, kvs, kvs, lcol(3)],
        out_specs=lcol(0),
        out_shape=jax.ShapeDtypeStruct((N_TOK, BRANCH), F32),
        input_output_aliases={0: 0},
        compiler_params=_cparams("parallel", "parallel"),
    )(ctx, lam_vec, ng, z_a, z_a, cos, sin, kv_lat[0], kv_lat[1], z_a)


def _da_attn_body_aliased(_, *refs, **kw):
    _da_attn_body(*refs, **kw)


MLA_HEAD_PAD = 128
MLA_QW = MLA_HEADS * MLA_HEAD_PAD


def _mla_prep_body(cq_ref, ckv_ref, kr_ref, krr_ref, cq_t, sq_t, ck_t, sk_t, qn_ref, kvn_ref, wq_ref, wqr_ref,
                   q_out, ckv_out, kr_out):
    cq = cq_ref[...]
    ms = jnp.sum(cq * cq, axis=-1, keepdims=True) * (1.0 / MLA_Q_RANK)
    qn = (cq * lax.rsqrt(ms + EPS) * qn_ref[...]).astype(BF16)
    q = _dot(qn, wq_ref[...]) * cq_t[...] + _dot(qn, wqr_ref[...]) * sq_t[...]
    q_out[...] = (q * ((MLA_NOPE + MLA_ROPE) ** -0.5)).astype(BF16)
    ckv = ckv_ref[...]
    ckv_out[...] = ckv * lax.rsqrt(jnp.mean(ckv * ckv, axis=-1, keepdims=True) + EPS) * kvn_ref[...]
    kr_out[...] = kr_ref[...] * ck_t[...] + krr_ref[...] * sk_t[...]


def _mla_prep(z_d, tabs, q_norm_pad, kv_norm, wq, wqr):
    def tab(w):
        return pl.BlockSpec(
            (ROW_TILE, w), lambda i: (jnp.where(i < CTX_TILES, LAT_TILES, (i - CTX_TILES) % LAT_TILES), 0))

    def col(w, c):
        return pl.BlockSpec((ROW_TILE, w), lambda i: (i, c))

    def const(shape):
        return pl.BlockSpec(shape, lambda i: (0, 0))
    return pl.pallas_call(
        _mla_prep_body,
        grid=(N_TILES,),
        in_specs=[col(256, 0), col(128, 2), col(128, 3), col(128, 6),
                  tab(MLA_QW), tab(MLA_QW), tab(128), tab(128),
                  const((1, 256)), const((1, 128)), const((256, MLA_QW)), const((256, MLA_QW))],
        out_specs=[col(MLA_QW, 0), col(128, 0), col(128, 0)],
        out_shape=[jax.ShapeDtypeStruct((N_TOK, MLA_QW), BF16),
                   jax.ShapeDtypeStruct((N_TOK, 128), F32),
                   jax.ShapeDtypeStruct((N_TOK, 128), F32)],
        compiler_params=_cparams("parallel"),
    )(z_d, z_d, z_d, z_d, *tabs, q_norm_pad, kv_norm, wq, wqr)


def _mla_kv_body(ckv_ref, kr_ref, wk_ref, wv_ref, k_out, v_out):
    c = ckv_ref[...].astype(BF16)
    kr = kr_ref[...]
    k_out[...] = (_dot(c, wk_ref[...]) + jnp.concatenate([kr] * MLA_HEADS, axis=-1)).astype(BF16)
    v_out[...] = _dot(c, wv_ref[...]).astype(BF16)


def _mla_kv(ckv, kr, wk, wv, n_rows):
    return pl.pallas_call(
        _mla_kv_body,
        grid=(n_rows // ROW_TILE,),
        in_specs=[pl.BlockSpec((ROW_TILE, 128), lambda i: (i, 0)),
                  pl.BlockSpec((ROW_TILE, 128), lambda i: (i, 0)),
                  pl.BlockSpec((128, MLA_QW), lambda i: (0, 0)),
                  pl.BlockSpec((128, BRANCH), lambda i: (0, 0))],
        out_specs=[pl.BlockSpec((ROW_TILE, MLA_QW), lambda i: (i, 0)),
                   pl.BlockSpec((ROW_TILE, BRANCH), lambda i: (i, 0))],
        out_shape=[jax.ShapeDtypeStruct((n_rows, MLA_QW), BF16),
                   jax.ShapeDtypeStruct((n_rows, BRANCH), BF16)],
        compiler_params=_cparams("parallel"),
    )(ckv, kr, wk, wv)


def _mla_attn_body(q_ref, k_ref, v_ref, g_ref, o_ref):
    q = q_ref[...]
    k = k_ref[...].reshape(-1, MLA_QW)
    v = v_ref[...].reshape(-1, BRANCH)
    lane = lax.broadcasted_iota(jnp.int32, (1, BRANCH), 1)
    acc = jnp.zeros((q.shape[0], BRANCH), F32)
    for h in range(MLA_HEADS):
        sl = slice(h * MLA_HEAD_PAD, (h + 1) * MLA_HEAD_PAD)
        p = _softmax_rows(_dot_nt(q[:, sl], k[:, sl])).astype(BF16)
        acc = jnp.where(lane // 64 == h, _dot(p, v), acc)
    o_ref[...] = acc * _silu(g_ref[...])


def _mla_attn_body_aliased(_, *refs):
    _mla_attn_body(*refs)


def _mla_attention(z_d, q, k_ctx, v_ctx, k_lat, v_lat):
    ctx = pl.pallas_call(
        _mla_attn_body,
        grid=(CTX_TILES,),
        in_specs=[pl.BlockSpec((ROW_TILE, MLA_QW), lambda i: (i, 0)),
                  pl.BlockSpec((ROW_TILE, MLA_QW), lambda i: (i, 0)),
                  pl.BlockSpec((ROW_TILE, BRANCH), lambda i: (i, 0)),
                  pl.BlockSpec((ROW_TILE, BRANCH), lambda i: (i, 2))],
        out_specs=pl.BlockSpec((ROW_TILE, BRANCH), lambda i: (i, 0)),
        out_shape=jax.ShapeDtypeStruct((N_TOK, BRANCH), F32),
        compiler_params=_cparams("parallel"),
    )(q, k_ctx, v_ctx, z_d)
    lk = LAT_LEN + PAST_LEN
    return pl.pallas_call(
        _mla_attn_body_aliased,
        grid=(N_LAT_SEQ, LAT_TILES),
        in_specs=[pl.BlockSpec(memory_space=pl.ANY),
                  pl.BlockSpec((ROW_TILE, MLA_QW), lambda b, j: (CTX_TILES + b * LAT_TILES + j, 0)),
                  pl.BlockSpec((1, lk, MLA_QW), lambda b, j: (b, 0, 0)),
                  pl.BlockSpec((1, lk, BRANCH), lambda b, j: (b, 0, 0)),
                  pl.BlockSpec((ROW_TILE, BRANCH), lambda b, j: (CTX_TILES + b * LAT_TILES + j, 2))],
        out_specs=pl.BlockSpec((ROW_TILE, BRANCH), lambda b, j: (CTX_TILES + b * LAT_TILES + j, 0)),
        out_shape=jax.ShapeDtypeStruct((N_TOK, BRANCH), F32),
        input_output_aliases={0: 0},
        compiler_params=_cparams("parallel", "parallel"),
    )(ctx, q, k_lat.reshape(N_LAT_SEQ, lk, MLA_QW), v_lat.reshape(N_LAT_SEQ, lk, BRANCH), z_d)


S5_CTX_ROWS = (CTX_LEN // S5_CHUNK) * N_CTX_SEQ
S5_LAT_STEPS = LAT_LEN // S5_CHUNK
S5_ROWS = S5_CTX_ROWS + S5_LAT_STEPS * S5_LAT_PAD
S5_TAP = S5_CHUNK * S5_CH


def _s5_body(u_ref, m_ref, bs_ref, co_ref, a_ref, h0_ref, y_ref, fin_ref, s_scr, h_scr):
    u = u_ref[0]
    y_ref[0] = _dot(u, m_ref[0])
    for k in range(4):
        s_scr[k] = _dot(u, bs_ref[0, k])
    a = a_ref[0]
    n_ctx_steps = CTX_LEN // S5_CHUNK

    def scan(d, base, rows, steps, init, reverse):
        are, aim = a[2 * d:2 * d + 1], a[2 * d + 1:2 * d + 2]

        def step(i, carry):
            hr, hi = carry
            c = (steps - 1 - i) if reverse else i
            r0 = pl.multiple_of(base + c * rows, rows)
            h_scr[2 * d, pl.ds(r0, rows), :] = hr
            h_scr[2 * d + 1, pl.ds(r0, rows), :] = hi
            sr = s_scr[2 * d, pl.ds(r0, rows), :]
            si = s_scr[2 * d + 1, pl.ds(r0, rows), :]
            return are * hr - aim * hi + sr, are * hi + aim * hr + si
        return lax.fori_loop(0, steps, step, init)

    zero = jnp.zeros((N_CTX_SEQ, 128), F32)
    for d in range(2):
        hr, hi = scan(d, 0, N_CTX_SEQ, n_ctx_steps, (zero, zero), reverse=(d == 1))
        fin_ref[0, 2 * d] = hr
        fin_ref[0, 2 * d + 1] = hi
        scan(d, S5_CTX_ROWS, S5_LAT_PAD, S5_LAT_STEPS, (h0_ref[0, 2 * d], h0_ref[0, 2 * d + 1]), reverse=(d == 1))
    acc = y_ref[0]
    for k in range(4):
        acc = acc + _dot(h_scr[k].astype(BF16), co_ref[0, k])
    y_ref[0] = acc


def _s5_scan(u_g, m, bs, co, a16, h0):
    g = S5_GROUPS
    return pl.pallas_call(
        _s5_body,
        grid=(g,),
        in_specs=[pl.BlockSpec((1, S5_ROWS, S5_TAP), lambda i: (i, 0, 0)),
                  pl.BlockSpec((1, S5_TAP, S5_TAP), lambda i: (i, 0, 0)),
                  pl.BlockSpec((1, 4, S5_TAP, 128), lambda i: (i, 0, 0, 0)),
                  pl.BlockSpec((1, 4, 128, S5_TAP), lambda i: (i, 0, 0, 0)),
                  pl.BlockSpec((1, 4, 128), lambda i: (i, 0, 0)),
                  pl.BlockSpec((1, 4, S5_LAT_PAD, 128), lambda i: (i, 0, 0, 0))],
        out_specs=[pl.BlockSpec((1, S5_ROWS, S5_TAP), lambda i: (i, 0, 0)),
                   pl.BlockSpec((1, 4, N_CTX_SEQ, 128), lambda i: (i, 0, 0, 0))],
        out_shape=[jax.ShapeDtypeStruct((g, S5_ROWS, S5_TAP), F32),
                   jax.ShapeDtypeStruct((g, 4, N_CTX_SEQ, 128), F32)],
        scratch_shapes=[pltpu.VMEM((4, S5_ROWS, 128), F32), pltpu.VMEM((4, S5_ROWS, 128), F32)],
        compiler_params=_cparams("parallel"),
    )(u_g, m, bs, co, a16, h0)


def _s5_out_body(y_ref, u_ref, g_ref, d_ref, w_ref, o_ref):
    y = u_ref[...] * d_ref[...] + y_ref[...]
    ge = 0.5 * y * (1.0 + jnp.tanh(0.7978845608028654 * (y + 0.044715 * (y * y * y))))
    gl = _dot(ge.astype(BF16), w_ref[...])
    o_ref[...] = gl[:, :BRANCH] * (1.0 / (1.0 + jnp.exp(-gl[:, BRANCH:]))) * _silu(g_ref[...])


def _s5_out(y_ssm, z_b, d_skip, w_glu):
    def col(c):
        return pl.BlockSpec((ROW_TILE, BRANCH), lambda i: (i, c))
    return pl.pallas_call(
        _s5_out_body,
        grid=(N_TILES,),
        in_specs=[col(0), col(0), col(1),
                  pl.BlockSpec((1, BRANCH), lambda i: (0, 0)),
                  pl.BlockSpec((BRANCH, 2 * BRANCH), lambda i: (0, 0))],
        out_specs=col(0),
        out_shape=jax.ShapeDtypeStruct((N_TOK, BRANCH), F32),
        compiler_params=_cparams("parallel"),
    )(y_ssm, z_b, z_b, d_skip, w_glu)


HG_CHUNK = ROW_TILE
HG_W = 2 * HG_HEADS * HG_DK
HG_HEAD_W = 2 * HG_DK
HG_LAT_CHUNKS = LAT_LEN // HG_CHUNK
HG_CHUNKS = N_TOK // HG_CHUNK


def _hg_gates(z, lb):
    e = jnp.exp(-jnp.abs(z))
    r = 1.0 / (1.0 + e)
    sig_pos = jnp.where(z >= 0, r, e * r)
    sig_neg = jnp.where(z >= 0, e * r, r)
    return lb + (1.0 - lb) * sig_pos, (1.0 - lb) * sig_neg


def _bcast_row(x, period, r):
    n, w = x.shape
    if period >= 8:
        x3 = x.reshape(n // period, period, w)
        return jnp.broadcast_to(x3[:, r:r + 1, :], x3.shape).reshape(n, w)
    x3 = x.reshape(n // 8, 8, w)
    sub = lax.broadcasted_iota(jnp.int32, (1, 8, 1), 1)
    out = None
    for j in range(8 // period):
        b = jnp.broadcast_to(x3[:, j * period + r:j * period + r + 1, :], x3.shape)
        out = b if out is None else jnp.where(sub >= j * period, b, out)
    return out.reshape(n, w)


def _hg_scans(f, isb):
    n = f.shape[0]
    row = lax.broadcasted_iota(jnp.int32, (n, 1), 0)
    p, r = f, jnp.ones_like(f)
    levels = []
    h, sh = 1, 0
    while h < n:
        levels.append((h, sh, p, r))
        up = (row >> sh) & 1
        tot_p = jnp.where(isb == 1, _bcast_row(p, 2 * h, h), _bcast_row(p, 2 * h, h - 1))
        tot_r = jnp.where(isb == 1, _bcast_row(p, 2 * h, 0), _bcast_row(p, 2 * h, 2 * h - 1))
        p = p * jnp.where(up != isb, tot_p, 1.0)
        r = r * jnp.where(up == isb, tot_r, 1.0)
        h, sh = 2 * h, sh + 1
    return levels, p, r


def _hg_state_body(zf_ref, zb_ref, vf_ref, vb_ref, lb_ref, s0_ref, sf_out, sb_out, fin_out, s_scr):
    i = pl.program_id(0)
    first = jnp.logical_or(i < N_CTX_SEQ, (i - N_CTX_SEQ) % HG_LAT_CHUNKS == 0)

    @pl.when(first)
    def _():
        s_scr[...] = s0_ref[0]

    sf_out[0] = s_scr[:, 0:HG_DK, :]
    sb_out[0] = s_scr[:, HG_DK:, :]
    lane5 = lax.broadcasted_iota(jnp.int32, (1, HG_W), 1)
    isb = (lane5 >> 6) & 1
    z = jnp.where(isb == 1, zb_ref[...], zf_ref[...])
    f, k = _hg_gates(z, lb_ref[...])
    _, p, r = _hg_scans(f, isb)
    kt = k * r
    ptot = jnp.where(isb == 1, p[0:1], p[HG_CHUNK - 1:HG_CHUNK])
    lane = lax.broadcasted_iota(jnp.int32, (1, BRANCH), 1)
    vf = vf_ref[...]
    vb = vb_ref[...]
    for hd in range(HG_HEADS):
        sl = slice(hd * HG_HEAD_W, (hd + 1) * HG_HEAD_W)
        kth = kt[:, sl].T.astype(BF16)
        hm = (lane >> 6) == hd
        d_f = _dot(kth, jnp.where(hm, vf, 0.0).astype(BF16))
        d_b = _dot(kth, jnp.where(hm, vb, 0.0).astype(BF16))
        ds = jnp.concatenate([d_f[:HG_DK], d_b[HG_DK:]], axis=0)
        pcol = jnp.broadcast_to(ptot[:, sl], (HG_HEAD_W, HG_HEAD_W)).T[:, 0:1]
        s_scr[hd] = s_scr[hd] * pcol + ds

    @pl.when(i < N_CTX_SEQ)
    def _():
        fin_out[0] = s_scr[...]


def _hg_rev(i):
    j = i - N_CTX_SEQ
    return jnp.where(i < N_CTX_SEQ, i, N_CTX_SEQ + (j // HG_LAT_CHUNKS) * HG_LAT_CHUNKS
                     + (HG_LAT_CHUNKS - 1 - j % HG_LAT_CHUNKS))


def _hg_states(z_c, lb, s0):
    zz_f = pl.BlockSpec((HG_CHUNK, HG_W), lambda i: (i, 1))
    zz_b = pl.BlockSpec((HG_CHUNK, HG_W), lambda i: (_hg_rev(i), 1))
    v_f = pl.BlockSpec((HG_CHUNK, BRANCH), lambda i: (i, 4))
    v_b = pl.BlockSpec((HG_CHUNK, BRANCH), lambda i: (_hg_rev(i), 4))
    st = (HG_HEADS, HG_HEAD_W, BRANCH)
    half = (HG_HEADS, HG_DK, BRANCH)
    return pl.pallas_call(
        _hg_state_body,
        grid=(HG_CHUNKS,),
        in_specs=[zz_f, zz_b, v_f, v_b,
                  pl.BlockSpec((1, HG_W), lambda i: (0, 0)),
                  pl.BlockSpec((1,) + st, lambda i: (
                      jnp.where(i < N_CTX_SEQ, 0, 1 + (i - N_CTX_SEQ) // HG_LAT_CHUNKS), 0, 0, 0))],
        out_specs=[pl.BlockSpec((1,) + half, lambda i: (i, 0, 0, 0)),
                   pl.BlockSpec((1,) + half, lambda i: (_hg_rev(i), 0, 0, 0)),
                   pl.BlockSpec((1,) + st, lambda i: (jnp.minimum(i, N_CTX_SEQ - 1), 0, 0, 0))],
        out_shape=[jax.ShapeDtypeStruct((HG_CHUNKS,) + half, F32),
                   jax.ShapeDtypeStruct((HG_CHUNKS,) + half, F32),
                   jax.ShapeDtypeStruct((N_CTX_SEQ,) + st, F32)],
        scratch_shapes=[pltpu.VMEM(st, F32)],
        compiler_params=_cparams("arbitrary"),
    )(z_c, z_c, z_c, z_c, lb, s0)


def _hg_main_body(qq_ref, zz_ref, v_ref, g_ref, sf_ref, sb_ref, lb_ref, ng_ref, o_ref):
    n = HG_CHUNK
    qq = qq_ref[...]
    lane5 = lax.broadcasted_iota(jnp.int32, (1, HG_W), 1)
    isb = (lane5 >> 6) & 1
    f, k = _hg_gates(zz_ref[...], lb_ref[...])
    levels, pfull, _ = _hg_scans(f, isb)
    row = lax.broadcasted_iota(jnp.int32, (n, 1), 0)
    col = lax.broadcasted_iota(jnp.int32, (1, n), 1)
    ops = [(qq.astype(BF16), k.astype(BF16), row == col)]
    for h, sh, p, r in levels:
        up = (row >> sh) & 1
        qt = jnp.where(up != isb, qq * p, 0.0).astype(BF16)
        kt = jnp.where(up == isb, k * r, 0.0).astype(BF16)
        ops.append((qt, kt, (row >> (sh + 1)) == (col >> (sh + 1))))
    qc = (qq * pfull).astype(BF16)
    vb = v_ref[...].astype(BF16)
    lane = lax.broadcasted_iota(jnp.int32, (1, BRANCH), 1)
    acc = jnp.zeros((n, BRANCH), F32)
    for hd in range(HG_HEADS):
        sl = slice(hd * HG_HEAD_W, (hd + 1) * HG_HEAD_W)
        a = jnp.zeros((n, n), F32)
        for qt, kt, mask in ops:
            a = a + jnp.where(mask, _dot_nt(qt[:, sl], kt[:, sl]), 0.0)
        s_in = jnp.concatenate([sf_ref[0, hd], sb_ref[0, hd]], axis=0).astype(BF16)
        o_h = _dot(a.astype(BF16), vb) + _dot(qc[:, sl], s_in)
        acc = jnp.where((lane >> 6) == hd, o_h, acc)
    sq = acc * acc
    ms = jnp.zeros((n, BRANCH), F32)
    for hd in range(HG_HEADS):
        hm = (lane >> 6) == hd
        ms = jnp.where(hm, jnp.sum(jnp.where(hm, sq, 0.0), axis=-1, keepdims=True), ms)
    o_ref[...] = acc * lax.rsqrt(ms * (1.0 / HG_DK) + EPS) * ng_ref[...] * _silu(g_ref[...])


def _hg_main(z_c, s_f, s_b, lb, norm_g):
    half = (1, HG_HEADS, HG_DK, BRANCH)
    return pl.pallas_call(
        _hg_main_body,
        grid=(HG_CHUNKS,),
        in_specs=[pl.BlockSpec((HG_CHUNK, HG_W), lambda i: (i, 0)),
                  pl.BlockSpec((HG_CHUNK, HG_W), lambda i: (i, 1)),
                  pl.BlockSpec((HG_CHUNK, BRANCH), lambda i: (i, 4)),
                  pl.BlockSpec((HG_CHUNK, BRANCH), lambda i: (i, 5)),
                  pl.BlockSpec(half, lambda i: (i, 0, 0, 0)),
                  pl.BlockSpec(half, lambda i: (i, 0, 0, 0)),
                  pl.BlockSpec((1, HG_W), lambda i: (0, 0)),
                  pl.BlockSpec((1, BRANCH), lambda i: (0, 0))],
        out_specs=pl.BlockSpec((HG_CHUNK, BRANCH), lambda i: (i, 0)),
        out_shape=jax.ShapeDtypeStruct((N_TOK, BRANCH), F32),
        compiler_params=_cparams("parallel"),
    )(z_c, z_c, z_c, z_c, s_f, s_b, lb, norm_g)


def _rot_cols(w):
    n = w.shape[-1]
    j = jnp.arange(n)
    src = (j // 16) * 16 + ((j % 16) + 8) % 16
    sign = jnp.where(j % 16 < 8, -1.0, 1.0).astype(w.dtype)
    return w[:, src] * sign


def _pad_cols(w, left, total):
    return jnp.pad(w, ((0, 0), (left, total - left - w.shape[1])))


_IN_OFF = {}
_off = 0
for _name, _n in (("da_q", 256), ("da_k", 256), ("da_v", 256), ("da_g", 256), ("s5_u", 256), ("s5_g", 256),
                  ("hg_q", 256), ("hg_ff", 256), ("hg_fb", 256), ("hg_i", 256), ("hg_g", 256),
                  ("mla_cq", MLA_Q_RANK), ("mla_ckv", MLA_KV_RANK), ("mla_kr", MLA_ROPE), ("mla_g", 256)):
    _IN_OFF[_name] = (_off, _off + _n)
    _off += _n


def _in_proj_weights(w):
    def c(name):
        a, b = _IN_OFF[name]
        return w[:, a:b]

    def per_head(x, y):
        d = x.shape[0]
        return jnp.concatenate([x.reshape(d, HG_HEADS, HG_DK), y.reshape(d, HG_HEADS, HG_DK)], axis=-1).reshape(d, -1)
    kr = c("mla_kr")
    cols = [c("da_q"), c("da_k"), c("da_v"), c("da_g"), _rot_cols(c("da_q")), _rot_cols(c("da_k")),
            c("s5_u"), c("s5_g"),
            per_head(c("hg_q"), c("hg_q")), per_head(c("hg_ff"), c("hg_fb")), c("hg_i"), c("hg_g"),
            _pad_cols(c("mla_cq"), 0, 256), c("mla_ckv"), _pad_cols(kr, MLA_NOPE, 128), c("mla_g"),
            _pad_cols(_rot_cols(kr), MLA_NOPE, 128)]
    return jnp.concatenate(cols, axis=1).astype(BF16)


def _rope_tables():
    t = jnp.arange(LAT_LEN)
    pos = jnp.stack([t // GRID_W, t % GRID_W], axis=1).astype(F32)
    inv_freq = ROPE_BASE ** (-jnp.arange(8, dtype=F32) / 8)
    r = jnp.arange(MLA_ROPE)
    ang = pos[:, r // 16] * inv_freq[r % 8][None, :]
    cos32, sin32 = jnp.cos(ang), jnp.sin(ang)
    cos_a, sin_a = jnp.tile(cos32, (1, 8)), jnp.tile(sin32, (1, 8))
    one, zero = jnp.ones((LAT_LEN, 1), F32), jnp.zeros((LAT_LEN, 1), F32)
    cos_h = jnp.concatenate([jnp.tile(one, (1, 64)), cos32, jnp.tile(one, (1, 32))], axis=1)
    sin_h = jnp.concatenate([jnp.tile(zero, (1, 64)), sin32, jnp.tile(zero, (1, 32))], axis=1)

    def with_identity(c, s):
        return (jnp.concatenate([c, jnp.ones((ROW_TILE, c.shape[1]), F32)], axis=0),
                jnp.concatenate([s, jnp.zeros((ROW_TILE, s.shape[1]), F32)], axis=0))
    cq, sq = with_identity(jnp.tile(cos_h, (1, MLA_HEADS)), jnp.tile(sin_h, (1, MLA_HEADS)))
    ck, sk = with_identity(cos_h, sin_h)
    return (cos_a, sin_a), (cq, sq, ck, sk)


def _mla_weights(w_uq, w_ukv, q_norm):
    hd = MLA_NOPE + MLA_ROPE
    wq3 = w_uq.reshape(MLA_Q_RANK, MLA_HEADS, hd)
    pad_h = lambda x: jnp.pad(x, ((0, 256 - MLA_Q_RANK), (0, 0), (0, MLA_HEAD_PAD - x.shape[-1]))).reshape(256, MLA_QW)
    wq = pad_h(wq3)
    rope = wq3[:, :, MLA_NOPE:]
    rot = _rot_cols(rope.reshape(MLA_Q_RANK, -1)).reshape(MLA_Q_RANK, MLA_HEADS, MLA_ROPE)
    wqr = pad_h(jnp.concatenate([jnp.zeros_like(wq3[:, :, :MLA_NOPE]), rot], axis=-1))
    kv3 = w_ukv.reshape(MLA_KV_RANK, MLA_HEADS, 2 * MLA_NOPE)
    wk = jnp.pad(kv3[:, :, :MLA_NOPE], ((0, 0), (0, 0), (0, MLA_HEAD_PAD - MLA_NOPE))).reshape(MLA_KV_RANK, MLA_QW)
    wv = kv3[:, :, MLA_NOPE:].reshape(MLA_KV_RANK, BRANCH)
    qn = jnp.pad(q_norm, (0, 256 - MLA_Q_RANK)).reshape(1, 256)
    return wq.astype(BF16), wqr.astype(BF16), wk.astype(BF16), wv.astype(BF16), qn


def _s5_tables(a_re, a_im, log_dt, b_re, b_im, c_re, c_im):
    hp = lax.Precision.HIGHEST
    g, n, ch, t = S5_GROUPS, S5_STATE, S5_CH, S5_CHUNK
    step = jnp.exp(log_dt)[..., None]
    mag = jnp.exp(a_re * step)
    ab_re, ab_im = mag * jnp.cos(a_im * step), mag * jnp.sin(a_im * step)
    den = a_re * a_re + a_im * a_im
    f_re = ((ab_re - 1.0) * a_re + ab_im * a_im) / den
    f_im = (ab_im * a_re - (ab_re - 1.0) * a_im) / den
    bb_re = f_re[..., None] * b_re - f_im[..., None] * b_im
    bb_im = f_re[..., None] * b_im + f_im[..., None] * b_re
    pr, pi = [jnp.ones_like(ab_re)], [jnp.zeros_like(ab_re)]
    for _ in range(t):
        pr, pi = pr + [pr[-1] * ab_re - pi[-1] * ab_im], pi + [pr[-1] * ab_im + pi[-1] * ab_re]
    pw_re, pw_im = jnp.stack(pr), jnp.stack(pi)
    w_re = c_re[None] * pw_re[:, :, :, None, :] - c_im[None] * pw_im[:, :, :, None, :]
    w_im = c_re[None] * pw_im[:, :, :, None, :] + c_im[None] * pw_re[:, :, :, None, :]
    taps = (jnp.einsum("tdghp,dgpk->tdghk", w_re, bb_re, precision=hp)
            - jnp.einsum("tdghp,dgpk->tdghk", w_im, bb_im, precision=hp))
    kall = jnp.concatenate([taps[t - 1:0:-1, 1], (taps[0, 0] + taps[0, 1])[None], taps[1:t, 0]], axis=0)
    ti = jnp.arange(t)
    m = kall[ti[:, None] - ti[None, :] + t - 1]
    m = m.transpose(2, 1, 4, 0, 3).reshape(g, t * ch, t * ch)

    def in_mat(p_re, p_im, d):
        re = p_re[..., None] * bb_re[d][None] - p_im[..., None] * bb_im[d][None]
        im = p_re[..., None] * bb_im[d][None] + p_im[..., None] * bb_re[d][None]
        f = lambda x: jnp.pad(x.transpose(1, 0, 3, 2).reshape(g, t * ch, n), ((0, 0), (0, 0), (0, 128 - n)))
        return f(re), f(im)
    bs = jnp.stack(in_mat(pw_re[t - 1::-1, 0], pw_im[t - 1::-1, 0], 0) + in_mat(pw_re[:t, 1], pw_im[:t, 1], 1), axis=1)

    def out_mat(x):
        return jnp.pad(x.transpose(1, 3, 0, 2).reshape(g, n, t * ch), ((0, 0), (0, 128 - n), (0, 0)))
    co = jnp.stack([out_mat(w_re[1:t + 1, 0]), out_mat(-w_im[1:t + 1, 0]),
                    out_mat(w_re[t:0:-1, 1]), out_mat(-w_im[t:0:-1, 1])], axis=1)
    a16 = jnp.pad(jnp.stack([pw_re[t, 0], pw_im[t, 0], pw_re[t, 1], pw_im[t, 1]], axis=1), ((0, 0), (0, 0), (0, 128 - n)))
    return m.astype(BF16), bs.astype(BF16), co.astype(BF16), a16


def _s5_group_rows(u):
    nc = CTX_LEN // S5_CHUNK
    ctx = u[:N_CTX].reshape(N_CTX_SEQ, nc, S5_CHUNK, S5_GROUPS, S5_CH).transpose(3, 1, 0, 2, 4)
    lat = u[N_CTX:].reshape(N_LAT_SEQ, S5_LAT_STEPS, S5_CHUNK, S5_GROUPS, S5_CH).transpose(3, 1, 0, 2, 4)
    lat = jnp.pad(lat, ((0, 0), (0, 0), (0, S5_LAT_PAD - N_LAT_SEQ), (0, 0), (0, 0)))
    return jnp.concatenate([ctx.reshape(S5_GROUPS, S5_CTX_ROWS, S5_TAP),
                            lat.reshape(S5_GROUPS, S5_LAT_STEPS * S5_LAT_PAD, S5_TAP)], axis=1)


def _s5_token_rows(y):
    nc = CTX_LEN // S5_CHUNK
    ctx = y[:, :S5_CTX_ROWS].reshape(S5_GROUPS, nc, N_CTX_SEQ, S5_CHUNK, S5_CH).transpose(2, 1, 3, 0, 4)
    lat = y[:, S5_CTX_ROWS:].reshape(S5_GROUPS, S5_LAT_STEPS, S5_LAT_PAD, S5_CHUNK, S5_CH)[:, :, :N_LAT_SEQ]
    lat = lat.transpose(2, 1, 3, 0, 4)
    return jnp.concatenate([ctx.reshape(N_CTX, BRANCH), lat.reshape(N_LAT, BRANCH)], axis=0)


def kernel(x_prompt, x_sample, cache_diff_k, cache_diff_v, state_s5, state_hgrn, cache_mla_ckv, cache_mla_krope, c, c_ctx, w_mod, b_mod, w_in, w_out, da_lambda, da_norm, s5_a_re, s5_a_im, s5_log_dt, s5_b_re, s5_b_im, s5_c_re, s5_c_im, s5_d, s5_w_glu, hg_lb, hg_norm, mla_q_norm, mla_w_uq, mla_kv_norm, mla_w_ukv, final_norm):
    lb_w = jax.nn.softmax(hg_lb.astype(F32), axis=0)
    lb_all = jnp.cumsum(lb_w, axis=0) - lb_w[0:1]
    c_rows = jnp.concatenate([c_ctx[None], c, jnp.zeros((8 - 1 - N_LAT_SEQ, D_MODEL), F32)], axis=0)
    mods = _modulation(c_rows, w_mod, b_mod)
    (cos_a, sin_a), mla_tabs = _rope_tables()
    x = jnp.concatenate([x_prompt.reshape(N_CTX, D_MODEL), x_sample.reshape(N_LAT, D_MODEL)], axis=0)
    new_k, new_v, new_s5, new_hg, new_ckv, new_kr = [], [], [], [], [], []
    for l in range(DEPTH):
        mod = mods[l, :3].reshape(3, 3, D_MODEL)
        z_a, z_b, z_c, z_d = _in_proj(x, mod, _in_proj_weights(w_in[l]))

        lam_init = 0.8 - 0.6 * math.exp(-0.3 * l)
        kv_lat = _da_latent_kv(z_a, cos_a, sin_a,
                               cache_diff_k[:, l].reshape(N_LAT_SEQ, PAST_LEN, BRANCH),
                               cache_diff_v[:, l].reshape(N_LAT_SEQ, PAST_LEN, BRANCH))
        a_out = _da_attention(z_a, da_lambda[l], da_norm[l], lam_init, cos_a, sin_a, kv_lat)
        new_k.append(z_a[:N_CTX, 256:512].reshape(N_CTX_SEQ, CTX_LEN, DA_HEADS, 2 * DA_QK))
        new_v.append(z_a[:N_CTX, 512:768].reshape(N_CTX_SEQ, CTX_LEN, DA_HEADS, 2 * DA_QK))

        m, bs, co, a16 = _s5_tables(s5_a_re[l], s5_a_im[l], s5_log_dt[l], s5_b_re[l], s5_b_im[l], s5_c_re[l], s5_c_im[l])
        h0 = state_s5[:, l].transpose(2, 1, 4, 0, 3).reshape(S5_GROUPS, 4, N_LAT_SEQ, S5_STATE)
        h0 = jnp.pad(h0, ((0, 0), (0, 0), (0, S5_LAT_PAD - N_LAT_SEQ), (0, 128 - S5_STATE)))
        y_g, fin = _s5_scan(_s5_group_rows(z_b[:, :BRANCH]).astype(BF16), m, bs, co, a16, h0)
        b_out = _s5_out(_s5_token_rows(y_g), z_b, s5_d[l].reshape(1, BRANCH), s5_w_glu[l].astype(BF16))
        new_s5.append(fin[..., :S5_STATE].reshape(S5_GROUPS, 2, 2, N_CTX_SEQ, S5_STATE).transpose(3, 1, 0, 4, 2))

        lb = jnp.concatenate([lb_all[l, 0].reshape(HG_HEADS, HG_DK), lb_all[l, 1].reshape(HG_HEADS, HG_DK)],
                             axis=-1).reshape(1, HG_W)
        s0 = state_hgrn[:, l].transpose(0, 2, 1, 3, 4).reshape(N_LAT_SEQ, HG_HEADS, HG_HEAD_W, HG_DK)
        s0p = jnp.zeros((1 + N_LAT_SEQ, HG_HEADS, HG_HEAD_W, HG_HEADS, HG_DK), F32)
        for hd in range(HG_HEADS):
            s0p = s0p.at[1:, hd, :, hd, :].set(s0[:, hd])
        s_f, s_b, s_fin = _hg_states(z_c, lb, s0p.reshape(1 + N_LAT_SEQ, HG_HEADS, HG_HEAD_W, BRANCH))
        c_out = _hg_main(z_c, s_f, s_b, lb, jnp.tile(hg_norm[l].reshape(1, HG_DK), (1, HG_HEADS)))
        fin6 = s_fin.reshape(N_CTX_SEQ, HG_HEADS, 2, HG_DK, HG_HEADS, HG_DK)
        new_hg.append(jnp.stack([fin6[:, hd, :, :, hd, :] for hd in range(HG_HEADS)], axis=2))

        wq, wqr, wk, wv, qn = _mla_weights(mla_w_uq[l], mla_w_ukv[l], mla_q_norm[l])
        q, ckv_n, kr = _mla_prep(z_d, mla_tabs, qn, mla_kv_norm[l].reshape(1, MLA_KV_RANK), wq, wqr)
        k_ctx, v_ctx = _mla_kv(ckv_n, kr, wk, wv, N_CTX)
        kr_cache = jnp.pad(cache_mla_krope[:, l], ((0, 0), (0, 0), (MLA_NOPE, 128 - MLA_NOPE - MLA_ROPE)))
        lk = LAT_LEN + PAST_LEN
        ckv_all = jnp.concatenate([ckv_n[N_CTX:].reshape(N_LAT_SEQ, LAT_LEN, 128), cache_mla_ckv[:, l]], axis=1)
        kr_all = jnp.concatenate([kr[N_CTX:].reshape(N_LAT_SEQ, LAT_LEN, 128), kr_cache], axis=1)
        k_lat, v_lat = _mla_kv(ckv_all.reshape(N_LAT_SEQ * lk, 128), kr_all.reshape(N_LAT_SEQ * lk, 128),
                               wk, wv, N_LAT_SEQ * lk)
        d_out = _mla_attention(z_d, q, k_ctx, v_ctx, k_lat, v_lat)
        new_ckv.append(ckv_n[:N_CTX].reshape(N_CTX_SEQ, CTX_LEN, MLA_KV_RANK))
        new_kr.append(kr[:N_CTX, MLA_NOPE:MLA_NOPE + MLA_ROPE].reshape(N_CTX_SEQ, CTX_LEN, MLA_ROPE))

        x = _out_proj(a_out, b_out, c_out, d_out, x, mod, w_out[l].astype(BF16),
                      final_norm.reshape(1, D_MODEL), final=(l == DEPTH - 1))
    y_prompt = x[:N_CTX].reshape(N_CTX_SEQ, CTX_LEN, D_MODEL)
    y_sample = x[N_CTX:].reshape(N_LAT_SEQ, LAT_LEN, D_MODEL)
    st = lambda xs: jnp.stack(xs, axis=1)
    return (y_prompt, y_sample, st(new_k), st(new_v), st(new_s5), st(new_hg), st(new_ckv), st(new_kr))
```

```python
import functools
import math

import numpy as np

import jax
import jax.numpy as jnp
from jax import lax
from jax.experimental import pallas as pl
from jax.experimental.pallas import tpu as pltpu

F32 = jnp.float32
BF16 = jnp.bfloat16

D_MODEL = 1024
DEPTH = 2
N_CTX_SEQ = 16
CTX_LEN = 256
N_LAT_SEQ = 2
LAT_LEN = 2048
PAST_LEN = 256
GRID_W = 64
N_CTX = N_CTX_SEQ * CTX_LEN
N_LAT = N_LAT_SEQ * LAT_LEN
N_TOK = N_CTX + N_LAT
BRANCH = 256
EPS = 1e-6
ROPE_BASE = 10000.0
ROW_TILE = 256
LAT_TILES = LAT_LEN // ROW_TILE
N_TILES = N_TOK // ROW_TILE
CTX_TILES = N_CTX // ROW_TILE
VMEM_LIMIT = 48 * 1024 * 1024

DA_HEADS = 4
DA_QK = 32
MLA_HEADS = 4
MLA_NOPE = 64
MLA_ROPE = 32
MLA_Q_RANK = 192
MLA_KV_RANK = 128
S5_GROUPS = 16
S5_CH = 16
S5_STATE = 64
S5_CHUNK = 16
HG_HEADS = 4
HG_DK = 64

W_A = 1536
W_B = 512
W_C = 1536
W_D = 896
W_ABC = W_A + W_B + W_C


def _cparams(*sem):
    return pltpu.CompilerParams(dimension_semantics=sem, vmem_limit_bytes=VMEM_LIMIT)


def _tile_seq(i):
    return jnp.where(i < CTX_TILES, 0, 1 + (i - CTX_TILES) // LAT_TILES)


def _silu(x):
    return x * (1.0 / (1.0 + jnp.exp(-x)))


def _dot(a, b):
    return jnp.dot(a, b, preferred_element_type=F32)


def _dot_nt(a, b):
    return lax.dot_general(a, b, (((1,), (1,)), ((), ())), preferred_element_type=F32)


def _mod_body(c_ref, w_ref, b_ref, o_ref):
    c = _silu(c_ref[...]).astype(BF16)
    o_ref[0] = _dot(c, w_ref[0].astype(BF16)) + b_ref[0]


def _modulation(c_rows, w_mod, b_mod):
    tn = 768
    return pl.pallas_call(
        _mod_body,
        grid=(DEPTH, 3 * D_MODEL // tn),
        in_specs=[pl.BlockSpec((8, D_MODEL), lambda l, j: (0, 0)),
                  pl.BlockSpec((1, D_MODEL, tn), lambda l, j: (l, 0, j)),
                  pl.BlockSpec((1, 1, tn), lambda l, j: (l, 0, j))],
        out_specs=pl.BlockSpec((1, 8, tn), lambda l, j: (l, 0, j)),
        out_shape=jax.ShapeDtypeStruct((DEPTH, 8, 3 * D_MODEL), F32),
        compiler_params=_cparams("parallel", "parallel"),
        name="modulation",
    )(c_rows, w_mod, b_mod.reshape(DEPTH, 1, 3 * D_MODEL))


def _split_rows(i, ctx_ref, lat_ref):
    return jnp.where(i < CTX_TILES, ctx_ref[...], lat_ref[...])


def _ctx_tile_spec(w):
    return pl.BlockSpec((ROW_TILE, w), lambda i: (jnp.minimum(i, CTX_TILES - 1), 0))


def _lat_tile_spec(w):
    return pl.BlockSpec((ROW_TILE, w), lambda i: (jnp.maximum(i - CTX_TILES, 0), 0))


def _in_proj_body(*refs, split):
    if split:
        xc_ref, xl_ref, mod_ref, w_ref, wd_ref, oa, ob, oc, od = refs
        x = _split_rows(pl.program_id(0), xc_ref, xl_ref)
    else:
        x_ref, mod_ref, w_ref, wd_ref, oa, ob, oc, od = refs
        x = x_ref[...]
    xn = x * lax.rsqrt(jnp.mean(x * x, axis=-1, keepdims=True) + EPS)
    mod = mod_ref[0]
    h = (xn * (1.0 + mod[1:2]) + mod[0:1]).astype(BF16)
    off = 0
    for o in (oa, ob, oc):
        w = o.shape[-1]
        o[...] = _dot(h, w_ref[0, :, off:off + w])
        off += w
    od[...] = _dot(h, wd_ref[0])


def _in_proj(xs, mod, w_abc, w_d, l):
    widths = (W_A, W_B, W_C, W_D)
    split = len(xs) == 2
    x_specs = ([_ctx_tile_spec(D_MODEL), _lat_tile_spec(D_MODEL)] if split
               else [pl.BlockSpec((ROW_TILE, D_MODEL), lambda i: (i, 0))])
    return pl.pallas_call(
        functools.partial(_in_proj_body, split=split),
        grid=(N_TILES,),
        in_specs=x_specs + [pl.BlockSpec((1, 3, D_MODEL), lambda i: (_tile_seq(i), 0, 0)),
                            pl.BlockSpec((1, D_MODEL, W_ABC), lambda i: (l, 0, 0)),
                            pl.BlockSpec((1, D_MODEL, W_D), lambda i: (l, 0, 0))],
        out_specs=[pl.BlockSpec((ROW_TILE, w), lambda i: (i, 0)) for w in widths],
        out_shape=[jax.ShapeDtypeStruct((N_TOK, w), F32) for w in widths],
        compiler_params=_cparams("parallel"),
        name="in_proj",
    )(*xs, mod, w_abc, w_d)


def _arrange_body(w_ref, prot_ref, pdup_ref, pint_ref, o_ref):
    w = w_ref[0].astype(BF16)

    def perm(x, p_ref):
        return _dot(x, p_ref[...]).astype(BF16)
    o_ref[0] = jnp.concatenate(
        [w[:, 0:1024], perm(w[:, 0:256], prot_ref), perm(w[:, 256:512], prot_ref), w[:, 1024:1536],
         perm(w[:, 1536:1792], pdup_ref), perm(w[:, 1792:2304], pint_ref), w[:, 2304:2816]], axis=-1)


def _arrange_w_in(w_in, prot, pdup, pint):
    rows = 128
    const = lambda a: pl.BlockSpec(a.shape, lambda l, i: (0, 0))
    return pl.pallas_call(
        _arrange_body,
        grid=(DEPTH, D_MODEL // rows),
        in_specs=[pl.BlockSpec((1, rows, w_in.shape[-1]), lambda l, i: (l, i, 0)), const(prot), const(pdup), const(pint)],
        out_specs=pl.BlockSpec((1, rows, W_ABC), lambda l, i: (l, i, 0)),
        out_shape=jax.ShapeDtypeStruct((DEPTH, D_MODEL, W_ABC), BF16),
        compiler_params=_cparams("parallel", "parallel"),
        name="arrange_w_in",
    )(w_in, prot, pdup, pint)


def _out_proj_body(*refs, split_in, final):
    ac_ref, al_ref, b_ref, c_ref, dc_ref, dl_ref = refs[:6]
    i = pl.program_id(0)
    if split_in:
        xc_ref, xl_ref, mod_ref, w_ref, fn_ref = refs[6:11]
        x = _split_rows(i, xc_ref, xl_ref)
    else:
        x_ref, mod_ref, w_ref, fn_ref = refs[6:10]
        x = x_ref[...]
    branches = (_split_rows(i, ac_ref, al_ref), b_ref[...], c_ref[...], _split_rows(i, dc_ref, dl_ref))
    acc = None
    for j, r in enumerate(branches):
        t = _dot(r.astype(BF16), w_ref[0, j * BRANCH:(j + 1) * BRANCH, :])
        acc = t if acc is None else acc + t
    x = x + mod_ref[0][2:3] * acc
    if not final:
        refs[-1][...] = x
        return
    y = x * lax.rsqrt(jnp.mean(x * x, axis=-1, keepdims=True) + EPS) * fn_ref[...]
    yc_ref, yl_ref = refs[-2:]

    @pl.when(i < CTX_TILES)
    def _():
        yc_ref[...] = y

    @pl.when(i >= CTX_TILES)
    def _():
        yl_ref[...] = y


def _out_proj(a, b, c, d, xs, mod, w_out, l, final_norm, final):
    br = pl.BlockSpec((ROW_TILE, BRANCH), lambda i: (i, 0))
    pair = [_ctx_tile_spec(BRANCH), _lat_tile_spec(BRANCH)]
    split_in = len(xs) == 2
    x_specs = ([_ctx_tile_spec(D_MODEL), _lat_tile_spec(D_MODEL)] if split_in
               else [pl.BlockSpec((ROW_TILE, D_MODEL), lambda i: (i, 0))])
    if final:
        out_specs = [_ctx_tile_spec(D_MODEL), _lat_tile_spec(D_MODEL)]
        out_shape = [jax.ShapeDtypeStruct((N_CTX, D_MODEL), F32), jax.ShapeDtypeStruct((N_LAT, D_MODEL), F32)]
    else:
        out_specs = pl.BlockSpec((ROW_TILE, D_MODEL), lambda i: (i, 0))
        out_shape = jax.ShapeDtypeStruct((N_TOK, D_MODEL), F32)
    return pl.pallas_call(
        functools.partial(_out_proj_body, split_in=split_in, final=final),
        grid=(N_TILES,),
        in_specs=pair + [br, br] + pair + x_specs + [
            pl.BlockSpec((1, 3, D_MODEL), lambda i: (_tile_seq(i), 0, 0)),
            pl.BlockSpec((1, D_MODEL, D_MODEL), lambda i: (l, 0, 0)),
            pl.BlockSpec((1, D_MODEL), lambda i: (0, 0))],
        out_specs=out_specs,
        out_shape=out_shape,
        compiler_params=_cparams("arbitrary"),
        name="out_proj",
    )(*a, b, c, *d, *xs, mod, w_out, final_norm)


def _softmax_rows(s):
    e = jnp.exp(s - jnp.max(s, axis=-1, keepdims=True))
    return e * (1.0 / jnp.sum(e, axis=-1, keepdims=True))


def _da_kv_body(k_ref, kr_ref, v_ref, cos_ref, sin_ref, ck_ref, cv_ref, ko_ref, vo_ref):
    j = pl.program_id(1)

    @pl.when(j < LAT_TILES)
    def _():
        ko_ref[0] = (k_ref[...] * cos_ref[...] + kr_ref[...] * sin_ref[...]).astype(BF16)
        vo_ref[0] = v_ref[...].astype(BF16)

    @pl.when(j == LAT_TILES)
    def _():
        ko_ref[0] = ck_ref[0].astype(BF16)
        vo_ref[0] = cv_ref[0].astype(BF16)


def _da_latent_kv(z_a, cos, sin, cache_k, cache_v):
    def rows(col):
        return pl.BlockSpec(
            (ROW_TILE, BRANCH),
            lambda b, j: (CTX_TILES + b * LAT_TILES + jnp.minimum(j, LAT_TILES - 1), col))
    tab = pl.BlockSpec((ROW_TILE, BRANCH), lambda b, j: (jnp.minimum(j, LAT_TILES - 1), 0))
    cache = pl.BlockSpec((1, PAST_LEN, BRANCH), lambda b, j: (b, 0, 0))
    out = pl.BlockSpec((1, ROW_TILE, BRANCH), lambda b, j: (b, j, 0))
    shp = jax.ShapeDtypeStruct((N_LAT_SEQ, LAT_LEN + PAST_LEN, BRANCH), BF16)
    return pl.pallas_call(
        _da_kv_body,
        grid=(N_LAT_SEQ, LAT_TILES + 1),
        in_specs=[rows(1), rows(5), rows(2), tab, tab, cache, cache],
        out_specs=[out, out],
        out_shape=[shp, shp],
        compiler_params=_cparams("parallel", "parallel"),
        name="da_kv",
    )(z_a, z_a, z_a, cos, sin, cache_k, cache_v)


def _da_attn_body(lam_ref, ng_ref, q_ref, *rest, rope, lam_init):
    if rope:
        qr_ref, cos_ref, sin_ref, k_ref, v_ref, g_ref, o_ref = rest
        q = q_ref[...] * cos_ref[...] + qr_ref[...] * sin_ref[...]
        k = k_ref[0]
        v = v_ref[0]
    else:
        k_ref, v_ref, g_ref, o_ref = rest
        q = q_ref[...]
        k = k_ref[...].astype(BF16)
        v = v_ref[...].astype(BF16)
    q = q * (DA_QK ** -0.5)
    lv = lam_ref[...]
    lam = (jnp.exp(jnp.sum(lv[0:1] * lv[1:2], axis=-1, keepdims=True))
           - jnp.exp(jnp.sum(lv[2:3] * lv[3:4], axis=-1, keepdims=True)) + lam_init)
    lane = lax.broadcasted_iota(jnp.int32, (1, BRANCH), 1)
    acc = jnp.zeros(q.shape, F32)
    for h in range(DA_HEADS):
        q1 = jnp.where(lane // DA_QK == 2 * h, q, 0.0).astype(BF16)
        q2 = jnp.where(lane // DA_QK == 2 * h + 1, q, 0.0).astype(BF16)
        p1 = _softmax_rows(_dot_nt(q1, k))
        p2 = _softmax_rows(_dot_nt(q2, k))
        a = (p1 - lam * p2).astype(BF16)
        acc = jnp.where(lane // (2 * DA_QK) == h, _dot(a, v), acc)
    sq = acc * acc
    ms = jnp.zeros(q.shape, F32)
    for h in range(DA_HEADS):
        hm = lane // (2 * DA_QK) == h
        ms = jnp.where(hm, jnp.sum(jnp.where(hm, sq, 0.0), axis=-1, keepdims=True), ms)
    o = acc * lax.rsqrt(ms * (1.0 / (2 * DA_QK)) + EPS) * (ng_ref[...] * (1.0 - lam_init))
    o_ref[...] = o * _silu(g_ref[...])


def _da_attention(z_a, lam_vec, norm_g, lam_init, cos, sin, kv_lat):
    ng = jnp.tile(norm_g.reshape(1, 2 * DA_QK), (1, DA_HEADS))
    small = [pl.BlockSpec((4, DA_QK), lambda *_: (0, 0)), pl.BlockSpec((1, BRANCH), lambda *_: (0, 0))]

    def col(c):
        return pl.BlockSpec((ROW_TILE, BRANCH), lambda i: (i, c))
    ctx = pl.pallas_call(
        functools.partial(_da_attn_body, rope=False, lam_init=lam_init),
        grid=(CTX_TILES,),
        in_specs=small + [col(0), col(1), col(2), col(3)],
        out_specs=pl.BlockSpec((ROW_TILE, BRANCH), lambda i: (i, 0)),
        out_shape=jax.ShapeDtypeStruct((N_CTX, BRANCH), F32),
        compiler_params=_cparams("parallel"),
        name="da_attn_ctx",
    )(lam_vec, ng, z_a, z_a, z_a, z_a)

    def lcol(c):
        return pl.BlockSpec((ROW_TILE, BRANCH), lambda b, j: (CTX_TILES + b * LAT_TILES + j, c))
    tab = pl.BlockSpec((ROW_TILE, BRANCH), lambda b, j: (j, 0))
    kvs = pl.BlockSpec((1, LAT_LEN + PAST_LEN, BRANCH), lambda b, j: (b, 0, 0))
    lat = pl.pallas_call(
        functools.partial(_da_attn_body, rope=True, lam_init=lam_init),
        grid=(N_LAT_SEQ, LAT_TILES),
        in_specs=small + [lcol(0), lcol(4), tab, tab, kvs, kvs, lcol(3)],
        out_specs=pl.BlockSpec((ROW_TILE, BRANCH), lambda b, j: (b * LAT_TILES + j, 0)),
        out_shape=jax.ShapeDtypeStruct((N_LAT, BRANCH), F32),
        compiler_params=_cparams("parallel", "parallel"),
        name="da_attn_lat",
    )(lam_vec, ng, z_a, z_a, cos, sin, kv_lat[0], kv_lat[1], z_a)
    return ctx, lat


MLA_HEAD_PAD = 128
MLA_QW = MLA_HEADS * MLA_HEAD_PAD


def _mla_prep_body(cq_ref, ckv_ref, kr_ref, krr_ref, cq_t, sq_t, ck_t, sk_t, qn_ref, kvn_ref, wq_ref, wqr_ref,
                   q_out, ckv_out, kr_out):
    cq = cq_ref[...]
    ms = jnp.sum(cq * cq, axis=-1, keepdims=True) * (1.0 / MLA_Q_RANK)
    qn = (cq * lax.rsqrt(ms + EPS) * qn_ref[...]).astype(BF16)
    q = _dot(qn, wq_ref[...]) * cq_t[...] + _dot(qn, wqr_ref[...]) * sq_t[...]
    q_out[...] = (q * ((MLA_NOPE + MLA_ROPE) ** -0.5)).astype(BF16)
    ckv = ckv_ref[...]
    ckv_out[...] = ckv * lax.rsqrt(jnp.mean(ckv * ckv, axis=-1, keepdims=True) + EPS) * kvn_ref[...]
    kr_out[...] = kr_ref[...] * ck_t[...] + krr_ref[...] * sk_t[...]


def _mla_prep(z_d, tabs, q_norm_pad, kv_norm, wq, wqr):
    def tab(w):
        return pl.BlockSpec(
            (ROW_TILE, w), lambda i: (jnp.where(i < CTX_TILES, LAT_TILES, (i - CTX_TILES) % LAT_TILES), 0))

    def col(w, c):
        return pl.BlockSpec((ROW_TILE, w), lambda i: (i, c))

    def const(shape):
        return pl.BlockSpec(shape, lambda i: (0, 0))
    return pl.pallas_call(
        _mla_prep_body,
        grid=(N_TILES,),
        in_specs=[col(256, 0), col(128, 2), col(128, 3), col(128, 6),
                  tab(MLA_QW), tab(MLA_QW), tab(128), tab(128),
                  const((1, 256)), const((1, 128)), const((256, MLA_QW)), const((256, MLA_QW))],
        out_specs=[col(MLA_QW, 0), col(128, 0), col(128, 0)],
        out_shape=[jax.ShapeDtypeStruct((N_TOK, MLA_QW), BF16),
                   jax.ShapeDtypeStruct((N_TOK, 128), F32),
                   jax.ShapeDtypeStruct((N_TOK, 128), F32)],
        compiler_params=_cparams("parallel"),
        name="mla_prep",
    )(z_d, z_d, z_d, z_d, *tabs, q_norm_pad, kv_norm, wq, wqr)


def _mla_kv_body(ckv_ref, kr_ref, wk_ref, wv_ref, k_out, v_out):
    c = ckv_ref[...].astype(BF16)
    kr = kr_ref[...]
    k_out[...] = (_dot(c, wk_ref[...]) + jnp.concatenate([kr] * MLA_HEADS, axis=-1)).astype(BF16)
    v_out[...] = _dot(c, wv_ref[...]).astype(BF16)


def _mla_kv(ckv, kr, wk, wv, n_rows):
    return pl.pallas_call(
        _mla_kv_body,
        grid=(n_rows // ROW_TILE,),
        in_specs=[pl.BlockSpec((ROW_TILE, 128), lambda i: (i, 0)),
                  pl.BlockSpec((ROW_TILE, 128), lambda i: (i, 0)),
                  pl.BlockSpec((128, MLA_QW), lambda i: (0, 0)),
                  pl.BlockSpec((128, BRANCH), lambda i: (0, 0))],
        out_specs=[pl.BlockSpec((ROW_TILE, MLA_QW), lambda i: (i, 0)),
                   pl.BlockSpec((ROW_TILE, BRANCH), lambda i: (i, 0))],
        out_shape=[jax.ShapeDtypeStruct((n_rows, MLA_QW), BF16),
                   jax.ShapeDtypeStruct((n_rows, BRANCH), BF16)],
        compiler_params=_cparams("parallel"),
        name="mla_kv",
    )(ckv, kr, wk, wv)


def _mla_attn_body(q_ref, k_ref, v_ref, g_ref, o_ref):
    q = q_ref[...]
    k = k_ref[...].reshape(-1, MLA_QW)
    v = v_ref[...].reshape(-1, BRANCH)
    lane = lax.broadcasted_iota(jnp.int32, (1, BRANCH), 1)
    acc = jnp.zeros((q.shape[0], BRANCH), F32)
    for h in range(MLA_HEADS):
        sl = slice(h * MLA_HEAD_PAD, (h + 1) * MLA_HEAD_PAD)
        p = _softmax_rows(_dot_nt(q[:, sl], k[:, sl])).astype(BF16)
        acc = jnp.where(lane // 64 == h, _dot(p, v), acc)
    o_ref[...] = acc * _silu(g_ref[...])


def _mla_attention(z_d, q, k_ctx, v_ctx, k_lat, v_lat):
    ctx = pl.pallas_call(
        _mla_attn_body,
        grid=(CTX_TILES,),
        in_specs=[pl.BlockSpec((ROW_TILE, MLA_QW), lambda i: (i, 0)),
                  pl.BlockSpec((ROW_TILE, MLA_QW), lambda i: (i, 0)),
                  pl.BlockSpec((ROW_TILE, BRANCH), lambda i: (i, 0)),
                  pl.BlockSpec((ROW_TILE, BRANCH), lambda i: (i, 2))],
        out_specs=pl.BlockSpec((ROW_TILE, BRANCH), lambda i: (i, 0)),
        out_shape=jax.ShapeDtypeStruct((N_CTX, BRANCH), F32),
        compiler_params=_cparams("parallel"),
        name="mla_attn_ctx",
    )(q, k_ctx, v_ctx, z_d)
    lk = LAT_LEN + PAST_LEN
    lat = pl.pallas_call(
        _mla_attn_body,
        grid=(N_LAT_SEQ, LAT_TILES),
        in_specs=[pl.BlockSpec((ROW_TILE, MLA_QW), lambda b, j: (CTX_TILES + b * LAT_TILES + j, 0)),
                  pl.BlockSpec((1, lk, MLA_QW), lambda b, j: (b, 0, 0)),
                  pl.BlockSpec((1, lk, BRANCH), lambda b, j: (b, 0, 0)),
                  pl.BlockSpec((ROW_TILE, BRANCH), lambda b, j: (CTX_TILES + b * LAT_TILES + j, 2))],
        out_specs=pl.BlockSpec((ROW_TILE, BRANCH), lambda b, j: (b * LAT_TILES + j, 0)),
        out_shape=jax.ShapeDtypeStruct((N_LAT, BRANCH), F32),
        compiler_params=_cparams("parallel", "parallel"),
        name="mla_attn_lat",
    )(q, k_lat.reshape(N_LAT_SEQ, lk, MLA_QW), v_lat.reshape(N_LAT_SEQ, lk, BRANCH), z_d)
    return ctx, lat


S5_TAP = S5_CHUNK * S5_CH
S5_NCHUNK = N_TOK // S5_CHUNK
S5_CTX_CH = N_CTX // S5_CHUNK
S5_CTX_SEQ_CH = CTX_LEN // S5_CHUNK
S5_LAT_SEQ_CH = LAT_LEN // S5_CHUNK
S5_SCAN_STEPS = S5_LAT_SEQ_CH.bit_length() - 1


def _s5_body(x_ref, mt_ref, bst_ref, cot_ref, a_ref, h0_ref, y_ref, fin_ref):
    x = x_ref[...].reshape(S5_TAP, S5_NCHUNK)
    y = _dot(mt_ref[0, 0], x)
    s = _dot(bst_ref[0, 0], x)
    lane = lax.broadcasted_iota(jnp.int32, (1, S5_NCHUNK), 1)
    is_lat = lane >= S5_CTX_CH
    pos_f = jnp.where(is_lat, (lane - S5_CTX_CH) & (S5_LAT_SEQ_CH - 1), lane & (S5_CTX_SEQ_CH - 1))
    pos_b = jnp.where(is_lat, S5_LAT_SEQ_CH - 1, S5_CTX_SEQ_CH - 1) - pos_f
    hin = []
    for d in range(2):
        n = S5_STATE
        sre, sim = s[2 * d * n:(2 * d + 1) * n], s[(2 * d + 1) * n:(2 * d + 2) * n]
        are = jnp.concatenate([a_ref[0, 0, 2 * d]] * (S5_NCHUNK // 128), axis=-1)
        aim = jnp.concatenate([a_ref[0, 0, 2 * d + 1]] * (S5_NCHUNK // 128), axis=-1)
        pos = pos_f if d == 0 else pos_b
        h0r, h0i = jnp.zeros_like(sre), jnp.zeros_like(sre)
        for b in range(N_LAT_SEQ):
            first = S5_CTX_CH + b * S5_LAT_SEQ_CH + (0 if d == 0 else S5_LAT_SEQ_CH - 1)
            h0r = jnp.where(lane == first, h0_ref[0, 2 * d][:, b:b + 1], h0r)
            h0i = jnp.where(lane == first, h0_ref[0, 2 * d + 1][:, b:b + 1], h0i)
        xr = sre + are * h0r - aim * h0i
        xi = sim + are * h0i + aim * h0r
        pr, pi = are, aim
        for j in range(S5_SCAN_STEPS):
            sh = 1 << j
            shift = sh if d == 0 else S5_NCHUNK - sh
            rr, ri = pltpu.roll(xr, shift, 1), pltpu.roll(xi, shift, 1)
            ok = pos >= sh
            xr, xi = (xr + jnp.where(ok, pr * rr - pi * ri, 0.0), xi + jnp.where(ok, pr * ri + pi * rr, 0.0))
            pr, pi = pr * pr - pi * pi, 2.0 * pr * pi
        fin_ref[0, 2 * d] = xr[:, :S5_CTX_CH]
        fin_ref[0, 2 * d + 1] = xi[:, :S5_CTX_CH]
        one = 1 if d == 0 else S5_NCHUNK - 1
        hin.append(jnp.where(pos >= 1, pltpu.roll(xr, one, 1), h0r))
        hin.append(jnp.where(pos >= 1, pltpu.roll(xi, one, 1), h0i))
    y = y + _dot(cot_ref[0, 0], jnp.concatenate(hin, axis=0).astype(BF16))
    y_ref[...] = y.reshape(S5_CHUNK, S5_CH, S5_NCHUNK)


def _s5_scan(x_all, mt, bst, cot, a16, h0, l):
    g = S5_GROUPS
    sq = pl.BlockSpec((1, 1, S5_TAP, S5_TAP), lambda i: (l, i, 0, 0))
    st = pl.BlockSpec((1, 4, S5_STATE, 128), lambda i: (i, 0, 0, 0))
    return pl.pallas_call(
        _s5_body,
        grid=(g,),
        in_specs=[pl.BlockSpec((S5_CHUNK, S5_CH, S5_NCHUNK), lambda i: (0, i, 0)), sq, sq, sq,
                  pl.BlockSpec((1, 1, 4, S5_STATE, 128), lambda i: (l, i, 0, 0, 0)), st],
        out_specs=[pl.BlockSpec((S5_CHUNK, S5_CH, S5_NCHUNK), lambda i: (0, i, 0)),
                   pl.BlockSpec((1, 4, S5_STATE, S5_CTX_CH), lambda i: (i, 0, 0, 0))],
        out_shape=[jax.ShapeDtypeStruct((S5_CHUNK, BRANCH, S5_NCHUNK), F32),
                   jax.ShapeDtypeStruct((g, 4, S5_STATE, S5_CTX_CH), F32)],
        compiler_params=_cparams("parallel"),
        name="s5_scan",
    )(x_all, mt, bst, cot, a16, h0)


def _s5_out_body(y_ref, u_ref, g_ref, d_ref, w_ref, o_ref):
    y = u_ref[...] * d_ref[...] + y_ref[...]
    ge = 0.5 * y * (1.0 + jnp.tanh(0.7978845608028654 * (y + 0.044715 * (y * y * y))))
    gl = _dot(ge.astype(BF16), w_ref[...])
    o_ref[...] = gl[:, :BRANCH] * (1.0 / (1.0 + jnp.exp(-gl[:, BRANCH:]))) * _silu(g_ref[...])


def _s5_out(y_ssm, z_b, d_skip, w_glu):
    def col(c):
        return pl.BlockSpec((ROW_TILE, BRANCH), lambda i: (i, c))
    return pl.pallas_call(
        _s5_out_body,
        grid=(N_TILES,),
        in_specs=[col(0), col(0), col(1),
                  pl.BlockSpec((1, BRANCH), lambda i: (0, 0)),
                  pl.BlockSpec((BRANCH, 2 * BRANCH), lambda i: (0, 0))],
        out_specs=col(0),
        out_shape=jax.ShapeDtypeStruct((N_TOK, BRANCH), F32),
        compiler_params=_cparams("parallel"),
        name="s5_out",
    )(y_ssm, z_b, z_b, d_skip, w_glu)


HG_CHUNK = ROW_TILE
HG_W = 2 * HG_HEADS * HG_DK
HG_HEAD_W = 2 * HG_DK
HG_LAT_CHUNKS = LAT_LEN // HG_CHUNK
HG_CHUNKS = N_TOK // HG_CHUNK


def _hg_gates(z, lb):
    e = jnp.exp(-jnp.abs(z))
    r = 1.0 / (1.0 + e)
    sig_pos = jnp.where(z >= 0, r, e * r)
    sig_neg = jnp.where(z >= 0, e * r, r)
    return lb + (1.0 - lb) * sig_pos, (1.0 - lb) * sig_neg


def _bcast_row(x, period, r):
    n, w = x.shape
    if period >= 8:
        x3 = x.reshape(n // period, period, w)
        return jnp.broadcast_to(x3[:, r:r + 1, :], x3.shape).reshape(n, w)
    x3 = x.reshape(n // 8, 8, w)
    sub = lax.broadcasted_iota(jnp.int32, (1, 8, 1), 1)
    out = None
    for j in range(8 // period):
        b = jnp.broadcast_to(x3[:, j * period + r:j * period + r + 1, :], x3.shape)
        out = b if out is None else jnp.where(sub >= j * period, b, out)
    return out.reshape(n, w)


def _hg_scans(f, isb):
    n = f.shape[0]
    row = lax.broadcasted_iota(jnp.int32, (n, 1), 0)
    p, r = f, jnp.ones_like(f)
    levels = []
    h, sh = 1, 0
    while h < n:
        levels.append((h, sh, p, r))
        up = (row >> sh) & 1
        tot_p = jnp.where(isb == 1, _bcast_row(p, 2 * h, h), _bcast_row(p, 2 * h, h - 1))
        tot_r = jnp.where(isb == 1, _bcast_row(p, 2 * h, 0), _bcast_row(p, 2 * h, 2 * h - 1))
        p = p * jnp.where(up != isb, tot_p, 1.0)
        r = r * jnp.where(up == isb, tot_r, 1.0)
        h, sh = 2 * h, sh + 1
    return levels, p, r


def _hg_state_body(zf_ref, zb_ref, vf_ref, vb_ref, lb_ref, s0_ref, sf_out, sb_out, fin_out, s_scr):
    i = pl.program_id(0)
    first = jnp.logical_or(i < N_CTX_SEQ, (i - N_CTX_SEQ) % HG_LAT_CHUNKS == 0)

    @pl.when(first)
    def _():
        s_scr[...] = s0_ref[0]

    sf_out[0] = s_scr[:, 0:HG_DK, :]
    sb_out[0] = s_scr[:, HG_DK:, :]
    lane5 = lax.broadcasted_iota(jnp.int32, (1, HG_W), 1)
    isb = (lane5 >> 6) & 1
    z = jnp.where(isb == 1, zb_ref[...], zf_ref[...])
    f, k = _hg_gates(z, lb_ref[...])
    r, ptot = _hg_chunk_decay(f, isb)
    kt = k * r
    lane = lax.broadcasted_iota(jnp.int32, (1, BRANCH), 1)
    vf = vf_ref[...]
    vb = vb_ref[...]
    for hd in range(HG_HEADS):
        sl = slice(hd * HG_HEAD_W, (hd + 1) * HG_HEAD_W)
        kth = kt[:, sl].T.astype(BF16)
        hm = (lane >> 6) == hd
        d_f = _dot(kth, jnp.where(hm, vf, 0.0).astype(BF16))
        d_b = _dot(kth, jnp.where(hm, vb, 0.0).astype(BF16))
        ds = jnp.concatenate([d_f[:HG_DK], d_b[HG_DK:]], axis=0)
        pcol = jnp.broadcast_to(ptot[:, sl], (HG_HEAD_W, HG_HEAD_W)).T[:, 0:1]
        s_scr[hd] = s_scr[hd] * pcol + ds

    @pl.when(i < N_CTX_SEQ)
    def _():
        for hd in range(HG_HEADS):
            fin_out[0, hd] = s_scr[hd][:, hd * HG_DK:(hd + 1) * HG_DK]


def _hg_chunk_decay(f, isb):
    n = f.shape[0]
    row = lax.broadcasted_iota(jnp.int32, (n, 1), 0)
    dist = jnp.where(isb == 1, row, n - 1 - row)
    x = f
    sh = 1
    while sh < n:
        src = jnp.where(isb == 1, pltpu.roll(x, sh, 0), pltpu.roll(x, n - sh, 0))
        x = x * jnp.where(dist >= sh, src, 1.0)
        sh *= 2
    total = jnp.where(isb == 1, x[n - 1:n], x[0:1])
    nxt = jnp.where(isb == 1, pltpu.roll(x, 1, 0), pltpu.roll(x, n - 1, 0))
    return jnp.where(dist >= 1, nxt, 1.0), total


def _hg_rev(i):
    j = i - N_CTX_SEQ
    return jnp.where(i < N_CTX_SEQ, i, N_CTX_SEQ + (j // HG_LAT_CHUNKS) * HG_LAT_CHUNKS
                     + (HG_LAT_CHUNKS - 1 - j % HG_LAT_CHUNKS))


def _hg_states(z_c, lb, s0):
    zz_f = pl.BlockSpec((HG_CHUNK, HG_W), lambda i: (i, 1))
    zz_b = pl.BlockSpec((HG_CHUNK, HG_W), lambda i: (_hg_rev(i), 1))
    v_f = pl.BlockSpec((HG_CHUNK, BRANCH), lambda i: (i, 4))
    v_b = pl.BlockSpec((HG_CHUNK, BRANCH), lambda i: (_hg_rev(i), 4))
    st = (HG_HEADS, HG_HEAD_W, BRANCH)
    half = (HG_HEADS, HG_DK, BRANCH)
    fin = (HG_HEADS, HG_HEAD_W, HG_DK)
    return pl.pallas_call(
        _hg_state_body,
        grid=(HG_CHUNKS,),
        in_specs=[zz_f, zz_b, v_f, v_b,
                  pl.BlockSpec((1, HG_W), lambda i: (0, 0)),
                  pl.BlockSpec((1,) + st, lambda i: (
                      jnp.where(i < N_CTX_SEQ, 0, 1 + (i - N_CTX_SEQ) // HG_LAT_CHUNKS), 0, 0, 0))],
        out_specs=[pl.BlockSpec((1,) + half, lambda i: (i, 0, 0, 0)),
                   pl.BlockSpec((1,) + half, lambda i: (_hg_rev(i), 0, 0, 0)),
                   pl.BlockSpec((1,) + fin, lambda i: (jnp.minimum(i, N_CTX_SEQ - 1), 0, 0, 0))],
        out_shape=[jax.ShapeDtypeStruct((HG_CHUNKS,) + half, F32),
                   jax.ShapeDtypeStruct((HG_CHUNKS,) + half, F32),
                   jax.ShapeDtypeStruct((N_CTX_SEQ,) + fin, F32)],
        scratch_shapes=[pltpu.VMEM(st, F32)],
        compiler_params=_cparams("arbitrary"),
        name="hg_states",
    )(z_c, z_c, z_c, z_c, lb, s0)


def _hg_main_body(qq_ref, zz_ref, v_ref, g_ref, sf_ref, sb_ref, lb_ref, ng_ref, o_ref):
    n = HG_CHUNK
    qq = qq_ref[...]
    lane5 = lax.broadcasted_iota(jnp.int32, (1, HG_W), 1)
    isb = (lane5 >> 6) & 1
    f, k = _hg_gates(zz_ref[...], lb_ref[...])
    levels, pfull, _ = _hg_scans(f, isb)
    row = lax.broadcasted_iota(jnp.int32, (n, 1), 0)
    col = lax.broadcasted_iota(jnp.int32, (1, n), 1)
    ops = [(qq.astype(BF16), k.astype(BF16), row == col)]
    for h, sh, p, r in levels:
        up = (row >> sh) & 1
        qt = jnp.where(up != isb, qq * p, 0.0).astype(BF16)
        kt = jnp.where(up == isb, k * r, 0.0).astype(BF16)
        ops.append((qt, kt, (row >> (sh + 1)) == (col >> (sh + 1))))
    qc = (qq * pfull).astype(BF16)
    vb = v_ref[...].astype(BF16)
    lane = lax.broadcasted_iota(jnp.int32, (1, BRANCH), 1)
    acc = jnp.zeros((n, BRANCH), F32)
    for hd in range(HG_HEADS):
        sl = slice(hd * HG_HEAD_W, (hd + 1) * HG_HEAD_W)
        a = jnp.zeros((n, n), F32)
        for qt, kt, mask in ops:
            a = a + jnp.where(mask, _dot_nt(qt[:, sl], kt[:, sl]), 0.0)
        s_in = jnp.concatenate([sf_ref[0, hd], sb_ref[0, hd]], axis=0).astype(BF16)
        o_h = _dot(a.astype(BF16), vb) + _dot(qc[:, sl], s_in)
        acc = jnp.where((lane >> 6) == hd, o_h, acc)
    sq = acc * acc
    ms = jnp.zeros((n, BRANCH), F32)
    for hd in range(HG_HEADS):
        hm = (lane >> 6) == hd
        ms = jnp.where(hm, jnp.sum(jnp.where(hm, sq, 0.0), axis=-1, keepdims=True), ms)
    o_ref[...] = acc * lax.rsqrt(ms * (1.0 / HG_DK) + EPS) * ng_ref[...] * _silu(g_ref[...])


def _hg_main(z_c, s_f, s_b, lb, norm_g):
    half = (1, HG_HEADS, HG_DK, BRANCH)
    return pl.pallas_call(
        _hg_main_body,
        grid=(HG_CHUNKS,),
        in_specs=[pl.BlockSpec((HG_CHUNK, HG_W), lambda i: (i, 0)),
                  pl.BlockSpec((HG_CHUNK, HG_W), lambda i: (i, 1)),
                  pl.BlockSpec((HG_CHUNK, BRANCH), lambda i: (i, 4)),
                  pl.BlockSpec((HG_CHUNK, BRANCH), lambda i: (i, 5)),
                  pl.BlockSpec(half, lambda i: (i, 0, 0, 0)),
                  pl.BlockSpec(half, lambda i: (i, 0, 0, 0)),
                  pl.BlockSpec((1, HG_W), lambda i: (0, 0)),
                  pl.BlockSpec((1, BRANCH), lambda i: (0, 0))],
        out_specs=pl.BlockSpec((HG_CHUNK, BRANCH), lambda i: (i, 0)),
        out_shape=jax.ShapeDtypeStruct((N_TOK, BRANCH), F32),
        compiler_params=_cparams("parallel"),
        name="hg_main",
    )(z_c, z_c, z_c, z_c, s_f, s_b, lb, norm_g)


def _rot_src(cols):
    j = np.arange(len(cols))
    return cols[(j // 16) * 16 + ((j % 16) + 8) % 16], np.where(j % 16 < 8, -1.0, 1.0)


def _take_cols(w, plan):
    idx = np.concatenate([p[0] for p in plan]).astype(np.int32)
    sign = np.concatenate([np.broadcast_to(p[1], p[0].shape) for p in plan]).astype(np.float32)
    return jnp.take(w, jnp.asarray(idx), axis=-1) * jnp.asarray(sign)


def _zeros(n):
    return (np.zeros(n, np.int64), 0.0)


_IN_OFF = {}
_off = 0
for _name, _n in (("da_q", 256), ("da_k", 256), ("da_v", 256), ("da_g", 256), ("s5_u", 256), ("s5_g", 256),
                  ("hg_q", 256), ("hg_ff", 256), ("hg_fb", 256), ("hg_i", 256), ("hg_g", 256),
                  ("mla_cq", MLA_Q_RANK), ("mla_ckv", MLA_KV_RANK), ("mla_kr", MLA_ROPE), ("mla_g", 256)):
    _IN_OFF[_name] = np.arange(_off, _off + _n)
    _off += _n


def _perm_matrix(plan, first, k):
    idx = np.concatenate([p[0] for p in plan]) - first
    sign = np.concatenate([np.broadcast_to(p[1], p[0].shape) for p in plan])
    m = np.zeros((k, len(idx)), np.float32)
    m[idx, np.arange(len(idx))] = sign
    return jnp.asarray(m, BF16)


def _in_proj_weights(w_in):
    c = _IN_OFF

    def per_head(x, y):
        return (np.concatenate([c[x].reshape(HG_HEADS, HG_DK), c[y].reshape(HG_HEADS, HG_DK)], axis=1).reshape(-1), 1.0)
    prot = _perm_matrix([_rot_src(c["da_q"])], 0, 256)
    pdup = _perm_matrix([per_head("hg_q", "hg_q")], c["hg_q"][0], 256)
    pint = _perm_matrix([per_head("hg_ff", "hg_fb")], c["hg_ff"][0], 512)
    col = lambda name: w_in[..., c[name][0]:c[name][-1] + 1]
    zero = lambda n: jnp.zeros(w_in.shape[:-1] + (n,), w_in.dtype)
    kr = col("mla_kr")
    kr_rot = jnp.concatenate([-kr[..., 8:16], kr[..., 0:8], -kr[..., 24:32], kr[..., 16:24]], axis=-1)
    tail = 128 - MLA_NOPE - MLA_ROPE
    w_d = jnp.concatenate([col("mla_cq"), zero(256 - MLA_Q_RANK), col("mla_ckv"), zero(MLA_NOPE), kr, zero(tail),
                           col("mla_g"), zero(MLA_NOPE), kr_rot, zero(tail)], axis=-1)
    return _arrange_w_in(w_in, prot, pdup, pint), w_d.astype(BF16)


def _rope_tables():
    t = np.arange(LAT_LEN)
    pos = np.stack([t // GRID_W, t % GRID_W], axis=1).astype(np.float32)
    inv_freq = (np.float32(ROPE_BASE) ** (-np.arange(8, dtype=np.float32) / np.float32(8))).astype(np.float32)
    r = np.arange(MLA_ROPE)
    ang = (pos[:, r // 16] * inv_freq[r % 8][None, :]).astype(np.float32)
    cos32, sin32 = jnp.asarray(np.cos(ang.astype(np.float64)), F32), jnp.asarray(np.sin(ang.astype(np.float64)), F32)
    cos_a, sin_a = jnp.tile(cos32, (1, 8)), jnp.tile(sin32, (1, 8))
    ones, zeros = jnp.ones((LAT_LEN, 1), F32), jnp.zeros((LAT_LEN, 1), F32)
    cos_h = jnp.concatenate([jnp.tile(ones, (1, 64)), cos32, jnp.tile(ones, (1, 32))], axis=1)
    sin_h = jnp.concatenate([jnp.tile(zeros, (1, 64)), sin32, jnp.tile(zeros, (1, 32))], axis=1)

    def with_identity(c, s):
        return (jnp.concatenate([c, jnp.ones((ROW_TILE, c.shape[1]), F32)], axis=0),
                jnp.concatenate([s, jnp.zeros((ROW_TILE, s.shape[1]), F32)], axis=0))
    cq, sq = with_identity(jnp.tile(cos_h, (1, MLA_HEADS)), jnp.tile(sin_h, (1, MLA_HEADS)))
    ck, sk = with_identity(cos_h, sin_h)
    return (cos_a, sin_a), (cq, sq, ck, sk)


def _mla_weights(w_uq, w_ukv, q_norm):
    hd = MLA_NOPE + MLA_ROPE
    pad_tail = _zeros(MLA_HEAD_PAD - hd)
    q_plan, qr_plan, k_plan, v_plan = [], [], [], []
    for h in range(MLA_HEADS):
        nope, rope = np.arange(h * hd, h * hd + MLA_NOPE), np.arange(h * hd + MLA_NOPE, (h + 1) * hd)
        q_plan += [(nope, 1.0), (rope, 1.0), pad_tail]
        qr_plan += [_zeros(MLA_NOPE), _rot_src(rope), pad_tail]
        k_plan += [(np.arange(h * 2 * MLA_NOPE, h * 2 * MLA_NOPE + MLA_NOPE), 1.0), _zeros(MLA_HEAD_PAD - MLA_NOPE)]
        v_plan += [(np.arange(h * 2 * MLA_NOPE + MLA_NOPE, (h + 1) * 2 * MLA_NOPE), 1.0)]
    pad_rows = lambda x: jnp.pad(x, ((0, 256 - MLA_Q_RANK), (0, 0))).astype(BF16)
    qn = jnp.pad(q_norm, (0, 256 - MLA_Q_RANK)).reshape(1, 256)
    return (pad_rows(_take_cols(w_uq, q_plan)), pad_rows(_take_cols(w_uq, qr_plan)),
            _take_cols(w_ukv, k_plan).astype(BF16), _take_cols(w_ukv, v_plan).astype(BF16), qn)


def _s5_tables(a_re, a_im, log_dt, b_re, b_im, c_re, c_im):
    hp = lax.Precision.HIGHEST
    nl, g, n, ch, t = a_re.shape[0], S5_GROUPS, S5_STATE, S5_CH, S5_CHUNK
    step = jnp.exp(log_dt)[..., None]
    mag = jnp.exp(a_re * step)
    ab_re, ab_im = mag * jnp.cos(a_im * step), mag * jnp.sin(a_im * step)
    den = a_re * a_re + a_im * a_im
    f_re = ((ab_re - 1.0) * a_re + ab_im * a_im) / den
    f_im = (ab_im * a_re - (ab_re - 1.0) * a_im) / den
    bb_re = f_re[..., None] * b_re - f_im[..., None] * b_im
    bb_im = f_re[..., None] * b_im + f_im[..., None] * b_re
    tau = jnp.arange(t + 1, dtype=F32).reshape(t + 1, 1, 1, 1, 1)
    pmag = jnp.exp(tau * (a_re * step))
    pw_re, pw_im = pmag * jnp.cos(tau * (a_im * step)), pmag * jnp.sin(tau * (a_im * step))
    w_re = c_re[None] * pw_re[..., None, :] - c_im[None] * pw_im[..., None, :]
    w_im = c_re[None] * pw_im[..., None, :] + c_im[None] * pw_re[..., None, :]
    taps = (jnp.einsum("tldghp,ldgpk->tldghk", w_re, bb_re, precision=hp)
            - jnp.einsum("tldghp,ldgpk->tldghk", w_im, bb_im, precision=hp))
    kall = jnp.concatenate([taps[t - 1:0:-1, :, 1], (taps[0, :, 0] + taps[0, :, 1])[None], taps[1:t, :, 0]], axis=0)
    ti = jnp.arange(t)
    m = kall[ti[:, None] - ti[None, :] + t - 1]
    mt = m.transpose(2, 3, 0, 4, 1, 5).reshape(nl, g, t * ch, t * ch)

    def in_mat(p_re, p_im, d):
        re = p_re[..., None] * bb_re[None, :, d] - p_im[..., None] * bb_im[None, :, d]
        im = p_re[..., None] * bb_im[None, :, d] + p_im[..., None] * bb_re[None, :, d]
        f = lambda x: x.transpose(1, 2, 3, 0, 4).reshape(nl, g, n, t * ch)
        return [f(re), f(im)]
    bst = jnp.concatenate(in_mat(pw_re[t - 1::-1, :, 0], pw_im[t - 1::-1, :, 0], 0)
                          + in_mat(pw_re[:t, :, 1], pw_im[:t, :, 1], 1), axis=2)

    def out_mat(x):
        return x.transpose(1, 2, 0, 3, 4).reshape(nl, g, t * ch, n)
    cot = jnp.concatenate([out_mat(w_re[1:t + 1, :, 0]), out_mat(-w_im[1:t + 1, :, 0]),
                           out_mat(w_re[t:0:-1, :, 1]), out_mat(-w_im[t:0:-1, :, 1])], axis=-1)
    a16 = jnp.stack([pw_re[t, :, 0], pw_im[t, :, 0], pw_re[t, :, 1], pw_im[t, :, 1]], axis=2)
    a16 = jnp.broadcast_to(a16[..., None], (nl, g, 4, n, 128))
    return mt.astype(BF16), bst.astype(BF16), cot.astype(BF16), a16


def _s5_chunk_lanes(u):
    return u.reshape(S5_NCHUNK, S5_CHUNK, BRANCH).transpose(1, 2, 0)


def _s5_token_rows(y):
    return y.transpose(2, 0, 1).reshape(N_TOK, BRANCH)


def kernel(x_prompt, x_sample, cache_diff_k, cache_diff_v, state_s5, state_hgrn, cache_mla_ckv, cache_mla_krope, c, c_ctx, w_mod, b_mod, w_in, w_out, da_lambda, da_norm, s5_a_re, s5_a_im, s5_log_dt, s5_b_re, s5_b_im, s5_c_re, s5_c_im, s5_d, s5_w_glu, hg_lb, hg_norm, mla_q_norm, mla_w_uq, mla_kv_norm, mla_w_ukv, final_norm):
    lb_w = jax.nn.softmax(hg_lb.astype(F32), axis=0)
    lb_all = jnp.cumsum(lb_w, axis=0) - lb_w[0:1]
    c_rows = jnp.concatenate([c_ctx[None], c, jnp.zeros((8 - 1 - N_LAT_SEQ, D_MODEL), F32)], axis=0)
    mods = _modulation(c_rows, w_mod, b_mod)
    (cos_a, sin_a), mla_tabs = _rope_tables()
    xs = (x_prompt.reshape(N_CTX, D_MODEL), x_sample.reshape(N_LAT, D_MODEL))
    new_k, new_v, new_s5, new_hg, new_ckv, new_kr = [], [], [], [], [], []
    s5_tabs = _s5_tables(s5_a_re, s5_a_im, s5_log_dt, s5_b_re, s5_b_im, s5_c_re, s5_c_im)
    w_abc, w_d = _in_proj_weights(w_in)
    w_out_bf = w_out.astype(BF16)
    for l in range(DEPTH):
        mod = mods[l, :3].reshape(3, 3, D_MODEL)
        z_a, z_b, z_c, z_d = _in_proj(xs, mod, w_abc, w_d, l)

        lam_init = 0.8 - 0.6 * math.exp(-0.3 * l)
        kv_lat = _da_latent_kv(z_a, cos_a, sin_a,
                               cache_diff_k[:, l].reshape(N_LAT_SEQ, PAST_LEN, BRANCH),
                               cache_diff_v[:, l].reshape(N_LAT_SEQ, PAST_LEN, BRANCH))
        a_out = _da_attention(z_a, da_lambda[l], da_norm[l], lam_init, cos_a, sin_a, kv_lat)
        new_k.append(z_a[:N_CTX, 256:512].reshape(N_CTX_SEQ, CTX_LEN, DA_HEADS, 2 * DA_QK))
        new_v.append(z_a[:N_CTX, 512:768].reshape(N_CTX_SEQ, CTX_LEN, DA_HEADS, 2 * DA_QK))

        h0 = state_s5[:, l].transpose(2, 1, 4, 3, 0).reshape(S5_GROUPS, 4, S5_STATE, N_LAT_SEQ)
        h0 = jnp.pad(h0, ((0, 0), (0, 0), (0, 0), (0, 128 - N_LAT_SEQ)))
        y_all, fin = _s5_scan(_s5_chunk_lanes(z_b[:, :BRANCH]).astype(BF16), *s5_tabs, h0, l)
        b_out = _s5_out(_s5_token_rows(y_all), z_b, s5_d[l].reshape(1, BRANCH), s5_w_glu[l].astype(BF16))
        fin = jnp.stack([fin[:, 0:2, :, S5_CTX_SEQ_CH - 1::S5_CTX_SEQ_CH], fin[:, 2:4, :, 0::S5_CTX_SEQ_CH]], axis=1)
        new_s5.append(fin.transpose(4, 1, 0, 3, 2))

        lb = jnp.concatenate([lb_all[l, 0].reshape(HG_HEADS, HG_DK), lb_all[l, 1].reshape(HG_HEADS, HG_DK)],
                             axis=-1).reshape(1, HG_W)
        head_eye = jnp.eye(HG_HEADS, dtype=F32)
        s0 = state_hgrn[:, l].transpose(0, 2, 1, 3, 4).reshape(N_LAT_SEQ, HG_HEADS, HG_HEAD_W, 1, HG_DK)
        s0 = jnp.pad(s0 * head_eye[None, :, None, :, None], ((1, 0), (0, 0), (0, 0), (0, 0), (0, 0)))
        s_f, s_b, s_fin = _hg_states(z_c, lb, s0.reshape(1 + N_LAT_SEQ, HG_HEADS, HG_HEAD_W, BRANCH))
        c_out = _hg_main(z_c, s_f, s_b, lb, jnp.tile(hg_norm[l].reshape(1, HG_DK), (1, HG_HEADS)))
        new_hg.append(s_fin.reshape(N_CTX_SEQ, HG_HEADS, 2, HG_DK, HG_DK).transpose(0, 2, 1, 3, 4))

        wq, wqr, wk, wv, qn = _mla_weights(mla_w_uq[l], mla_w_ukv[l], mla_q_norm[l])
        q, ckv_n, kr = _mla_prep(z_d, mla_tabs, qn, mla_kv_norm[l].reshape(1, MLA_KV_RANK), wq, wqr)
        k_ctx, v_ctx = _mla_kv(ckv_n, kr, wk, wv, N_CTX)
        kr_cache = jnp.pad(cache_mla_krope[:, l], ((0, 0), (0, 0), (MLA_NOPE, 128 - MLA_NOPE - MLA_ROPE)))
        lk = LAT_LEN + PAST_LEN
        ckv_all = jnp.concatenate([ckv_n[N_CTX:].reshape(N_LAT_SEQ, LAT_LEN, 128), cache_mla_ckv[:, l]], axis=1)
        kr_all = jnp.concatenate([kr[N_CTX:].reshape(N_LAT_SEQ, LAT_LEN, 128), kr_cache], axis=1)
        k_lat, v_lat = _mla_kv(ckv_all.reshape(N_LAT_SEQ * lk, 128), kr_all.reshape(N_LAT_SEQ * lk, 128),
                               wk, wv, N_LAT_SEQ * lk)
        d_out = _mla_attention(z_d, q, k_ctx, v_ctx, k_lat, v_lat)
        new_ckv.append(ckv_n[:N_CTX].reshape(N_CTX_SEQ, CTX_LEN, MLA_KV_RANK))
        new_kr.append(kr[:N_CTX, MLA_NOPE:MLA_NOPE + MLA_ROPE].reshape(N_CTX_SEQ, CTX_LEN, MLA_ROPE))

        xs = _out_proj(a_out, b_out, c_out, d_out, xs, mod, w_out_bf, l,
                       final_norm.reshape(1, D_MODEL), final=(l == DEPTH - 1))
        xs = tuple(xs) if l == DEPTH - 1 else (xs,)
    y_prompt = xs[0].reshape(N_CTX_SEQ, CTX_LEN, D_MODEL)
    y_sample = xs[1].reshape(N_LAT_SEQ, LAT_LEN, D_MODEL)
    st = lambda parts: jnp.stack(parts, axis=1)
    return (y_prompt, y_sample, st(new_k), st(new_v), st(new_s5), st(new_hg), st(new_ckv), st(new_kr))
```

```python
import functools
import math

import numpy as np

import jax
import jax.numpy as jnp
from jax import lax
from jax.experimental import pallas as pl
from jax.experimental.pallas import tpu as pltpu

F32 = jnp.float32
BF16 = jnp.bfloat16

D_MODEL = 1024
DEPTH = 2
N_CTX_SEQ = 16
CTX_LEN = 256
N_LAT_SEQ = 2
LAT_LEN = 2048
PAST_LEN = 256
GRID_W = 64
N_CTX = N_CTX_SEQ * CTX_LEN
N_LAT = N_LAT_SEQ * LAT_LEN
N_TOK = N_CTX + N_LAT
BRANCH = 256
EPS = 1e-6
ROPE_BASE = 10000.0
ROW_TILE = 256
LAT_TILES = LAT_LEN // ROW_TILE
N_TILES = N_TOK // ROW_TILE
CTX_TILES = N_CTX // ROW_TILE
VMEM_LIMIT = 48 * 1024 * 1024

DA_HEADS = 4
DA_QK = 32
MLA_HEADS = 4
MLA_NOPE = 64
MLA_ROPE = 32
MLA_Q_RANK = 192
MLA_KV_RANK = 128
S5_GROUPS = 16
S5_CH = 16
S5_STATE = 64
S5_CHUNK = 16
HG_HEADS = 4
HG_DK = 64

W_A = 1536
W_B = 512
W_C = 1536
W_D = 896
W_ABC = W_A + W_B + W_C


def _cparams(*sem):
    return pltpu.CompilerParams(dimension_semantics=sem, vmem_limit_bytes=VMEM_LIMIT)


def _tile_seq(i):
    return jnp.where(i < CTX_TILES, 0, 1 + (i - CTX_TILES) // LAT_TILES)


def _silu(x):
    return x * (1.0 / (1.0 + jnp.exp(-x)))


def _dot(a, b):
    return jnp.dot(a, b, preferred_element_type=F32)


def _dot_nt(a, b):
    return lax.dot_general(a, b, (((1,), (1,)), ((), ())), preferred_element_type=F32)


def _mod_body(c_ref, w_ref, b_ref, o_ref):
    c = _silu(c_ref[...]).astype(BF16)
    o_ref[0] = _dot(c, w_ref[0].astype(BF16)) + b_ref[0]


def _modulation(c_rows, w_mod, b_mod):
    tn = 768
    return pl.pallas_call(
        _mod_body,
        grid=(DEPTH, 3 * D_MODEL // tn),
        in_specs=[pl.BlockSpec((8, D_MODEL), lambda l, j: (0, 0)),
                  pl.BlockSpec((1, D_MODEL, tn), lambda l, j: (l, 0, j)),
                  pl.BlockSpec((1, 1, tn), lambda l, j: (l, 0, j))],
        out_specs=pl.BlockSpec((1, 8, tn), lambda l, j: (l, 0, j)),
        out_shape=jax.ShapeDtypeStruct((DEPTH, 8, 3 * D_MODEL), F32),
        compiler_params=_cparams("parallel", "parallel"),
        name="modulation",
    )(c_rows, w_mod, b_mod.reshape(DEPTH, 1, 3 * D_MODEL))


def _split_rows(i, ctx_ref, lat_ref):
    return jnp.where(i < CTX_TILES, ctx_ref[...], lat_ref[...])


def _ctx_tile_spec(w):
    return pl.BlockSpec((ROW_TILE, w), lambda i: (jnp.minimum(i, CTX_TILES - 1), 0))


def _lat_tile_spec(w):
    return pl.BlockSpec((ROW_TILE, w), lambda i: (jnp.maximum(i - CTX_TILES, 0), 0))


def _in_proj_body(*refs, split):
    if split:
        xc_ref, xl_ref, mod_ref, w_ref, wd_ref, oa, ob, oc, od = refs
        x = _split_rows(pl.program_id(0), xc_ref, xl_ref)
    else:
        x_ref, mod_ref, w_ref, wd_ref, oa, ob, oc, od = refs
        x = x_ref[...]
    xn = x * lax.rsqrt(jnp.mean(x * x, axis=-1, keepdims=True) + EPS)
    mod = mod_ref[0]
    h = (xn * (1.0 + mod[1:2]) + mod[0:1]).astype(BF16)
    off = 0
    for o in (oa, ob, oc):
        w = o.shape[-1]
        o[...] = _dot(h, w_ref[0, :, off:off + w])
        off += w
    od[...] = _dot(h, wd_ref[0])


def _in_proj(xs, mod, w_abc, w_d, l):
    widths = (W_A, W_B, W_C, W_D)
    split = len(xs) == 2
    x_specs = ([_ctx_tile_spec(D_MODEL), _lat_tile_spec(D_MODEL)] if split
               else [pl.BlockSpec((ROW_TILE, D_MODEL), lambda i: (i, 0))])
    return pl.pallas_call(
        functools.partial(_in_proj_body, split=split),
        grid=(N_TILES,),
        in_specs=x_specs + [pl.BlockSpec((1, 3, D_MODEL), lambda i: (_tile_seq(i), 0, 0)),
                            pl.BlockSpec((1, D_MODEL, W_ABC), lambda i: (l, 0, 0)),
                            pl.BlockSpec((1, D_MODEL, W_D), lambda i: (l, 0, 0))],
        out_specs=[pl.BlockSpec((ROW_TILE, w), lambda i: (i, 0)) for w in widths],
        out_shape=[jax.ShapeDtypeStruct((N_TOK, w), F32) for w in widths],
        compiler_params=_cparams("parallel"),
        name="in_proj",
    )(*xs, mod, w_abc, w_d)


def _arrange_body(w_ref, prot_ref, pdup_ref, pint_ref, o_ref):
    w = w_ref[0].astype(BF16)

    def perm(x, p_ref):
        return _dot(x, p_ref[...]).astype(BF16)
    o_ref[0] = jnp.concatenate(
        [w[:, 0:1024], perm(w[:, 0:256], prot_ref), perm(w[:, 256:512], prot_ref), w[:, 1024:1536],
         perm(w[:, 1536:1792], pdup_ref), perm(w[:, 1792:2304], pint_ref), w[:, 2304:2816]], axis=-1)


def _arrange_w_in(w_in, prot, pdup, pint):
    rows = 128
    const = lambda a: pl.BlockSpec(a.shape, lambda l, i: (0, 0))
    return pl.pallas_call(
        _arrange_body,
        grid=(DEPTH, D_MODEL // rows),
        in_specs=[pl.BlockSpec((1, rows, w_in.shape[-1]), lambda l, i: (l, i, 0)), const(prot), const(pdup), const(pint)],
        out_specs=pl.BlockSpec((1, rows, W_ABC), lambda l, i: (l, i, 0)),
        out_shape=jax.ShapeDtypeStruct((DEPTH, D_MODEL, W_ABC), BF16),
        compiler_params=_cparams("parallel", "parallel"),
        name="arrange_w_in",
    )(w_in, prot, pdup, pint)


def _out_proj_body(*refs, split_in, final):
    ac_ref, al_ref, b_ref, c_ref, dc_ref, dl_ref = refs[:6]
    i = pl.program_id(0)
    if split_in:
        xc_ref, xl_ref, mod_ref, w_ref, fn_ref = refs[6:11]
        x = _split_rows(i, xc_ref, xl_ref)
    else:
        x_ref, mod_ref, w_ref, fn_ref = refs[6:10]
        x = x_ref[...]
    branches = (_split_rows(i, ac_ref, al_ref), b_ref[...], c_ref[...], _split_rows(i, dc_ref, dl_ref))
    acc = None
    for j, r in enumerate(branches):
        t = _dot(r.astype(BF16), w_ref[0, j * BRANCH:(j + 1) * BRANCH, :])
        acc = t if acc is None else acc + t
    x = x + mod_ref[0][2:3] * acc
    if not final:
        refs[-1][...] = x
        return
    y = x * lax.rsqrt(jnp.mean(x * x, axis=-1, keepdims=True) + EPS) * fn_ref[...]
    yc_ref, yl_ref = refs[-2:]

    @pl.when(i < CTX_TILES)
    def _():
        yc_ref[...] = y

    @pl.when(i >= CTX_TILES)
    def _():
        yl_ref[...] = y


def _out_proj(a, b, c, d, xs, mod, w_out, l, final_norm, final):
    br = pl.BlockSpec((ROW_TILE, BRANCH), lambda i: (i, 0))
    pair = [_ctx_tile_spec(BRANCH), _lat_tile_spec(BRANCH)]
    split_in = len(xs) == 2
    x_specs = ([_ctx_tile_spec(D_MODEL), _lat_tile_spec(D_MODEL)] if split_in
               else [pl.BlockSpec((ROW_TILE, D_MODEL), lambda i: (i, 0))])
    if final:
        out_specs = [_ctx_tile_spec(D_MODEL), _lat_tile_spec(D_MODEL)]
        out_shape = [jax.ShapeDtypeStruct((N_CTX, D_MODEL), F32), jax.ShapeDtypeStruct((N_LAT, D_MODEL), F32)]
    else:
        out_specs = pl.BlockSpec((ROW_TILE, D_MODEL), lambda i: (i, 0))
        out_shape = jax.ShapeDtypeStruct((N_TOK, D_MODEL), F32)
    return pl.pallas_call(
        functools.partial(_out_proj_body, split_in=split_in, final=final),
        grid=(N_TILES,),
        in_specs=pair + [br, br] + pair + x_specs + [
            pl.BlockSpec((1, 3, D_MODEL), lambda i: (_tile_seq(i), 0, 0)),
            pl.BlockSpec((1, D_MODEL, D_MODEL), lambda i: (l, 0, 0)),
            pl.BlockSpec((1, D_MODEL), lambda i: (0, 0))],
        out_specs=out_specs,
        out_shape=out_shape,
        compiler_params=_cparams("arbitrary"),
        name="out_proj",
    )(*a, b, c, *d, *xs, mod, w_out, final_norm)


def _softmax_rows(s):
    e = jnp.exp(s - jnp.max(s, axis=-1, keepdims=True))
    return e * (1.0 / jnp.sum(e, axis=-1, keepdims=True))


def _da_kv_body(k_ref, kr_ref, v_ref, cos_ref, sin_ref, ck_ref, cv_ref, ko_ref, vo_ref):
    j = pl.program_id(1)

    @pl.when(j < LAT_TILES)
    def _():
        ko_ref[0] = (k_ref[...] * cos_ref[...] + kr_ref[...] * sin_ref[...]).astype(BF16)
        vo_ref[0] = v_ref[...].astype(BF16)

    @pl.when(j == LAT_TILES)
    def _():
        ko_ref[0] = ck_ref[0].astype(BF16)
        vo_ref[0] = cv_ref[0].astype(BF16)


def _da_latent_kv(z_a, cos, sin, cache_k, cache_v):
    def rows(col):
        return pl.BlockSpec(
            (ROW_TILE, BRANCH),
            lambda b, j: (CTX_TILES + b * LAT_TILES + jnp.minimum(j, LAT_TILES - 1), col))
    tab = pl.BlockSpec((ROW_TILE, BRANCH), lambda b, j: (jnp.minimum(j, LAT_TILES - 1), 0))
    cache = pl.BlockSpec((1, PAST_LEN, BRANCH), lambda b, j: (b, 0, 0))
    out = pl.BlockSpec((1, ROW_TILE, BRANCH), lambda b, j: (b, j, 0))
    shp = jax.ShapeDtypeStruct((N_LAT_SEQ, LAT_LEN + PAST_LEN, BRANCH), BF16)
    return pl.pallas_call(
        _da_kv_body,
        grid=(N_LAT_SEQ, LAT_TILES + 1),
        in_specs=[rows(1), rows(5), rows(2), tab, tab, cache, cache],
        out_specs=[out, out],
        out_shape=[shp, shp],
        compiler_params=_cparams("parallel", "parallel"),
        name="da_kv",
    )(z_a, z_a, z_a, cos, sin, cache_k, cache_v)


def _da_attn_body(lam_ref, ng_ref, q_ref, *rest, rope, lam_init):
    if rope:
        qr_ref, cos_ref, sin_ref, k_ref, v_ref, g_ref, o_ref = rest
        q = q_ref[...] * cos_ref[...] + qr_ref[...] * sin_ref[...]
        k = k_ref[0]
        v = v_ref[0]
    else:
        k_ref, v_ref, g_ref, o_ref = rest
        q = q_ref[...]
        k = k_ref[...].astype(BF16)
        v = v_ref[...].astype(BF16)
    q = q * (DA_QK ** -0.5)
    lv = lam_ref[...]
    lam = (jnp.exp(jnp.sum(lv[0:1] * lv[1:2], axis=-1, keepdims=True))
           - jnp.exp(jnp.sum(lv[2:3] * lv[3:4], axis=-1, keepdims=True)) + lam_init)
    lane = lax.broadcasted_iota(jnp.int32, (1, BRANCH), 1)
    acc = jnp.zeros(q.shape, F32)
    for h in range(DA_HEADS):
        q1 = jnp.where(lane // DA_QK == 2 * h, q, 0.0).astype(BF16)
        q2 = jnp.where(lane // DA_QK == 2 * h + 1, q, 0.0).astype(BF16)
        p1 = _softmax_rows(_dot_nt(q1, k))
        p2 = _softmax_rows(_dot_nt(q2, k))
        a = (p1 - lam * p2).astype(BF16)
        acc = jnp.where(lane // (2 * DA_QK) == h, _dot(a, v), acc)
    sq = acc * acc
    ms = jnp.zeros(q.shape, F32)
    for h in range(DA_HEADS):
        hm = lane // (2 * DA_QK) == h
        ms = jnp.where(hm, jnp.sum(jnp.where(hm, sq, 0.0), axis=-1, keepdims=True), ms)
    o = acc * lax.rsqrt(ms * (1.0 / (2 * DA_QK)) + EPS) * (ng_ref[...] * (1.0 - lam_init))
    o_ref[...] = o * _silu(g_ref[...])


def _da_attention(z_a, lam_vec, norm_g, lam_init, cos, sin, kv_lat):
    ng = jnp.tile(norm_g.reshape(1, 2 * DA_QK), (1, DA_HEADS))
    small = [pl.BlockSpec((4, DA_QK), lambda *_: (0, 0)), pl.BlockSpec((1, BRANCH), lambda *_: (0, 0))]

    def col(c):
        return pl.BlockSpec((ROW_TILE, BRANCH), lambda i: (i, c))
    ctx = pl.pallas_call(
        functools.partial(_da_attn_body, rope=False, lam_init=lam_init),
        grid=(CTX_TILES,),
        in_specs=small + [col(0), col(1), col(2), col(3)],
        out_specs=pl.BlockSpec((ROW_TILE, BRANCH), lambda i: (i, 0)),
        out_shape=jax.ShapeDtypeStruct((N_CTX, BRANCH), F32),
        compiler_params=_cparams("parallel"),
        name="da_attn_ctx",
    )(lam_vec, ng, z_a, z_a, z_a, z_a)

    def lcol(c):
        return pl.BlockSpec((ROW_TILE, BRANCH), lambda b, j: (CTX_TILES + b * LAT_TILES + j, c))
    tab = pl.BlockSpec((ROW_TILE, BRANCH), lambda b, j: (j, 0))
    kvs = pl.BlockSpec((1, LAT_LEN + PAST_LEN, BRANCH), lambda b, j: (b, 0, 0))
    lat = pl.pallas_call(
        functools.partial(_da_attn_body, rope=True, lam_init=lam_init),
        grid=(N_LAT_SEQ, LAT_TILES),
        in_specs=small + [lcol(0), lcol(4), tab, tab, kvs, kvs, lcol(3)],
        out_specs=pl.BlockSpec((ROW_TILE, BRANCH), lambda b, j: (b * LAT_TILES + j, 0)),
        out_shape=jax.ShapeDtypeStruct((N_LAT, BRANCH), F32),
        compiler_params=_cparams("parallel", "parallel"),
        name="da_attn_lat",
    )(lam_vec, ng, z_a, z_a, cos, sin, kv_lat[0], kv_lat[1], z_a)
    return ctx, lat


MLA_HEAD_PAD = 128
MLA_QW = MLA_HEADS * MLA_HEAD_PAD


def _mla_prep_body(cq_ref, ckv_ref, kr_ref, krr_ref, cq_t, sq_t, ck_t, sk_t, qn_ref, kvn_ref, wq_ref, wqr_ref,
                   q_out, ckv_out, kr_out):
    cq = cq_ref[...]
    ms = jnp.sum(cq * cq, axis=-1, keepdims=True) * (1.0 / MLA_Q_RANK)
    qn = (cq * lax.rsqrt(ms + EPS) * qn_ref[...]).astype(BF16)
    q = _dot(qn, wq_ref[...]) * cq_t[...] + _dot(qn, wqr_ref[...]) * sq_t[...]
    q_out[...] = (q * ((MLA_NOPE + MLA_ROPE) ** -0.5)).astype(BF16)
    ckv = ckv_ref[...]
    ckv_out[...] = ckv * lax.rsqrt(jnp.mean(ckv * ckv, axis=-1, keepdims=True) + EPS) * kvn_ref[...]
    kr_out[...] = kr_ref[...] * ck_t[...] + krr_ref[...] * sk_t[...]


def _mla_prep(z_d, tabs, q_norm_pad, kv_norm, wq, wqr):
    def tab(w):
        return pl.BlockSpec(
            (ROW_TILE, w), lambda i: (jnp.where(i < CTX_TILES, LAT_TILES, (i - CTX_TILES) % LAT_TILES), 0))

    def col(w, c):
        return pl.BlockSpec((ROW_TILE, w), lambda i: (i, c))

    def const(shape):
        return pl.BlockSpec(shape, lambda i: (0, 0))
    return pl.pallas_call(
        _mla_prep_body,
        grid=(N_TILES,),
        in_specs=[col(256, 0), col(128, 2), col(128, 3), col(128, 6),
                  tab(MLA_QW), tab(MLA_QW), tab(128), tab(128),
                  const((1, 256)), const((1, 128)), const((256, MLA_QW)), const((256, MLA_QW))],
        out_specs=[col(MLA_QW, 0), col(128, 0), col(128, 0)],
        out_shape=[jax.ShapeDtypeStruct((N_TOK, MLA_QW), BF16),
                   jax.ShapeDtypeStruct((N_TOK, 128), F32),
                   jax.ShapeDtypeStruct((N_TOK, 128), F32)],
        compiler_params=_cparams("parallel"),
        name="mla_prep",
    )(z_d, z_d, z_d, z_d, *tabs, q_norm_pad, kv_norm, wq, wqr)


def _mla_kv_body(ckv_ref, kr_ref, wk_ref, wv_ref, k_out, v_out):
    c = ckv_ref[...].astype(BF16)
    kr = kr_ref[...]
    k_out[...] = (_dot(c, wk_ref[...]) + jnp.concatenate([kr] * MLA_HEADS, axis=-1)).astype(BF16)
    v_out[...] = _dot(c, wv_ref[...]).astype(BF16)


def _mla_kv(ckv, kr, wk, wv, n_rows):
    return pl.pallas_call(
        _mla_kv_body,
        grid=(n_rows // ROW_TILE,),
        in_specs=[pl.BlockSpec((ROW_TILE, 128), lambda i: (i, 0)),
                  pl.BlockSpec((ROW_TILE, 128), lambda i: (i, 0)),
                  pl.BlockSpec((128, MLA_QW), lambda i: (0, 0)),
                  pl.BlockSpec((128, BRANCH), lambda i: (0, 0))],
        out_specs=[pl.BlockSpec((ROW_TILE, MLA_QW), lambda i: (i, 0)),
                   pl.BlockSpec((ROW_TILE, BRANCH), lambda i: (i, 0))],
        out_shape=[jax.ShapeDtypeStruct((n_rows, MLA_QW), BF16),
                   jax.ShapeDtypeStruct((n_rows, BRANCH), BF16)],
        compiler_params=_cparams("parallel"),
        name="mla_kv",
    )(ckv, kr, wk, wv)


def _mla_attn_body(q_ref, k_ref, v_ref, g_ref, o_ref):
    q = q_ref[...]
    k = k_ref[...].reshape(-1, MLA_QW)
    v = v_ref[...].reshape(-1, BRANCH)
    lane = lax.broadcasted_iota(jnp.int32, (1, BRANCH), 1)
    acc = jnp.zeros((q.shape[0], BRANCH), F32)
    for h in range(MLA_HEADS):
        sl = slice(h * MLA_HEAD_PAD, (h + 1) * MLA_HEAD_PAD)
        p = _softmax_rows(_dot_nt(q[:, sl], k[:, sl])).astype(BF16)
        acc = jnp.where(lane // 64 == h, _dot(p, v), acc)
    o_ref[...] = acc * _silu(g_ref[...])


def _mla_attention(z_d, q, k_ctx, v_ctx, k_lat, v_lat):
    ctx = pl.pallas_call(
        _mla_attn_body,
        grid=(CTX_TILES,),
        in_specs=[pl.BlockSpec((ROW_TILE, MLA_QW), lambda i: (i, 0)),
                  pl.BlockSpec((ROW_TILE, MLA_QW), lambda i: (i, 0)),
                  pl.BlockSpec((ROW_TILE, BRANCH), lambda i: (i, 0)),
                  pl.BlockSpec((ROW_TILE, BRANCH), lambda i: (i, 2))],
        out_specs=pl.BlockSpec((ROW_TILE, BRANCH), lambda i: (i, 0)),
        out_shape=jax.ShapeDtypeStruct((N_CTX, BRANCH), F32),
        compiler_params=_cparams("parallel"),
        name="mla_attn_ctx",
    )(q, k_ctx, v_ctx, z_d)
    lk = LAT_LEN + PAST_LEN
    lat = pl.pallas_call(
        _mla_attn_body,
        grid=(N_LAT_SEQ, LAT_TILES),
        in_specs=[pl.BlockSpec((ROW_TILE, MLA_QW), lambda b, j: (CTX_TILES + b * LAT_TILES + j, 0)),
                  pl.BlockSpec((1, lk, MLA_QW), lambda b, j: (b, 0, 0)),
                  pl.BlockSpec((1, lk, BRANCH), lambda b, j: (b, 0, 0)),
                  pl.BlockSpec((ROW_TILE, BRANCH), lambda b, j: (CTX_TILES + b * LAT_TILES + j, 2))],
        out_specs=pl.BlockSpec((ROW_TILE, BRANCH), lambda b, j: (b * LAT_TILES + j, 0)),
        out_shape=jax.ShapeDtypeStruct((N_LAT, BRANCH), F32),
        compiler_params=_cparams("parallel", "parallel"),
        name="mla_attn_lat",
    )(q, k_lat.reshape(N_LAT_SEQ, lk, MLA_QW), v_lat.reshape(N_LAT_SEQ, lk, BRANCH), z_d)
    return ctx, lat


S5_TAP = S5_CHUNK * S5_CH
S5_NCHUNK = N_TOK // S5_CHUNK
S5_CTX_CH = N_CTX // S5_CHUNK
S5_CTX_SEQ_CH = CTX_LEN // S5_CHUNK
S5_LAT_SEQ_CH = LAT_LEN // S5_CHUNK
S5_SCAN_STEPS = S5_LAT_SEQ_CH.bit_length() - 1


def _s5_body(x_ref, mt_ref, bst_ref, cot_ref, a_ref, h0_ref, y_ref, fin_ref):
    x = x_ref[...].reshape(S5_TAP, S5_NCHUNK)
    y = _dot(mt_ref[0, 0], x)
    s = _dot(bst_ref[0, 0], x)
    lane = lax.broadcasted_iota(jnp.int32, (1, S5_NCHUNK), 1)
    is_lat = lane >= S5_CTX_CH
    pos_f = jnp.where(is_lat, (lane - S5_CTX_CH) & (S5_LAT_SEQ_CH - 1), lane & (S5_CTX_SEQ_CH - 1))
    pos_b = jnp.where(is_lat, S5_LAT_SEQ_CH - 1, S5_CTX_SEQ_CH - 1) - pos_f
    hin = []
    for d in range(2):
        n = S5_STATE
        sre, sim = s[2 * d * n:(2 * d + 1) * n], s[(2 * d + 1) * n:(2 * d + 2) * n]
        are = jnp.concatenate([a_ref[0, 0, 2 * d]] * (S5_NCHUNK // 128), axis=-1)
        aim = jnp.concatenate([a_ref[0, 0, 2 * d + 1]] * (S5_NCHUNK // 128), axis=-1)
        pos = pos_f if d == 0 else pos_b
        h0r, h0i = jnp.zeros_like(sre), jnp.zeros_like(sre)
        for b in range(N_LAT_SEQ):
            first = S5_CTX_CH + b * S5_LAT_SEQ_CH + (0 if d == 0 else S5_LAT_SEQ_CH - 1)
            h0r = jnp.where(lane == first, h0_ref[0, 2 * d][:, b:b + 1], h0r)
            h0i = jnp.where(lane == first, h0_ref[0, 2 * d + 1][:, b:b + 1], h0i)
        xr = sre + are * h0r - aim * h0i
        xi = sim + are * h0i + aim * h0r
        pr, pi = are, aim
        for j in range(S5_SCAN_STEPS):
            sh = 1 << j
            shift = sh if d == 0 else S5_NCHUNK - sh
            rr, ri = pltpu.roll(xr, shift, 1), pltpu.roll(xi, shift, 1)
            ok = pos >= sh
            xr, xi = (xr + jnp.where(ok, pr * rr - pi * ri, 0.0), xi + jnp.where(ok, pr * ri + pi * rr, 0.0))
            pr, pi = pr * pr - pi * pi, 2.0 * pr * pi
        fin_ref[0, 2 * d] = xr[:, :S5_CTX_CH]
        fin_ref[0, 2 * d + 1] = xi[:, :S5_CTX_CH]
        one = 1 if d == 0 else S5_NCHUNK - 1
        hin.append(jnp.where(pos >= 1, pltpu.roll(xr, one, 1), h0r))
        hin.append(jnp.where(pos >= 1, pltpu.roll(xi, one, 1), h0i))
    y = y + _dot(cot_ref[0, 0], jnp.concatenate(hin, axis=0).astype(BF16))
    y_ref[...] = y.reshape(S5_CHUNK, S5_CH, S5_NCHUNK)


def _s5_scan(x_all, mt, bst, cot, a16, h0, l):
    g = S5_GROUPS
    sq = pl.BlockSpec((1, 1, S5_TAP, S5_TAP), lambda i: (l, i, 0, 0))
    st = pl.BlockSpec((1, 4, S5_STATE, 128), lambda i: (i, 0, 0, 0))
    return pl.pallas_call(
        _s5_body,
        grid=(g,),
        in_specs=[pl.BlockSpec((S5_CHUNK, S5_CH, S5_NCHUNK), lambda i: (0, i, 0)), sq, sq, sq,
                  pl.BlockSpec((1, 1, 4, S5_STATE, 128), lambda i: (l, i, 0, 0, 0)), st],
        out_specs=[pl.BlockSpec((S5_CHUNK, S5_CH, S5_NCHUNK), lambda i: (0, i, 0)),
                   pl.BlockSpec((1, 4, S5_STATE, S5_CTX_CH), lambda i: (i, 0, 0, 0))],
        out_shape=[jax.ShapeDtypeStruct((S5_CHUNK, BRANCH, S5_NCHUNK), F32),
                   jax.ShapeDtypeStruct((g, 4, S5_STATE, S5_CTX_CH), F32)],
        compiler_params=_cparams("parallel"),
        name="s5_scan",
    )(x_all, mt, bst, cot, a16, h0)


def _s5_out_body(y_ref, u_ref, g_ref, d_ref, w_ref, o_ref):
    y = u_ref[...] * d_ref[...] + y_ref[...]
    ge = 0.5 * y * (1.0 + jnp.tanh(0.7978845608028654 * (y + 0.044715 * (y * y * y))))
    gl = _dot(ge.astype(BF16), w_ref[...])
    o_ref[...] = gl[:, :BRANCH] * (1.0 / (1.0 + jnp.exp(-gl[:, BRANCH:]))) * _silu(g_ref[...])


def _s5_out(y_ssm, z_b, d_skip, w_glu):
    def col(c):
        return pl.BlockSpec((ROW_TILE, BRANCH), lambda i: (i, c))
    return pl.pallas_call(
        _s5_out_body,
        grid=(N_TILES,),
        in_specs=[col(0), col(0), col(1),
                  pl.BlockSpec((1, BRANCH), lambda i: (0, 0)),
                  pl.BlockSpec((BRANCH, 2 * BRANCH), lambda i: (0, 0))],
        out_specs=col(0),
        out_shape=jax.ShapeDtypeStruct((N_TOK, BRANCH), F32),
        compiler_params=_cparams("parallel"),
        name="s5_out",
    )(y_ssm, z_b, z_b, d_skip, w_glu)


HG_CHUNK = ROW_TILE
HG_W = 2 * HG_HEADS * HG_DK
HG_HEAD_W = 2 * HG_DK
HG_LAT_CHUNKS = LAT_LEN // HG_CHUNK
HG_CHUNKS = N_TOK // HG_CHUNK


def _hg_gates(z, lb):
    e = jnp.exp(-jnp.abs(z))
    r = 1.0 / (1.0 + e)
    sig_pos = jnp.where(z >= 0, r, e * r)
    sig_neg = jnp.where(z >= 0, e * r, r)
    return lb + (1.0 - lb) * sig_pos, (1.0 - lb) * sig_neg


def _bcast_row(x, period, r):
    n, w = x.shape
    if period >= 8:
        x3 = x.reshape(n // period, period, w)
        return jnp.broadcast_to(x3[:, r:r + 1, :], x3.shape).reshape(n, w)
    x3 = x.reshape(n // 8, 8, w)
    sub = lax.broadcasted_iota(jnp.int32, (1, 8, 1), 1)
    out = None
    for j in range(8 // period):
        b = jnp.broadcast_to(x3[:, j * period + r:j * period + r + 1, :], x3.shape)
        out = b if out is None else jnp.where(sub >= j * period, b, out)
    return out.reshape(n, w)


def _hg_scans(f, isb):
    n = f.shape[0]
    row = lax.broadcasted_iota(jnp.int32, (n, 1), 0)
    p, r = f, jnp.ones_like(f)
    levels = []
    h, sh = 1, 0
    while h < n:
        levels.append((h, sh, p, r))
        up = (row >> sh) & 1
        tot_p = jnp.where(isb == 1, _bcast_row(p, 2 * h, h), _bcast_row(p, 2 * h, h - 1))
        tot_r = jnp.where(isb == 1, _bcast_row(p, 2 * h, 0), _bcast_row(p, 2 * h, 2 * h - 1))
        p = p * jnp.where(up != isb, tot_p, 1.0)
        r = r * jnp.where(up == isb, tot_r, 1.0)
        h, sh = 2 * h, sh + 1
    return levels, p, r


def _hg_state_body(zf_ref, zb_ref, vf_ref, vb_ref, lb_ref, s0_ref, sf_out, sb_out, fin_out, s_scr):
    i = pl.program_id(0)
    first = jnp.logical_or(i < N_CTX_SEQ, (i - N_CTX_SEQ) % HG_LAT_CHUNKS == 0)

    @pl.when(first)
    def _():
        s_scr[...] = s0_ref[0]

    sf_out[0] = s_scr[:, 0:HG_DK, :]
    sb_out[0] = s_scr[:, HG_DK:, :]
    lane5 = lax.broadcasted_iota(jnp.int32, (1, HG_W), 1)
    isb = (lane5 >> 6) & 1
    z = jnp.where(isb == 1, zb_ref[...], zf_ref[...])
    f, k = _hg_gates(z, lb_ref[...])
    r, ptot = _hg_chunk_decay(f, isb)
    kt = k * r
    lane = lax.broadcasted_iota(jnp.int32, (1, BRANCH), 1)
    vf = vf_ref[...]
    vb = vb_ref[...]
    for hd in range(HG_HEADS):
        sl = slice(hd * HG_HEAD_W, (hd + 1) * HG_HEAD_W)
        kth = kt[:, sl].T.astype(BF16)
        hm = (lane >> 6) == hd
        d_f = _dot(kth, jnp.where(hm, vf, 0.0).astype(BF16))
        d_b = _dot(kth, jnp.where(hm, vb, 0.0).astype(BF16))
        ds = jnp.concatenate([d_f[:HG_DK], d_b[HG_DK:]], axis=0)
        pcol = jnp.broadcast_to(ptot[:, sl], (HG_HEAD_W, HG_HEAD_W)).T[:, 0:1]
        s_scr[hd] = s_scr[hd] * pcol + ds

    @pl.when(i < N_CTX_SEQ)
    def _():
        for hd in range(HG_HEADS):
            fin_out[0, hd] = s_scr[hd][:, hd * HG_DK:(hd + 1) * HG_DK]


def _hg_chunk_decay(f, isb):
    n = f.shape[0]
    row = lax.broadcasted_iota(jnp.int32, (n, 1), 0)
    dist = jnp.where(isb == 1, row, n - 1 - row)
    x = f
    sh = 1
    while sh < n:
        src = jnp.where(isb == 1, pltpu.roll(x, sh, 0), pltpu.roll(x, n - sh, 0))
        x = x * jnp.where(dist >= sh, src, 1.0)
        sh *= 2
    total = jnp.where(isb == 1, x[n - 1:n], x[0:1])
    nxt = jnp.where(isb == 1, pltpu.roll(x, 1, 0), pltpu.roll(x, n - 1, 0))
    return jnp.where(dist >= 1, nxt, 1.0), total


def _hg_rev(i):
    j = i - N_CTX_SEQ
    return jnp.where(i < N_CTX_SEQ, i, N_CTX_SEQ + (j // HG_LAT_CHUNKS) * HG_LAT_CHUNKS
                     + (HG_LAT_CHUNKS - 1 - j % HG_LAT_CHUNKS))


def _hg_states(z_c, lb, s0):
    zz_f = pl.BlockSpec((HG_CHUNK, HG_W), lambda i: (i, 1))
    zz_b = pl.BlockSpec((HG_CHUNK, HG_W), lambda i: (_hg_rev(i), 1))
    v_f = pl.BlockSpec((HG_CHUNK, BRANCH), lambda i: (i, 4))
    v_b = pl.BlockSpec((HG_CHUNK, BRANCH), lambda i: (_hg_rev(i), 4))
    st = (HG_HEADS, HG_HEAD_W, BRANCH)
    half = (HG_HEADS, HG_DK, BRANCH)
    fin = (HG_HEADS, HG_HEAD_W, HG_DK)
    return pl.pallas_call(
        _hg_state_body,
        grid=(HG_CHUNKS,),
        in_specs=[zz_f, zz_b, v_f, v_b,
                  pl.BlockSpec((1, HG_W), lambda i: (0, 0)),
                  pl.BlockSpec((1,) + st, lambda i: (
                      jnp.where(i < N_CTX_SEQ, 0, 1 + (i - N_CTX_SEQ) // HG_LAT_CHUNKS), 0, 0, 0))],
        out_specs=[pl.BlockSpec((1,) + half, lambda i: (i, 0, 0, 0)),
                   pl.BlockSpec((1,) + half, lambda i: (_hg_rev(i), 0, 0, 0)),
                   pl.BlockSpec((1,) + fin, lambda i: (jnp.minimum(i, N_CTX_SEQ - 1), 0, 0, 0))],
        out_shape=[jax.ShapeDtypeStruct((HG_CHUNKS,) + half, F32),
                   jax.ShapeDtypeStruct((HG_CHUNKS,) + half, F32),
                   jax.ShapeDtypeStruct((N_CTX_SEQ,) + fin, F32)],
        scratch_shapes=[pltpu.VMEM(st, F32)],
        compiler_params=_cparams("arbitrary"),
        name="hg_states",
    )(z_c, z_c, z_c, z_c, lb, s0)


def _hg_main_body(qq_ref, zz_ref, v_ref, g_ref, sf_ref, sb_ref, lb_ref, ng_ref, o_ref):
    n = HG_CHUNK
    qq = qq_ref[...]
    lane5 = lax.broadcasted_iota(jnp.int32, (1, HG_W), 1)
    isb = (lane5 >> 6) & 1
    f, k = _hg_gates(zz_ref[...], lb_ref[...])
    levels, pfull, _ = _hg_scans(f, isb)
    row = lax.broadcasted_iota(jnp.int32, (n, 1), 0)
    col = lax.broadcasted_iota(jnp.int32, (1, n), 1)
    ops = [(qq.astype(BF16), k.astype(BF16), row == col)]
    for h, sh, p, r in levels:
        up = (row >> sh) & 1
        qt = jnp.where(up != isb, qq * p, 0.0).astype(BF16)
        kt = jnp.where(up == isb, k * r, 0.0).astype(BF16)
        ops.append((qt, kt, (row >> (sh + 1)) == (col >> (sh + 1))))
    qc = (qq * pfull).astype(BF16)
    vb = v_ref[...].astype(BF16)
    lane = lax.broadcasted_iota(jnp.int32, (1, BRANCH), 1)
    acc = jnp.zeros((n, BRANCH), F32)
    for hd in range(HG_HEADS):
        sl = slice(hd * HG_HEAD_W, (hd + 1) * HG_HEAD_W)
        a = jnp.zeros((n, n), F32)
        for qt, kt, mask in ops:
            a = a + jnp.where(mask, _dot_nt(qt[:, sl], kt[:, sl]), 0.0)
        s_in = jnp.concatenate([sf_ref[0, hd], sb_ref[0, hd]], axis=0).astype(BF16)
        o_h = _dot(a.astype(BF16), vb) + _dot(qc[:, sl], s_in)
        acc = jnp.where((lane >> 6) == hd, o_h, acc)
    sq = acc * acc
    ms = jnp.zeros((n, BRANCH), F32)
    for hd in range(HG_HEADS):
        hm = (lane >> 6) == hd
        ms = jnp.where(hm, jnp.sum(jnp.where(hm, sq, 0.0), axis=-1, keepdims=True), ms)
    o_ref[...] = acc * lax.rsqrt(ms * (1.0 / HG_DK) + EPS) * ng_ref[...] * _silu(g_ref[...])


def _hg_main(z_c, s_f, s_b, lb, norm_g):
    half = (1, HG_HEADS, HG_DK, BRANCH)
    return pl.pallas_call(
        _hg_main_body,
        grid=(HG_CHUNKS,),
        in_specs=[pl.BlockSpec((HG_CHUNK, HG_W), lambda i: (i, 0)),
                  pl.BlockSpec((HG_CHUNK, HG_W), lambda i: (i, 1)),
                  pl.BlockSpec((HG_CHUNK, BRANCH), lambda i: (i, 4)),
                  pl.BlockSpec((HG_CHUNK, BRANCH), lambda i: (i, 5)),
                  pl.BlockSpec(half, lambda i: (i, 0, 0, 0)),
                  pl.BlockSpec(half, lambda i: (i, 0, 0, 0)),
                  pl.BlockSpec((1, HG_W), lambda i: (0, 0)),
                  pl.BlockSpec((1, BRANCH), lambda i: (0, 0))],
        out_specs=pl.BlockSpec((HG_CHUNK, BRANCH), lambda i: (i, 0)),
        out_shape=jax.ShapeDtypeStruct((N_TOK, BRANCH), F32),
        compiler_params=_cparams("parallel"),
        name="hg_main",
    )(z_c, z_c, z_c, z_c, s_f, s_b, lb, norm_g)


def _rot_src(cols):
    j = np.arange(len(cols))
    return cols[(j // 16) * 16 + ((j % 16) + 8) % 16], np.where(j % 16 < 8, -1.0, 1.0)


def _take_cols(w, plan):
    idx = np.concatenate([p[0] for p in plan]).astype(np.int32)
    sign = np.concatenate([np.broadcast_to(p[1], p[0].shape) for p in plan]).astype(np.float32)
    return jnp.take(w, jnp.asarray(idx), axis=-1) * jnp.asarray(sign)


def _zeros(n):
    return (np.zeros(n, np.int64), 0.0)


_IN_OFF = {}
_off = 0
for _name, _n in (("da_q", 256), ("da_k", 256), ("da_v", 256), ("da_g", 256), ("s5_u", 256), ("s5_g", 256),
                  ("hg_q", 256), ("hg_ff", 256), ("hg_fb", 256), ("hg_i", 256), ("hg_g", 256),
                  ("mla_cq", MLA_Q_RANK), ("mla_ckv", MLA_KV_RANK), ("mla_kr", MLA_ROPE), ("mla_g", 256)):
    _IN_OFF[_name] = np.arange(_off, _off + _n)
    _off += _n


def _perm_matrix(plan, first, k):
    idx = np.concatenate([p[0] for p in plan]) - first
    sign = np.concatenate([np.broadcast_to(p[1], p[0].shape) for p in plan])
    m = np.zeros((k, len(idx)), np.float32)
    m[idx, np.arange(len(idx))] = sign
    return jnp.asarray(m, BF16)


def _in_proj_weights(w_in):
    c = _IN_OFF

    def per_head(x, y):
        return (np.concatenate([c[x].reshape(HG_HEADS, HG_DK), c[y].reshape(HG_HEADS, HG_DK)], axis=1).reshape(-1), 1.0)
    prot = _perm_matrix([_rot_src(c["da_q"])], 0, 256)
    pdup = _perm_matrix([per_head("hg_q", "hg_q")], c["hg_q"][0], 256)
    pint = _perm_matrix([per_head("hg_ff", "hg_fb")], c["hg_ff"][0], 512)
    col = lambda name: w_in[..., c[name][0]:c[name][-1] + 1]
    zero = lambda n: jnp.zeros(w_in.shape[:-1] + (n,), w_in.dtype)
    kr = col("mla_kr")
    kr_rot = jnp.concatenate([-kr[..., 8:16], kr[..., 0:8], -kr[..., 24:32], kr[..., 16:24]], axis=-1)
    tail = 128 - MLA_NOPE - MLA_ROPE
    w_d = jnp.concatenate([col("mla_cq"), zero(256 - MLA_Q_RANK), col("mla_ckv"), zero(MLA_NOPE), kr, zero(tail),
                           col("mla_g"), zero(MLA_NOPE), kr_rot, zero(tail)], axis=-1)
    return _arrange_w_in(w_in, prot, pdup, pint), w_d.astype(BF16)


def _rope_tables():
    t = np.arange(LAT_LEN)
    pos = np.stack([t // GRID_W, t % GRID_W], axis=1).astype(np.float32)
    inv_freq = (np.float32(ROPE_BASE) ** (-np.arange(8, dtype=np.float32) / np.float32(8))).astype(np.float32)
    r = np.arange(MLA_ROPE)
    ang = (pos[:, r // 16] * inv_freq[r % 8][None, :]).astype(np.float32)
    cos32, sin32 = jnp.asarray(np.cos(ang.astype(np.float64)), F32), jnp.asarray(np.sin(ang.astype(np.float64)), F32)
    cos_a, sin_a = jnp.tile(cos32, (1, 8)), jnp.tile(sin32, (1, 8))
    ones, zeros = jnp.ones((LAT_LEN, 1), F32), jnp.zeros((LAT_LEN, 1), F32)
    cos_h = jnp.concatenate([jnp.tile(ones, (1, 64)), cos32, jnp.tile(ones, (1, 32))], axis=1)
    sin_h = jnp.concatenate([jnp.tile(zeros, (1, 64)), sin32, jnp.tile(zeros, (1, 32))], axis=1)

    def with_identity(c, s):
        return (jnp.concatenate([c, jnp.ones((ROW_TILE, c.shape[1]), F32)], axis=0),
                jnp.concatenate([s, jnp.zeros((ROW_TILE, s.shape[1]), F32)], axis=0))
    cq, sq = with_identity(jnp.tile(cos_h, (1, MLA_HEADS)), jnp.tile(sin_h, (1, MLA_HEADS)))
    ck, sk = with_identity(cos_h, sin_h)
    return (cos_a, sin_a), (cq, sq, ck, sk)


def _mla_weights(w_uq, w_ukv, q_norm):
    hd = MLA_NOPE + MLA_ROPE
    pad_tail = _zeros(MLA_HEAD_PAD - hd)
    q_plan, qr_plan, k_plan, v_plan = [], [], [], []
    for h in range(MLA_HEADS):
        nope, rope = np.arange(h * hd, h * hd + MLA_NOPE), np.arange(h * hd + MLA_NOPE, (h + 1) * hd)
        q_plan += [(nope, 1.0), (rope, 1.0), pad_tail]
        qr_plan += [_zeros(MLA_NOPE), _rot_src(rope), pad_tail]
        k_plan += [(np.arange(h * 2 * MLA_NOPE, h * 2 * MLA_NOPE + MLA_NOPE), 1.0), _zeros(MLA_HEAD_PAD - MLA_NOPE)]
        v_plan += [(np.arange(h * 2 * MLA_NOPE + MLA_NOPE, (h + 1) * 2 * MLA_NOPE), 1.0)]
    pad_rows = lambda x: jnp.pad(x, ((0, 256 - MLA_Q_RANK), (0, 0))).astype(BF16)
    qn = jnp.pad(q_norm, (0, 256 - MLA_Q_RANK)).reshape(1, 256)
    return (pad_rows(_take_cols(w_uq, q_plan)), pad_rows(_take_cols(w_uq, qr_plan)),
            _take_cols(w_ukv, k_plan).astype(BF16), _take_cols(w_ukv, v_plan).astype(BF16), qn)


def _dot_hi(a, b):
    return jnp.dot(a, b, precision=lax.Precision.HIGHEST, preferred_element_type=F32)


def _s5_table_body(xy_ref, bb_ref, c_ref, ct_ref, mt_ref, bst_ref, cot_ref, a_ref):
    n, t, ch = S5_STATE, S5_CHUNK, S5_CH
    wide = 2 * S5_TAP
    xy = xy_ref[0, 0]
    tau_i = lax.broadcasted_iota(jnp.int32, (1, 128), 1)
    tau = tau_i.astype(F32)
    sel_row = lax.broadcasted_iota(jnp.int32, (128, 1), 0)

    def lag(width):
        return lax.broadcasted_iota(jnp.int32, (1, width), 1) >> 4

    def onehot(cond):
        return jnp.where(cond, 1.0, 0.0).astype(F32)
    j = lag(wide)
    e_z = (onehot((j <= t - 1) & (sel_row == t - 1 - j)), onehot((j >= t - 1) & (j <= 2 * t - 2) & (sel_row == j - (t - 1))))
    jc = lag(S5_TAP)
    e_c = (onehot(sel_row == jc + 1), onehot(sel_row == t - jc))
    ch_row = lax.broadcasted_iota(jnp.int32, (ch, 1), 0)
    tile_w = onehot((lax.broadcasted_iota(jnp.int32, (1, wide), 1) & (ch - 1)) == ch_row)
    tile_n = onehot((lax.broadcasted_iota(jnp.int32, (1, S5_TAP), 1) & (ch - 1)) == ch_row)

    z, cot_rows, klong = [], [], None
    for d in range(2):
        x, y = xy[:, 2 * d:2 * d + 1], xy[:, 2 * d + 1:2 * d + 2]
        mag = jnp.exp(jnp.where(tau_i <= t, tau, 0.0) * x)
        ang = jnp.where(tau_i <= t, tau, 0.0) * y
        p_re = jnp.where(tau_i <= t, mag * jnp.cos(ang), 0.0)
        p_im = jnp.where(tau_i <= t, mag * jnp.sin(ang), 0.0)
        a_ref[0, 0, 2 * d] = jnp.broadcast_to(p_re[:, t:t + 1], (n, 128))
        a_ref[0, 0, 2 * d + 1] = jnp.broadcast_to(p_im[:, t:t + 1], (n, 128))
        pz_re, pz_im = _dot_hi(p_re, e_z[d]), _dot_hi(p_im, e_z[d])
        b_re, b_im = _dot_hi(bb_ref[0, 0, 2 * d], tile_w), _dot_hi(bb_ref[0, 0, 2 * d + 1], tile_w)
        z_re, z_im = pz_re * b_re - pz_im * b_im, pz_re * b_im + pz_im * b_re
        z += [z_re, z_im]
        part = _dot_hi(c_ref[0, 0, 2 * d], z_re) - _dot_hi(c_ref[0, 0, 2 * d + 1], z_im)
        klong = part if klong is None else klong + part
        pc_re, pc_im = _dot_hi(p_re, e_c[d]), _dot_hi(p_im, e_c[d])
        c_re, c_im = _dot_hi(ct_ref[0, 0, 2 * d], tile_n), _dot_hi(ct_ref[0, 0, 2 * d + 1], tile_n)
        cot_rows += [c_re * pc_re - c_im * pc_im, -(c_re * pc_im + c_im * pc_re)]
    for tt in range(t):
        off = (t - 1 - tt) * ch
        win = klong if off == 0 else pltpu.roll(klong, wide - off, 1)
        mt_ref[0, 0, tt * ch:(tt + 1) * ch, :] = win[:, :S5_TAP].astype(BF16)
    back = pltpu.roll(z[2], wide - (t - 1) * ch, 1), pltpu.roll(z[3], wide - (t - 1) * ch, 1)
    for k, rows in enumerate((z[0], z[1], back[0], back[1])):
        bst_ref[0, 0, k * n:(k + 1) * n, :] = rows[:, :S5_TAP].astype(BF16)
    cot_ref[0, 0] = jnp.concatenate(cot_rows, axis=0).T.astype(BF16)


def _s5_tables(a_re, a_im, log_dt, b_re, b_im, c_re, c_im):
    nl, g, n, ch = a_re.shape[0], S5_GROUPS, S5_STATE, S5_CH
    step = jnp.exp(log_dt)[..., None]
    mag = jnp.exp(a_re * step)
    ab_re, ab_im = mag * jnp.cos(a_im * step), mag * jnp.sin(a_im * step)
    den = a_re * a_re + a_im * a_im
    f_re = ((ab_re - 1.0) * a_re + ab_im * a_im) / den
    f_im = (ab_im * a_re - (ab_re - 1.0) * a_im) / den
    bb_re = f_re[..., None] * b_re - f_im[..., None] * b_im
    bb_im = f_re[..., None] * b_im + f_im[..., None] * b_re
    by_group = lambda x: jnp.moveaxis(x, 1, 2)
    pair = lambda re, im: jnp.stack([by_group(re), by_group(im)], axis=3).reshape((nl, g, 4) + re.shape[3:])
    xy = jnp.stack([by_group(a_re * step), by_group(a_im * step)], axis=3).reshape(nl, g, 4, n)
    xy = jnp.pad(jnp.swapaxes(xy, 2, 3), ((0, 0), (0, 0), (0, 0), (0, 4)))
    mat = pl.BlockSpec((1, 1, S5_TAP, S5_TAP), lambda l, i: (l, i, 0, 0))
    return pl.pallas_call(
        _s5_table_body,
        grid=(nl, g),
        in_specs=[pl.BlockSpec((1, 1, n, 8), lambda l, i: (l, i, 0, 0)),
                  pl.BlockSpec((1, 1, 4, n, ch), lambda l, i: (l, i, 0, 0, 0)),
                  pl.BlockSpec((1, 1, 4, ch, n), lambda l, i: (l, i, 0, 0, 0)),
                  pl.BlockSpec((1, 1, 4, n, ch), lambda l, i: (l, i, 0, 0, 0))],
        out_specs=[mat, mat, mat, pl.BlockSpec((1, 1, 4, n, 128), lambda l, i: (l, i, 0, 0, 0))],
        out_shape=[jax.ShapeDtypeStruct((nl, g, S5_TAP, S5_TAP), BF16)] * 3
        + [jax.ShapeDtypeStruct((nl, g, 4, n, 128), F32)],
        compiler_params=_cparams("parallel", "parallel"),
        name="s5_tables",
    )(xy, pair(bb_re, bb_im), pair(c_re, c_im), pair(jnp.swapaxes(c_re, -1, -2), jnp.swapaxes(c_im, -1, -2)))


def _s5_chunk_lanes(u):
    return u.reshape(S5_NCHUNK, S5_CHUNK, BRANCH).transpose(1, 2, 0)


def _s5_token_rows(y):
    return y.transpose(2, 0, 1).reshape(N_TOK, BRANCH)


def kernel(x_prompt, x_sample, cache_diff_k, cache_diff_v, state_s5, state_hgrn, cache_mla_ckv, cache_mla_krope, c, c_ctx, w_mod, b_mod, w_in, w_out, da_lambda, da_norm, s5_a_re, s5_a_im, s5_log_dt, s5_b_re, s5_b_im, s5_c_re, s5_c_im, s5_d, s5_w_glu, hg_lb, hg_norm, mla_q_norm, mla_w_uq, mla_kv_norm, mla_w_ukv, final_norm):
    lb_w = jax.nn.softmax(hg_lb.astype(F32), axis=0)
    lb_all = jnp.cumsum(lb_w, axis=0) - lb_w[0:1]
    c_rows = jnp.concatenate([c_ctx[None], c, jnp.zeros((8 - 1 - N_LAT_SEQ, D_MODEL), F32)], axis=0)
    mods = _modulation(c_rows, w_mod, b_mod)
    (cos_a, sin_a), mla_tabs = _rope_tables()
    xs = (x_prompt.reshape(N_CTX, D_MODEL), x_sample.reshape(N_LAT, D_MODEL))
    new_k, new_v, new_s5, new_hg, new_ckv, new_kr = [], [], [], [], [], []
    s5_tabs = _s5_tables(s5_a_re, s5_a_im, s5_log_dt, s5_b_re, s5_b_im, s5_c_re, s5_c_im)
    w_abc, w_d = _in_proj_weights(w_in)
    w_out_bf = w_out.astype(BF16)
    for l in range(DEPTH):
        mod = mods[l, :3].reshape(3, 3, D_MODEL)
        z_a, z_b, z_c, z_d = _in_proj(xs, mod, w_abc, w_d, l)

        lam_init = 0.8 - 0.6 * math.exp(-0.3 * l)
        kv_lat = _da_latent_kv(z_a, cos_a, sin_a,
                               cache_diff_k[:, l].reshape(N_LAT_SEQ, PAST_LEN, BRANCH),
                               cache_diff_v[:, l].reshape(N_LAT_SEQ, PAST_LEN, BRANCH))
        a_out = _da_attention(z_a, da_lambda[l], da_norm[l], lam_init, cos_a, sin_a, kv_lat)
        new_k.append(z_a[:N_CTX, 256:512].reshape(N_CTX_SEQ, CTX_LEN, DA_HEADS, 2 * DA_QK))
        new_v.append(z_a[:N_CTX, 512:768].reshape(N_CTX_SEQ, CTX_LEN, DA_HEADS, 2 * DA_QK))

        h0 = state_s5[:, l].transpose(2, 1, 4, 3, 0).reshape(S5_GROUPS, 4, S5_STATE, N_LAT_SEQ)
        h0 = jnp.pad(h0, ((0, 0), (0, 0), (0, 0), (0, 128 - N_LAT_SEQ)))
        y_all, fin = _s5_scan(_s5_chunk_lanes(z_b[:, :BRANCH]).astype(BF16), *s5_tabs, h0, l)
        b_out = _s5_out(_s5_token_rows(y_all), z_b, s5_d[l].reshape(1, BRANCH), s5_w_glu[l].astype(BF16))
        fin = jnp.stack([fin[:, 0:2, :, S5_CTX_SEQ_CH - 1::S5_CTX_SEQ_CH], fin[:, 2:4, :, 0::S5_CTX_SEQ_CH]], axis=1)
        new_s5.append(fin.transpose(4, 1, 0, 3, 2))

        lb = jnp.concatenate([lb_all[l, 0].reshape(HG_HEADS, HG_DK), lb_all[l, 1].reshape(HG_HEADS, HG_DK)],
                             axis=-1).reshape(1, HG_W)
        head_eye = jnp.eye(HG_HEADS, dtype=F32)
        s0 = state_hgrn[:, l].transpose(0, 2, 1, 3, 4).reshape(N_LAT_SEQ, HG_HEADS, HG_HEAD_W, 1, HG_DK)
        s0 = jnp.pad(s0 * head_eye[None, :, None, :, None], ((1, 0), (0, 0), (0, 0), (0, 0), (0, 0)))
        s_f, s_b, s_fin = _hg_states(z_c, lb, s0.reshape(1 + N_LAT_SEQ, HG_HEADS, HG_HEAD_W, BRANCH))
        c_out = _hg_main(z_c, s_f, s_b, lb, jnp.tile(hg_norm[l].reshape(1, HG_DK), (1, HG_HEADS)))
        new_hg.append(s_fin.reshape(N_CTX_SEQ, HG_HEADS, 2, HG_DK, HG_DK).transpose(0, 2, 1, 3, 4))

        wq, wqr, wk, wv, qn = _mla_weights(mla_w_uq[l], mla_w_ukv[l], mla_q_norm[l])
        q, ckv_n, kr = _mla_prep(z_d, mla_tabs, qn, mla_kv_norm[l].reshape(1, MLA_KV_RANK), wq, wqr)
        k_ctx, v_ctx = _mla_kv(ckv_n, kr, wk, wv, N_CTX)
        kr_cache = jnp.pad(cache_mla_krope[:, l], ((0, 0), (0, 0), (MLA_NOPE, 128 - MLA_NOPE - MLA_ROPE)))
        lk = LAT_LEN + PAST_LEN
        ckv_all = jnp.concatenate([ckv_n[N_CTX:].reshape(N_LAT_SEQ, LAT_LEN, 128), cache_mla_ckv[:, l]], axis=1)
        kr_all = jnp.concatenate([kr[N_CTX:].reshape(N_LAT_SEQ, LAT_LEN, 128), kr_cache], axis=1)
        k_lat, v_lat = _mla_kv(ckv_all.reshape(N_LAT_SEQ * lk, 128), kr_all.reshape(N_LAT_SEQ * lk, 128),
                               wk, wv, N_LAT_SEQ * lk)
        d_out = _mla_attention(z_d, q, k_ctx, v_ctx, k_lat, v_lat)
        new_ckv.append(ckv_n[:N_CTX].reshape(N_CTX_SEQ, CTX_LEN, MLA_KV_RANK))
        new_kr.append(kr[:N_CTX, MLA_NOPE:MLA_NOPE + MLA_ROPE].reshape(N_CTX_SEQ, CTX_LEN, MLA_ROPE))

        xs = _out_proj(a_out, b_out, c_out, d_out, xs, mod, w_out_bf, l,
                       final_norm.reshape(1, D_MODEL), final=(l == DEPTH - 1))
        xs = tuple(xs) if l == DEPTH - 1 else (xs,)
    y_prompt = xs[0].reshape(N_CTX_SEQ, CTX_LEN, D_MODEL)
    y_sample = xs[1].reshape(N_LAT_SEQ, LAT_LEN, D_MODEL)
    st = lambda parts: jnp.stack(parts, axis=1)
    return (y_prompt, y_sample, st(new_k), st(new_v), st(new_s5), st(new_hg), st(new_ckv), st(new_kr))
```

```python
import functools
import math

import numpy as np

import jax
import jax.numpy as jnp
from jax import lax
from jax.experimental import pallas as pl
from jax.experimental.pallas import tpu as pltpu

F32 = jnp.float32
BF16 = jnp.bfloat16

D_MODEL = 1024
DEPTH = 2
N_CTX_SEQ = 16
CTX_LEN = 256
N_LAT_SEQ = 2
LAT_LEN = 2048
PAST_LEN = 256
GRID_W = 64
N_CTX = N_CTX_SEQ * CTX_LEN
N_LAT = N_LAT_SEQ * LAT_LEN
N_TOK = N_CTX + N_LAT
BRANCH = 256
EPS = 1e-6
ROPE_BASE = 10000.0
ROW_TILE = 256
LAT_TILES = LAT_LEN // ROW_TILE
N_TILES = N_TOK // ROW_TILE
CTX_TILES = N_CTX // ROW_TILE
VMEM_LIMIT = 48 * 1024 * 1024

DA_HEADS = 4
DA_QK = 32
MLA_HEADS = 4
MLA_NOPE = 64
MLA_ROPE = 32
MLA_Q_RANK = 192
MLA_KV_RANK = 128
S5_GROUPS = 16
S5_CH = 16
S5_STATE = 64
S5_CHUNK = 16
HG_HEADS = 4
HG_DK = 64

W_A = 1536
W_B = 512
W_C = 1536
W_D = 896
W_ABC = W_A + W_B + W_C


def _cparams(*sem):
    return pltpu.CompilerParams(dimension_semantics=sem, vmem_limit_bytes=VMEM_LIMIT)


def _tile_seq(i):
    return jnp.where(i < CTX_TILES, 0, 1 + (i - CTX_TILES) // LAT_TILES)


def _silu(x):
    return x * (1.0 / (1.0 + jnp.exp(-x)))


def _dot(a, b):
    return jnp.dot(a, b, preferred_element_type=F32)


def _dot_nt(a, b):
    return lax.dot_general(a, b, (((1,), (1,)), ((), ())), preferred_element_type=F32)


def _mod_body(c_ref, w_ref, b_ref, o_ref):
    c = _silu(c_ref[...]).astype(BF16)
    o_ref[0] = _dot(c, w_ref[0].astype(BF16)) + b_ref[0]


def _modulation(c_rows, w_mod, b_mod):
    tn = 768
    return pl.pallas_call(
        _mod_body,
        grid=(DEPTH, 3 * D_MODEL // tn),
        in_specs=[pl.BlockSpec((8, D_MODEL), lambda l, j: (0, 0)),
                  pl.BlockSpec((1, D_MODEL, tn), lambda l, j: (l, 0, j)),
                  pl.BlockSpec((1, 1, tn), lambda l, j: (l, 0, j))],
        out_specs=pl.BlockSpec((1, 8, tn), lambda l, j: (l, 0, j)),
        out_shape=jax.ShapeDtypeStruct((DEPTH, 8, 3 * D_MODEL), F32),
        compiler_params=_cparams("parallel", "parallel"),
        name="modulation",
    )(c_rows, w_mod, b_mod.reshape(DEPTH, 1, 3 * D_MODEL))


def _split_rows(i, ctx_ref, lat_ref):
    return jnp.where(i < CTX_TILES, ctx_ref[...], lat_ref[...])


def _ctx_tile_spec(w):
    return pl.BlockSpec((ROW_TILE, w), lambda i: (jnp.minimum(i, CTX_TILES - 1), 0))


def _lat_tile_spec(w):
    return pl.BlockSpec((ROW_TILE, w), lambda i: (jnp.maximum(i - CTX_TILES, 0), 0))


def _in_proj_body(*refs, split):
    if split:
        xc_ref, xl_ref, mod_ref, w_ref, wd_ref, oa, ob, oc, od = refs
        x = _split_rows(pl.program_id(0), xc_ref, xl_ref)
    else:
        x_ref, mod_ref, w_ref, wd_ref, oa, ob, oc, od = refs
        x = x_ref[...]
    xn = x * lax.rsqrt(jnp.mean(x * x, axis=-1, keepdims=True) + EPS)
    mod = mod_ref[0]
    h = (xn * (1.0 + mod[1:2]) + mod[0:1]).astype(BF16)
    off = 0
    for o in (oa, ob, oc):
        w = o.shape[-1]
        o[...] = _dot(h, w_ref[0, :, off:off + w])
        off += w
    od[...] = _dot(h, wd_ref[0])


def _in_proj(xs, mod, w_abc, w_d, l):
    widths = (W_A, W_B, W_C, W_D)
    split = len(xs) == 2
    x_specs = ([_ctx_tile_spec(D_MODEL), _lat_tile_spec(D_MODEL)] if split
               else [pl.BlockSpec((ROW_TILE, D_MODEL), lambda i: (i, 0))])
    return pl.pallas_call(
        functools.partial(_in_proj_body, split=split),
        grid=(N_TILES,),
        in_specs=x_specs + [pl.BlockSpec((1, 3, D_MODEL), lambda i: (_tile_seq(i), 0, 0)),
                            pl.BlockSpec((1, D_MODEL, W_ABC), lambda i: (l, 0, 0)),
                            pl.BlockSpec((1, D_MODEL, W_D), lambda i: (l, 0, 0))],
        out_specs=[pl.BlockSpec((ROW_TILE, w), lambda i: (i, 0)) for w in widths],
        out_shape=[jax.ShapeDtypeStruct((N_TOK, w), F32) for w in widths],
        compiler_params=_cparams("parallel"),
        name="in_proj",
    )(*xs, mod, w_abc, w_d)


def _arrange_body(w_ref, prot_ref, pdup_ref, pint_ref, o_ref):
    w = w_ref[0].astype(BF16)

    def perm(x, p_ref):
        return _dot(x, p_ref[...]).astype(BF16)
    o_ref[0] = jnp.concatenate(
        [w[:, 0:1024], perm(w[:, 0:256], prot_ref), perm(w[:, 256:512], prot_ref), w[:, 1024:1536],
         perm(w[:, 1536:1792], pdup_ref), perm(w[:, 1792:2304], pint_ref), w[:, 2304:2816]], axis=-1)


def _arrange_w_in(w_in, prot, pdup, pint):
    rows = 128
    const = lambda a: pl.BlockSpec(a.shape, lambda l, i: (0, 0))
    return pl.pallas_call(
        _arrange_body,
        grid=(DEPTH, D_MODEL // rows),
        in_specs=[pl.BlockSpec((1, rows, w_in.shape[-1]), lambda l, i: (l, i, 0)), const(prot), const(pdup), const(pint)],
        out_specs=pl.BlockSpec((1, rows, W_ABC), lambda l, i: (l, i, 0)),
        out_shape=jax.ShapeDtypeStruct((DEPTH, D_MODEL, W_ABC), BF16),
        compiler_params=_cparams("parallel", "parallel"),
        name="arrange_w_in",
    )(w_in, prot, pdup, pint)


def _out_proj_body(*refs, split_in, final):
    ac_ref, al_ref, b_ref, c_ref, dc_ref, dl_ref = refs[:6]
    i = pl.program_id(0)
    if split_in:
        xc_ref, xl_ref, mod_ref, w_ref, fn_ref = refs[6:11]
        x = _split_rows(i, xc_ref, xl_ref)
    else:
        x_ref, mod_ref, w_ref, fn_ref = refs[6:10]
        x = x_ref[...]
    branches = (_split_rows(i, ac_ref, al_ref), b_ref[...], c_ref[...], _split_rows(i, dc_ref, dl_ref))
    acc = None
    for j, r in enumerate(branches):
        t = _dot(r.astype(BF16), w_ref[0, j * BRANCH:(j + 1) * BRANCH, :])
        acc = t if acc is None else acc + t
    x = x + mod_ref[0][2:3] * acc
    if not final:
        refs[-1][...] = x
        return
    y = x * lax.rsqrt(jnp.mean(x * x, axis=-1, keepdims=True) + EPS) * fn_ref[...]
    yc_ref, yl_ref = refs[-2:]

    @pl.when(i < CTX_TILES)
    def _():
        yc_ref[...] = y

    @pl.when(i >= CTX_TILES)
    def _():
        yl_ref[...] = y


def _out_proj(a, b, c, d, xs, mod, w_out, l, final_norm, final):
    br = pl.BlockSpec((ROW_TILE, BRANCH), lambda i: (i, 0))
    pair = [_ctx_tile_spec(BRANCH), _lat_tile_spec(BRANCH)]
    split_in = len(xs) == 2
    x_specs = ([_ctx_tile_spec(D_MODEL), _lat_tile_spec(D_MODEL)] if split_in
               else [pl.BlockSpec((ROW_TILE, D_MODEL), lambda i: (i, 0))])
    if final:
        out_specs = [_ctx_tile_spec(D_MODEL), _lat_tile_spec(D_MODEL)]
        out_shape = [jax.ShapeDtypeStruct((N_CTX, D_MODEL), F32), jax.ShapeDtypeStruct((N_LAT, D_MODEL), F32)]
    else:
        out_specs = pl.BlockSpec((ROW_TILE, D_MODEL), lambda i: (i, 0))
        out_shape = jax.ShapeDtypeStruct((N_TOK, D_MODEL), F32)
    return pl.pallas_call(
        functools.partial(_out_proj_body, split_in=split_in, final=final),
        grid=(N_TILES,),
        in_specs=pair + [br, br] + pair + x_specs + [
            pl.BlockSpec((1, 3, D_MODEL), lambda i: (_tile_seq(i), 0, 0)),
            pl.BlockSpec((1, D_MODEL, D_MODEL), lambda i: (l, 0, 0)),
            pl.BlockSpec((1, D_MODEL), lambda i: (0, 0))],
        out_specs=out_specs,
        out_shape=out_shape,
        compiler_params=_cparams("arbitrary"),
        name="out_proj",
    )(*a, b, c, *d, *xs, mod, w_out, final_norm)


LOG2E = 1.4426950408889634


def _exp2_rows(s):
    e = jnp.exp2(s - jnp.max(s, axis=-1, keepdims=True))
    return e, jnp.sum(e, axis=-1, keepdims=True)


def _da_kv_body(k_ref, kr_ref, v_ref, cos_ref, sin_ref, ck_ref, cv_ref, ko_ref, vo_ref):
    j = pl.program_id(1)

    @pl.when(j < LAT_TILES)
    def _():
        ko_ref[0] = (k_ref[...] * cos_ref[...] + kr_ref[...] * sin_ref[...]).astype(BF16)
        vo_ref[0] = v_ref[...].astype(BF16)

    @pl.when(j == LAT_TILES)
    def _():
        ko_ref[0] = ck_ref[0].astype(BF16)
        vo_ref[0] = cv_ref[0].astype(BF16)


def _da_latent_kv(z_a, cos, sin, cache_k, cache_v):
    def rows(col):
        return pl.BlockSpec(
            (ROW_TILE, BRANCH),
            lambda b, j: (CTX_TILES + b * LAT_TILES + jnp.minimum(j, LAT_TILES - 1), col))
    tab = pl.BlockSpec((ROW_TILE, BRANCH), lambda b, j: (jnp.minimum(j, LAT_TILES - 1), 0))
    cache = pl.BlockSpec((1, PAST_LEN, BRANCH), lambda b, j: (b, 0, 0))
    out = pl.BlockSpec((1, ROW_TILE, BRANCH), lambda b, j: (b, j, 0))
    shp = jax.ShapeDtypeStruct((N_LAT_SEQ, LAT_LEN + PAST_LEN, BRANCH), BF16)
    return pl.pallas_call(
        _da_kv_body,
        grid=(N_LAT_SEQ, LAT_TILES + 1),
        in_specs=[rows(1), rows(5), rows(2), tab, tab, cache, cache],
        out_specs=[out, out],
        out_shape=[shp, shp],
        compiler_params=_cparams("parallel", "parallel"),
        name="da_kv",
    )(z_a, z_a, z_a, cos, sin, cache_k, cache_v)


def _da_attn_body(lam_ref, ng_ref, q_ref, *rest, rope, lam_init):
    if rope:
        qr_ref, cos_ref, sin_ref, k_ref, v_ref, g_ref, o_ref = rest
        q = q_ref[...] * cos_ref[...] + qr_ref[...] * sin_ref[...]
        k = k_ref[0]
        v = v_ref[0]
    else:
        k_ref, v_ref, g_ref, o_ref = rest
        q = q_ref[...]
        k = k_ref[...].astype(BF16)
        v = v_ref[...].astype(BF16)
    q = q * (DA_QK ** -0.5 * LOG2E)
    lv = lam_ref[...]
    lam = (jnp.exp(jnp.sum(lv[0:1] * lv[1:2], axis=-1, keepdims=True))
           - jnp.exp(jnp.sum(lv[2:3] * lv[3:4], axis=-1, keepdims=True)) + lam_init)
    lane = lax.broadcasted_iota(jnp.int32, (1, BRANCH), 1)
    acc = jnp.zeros(q.shape, F32)
    for h in range(DA_HEADS):
        q1 = jnp.where(lane // DA_QK == 2 * h, q, 0.0).astype(BF16)
        q2 = jnp.where(lane // DA_QK == 2 * h + 1, q, 0.0).astype(BF16)
        e1, l1 = _exp2_rows(_dot_nt(q1, k))
        e2, l2 = _exp2_rows(_dot_nt(q2, k))
        a = (e1 - (lam * l1 / l2) * e2).astype(BF16)
        acc = jnp.where(lane // (2 * DA_QK) == h, _dot(a, v) * (1.0 / l1), acc)
    sq = acc * acc
    ms = jnp.zeros(q.shape, F32)
    for h in range(DA_HEADS):
        hm = lane // (2 * DA_QK) == h
        ms = jnp.where(hm, jnp.sum(jnp.where(hm, sq, 0.0), axis=-1, keepdims=True), ms)
    o = acc * lax.rsqrt(ms * (1.0 / (2 * DA_QK)) + EPS) * (ng_ref[...] * (1.0 - lam_init))
    o_ref[...] = o * _silu(g_ref[...])


def _da_attention(z_a, lam_vec, norm_g, lam_init, cos, sin, kv_lat):
    ng = jnp.tile(norm_g.reshape(1, 2 * DA_QK), (1, DA_HEADS))
    small = [pl.BlockSpec((4, DA_QK), lambda *_: (0, 0)), pl.BlockSpec((1, BRANCH), lambda *_: (0, 0))]

    def col(c):
        return pl.BlockSpec((ROW_TILE, BRANCH), lambda i: (i, c))
    ctx = pl.pallas_call(
        functools.partial(_da_attn_body, rope=False, lam_init=lam_init),
        grid=(CTX_TILES,),
        in_specs=small + [col(0), col(1), col(2), col(3)],
        out_specs=pl.BlockSpec((ROW_TILE, BRANCH), lambda i: (i, 0)),
        out_shape=jax.ShapeDtypeStruct((N_CTX, BRANCH), F32),
        compiler_params=_cparams("parallel"),
        name="da_attn_ctx",
    )(lam_vec, ng, z_a, z_a, z_a, z_a)

    def lcol(c):
        return pl.BlockSpec((ROW_TILE, BRANCH), lambda b, j: (CTX_TILES + b * LAT_TILES + j, c))
    tab = pl.BlockSpec((ROW_TILE, BRANCH), lambda b, j: (j, 0))
    kvs = pl.BlockSpec((1, LAT_LEN + PAST_LEN, BRANCH), lambda b, j: (b, 0, 0))
    lat = pl.pallas_call(
        functools.partial(_da_attn_body, rope=True, lam_init=lam_init),
        grid=(N_LAT_SEQ, LAT_TILES),
        in_specs=small + [lcol(0), lcol(4), tab, tab, kvs, kvs, lcol(3)],
        out_specs=pl.BlockSpec((ROW_TILE, BRANCH), lambda b, j: (b * LAT_TILES + j, 0)),
        out_shape=jax.ShapeDtypeStruct((N_LAT, BRANCH), F32),
        compiler_params=_cparams("parallel", "parallel"),
        name="da_attn_lat",
    )(lam_vec, ng, z_a, z_a, cos, sin, kv_lat[0], kv_lat[1], z_a)
    return ctx, lat


MLA_HEAD_PAD = 128
MLA_QW = MLA_HEADS * MLA_HEAD_PAD


def _mla_prep_body(cq_ref, ckv_ref, kr_ref, krr_ref, cq_t, sq_t, ck_t, sk_t, qn_ref, kvn_ref, wq_ref, wqr_ref,
                   q_out, ckv_out, kr_out):
    cq = cq_ref[...]
    ms = jnp.sum(cq * cq, axis=-1, keepdims=True) * (1.0 / MLA_Q_RANK)
    qn = (cq * lax.rsqrt(ms + EPS) * qn_ref[...]).astype(BF16)
    q = _dot(qn, wq_ref[...]) * cq_t[...] + _dot(qn, wqr_ref[...]) * sq_t[...]
    q_out[...] = (q * ((MLA_NOPE + MLA_ROPE) ** -0.5 * LOG2E)).astype(BF16)
    ckv = ckv_ref[...]
    ckv_out[...] = ckv * lax.rsqrt(jnp.mean(ckv * ckv, axis=-1, keepdims=True) + EPS) * kvn_ref[...]
    kr_out[...] = kr_ref[...] * ck_t[...] + krr_ref[...] * sk_t[...]


def _mla_prep(z_d, tabs, q_norm_pad, kv_norm, wq, wqr):
    def tab(w):
        return pl.BlockSpec(
            (ROW_TILE, w), lambda i: (jnp.where(i < CTX_TILES, LAT_TILES, (i - CTX_TILES) % LAT_TILES), 0))

    def col(w, c):
        return pl.BlockSpec((ROW_TILE, w), lambda i: (i, c))

    def const(shape):
        return pl.BlockSpec(shape, lambda i: (0, 0))
    return pl.pallas_call(
        _mla_prep_body,
        grid=(N_TILES,),
        in_specs=[col(256, 0), col(128, 2), col(128, 3), col(128, 6),
                  tab(MLA_QW), tab(MLA_QW), tab(128), tab(128),
                  const((1, 256)), const((1, 128)), const((256, MLA_QW)), const((256, MLA_QW))],
        out_specs=[col(MLA_QW, 0), col(128, 0), col(128, 0)],
        out_shape=[jax.ShapeDtypeStruct((N_TOK, MLA_QW), BF16),
                   jax.ShapeDtypeStruct((N_TOK, 128), F32),
                   jax.ShapeDtypeStruct((N_TOK, 128), F32)],
        compiler_params=_cparams("parallel"),
        name="mla_prep",
    )(z_d, z_d, z_d, z_d, *tabs, q_norm_pad, kv_norm, wq, wqr)


def _mla_kv_body(ckv_ref, kr_ref, wk_ref, wv_ref, k_out, v_out):
    c = ckv_ref[...].astype(BF16)
    kr = kr_ref[...]
    k_out[...] = (_dot(c, wk_ref[...]) + jnp.concatenate([kr] * MLA_HEADS, axis=-1)).astype(BF16)
    v_out[...] = _dot(c, wv_ref[...]).astype(BF16)


def _mla_kv(ckv, kr, wk, wv, n_rows):
    return pl.pallas_call(
        _mla_kv_body,
        grid=(n_rows // ROW_TILE,),
        in_specs=[pl.BlockSpec((ROW_TILE, 128), lambda i: (i, 0)),
                  pl.BlockSpec((ROW_TILE, 128), lambda i: (i, 0)),
                  pl.BlockSpec((128, MLA_QW), lambda i: (0, 0)),
                  pl.BlockSpec((128, BRANCH), lambda i: (0, 0))],
        out_specs=[pl.BlockSpec((ROW_TILE, MLA_QW), lambda i: (i, 0)),
                   pl.BlockSpec((ROW_TILE, BRANCH), lambda i: (i, 0))],
        out_shape=[jax.ShapeDtypeStruct((n_rows, MLA_QW), BF16),
                   jax.ShapeDtypeStruct((n_rows, BRANCH), BF16)],
        compiler_params=_cparams("parallel"),
        name="mla_kv",
    )(ckv, kr, wk, wv)


def _mla_attn_body(q_ref, k_ref, v_ref, g_ref, o_ref):
    q = q_ref[...]
    k = k_ref[...].reshape(-1, MLA_QW)
    v = v_ref[...].reshape(-1, BRANCH)
    lane = lax.broadcasted_iota(jnp.int32, (1, BRANCH), 1)
    acc = jnp.zeros((q.shape[0], BRANCH), F32)
    for h in range(MLA_HEADS):
        sl = slice(h * MLA_HEAD_PAD, (h + 1) * MLA_HEAD_PAD)
        e, l = _exp2_rows(_dot_nt(q[:, sl], k[:, sl]))
        acc = jnp.where(lane // 64 == h, _dot(e.astype(BF16), v) * (1.0 / l), acc)
    o_ref[...] = acc * _silu(g_ref[...])


def _mla_attention(z_d, q, k_ctx, v_ctx, k_lat, v_lat):
    ctx = pl.pallas_call(
        _mla_attn_body,
        grid=(CTX_TILES,),
        in_specs=[pl.BlockSpec((ROW_TILE, MLA_QW), lambda i: (i, 0)),
                  pl.BlockSpec((ROW_TILE, MLA_QW), lambda i: (i, 0)),
                  pl.BlockSpec((ROW_TILE, BRANCH), lambda i: (i, 0)),
                  pl.BlockSpec((ROW_TILE, BRANCH), lambda i: (i, 2))],
        out_specs=pl.BlockSpec((ROW_TILE, BRANCH), lambda i: (i, 0)),
        out_shape=jax.ShapeDtypeStruct((N_CTX, BRANCH), F32),
        compiler_params=_cparams("parallel"),
        name="mla_attn_ctx",
    )(q, k_ctx, v_ctx, z_d)
    lk = LAT_LEN + PAST_LEN
    lat = pl.pallas_call(
        _mla_attn_body,
        grid=(N_LAT_SEQ, LAT_TILES),
        in_specs=[pl.BlockSpec((ROW_TILE, MLA_QW), lambda b, j: (CTX_TILES + b * LAT_TILES + j, 0)),
                  pl.BlockSpec((1, lk, MLA_QW), lambda b, j: (b, 0, 0)),
                  pl.BlockSpec((1, lk, BRANCH), lambda b, j: (b, 0, 0)),
                  pl.BlockSpec((ROW_TILE, BRANCH), lambda b, j: (CTX_TILES + b * LAT_TILES + j, 2))],
        out_specs=pl.BlockSpec((ROW_TILE, BRANCH), lambda b, j: (b * LAT_TILES + j, 0)),
        out_shape=jax.ShapeDtypeStruct((N_LAT, BRANCH), F32),
        compiler_params=_cparams("parallel", "parallel"),
        name="mla_attn_lat",
    )(q, k_lat.reshape(N_LAT_SEQ, lk, MLA_QW), v_lat.reshape(N_LAT_SEQ, lk, BRANCH), z_d)
    return ctx, lat


S5_TAP = S5_CHUNK * S5_CH
S5_NCHUNK = N_TOK // S5_CHUNK
S5_CTX_CH = N_CTX // S5_CHUNK
S5_CTX_SEQ_CH = CTX_LEN // S5_CHUNK
S5_LAT_SEQ_CH = LAT_LEN // S5_CHUNK
S5_SCAN_STEPS = S5_LAT_SEQ_CH.bit_length() - 1


def _s5_body(x_ref, mt_ref, bst_ref, cot_ref, a_ref, h0_ref, y_ref, fin_ref):
    x = x_ref[...].reshape(S5_TAP, S5_NCHUNK)
    y = _dot(mt_ref[0, 0], x)
    s = _dot(bst_ref[0, 0], x)
    lane = lax.broadcasted_iota(jnp.int32, (1, S5_NCHUNK), 1)
    is_lat = lane >= S5_CTX_CH
    pos_f = jnp.where(is_lat, (lane - S5_CTX_CH) & (S5_LAT_SEQ_CH - 1), lane & (S5_CTX_SEQ_CH - 1))
    pos_b = jnp.where(is_lat, S5_LAT_SEQ_CH - 1, S5_CTX_SEQ_CH - 1) - pos_f
    hin = []
    for d in range(2):
        n = S5_STATE
        sre, sim = s[2 * d * n:(2 * d + 1) * n], s[(2 * d + 1) * n:(2 * d + 2) * n]
        are = jnp.concatenate([a_ref[0, 0, 2 * d]] * (S5_NCHUNK // 128), axis=-1)
        aim = jnp.concatenate([a_ref[0, 0, 2 * d + 1]] * (S5_NCHUNK // 128), axis=-1)
        pos = pos_f if d == 0 else pos_b
        h0r, h0i = jnp.zeros_like(sre), jnp.zeros_like(sre)
        for b in range(N_LAT_SEQ):
            first = S5_CTX_CH + b * S5_LAT_SEQ_CH + (0 if d == 0 else S5_LAT_SEQ_CH - 1)
            h0r = jnp.where(lane == first, h0_ref[0, 2 * d][:, b:b + 1], h0r)
            h0i = jnp.where(lane == first, h0_ref[0, 2 * d + 1][:, b:b + 1], h0i)
        xr = sre + are * h0r - aim * h0i
        xi = sim + are * h0i + aim * h0r
        pr, pi = are, aim
        for j in range(S5_SCAN_STEPS):
            sh = 1 << j
            shift = sh if d == 0 else S5_NCHUNK - sh
            rr, ri = pltpu.roll(xr, shift, 1), pltpu.roll(xi, shift, 1)
            ok = pos >= sh
            xr, xi = (xr + jnp.where(ok, pr * rr - pi * ri, 0.0), xi + jnp.where(ok, pr * ri + pi * rr, 0.0))
            pr, pi = pr * pr - pi * pi, 2.0 * pr * pi
        fin_ref[0, 2 * d] = xr[:, :S5_CTX_CH]
        fin_ref[0, 2 * d + 1] = xi[:, :S5_CTX_CH]
        one = 1 if d == 0 else S5_NCHUNK - 1
        hin.append(jnp.where(pos >= 1, pltpu.roll(xr, one, 1), h0r))
        hin.append(jnp.where(pos >= 1, pltpu.roll(xi, one, 1), h0i))
    y = y + _dot(cot_ref[0, 0], jnp.concatenate(hin, axis=0).astype(BF16))
    y_ref[...] = y.reshape(S5_CHUNK, S5_CH, S5_NCHUNK)


def _s5_scan(x_all, mt, bst, cot, a16, h0, l):
    g = S5_GROUPS
    sq = pl.BlockSpec((1, 1, S5_TAP, S5_TAP), lambda i: (l, i, 0, 0))
    st = pl.BlockSpec((1, 4, S5_STATE, 128), lambda i: (i, 0, 0, 0))
    return pl.pallas_call(
        _s5_body,
        grid=(g,),
        in_specs=[pl.BlockSpec((S5_CHUNK, S5_CH, S5_NCHUNK), lambda i: (0, i, 0)), sq, sq, sq,
                  pl.BlockSpec((1, 1, 4, S5_STATE, 128), lambda i: (l, i, 0, 0, 0)), st],
        out_specs=[pl.BlockSpec((S5_CHUNK, S5_CH, S5_NCHUNK), lambda i: (0, i, 0)),
                   pl.BlockSpec((1, 4, S5_STATE, S5_CTX_CH), lambda i: (i, 0, 0, 0))],
        out_shape=[jax.ShapeDtypeStruct((S5_CHUNK, BRANCH, S5_NCHUNK), F32),
                   jax.ShapeDtypeStruct((g, 4, S5_STATE, S5_CTX_CH), F32)],
        compiler_params=_cparams("parallel"),
        name="s5_scan",
    )(x_all, mt, bst, cot, a16, h0)


def _s5_out_body(y_ref, u_ref, g_ref, d_ref, w_ref, o_ref):
    y = u_ref[...] * d_ref[...] + y_ref[...]
    ge = 0.5 * y * (1.0 + jnp.tanh(0.7978845608028654 * (y + 0.044715 * (y * y * y))))
    gl = _dot(ge.astype(BF16), w_ref[...])
    o_ref[...] = gl[:, :BRANCH] * (1.0 / (1.0 + jnp.exp(-gl[:, BRANCH:]))) * _silu(g_ref[...])


def _s5_out(y_ssm, z_b, d_skip, w_glu):
    def col(c):
        return pl.BlockSpec((ROW_TILE, BRANCH), lambda i: (i, c))
    return pl.pallas_call(
        _s5_out_body,
        grid=(N_TILES,),
        in_specs=[col(0), col(0), col(1),
                  pl.BlockSpec((1, BRANCH), lambda i: (0, 0)),
                  pl.BlockSpec((BRANCH, 2 * BRANCH), lambda i: (0, 0))],
        out_specs=col(0),
        out_shape=jax.ShapeDtypeStruct((N_TOK, BRANCH), F32),
        compiler_params=_cparams("parallel"),
        name="s5_out",
    )(y_ssm, z_b, z_b, d_skip, w_glu)


HG_CHUNK = ROW_TILE
HG_W = 2 * HG_HEADS * HG_DK
HG_HEAD_W = 2 * HG_DK
HG_LAT_CHUNKS = LAT_LEN // HG_CHUNK
HG_CHUNKS = N_TOK // HG_CHUNK


def _hg_gates(z, lb):
    e = jnp.exp(-jnp.abs(z))
    r = 1.0 / (1.0 + e)
    sig_pos = jnp.where(z >= 0, r, e * r)
    sig_neg = jnp.where(z >= 0, e * r, r)
    return lb + (1.0 - lb) * sig_pos, (1.0 - lb) * sig_neg


def _bcast_row(x, period, r):
    n, w = x.shape
    if period >= 8:
        x3 = x.reshape(n // period, period, w)
        return jnp.broadcast_to(x3[:, r:r + 1, :], x3.shape).reshape(n, w)
    x3 = x.reshape(n // 8, 8, w)
    sub = lax.broadcasted_iota(jnp.int32, (1, 8, 1), 1)
    out = None
    for j in range(8 // period):
        b = jnp.broadcast_to(x3[:, j * period + r:j * period + r + 1, :], x3.shape)
        out = b if out is None else jnp.where(sub >= j * period, b, out)
    return out.reshape(n, w)


def _hg_scans(f, isb):
    n = f.shape[0]
    row = lax.broadcasted_iota(jnp.int32, (n, 1), 0)
    p, r = f, jnp.ones_like(f)
    levels = []
    h, sh = 1, 0
    while h < n:
        levels.append((h, sh, p, r))
        up = (row >> sh) & 1
        tot_p = jnp.where(isb == 1, _bcast_row(p, 2 * h, h), _bcast_row(p, 2 * h, h - 1))
        tot_r = jnp.where(isb == 1, _bcast_row(p, 2 * h, 0), _bcast_row(p, 2 * h, 2 * h - 1))
        p = p * jnp.where(up != isb, tot_p, 1.0)
        r = r * jnp.where(up == isb, tot_r, 1.0)
        h, sh = 2 * h, sh + 1
    return levels, p, r


def _hg_state_body(zf_ref, zb_ref, vf_ref, vb_ref, lb_ref, s0_ref, sf_out, sb_out, fin_out, s_scr):
    i = pl.program_id(0)
    first = jnp.logical_or(i < N_CTX_SEQ, (i - N_CTX_SEQ) % HG_LAT_CHUNKS == 0)

    @pl.when(first)
    def _():
        s_scr[...] = s0_ref[0]

    sf_out[0] = s_scr[:, 0:HG_DK, :]
    sb_out[0] = s_scr[:, HG_DK:, :]
    lane5 = lax.broadcasted_iota(jnp.int32, (1, HG_W), 1)
    isb = (lane5 >> 6) & 1
    z = jnp.where(isb == 1, zb_ref[...], zf_ref[...])
    f, k = _hg_gates(z, lb_ref[...])
    r, ptot = _hg_chunk_decay(f, isb)
    kt = k * r
    lane = lax.broadcasted_iota(jnp.int32, (1, BRANCH), 1)
    vf = vf_ref[...]
    vb = vb_ref[...]
    for hd in range(HG_HEADS):
        sl = slice(hd * HG_HEAD_W, (hd + 1) * HG_HEAD_W)
        kth = kt[:, sl].T.astype(BF16)
        hm = (lane >> 6) == hd
        d_f = _dot(kth, jnp.where(hm, vf, 0.0).astype(BF16))
        d_b = _dot(kth, jnp.where(hm, vb, 0.0).astype(BF16))
        ds = jnp.concatenate([d_f[:HG_DK], d_b[HG_DK:]], axis=0)
        pcol = jnp.broadcast_to(ptot[:, sl], (HG_HEAD_W, HG_HEAD_W)).T[:, 0:1]
        s_scr[hd] = s_scr[hd] * pcol + ds

    @pl.when(i < N_CTX_SEQ)
    def _():
        for hd in range(HG_HEADS):
            fin_out[0, hd] = s_scr[hd][:, hd * HG_DK:(hd + 1) * HG_DK]


def _hg_chunk_decay(f, isb):
    n = f.shape[0]
    row = lax.broadcasted_iota(jnp.int32, (n, 1), 0)
    dist = jnp.where(isb == 1, row, n - 1 - row)
    x = f
    sh = 1
    while sh < n:
        src = jnp.where(isb == 1, pltpu.roll(x, sh, 0), pltpu.roll(x, n - sh, 0))
        x = x * jnp.where(dist >= sh, src, 1.0)
        sh *= 2
    total = jnp.where(isb == 1, x[n - 1:n], x[0:1])
    nxt = jnp.where(isb == 1, pltpu.roll(x, 1, 0), pltpu.roll(x, n - 1, 0))
    return jnp.where(dist >= 1, nxt, 1.0), total


def _hg_rev(i):
    j = i - N_CTX_SEQ
    return jnp.where(i < N_CTX_SEQ, i, N_CTX_SEQ + (j // HG_LAT_CHUNKS) * HG_LAT_CHUNKS
                     + (HG_LAT_CHUNKS - 1 - j % HG_LAT_CHUNKS))


def _hg_states(z_c, lb, s0):
    zz_f = pl.BlockSpec((HG_CHUNK, HG_W), lambda i: (i, 1))
    zz_b = pl.BlockSpec((HG_CHUNK, HG_W), lambda i: (_hg_rev(i), 1))
    v_f = pl.BlockSpec((HG_CHUNK, BRANCH), lambda i: (i, 4))
    v_b = pl.BlockSpec((HG_CHUNK, BRANCH), lambda i: (_hg_rev(i), 4))
    st = (HG_HEADS, HG_HEAD_W, BRANCH)
    half = (HG_HEADS, HG_DK, BRANCH)
    fin = (HG_HEADS, HG_HEAD_W, HG_DK)
    return pl.pallas_call(
        _hg_state_body,
        grid=(HG_CHUNKS,),
        in_specs=[zz_f, zz_b, v_f, v_b,
                  pl.BlockSpec((1, HG_W), lambda i: (0, 0)),
                  pl.BlockSpec((1,) + st, lambda i: (
                      jnp.where(i < N_CTX_SEQ, 0, 1 + (i - N_CTX_SEQ) // HG_LAT_CHUNKS), 0, 0, 0))],
        out_specs=[pl.BlockSpec((1,) + half, lambda i: (i, 0, 0, 0)),
                   pl.BlockSpec((1,) + half, lambda i: (_hg_rev(i), 0, 0, 0)),
                   pl.BlockSpec((1,) + fin, lambda i: (jnp.minimum(i, N_CTX_SEQ - 1), 0, 0, 0))],
        out_shape=[jax.ShapeDtypeStruct((HG_CHUNKS,) + half, F32),
                   jax.ShapeDtypeStruct((HG_CHUNKS,) + half, F32),
                   jax.ShapeDtypeStruct((N_CTX_SEQ,) + fin, F32)],
        scratch_shapes=[pltpu.VMEM(st, F32)],
        compiler_params=_cparams("arbitrary"),
        name="hg_states",
    )(z_c, z_c, z_c, z_c, lb, s0)


def _hg_main_body(qq_ref, zz_ref, v_ref, g_ref, sf_ref, sb_ref, lb_ref, ng_ref, o_ref):
    n = HG_CHUNK
    qq = qq_ref[...]
    lane5 = lax.broadcasted_iota(jnp.int32, (1, HG_W), 1)
    isb = (lane5 >> 6) & 1
    f, k = _hg_gates(zz_ref[...], lb_ref[...])
    levels, pfull, _ = _hg_scans(f, isb)
    row = lax.broadcasted_iota(jnp.int32, (n, 1), 0)
    col = lax.broadcasted_iota(jnp.int32, (1, n), 1)
    ops = [(qq.astype(BF16), k.astype(BF16), row == col)]
    for h, sh, p, r in levels:
        up = (row >> sh) & 1
        qt = jnp.where(up != isb, qq * p, 0.0).astype(BF16)
        kt = jnp.where(up == isb, k * r, 0.0).astype(BF16)
        ops.append((qt, kt, (row >> (sh + 1)) == (col >> (sh + 1))))
    qc = (qq * pfull).astype(BF16)
    vb = v_ref[...].astype(BF16)
    lane = lax.broadcasted_iota(jnp.int32, (1, BRANCH), 1)
    acc = jnp.zeros((n, BRANCH), F32)
    for hd in range(HG_HEADS):
        sl = slice(hd * HG_HEAD_W, (hd + 1) * HG_HEAD_W)
        a = jnp.zeros((n, n), F32)
        for qt, kt, mask in ops:
            a = a + jnp.where(mask, _dot_nt(qt[:, sl], kt[:, sl]), 0.0)
        s_in = jnp.concatenate([sf_ref[0, hd], sb_ref[0, hd]], axis=0).astype(BF16)
        o_h = _dot(a.astype(BF16), vb) + _dot(qc[:, sl], s_in)
        acc = jnp.where((lane >> 6) == hd, o_h, acc)
    sq = acc * acc
    ms = jnp.zeros((n, BRANCH), F32)
    for hd in range(HG_HEADS):
        hm = (lane >> 6) == hd
        ms = jnp.where(hm, jnp.sum(jnp.where(hm, sq, 0.0), axis=-1, keepdims=True), ms)
    o_ref[...] = acc * lax.rsqrt(ms * (1.0 / HG_DK) + EPS) * ng_ref[...] * _silu(g_ref[...])


def _hg_main(z_c, s_f, s_b, lb, norm_g):
    half = (1, HG_HEADS, HG_DK, BRANCH)
    return pl.pallas_call(
        _hg_main_body,
        grid=(HG_CHUNKS,),
        in_specs=[pl.BlockSpec((HG_CHUNK, HG_W), lambda i: (i, 0)),
                  pl.BlockSpec((HG_CHUNK, HG_W), lambda i: (i, 1)),
                  pl.BlockSpec((HG_CHUNK, BRANCH), lambda i: (i, 4)),
                  pl.BlockSpec((HG_CHUNK, BRANCH), lambda i: (i, 5)),
                  pl.BlockSpec(half, lambda i: (i, 0, 0, 0)),
                  pl.BlockSpec(half, lambda i: (i, 0, 0, 0)),
                  pl.BlockSpec((1, HG_W), lambda i: (0, 0)),
                  pl.BlockSpec((1, BRANCH), lambda i: (0, 0))],
        out_specs=pl.BlockSpec((HG_CHUNK, BRANCH), lambda i: (i, 0)),
        out_shape=jax.ShapeDtypeStruct((N_TOK, BRANCH), F32),
        compiler_params=_cparams("parallel"),
        name="hg_main",
    )(z_c, z_c, z_c, z_c, s_f, s_b, lb, norm_g)


def _rot_src(cols):
    j = np.arange(len(cols))
    return cols[(j // 16) * 16 + ((j % 16) + 8) % 16], np.where(j % 16 < 8, -1.0, 1.0)


def _take_cols(w, plan):
    idx = np.concatenate([p[0] for p in plan]).astype(np.int32)
    sign = np.concatenate([np.broadcast_to(p[1], p[0].shape) for p in plan]).astype(np.float32)
    return jnp.take(w, jnp.asarray(idx), axis=-1) * jnp.asarray(sign)


def _zeros(n):
    return (np.zeros(n, np.int64), 0.0)


_IN_OFF = {}
_off = 0
for _name, _n in (("da_q", 256), ("da_k", 256), ("da_v", 256), ("da_g", 256), ("s5_u", 256), ("s5_g", 256),
                  ("hg_q", 256), ("hg_ff", 256), ("hg_fb", 256), ("hg_i", 256), ("hg_g", 256),
                  ("mla_cq", MLA_Q_RANK), ("mla_ckv", MLA_KV_RANK), ("mla_kr", MLA_ROPE), ("mla_g", 256)):
    _IN_OFF[_name] = np.arange(_off, _off + _n)
    _off += _n


def _perm_matrix(plan, first, k):
    idx = np.concatenate([p[0] for p in plan]) - first
    sign = np.concatenate([np.broadcast_to(p[1], p[0].shape) for p in plan])
    m = np.zeros((k, len(idx)), np.float32)
    m[idx, np.arange(len(idx))] = sign
    return jnp.asarray(m, BF16)


def _in_proj_weights(w_in):
    c = _IN_OFF

    def per_head(x, y):
        return (np.concatenate([c[x].reshape(HG_HEADS, HG_DK), c[y].reshape(HG_HEADS, HG_DK)], axis=1).reshape(-1), 1.0)
    prot = _perm_matrix([_rot_src(c["da_q"])], 0, 256)
    pdup = _perm_matrix([per_head("hg_q", "hg_q")], c["hg_q"][0], 256)
    pint = _perm_matrix([per_head("hg_ff", "hg_fb")], c["hg_ff"][0], 512)
    col = lambda name: w_in[..., c[name][0]:c[name][-1] + 1]
    zero = lambda n: jnp.zeros(w_in.shape[:-1] + (n,), w_in.dtype)
    kr = col("mla_kr")
    kr_rot = jnp.concatenate([-kr[..., 8:16], kr[..., 0:8], -kr[..., 24:32], kr[..., 16:24]], axis=-1)
    tail = 128 - MLA_NOPE - MLA_ROPE
    w_d = jnp.concatenate([col("mla_cq"), zero(256 - MLA_Q_RANK), col("mla_ckv"), zero(MLA_NOPE), kr, zero(tail),
                           col("mla_g"), zero(MLA_NOPE), kr_rot, zero(tail)], axis=-1)
    return _arrange_w_in(w_in, prot, pdup, pint), w_d.astype(BF16)


def _rope_tables():
    t = np.arange(LAT_LEN)
    pos = np.stack([t // GRID_W, t % GRID_W], axis=1).astype(np.float32)
    inv_freq = (np.float32(ROPE_BASE) ** (-np.arange(8, dtype=np.float32) / np.float32(8))).astype(np.float32)
    r = np.arange(MLA_ROPE)
    ang = (pos[:, r // 16] * inv_freq[r % 8][None, :]).astype(np.float32)
    cos32, sin32 = jnp.asarray(np.cos(ang.astype(np.float64)), F32), jnp.asarray(np.sin(ang.astype(np.float64)), F32)
    cos_a, sin_a = jnp.tile(cos32, (1, 8)), jnp.tile(sin32, (1, 8))
    ones, zeros = jnp.ones((LAT_LEN, 1), F32), jnp.zeros((LAT_LEN, 1), F32)
    cos_h = jnp.concatenate([jnp.tile(ones, (1, 64)), cos32, jnp.tile(ones, (1, 32))], axis=1)
    sin_h = jnp.concatenate([jnp.tile(zeros, (1, 64)), sin32, jnp.tile(zeros, (1, 32))], axis=1)

    def with_identity(c, s):
        return (jnp.concatenate([c, jnp.ones((ROW_TILE, c.shape[1]), F32)], axis=0),
                jnp.concatenate([s, jnp.zeros((ROW_TILE, s.shape[1]), F32)], axis=0))
    cq, sq = with_identity(jnp.tile(cos_h, (1, MLA_HEADS)), jnp.tile(sin_h, (1, MLA_HEADS)))
    ck, sk = with_identity(cos_h, sin_h)
    return (cos_a, sin_a), (cq, sq, ck, sk)


def _mla_weights(w_uq, w_ukv, q_norm):
    hd = MLA_NOPE + MLA_ROPE
    pad_tail = _zeros(MLA_HEAD_PAD - hd)
    q_plan, qr_plan, k_plan, v_plan = [], [], [], []
    for h in range(MLA_HEADS):
        nope, rope = np.arange(h * hd, h * hd + MLA_NOPE), np.arange(h * hd + MLA_NOPE, (h + 1) * hd)
        q_plan += [(nope, 1.0), (rope, 1.0), pad_tail]
        qr_plan += [_zeros(MLA_NOPE), _rot_src(rope), pad_tail]
        k_plan += [(np.arange(h * 2 * MLA_NOPE, h * 2 * MLA_NOPE + MLA_NOPE), 1.0), _zeros(MLA_HEAD_PAD - MLA_NOPE)]
        v_plan += [(np.arange(h * 2 * MLA_NOPE + MLA_NOPE, (h + 1) * 2 * MLA_NOPE), 1.0)]
    pad_rows = lambda x: jnp.pad(x, ((0, 256 - MLA_Q_RANK), (0, 0))).astype(BF16)
    qn = jnp.pad(q_norm, (0, 256 - MLA_Q_RANK)).reshape(1, 256)
    return (pad_rows(_take_cols(w_uq, q_plan)), pad_rows(_take_cols(w_uq, qr_plan)),
            _take_cols(w_ukv, k_plan).astype(BF16), _take_cols(w_ukv, v_plan).astype(BF16), qn)


def _split_bf16(a):
    hi = a.astype(BF16)
    return hi, (a - hi.astype(F32)).astype(BF16)


def _dot_sel(a, sel):
    hi, lo = _split_bf16(a)
    sel = sel.astype(BF16)
    return _dot(hi, sel) + _dot(lo, sel)


def _dot_x3(a, b):
    a_hi, a_lo = _split_bf16(a)
    b_hi, b_lo = _split_bf16(b)
    return _dot(a_hi, b_hi) + _dot(a_hi, b_lo) + _dot(a_lo, b_hi)


def _s5_table_body(xy_ref, bb_ref, c_ref, ct_ref, mt_ref, bst_ref, cot_ref, a_ref):
    n, t, ch = S5_STATE, S5_CHUNK, S5_CH
    wide = 2 * S5_TAP
    xy = xy_ref[0, 0]
    tau_i = lax.broadcasted_iota(jnp.int32, (1, 128), 1)
    tau = tau_i.astype(F32)
    sel_row = lax.broadcasted_iota(jnp.int32, (128, 1), 0)

    def lag(width):
        return lax.broadcasted_iota(jnp.int32, (1, width), 1) >> 4

    def onehot(cond):
        return jnp.where(cond, 1.0, 0.0).astype(F32)
    j = lag(wide)
    e_z = (onehot((j <= t - 1) & (sel_row == t - 1 - j)), onehot((j >= t - 1) & (j <= 2 * t - 2) & (sel_row == j - (t - 1))))
    jc = lag(S5_TAP)
    e_c = (onehot(sel_row == jc + 1), onehot(sel_row == t - jc))
    ch_row = lax.broadcasted_iota(jnp.int32, (ch, 1), 0)
    tile_w = onehot((lax.broadcasted_iota(jnp.int32, (1, wide), 1) & (ch - 1)) == ch_row)
    tile_n = onehot((lax.broadcasted_iota(jnp.int32, (1, S5_TAP), 1) & (ch - 1)) == ch_row)

    z, cot_rows, klong = [], [], None
    for d in range(2):
        x, y = xy[:, 2 * d:2 * d + 1], xy[:, 2 * d + 1:2 * d + 2]
        mag = jnp.exp(jnp.where(tau_i <= t, tau, 0.0) * x)
        ang = jnp.where(tau_i <= t, tau, 0.0) * y
        p_re = jnp.where(tau_i <= t, mag * jnp.cos(ang), 0.0)
        p_im = jnp.where(tau_i <= t, mag * jnp.sin(ang), 0.0)
        a_ref[0, 0, 2 * d] = jnp.broadcast_to(p_re[:, t:t + 1], (n, 128))
        a_ref[0, 0, 2 * d + 1] = jnp.broadcast_to(p_im[:, t:t + 1], (n, 128))
        pz_re, pz_im = _dot_sel(p_re, e_z[d]), _dot_sel(p_im, e_z[d])
        b_re, b_im = _dot_sel(bb_ref[0, 0, 2 * d], tile_w), _dot_sel(bb_ref[0, 0, 2 * d + 1], tile_w)
        z_re, z_im = pz_re * b_re - pz_im * b_im, pz_re * b_im + pz_im * b_re
        z += [z_re, z_im]
        part = _dot_x3(c_ref[0, 0, 2 * d], z_re) - _dot_x3(c_ref[0, 0, 2 * d + 1], z_im)
        klong = part if klong is None else klong + part
        pc_re, pc_im = _dot_sel(p_re, e_c[d]), _dot_sel(p_im, e_c[d])
        c_re, c_im = _dot_sel(ct_ref[0, 0, 2 * d], tile_n), _dot_sel(ct_ref[0, 0, 2 * d + 1], tile_n)
        cot_rows += [c_re * pc_re - c_im * pc_im, -(c_re * pc_im + c_im * pc_re)]
    for tt in range(t):
        off = (t - 1 - tt) * ch
        win = klong if off == 0 else pltpu.roll(klong, wide - off, 1)
        mt_ref[0, 0, tt * ch:(tt + 1) * ch, :] = win[:, :S5_TAP].astype(BF16)
    back = pltpu.roll(z[2], wide - (t - 1) * ch, 1), pltpu.roll(z[3], wide - (t - 1) * ch, 1)
    for k, rows in enumerate((z[0], z[1], back[0], back[1])):
        bst_ref[0, 0, k * n:(k + 1) * n, :] = rows[:, :S5_TAP].astype(BF16)
    cot_ref[0, 0] = jnp.concatenate(cot_rows, axis=0).T.astype(BF16)


def _s5_tables(a_re, a_im, log_dt, b_re, b_im, c_re, c_im):
    nl, g, n, ch = a_re.shape[0], S5_GROUPS, S5_STATE, S5_CH
    step = jnp.exp(log_dt)[..., None]
    mag = jnp.exp(a_re * step)
    ab_re, ab_im = mag * jnp.cos(a_im * step), mag * jnp.sin(a_im * step)
    den = a_re * a_re + a_im * a_im
    f_re = ((ab_re - 1.0) * a_re + ab_im * a_im) / den
    f_im = (ab_im * a_re - (ab_re - 1.0) * a_im) / den
    bb_re = f_re[..., None] * b_re - f_im[..., None] * b_im
    bb_im = f_re[..., None] * b_im + f_im[..., None] * b_re
    by_group = lambda x: jnp.moveaxis(x, 1, 2)
    pair = lambda re, im: jnp.stack([by_group(re), by_group(im)], axis=3).reshape((nl, g, 4) + re.shape[3:])
    xy = jnp.stack([by_group(a_re * step), by_group(a_im * step)], axis=3).reshape(nl, g, 4, n)
    xy = jnp.pad(jnp.swapaxes(xy, 2, 3), ((0, 0), (0, 0), (0, 0), (0, 4)))
    mat = pl.BlockSpec((1, 1, S5_TAP, S5_TAP), lambda l, i: (l, i, 0, 0))
    return pl.pallas_call(
        _s5_table_body,
        grid=(nl, g),
        in_specs=[pl.BlockSpec((1, 1, n, 8), lambda l, i: (l, i, 0, 0)),
                  pl.BlockSpec((1, 1, 4, n, ch), lambda l, i: (l, i, 0, 0, 0)),
                  pl.BlockSpec((1, 1, 4, ch, n), lambda l, i: (l, i, 0, 0, 0)),
                  pl.BlockSpec((1, 1, 4, n, ch), lambda l, i: (l, i, 0, 0, 0))],
        out_specs=[mat, mat, mat, pl.BlockSpec((1, 1, 4, n, 128), lambda l, i: (l, i, 0, 0, 0))],
        out_shape=[jax.ShapeDtypeStruct((nl, g, S5_TAP, S5_TAP), BF16)] * 3
        + [jax.ShapeDtypeStruct((nl, g, 4, n, 128), F32)],
        compiler_params=_cparams("parallel", "parallel"),
        name="s5_tables",
    )(xy, pair(bb_re, bb_im), pair(c_re, c_im), pair(jnp.swapaxes(c_re, -1, -2), jnp.swapaxes(c_im, -1, -2)))


def _s5_chunk_lanes(u):
    return u.reshape(S5_NCHUNK, S5_CHUNK, BRANCH).transpose(1, 2, 0)


def _s5_token_rows(y):
    return y.transpose(2, 0, 1).reshape(N_TOK, BRANCH)


def kernel(x_prompt, x_sample, cache_diff_k, cache_diff_v, state_s5, state_hgrn, cache_mla_ckv, cache_mla_krope, c, c_ctx, w_mod, b_mod, w_in, w_out, da_lambda, da_norm, s5_a_re, s5_a_im, s5_log_dt, s5_b_re, s5_b_im, s5_c_re, s5_c_im, s5_d, s5_w_glu, hg_lb, hg_norm, mla_q_norm, mla_w_uq, mla_kv_norm, mla_w_ukv, final_norm):
    lb_w = jax.nn.softmax(hg_lb.astype(F32), axis=0)
    lb_all = jnp.cumsum(lb_w, axis=0) - lb_w[0:1]
    c_rows = jnp.concatenate([c_ctx[None], c, jnp.zeros((8 - 1 - N_LAT_SEQ, D_MODEL), F32)], axis=0)
    mods = _modulation(c_rows, w_mod, b_mod)
    (cos_a, sin_a), mla_tabs = _rope_tables()
    xs = (x_prompt.reshape(N_CTX, D_MODEL), x_sample.reshape(N_LAT, D_MODEL))
    new_k, new_v, new_s5, new_hg, new_ckv, new_kr = [], [], [], [], [], []
    s5_tabs = _s5_tables(s5_a_re, s5_a_im, s5_log_dt, s5_b_re, s5_b_im, s5_c_re, s5_c_im)
    w_abc, w_d = _in_proj_weights(w_in)
    w_out_bf = w_out.astype(BF16)
    for l in range(DEPTH):
        mod = mods[l, :3].reshape(3, 3, D_MODEL)
        z_a, z_b, z_c, z_d = _in_proj(xs, mod, w_abc, w_d, l)

        lam_init = 0.8 - 0.6 * math.exp(-0.3 * l)
        kv_lat = _da_latent_kv(z_a, cos_a, sin_a,
                               cache_diff_k[:, l].reshape(N_LAT_SEQ, PAST_LEN, BRANCH),
                               cache_diff_v[:, l].reshape(N_LAT_SEQ, PAST_LEN, BRANCH))
        a_out = _da_attention(z_a, da_lambda[l], da_norm[l], lam_init, cos_a, sin_a, kv_lat)
        new_k.append(z_a[:N_CTX, 256:512].reshape(N_CTX_SEQ, CTX_LEN, DA_HEADS, 2 * DA_QK))
        new_v.append(z_a[:N_CTX, 512:768].reshape(N_CTX_SEQ, CTX_LEN, DA_HEADS, 2 * DA_QK))

        h0 = state_s5[:, l].transpose(2, 1, 4, 3, 0).reshape(S5_GROUPS, 4, S5_STATE, N_LAT_SEQ)
        h0 = jnp.pad(h0, ((0, 0), (0, 0), (0, 0), (0, 128 - N_LAT_SEQ)))
        y_all, fin = _s5_scan(_s5_chunk_lanes(z_b[:, :BRANCH]).astype(BF16), *s5_tabs, h0, l)
        b_out = _s5_out(_s5_token_rows(y_all), z_b, s5_d[l].reshape(1, BRANCH), s5_w_glu[l].astype(BF16))
        fin = jnp.stack([fin[:, 0:2, :, S5_CTX_SEQ_CH - 1::S5_CTX_SEQ_CH], fin[:, 2:4, :, 0::S5_CTX_SEQ_CH]], axis=1)
        new_s5.append(fin.transpose(4, 1, 0, 3, 2))

        lb = jnp.concatenate([lb_all[l, 0].reshape(HG_HEADS, HG_DK), lb_all[l, 1].reshape(HG_HEADS, HG_DK)],
                             axis=-1).reshape(1, HG_W)
        head_eye = jnp.eye(HG_HEADS, dtype=F32)
        s0 = state_hgrn[:, l].transpose(0, 2, 1, 3, 4).reshape(N_LAT_SEQ, HG_HEADS, HG_HEAD_W, 1, HG_DK)
        s0 = jnp.pad(s0 * head_eye[None, :, None, :, None], ((1, 0), (0, 0), (0, 0), (0, 0), (0, 0)))
        s_f, s_b, s_fin = _hg_states(z_c, lb, s0.reshape(1 + N_LAT_SEQ, HG_HEADS, HG_HEAD_W, BRANCH))
        c_out = _hg_main(z_c, s_f, s_b, lb, jnp.tile(hg_norm[l].reshape(1, HG_DK), (1, HG_HEADS)))
        new_hg.append(s_fin.reshape(N_CTX_SEQ, HG_HEADS, 2, HG_DK, HG_DK).transpose(0, 2, 1, 3, 4))

        wq, wqr, wk, wv, qn = _mla_weights(mla_w_uq[l], mla_w_ukv[l], mla_q_norm[l])
        q, ckv_n, kr = _mla_prep(z_d, mla_tabs, qn, mla_kv_norm[l].reshape(1, MLA_KV_RANK), wq, wqr)
        k_ctx, v_ctx = _mla_kv(ckv_n, kr, wk, wv, N_CTX)
        kr_cache = jnp.pad(cache_mla_krope[:, l], ((0, 0), (0, 0), (MLA_NOPE, 128 - MLA_NOPE - MLA_ROPE)))
        lk = LAT_LEN + PAST_LEN
        ckv_all = jnp.concatenate([ckv_n[N_CTX:].reshape(N_LAT_SEQ, LAT_LEN, 128), cache_mla_ckv[:, l]], axis=1)
        kr_all = jnp.concatenate([kr[N_CTX:].reshape(N_LAT_SEQ, LAT_LEN, 128), kr_cache], axis=1)
        k_lat, v_lat = _mla_kv(ckv_all.reshape(N_LAT_SEQ * lk, 128), kr_all.reshape(N_LAT_SEQ * lk, 128),
                               wk, wv, N_LAT_SEQ * lk)
        d_out = _mla_attention(z_d, q, k_ctx, v_ctx, k_lat, v_lat)
        new_ckv.append(ckv_n[:N_CTX].reshape(N_CTX_SEQ, CTX_LEN, MLA_KV_RANK))
        new_kr.append(kr[:N_CTX, MLA_NOPE:MLA_NOPE + MLA_ROPE].reshape(N_CTX_SEQ, CTX_LEN, MLA_ROPE))

        xs = _out_proj(a_out, b_out, c_out, d_out, xs, mod, w_out_bf, l,
                       final_norm.reshape(1, D_MODEL), final=(l == DEPTH - 1))
        xs = tuple(xs) if l == DEPTH - 1 else (xs,)
    y_prompt = xs[0].reshape(N_CTX_SEQ, CTX_LEN, D_MODEL)
    y_sample = xs[1].reshape(N_LAT_SEQ, LAT_LEN, D_MODEL)
    st = lambda parts: jnp.stack(parts, axis=1)
    return (y_prompt, y_sample, st(new_k), st(new_v), st(new_s5), st(new_hg), st(new_ckv), st(new_kr))
```

```python
import functools
import math

import numpy as np

import jax
import jax.numpy as jnp
from jax import lax
from jax.experimental import pallas as pl
from jax.experimental.pallas import tpu as pltpu

F32 = jnp.float32
BF16 = jnp.bfloat16

D_MODEL = 1024
DEPTH = 2
N_CTX_SEQ = 16
CTX_LEN = 256
N_LAT_SEQ = 2
LAT_LEN = 2048
PAST_LEN = 256
GRID_W = 64
N_CTX = N_CTX_SEQ * CTX_LEN
N_LAT = N_LAT_SEQ * LAT_LEN
N_TOK = N_CTX + N_LAT
BRANCH = 256
EPS = 1e-6
ROPE_BASE = 10000.0
ROW_TILE = 256
LAT_TILES = LAT_LEN // ROW_TILE
N_TILES = N_TOK // ROW_TILE
CTX_TILES = N_CTX // ROW_TILE
VMEM_LIMIT = 48 * 1024 * 1024
LAT_Q_TILE = 512
LAT_Q_TILES = LAT_LEN // LAT_Q_TILE

DA_HEADS = 4
DA_QK = 32
MLA_HEADS = 4
MLA_NOPE = 64
MLA_ROPE = 32
MLA_Q_RANK = 192
MLA_KV_RANK = 128
S5_GROUPS = 16
S5_CH = 16
S5_STATE = 64
S5_CHUNK = 16
HG_HEADS = 4
HG_DK = 64

W_A = 1536
W_B = 512
W_C = 1536
W_D = 896
W_ABC = W_A + W_B + W_C


def _cparams(*sem):
    return pltpu.CompilerParams(dimension_semantics=sem, vmem_limit_bytes=VMEM_LIMIT)


def _tile_seq(i):
    return jnp.where(i < CTX_TILES, 0, 1 + (i - CTX_TILES) // LAT_TILES)


def _silu(x):
    return x * (1.0 / (1.0 + jnp.exp(-x)))


def _dot(a, b):
    return jnp.dot(a, b, preferred_element_type=F32)


def _dot_nt(a, b):
    return lax.dot_general(a, b, (((1,), (1,)), ((), ())), preferred_element_type=F32)


def _mod_body(c_ref, w_ref, b_ref, o_ref):
    c = _silu(c_ref[...]).astype(BF16)
    o_ref[0] = _dot(c, w_ref[0].astype(BF16)) + b_ref[0]


def _modulation(c_rows, w_mod, b_mod):
    tn = 768
    return pl.pallas_call(
        _mod_body,
        grid=(DEPTH, 3 * D_MODEL // tn),
        in_specs=[pl.BlockSpec((8, D_MODEL), lambda l, j: (0, 0)),
                  pl.BlockSpec((1, D_MODEL, tn), lambda l, j: (l, 0, j)),
                  pl.BlockSpec((1, 1, tn), lambda l, j: (l, 0, j))],
        out_specs=pl.BlockSpec((1, 8, tn), lambda l, j: (l, 0, j)),
        out_shape=jax.ShapeDtypeStruct((DEPTH, 8, 3 * D_MODEL), F32),
        compiler_params=_cparams("parallel", "parallel"),
        name="modulation",
    )(c_rows, w_mod, b_mod.reshape(DEPTH, 1, 3 * D_MODEL))


def _split_rows(i, ctx_ref, lat_ref):
    return jnp.where(i < CTX_TILES, ctx_ref[...], lat_ref[...])


def _ctx_tile_spec(w):
    return pl.BlockSpec((ROW_TILE, w), lambda i: (jnp.minimum(i, CTX_TILES - 1), 0))


def _lat_tile_spec(w):
    return pl.BlockSpec((ROW_TILE, w), lambda i: (jnp.maximum(i - CTX_TILES, 0), 0))


def _in_proj_body(*refs, split):
    if split:
        xc_ref, xl_ref, mod_ref, w_ref, wd_ref, oa, ob, oc, od = refs
        x = _split_rows(pl.program_id(0), xc_ref, xl_ref)
    else:
        x_ref, mod_ref, w_ref, wd_ref, oa, ob, oc, od = refs
        x = x_ref[...]
    xn = x * lax.rsqrt(jnp.mean(x * x, axis=-1, keepdims=True) + EPS)
    mod = mod_ref[0]
    h = (xn * (1.0 + mod[1:2]) + mod[0:1]).astype(BF16)
    off = 0
    for o in (oa, ob, oc):
        w = o.shape[-1]
        o[...] = _dot(h, w_ref[0, :, off:off + w])
        off += w
    od[...] = _dot(h, wd_ref[0])


def _in_proj(xs, mod, w_abc, w_d, l):
    widths = (W_A, W_B, W_C, W_D)
    split = len(xs) == 2
    x_specs = ([_ctx_tile_spec(D_MODEL), _lat_tile_spec(D_MODEL)] if split
               else [pl.BlockSpec((ROW_TILE, D_MODEL), lambda i: (i, 0))])
    return pl.pallas_call(
        functools.partial(_in_proj_body, split=split),
        grid=(N_TILES,),
        in_specs=x_specs + [pl.BlockSpec((1, 3, D_MODEL), lambda i: (_tile_seq(i), 0, 0)),
                            pl.BlockSpec((1, D_MODEL, W_ABC), lambda i: (l, 0, 0)),
                            pl.BlockSpec((1, D_MODEL, W_D), lambda i: (l, 0, 0))],
        out_specs=[pl.BlockSpec((ROW_TILE, w), lambda i: (i, 0)) for w in widths],
        out_shape=[jax.ShapeDtypeStruct((N_TOK, w), F32) for w in widths],
        compiler_params=_cparams("parallel"),
        name="in_proj",
    )(*xs, mod, w_abc, w_d)


def _arrange_body(w_ref, prot_ref, pdup_ref, pint_ref, o_ref):
    w = w_ref[0].astype(BF16)

    def perm(x, p_ref):
        return _dot(x, p_ref[...]).astype(BF16)
    o_ref[0] = jnp.concatenate(
        [w[:, 0:1024], perm(w[:, 0:256], prot_ref), perm(w[:, 256:512], prot_ref), w[:, 1024:1536],
         perm(w[:, 1536:1792], pdup_ref), perm(w[:, 1792:2304], pint_ref), w[:, 2304:2816]], axis=-1)


def _arrange_w_in(w_in, prot, pdup, pint):
    rows = 128
    const = lambda a: pl.BlockSpec(a.shape, lambda l, i: (0, 0))
    return pl.pallas_call(
        _arrange_body,
        grid=(DEPTH, D_MODEL // rows),
        in_specs=[pl.BlockSpec((1, rows, w_in.shape[-1]), lambda l, i: (l, i, 0)), const(prot), const(pdup), const(pint)],
        out_specs=pl.BlockSpec((1, rows, W_ABC), lambda l, i: (l, i, 0)),
        out_shape=jax.ShapeDtypeStruct((DEPTH, D_MODEL, W_ABC), BF16),
        compiler_params=_cparams("parallel", "parallel"),
        name="arrange_w_in",
    )(w_in, prot, pdup, pint)


def _out_proj_body(*refs, split_in, final):
    ac_ref, al_ref, b_ref, c_ref, dc_ref, dl_ref = refs[:6]
    i = pl.program_id(0)
    if split_in:
        xc_ref, xl_ref, mod_ref, w_ref, fn_ref = refs[6:11]
        x = _split_rows(i, xc_ref, xl_ref)
    else:
        x_ref, mod_ref, w_ref, fn_ref = refs[6:10]
        x = x_ref[...]
    branches = (_split_rows(i, ac_ref, al_ref), b_ref[...], c_ref[...], _split_rows(i, dc_ref, dl_ref))
    acc = None
    for j, r in enumerate(branches):
        t = _dot(r.astype(BF16), w_ref[0, j * BRANCH:(j + 1) * BRANCH, :])
        acc = t if acc is None else acc + t
    x = x + mod_ref[0][2:3] * acc
    if not final:
        refs[-1][...] = x
        return
    y = x * lax.rsqrt(jnp.mean(x * x, axis=-1, keepdims=True) + EPS) * fn_ref[...]
    yc_ref, yl_ref = refs[-2:]

    @pl.when(i < CTX_TILES)
    def _():
        yc_ref[...] = y

    @pl.when(i >= CTX_TILES)
    def _():
        yl_ref[...] = y


def _out_proj(a, b, c, d, xs, mod, w_out, l, final_norm, final):
    br = pl.BlockSpec((ROW_TILE, BRANCH), lambda i: (i, 0))
    pair = [_ctx_tile_spec(BRANCH), _lat_tile_spec(BRANCH)]
    split_in = len(xs) == 2
    x_specs = ([_ctx_tile_spec(D_MODEL), _lat_tile_spec(D_MODEL)] if split_in
               else [pl.BlockSpec((ROW_TILE, D_MODEL), lambda i: (i, 0))])
    if final:
        out_specs = [_ctx_tile_spec(D_MODEL), _lat_tile_spec(D_MODEL)]
        out_shape = [jax.ShapeDtypeStruct((N_CTX, D_MODEL), F32), jax.ShapeDtypeStruct((N_LAT, D_MODEL), F32)]
    else:
        out_specs = pl.BlockSpec((ROW_TILE, D_MODEL), lambda i: (i, 0))
        out_shape = jax.ShapeDtypeStruct((N_TOK, D_MODEL), F32)
    return pl.pallas_call(
        functools.partial(_out_proj_body, split_in=split_in, final=final),
        grid=(N_TILES,),
        in_specs=pair + [br, br] + pair + x_specs + [
            pl.BlockSpec((1, 3, D_MODEL), lambda i: (_tile_seq(i), 0, 0)),
            pl.BlockSpec((1, D_MODEL, D_MODEL), lambda i: (l, 0, 0)),
            pl.BlockSpec((1, D_MODEL), lambda i: (0, 0))],
        out_specs=out_specs,
        out_shape=out_shape,
        compiler_params=_cparams("arbitrary"),
        name="out_proj",
    )(*a, b, c, *d, *xs, mod, w_out, final_norm)


LOG2E = 1.4426950408889634


def _exp2_rows(s):
    e = jnp.exp2(s - jnp.max(s, axis=-1, keepdims=True))
    return e, jnp.sum(e, axis=-1, keepdims=True)


def _da_kv_body(k_ref, kr_ref, v_ref, cos_ref, sin_ref, ck_ref, cv_ref, ko_ref, vo_ref):
    j = pl.program_id(1)

    @pl.when(j < LAT_TILES)
    def _():
        ko_ref[0] = (k_ref[...] * cos_ref[...] + kr_ref[...] * sin_ref[...]).astype(BF16)
        vo_ref[0] = v_ref[...].astype(BF16)

    @pl.when(j == LAT_TILES)
    def _():
        ko_ref[0] = ck_ref[0].astype(BF16)
        vo_ref[0] = cv_ref[0].astype(BF16)


def _da_latent_kv(z_a, cos, sin, cache_k, cache_v):
    def rows(col):
        return pl.BlockSpec(
            (ROW_TILE, BRANCH),
            lambda b, j: (CTX_TILES + b * LAT_TILES + jnp.minimum(j, LAT_TILES - 1), col))
    tab = pl.BlockSpec((ROW_TILE, BRANCH), lambda b, j: (jnp.minimum(j, LAT_TILES - 1), 0))
    cache = pl.BlockSpec((1, PAST_LEN, BRANCH), lambda b, j: (b, 0, 0))
    out = pl.BlockSpec((1, ROW_TILE, BRANCH), lambda b, j: (b, j, 0))
    shp = jax.ShapeDtypeStruct((N_LAT_SEQ, LAT_LEN + PAST_LEN, BRANCH), BF16)
    return pl.pallas_call(
        _da_kv_body,
        grid=(N_LAT_SEQ, LAT_TILES + 1),
        in_specs=[rows(1), rows(5), rows(2), tab, tab, cache, cache],
        out_specs=[out, out],
        out_shape=[shp, shp],
        compiler_params=_cparams("parallel", "parallel"),
        name="da_kv",
    )(z_a, z_a, z_a, cos, sin, cache_k, cache_v)


def _da_attn_body(lam_ref, ng_ref, q_ref, *rest, rope, lam_init):
    if rope:
        qr_ref, cos_ref, sin_ref, k_ref, v_ref, g_ref, o_ref = rest
        q = q_ref[...] * cos_ref[...] + qr_ref[...] * sin_ref[...]
        k = k_ref[0]
        v = v_ref[0]
    else:
        k_ref, v_ref, g_ref, o_ref = rest
        q = q_ref[...]
        k = k_ref[...].astype(BF16)
        v = v_ref[...].astype(BF16)
    q = q * (DA_QK ** -0.5 * LOG2E)
    lv = lam_ref[...]
    lam = (jnp.exp(jnp.sum(lv[0:1] * lv[1:2], axis=-1, keepdims=True))
           - jnp.exp(jnp.sum(lv[2:3] * lv[3:4], axis=-1, keepdims=True)) + lam_init)
    lane = lax.broadcasted_iota(jnp.int32, (1, BRANCH), 1)
    acc = jnp.zeros(q.shape, F32)
    for h in range(DA_HEADS):
        q1 = jnp.where(lane // DA_QK == 2 * h, q, 0.0).astype(BF16)
        q2 = jnp.where(lane // DA_QK == 2 * h + 1, q, 0.0).astype(BF16)
        e1, l1 = _exp2_rows(_dot_nt(q1, k))
        e2, l2 = _exp2_rows(_dot_nt(q2, k))
        a = (e1 - (lam * l1 / l2) * e2).astype(BF16)
        acc = jnp.where(lane // (2 * DA_QK) == h, _dot(a, v) * (1.0 / l1), acc)
    sq = acc * acc
    ms = jnp.zeros(q.shape, F32)
    for h in range(DA_HEADS):
        hm = lane // (2 * DA_QK) == h
        ms = jnp.where(hm, jnp.sum(jnp.where(hm, sq, 0.0), axis=-1, keepdims=True), ms)
    o = acc * lax.rsqrt(ms * (1.0 / (2 * DA_QK)) + EPS) * (ng_ref[...] * (1.0 - lam_init))
    o_ref[...] = o * _silu(g_ref[...])


def _da_attention(z_a, lam_vec, norm_g, lam_init, cos, sin, kv_lat):
    ng = jnp.tile(norm_g.reshape(1, 2 * DA_QK), (1, DA_HEADS))
    small = [pl.BlockSpec((4, DA_QK), lambda *_: (0, 0)), pl.BlockSpec((1, BRANCH), lambda *_: (0, 0))]

    def col(c):
        return pl.BlockSpec((ROW_TILE, BRANCH), lambda i: (i, c))
    ctx = pl.pallas_call(
        functools.partial(_da_attn_body, rope=False, lam_init=lam_init),
        grid=(CTX_TILES,),
        in_specs=small + [col(0), col(1), col(2), col(3)],
        out_specs=pl.BlockSpec((ROW_TILE, BRANCH), lambda i: (i, 0)),
        out_shape=jax.ShapeDtypeStruct((N_CTX, BRANCH), F32),
        compiler_params=_cparams("parallel"),
        name="da_attn_ctx",
    )(lam_vec, ng, z_a, z_a, z_a, z_a)

    def lcol(c):
        return pl.BlockSpec((LAT_Q_TILE, BRANCH), lambda b, j: (N_CTX // LAT_Q_TILE + b * LAT_Q_TILES + j, c))
    tab = pl.BlockSpec((LAT_Q_TILE, BRANCH), lambda b, j: (j, 0))
    kvs = pl.BlockSpec((1, LAT_LEN + PAST_LEN, BRANCH), lambda b, j: (b, 0, 0))
    lat = pl.pallas_call(
        functools.partial(_da_attn_body, rope=True, lam_init=lam_init),
        grid=(N_LAT_SEQ, LAT_Q_TILES),
        in_specs=small + [lcol(0), lcol(4), tab, tab, kvs, kvs, lcol(3)],
        out_specs=pl.BlockSpec((LAT_Q_TILE, BRANCH), lambda b, j: (b * LAT_Q_TILES + j, 0)),
        out_shape=jax.ShapeDtypeStruct((N_LAT, BRANCH), F32),
        compiler_params=_cparams("parallel", "parallel"),
        name="da_attn_lat",
    )(lam_vec, ng, z_a, z_a, cos, sin, kv_lat[0], kv_lat[1], z_a)
    return ctx, lat


MLA_HEAD_PAD = 128
MLA_QW = MLA_HEADS * MLA_HEAD_PAD


def _mla_prep_body(cq_ref, ckv_ref, kr_ref, krr_ref, cq_t, sq_t, ck_t, sk_t, qn_ref, kvn_ref, wq_ref, wqr_ref,
                   q_out, ckv_out, kr_out):
    cq = cq_ref[...]
    ms = jnp.sum(cq * cq, axis=-1, keepdims=True) * (1.0 / MLA_Q_RANK)
    qn = (cq * lax.rsqrt(ms + EPS) * qn_ref[...]).astype(BF16)
    q = _dot(qn, wq_ref[...]) * cq_t[...] + _dot(qn, wqr_ref[...]) * sq_t[...]
    q_out[...] = (q * ((MLA_NOPE + MLA_ROPE) ** -0.5 * LOG2E)).astype(BF16)
    ckv = ckv_ref[...]
    ckv_out[...] = ckv * lax.rsqrt(jnp.mean(ckv * ckv, axis=-1, keepdims=True) + EPS) * kvn_ref[...]
    kr_out[...] = kr_ref[...] * ck_t[...] + krr_ref[...] * sk_t[...]


def _mla_prep(z_d, tabs, q_norm_pad, kv_norm, wq, wqr):
    def tab(w):
        return pl.BlockSpec(
            (ROW_TILE, w), lambda i: (jnp.where(i < CTX_TILES, LAT_TILES, (i - CTX_TILES) % LAT_TILES), 0))

    def col(w, c):
        return pl.BlockSpec((ROW_TILE, w), lambda i: (i, c))

    def const(shape):
        return pl.BlockSpec(shape, lambda i: (0, 0))
    return pl.pallas_call(
        _mla_prep_body,
        grid=(N_TILES,),
        in_specs=[col(256, 0), col(128, 2), col(128, 3), col(128, 6),
                  tab(MLA_QW), tab(MLA_QW), tab(128), tab(128),
                  const((1, 256)), const((1, 128)), const((256, MLA_QW)), const((256, MLA_QW))],
        out_specs=[col(MLA_QW, 0), col(128, 0), col(128, 0)],
        out_shape=[jax.ShapeDtypeStruct((N_TOK, MLA_QW), BF16),
                   jax.ShapeDtypeStruct((N_TOK, 128), F32),
                   jax.ShapeDtypeStruct((N_TOK, 128), F32)],
        compiler_params=_cparams("parallel"),
        name="mla_prep",
    )(z_d, z_d, z_d, z_d, *tabs, q_norm_pad, kv_norm, wq, wqr)


def _mla_kv_body(ckv_ref, kr_ref, wk_ref, wv_ref, k_out, v_out):
    c = ckv_ref[...].astype(BF16)
    kr = kr_ref[...]
    k_out[...] = (_dot(c, wk_ref[...]) + jnp.concatenate([kr] * MLA_HEADS, axis=-1)).astype(BF16)
    v_out[...] = _dot(c, wv_ref[...]).astype(BF16)


def _mla_kv(ckv, kr, wk, wv, n_rows):
    return pl.pallas_call(
        _mla_kv_body,
        grid=(n_rows // ROW_TILE,),
        in_specs=[pl.BlockSpec((ROW_TILE, 128), lambda i: (i, 0)),
                  pl.BlockSpec((ROW_TILE, 128), lambda i: (i, 0)),
                  pl.BlockSpec((128, MLA_QW), lambda i: (0, 0)),
                  pl.BlockSpec((128, BRANCH), lambda i: (0, 0))],
        out_specs=[pl.BlockSpec((ROW_TILE, MLA_QW), lambda i: (i, 0)),
                   pl.BlockSpec((ROW_TILE, BRANCH), lambda i: (i, 0))],
        out_shape=[jax.ShapeDtypeStruct((n_rows, MLA_QW), BF16),
                   jax.ShapeDtypeStruct((n_rows, BRANCH), BF16)],
        compiler_params=_cparams("parallel"),
        name="mla_kv",
    )(ckv, kr, wk, wv)


def _mla_attn_body(q_ref, k_ref, v_ref, g_ref, o_ref):
    q = q_ref[...]
    k = k_ref[...].reshape(-1, MLA_QW)
    v = v_ref[...].reshape(-1, BRANCH)
    lane = lax.broadcasted_iota(jnp.int32, (1, BRANCH), 1)
    acc = jnp.zeros((q.shape[0], BRANCH), F32)
    for h in range(MLA_HEADS):
        sl = slice(h * MLA_HEAD_PAD, (h + 1) * MLA_HEAD_PAD)
        e, l = _exp2_rows(_dot_nt(q[:, sl], k[:, sl]))
        acc = jnp.where(lane // 64 == h, _dot(e.astype(BF16), v) * (1.0 / l), acc)
    o_ref[...] = acc * _silu(g_ref[...])


def _mla_attention(z_d, q, k_ctx, v_ctx, k_lat, v_lat):
    ctx = pl.pallas_call(
        _mla_attn_body,
        grid=(CTX_TILES,),
        in_specs=[pl.BlockSpec((ROW_TILE, MLA_QW), lambda i: (i, 0)),
                  pl.BlockSpec((ROW_TILE, MLA_QW), lambda i: (i, 0)),
                  pl.BlockSpec((ROW_TILE, BRANCH), lambda i: (i, 0)),
                  pl.BlockSpec((ROW_TILE, BRANCH), lambda i: (i, 2))],
        out_specs=pl.BlockSpec((ROW_TILE, BRANCH), lambda i: (i, 0)),
        out_shape=jax.ShapeDtypeStruct((N_CTX, BRANCH), F32),
        compiler_params=_cparams("parallel"),
        name="mla_attn_ctx",
    )(q, k_ctx, v_ctx, z_d)
    lk = LAT_LEN + PAST_LEN
    lat = pl.pallas_call(
        _mla_attn_body,
        grid=(N_LAT_SEQ, LAT_Q_TILES),
        in_specs=[pl.BlockSpec((LAT_Q_TILE, MLA_QW), lambda b, j: (N_CTX // LAT_Q_TILE + b * LAT_Q_TILES + j, 0)),
                  pl.BlockSpec((1, lk, MLA_QW), lambda b, j: (b, 0, 0)),
                  pl.BlockSpec((1, lk, BRANCH), lambda b, j: (b, 0, 0)),
                  pl.BlockSpec((LAT_Q_TILE, BRANCH), lambda b, j: (N_CTX // LAT_Q_TILE + b * LAT_Q_TILES + j, 2))],
        out_specs=pl.BlockSpec((LAT_Q_TILE, BRANCH), lambda b, j: (b * LAT_Q_TILES + j, 0)),
        out_shape=jax.ShapeDtypeStruct((N_LAT, BRANCH), F32),
        compiler_params=_cparams("parallel", "parallel"),
        name="mla_attn_lat",
    )(q, k_lat.reshape(N_LAT_SEQ, lk, MLA_QW), v_lat.reshape(N_LAT_SEQ, lk, BRANCH), z_d)
    return ctx, lat


S5_TAP = S5_CHUNK * S5_CH
S5_NCHUNK = N_TOK // S5_CHUNK
S5_CTX_CH = N_CTX // S5_CHUNK
S5_CTX_SEQ_CH = CTX_LEN // S5_CHUNK
S5_LAT_SEQ_CH = LAT_LEN // S5_CHUNK
S5_SCAN_STEPS = S5_LAT_SEQ_CH.bit_length() - 1


def _s5_body(x_ref, mt_ref, bst_ref, cot_ref, a_ref, h0_ref, y_ref, fin_ref):
    x = x_ref[...].reshape(S5_TAP, S5_NCHUNK)
    y = _dot(mt_ref[0, 0], x)
    s = _dot(bst_ref[0, 0], x)
    lane = lax.broadcasted_iota(jnp.int32, (1, S5_NCHUNK), 1)
    is_lat = lane >= S5_CTX_CH
    pos_f = jnp.where(is_lat, (lane - S5_CTX_CH) & (S5_LAT_SEQ_CH - 1), lane & (S5_CTX_SEQ_CH - 1))
    pos_b = jnp.where(is_lat, S5_LAT_SEQ_CH - 1, S5_CTX_SEQ_CH - 1) - pos_f
    hin = []
    for d in range(2):
        n = S5_STATE
        sre, sim = s[2 * d * n:(2 * d + 1) * n], s[(2 * d + 1) * n:(2 * d + 2) * n]
        are = jnp.concatenate([a_ref[0, 0, 2 * d]] * (S5_NCHUNK // 128), axis=-1)
        aim = jnp.concatenate([a_ref[0, 0, 2 * d + 1]] * (S5_NCHUNK // 128), axis=-1)
        pos = pos_f if d == 0 else pos_b
        h0r, h0i = jnp.zeros_like(sre), jnp.zeros_like(sre)
        for b in range(N_LAT_SEQ):
            first = S5_CTX_CH + b * S5_LAT_SEQ_CH + (0 if d == 0 else S5_LAT_SEQ_CH - 1)
            h0r = jnp.where(lane == first, h0_ref[0, 2 * d][:, b:b + 1], h0r)
            h0i = jnp.where(lane == first, h0_ref[0, 2 * d + 1][:, b:b + 1], h0i)
        xr = sre + are * h0r - aim * h0i
        xi = sim + are * h0i + aim * h0r
        pr, pi = are, aim
        for j in range(S5_SCAN_STEPS):
            sh = 1 << j
            shift = sh if d == 0 else S5_NCHUNK - sh
            rr, ri = pltpu.roll(xr, shift, 1), pltpu.roll(xi, shift, 1)
            ok = pos >= sh
            xr, xi = (xr + jnp.where(ok, pr * rr - pi * ri, 0.0), xi + jnp.where(ok, pr * ri + pi * rr, 0.0))
            pr, pi = pr * pr - pi * pi, 2.0 * pr * pi
        fin_ref[0, 2 * d] = xr[:, :S5_CTX_CH]
        fin_ref[0, 2 * d + 1] = xi[:, :S5_CTX_CH]
        one = 1 if d == 0 else S5_NCHUNK - 1
        hin.append(jnp.where(pos >= 1, pltpu.roll(xr, one, 1), h0r))
        hin.append(jnp.where(pos >= 1, pltpu.roll(xi, one, 1), h0i))
    y = y + _dot(cot_ref[0, 0], jnp.concatenate(hin, axis=0).astype(BF16))
    y_ref[...] = y.reshape(S5_CHUNK, S5_CH, S5_NCHUNK)


def _s5_scan(x_all, mt, bst, cot, a16, h0, l):
    g = S5_GROUPS
    sq = pl.BlockSpec((1, 1, S5_TAP, S5_TAP), lambda i: (l, i, 0, 0))
    st = pl.BlockSpec((1, 4, S5_STATE, 128), lambda i: (i, 0, 0, 0))
    return pl.pallas_call(
        _s5_body,
        grid=(g,),
        in_specs=[pl.BlockSpec((S5_CHUNK, S5_CH, S5_NCHUNK), lambda i: (0, i, 0)), sq, sq, sq,
                  pl.BlockSpec((1, 1, 4, S5_STATE, 128), lambda i: (l, i, 0, 0, 0)), st],
        out_specs=[pl.BlockSpec((S5_CHUNK, S5_CH, S5_NCHUNK), lambda i: (0, i, 0)),
                   pl.BlockSpec((1, 4, S5_STATE, S5_CTX_CH), lambda i: (i, 0, 0, 0))],
        out_shape=[jax.ShapeDtypeStruct((S5_CHUNK, BRANCH, S5_NCHUNK), F32),
                   jax.ShapeDtypeStruct((g, 4, S5_STATE, S5_CTX_CH), F32)],
        compiler_params=_cparams("parallel"),
        name="s5_scan",
    )(x_all, mt, bst, cot, a16, h0)


def _s5_out_body(y_ref, u_ref, g_ref, d_ref, w_ref, o_ref):
    y = u_ref[...] * d_ref[...] + y_ref[...]
    ge = 0.5 * y * (1.0 + jnp.tanh(0.7978845608028654 * (y + 0.044715 * (y * y * y))))
    gl = _dot(ge.astype(BF16), w_ref[...])
    o_ref[...] = gl[:, :BRANCH] * (1.0 / (1.0 + jnp.exp(-gl[:, BRANCH:]))) * _silu(g_ref[...])


def _s5_out(y_ssm, z_b, d_skip, w_glu):
    def col(c):
        return pl.BlockSpec((ROW_TILE, BRANCH), lambda i: (i, c))
    return pl.pallas_call(
        _s5_out_body,
        grid=(N_TILES,),
        in_specs=[col(0), col(0), col(1),
                  pl.BlockSpec((1, BRANCH), lambda i: (0, 0)),
                  pl.BlockSpec((BRANCH, 2 * BRANCH), lambda i: (0, 0))],
        out_specs=col(0),
        out_shape=jax.ShapeDtypeStruct((N_TOK, BRANCH), F32),
        compiler_params=_cparams("parallel"),
        name="s5_out",
    )(y_ssm, z_b, z_b, d_skip, w_glu)


HG_CHUNK = ROW_TILE
HG_W = 2 * HG_HEADS * HG_DK
HG_HEAD_W = 2 * HG_DK
HG_LAT_CHUNKS = LAT_LEN // HG_CHUNK
HG_CHUNKS = N_TOK // HG_CHUNK


def _hg_gates(z, lb):
    e = jnp.exp(-jnp.abs(z))
    r = 1.0 / (1.0 + e)
    sig_pos = jnp.where(z >= 0, r, e * r)
    sig_neg = jnp.where(z >= 0, e * r, r)
    return lb + (1.0 - lb) * sig_pos, (1.0 - lb) * sig_neg


def _bcast_row(x, period, r):
    n, w = x.shape
    if period >= 8:
        x3 = x.reshape(n // period, period, w)
        return jnp.broadcast_to(x3[:, r:r + 1, :], x3.shape).reshape(n, w)
    x3 = x.reshape(n // 8, 8, w)
    sub = lax.broadcasted_iota(jnp.int32, (1, 8, 1), 1)
    out = None
    for j in range(8 // period):
        b = jnp.broadcast_to(x3[:, j * period + r:j * period + r + 1, :], x3.shape)
        out = b if out is None else jnp.where(sub >= j * period, b, out)
    return out.reshape(n, w)


def _hg_scans(f, isb):
    n = f.shape[0]
    row = lax.broadcasted_iota(jnp.int32, (n, 1), 0)
    p, r = f, jnp.ones_like(f)
    levels = []
    h, sh = 1, 0
    while h < n:
        levels.append((h, sh, p, r))
        up = (row >> sh) & 1
        tot_p = jnp.where(isb == 1, _bcast_row(p, 2 * h, h), _bcast_row(p, 2 * h, h - 1))
        tot_r = jnp.where(isb == 1, _bcast_row(p, 2 * h, 0), _bcast_row(p, 2 * h, 2 * h - 1))
        p = p * jnp.where(up != isb, tot_p, 1.0)
        r = r * jnp.where(up == isb, tot_r, 1.0)
        h, sh = 2 * h, sh + 1
    return levels, p, r


def _hg_state_body(zf_ref, zb_ref, vf_ref, vb_ref, lb_ref, s0_ref, sf_out, sb_out, s_scr):
    i = pl.program_id(0)

    @pl.when(i % HG_LAT_CHUNKS == 0)
    def _():
        s_scr[...] = s0_ref[0]

    sf_out[0] = s_scr[:, 0:HG_DK, :]
    sb_out[0] = s_scr[:, HG_DK:, :]
    lane5 = lax.broadcasted_iota(jnp.int32, (1, HG_W), 1)
    isb = (lane5 >> 6) & 1
    z = jnp.where(isb == 1, zb_ref[...], zf_ref[...])
    f, k = _hg_gates(z, lb_ref[...])
    r, ptot = _hg_chunk_decay(f, isb)
    kt = k * r
    lane = lax.broadcasted_iota(jnp.int32, (1, BRANCH), 1)
    vf = vf_ref[...]
    vb = vb_ref[...]
    for hd in range(HG_HEADS):
        sl = slice(hd * HG_HEAD_W, (hd + 1) * HG_HEAD_W)
        kth = kt[:, sl].T.astype(BF16)
        hm = (lane >> 6) == hd
        d_f = _dot(kth, jnp.where(hm, vf, 0.0).astype(BF16))
        d_b = _dot(kth, jnp.where(hm, vb, 0.0).astype(BF16))
        ds = jnp.concatenate([d_f[:HG_DK], d_b[HG_DK:]], axis=0)
        pcol = jnp.broadcast_to(ptot[:, sl], (HG_HEAD_W, HG_HEAD_W)).T[:, 0:1]
        s_scr[hd] = s_scr[hd] * pcol + ds


def _hg_chunk_decay(f, isb):
    n = f.shape[0]
    row = lax.broadcasted_iota(jnp.int32, (n, 1), 0)
    dist = jnp.where(isb == 1, row, n - 1 - row)
    x = f
    sh = 1
    while sh < n:
        src = jnp.where(isb == 1, pltpu.roll(x, sh, 0), pltpu.roll(x, n - sh, 0))
        x = x * jnp.where(dist >= sh, src, 1.0)
        sh *= 2
    total = jnp.where(isb == 1, x[n - 1:n], x[0:1])
    nxt = jnp.where(isb == 1, pltpu.roll(x, 1, 0), pltpu.roll(x, n - 1, 0))
    return jnp.where(dist >= 1, nxt, 1.0), total


HG_LAT_STEPS = N_LAT_SEQ * HG_LAT_CHUNKS


def _hg_lat_rev(i):
    return (i // HG_LAT_CHUNKS) * HG_LAT_CHUNKS + (HG_LAT_CHUNKS - 1 - i % HG_LAT_CHUNKS)


def _hg_states(z_c, lb, s0):
    first = N_CTX_SEQ
    zz_f = pl.BlockSpec((HG_CHUNK, HG_W), lambda i: (first + i, 1))
    zz_b = pl.BlockSpec((HG_CHUNK, HG_W), lambda i: (first + _hg_lat_rev(i), 1))
    v_f = pl.BlockSpec((HG_CHUNK, BRANCH), lambda i: (first + i, 4))
    v_b = pl.BlockSpec((HG_CHUNK, BRANCH), lambda i: (first + _hg_lat_rev(i), 4))
    st = (HG_HEADS, HG_HEAD_W, BRANCH)
    half = (HG_HEADS, HG_DK, BRANCH)
    return pl.pallas_call(
        _hg_state_body,
        grid=(HG_LAT_STEPS,),
        in_specs=[zz_f, zz_b, v_f, v_b,
                  pl.BlockSpec((1, HG_W), lambda i: (0, 0)),
                  pl.BlockSpec((1,) + st, lambda i: (i // HG_LAT_CHUNKS, 0, 0, 0))],
        out_specs=[pl.BlockSpec((1,) + half, lambda i: (i, 0, 0, 0)),
                   pl.BlockSpec((1,) + half, lambda i: (_hg_lat_rev(i), 0, 0, 0))],
        out_shape=[jax.ShapeDtypeStruct((HG_LAT_STEPS,) + half, F32),
                   jax.ShapeDtypeStruct((HG_LAT_STEPS,) + half, F32)],
        scratch_shapes=[pltpu.VMEM(st, F32)],
        compiler_params=_cparams("arbitrary"),
        name="hg_states",
    )(z_c, z_c, z_c, z_c, lb, s0)


def _hg_main_body(qq_ref, zz_ref, v_ref, g_ref, sf_ref, sb_ref, lb_ref, ng_ref, o_ref, fin_ref):
    n = HG_CHUNK
    i = pl.program_id(0)
    qq = qq_ref[...]
    lane5 = lax.broadcasted_iota(jnp.int32, (1, HG_W), 1)
    isb = (lane5 >> 6) & 1
    f, k = _hg_gates(zz_ref[...], lb_ref[...])
    levels, pfull, rfull = _hg_scans(f, isb)
    row = lax.broadcasted_iota(jnp.int32, (n, 1), 0)
    col = lax.broadcasted_iota(jnp.int32, (1, n), 1)
    ops = [(qq.astype(BF16), k.astype(BF16), row == col)]
    for h, sh, p, r in levels:
        up = (row >> sh) & 1
        qt = jnp.where(up != isb, qq * p, 0.0).astype(BF16)
        kt = jnp.where(up == isb, k * r, 0.0).astype(BF16)
        ops.append((qt, kt, (row >> (sh + 1)) == (col >> (sh + 1))))
    qc = (qq * pfull).astype(BF16)
    v = v_ref[...]
    vb = v.astype(BF16)
    lane = lax.broadcasted_iota(jnp.int32, (1, BRANCH), 1)
    latent = i >= N_CTX_SEQ
    acc = jnp.zeros((n, BRANCH), F32)
    for hd in range(HG_HEADS):
        sl = slice(hd * HG_HEAD_W, (hd + 1) * HG_HEAD_W)
        a = jnp.zeros((n, n), F32)
        for qt, kt, mask in ops:
            a = a + jnp.where(mask, _dot_nt(qt[:, sl], kt[:, sl]), 0.0)
        s_in = jnp.concatenate([sf_ref[0, hd], sb_ref[0, hd]], axis=0)
        s_in = jnp.where(latent, s_in, 0.0).astype(BF16)
        o_h = _dot(a.astype(BF16), vb) + _dot(qc[:, sl], s_in)
        acc = jnp.where((lane >> 6) == hd, o_h, acc)
    sq = acc * acc
    ms = jnp.zeros((n, BRANCH), F32)
    for hd in range(HG_HEADS):
        hm = (lane >> 6) == hd
        ms = jnp.where(hm, jnp.sum(jnp.where(hm, sq, 0.0), axis=-1, keepdims=True), ms)
    o_ref[...] = acc * lax.rsqrt(ms * (1.0 / HG_DK) + EPS) * ng_ref[...] * _silu(g_ref[...])

    @pl.when(i < N_CTX_SEQ)
    def _():
        kt_full = k * rfull
        for hd in range(HG_HEADS):
            kth = kt_full[:, hd * HG_HEAD_W:(hd + 1) * HG_HEAD_W].T.astype(BF16)
            ds = _dot(kth, jnp.where((lane >> 6) == hd, v, 0.0).astype(BF16))
            fin_ref[0, hd] = ds[:, hd * HG_DK:(hd + 1) * HG_DK]


def _hg_main(z_c, s_f, s_b, lb, norm_g):
    half = (1, HG_HEADS, HG_DK, BRANCH)
    lat = lambda i: (jnp.maximum(i - N_CTX_SEQ, 0), 0, 0, 0)
    fin = (HG_HEADS, HG_HEAD_W, HG_DK)
    return pl.pallas_call(
        _hg_main_body,
        grid=(HG_CHUNKS,),
        in_specs=[pl.BlockSpec((HG_CHUNK, HG_W), lambda i: (i, 0)),
                  pl.BlockSpec((HG_CHUNK, HG_W), lambda i: (i, 1)),
                  pl.BlockSpec((HG_CHUNK, BRANCH), lambda i: (i, 4)),
                  pl.BlockSpec((HG_CHUNK, BRANCH), lambda i: (i, 5)),
                  pl.BlockSpec(half, lat),
                  pl.BlockSpec(half, lat),
                  pl.BlockSpec((1, HG_W), lambda i: (0, 0)),
                  pl.BlockSpec((1, BRANCH), lambda i: (0, 0))],
        out_specs=[pl.BlockSpec((HG_CHUNK, BRANCH), lambda i: (i, 0)),
                   pl.BlockSpec((1,) + fin, lambda i: (jnp.minimum(i, N_CTX_SEQ - 1), 0, 0, 0))],
        out_shape=[jax.ShapeDtypeStruct((N_TOK, BRANCH), F32),
                   jax.ShapeDtypeStruct((N_CTX_SEQ,) + fin, F32)],
        compiler_params=_cparams("arbitrary"),
        name="hg_main",
    )(z_c, z_c, z_c, z_c, s_f, s_b, lb, norm_g)


def _rot_src(cols):
    j = np.arange(len(cols))
    return cols[(j // 16) * 16 + ((j % 16) + 8) % 16], np.where(j % 16 < 8, -1.0, 1.0)


def _take_cols(w, plan):
    idx = np.concatenate([p[0] for p in plan]).astype(np.int32)
    sign = np.concatenate([np.broadcast_to(p[1], p[0].shape) for p in plan]).astype(np.float32)
    return jnp.take(w, jnp.asarray(idx), axis=-1) * jnp.asarray(sign)


def _zeros(n):
    return (np.zeros(n, np.int64), 0.0)


_IN_OFF = {}
_off = 0
for _name, _n in (("da_q", 256), ("da_k", 256), ("da_v", 256), ("da_g", 256), ("s5_u", 256), ("s5_g", 256),
                  ("hg_q", 256), ("hg_ff", 256), ("hg_fb", 256), ("hg_i", 256), ("hg_g", 256),
                  ("mla_cq", MLA_Q_RANK), ("mla_ckv", MLA_KV_RANK), ("mla_kr", MLA_ROPE), ("mla_g", 256)):
    _IN_OFF[_name] = np.arange(_off, _off + _n)
    _off += _n


def _perm_matrix(plan, first, k):
    idx = np.concatenate([p[0] for p in plan]) - first
    sign = np.concatenate([np.broadcast_to(p[1], p[0].shape) for p in plan])
    m = np.zeros((k, len(idx)), np.float32)
    m[idx, np.arange(len(idx))] = sign
    return jnp.asarray(m, BF16)


def _in_proj_weights(w_in):
    c = _IN_OFF

    def per_head(x, y):
        return (np.concatenate([c[x].reshape(HG_HEADS, HG_DK), c[y].reshape(HG_HEADS, HG_DK)], axis=1).reshape(-1), 1.0)
    prot = _perm_matrix([_rot_src(c["da_q"])], 0, 256)
    pdup = _perm_matrix([per_head("hg_q", "hg_q")], c["hg_q"][0], 256)
    pint = _perm_matrix([per_head("hg_ff", "hg_fb")], c["hg_ff"][0], 512)
    col = lambda name: w_in[..., c[name][0]:c[name][-1] + 1]
    zero = lambda n: jnp.zeros(w_in.shape[:-1] + (n,), w_in.dtype)
    kr = col("mla_kr")
    kr_rot = jnp.concatenate([-kr[..., 8:16], kr[..., 0:8], -kr[..., 24:32], kr[..., 16:24]], axis=-1)
    tail = 128 - MLA_NOPE - MLA_ROPE
    w_d = jnp.concatenate([col("mla_cq"), zero(256 - MLA_Q_RANK), col("mla_ckv"), zero(MLA_NOPE), kr, zero(tail),
                           col("mla_g"), zero(MLA_NOPE), kr_rot, zero(tail)], axis=-1)
    return _arrange_w_in(w_in, prot, pdup, pint), w_d.astype(BF16)


def _rope_tables():
    t = np.arange(LAT_LEN)
    pos = np.stack([t // GRID_W, t % GRID_W], axis=1).astype(np.float32)
    inv_freq = (np.float32(ROPE_BASE) ** (-np.arange(8, dtype=np.float32) / np.float32(8))).astype(np.float32)
    r = np.arange(MLA_ROPE)
    ang = (pos[:, r // 16] * inv_freq[r % 8][None, :]).astype(np.float64)
    cos32, sin32 = np.cos(ang).astype(np.float32), np.sin(ang).astype(np.float32)
    cos_a, sin_a = np.tile(cos32, (1, 8)), np.tile(sin32, (1, 8))
    ones, zeros = np.ones((LAT_LEN, 1), np.float32), np.zeros((LAT_LEN, 1), np.float32)
    cos_h = np.concatenate([np.tile(ones, (1, 64)), cos32, np.tile(ones, (1, 32))], axis=1)
    sin_h = np.concatenate([np.tile(zeros, (1, 64)), sin32, np.tile(zeros, (1, 32))], axis=1)

    def with_identity(c, s):
        return (np.concatenate([c, np.ones((ROW_TILE, c.shape[1]), np.float32)], axis=0),
                np.concatenate([s, np.zeros((ROW_TILE, s.shape[1]), np.float32)], axis=0))
    cq, sq = with_identity(np.tile(cos_h, (1, MLA_HEADS)), np.tile(sin_h, (1, MLA_HEADS)))
    ck, sk = with_identity(cos_h, sin_h)
    return (jnp.asarray(cos_a), jnp.asarray(sin_a)), tuple(jnp.asarray(x) for x in (cq, sq, ck, sk))


def _mla_weights(w_uq, w_ukv, q_norm):
    hd = MLA_NOPE + MLA_ROPE
    pad_tail = _zeros(MLA_HEAD_PAD - hd)
    q_plan, qr_plan, k_plan, v_plan = [], [], [], []
    for h in range(MLA_HEADS):
        nope, rope = np.arange(h * hd, h * hd + MLA_NOPE), np.arange(h * hd + MLA_NOPE, (h + 1) * hd)
        q_plan += [(nope, 1.0), (rope, 1.0), pad_tail]
        qr_plan += [_zeros(MLA_NOPE), _rot_src(rope), pad_tail]
        k_plan += [(np.arange(h * 2 * MLA_NOPE, h * 2 * MLA_NOPE + MLA_NOPE), 1.0), _zeros(MLA_HEAD_PAD - MLA_NOPE)]
        v_plan += [(np.arange(h * 2 * MLA_NOPE + MLA_NOPE, (h + 1) * 2 * MLA_NOPE), 1.0)]
    pad_rows = lambda x: jnp.pad(x, ((0, 256 - MLA_Q_RANK), (0, 0))).astype(BF16)
    qn = jnp.pad(q_norm, (0, 256 - MLA_Q_RANK)).reshape(1, 256)
    return (pad_rows(_take_cols(w_uq, q_plan)), pad_rows(_take_cols(w_uq, qr_plan)),
            _take_cols(w_ukv, k_plan).astype(BF16), _take_cols(w_ukv, v_plan).astype(BF16), qn)


def _split_bf16(a):
    hi = a.astype(BF16)
    return hi, (a - hi.astype(F32)).astype(BF16)


def _dot_sel(a, sel):
    hi, lo = _split_bf16(a)
    sel = sel.astype(BF16)
    return _dot(hi, sel) + _dot(lo, sel)


def _dot_x3(a, b):
    a_hi, a_lo = _split_bf16(a)
    b_hi, b_lo = _split_bf16(b)
    return _dot(a_hi, b_hi) + _dot(a_hi, b_lo) + _dot(a_lo, b_hi)


def _s5_table_body(xy_ref, bb_ref, c_ref, ct_ref, mt_ref, bst_ref, cot_ref, a_ref):
    n, t, ch = S5_STATE, S5_CHUNK, S5_CH
    wide = 2 * S5_TAP
    xy = xy_ref[0, 0]
    tau_i = lax.broadcasted_iota(jnp.int32, (1, 128), 1)
    tau = tau_i.astype(F32)
    sel_row = lax.broadcasted_iota(jnp.int32, (128, 1), 0)

    def lag(width):
        return lax.broadcasted_iota(jnp.int32, (1, width), 1) >> 4

    def onehot(cond):
        return jnp.where(cond, 1.0, 0.0).astype(F32)
    j = lag(wide)
    e_z = (onehot((j <= t - 1) & (sel_row == t - 1 - j)), onehot((j >= t - 1) & (j <= 2 * t - 2) & (sel_row == j - (t - 1))))
    jc = lag(S5_TAP)
    e_c = (onehot(sel_row == jc + 1), onehot(sel_row == t - jc))
    ch_row = lax.broadcasted_iota(jnp.int32, (ch, 1), 0)
    tile_w = onehot((lax.broadcasted_iota(jnp.int32, (1, wide), 1) & (ch - 1)) == ch_row)
    tile_n = onehot((lax.broadcasted_iota(jnp.int32, (1, S5_TAP), 1) & (ch - 1)) == ch_row)

    z, cot_rows, klong = [], [], None
    for d in range(2):
        x, y = xy[:, 2 * d:2 * d + 1], xy[:, 2 * d + 1:2 * d + 2]
        mag = jnp.exp(jnp.where(tau_i <= t, tau, 0.0) * x)
        ang = jnp.where(tau_i <= t, tau, 0.0) * y
        p_re = jnp.where(tau_i <= t, mag * jnp.cos(ang), 0.0)
        p_im = jnp.where(tau_i <= t, mag * jnp.sin(ang), 0.0)
        a_ref[0, 0, 2 * d] = jnp.broadcast_to(p_re[:, t:t + 1], (n, 128))
        a_ref[0, 0, 2 * d + 1] = jnp.broadcast_to(p_im[:, t:t + 1], (n, 128))
        pz_re, pz_im = _dot_sel(p_re, e_z[d]), _dot_sel(p_im, e_z[d])
        b_re, b_im = _dot_sel(bb_ref[0, 0, 2 * d], tile_w), _dot_sel(bb_ref[0, 0, 2 * d + 1], tile_w)
        z_re, z_im = pz_re * b_re - pz_im * b_im, pz_re * b_im + pz_im * b_re
        z += [z_re, z_im]
        part = _dot_x3(c_ref[0, 0, 2 * d], z_re) - _dot_x3(c_ref[0, 0, 2 * d + 1], z_im)
        klong = part if klong is None else klong + part
        pc_re, pc_im = _dot_sel(p_re, e_c[d]), _dot_sel(p_im, e_c[d])
        c_re, c_im = _dot_sel(ct_ref[0, 0, 2 * d], tile_n), _dot_sel(ct_ref[0, 0, 2 * d + 1], tile_n)
        cot_rows += [c_re * pc_re - c_im * pc_im, -(c_re * pc_im + c_im * pc_re)]
    for tt in range(t):
        off = (t - 1 - tt) * ch
        win = klong if off == 0 else pltpu.roll(klong, wide - off, 1)
        mt_ref[0, 0, tt * ch:(tt + 1) * ch, :] = win[:, :S5_TAP].astype(BF16)
    back = pltpu.roll(z[2], wide - (t - 1) * ch, 1), pltpu.roll(z[3], wide - (t - 1) * ch, 1)
    for k, rows in enumerate((z[0], z[1], back[0], back[1])):
        bst_ref[0, 0, k * n:(k + 1) * n, :] = rows[:, :S5_TAP].astype(BF16)
    cot_ref[0, 0] = jnp.concatenate(cot_rows, axis=0).T.astype(BF16)


def _s5_tables(a_re, a_im, log_dt, b_re, b_im, c_re, c_im):
    nl, g, n, ch = a_re.shape[0], S5_GROUPS, S5_STATE, S5_CH
    step = jnp.exp(log_dt)[..., None]
    mag = jnp.exp(a_re * step)
    ab_re, ab_im = mag * jnp.cos(a_im * step), mag * jnp.sin(a_im * step)
    den = a_re * a_re + a_im * a_im
    f_re = ((ab_re - 1.0) * a_re + ab_im * a_im) / den
    f_im = (ab_im * a_re - (ab_re - 1.0) * a_im) / den
    bb_re = f_re[..., None] * b_re - f_im[..., None] * b_im
    bb_im = f_re[..., None] * b_im + f_im[..., None] * b_re
    by_group = lambda x: jnp.moveaxis(x, 1, 2)
    pair = lambda re, im: jnp.stack([by_group(re), by_group(im)], axis=3).reshape((nl, g, 4) + re.shape[3:])
    xy = jnp.stack([by_group(a_re * step), by_group(a_im * step)], axis=3).reshape(nl, g, 4, n)
    xy = jnp.pad(jnp.swapaxes(xy, 2, 3), ((0, 0), (0, 0), (0, 0), (0, 4)))
    mat = pl.BlockSpec((1, 1, S5_TAP, S5_TAP), lambda l, i: (l, i, 0, 0))
    return pl.pallas_call(
        _s5_table_body,
        grid=(nl, g),
        in_specs=[pl.BlockSpec((1, 1, n, 8), lambda l, i: (l, i, 0, 0)),
                  pl.BlockSpec((1, 1, 4, n, ch), lambda l, i: (l, i, 0, 0, 0)),
                  pl.BlockSpec((1, 1, 4, ch, n), lambda l, i: (l, i, 0, 0, 0)),
                  pl.BlockSpec((1, 1, 4, n, ch), lambda l, i: (l, i, 0, 0, 0))],
        out_specs=[mat, mat, mat, pl.BlockSpec((1, 1, 4, n, 128), lambda l, i: (l, i, 0, 0, 0))],
        out_shape=[jax.ShapeDtypeStruct((nl, g, S5_TAP, S5_TAP), BF16)] * 3
        + [jax.ShapeDtypeStruct((nl, g, 4, n, 128), F32)],
        compiler_params=_cparams("parallel", "parallel"),
        name="s5_tables",
    )(xy, pair(bb_re, bb_im), pair(c_re, c_im), pair(jnp.swapaxes(c_re, -1, -2), jnp.swapaxes(c_im, -1, -2)))


def _s5_chunk_lanes(u):
    return u.reshape(S5_NCHUNK, S5_CHUNK, BRANCH).transpose(1, 2, 0)


def _s5_token_rows(y):
    return y.transpose(2, 0, 1).reshape(N_TOK, BRANCH)


def kernel(x_prompt, x_sample, cache_diff_k, cache_diff_v, state_s5, state_hgrn, cache_mla_ckv, cache_mla_krope, c, c_ctx, w_mod, b_mod, w_in, w_out, da_lambda, da_norm, s5_a_re, s5_a_im, s5_log_dt, s5_b_re, s5_b_im, s5_c_re, s5_c_im, s5_d, s5_w_glu, hg_lb, hg_norm, mla_q_norm, mla_w_uq, mla_kv_norm, mla_w_ukv, final_norm):
    lb_w = jax.nn.softmax(hg_lb.astype(F32), axis=0)
    lb_all = jnp.cumsum(lb_w, axis=0) - lb_w[0:1]
    c_rows = jnp.concatenate([c_ctx[None], c, jnp.zeros((8 - 1 - N_LAT_SEQ, D_MODEL), F32)], axis=0)
    mods = _modulation(c_rows, w_mod, b_mod)
    (cos_a, sin_a), mla_tabs = _rope_tables()
    xs = (x_prompt.reshape(N_CTX, D_MODEL), x_sample.reshape(N_LAT, D_MODEL))
    new_k, new_v, new_s5, new_hg, new_ckv, new_kr = [], [], [], [], [], []
    s5_tabs = _s5_tables(s5_a_re, s5_a_im, s5_log_dt, s5_b_re, s5_b_im, s5_c_re, s5_c_im)
    w_abc, w_d = _in_proj_weights(w_in)
    w_out_bf = w_out.astype(BF16)
    for l in range(DEPTH):
        mod = mods[l, :3].reshape(3, 3, D_MODEL)
        z_a, z_b, z_c, z_d = _in_proj(xs, mod, w_abc, w_d, l)

        lam_init = 0.8 - 0.6 * math.exp(-0.3 * l)
        kv_lat = _da_latent_kv(z_a, cos_a, sin_a,
                               cache_diff_k[:, l].reshape(N_LAT_SEQ, PAST_LEN, BRANCH),
                               cache_diff_v[:, l].reshape(N_LAT_SEQ, PAST_LEN, BRANCH))
        a_out = _da_attention(z_a, da_lambda[l], da_norm[l], lam_init, cos_a, sin_a, kv_lat)
        new_k.append(z_a[:N_CTX, 256:512].reshape(N_CTX_SEQ, CTX_LEN, DA_HEADS, 2 * DA_QK))
        new_v.append(z_a[:N_CTX, 512:768].reshape(N_CTX_SEQ, CTX_LEN, DA_HEADS, 2 * DA_QK))

        h0 = state_s5[:, l].transpose(2, 1, 4, 3, 0).reshape(S5_GROUPS, 4, S5_STATE, N_LAT_SEQ)
        h0 = jnp.pad(h0, ((0, 0), (0, 0), (0, 0), (0, 128 - N_LAT_SEQ)))
        y_all, fin = _s5_scan(_s5_chunk_lanes(z_b[:, :BRANCH]).astype(BF16), *s5_tabs, h0, l)
        b_out = _s5_out(_s5_token_rows(y_all), z_b, s5_d[l].reshape(1, BRANCH), s5_w_glu[l].astype(BF16))
        fin = jnp.stack([fin[:, 0:2, :, S5_CTX_SEQ_CH - 1::S5_CTX_SEQ_CH], fin[:, 2:4, :, 0::S5_CTX_SEQ_CH]], axis=1)
        new_s5.append(fin.transpose(4, 1, 0, 3, 2))

        lb = jnp.concatenate([lb_all[l, 0].reshape(HG_HEADS, HG_DK), lb_all[l, 1].reshape(HG_HEADS, HG_DK)],
                             axis=-1).reshape(1, HG_W)
        head_eye = jnp.eye(HG_HEADS, dtype=F32)
        s0 = state_hgrn[:, l].transpose(0, 2, 1, 3, 4).reshape(N_LAT_SEQ, HG_HEADS, HG_HEAD_W, 1, HG_DK)
        s0 = (s0 * head_eye[None, :, None, :, None]).reshape(N_LAT_SEQ, HG_HEADS, HG_HEAD_W, BRANCH)
        s_f, s_b = _hg_states(z_c, lb, s0)
        c_out, s_fin = _hg_main(z_c, s_f, s_b, lb, jnp.tile(hg_norm[l].reshape(1, HG_DK), (1, HG_HEADS)))
        new_hg.append(s_fin.reshape(N_CTX_SEQ, HG_HEADS, 2, HG_DK, HG_DK).transpose(0, 2, 1, 3, 4))

        wq, wqr, wk, wv, qn = _mla_weights(mla_w_uq[l], mla_w_ukv[l], mla_q_norm[l])
        q, ckv_n, kr = _mla_prep(z_d, mla_tabs, qn, mla_kv_norm[l].reshape(1, MLA_KV_RANK), wq, wqr)
        k_ctx, v_ctx = _mla_kv(ckv_n, kr, wk, wv, N_CTX)
        kr_cache = jnp.pad(cache_mla_krope[:, l], ((0, 0), (0, 0), (MLA_NOPE, 128 - MLA_NOPE - MLA_ROPE)))
        lk = LAT_LEN + PAST_LEN
        ckv_all = jnp.concatenate([ckv_n[N_CTX:].reshape(N_LAT_SEQ, LAT_LEN, 128), cache_mla_ckv[:, l]], axis=1)
        kr_all = jnp.concatenate([kr[N_CTX:].reshape(N_LAT_SEQ, LAT_LEN, 128), kr_cache], axis=1)
        k_lat, v_lat = _mla_kv(ckv_all.reshape(N_LAT_SEQ * lk, 128), kr_all.reshape(N_LAT_SEQ * lk, 128),
                               wk, wv, N_LAT_SEQ * lk)
        d_out = _mla_attention(z_d, q, k_ctx, v_ctx, k_lat, v_lat)
        new_ckv.append(ckv_n[:N_CTX].reshape(N_CTX_SEQ, CTX_LEN, MLA_KV_RANK))
        new_kr.append(kr[:N_CTX, MLA_NOPE:MLA_NOPE + MLA_ROPE].reshape(N_CTX_SEQ, CTX_LEN, MLA_ROPE))

        xs = _out_proj(a_out, b_out, c_out, d_out, xs, mod, w_out_bf, l,
                       final_norm.reshape(1, D_MODEL), final=(l == DEPTH - 1))
        xs = tuple(xs) if l == DEPTH - 1 else (xs,)
    y_prompt = xs[0].reshape(N_CTX_SEQ, CTX_LEN, D_MODEL)
    y_sample = xs[1].reshape(N_LAT_SEQ, LAT_LEN, D_MODEL)
    st = lambda parts: jnp.stack(parts, axis=1)
    return (y_prompt, y_sample, st(new_k), st(new_v), st(new_s5), st(new_hg), st(new_ckv), st(new_kr))
```

```python
import functools
import math

import numpy as np

import jax
import jax.numpy as jnp
from jax import lax
from jax.experimental import pallas as pl
from jax.experimental.pallas import tpu as pltpu

F32 = jnp.float32
BF16 = jnp.bfloat16

D_MODEL = 1024
DEPTH = 2
N_CTX_SEQ = 16
CTX_LEN = 256
N_LAT_SEQ = 2
LAT_LEN = 2048
PAST_LEN = 256
GRID_W = 64
N_CTX = N_CTX_SEQ * CTX_LEN
N_LAT = N_LAT_SEQ * LAT_LEN
N_TOK = N_CTX + N_LAT
BRANCH = 256
EPS = 1e-6
ROPE_BASE = 10000.0
ROW_TILE = 256
LAT_TILES = LAT_LEN // ROW_TILE
N_TILES = N_TOK // ROW_TILE
CTX_TILES = N_CTX // ROW_TILE
VMEM_LIMIT = 48 * 1024 * 1024
LAT_Q_TILE = 512
LAT_Q_TILES = LAT_LEN // LAT_Q_TILE

DA_HEADS = 4
DA_QK = 32
MLA_HEADS = 4
MLA_NOPE = 64
MLA_ROPE = 32
MLA_Q_RANK = 192
MLA_KV_RANK = 128
S5_GROUPS = 16
S5_CH = 16
S5_STATE = 64
S5_CHUNK = 16
HG_HEADS = 4
HG_DK = 64

W_A = 1024
W_B = 512
W_C = 1536
W_D = 768
W_ABC = W_A + W_B + W_C


def _cparams(*sem):
    return pltpu.CompilerParams(dimension_semantics=sem, vmem_limit_bytes=VMEM_LIMIT)


def _tile_seq(i):
    return jnp.where(i < CTX_TILES, 0, 1 + (i - CTX_TILES) // LAT_TILES)


def _silu(x):
    return x * (1.0 / (1.0 + jnp.exp(-x)))


def _dot(a, b):
    return jnp.dot(a, b, preferred_element_type=F32)


def _dot_nt(a, b):
    return lax.dot_general(a, b, (((1,), (1,)), ((), ())), preferred_element_type=F32)


def _mod_body(c_ref, w_ref, b_ref, o_ref):
    c = _silu(c_ref[...]).astype(BF16)
    o_ref[0] = _dot(c, w_ref[0].astype(BF16)) + b_ref[0]


def _modulation(c_rows, w_mod, b_mod):
    tn = 768
    return pl.pallas_call(
        _mod_body,
        grid=(DEPTH, 3 * D_MODEL // tn),
        in_specs=[pl.BlockSpec((8, D_MODEL), lambda l, j: (0, 0)),
                  pl.BlockSpec((1, D_MODEL, tn), lambda l, j: (l, 0, j)),
                  pl.BlockSpec((1, 1, tn), lambda l, j: (l, 0, j))],
        out_specs=pl.BlockSpec((1, 8, tn), lambda l, j: (l, 0, j)),
        out_shape=jax.ShapeDtypeStruct((DEPTH, 8, 3 * D_MODEL), F32),
        compiler_params=_cparams("parallel", "parallel"),
        name="modulation",
    )(c_rows, w_mod, b_mod.reshape(DEPTH, 1, 3 * D_MODEL))


def _split_rows(i, ctx_ref, lat_ref):
    return jnp.where(i < CTX_TILES, ctx_ref[...], lat_ref[...])


def _ctx_tile_spec(w):
    return pl.BlockSpec((ROW_TILE, w), lambda i: (jnp.minimum(i, CTX_TILES - 1), 0))


def _lat_tile_spec(w):
    return pl.BlockSpec((ROW_TILE, w), lambda i: (jnp.maximum(i - CTX_TILES, 0), 0))


def _in_proj_body(*refs, split):
    if split:
        xc_ref, xl_ref, mod_ref, w_ref, wd_ref, oa, ob, oc, od = refs
        x = _split_rows(pl.program_id(0), xc_ref, xl_ref)
    else:
        x_ref, mod_ref, w_ref, wd_ref, oa, ob, oc, od = refs
        x = x_ref[...]
    xn = x * lax.rsqrt(jnp.mean(x * x, axis=-1, keepdims=True) + EPS)
    mod = mod_ref[0]
    h = (xn * (1.0 + mod[1:2]) + mod[0:1]).astype(BF16)
    off = 0
    for o in (oa, ob, oc):
        w = o.shape[-1]
        o[...] = _dot(h, w_ref[0, :, off:off + w])
        off += w
    od[...] = _dot(h, wd_ref[0])


def _in_proj(xs, mod, w_abc, w_d, l):
    widths = (W_A, W_B, W_C, W_D)
    split = len(xs) == 2
    x_specs = ([_ctx_tile_spec(D_MODEL), _lat_tile_spec(D_MODEL)] if split
               else [pl.BlockSpec((ROW_TILE, D_MODEL), lambda i: (i, 0))])
    return pl.pallas_call(
        functools.partial(_in_proj_body, split=split),
        grid=(N_TILES,),
        in_specs=x_specs + [pl.BlockSpec((1, 3, D_MODEL), lambda i: (_tile_seq(i), 0, 0)),
                            pl.BlockSpec((1, D_MODEL, W_ABC), lambda i: (l, 0, 0)),
                            pl.BlockSpec((1, D_MODEL, W_D), lambda i: (l, 0, 0))],
        out_specs=[pl.BlockSpec((ROW_TILE, w), lambda i: (i, 0)) for w in widths],
        out_shape=[jax.ShapeDtypeStruct((N_TOK, w), F32) for w in widths],
        compiler_params=_cparams("parallel"),
        name="in_proj",
    )(*xs, mod, w_abc, w_d)


def _arrange_body(w_ref, pdup_ref, pint_ref, o_ref):
    w = w_ref[0].astype(BF16)

    def perm(x, p_ref):
        return _dot(x, p_ref[...]).astype(BF16)
    o_ref[0] = jnp.concatenate(
        [w[:, 0:1536],
         perm(w[:, 1536:1792], pdup_ref), perm(w[:, 1792:2304], pint_ref), w[:, 2304:2816]], axis=-1)


def _arrange_w_in(w_in, pdup, pint):
    rows = 128
    const = lambda a: pl.BlockSpec(a.shape, lambda l, i: (0, 0))
    return pl.pallas_call(
        _arrange_body,
        grid=(DEPTH, D_MODEL // rows),
        in_specs=[pl.BlockSpec((1, rows, w_in.shape[-1]), lambda l, i: (l, i, 0)), const(pdup), const(pint)],
        out_specs=pl.BlockSpec((1, rows, W_ABC), lambda l, i: (l, i, 0)),
        out_shape=jax.ShapeDtypeStruct((DEPTH, D_MODEL, W_ABC), BF16),
        compiler_params=_cparams("parallel", "parallel"),
        name="arrange_w_in",
    )(w_in, pdup, pint)


def _out_proj_body(*refs, split_in, final):
    ac_ref, al_ref, b_ref, c_ref, dc_ref, dl_ref = refs[:6]
    i = pl.program_id(0)
    if split_in:
        xc_ref, xl_ref, mod_ref, w_ref, fn_ref = refs[6:11]
        x = _split_rows(i, xc_ref, xl_ref)
    else:
        x_ref, mod_ref, w_ref, fn_ref = refs[6:10]
        x = x_ref[...]
    branches = (_split_rows(i, ac_ref, al_ref), b_ref[...], c_ref[...], _split_rows(i, dc_ref, dl_ref))
    acc = None
    for j, r in enumerate(branches):
        t = _dot(r.astype(BF16), w_ref[0, j * BRANCH:(j + 1) * BRANCH, :])
        acc = t if acc is None else acc + t
    x = x + mod_ref[0][2:3] * acc
    if not final:
        refs[-1][...] = x
        return
    y = x * lax.rsqrt(jnp.mean(x * x, axis=-1, keepdims=True) + EPS) * fn_ref[...]
    yc_ref, yl_ref = refs[-2:]

    @pl.when(i < CTX_TILES)
    def _():
        yc_ref[...] = y

    @pl.when(i >= CTX_TILES)
    def _():
        yl_ref[...] = y


def _out_proj(a, b, c, d, xs, mod, w_out, l, final_norm, final):
    br = pl.BlockSpec((ROW_TILE, BRANCH), lambda i: (i, 0))
    pair = [_ctx_tile_spec(BRANCH), _lat_tile_spec(BRANCH)]
    split_in = len(xs) == 2
    x_specs = ([_ctx_tile_spec(D_MODEL), _lat_tile_spec(D_MODEL)] if split_in
               else [pl.BlockSpec((ROW_TILE, D_MODEL), lambda i: (i, 0))])
    if final:
        out_specs = [_ctx_tile_spec(D_MODEL), _lat_tile_spec(D_MODEL)]
        out_shape = [jax.ShapeDtypeStruct((N_CTX, D_MODEL), F32), jax.ShapeDtypeStruct((N_LAT, D_MODEL), F32)]
    else:
        out_specs = pl.BlockSpec((ROW_TILE, D_MODEL), lambda i: (i, 0))
        out_shape = jax.ShapeDtypeStruct((N_TOK, D_MODEL), F32)
    return pl.pallas_call(
        functools.partial(_out_proj_body, split_in=split_in, final=final),
        grid=(N_TILES,),
        in_specs=pair + [br, br] + pair + x_specs + [
            pl.BlockSpec((1, 3, D_MODEL), lambda i: (_tile_seq(i), 0, 0)),
            pl.BlockSpec((1, D_MODEL, D_MODEL), lambda i: (l, 0, 0)),
            pl.BlockSpec((1, D_MODEL), lambda i: (0, 0))],
        out_specs=out_specs,
        out_shape=out_shape,
        compiler_params=_cparams("arbitrary"),
        name="out_proj",
    )(*a, b, c, *d, *xs, mod, w_out, final_norm)


LOG2E = 1.4426950408889634


def _exp2_rows(s):
    e = jnp.exp2(s - jnp.max(s, axis=-1, keepdims=True))
    return e, jnp.sum(e, axis=-1, keepdims=True)


def _rope(x, cos, sin_lo, sin_hi):
    w = x.shape[-1]
    return x * cos + pltpu.roll(x, w - 8, 1) * sin_lo + pltpu.roll(x, 8, 1) * sin_hi


def _da_kv_body(k_ref, v_ref, cos_ref, slo_ref, shi_ref, ck_ref, cv_ref, ko_ref, vo_ref):
    j = pl.program_id(1)

    @pl.when(j < LAT_TILES)
    def _():
        ko_ref[0] = _rope(k_ref[...], cos_ref[...], slo_ref[...], shi_ref[...]).astype(BF16)
        vo_ref[0] = v_ref[...].astype(BF16)

    @pl.when(j == LAT_TILES)
    def _():
        ko_ref[0] = ck_ref[0].astype(BF16)
        vo_ref[0] = cv_ref[0].astype(BF16)


def _da_latent_kv(z_a, tabs, cache_k, cache_v):
    def rows(col):
        return pl.BlockSpec(
            (ROW_TILE, BRANCH),
            lambda b, j: (CTX_TILES + b * LAT_TILES + jnp.minimum(j, LAT_TILES - 1), col))
    tab = pl.BlockSpec((ROW_TILE, BRANCH), lambda b, j: (jnp.minimum(j, LAT_TILES - 1), 0))
    cache = pl.BlockSpec((1, PAST_LEN, BRANCH), lambda b, j: (b, 0, 0))
    out = pl.BlockSpec((1, ROW_TILE, BRANCH), lambda b, j: (b, j, 0))
    shp = jax.ShapeDtypeStruct((N_LAT_SEQ, LAT_LEN + PAST_LEN, BRANCH), BF16)
    return pl.pallas_call(
        _da_kv_body,
        grid=(N_LAT_SEQ, LAT_TILES + 1),
        in_specs=[rows(1), rows(2), tab, tab, tab, cache, cache],
        out_specs=[out, out],
        out_shape=[shp, shp],
        compiler_params=_cparams("parallel", "parallel"),
        name="da_kv",
    )(z_a, z_a, *tabs, cache_k, cache_v)


def _da_attn_body(lam_ref, ng_ref, q_ref, *rest, rope, lam_init):
    if rope:
        cos_ref, slo_ref, shi_ref, k_ref, v_ref, g_ref, o_ref = rest
        q = _rope(q_ref[...], cos_ref[...], slo_ref[...], shi_ref[...])
        k = k_ref[0]
        v = v_ref[0]
    else:
        k_ref, v_ref, g_ref, o_ref = rest
        q = q_ref[...]
        k = k_ref[...].astype(BF16)
        v = v_ref[...].astype(BF16)
    q = q * (DA_QK ** -0.5 * LOG2E)
    lv = lam_ref[...]
    lam = (jnp.exp(jnp.sum(lv[0:1] * lv[1:2], axis=-1, keepdims=True))
           - jnp.exp(jnp.sum(lv[2:3] * lv[3:4], axis=-1, keepdims=True)) + lam_init)
    lane = lax.broadcasted_iota(jnp.int32, (1, BRANCH), 1)
    acc = jnp.zeros(q.shape, F32)
    for h in range(DA_HEADS):
        q1 = jnp.where(lane // DA_QK == 2 * h, q, 0.0).astype(BF16)
        q2 = jnp.where(lane // DA_QK == 2 * h + 1, q, 0.0).astype(BF16)
        e1, l1 = _exp2_rows(_dot_nt(q1, k))
        e2, l2 = _exp2_rows(_dot_nt(q2, k))
        a = (e1 - (lam * l1 / l2) * e2).astype(BF16)
        acc = jnp.where(lane // (2 * DA_QK) == h, _dot(a, v) * (1.0 / l1), acc)
    sq = acc * acc
    ms = jnp.zeros(q.shape, F32)
    for h in range(DA_HEADS):
        hm = lane // (2 * DA_QK) == h
        ms = jnp.where(hm, jnp.sum(jnp.where(hm, sq, 0.0), axis=-1, keepdims=True), ms)
    o = acc * lax.rsqrt(ms * (1.0 / (2 * DA_QK)) + EPS) * (ng_ref[...] * (1.0 - lam_init))
    o_ref[...] = o * _silu(g_ref[...])


def _da_attention(z_a, lam_vec, norm_g, lam_init, tabs, kv_lat):
    ng = jnp.tile(norm_g.reshape(1, 2 * DA_QK), (1, DA_HEADS))
    small = [pl.BlockSpec((4, DA_QK), lambda *_: (0, 0)), pl.BlockSpec((1, BRANCH), lambda *_: (0, 0))]

    def col(c):
        return pl.BlockSpec((ROW_TILE, BRANCH), lambda i: (i, c))
    ctx = pl.pallas_call(
        functools.partial(_da_attn_body, rope=False, lam_init=lam_init),
        grid=(CTX_TILES,),
        in_specs=small + [col(0), col(1), col(2), col(3)],
        out_specs=pl.BlockSpec((ROW_TILE, BRANCH), lambda i: (i, 0)),
        out_shape=jax.ShapeDtypeStruct((N_CTX, BRANCH), F32),
        compiler_params=_cparams("parallel"),
        name="da_attn_ctx",
    )(lam_vec, ng, z_a, z_a, z_a, z_a)

    def lcol(c):
        return pl.BlockSpec((LAT_Q_TILE, BRANCH), lambda b, j: (N_CTX // LAT_Q_TILE + b * LAT_Q_TILES + j, c))
    tab = pl.BlockSpec((LAT_Q_TILE, BRANCH), lambda b, j: (j, 0))
    kvs = pl.BlockSpec((1, LAT_LEN + PAST_LEN, BRANCH), lambda b, j: (b, 0, 0))
    lat = pl.pallas_call(
        functools.partial(_da_attn_body, rope=True, lam_init=lam_init),
        grid=(N_LAT_SEQ, LAT_Q_TILES),
        in_specs=small + [lcol(0), tab, tab, tab, kvs, kvs, lcol(3)],
        out_specs=pl.BlockSpec((LAT_Q_TILE, BRANCH), lambda b, j: (b * LAT_Q_TILES + j, 0)),
        out_shape=jax.ShapeDtypeStruct((N_LAT, BRANCH), F32),
        compiler_params=_cparams("parallel", "parallel"),
        name="da_attn_lat",
    )(lam_vec, ng, z_a, *tabs, kv_lat[0], kv_lat[1], z_a)
    return ctx, lat


MLA_HEAD_PAD = 128
MLA_QW = MLA_HEADS * MLA_HEAD_PAD


def _mla_prep_body(cq_ref, ckv_ref, kr_ref, cq_t, sq_lo, sq_hi, ck_t, sk_lo, sk_hi, qn_ref, kvn_ref, wq_ref,
                   q_out, ckv_out, kr_out):
    cq = cq_ref[...]
    ms = jnp.sum(cq * cq, axis=-1, keepdims=True) * (1.0 / MLA_Q_RANK)
    qn = (cq * lax.rsqrt(ms + EPS) * qn_ref[...]).astype(BF16)
    q = _rope(_dot(qn, wq_ref[...]), cq_t[...], sq_lo[...], sq_hi[...])
    q_out[...] = (q * ((MLA_NOPE + MLA_ROPE) ** -0.5 * LOG2E)).astype(BF16)
    ckv = ckv_ref[...]
    ckv_out[...] = ckv * lax.rsqrt(jnp.mean(ckv * ckv, axis=-1, keepdims=True) + EPS) * kvn_ref[...]
    kr_out[...] = _rope(kr_ref[...], ck_t[...], sk_lo[...], sk_hi[...])


def _mla_prep(z_d, tabs, q_norm_pad, kv_norm, wq):
    def tab(w):
        return pl.BlockSpec(
            (ROW_TILE, w), lambda i: (jnp.where(i < CTX_TILES, LAT_TILES, (i - CTX_TILES) % LAT_TILES), 0))

    def col(w, c):
        return pl.BlockSpec((ROW_TILE, w), lambda i: (i, c))

    def const(shape):
        return pl.BlockSpec(shape, lambda i: (0, 0))
    return pl.pallas_call(
        _mla_prep_body,
        grid=(N_TILES,),
        in_specs=[col(256, 0), col(128, 2), col(128, 3),
                  tab(MLA_QW), tab(MLA_QW), tab(MLA_QW), tab(128), tab(128), tab(128),
                  const((1, 256)), const((1, 128)), const((256, MLA_QW))],
        out_specs=[col(MLA_QW, 0), col(128, 0), col(128, 0)],
        out_shape=[jax.ShapeDtypeStruct((N_TOK, MLA_QW), BF16),
                   jax.ShapeDtypeStruct((N_TOK, 128), F32),
                   jax.ShapeDtypeStruct((N_TOK, 128), F32)],
        compiler_params=_cparams("parallel"),
        name="mla_prep",
    )(z_d, z_d, z_d, *tabs, q_norm_pad, kv_norm, wq)


def _mla_kv_body(ckv_ref, kr_ref, wk_ref, wv_ref, k_out, v_out):
    c = ckv_ref[...].astype(BF16)
    kr = kr_ref[...]
    k_out[...] = (_dot(c, wk_ref[...]) + jnp.concatenate([kr] * MLA_HEADS, axis=-1)).astype(BF16)
    v_out[...] = _dot(c, wv_ref[...]).astype(BF16)


def _mla_kv(ckv, kr, wk, wv, n_rows):
    return pl.pallas_call(
        _mla_kv_body,
        grid=(n_rows // ROW_TILE,),
        in_specs=[pl.BlockSpec((ROW_TILE, 128), lambda i: (i, 0)),
                  pl.BlockSpec((ROW_TILE, 128), lambda i: (i, 0)),
                  pl.BlockSpec((128, MLA_QW), lambda i: (0, 0)),
                  pl.BlockSpec((128, BRANCH), lambda i: (0, 0))],
        out_specs=[pl.BlockSpec((ROW_TILE, MLA_QW), lambda i: (i, 0)),
                   pl.BlockSpec((ROW_TILE, BRANCH), lambda i: (i, 0))],
        out_shape=[jax.ShapeDtypeStruct((n_rows, MLA_QW), BF16),
                   jax.ShapeDtypeStruct((n_rows, BRANCH), BF16)],
        compiler_params=_cparams("parallel"),
        name="mla_kv",
    )(ckv, kr, wk, wv)


def _mla_attn_body(q_ref, k_ref, v_ref, g_ref, o_ref):
    q = q_ref[...]
    k = k_ref[...].reshape(-1, MLA_QW)
    v = v_ref[...].reshape(-1, BRANCH)
    lane = lax.broadcasted_iota(jnp.int32, (1, BRANCH), 1)
    acc = jnp.zeros((q.shape[0], BRANCH), F32)
    for h in range(MLA_HEADS):
        sl = slice(h * MLA_HEAD_PAD, (h + 1) * MLA_HEAD_PAD)
        e, l = _exp2_rows(_dot_nt(q[:, sl], k[:, sl]))
        acc = jnp.where(lane // 64 == h, _dot(e.astype(BF16), v) * (1.0 / l), acc)
    o_ref[...] = acc * _silu(g_ref[...])


def _mla_attention(z_d, q, k_ctx, v_ctx, k_lat, v_lat):
    ctx = pl.pallas_call(
        _mla_attn_body,
        grid=(CTX_TILES,),
        in_specs=[pl.BlockSpec((ROW_TILE, MLA_QW), lambda i: (i, 0)),
                  pl.BlockSpec((ROW_TILE, MLA_QW), lambda i: (i, 0)),
                  pl.BlockSpec((ROW_TILE, BRANCH), lambda i: (i, 0)),
                  pl.BlockSpec((ROW_TILE, BRANCH), lambda i: (i, 2))],
        out_specs=pl.BlockSpec((ROW_TILE, BRANCH), lambda i: (i, 0)),
        out_shape=jax.ShapeDtypeStruct((N_CTX, BRANCH), F32),
        compiler_params=_cparams("parallel"),
        name="mla_attn_ctx",
    )(q, k_ctx, v_ctx, z_d)
    lk = LAT_LEN + PAST_LEN
    lat = pl.pallas_call(
        _mla_attn_body,
        grid=(N_LAT_SEQ, LAT_Q_TILES),
        in_specs=[pl.BlockSpec((LAT_Q_TILE, MLA_QW), lambda b, j: (N_CTX // LAT_Q_TILE + b * LAT_Q_TILES + j, 0)),
                  pl.BlockSpec((1, lk, MLA_QW), lambda b, j: (b, 0, 0)),
                  pl.BlockSpec((1, lk, BRANCH), lambda b, j: (b, 0, 0)),
                  pl.BlockSpec((LAT_Q_TILE, BRANCH), lambda b, j: (N_CTX // LAT_Q_TILE + b * LAT_Q_TILES + j, 2))],
        out_specs=pl.BlockSpec((LAT_Q_TILE, BRANCH), lambda b, j: (b * LAT_Q_TILES + j, 0)),
        out_shape=jax.ShapeDtypeStruct((N_LAT, BRANCH), F32),
        compiler_params=_cparams("parallel", "parallel"),
        name="mla_attn_lat",
    )(q, k_lat.reshape(N_LAT_SEQ, lk, MLA_QW), v_lat.reshape(N_LAT_SEQ, lk, BRANCH), z_d)
    return ctx, lat


S5_TAP = S5_CHUNK * S5_CH
S5_NCHUNK = N_TOK // S5_CHUNK
S5_CTX_CH = N_CTX // S5_CHUNK
S5_CTX_SEQ_CH = CTX_LEN // S5_CHUNK
S5_LAT_SEQ_CH = LAT_LEN // S5_CHUNK
S5_SCAN_STEPS = S5_LAT_SEQ_CH.bit_length() - 1


def _s5_body(x_ref, mt_ref, bst_ref, cot_ref, a_ref, h0_ref, y_ref, fin_ref):
    x = x_ref[...].reshape(S5_TAP, S5_NCHUNK)
    y = _dot(mt_ref[0, 0], x)
    s = _dot(bst_ref[0, 0], x)
    lane = lax.broadcasted_iota(jnp.int32, (1, S5_NCHUNK), 1)
    is_lat = lane >= S5_CTX_CH
    pos_f = jnp.where(is_lat, (lane - S5_CTX_CH) & (S5_LAT_SEQ_CH - 1), lane & (S5_CTX_SEQ_CH - 1))
    pos_b = jnp.where(is_lat, S5_LAT_SEQ_CH - 1, S5_CTX_SEQ_CH - 1) - pos_f
    hin = []
    for d in range(2):
        n = S5_STATE
        sre, sim = s[2 * d * n:(2 * d + 1) * n], s[(2 * d + 1) * n:(2 * d + 2) * n]
        are = jnp.concatenate([a_ref[0, 0, 2 * d]] * (S5_NCHUNK // 128), axis=-1)
        aim = jnp.concatenate([a_ref[0, 0, 2 * d + 1]] * (S5_NCHUNK // 128), axis=-1)
        pos = pos_f if d == 0 else pos_b
        h0r, h0i = jnp.zeros_like(sre), jnp.zeros_like(sre)
        for b in range(N_LAT_SEQ):
            first = S5_CTX_CH + b * S5_LAT_SEQ_CH + (0 if d == 0 else S5_LAT_SEQ_CH - 1)
            h0r = jnp.where(lane == first, h0_ref[0, 2 * d][:, b:b + 1], h0r)
            h0i = jnp.where(lane == first, h0_ref[0, 2 * d + 1][:, b:b + 1], h0i)
        xr = sre + are * h0r - aim * h0i
        xi = sim + are * h0i + aim * h0r
        pr, pi = are, aim
        for j in range(S5_SCAN_STEPS):
            sh = 1 << j
            shift = sh if d == 0 else S5_NCHUNK - sh
            rr, ri = pltpu.roll(xr, shift, 1), pltpu.roll(xi, shift, 1)
            ok = pos >= sh
            xr, xi = (xr + jnp.where(ok, pr * rr - pi * ri, 0.0), xi + jnp.where(ok, pr * ri + pi * rr, 0.0))
            pr, pi = pr * pr - pi * pi, 2.0 * pr * pi
        fin_ref[0, 2 * d] = xr[:, :S5_CTX_CH]
        fin_ref[0, 2 * d + 1] = xi[:, :S5_CTX_CH]
        one = 1 if d == 0 else S5_NCHUNK - 1
        hin.append(jnp.where(pos >= 1, pltpu.roll(xr, one, 1), h0r))
        hin.append(jnp.where(pos >= 1, pltpu.roll(xi, one, 1), h0i))
    y = y + _dot(cot_ref[0, 0], jnp.concatenate(hin, axis=0).astype(BF16))
    y_ref[...] = y.reshape(S5_CHUNK, S5_CH, S5_NCHUNK)


def _s5_scan(x_all, mt, bst, cot, a16, h0, l):
    g = S5_GROUPS
    sq = pl.BlockSpec((1, 1, S5_TAP, S5_TAP), lambda i: (l, i, 0, 0))
    st = pl.BlockSpec((1, 4, S5_STATE, 128), lambda i: (i, 0, 0, 0))
    return pl.pallas_call(
        _s5_body,
        grid=(g,),
        in_specs=[pl.BlockSpec((S5_CHUNK, S5_CH, S5_NCHUNK), lambda i: (0, i, 0)), sq, sq, sq,
                  pl.BlockSpec((1, 1, 4, S5_STATE, 128), lambda i: (l, i, 0, 0, 0)), st],
        out_specs=[pl.BlockSpec((S5_CHUNK, S5_CH, S5_NCHUNK), lambda i: (0, i, 0)),
                   pl.BlockSpec((1, 4, S5_STATE, S5_CTX_CH), lambda i: (i, 0, 0, 0))],
        out_shape=[jax.ShapeDtypeStruct((S5_CHUNK, BRANCH, S5_NCHUNK), F32),
                   jax.ShapeDtypeStruct((g, 4, S5_STATE, S5_CTX_CH), F32)],
        compiler_params=_cparams("parallel"),
        name="s5_scan",
    )(x_all, mt, bst, cot, a16, h0)


def _s5_out_body(y_ref, u_ref, g_ref, d_ref, w_ref, o_ref):
    y = u_ref[...] * d_ref[...] + y_ref[...]
    ge = 0.5 * y * (1.0 + jnp.tanh(0.7978845608028654 * (y + 0.044715 * (y * y * y))))
    gl = _dot(ge.astype(BF16), w_ref[...])
    o_ref[...] = gl[:, :BRANCH] * (1.0 / (1.0 + jnp.exp(-gl[:, BRANCH:]))) * _silu(g_ref[...])


def _s5_out(y_ssm, z_b, d_skip, w_glu):
    def col(c):
        return pl.BlockSpec((ROW_TILE, BRANCH), lambda i: (i, c))
    return pl.pallas_call(
        _s5_out_body,
        grid=(N_TILES,),
        in_specs=[col(0), col(0), col(1),
                  pl.BlockSpec((1, BRANCH), lambda i: (0, 0)),
                  pl.BlockSpec((BRANCH, 2 * BRANCH), lambda i: (0, 0))],
        out_specs=col(0),
        out_shape=jax.ShapeDtypeStruct((N_TOK, BRANCH), F32),
        compiler_params=_cparams("parallel"),
        name="s5_out",
    )(y_ssm, z_b, z_b, d_skip, w_glu)


HG_CHUNK = ROW_TILE
HG_W = 2 * HG_HEADS * HG_DK
HG_HEAD_W = 2 * HG_DK
HG_LAT_CHUNKS = LAT_LEN // HG_CHUNK
HG_CHUNKS = N_TOK // HG_CHUNK


def _hg_gates(z, lb):
    e = jnp.exp(-jnp.abs(z))
    r = 1.0 / (1.0 + e)
    sig_pos = jnp.where(z >= 0, r, e * r)
    sig_neg = jnp.where(z >= 0, e * r, r)
    return lb + (1.0 - lb) * sig_pos, (1.0 - lb) * sig_neg


def _bcast_row(x, period, r):
    n, w = x.shape
    if period >= 8:
        x3 = x.reshape(n // period, period, w)
        return jnp.broadcast_to(x3[:, r:r + 1, :], x3.shape).reshape(n, w)
    x3 = x.reshape(n // 8, 8, w)
    sub = lax.broadcasted_iota(jnp.int32, (1, 8, 1), 1)
    out = None
    for j in range(8 // period):
        b = jnp.broadcast_to(x3[:, j * period + r:j * period + r + 1, :], x3.shape)
        out = b if out is None else jnp.where(sub >= j * period, b, out)
    return out.reshape(n, w)


def _hg_scans(f, isb):
    n = f.shape[0]
    row = lax.broadcasted_iota(jnp.int32, (n, 1), 0)
    p, r = f, jnp.ones_like(f)
    levels = []
    h, sh = 1, 0
    while h < n:
        levels.append((h, sh, p, r))
        up = (row >> sh) & 1
        tot_p = jnp.where(isb == 1, _bcast_row(p, 2 * h, h), _bcast_row(p, 2 * h, h - 1))
        tot_r = jnp.where(isb == 1, _bcast_row(p, 2 * h, 0), _bcast_row(p, 2 * h, 2 * h - 1))
        p = p * jnp.where(up != isb, tot_p, 1.0)
        r = r * jnp.where(up == isb, tot_r, 1.0)
        h, sh = 2 * h, sh + 1
    return levels, p, r


def _hg_state_body(zf_ref, zb_ref, vf_ref, vb_ref, lb_ref, s0_ref, sf_out, sb_out, s_scr):
    i = pl.program_id(0)

    @pl.when(i % HG_LAT_CHUNKS == 0)
    def _():
        s_scr[...] = s0_ref[0]

    sf_out[0] = s_scr[:, 0:HG_DK, :]
    sb_out[0] = s_scr[:, HG_DK:, :]
    lane5 = lax.broadcasted_iota(jnp.int32, (1, HG_W), 1)
    isb = (lane5 >> 6) & 1
    z = jnp.where(isb == 1, zb_ref[...], zf_ref[...])
    f, k = _hg_gates(z, lb_ref[...])
    r, ptot = _hg_chunk_decay(f, isb)
    kt = k * r
    lane = lax.broadcasted_iota(jnp.int32, (1, BRANCH), 1)
    vf = vf_ref[...]
    vb = vb_ref[...]
    for hd in range(HG_HEADS):
        sl = slice(hd * HG_HEAD_W, (hd + 1) * HG_HEAD_W)
        kth = kt[:, sl].T.astype(BF16)
        hm = (lane >> 6) == hd
        d_f = _dot(kth, jnp.where(hm, vf, 0.0).astype(BF16))
        d_b = _dot(kth, jnp.where(hm, vb, 0.0).astype(BF16))
        ds = jnp.concatenate([d_f[:HG_DK], d_b[HG_DK:]], axis=0)
        pcol = jnp.broadcast_to(ptot[:, sl], (HG_HEAD_W, HG_HEAD_W)).T[:, 0:1]
        s_scr[hd] = s_scr[hd] * pcol + ds


def _hg_chunk_decay(f, isb):
    n = f.shape[0]
    row = lax.broadcasted_iota(jnp.int32, (n, 1), 0)
    dist = jnp.where(isb == 1, row, n - 1 - row)
    x = f
    sh = 1
    while sh < n:
        src = jnp.where(isb == 1, pltpu.roll(x, sh, 0), pltpu.roll(x, n - sh, 0))
        x = x * jnp.where(dist >= sh, src, 1.0)
        sh *= 2
    total = jnp.where(isb == 1, x[n - 1:n], x[0:1])
    nxt = jnp.where(isb == 1, pltpu.roll(x, 1, 0), pltpu.roll(x, n - 1, 0))
    return jnp.where(dist >= 1, nxt, 1.0), total


HG_LAT_STEPS = N_LAT_SEQ * HG_LAT_CHUNKS


def _hg_lat_rev(i):
    return (i // HG_LAT_CHUNKS) * HG_LAT_CHUNKS + (HG_LAT_CHUNKS - 1 - i % HG_LAT_CHUNKS)


def _hg_states(z_c, lb, s0):
    first = N_CTX_SEQ
    zz_f = pl.BlockSpec((HG_CHUNK, HG_W), lambda i: (first + i, 1))
    zz_b = pl.BlockSpec((HG_CHUNK, HG_W), lambda i: (first + _hg_lat_rev(i), 1))
    v_f = pl.BlockSpec((HG_CHUNK, BRANCH), lambda i: (first + i, 4))
    v_b = pl.BlockSpec((HG_CHUNK, BRANCH), lambda i: (first + _hg_lat_rev(i), 4))
    st = (HG_HEADS, HG_HEAD_W, BRANCH)
    half = (HG_HEADS, HG_DK, BRANCH)
    return pl.pallas_call(
        _hg_state_body,
        grid=(HG_LAT_STEPS,),
        in_specs=[zz_f, zz_b, v_f, v_b,
                  pl.BlockSpec((1, HG_W), lambda i: (0, 0)),
                  pl.BlockSpec((1,) + st, lambda i: (i // HG_LAT_CHUNKS, 0, 0, 0))],
        out_specs=[pl.BlockSpec((1,) + half, lambda i: (i, 0, 0, 0)),
                   pl.BlockSpec((1,) + half, lambda i: (_hg_lat_rev(i), 0, 0, 0))],
        out_shape=[jax.ShapeDtypeStruct((HG_LAT_STEPS,) + half, F32),
                   jax.ShapeDtypeStruct((HG_LAT_STEPS,) + half, F32)],
        scratch_shapes=[pltpu.VMEM(st, F32)],
        compiler_params=_cparams("arbitrary"),
        name="hg_states",
    )(z_c, z_c, z_c, z_c, lb, s0)


def _hg_main_body(qq_ref, zz_ref, v_ref, g_ref, sf_ref, sb_ref, lb_ref, ng_ref, o_ref, fin_ref):
    n = HG_CHUNK
    i = pl.program_id(0)
    qq = qq_ref[...]
    lane5 = lax.broadcasted_iota(jnp.int32, (1, HG_W), 1)
    isb = (lane5 >> 6) & 1
    f, k = _hg_gates(zz_ref[...], lb_ref[...])
    levels, pfull, rfull = _hg_scans(f, isb)
    row = lax.broadcasted_iota(jnp.int32, (n, 1), 0)
    col = lax.broadcasted_iota(jnp.int32, (1, n), 1)
    ops = [(qq.astype(BF16), k.astype(BF16), row == col)]
    for h, sh, p, r in levels:
        up = (row >> sh) & 1
        qt = jnp.where(up != isb, qq * p, 0.0).astype(BF16)
        kt = jnp.where(up == isb, k * r, 0.0).astype(BF16)
        ops.append((qt, kt, (row >> (sh + 1)) == (col >> (sh + 1))))
    qc = (qq * pfull).astype(BF16)
    v = v_ref[...]
    vb = v.astype(BF16)
    lane = lax.broadcasted_iota(jnp.int32, (1, BRANCH), 1)
    latent = i >= N_CTX_SEQ
    acc = jnp.zeros((n, BRANCH), F32)
    for hd in range(HG_HEADS):
        sl = slice(hd * HG_HEAD_W, (hd + 1) * HG_HEAD_W)
        a = jnp.zeros((n, n), F32)
        for qt, kt, mask in ops:
            a = a + jnp.where(mask, _dot_nt(qt[:, sl], kt[:, sl]), 0.0)
        s_in = jnp.concatenate([sf_ref[0, hd], sb_ref[0, hd]], axis=0)
        s_in = jnp.where(latent, s_in, 0.0).astype(BF16)
        o_h = _dot(a.astype(BF16), vb) + _dot(qc[:, sl], s_in)
        acc = jnp.where((lane >> 6) == hd, o_h, acc)
    sq = acc * acc
    ms = jnp.zeros((n, BRANCH), F32)
    for hd in range(HG_HEADS):
        hm = (lane >> 6) == hd
        ms = jnp.where(hm, jnp.sum(jnp.where(hm, sq, 0.0), axis=-1, keepdims=True), ms)
    o_ref[...] = acc * lax.rsqrt(ms * (1.0 / HG_DK) + EPS) * ng_ref[...] * _silu(g_ref[...])

    @pl.when(i < N_CTX_SEQ)
    def _():
        kt_full = k * rfull
        for hd in range(HG_HEADS):
            kth = kt_full[:, hd * HG_HEAD_W:(hd + 1) * HG_HEAD_W].T.astype(BF16)
            ds = _dot(kth, jnp.where((lane >> 6) == hd, v, 0.0).astype(BF16))
            fin_ref[0, hd] = ds[:, hd * HG_DK:(hd + 1) * HG_DK]


def _hg_main(z_c, s_f, s_b, lb, norm_g):
    half = (1, HG_HEADS, HG_DK, BRANCH)
    lat = lambda i: (jnp.maximum(i - N_CTX_SEQ, 0), 0, 0, 0)
    fin = (HG_HEADS, HG_HEAD_W, HG_DK)
    return pl.pallas_call(
        _hg_main_body,
        grid=(HG_CHUNKS,),
        in_specs=[pl.BlockSpec((HG_CHUNK, HG_W), lambda i: (i, 0)),
                  pl.BlockSpec((HG_CHUNK, HG_W), lambda i: (i, 1)),
                  pl.BlockSpec((HG_CHUNK, BRANCH), lambda i: (i, 4)),
                  pl.BlockSpec((HG_CHUNK, BRANCH), lambda i: (i, 5)),
                  pl.BlockSpec(half, lat),
                  pl.BlockSpec(half, lat),
                  pl.BlockSpec((1, HG_W), lambda i: (0, 0)),
                  pl.BlockSpec((1, BRANCH), lambda i: (0, 0))],
        out_specs=[pl.BlockSpec((HG_CHUNK, BRANCH), lambda i: (i, 0)),
                   pl.BlockSpec((1,) + fin, lambda i: (jnp.minimum(i, N_CTX_SEQ - 1), 0, 0, 0))],
        out_shape=[jax.ShapeDtypeStruct((N_TOK, BRANCH), F32),
                   jax.ShapeDtypeStruct((N_CTX_SEQ,) + fin, F32)],
        compiler_params=_cparams("arbitrary"),
        name="hg_main",
    )(z_c, z_c, z_c, z_c, s_f, s_b, lb, norm_g)


def _take_cols(w, plan):
    idx = np.concatenate([p[0] for p in plan]).astype(np.int32)
    sign = np.concatenate([np.broadcast_to(p[1], p[0].shape) for p in plan]).astype(np.float32)
    return jnp.take(w, jnp.asarray(idx), axis=-1) * jnp.asarray(sign)


def _zeros(n):
    return (np.zeros(n, np.int64), 0.0)


_IN_OFF = {}
_off = 0
for _name, _n in (("da_q", 256), ("da_k", 256), ("da_v", 256), ("da_g", 256), ("s5_u", 256), ("s5_g", 256),
                  ("hg_q", 256), ("hg_ff", 256), ("hg_fb", 256), ("hg_i", 256), ("hg_g", 256),
                  ("mla_cq", MLA_Q_RANK), ("mla_ckv", MLA_KV_RANK), ("mla_kr", MLA_ROPE), ("mla_g", 256)):
    _IN_OFF[_name] = np.arange(_off, _off + _n)
    _off += _n


def _perm_matrix(plan, first, k):
    idx = np.concatenate([p[0] for p in plan]) - first
    sign = np.concatenate([np.broadcast_to(p[1], p[0].shape) for p in plan])
    m = np.zeros((k, len(idx)), np.float32)
    m[idx, np.arange(len(idx))] = sign
    return jnp.asarray(m, BF16)


def _in_proj_weights(w_in):
    c = _IN_OFF

    def per_head(x, y):
        return (np.concatenate([c[x].reshape(HG_HEADS, HG_DK), c[y].reshape(HG_HEADS, HG_DK)], axis=1).reshape(-1), 1.0)
    pdup = _perm_matrix([per_head("hg_q", "hg_q")], c["hg_q"][0], 256)
    pint = _perm_matrix([per_head("hg_ff", "hg_fb")], c["hg_ff"][0], 512)
    col = lambda name: w_in[..., c[name][0]:c[name][-1] + 1]
    zero = lambda n: jnp.zeros(w_in.shape[:-1] + (n,), w_in.dtype)
    w_d = jnp.concatenate([col("mla_cq"), zero(256 - MLA_Q_RANK), col("mla_ckv"), zero(MLA_NOPE), col("mla_kr"),
                           zero(128 - MLA_NOPE - MLA_ROPE), col("mla_g")], axis=-1)
    return _arrange_w_in(w_in, pdup, pint), w_d.astype(BF16)


def _rope_tables():
    t = np.arange(LAT_LEN)
    pos = np.stack([t // GRID_W, t % GRID_W], axis=1).astype(np.float32)
    inv_freq = (np.float32(ROPE_BASE) ** (-np.arange(8, dtype=np.float32) / np.float32(8))).astype(np.float32)
    r = np.arange(MLA_ROPE)
    ang = (pos[:, r // 16] * inv_freq[r % 8][None, :]).astype(np.float64)
    cos32, sin32 = np.cos(ang).astype(np.float32), np.sin(ang).astype(np.float32)
    lo = (np.arange(MLA_ROPE) % 16 < 8)[None, :]
    sin_lo32, sin_hi32 = np.where(lo, -sin32, 0.0).astype(np.float32), np.where(lo, 0.0, sin32).astype(np.float32)
    da_tabs = tuple(np.tile(x, (1, 8)) for x in (cos32, sin_lo32, sin_hi32))

    def head(x, fill):
        h = np.concatenate([np.full((LAT_LEN, MLA_NOPE), fill, np.float32), x,
                            np.full((LAT_LEN, MLA_HEAD_PAD - MLA_NOPE - MLA_ROPE), fill, np.float32)], axis=1)
        return np.concatenate([h, np.full((ROW_TILE, MLA_HEAD_PAD), fill, np.float32)], axis=0)
    k_tabs = (head(cos32, 1.0), head(sin_lo32, 0.0), head(sin_hi32, 0.0))
    q_tabs = tuple(np.tile(x, (1, MLA_HEADS)) for x in k_tabs)
    return tuple(jnp.asarray(x) for x in da_tabs), tuple(jnp.asarray(x) for x in q_tabs + k_tabs)


def _mla_weights(w_uq, w_ukv, q_norm):
    hd = MLA_NOPE + MLA_ROPE
    pad_tail = _zeros(MLA_HEAD_PAD - hd)
    q_plan, k_plan, v_plan = [], [], []
    for h in range(MLA_HEADS):
        nope, rope = np.arange(h * hd, h * hd + MLA_NOPE), np.arange(h * hd + MLA_NOPE, (h + 1) * hd)
        q_plan += [(nope, 1.0), (rope, 1.0), pad_tail]
        k_plan += [(np.arange(h * 2 * MLA_NOPE, h * 2 * MLA_NOPE + MLA_NOPE), 1.0), _zeros(MLA_HEAD_PAD - MLA_NOPE)]
        v_plan += [(np.arange(h * 2 * MLA_NOPE + MLA_NOPE, (h + 1) * 2 * MLA_NOPE), 1.0)]
    pad_rows = lambda x: jnp.pad(x, ((0, 256 - MLA_Q_RANK), (0, 0))).astype(BF16)
    qn = jnp.pad(q_norm, (0, 256 - MLA_Q_RANK)).reshape(1, 256)
    return (pad_rows(_take_cols(w_uq, q_plan)), _take_cols(w_ukv, k_plan).astype(BF16),
            _take_cols(w_ukv, v_plan).astype(BF16), qn)


def _split_bf16(a):
    hi = a.astype(BF16)
    return hi, (a - hi.astype(F32)).astype(BF16)


def _dot_sel(a, sel):
    hi, lo = _split_bf16(a)
    sel = sel.astype(BF16)
    return _dot(hi, sel) + _dot(lo, sel)


def _dot_x3(a, b):
    a_hi, a_lo = _split_bf16(a)
    b_hi, b_lo = _split_bf16(b)
    return _dot(a_hi, b_hi) + _dot(a_hi, b_lo) + _dot(a_lo, b_hi)


def _s5_table_body(xy_ref, bb_ref, c_ref, ct_ref, mt_ref, bst_ref, cot_ref, a_ref):
    n, t, ch = S5_STATE, S5_CHUNK, S5_CH
    wide = 2 * S5_TAP
    xy = xy_ref[0, 0]
    tau_i = lax.broadcasted_iota(jnp.int32, (1, 128), 1)
    tau = tau_i.astype(F32)
    sel_row = lax.broadcasted_iota(jnp.int32, (128, 1), 0)

    def lag(width):
        return lax.broadcasted_iota(jnp.int32, (1, width), 1) >> 4

    def onehot(cond):
        return jnp.where(cond, 1.0, 0.0).astype(F32)
    j = lag(wide)
    e_z = (onehot((j <= t - 1) & (sel_row == t - 1 - j)), onehot((j >= t - 1) & (j <= 2 * t - 2) & (sel_row == j - (t - 1))))
    jc = lag(S5_TAP)
    e_c = (onehot(sel_row == jc + 1), onehot(sel_row == t - jc))
    ch_row = lax.broadcasted_iota(jnp.int32, (ch, 1), 0)
    tile_w = onehot((lax.broadcasted_iota(jnp.int32, (1, wide), 1) & (ch - 1)) == ch_row)
    tile_n = onehot((lax.broadcasted_iota(jnp.int32, (1, S5_TAP), 1) & (ch - 1)) == ch_row)

    z, cot_rows, klong = [], [], None
    for d in range(2):
        x, y = xy[:, 2 * d:2 * d + 1], xy[:, 2 * d + 1:2 * d + 2]
        mag = jnp.exp(jnp.where(tau_i <= t, tau, 0.0) * x)
        ang = jnp.where(tau_i <= t, tau, 0.0) * y
        p_re = jnp.where(tau_i <= t, mag * jnp.cos(ang), 0.0)
        p_im = jnp.where(tau_i <= t, mag * jnp.sin(ang), 0.0)
        a_ref[0, 0, 2 * d] = jnp.broadcast_to(p_re[:, t:t + 1], (n, 128))
        a_ref[0, 0, 2 * d + 1] = jnp.broadcast_to(p_im[:, t:t + 1], (n, 128))
        pz_re, pz_im = _dot_sel(p_re, e_z[d]), _dot_sel(p_im, e_z[d])
        b_re, b_im = _dot_sel(bb_ref[0, 0, 2 * d], tile_w), _dot_sel(bb_ref[0, 0, 2 * d + 1], tile_w)
        z_re, z_im = pz_re * b_re - pz_im * b_im, pz_re * b_im + pz_im * b_re
        z += [z_re, z_im]
        part = _dot_x3(c_ref[0, 0, 2 * d], z_re) - _dot_x3(c_ref[0, 0, 2 * d + 1], z_im)
        klong = part if klong is None else klong + part
        pc_re, pc_im = _dot_sel(p_re, e_c[d]), _dot_sel(p_im, e_c[d])
        c_re, c_im = _dot_sel(ct_ref[0, 0, 2 * d], tile_n), _dot_sel(ct_ref[0, 0, 2 * d + 1], tile_n)
        cot_rows += [c_re * pc_re - c_im * pc_im, -(c_re * pc_im + c_im * pc_re)]
    for tt in range(t):
        off = (t - 1 - tt) * ch
        win = klong if off == 0 else pltpu.roll(klong, wide - off, 1)
        mt_ref[0, 0, tt * ch:(tt + 1) * ch, :] = win[:, :S5_TAP].astype(BF16)
    back = pltpu.roll(z[2], wide - (t - 1) * ch, 1), pltpu.roll(z[3], wide - (t - 1) * ch, 1)
    for k, rows in enumerate((z[0], z[1], back[0], back[1])):
        bst_ref[0, 0, k * n:(k + 1) * n, :] = rows[:, :S5_TAP].astype(BF16)
    cot_ref[0, 0] = jnp.concatenate(cot_rows, axis=0).T.astype(BF16)


def _s5_tables(a_re, a_im, log_dt, b_re, b_im, c_re, c_im):
    nl, g, n, ch = a_re.shape[0], S5_GROUPS, S5_STATE, S5_CH
    step = jnp.exp(log_dt)[..., None]
    mag = jnp.exp(a_re * step)
    ab_re, ab_im = mag * jnp.cos(a_im * step), mag * jnp.sin(a_im * step)
    den = a_re * a_re + a_im * a_im
    f_re = ((ab_re - 1.0) * a_re + ab_im * a_im) / den
    f_im = (ab_im * a_re - (ab_re - 1.0) * a_im) / den
    bb_re = f_re[..., None] * b_re - f_im[..., None] * b_im
    bb_im = f_re[..., None] * b_im + f_im[..., None] * b_re
    by_group = lambda x: jnp.moveaxis(x, 1, 2)
    pair = lambda re, im: jnp.stack([by_group(re), by_group(im)], axis=3).reshape((nl, g, 4) + re.shape[3:])
    xy = jnp.stack([by_group(a_re * step), by_group(a_im * step)], axis=3).reshape(nl, g, 4, n)
    xy = jnp.pad(jnp.swapaxes(xy, 2, 3), ((0, 0), (0, 0), (0, 0), (0, 4)))
    mat = pl.BlockSpec((1, 1, S5_TAP, S5_TAP), lambda l, i: (l, i, 0, 0))
    return pl.pallas_call(
        _s5_table_body,
        grid=(nl, g),
        in_specs=[pl.BlockSpec((1, 1, n, 8), lambda l, i: (l, i, 0, 0)),
                  pl.BlockSpec((1, 1, 4, n, ch), lambda l, i: (l, i, 0, 0, 0)),
                  pl.BlockSpec((1, 1, 4, ch, n), lambda l, i: (l, i, 0, 0, 0)),
                  pl.BlockSpec((1, 1, 4, n, ch), lambda l, i: (l, i, 0, 0, 0))],
        out_specs=[mat, mat, mat, pl.BlockSpec((1, 1, 4, n, 128), lambda l, i: (l, i, 0, 0, 0))],
        out_shape=[jax.ShapeDtypeStruct((nl, g, S5_TAP, S5_TAP), BF16)] * 3
        + [jax.ShapeDtypeStruct((nl, g, 4, n, 128), F32)],
        compiler_params=_cparams("parallel", "parallel"),
        name="s5_tables",
    )(xy, pair(bb_re, bb_im), pair(c_re, c_im), pair(jnp.swapaxes(c_re, -1, -2), jnp.swapaxes(c_im, -1, -2)))


def _s5_chunk_lanes(u):
    return u.reshape(S5_NCHUNK, S5_CHUNK, BRANCH).transpose(1, 2, 0)


def _s5_token_rows(y):
    return y.transpose(2, 0, 1).reshape(N_TOK, BRANCH)


def kernel(x_prompt, x_sample, cache_diff_k, cache_diff_v, state_s5, state_hgrn, cache_mla_ckv, cache_mla_krope, c, c_ctx, w_mod, b_mod, w_in, w_out, da_lambda, da_norm, s5_a_re, s5_a_im, s5_log_dt, s5_b_re, s5_b_im, s5_c_re, s5_c_im, s5_d, s5_w_glu, hg_lb, hg_norm, mla_q_norm, mla_w_uq, mla_kv_norm, mla_w_ukv, final_norm):
    lb_w = jax.nn.softmax(hg_lb.astype(F32), axis=0)
    lb_all = jnp.cumsum(lb_w, axis=0) - lb_w[0:1]
    c_rows = jnp.concatenate([c_ctx[None], c, jnp.zeros((8 - 1 - N_LAT_SEQ, D_MODEL), F32)], axis=0)
    mods = _modulation(c_rows, w_mod, b_mod)
    da_tabs, mla_tabs = _rope_tables()
    xs = (x_prompt.reshape(N_CTX, D_MODEL), x_sample.reshape(N_LAT, D_MODEL))
    new_k, new_v, new_s5, new_hg, new_ckv, new_kr = [], [], [], [], [], []
    s5_tabs = _s5_tables(s5_a_re, s5_a_im, s5_log_dt, s5_b_re, s5_b_im, s5_c_re, s5_c_im)
    w_abc, w_d = _in_proj_weights(w_in)
    w_out_bf = w_out.astype(BF16)
    for l in range(DEPTH):
        mod = mods[l, :3].reshape(3, 3, D_MODEL)
        z_a, z_b, z_c, z_d = _in_proj(xs, mod, w_abc, w_d, l)

        lam_init = 0.8 - 0.6 * math.exp(-0.3 * l)
        kv_lat = _da_latent_kv(z_a, da_tabs,
                               cache_diff_k[:, l].reshape(N_LAT_SEQ, PAST_LEN, BRANCH),
                               cache_diff_v[:, l].reshape(N_LAT_SEQ, PAST_LEN, BRANCH))
        a_out = _da_attention(z_a, da_lambda[l], da_norm[l], lam_init, da_tabs, kv_lat)
        new_k.append(z_a[:N_CTX, 256:512].reshape(N_CTX_SEQ, CTX_LEN, DA_HEADS, 2 * DA_QK))
        new_v.append(z_a[:N_CTX, 512:768].reshape(N_CTX_SEQ, CTX_LEN, DA_HEADS, 2 * DA_QK))

        h0 = state_s5[:, l].transpose(2, 1, 4, 3, 0).reshape(S5_GROUPS, 4, S5_STATE, N_LAT_SEQ)
        h0 = jnp.pad(h0, ((0, 0), (0, 0), (0, 0), (0, 128 - N_LAT_SEQ)))
        y_all, fin = _s5_scan(_s5_chunk_lanes(z_b[:, :BRANCH]).astype(BF16), *s5_tabs, h0, l)
        b_out = _s5_out(_s5_token_rows(y_all), z_b, s5_d[l].reshape(1, BRANCH), s5_w_glu[l].astype(BF16))
        fin = jnp.stack([fin[:, 0:2, :, S5_CTX_SEQ_CH - 1::S5_CTX_SEQ_CH], fin[:, 2:4, :, 0::S5_CTX_SEQ_CH]], axis=1)
        new_s5.append(fin.transpose(4, 1, 0, 3, 2))

        lb = jnp.concatenate([lb_all[l, 0].reshape(HG_HEADS, HG_DK), lb_all[l, 1].reshape(HG_HEADS, HG_DK)],
                             axis=-1).reshape(1, HG_W)
        head_eye = jnp.eye(HG_HEADS, dtype=F32)
        s0 = state_hgrn[:, l].transpose(0, 2, 1, 3, 4).reshape(N_LAT_SEQ, HG_HEADS, HG_HEAD_W, 1, HG_DK)
        s0 = (s0 * head_eye[None, :, None, :, None]).reshape(N_LAT_SEQ, HG_HEADS, HG_HEAD_W, BRANCH)
        s_f, s_b = _hg_states(z_c, lb, s0)
        c_out, s_fin = _hg_main(z_c, s_f, s_b, lb, jnp.tile(hg_norm[l].reshape(1, HG_DK), (1, HG_HEADS)))
        new_hg.append(s_fin.reshape(N_CTX_SEQ, HG_HEADS, 2, HG_DK, HG_DK).transpose(0, 2, 1, 3, 4))

        wq, wk, wv, qn = _mla_weights(mla_w_uq[l], mla_w_ukv[l], mla_q_norm[l])
        q, ckv_n, kr = _mla_prep(z_d, mla_tabs, qn, mla_kv_norm[l].reshape(1, MLA_KV_RANK), wq)
        k_ctx, v_ctx = _mla_kv(ckv_n, kr, wk, wv, N_CTX)
        kr_cache = jnp.pad(cache_mla_krope[:, l], ((0, 0), (0, 0), (MLA_NOPE, 128 - MLA_NOPE - MLA_ROPE)))
        lk = LAT_LEN + PAST_LEN
        ckv_all = jnp.concatenate([ckv_n[N_CTX:].reshape(N_LAT_SEQ, LAT_LEN, 128), cache_mla_ckv[:, l]], axis=1)
        kr_all = jnp.concatenate([kr[N_CTX:].reshape(N_LAT_SEQ, LAT_LEN, 128), kr_cache], axis=1)
        k_lat, v_lat = _mla_kv(ckv_all.reshape(N_LAT_SEQ * lk, 128), kr_all.reshape(N_LAT_SEQ * lk, 128),
                               wk, wv, N_LAT_SEQ * lk)
        d_out = _mla_attention(z_d, q, k_ctx, v_ctx, k_lat, v_lat)
        new_ckv.append(ckv_n[:N_CTX].reshape(N_CTX_SEQ, CTX_LEN, MLA_KV_RANK))
        new_kr.append(kr[:N_CTX, MLA_NOPE:MLA_NOPE + MLA_ROPE].reshape(N_CTX_SEQ, CTX_LEN, MLA_ROPE))

        xs = _out_proj(a_out, b_out, c_out, d_out, xs, mod, w_out_bf, l,
                       final_norm.reshape(1, D_MODEL), final=(l == DEPTH - 1))
        xs = tuple(xs) if l == DEPTH - 1 else (xs,)
    y_prompt = xs[0].reshape(N_CTX_SEQ, CTX_LEN, D_MODEL)
    y_sample = xs[1].reshape(N_LAT_SEQ, LAT_LEN, D_MODEL)
    st = lambda parts: jnp.stack(parts, axis=1)
    return (y_prompt, y_sample, st(new_k), st(new_v), st(new_s5), st(new_hg), st(new_ckv), st(new_kr))
```

```python
import functools
import math

import numpy as np

import jax
import jax.numpy as jnp
from jax import lax
from jax.experimental import pallas as pl
from jax.experimental.pallas import tpu as pltpu

F32 = jnp.float32
BF16 = jnp.bfloat16

D_MODEL = 1024
DEPTH = 2
N_CTX_SEQ = 16
CTX_LEN = 256
N_LAT_SEQ = 2
LAT_LEN = 2048
PAST_LEN = 256
GRID_W = 64
N_CTX = N_CTX_SEQ * CTX_LEN
N_LAT = N_LAT_SEQ * LAT_LEN
N_TOK = N_CTX + N_LAT
BRANCH = 256
EPS = 1e-6
ROPE_BASE = 10000.0
ROW_TILE = 256
LAT_TILES = LAT_LEN // ROW_TILE
N_TILES = N_TOK // ROW_TILE
CTX_TILES = N_CTX // ROW_TILE
VMEM_LIMIT = 48 * 1024 * 1024
LAT_Q_TILE = 512
LAT_Q_TILES = LAT_LEN // LAT_Q_TILE

DA_HEADS = 4
DA_QK = 32
MLA_HEADS = 4
MLA_NOPE = 64
MLA_ROPE = 32
MLA_Q_RANK = 192
MLA_KV_RANK = 128
S5_GROUPS = 16
S5_CH = 16
S5_STATE = 64
S5_CHUNK = 16
HG_HEADS = 4
HG_DK = 64

W_A = 1024
W_B = 512
W_C = 1536
W_D = 768
W_ABC = W_A + W_B + W_C


def _cparams(*sem):
    return pltpu.CompilerParams(dimension_semantics=sem, vmem_limit_bytes=VMEM_LIMIT)


def _tile_seq(i):
    return jnp.where(i < CTX_TILES, 0, 1 + (i - CTX_TILES) // LAT_TILES)


def _silu(x):
    return x * (1.0 / (1.0 + jnp.exp(-x)))


def _dot(a, b):
    return jnp.dot(a, b, preferred_element_type=F32)


def _dot_nt(a, b):
    return lax.dot_general(a, b, (((1,), (1,)), ((), ())), preferred_element_type=F32)


def _mod_body(c_ref, w_ref, b_ref, o_ref):
    c = _silu(c_ref[...]).astype(BF16)
    o_ref[0] = _dot(c, w_ref[0].astype(BF16)) + b_ref[0]


def _modulation(c_rows, w_mod, b_mod):
    tn = 768
    return pl.pallas_call(
        _mod_body,
        grid=(DEPTH, 3 * D_MODEL // tn),
        in_specs=[pl.BlockSpec((8, D_MODEL), lambda l, j: (0, 0)),
                  pl.BlockSpec((1, D_MODEL, tn), lambda l, j: (l, 0, j)),
                  pl.BlockSpec((1, 1, tn), lambda l, j: (l, 0, j))],
        out_specs=pl.BlockSpec((1, 8, tn), lambda l, j: (l, 0, j)),
        out_shape=jax.ShapeDtypeStruct((DEPTH, 8, 3 * D_MODEL), F32),
        compiler_params=_cparams("parallel", "parallel"),
        name="modulation",
    )(c_rows, w_mod, b_mod.reshape(DEPTH, 1, 3 * D_MODEL))


def _split_rows(i, ctx_ref, lat_ref):
    return jnp.where(i < CTX_TILES, ctx_ref[...], lat_ref[...])


def _ctx_tile_spec(w):
    return pl.BlockSpec((ROW_TILE, w), lambda i: (jnp.minimum(i, CTX_TILES - 1), 0))


def _lat_tile_spec(w):
    return pl.BlockSpec((ROW_TILE, w), lambda i: (jnp.maximum(i - CTX_TILES, 0), 0))


def _in_proj_body(*refs, split):
    if split:
        xc_ref, xl_ref, mod_ref, w_ref, wd_ref, oa, ob, oc, od, ok, ov = refs
        x = _split_rows(pl.program_id(0), xc_ref, xl_ref)
    else:
        x_ref, mod_ref, w_ref, wd_ref, oa, ob, oc, od, ok, ov = refs
        x = x_ref[...]
    xn = x * lax.rsqrt(jnp.mean(x * x, axis=-1, keepdims=True) + EPS)
    mod = mod_ref[0]
    h = (xn * (1.0 + mod[1:2]) + mod[0:1]).astype(BF16)
    off = 0
    for o in (oa, ob, oc):
        w = o.shape[-1]
        o[...] = _dot(h, w_ref[0, :, off:off + w])
        off += w
    od[...] = _dot(h, wd_ref[0])

    @pl.when(pl.program_id(0) < CTX_TILES)
    def _():
        ok[...] = oa[:, BRANCH:2 * BRANCH]
        ov[...] = oa[:, 2 * BRANCH:3 * BRANCH]


def _in_proj(xs, mod, w_abc, w_d, l):
    widths = (W_A, W_B, W_C, W_D)
    split = len(xs) == 2
    x_specs = ([_ctx_tile_spec(D_MODEL), _lat_tile_spec(D_MODEL)] if split
               else [pl.BlockSpec((ROW_TILE, D_MODEL), lambda i: (i, 0))])
    return pl.pallas_call(
        functools.partial(_in_proj_body, split=split),
        grid=(N_TILES,),
        in_specs=x_specs + [pl.BlockSpec((1, 3, D_MODEL), lambda i: (_tile_seq(i), 0, 0)),
                            pl.BlockSpec((1, D_MODEL, W_ABC), lambda i: (l, 0, 0)),
                            pl.BlockSpec((1, D_MODEL, W_D), lambda i: (l, 0, 0))],
        out_specs=[pl.BlockSpec((ROW_TILE, w), lambda i: (i, 0)) for w in widths] + [_ctx_tile_spec(BRANCH)] * 2,
        out_shape=[jax.ShapeDtypeStruct((N_TOK, w), F32) for w in widths]
        + [jax.ShapeDtypeStruct((N_CTX, BRANCH), F32)] * 2,
        compiler_params=_cparams("arbitrary"),
        name="in_proj",
    )(*xs, mod, w_abc, w_d)


def _arrange_body(w_ref, pdup_ref, pint_ref, o_ref):
    w = w_ref[0].astype(BF16)

    def perm(x, p_ref):
        return _dot(x, p_ref[...]).astype(BF16)
    o_ref[0] = jnp.concatenate(
        [w[:, 0:1536],
         perm(w[:, 1536:1792], pdup_ref), perm(w[:, 1792:2304], pint_ref), w[:, 2304:2816]], axis=-1)


def _arrange_w_in(w_in, pdup, pint):
    rows = 128
    const = lambda a: pl.BlockSpec(a.shape, lambda l, i: (0, 0))
    return pl.pallas_call(
        _arrange_body,
        grid=(DEPTH, D_MODEL // rows),
        in_specs=[pl.BlockSpec((1, rows, w_in.shape[-1]), lambda l, i: (l, i, 0)), const(pdup), const(pint)],
        out_specs=pl.BlockSpec((1, rows, W_ABC), lambda l, i: (l, i, 0)),
        out_shape=jax.ShapeDtypeStruct((DEPTH, D_MODEL, W_ABC), BF16),
        compiler_params=_cparams("parallel", "parallel"),
        name="arrange_w_in",
    )(w_in, pdup, pint)


def _out_proj_body(*refs, split_in, final):
    ac_ref, al_ref, b_ref, c_ref, dc_ref, dl_ref = refs[:6]
    i = pl.program_id(0)
    if split_in:
        xc_ref, xl_ref, mod_ref, w_ref, fn_ref = refs[6:11]
        x = _split_rows(i, xc_ref, xl_ref)
    else:
        x_ref, mod_ref, w_ref, fn_ref = refs[6:10]
        x = x_ref[...]
    branches = (_split_rows(i, ac_ref, al_ref), b_ref[...], c_ref[...], _split_rows(i, dc_ref, dl_ref))
    acc = None
    for j, r in enumerate(branches):
        t = _dot(r.astype(BF16), w_ref[0, j * BRANCH:(j + 1) * BRANCH, :])
        acc = t if acc is None else acc + t
    x = x + mod_ref[0][2:3] * acc
    if not final:
        refs[-1][...] = x
        return
    y = x * lax.rsqrt(jnp.mean(x * x, axis=-1, keepdims=True) + EPS) * fn_ref[...]
    yc_ref, yl_ref = refs[-2:]

    @pl.when(i < CTX_TILES)
    def _():
        yc_ref[...] = y

    @pl.when(i >= CTX_TILES)
    def _():
        yl_ref[...] = y


def _out_proj(a, b, c, d, xs, mod, w_out, l, final_norm, final):
    br = pl.BlockSpec((ROW_TILE, BRANCH), lambda i: (i, 0))
    pair = [_ctx_tile_spec(BRANCH), _lat_tile_spec(BRANCH)]
    split_in = len(xs) == 2
    x_specs = ([_ctx_tile_spec(D_MODEL), _lat_tile_spec(D_MODEL)] if split_in
               else [pl.BlockSpec((ROW_TILE, D_MODEL), lambda i: (i, 0))])
    if final:
        out_specs = [_ctx_tile_spec(D_MODEL), _lat_tile_spec(D_MODEL)]
        out_shape = [jax.ShapeDtypeStruct((N_CTX, D_MODEL), F32), jax.ShapeDtypeStruct((N_LAT, D_MODEL), F32)]
    else:
        out_specs = pl.BlockSpec((ROW_TILE, D_MODEL), lambda i: (i, 0))
        out_shape = jax.ShapeDtypeStruct((N_TOK, D_MODEL), F32)
    return pl.pallas_call(
        functools.partial(_out_proj_body, split_in=split_in, final=final),
        grid=(N_TILES,),
        in_specs=pair + [br, br] + pair + x_specs + [
            pl.BlockSpec((1, 3, D_MODEL), lambda i: (_tile_seq(i), 0, 0)),
            pl.BlockSpec((1, D_MODEL, D_MODEL), lambda i: (l, 0, 0)),
            pl.BlockSpec((1, D_MODEL), lambda i: (0, 0))],
        out_specs=out_specs,
        out_shape=out_shape,
        compiler_params=_cparams("arbitrary"),
        name="out_proj",
    )(*a, b, c, *d, *xs, mod, w_out, final_norm)


LOG2E = 1.4426950408889634


def _exp2_rows(s):
    e = jnp.exp2(s - jnp.max(s, axis=-1, keepdims=True))
    return e, jnp.sum(e, axis=-1, keepdims=True)


def _rope(x, cos, sin_lo, sin_hi):
    w = x.shape[-1]
    return x * cos + pltpu.roll(x, w - 8, 1) * sin_lo + pltpu.roll(x, 8, 1) * sin_hi


def _da_kv_body(k_ref, v_ref, cos_ref, slo_ref, shi_ref, ck_ref, cv_ref, ko_ref, vo_ref):
    j = pl.program_id(1)

    @pl.when(j < LAT_TILES)
    def _():
        ko_ref[0] = _rope(k_ref[...], cos_ref[...], slo_ref[...], shi_ref[...]).astype(BF16)
        vo_ref[0] = v_ref[...].astype(BF16)

    @pl.when(j == LAT_TILES)
    def _():
        ko_ref[0] = ck_ref[0].astype(BF16)
        vo_ref[0] = cv_ref[0].astype(BF16)


def _da_latent_kv(z_a, tabs, cache_k, cache_v):
    def rows(col):
        return pl.BlockSpec(
            (ROW_TILE, BRANCH),
            lambda b, j: (CTX_TILES + b * LAT_TILES + jnp.minimum(j, LAT_TILES - 1), col))
    tab = pl.BlockSpec((ROW_TILE, BRANCH), lambda b, j: (jnp.minimum(j, LAT_TILES - 1), 0))
    cache = pl.BlockSpec((1, PAST_LEN, BRANCH), lambda b, j: (b, 0, 0))
    out = pl.BlockSpec((1, ROW_TILE, BRANCH), lambda b, j: (b, j, 0))
    shp = jax.ShapeDtypeStruct((N_LAT_SEQ, LAT_LEN + PAST_LEN, BRANCH), BF16)
    return pl.pallas_call(
        _da_kv_body,
        grid=(N_LAT_SEQ, LAT_TILES + 1),
        in_specs=[rows(1), rows(2), tab, tab, tab, cache, cache],
        out_specs=[out, out],
        out_shape=[shp, shp],
        compiler_params=_cparams("parallel", "parallel"),
        name="da_kv",
    )(z_a, z_a, *tabs, cache_k, cache_v)


def _da_attn_body(lam_ref, ng_ref, q_ref, *rest, rope, lam_init):
    if rope:
        cos_ref, slo_ref, shi_ref, k_ref, v_ref, g_ref, o_ref = rest
        q = _rope(q_ref[...], cos_ref[...], slo_ref[...], shi_ref[...])
        k = k_ref[0]
        v = v_ref[0]
    else:
        k_ref, v_ref, g_ref, o_ref = rest
        q = q_ref[...]
        k = k_ref[...].astype(BF16)
        v = v_ref[...].astype(BF16)
    q = q * (DA_QK ** -0.5 * LOG2E)
    lv = lam_ref[...]
    lam = (jnp.exp(jnp.sum(lv[0:1] * lv[1:2], axis=-1, keepdims=True))
           - jnp.exp(jnp.sum(lv[2:3] * lv[3:4], axis=-1, keepdims=True)) + lam_init)
    lane = lax.broadcasted_iota(jnp.int32, (1, BRANCH), 1)
    acc = jnp.zeros(q.shape, F32)
    for h in range(DA_HEADS):
        q1 = jnp.where(lane // DA_QK == 2 * h, q, 0.0).astype(BF16)
        q2 = jnp.where(lane // DA_QK == 2 * h + 1, q, 0.0).astype(BF16)
        e1, l1 = _exp2_rows(_dot_nt(q1, k))
        e2, l2 = _exp2_rows(_dot_nt(q2, k))
        a = (e1 - (lam * l1 / l2) * e2).astype(BF16)
        acc = jnp.where(lane // (2 * DA_QK) == h, _dot(a, v) * (1.0 / l1), acc)
    sq = acc * acc
    ms = jnp.zeros(q.shape, F32)
    for h in range(DA_HEADS):
        hm = lane // (2 * DA_QK) == h
        ms = jnp.where(hm, jnp.sum(jnp.where(hm, sq, 0.0), axis=-1, keepdims=True), ms)
    o = acc * lax.rsqrt(ms * (1.0 / (2 * DA_QK)) + EPS) * (ng_ref[...] * (1.0 - lam_init))
    o_ref[...] = o * _silu(g_ref[...])


def _da_attention(z_a, lam_vec, norm_g, lam_init, tabs, kv_lat):
    ng = jnp.tile(norm_g.reshape(1, 2 * DA_QK), (1, DA_HEADS))
    small = [pl.BlockSpec((4, DA_QK), lambda *_: (0, 0)), pl.BlockSpec((1, BRANCH), lambda *_: (0, 0))]

    def col(c):
        return pl.BlockSpec((ROW_TILE, BRANCH), lambda i: (i, c))
    ctx = pl.pallas_call(
        functools.partial(_da_attn_body, rope=False, lam_init=lam_init),
        grid=(CTX_TILES,),
        in_specs=small + [col(0), col(1), col(2), col(3)],
        out_specs=pl.BlockSpec((ROW_TILE, BRANCH), lambda i: (i, 0)),
        out_shape=jax.ShapeDtypeStruct((N_CTX, BRANCH), F32),
        compiler_params=_cparams("parallel"),
        name="da_attn_ctx",
    )(lam_vec, ng, z_a, z_a, z_a, z_a)

    def lcol(c):
        return pl.BlockSpec((LAT_Q_TILE, BRANCH), lambda b, j: (N_CTX // LAT_Q_TILE + b * LAT_Q_TILES + j, c))
    tab = pl.BlockSpec((LAT_Q_TILE, BRANCH), lambda b, j: (j, 0))
    kvs = pl.BlockSpec((1, LAT_LEN + PAST_LEN, BRANCH), lambda b, j: (b, 0, 0))
    lat = pl.pallas_call(
        functools.partial(_da_attn_body, rope=True, lam_init=lam_init),
        grid=(N_LAT_SEQ, LAT_Q_TILES),
        in_specs=small + [lcol(0), tab, tab, tab, kvs, kvs, lcol(3)],
        out_specs=pl.BlockSpec((LAT_Q_TILE, BRANCH), lambda b, j: (b * LAT_Q_TILES + j, 0)),
        out_shape=jax.ShapeDtypeStruct((N_LAT, BRANCH), F32),
        compiler_params=_cparams("parallel", "parallel"),
        name="da_attn_lat",
    )(lam_vec, ng, z_a, *tabs, kv_lat[0], kv_lat[1], z_a)
    return ctx, lat


MLA_HEAD_PAD = 128
MLA_QW = MLA_HEADS * MLA_HEAD_PAD


def _mla_prep_body(cq_ref, ckv_ref, kr_ref, ck_t, sk_lo, sk_hi, qn_ref, kvn_ref, wq_ref, q_out, ckv_out, kr_out):
    cq = cq_ref[...]
    ms = jnp.sum(cq * cq, axis=-1, keepdims=True) * (1.0 / MLA_Q_RANK)
    qn = (cq * lax.rsqrt(ms + EPS) * qn_ref[...]).astype(BF16)
    heads = lambda t: jnp.concatenate([t[...]] * MLA_HEADS, axis=-1)
    q = _rope(_dot(qn, wq_ref[...]), heads(ck_t), heads(sk_lo), heads(sk_hi))
    q_out[...] = (q * ((MLA_NOPE + MLA_ROPE) ** -0.5 * LOG2E)).astype(BF16)
    ckv = ckv_ref[...]
    ckv_out[...] = ckv * lax.rsqrt(jnp.mean(ckv * ckv, axis=-1, keepdims=True) + EPS) * kvn_ref[...]
    kr_out[...] = _rope(kr_ref[...], ck_t[...], sk_lo[...], sk_hi[...])


def _mla_prep(z_d, tabs, q_norm_pad, kv_norm, wq):
    def tab(w):
        return pl.BlockSpec(
            (ROW_TILE, w), lambda i: (jnp.where(i < CTX_TILES, LAT_TILES, (i - CTX_TILES) % LAT_TILES), 0))

    def col(w, c):
        return pl.BlockSpec((ROW_TILE, w), lambda i: (i, c))

    def const(shape):
        return pl.BlockSpec(shape, lambda i: (0, 0))
    return pl.pallas_call(
        _mla_prep_body,
        grid=(N_TILES,),
        in_specs=[col(256, 0), col(128, 2), col(128, 3),
                  tab(128), tab(128), tab(128),
                  const((1, 256)), const((1, 128)), const((256, MLA_QW))],
        out_specs=[col(MLA_QW, 0), col(128, 0), col(128, 0)],
        out_shape=[jax.ShapeDtypeStruct((N_TOK, MLA_QW), BF16),
                   jax.ShapeDtypeStruct((N_TOK, 128), F32),
                   jax.ShapeDtypeStruct((N_TOK, 128), F32)],
        compiler_params=_cparams("parallel"),
        name="mla_prep",
    )(z_d, z_d, z_d, *tabs, q_norm_pad, kv_norm, wq)


def _mla_kv_math(ckv, kr, wk_ref, wv_ref, k_out, v_out):
    c = ckv.astype(BF16)
    k_out[...] = (_dot(c, wk_ref[...]) + jnp.concatenate([kr] * MLA_HEADS, axis=-1)).astype(BF16).reshape(k_out.shape)
    v_out[...] = _dot(c, wv_ref[...]).astype(BF16).reshape(v_out.shape)


def _mla_kv_ctx_body(ckv_ref, kr_ref, wk_ref, wv_ref, k_out, v_out):
    _mla_kv_math(ckv_ref[...], kr_ref[...], wk_ref, wv_ref, k_out, v_out)


def _mla_kv_lat_body(ckv_ref, kr_ref, cckv_ref, ckr_ref, wk_ref, wv_ref, k_out, v_out):
    j = pl.program_id(1)

    @pl.when(j < LAT_TILES)
    def _():
        _mla_kv_math(ckv_ref[...], kr_ref[...], wk_ref, wv_ref, k_out, v_out)

    @pl.when(j == LAT_TILES)
    def _():
        _mla_kv_math(cckv_ref[0], ckr_ref[0], wk_ref, wv_ref, k_out, v_out)


def _mla_kv(ckv, kr, cache_ckv, cache_kr, wk, wv):
    weights = [pl.BlockSpec((128, MLA_QW), lambda *_: (0, 0)), pl.BlockSpec((128, BRANCH), lambda *_: (0, 0))]
    k_ctx, v_ctx = pl.pallas_call(
        _mla_kv_ctx_body,
        grid=(CTX_TILES,),
        in_specs=[pl.BlockSpec((ROW_TILE, 128), lambda i: (i, 0)), pl.BlockSpec((ROW_TILE, 128), lambda i: (i, 0))] + weights,
        out_specs=[pl.BlockSpec((ROW_TILE, MLA_QW), lambda i: (i, 0)),
                   pl.BlockSpec((ROW_TILE, BRANCH), lambda i: (i, 0))],
        out_shape=[jax.ShapeDtypeStruct((N_CTX, MLA_QW), BF16), jax.ShapeDtypeStruct((N_CTX, BRANCH), BF16)],
        compiler_params=_cparams("parallel"),
        name="mla_kv_ctx",
    )(ckv, kr, wk, wv)
    rows = pl.BlockSpec((ROW_TILE, 128), lambda b, j: (CTX_TILES + b * LAT_TILES + jnp.minimum(j, LAT_TILES - 1), 0))
    cache = pl.BlockSpec((1, PAST_LEN, 128), lambda b, j: (b, 0, 0))
    lk = LAT_LEN + PAST_LEN
    k_lat, v_lat = pl.pallas_call(
        _mla_kv_lat_body,
        grid=(N_LAT_SEQ, LAT_TILES + 1),
        in_specs=[rows, rows, cache, cache] + weights,
        out_specs=[pl.BlockSpec((1, ROW_TILE, MLA_QW), lambda b, j: (b, j, 0)),
                   pl.BlockSpec((1, ROW_TILE, BRANCH), lambda b, j: (b, j, 0))],
        out_shape=[jax.ShapeDtypeStruct((N_LAT_SEQ, lk, MLA_QW), BF16), jax.ShapeDtypeStruct((N_LAT_SEQ, lk, BRANCH), BF16)],
        compiler_params=_cparams("parallel", "parallel"),
        name="mla_kv_lat",
    )(ckv, kr, cache_ckv, cache_kr, wk, wv)
    return k_ctx, v_ctx, k_lat, v_lat


def _mla_attn_body(q_ref, k_ref, v_ref, g_ref, o_ref):
    q = q_ref[...]
    k = k_ref[...].reshape(-1, MLA_QW)
    v = v_ref[...].reshape(-1, BRANCH)
    lane = lax.broadcasted_iota(jnp.int32, (1, BRANCH), 1)
    acc = jnp.zeros((q.shape[0], BRANCH), F32)
    for h in range(MLA_HEADS):
        sl = slice(h * MLA_HEAD_PAD, (h + 1) * MLA_HEAD_PAD)
        e, l = _exp2_rows(_dot_nt(q[:, sl], k[:, sl]))
        acc = jnp.where(lane // 64 == h, _dot(e.astype(BF16), v) * (1.0 / l), acc)
    o_ref[...] = acc * _silu(g_ref[...])


def _mla_attention(z_d, q, k_ctx, v_ctx, k_lat, v_lat):
    ctx = pl.pallas_call(
        _mla_attn_body,
        grid=(CTX_TILES,),
        in_specs=[pl.BlockSpec((ROW_TILE, MLA_QW), lambda i: (i, 0)),
                  pl.BlockSpec((ROW_TILE, MLA_QW), lambda i: (i, 0)),
                  pl.BlockSpec((ROW_TILE, BRANCH), lambda i: (i, 0)),
                  pl.BlockSpec((ROW_TILE, BRANCH), lambda i: (i, 2))],
        out_specs=pl.BlockSpec((ROW_TILE, BRANCH), lambda i: (i, 0)),
        out_shape=jax.ShapeDtypeStruct((N_CTX, BRANCH), F32),
        compiler_params=_cparams("parallel"),
        name="mla_attn_ctx",
    )(q, k_ctx, v_ctx, z_d)
    lk = LAT_LEN + PAST_LEN
    lat = pl.pallas_call(
        _mla_attn_body,
        grid=(N_LAT_SEQ, LAT_Q_TILES),
        in_specs=[pl.BlockSpec((LAT_Q_TILE, MLA_QW), lambda b, j: (N_CTX // LAT_Q_TILE + b * LAT_Q_TILES + j, 0)),
                  pl.BlockSpec((1, lk, MLA_QW), lambda b, j: (b, 0, 0)),
                  pl.BlockSpec((1, lk, BRANCH), lambda b, j: (b, 0, 0)),
                  pl.BlockSpec((LAT_Q_TILE, BRANCH), lambda b, j: (N_CTX // LAT_Q_TILE + b * LAT_Q_TILES + j, 2))],
        out_specs=pl.BlockSpec((LAT_Q_TILE, BRANCH), lambda b, j: (b * LAT_Q_TILES + j, 0)),
        out_shape=jax.ShapeDtypeStruct((N_LAT, BRANCH), F32),
        compiler_params=_cparams("parallel", "parallel"),
        name="mla_attn_lat",
    )(q, k_lat, v_lat, z_d)
    return ctx, lat


S5_TAP = S5_CHUNK * S5_CH
S5_NCHUNK = N_TOK // S5_CHUNK
S5_CTX_CH = N_CTX // S5_CHUNK
S5_CTX_SEQ_CH = CTX_LEN // S5_CHUNK
S5_LAT_SEQ_CH = LAT_LEN // S5_CHUNK
S5_SCAN_STEPS = S5_LAT_SEQ_CH.bit_length() - 1


def _s5_body(x_ref, mt_ref, bst_ref, cot_ref, a_ref, h0_ref, y_ref, fin_ref):
    x = x_ref[...].reshape(S5_TAP, S5_NCHUNK)
    y = _dot(mt_ref[0, 0], x)
    s = _dot(bst_ref[0, 0], x)
    lane = lax.broadcasted_iota(jnp.int32, (1, S5_NCHUNK), 1)
    is_lat = lane >= S5_CTX_CH
    pos_f = jnp.where(is_lat, (lane - S5_CTX_CH) & (S5_LAT_SEQ_CH - 1), lane & (S5_CTX_SEQ_CH - 1))
    pos_b = jnp.where(is_lat, S5_LAT_SEQ_CH - 1, S5_CTX_SEQ_CH - 1) - pos_f
    hin = []
    for d in range(2):
        n = S5_STATE
        sre, sim = s[2 * d * n:(2 * d + 1) * n], s[(2 * d + 1) * n:(2 * d + 2) * n]
        are = jnp.concatenate([a_ref[0, 0, 2 * d]] * (S5_NCHUNK // 128), axis=-1)
        aim = jnp.concatenate([a_ref[0, 0, 2 * d + 1]] * (S5_NCHUNK // 128), axis=-1)
        pos = pos_f if d == 0 else pos_b
        h0r, h0i = jnp.zeros_like(sre), jnp.zeros_like(sre)
        for b in range(N_LAT_SEQ):
            first = S5_CTX_CH + b * S5_LAT_SEQ_CH + (0 if d == 0 else S5_LAT_SEQ_CH - 1)
            h0r = jnp.where(lane == first, h0_ref[0, 2 * d][:, b:b + 1], h0r)
            h0i = jnp.where(lane == first, h0_ref[0, 2 * d + 1][:, b:b + 1], h0i)
        xr = sre + are * h0r - aim * h0i
        xi = sim + are * h0i + aim * h0r
        pr, pi = are, aim
        for j in range(S5_SCAN_STEPS):
            sh = 1 << j
            shift = sh if d == 0 else S5_NCHUNK - sh
            rr, ri = pltpu.roll(xr, shift, 1), pltpu.roll(xi, shift, 1)
            ok = pos >= sh
            xr, xi = (xr + jnp.where(ok, pr * rr - pi * ri, 0.0), xi + jnp.where(ok, pr * ri + pi * rr, 0.0))
            pr, pi = pr * pr - pi * pi, 2.0 * pr * pi
        fin_ref[0, 2 * d] = xr[:, :S5_CTX_CH]
        fin_ref[0, 2 * d + 1] = xi[:, :S5_CTX_CH]
        one = 1 if d == 0 else S5_NCHUNK - 1
        hin.append(jnp.where(pos >= 1, pltpu.roll(xr, one, 1), h0r))
        hin.append(jnp.where(pos >= 1, pltpu.roll(xi, one, 1), h0i))
    y = y + _dot(cot_ref[0, 0], jnp.concatenate(hin, axis=0).astype(BF16))
    y_ref[...] = y.reshape(S5_CHUNK, S5_CH, S5_NCHUNK)


def _s5_scan(x_all, mt, bst, cot, a16, h0, l):
    g = S5_GROUPS
    sq = pl.BlockSpec((1, 1, S5_TAP, S5_TAP), lambda i: (l, i, 0, 0))
    st = pl.BlockSpec((1, 4, S5_STATE, 128), lambda i: (i, 0, 0, 0))
    return pl.pallas_call(
        _s5_body,
        grid=(g,),
        in_specs=[pl.BlockSpec((S5_CHUNK, S5_CH, S5_NCHUNK), lambda i: (0, i, 0)), sq, sq, sq,
                  pl.BlockSpec((1, 1, 4, S5_STATE, 128), lambda i: (l, i, 0, 0, 0)), st],
        out_specs=[pl.BlockSpec((S5_CHUNK, S5_CH, S5_NCHUNK), lambda i: (0, i, 0)),
                   pl.BlockSpec((1, 4, S5_STATE, S5_CTX_CH), lambda i: (i, 0, 0, 0))],
        out_shape=[jax.ShapeDtypeStruct((S5_CHUNK, BRANCH, S5_NCHUNK), F32),
                   jax.ShapeDtypeStruct((g, 4, S5_STATE, S5_CTX_CH), F32)],
        compiler_params=_cparams("parallel"),
        name="s5_scan",
    )(x_all, mt, bst, cot, a16, h0)


def _s5_out_body(y_ref, u_ref, g_ref, d_ref, w_ref, o_ref):
    y = u_ref[...] * d_ref[...] + y_ref[...]
    ge = 0.5 * y * (1.0 + jnp.tanh(0.7978845608028654 * (y + 0.044715 * (y * y * y))))
    gl = _dot(ge.astype(BF16), w_ref[...])
    o_ref[...] = gl[:, :BRANCH] * (1.0 / (1.0 + jnp.exp(-gl[:, BRANCH:]))) * _silu(g_ref[...])


def _s5_out(y_ssm, z_b, d_skip, w_glu):
    def col(c):
        return pl.BlockSpec((ROW_TILE, BRANCH), lambda i: (i, c))
    return pl.pallas_call(
        _s5_out_body,
        grid=(N_TILES,),
        in_specs=[col(0), col(0), col(1),
                  pl.BlockSpec((1, BRANCH), lambda i: (0, 0)),
                  pl.BlockSpec((BRANCH, 2 * BRANCH), lambda i: (0, 0))],
        out_specs=col(0),
        out_shape=jax.ShapeDtypeStruct((N_TOK, BRANCH), F32),
        compiler_params=_cparams("parallel"),
        name="s5_out",
    )(y_ssm, z_b, z_b, d_skip, w_glu)


HG_CHUNK = ROW_TILE
HG_W = 2 * HG_HEADS * HG_DK
HG_HEAD_W = 2 * HG_DK
HG_LAT_CHUNKS = LAT_LEN // HG_CHUNK
HG_CHUNKS = N_TOK // HG_CHUNK


def _hg_gates(z, lb):
    e = jnp.exp(-jnp.abs(z))
    r = 1.0 / (1.0 + e)
    sig_pos = jnp.where(z >= 0, r, e * r)
    sig_neg = jnp.where(z >= 0, e * r, r)
    return lb + (1.0 - lb) * sig_pos, (1.0 - lb) * sig_neg


def _bcast_row(x, period, r):
    n, w = x.shape
    if period >= 8:
        x3 = x.reshape(n // period, period, w)
        return jnp.broadcast_to(x3[:, r:r + 1, :], x3.shape).reshape(n, w)
    x3 = x.reshape(n // 8, 8, w)
    sub = lax.broadcasted_iota(jnp.int32, (1, 8, 1), 1)
    out = None
    for j in range(8 // period):
        b = jnp.broadcast_to(x3[:, j * period + r:j * period + r + 1, :], x3.shape)
        out = b if out is None else jnp.where(sub >= j * period, b, out)
    return out.reshape(n, w)


def _hg_scans(f, isb):
    n = f.shape[0]
    row = lax.broadcasted_iota(jnp.int32, (n, 1), 0)
    p, r = f, jnp.ones_like(f)
    levels = []
    h, sh = 1, 0
    while h < n:
        levels.append((h, sh, p, r))
        up = (row >> sh) & 1
        tot_p = jnp.where(isb == 1, _bcast_row(p, 2 * h, h), _bcast_row(p, 2 * h, h - 1))
        tot_r = jnp.where(isb == 1, _bcast_row(p, 2 * h, 0), _bcast_row(p, 2 * h, 2 * h - 1))
        p = p * jnp.where(up != isb, tot_p, 1.0)
        r = r * jnp.where(up == isb, tot_r, 1.0)
        h, sh = 2 * h, sh + 1
    return levels, p, r


def _hg_state_body(zf_ref, zb_ref, vf_ref, vb_ref, lb_ref, s0_ref, sf_out, sb_out, s_scr):
    i = pl.program_id(0)

    @pl.when(i % HG_LAT_CHUNKS == 0)
    def _():
        s_scr[...] = s0_ref[0]

    sf_out[0] = s_scr[:, 0:HG_DK, :]
    sb_out[0] = s_scr[:, HG_DK:, :]
    lane5 = lax.broadcasted_iota(jnp.int32, (1, HG_W), 1)
    isb = (lane5 >> 6) & 1
    z = jnp.where(isb == 1, zb_ref[...], zf_ref[...])
    f, k = _hg_gates(z, lb_ref[...])
    r, ptot = _hg_chunk_decay(f, isb)
    kt = k * r
    lane = lax.broadcasted_iota(jnp.int32, (1, BRANCH), 1)
    vf = vf_ref[...]
    vb = vb_ref[...]
    for hd in range(HG_HEADS):
        sl = slice(hd * HG_HEAD_W, (hd + 1) * HG_HEAD_W)
        kth = kt[:, sl].T.astype(BF16)
        hm = (lane >> 6) == hd
        d_f = _dot(kth, jnp.where(hm, vf, 0.0).astype(BF16))
        d_b = _dot(kth, jnp.where(hm, vb, 0.0).astype(BF16))
        ds = jnp.concatenate([d_f[:HG_DK], d_b[HG_DK:]], axis=0)
        pcol = jnp.broadcast_to(ptot[:, sl], (HG_HEAD_W, HG_HEAD_W)).T[:, 0:1]
        s_scr[hd] = s_scr[hd] * pcol + ds


def _hg_chunk_decay(f, isb):
    n = f.shape[0]
    row = lax.broadcasted_iota(jnp.int32, (n, 1), 0)
    dist = jnp.where(isb == 1, row, n - 1 - row)
    x = f
    sh = 1
    while sh < n:
        src = jnp.where(isb == 1, pltpu.roll(x, sh, 0), pltpu.roll(x, n - sh, 0))
        x = x * jnp.where(dist >= sh, src, 1.0)
        sh *= 2
    total = jnp.where(isb == 1, x[n - 1:n], x[0:1])
    nxt = jnp.where(isb == 1, pltpu.roll(x, 1, 0), pltpu.roll(x, n - 1, 0))
    return jnp.where(dist >= 1, nxt, 1.0), total


HG_LAT_STEPS = N_LAT_SEQ * HG_LAT_CHUNKS


def _hg_lat_rev(i):
    return (i // HG_LAT_CHUNKS) * HG_LAT_CHUNKS + (HG_LAT_CHUNKS - 1 - i % HG_LAT_CHUNKS)


def _hg_states(z_c, lb, s0):
    first = N_CTX_SEQ
    zz_f = pl.BlockSpec((HG_CHUNK, HG_W), lambda i: (first + i, 1))
    zz_b = pl.BlockSpec((HG_CHUNK, HG_W), lambda i: (first + _hg_lat_rev(i), 1))
    v_f = pl.BlockSpec((HG_CHUNK, BRANCH), lambda i: (first + i, 4))
    v_b = pl.BlockSpec((HG_CHUNK, BRANCH), lambda i: (first + _hg_lat_rev(i), 4))
    st = (HG_HEADS, HG_HEAD_W, BRANCH)
    half = (HG_HEADS, HG_DK, BRANCH)
    return pl.pallas_call(
        _hg_state_body,
        grid=(HG_LAT_STEPS,),
        in_specs=[zz_f, zz_b, v_f, v_b,
                  pl.BlockSpec((1, HG_W), lambda i: (0, 0)),
                  pl.BlockSpec((1,) + st, lambda i: (i // HG_LAT_CHUNKS, 0, 0, 0))],
        out_specs=[pl.BlockSpec((1,) + half, lambda i: (i, 0, 0, 0)),
                   pl.BlockSpec((1,) + half, lambda i: (_hg_lat_rev(i), 0, 0, 0))],
        out_shape=[jax.ShapeDtypeStruct((HG_LAT_STEPS,) + half, F32),
                   jax.ShapeDtypeStruct((HG_LAT_STEPS,) + half, F32)],
        scratch_shapes=[pltpu.VMEM(st, F32)],
        compiler_params=_cparams("arbitrary"),
        name="hg_states",
    )(z_c, z_c, z_c, z_c, lb, s0)


def _hg_main_body(qq_ref, zz_ref, v_ref, g_ref, sf_ref, sb_ref, lb_ref, ng_ref, o_ref, fin_ref):
    n = HG_CHUNK
    i = pl.program_id(0)
    qq = qq_ref[...]
    lane5 = lax.broadcasted_iota(jnp.int32, (1, HG_W), 1)
    isb = (lane5 >> 6) & 1
    f, k = _hg_gates(zz_ref[...], lb_ref[...])
    levels, pfull, rfull = _hg_scans(f, isb)
    row = lax.broadcasted_iota(jnp.int32, (n, 1), 0)
    col = lax.broadcasted_iota(jnp.int32, (1, n), 1)
    ops = [(qq.astype(BF16), k.astype(BF16), row == col)]
    for h, sh, p, r in levels:
        up = (row >> sh) & 1
        qt = jnp.where(up != isb, qq * p, 0.0).astype(BF16)
        kt = jnp.where(up == isb, k * r, 0.0).astype(BF16)
        ops.append((qt, kt, (row >> (sh + 1)) == (col >> (sh + 1))))
    qc = (qq * pfull).astype(BF16)
    v = v_ref[...]
    vb = v.astype(BF16)
    lane = lax.broadcasted_iota(jnp.int32, (1, BRANCH), 1)
    latent = i >= N_CTX_SEQ
    acc = jnp.zeros((n, BRANCH), F32)
    for hd in range(HG_HEADS):
        sl = slice(hd * HG_HEAD_W, (hd + 1) * HG_HEAD_W)
        a = jnp.zeros((n, n), F32)
        for qt, kt, mask in ops:
            a = a + jnp.where(mask, _dot_nt(qt[:, sl], kt[:, sl]), 0.0)
        s_in = jnp.concatenate([sf_ref[0, hd], sb_ref[0, hd]], axis=0)
        s_in = jnp.where(latent, s_in, 0.0).astype(BF16)
        o_h = _dot(a.astype(BF16), vb) + _dot(qc[:, sl], s_in)
        acc = jnp.where((lane >> 6) == hd, o_h, acc)
    sq = acc * acc
    ms = jnp.zeros((n, BRANCH), F32)
    for hd in range(HG_HEADS):
        hm = (lane >> 6) == hd
        ms = jnp.where(hm, jnp.sum(jnp.where(hm, sq, 0.0), axis=-1, keepdims=True), ms)
    o_ref[...] = acc * lax.rsqrt(ms * (1.0 / HG_DK) + EPS) * ng_ref[...] * _silu(g_ref[...])

    @pl.when(i < N_CTX_SEQ)
    def _():
        kt_full = k * rfull
        for hd in range(HG_HEADS):
            kth = kt_full[:, hd * HG_HEAD_W:(hd + 1) * HG_HEAD_W].T.astype(BF16)
            ds = _dot(kth, jnp.where((lane >> 6) == hd, v, 0.0).astype(BF16))
            fin_ref[0, hd] = ds[:, hd * HG_DK:(hd + 1) * HG_DK]


def _hg_main(z_c, s_f, s_b, lb, norm_g):
    half = (1, HG_HEADS, HG_DK, BRANCH)
    lat = lambda i: (jnp.maximum(i - N_CTX_SEQ, 0), 0, 0, 0)
    fin = (HG_HEADS, HG_HEAD_W, HG_DK)
    return pl.pallas_call(
        _hg_main_body,
        grid=(HG_CHUNKS,),
        in_specs=[pl.BlockSpec((HG_CHUNK, HG_W), lambda i: (i, 0)),
                  pl.BlockSpec((HG_CHUNK, HG_W), lambda i: (i, 1)),
                  pl.BlockSpec((HG_CHUNK, BRANCH), lambda i: (i, 4)),
                  pl.BlockSpec((HG_CHUNK, BRANCH), lambda i: (i, 5)),
                  pl.BlockSpec(half, lat),
                  pl.BlockSpec(half, lat),
                  pl.BlockSpec((1, HG_W), lambda i: (0, 0)),
                  pl.BlockSpec((1, BRANCH), lambda i: (0, 0))],
        out_specs=[pl.BlockSpec((HG_CHUNK, BRANCH), lambda i: (i, 0)),
                   pl.BlockSpec((1,) + fin, lambda i: (jnp.minimum(i, N_CTX_SEQ - 1), 0, 0, 0))],
        out_shape=[jax.ShapeDtypeStruct((N_TOK, BRANCH), F32),
                   jax.ShapeDtypeStruct((N_CTX_SEQ,) + fin, F32)],
        compiler_params=_cparams("arbitrary"),
        name="hg_main",
    )(z_c, z_c, z_c, z_c, s_f, s_b, lb, norm_g)


def _take_cols(w, plan):
    idx = np.concatenate([p[0] for p in plan]).astype(np.int32)
    sign = np.concatenate([np.broadcast_to(p[1], p[0].shape) for p in plan]).astype(np.float32)
    return jnp.take(w, jnp.asarray(idx), axis=-1) * jnp.asarray(sign)


def _zeros(n):
    return (np.zeros(n, np.int64), 0.0)


_IN_OFF = {}
_off = 0
for _name, _n in (("da_q", 256), ("da_k", 256), ("da_v", 256), ("da_g", 256), ("s5_u", 256), ("s5_g", 256),
                  ("hg_q", 256), ("hg_ff", 256), ("hg_fb", 256), ("hg_i", 256), ("hg_g", 256),
                  ("mla_cq", MLA_Q_RANK), ("mla_ckv", MLA_KV_RANK), ("mla_kr", MLA_ROPE), ("mla_g", 256)):
    _IN_OFF[_name] = np.arange(_off, _off + _n)
    _off += _n


def _perm_matrix(plan, first, k):
    idx = np.concatenate([p[0] for p in plan]) - first
    sign = np.concatenate([np.broadcast_to(p[1], p[0].shape) for p in plan])
    m = np.zeros((k, len(idx)), np.float32)
    m[idx, np.arange(len(idx))] = sign
    return jnp.asarray(m, BF16)


def _in_proj_weights(w_in):
    c = _IN_OFF

    def per_head(x, y):
        return (np.concatenate([c[x].reshape(HG_HEADS, HG_DK), c[y].reshape(HG_HEADS, HG_DK)], axis=1).reshape(-1), 1.0)
    pdup = _perm_matrix([per_head("hg_q", "hg_q")], c["hg_q"][0], 256)
    pint = _perm_matrix([per_head("hg_ff", "hg_fb")], c["hg_ff"][0], 512)
    col = lambda name: w_in[..., c[name][0]:c[name][-1] + 1]
    zero = lambda n: jnp.zeros(w_in.shape[:-1] + (n,), w_in.dtype)
    w_d = jnp.concatenate([col("mla_cq"), zero(256 - MLA_Q_RANK), col("mla_ckv"), zero(MLA_NOPE), col("mla_kr"),
                           zero(128 - MLA_NOPE - MLA_ROPE), col("mla_g")], axis=-1)
    return _arrange_w_in(w_in, pdup, pint), w_d.astype(BF16)


def _rope_tables():
    t = np.arange(LAT_LEN)
    pos = np.stack([t // GRID_W, t % GRID_W], axis=1).astype(np.float32)
    inv_freq = (np.float32(ROPE_BASE) ** (-np.arange(8, dtype=np.float32) / np.float32(8))).astype(np.float32)
    r = np.arange(MLA_ROPE)
    ang = (pos[:, r // 16] * inv_freq[r % 8][None, :]).astype(np.float64)
    cos32, sin32 = np.cos(ang).astype(np.float32), np.sin(ang).astype(np.float32)
    lo = (np.arange(MLA_ROPE) % 16 < 8)[None, :]
    sin_lo32, sin_hi32 = np.where(lo, -sin32, 0.0).astype(np.float32), np.where(lo, 0.0, sin32).astype(np.float32)
    da_tabs = tuple(np.tile(x, (1, 8)) for x in (cos32, sin_lo32, sin_hi32))

    def head(x, fill):
        h = np.concatenate([np.full((LAT_LEN, MLA_NOPE), fill, np.float32), x,
                            np.full((LAT_LEN, MLA_HEAD_PAD - MLA_NOPE - MLA_ROPE), fill, np.float32)], axis=1)
        return np.concatenate([h, np.full((ROW_TILE, MLA_HEAD_PAD), fill, np.float32)], axis=0)
    k_tabs = (head(cos32, 1.0), head(sin_lo32, 0.0), head(sin_hi32, 0.0))
    return tuple(jnp.asarray(x) for x in da_tabs), tuple(jnp.asarray(x) for x in k_tabs)


def _mla_weights(w_uq, w_ukv, q_norm):
    hd = MLA_NOPE + MLA_ROPE
    pad_tail = _zeros(MLA_HEAD_PAD - hd)
    q_plan, k_plan, v_plan = [], [], []
    for h in range(MLA_HEADS):
        nope, rope = np.arange(h * hd, h * hd + MLA_NOPE), np.arange(h * hd + MLA_NOPE, (h + 1) * hd)
        q_plan += [(nope, 1.0), (rope, 1.0), pad_tail]
        k_plan += [(np.arange(h * 2 * MLA_NOPE, h * 2 * MLA_NOPE + MLA_NOPE), 1.0), _zeros(MLA_HEAD_PAD - MLA_NOPE)]
        v_plan += [(np.arange(h * 2 * MLA_NOPE + MLA_NOPE, (h + 1) * 2 * MLA_NOPE), 1.0)]
    pad_rows = lambda x: jnp.pad(x, ((0, 256 - MLA_Q_RANK), (0, 0))).astype(BF16)
    qn = jnp.pad(q_norm, (0, 256 - MLA_Q_RANK)).reshape(1, 256)
    return (pad_rows(_take_cols(w_uq, q_plan)), _take_cols(w_ukv, k_plan).astype(BF16),
            _take_cols(w_ukv, v_plan).astype(BF16), qn)


def _split_bf16(a):
    hi = a.astype(BF16)
    return hi, (a - hi.astype(F32)).astype(BF16)


def _dot_sel(a, sel):
    hi, lo = _split_bf16(a)
    sel = sel.astype(BF16)
    return _dot(hi, sel) + _dot(lo, sel)


def _dot_x3(a, b):
    a_hi, a_lo = _split_bf16(a)
    b_hi, b_lo = _split_bf16(b)
    return _dot(a_hi, b_hi) + _dot(a_hi, b_lo) + _dot(a_lo, b_hi)


def _s5_table_body(xy_ref, bb_ref, c_ref, ct_ref, mt_ref, bst_ref, cot_ref, a_ref):
    n, t, ch = S5_STATE, S5_CHUNK, S5_CH
    wide = 2 * S5_TAP
    xy = xy_ref[0, 0]
    tau_i = lax.broadcasted_iota(jnp.int32, (1, 128), 1)
    tau = tau_i.astype(F32)
    sel_row = lax.broadcasted_iota(jnp.int32, (128, 1), 0)

    def lag(width):
        return lax.broadcasted_iota(jnp.int32, (1, width), 1) >> 4

    def onehot(cond):
        return jnp.where(cond, 1.0, 0.0).astype(F32)
    j = lag(wide)
    e_z = (onehot((j <= t - 1) & (sel_row == t - 1 - j)), onehot((j >= t - 1) & (j <= 2 * t - 2) & (sel_row == j - (t - 1))))
    jc = lag(S5_TAP)
    e_c = (onehot(sel_row == jc + 1), onehot(sel_row == t - jc))
    ch_row = lax.broadcasted_iota(jnp.int32, (ch, 1), 0)
    tile_w = onehot((lax.broadcasted_iota(jnp.int32, (1, wide), 1) & (ch - 1)) == ch_row)
    tile_n = onehot((lax.broadcasted_iota(jnp.int32, (1, S5_TAP), 1) & (ch - 1)) == ch_row)

    z, cot_rows, klong = [], [], None
    for d in range(2):
        x, y = xy[:, 2 * d:2 * d + 1], xy[:, 2 * d + 1:2 * d + 2]
        mag = jnp.exp(jnp.where(tau_i <= t, tau, 0.0) * x)
        ang = jnp.where(tau_i <= t, tau, 0.0) * y
        p_re = jnp.where(tau_i <= t, mag * jnp.cos(ang), 0.0)
        p_im = jnp.where(tau_i <= t, mag * jnp.sin(ang), 0.0)
        a_ref[0, 0, 2 * d] = jnp.broadcast_to(p_re[:, t:t + 1], (n, 128))
        a_ref[0, 0, 2 * d + 1] = jnp.broadcast_to(p_im[:, t:t + 1], (n, 128))
        pz_re, pz_im = _dot_sel(p_re, e_z[d]), _dot_sel(p_im, e_z[d])
        b_re, b_im = _dot_sel(bb_ref[0, 0, 2 * d], tile_w), _dot_sel(bb_ref[0, 0, 2 * d + 1], tile_w)
        z_re, z_im = pz_re * b_re - pz_im * b_im, pz_re * b_im + pz_im * b_re
        z += [z_re, z_im]
        part = _dot_x3(c_ref[0, 0, 2 * d], z_re) - _dot_x3(c_ref[0, 0, 2 * d + 1], z_im)
        klong = part if klong is None else klong + part
        pc_re, pc_im = _dot_sel(p_re, e_c[d]), _dot_sel(p_im, e_c[d])
        c_re, c_im = _dot_sel(ct_ref[0, 0, 2 * d], tile_n), _dot_sel(ct_ref[0, 0, 2 * d + 1], tile_n)
        cot_rows += [c_re * pc_re - c_im * pc_im, -(c_re * pc_im + c_im * pc_re)]
    for tt in range(t):
        off = (t - 1 - tt) * ch
        win = klong if off == 0 else pltpu.roll(klong, wide - off, 1)
        mt_ref[0, 0, tt * ch:(tt + 1) * ch, :] = win[:, :S5_TAP].astype(BF16)
    back = pltpu.roll(z[2], wide - (t - 1) * ch, 1), pltpu.roll(z[3], wide - (t - 1) * ch, 1)
    for k, rows in enumerate((z[0], z[1], back[0], back[1])):
        bst_ref[0, 0, k * n:(k + 1) * n, :] = rows[:, :S5_TAP].astype(BF16)
    cot_ref[0, 0] = jnp.concatenate(cot_rows, axis=0).T.astype(BF16)


def _s5_tables(a_re, a_im, log_dt, b_re, b_im, c_re, c_im):
    nl, g, n, ch = a_re.shape[0], S5_GROUPS, S5_STATE, S5_CH
    step = jnp.exp(log_dt)[..., None]
    mag = jnp.exp(a_re * step)
    ab_re, ab_im = mag * jnp.cos(a_im * step), mag * jnp.sin(a_im * step)
    den = a_re * a_re + a_im * a_im
    f_re = ((ab_re - 1.0) * a_re + ab_im * a_im) / den
    f_im = (ab_im * a_re - (ab_re - 1.0) * a_im) / den
    bb_re = f_re[..., None] * b_re - f_im[..., None] * b_im
    bb_im = f_re[..., None] * b_im + f_im[..., None] * b_re
    by_group = lambda x: jnp.moveaxis(x, 1, 2)
    pair = lambda re, im: jnp.stack([by_group(re), by_group(im)], axis=3).reshape((nl, g, 4) + re.shape[3:])
    xy = jnp.stack([by_group(a_re * step), by_group(a_im * step)], axis=3).reshape(nl, g, 4, n)
    xy = jnp.pad(jnp.swapaxes(xy, 2, 3), ((0, 0), (0, 0), (0, 0), (0, 4)))
    mat = pl.BlockSpec((1, 1, S5_TAP, S5_TAP), lambda l, i: (l, i, 0, 0))
    return pl.pallas_call(
        _s5_table_body,
        grid=(nl, g),
        in_specs=[pl.BlockSpec((1, 1, n, 8), lambda l, i: (l, i, 0, 0)),
                  pl.BlockSpec((1, 1, 4, n, ch), lambda l, i: (l, i, 0, 0, 0)),
                  pl.BlockSpec((1, 1, 4, ch, n), lambda l, i: (l, i, 0, 0, 0)),
                  pl.BlockSpec((1, 1, 4, n, ch), lambda l, i: (l, i, 0, 0, 0))],
        out_specs=[mat, mat, mat, pl.BlockSpec((1, 1, 4, n, 128), lambda l, i: (l, i, 0, 0, 0))],
        out_shape=[jax.ShapeDtypeStruct((nl, g, S5_TAP, S5_TAP), BF16)] * 3
        + [jax.ShapeDtypeStruct((nl, g, 4, n, 128), F32)],
        compiler_params=_cparams("parallel", "parallel"),
        name="s5_tables",
    )(xy, pair(bb_re, bb_im), pair(c_re, c_im), pair(jnp.swapaxes(c_re, -1, -2), jnp.swapaxes(c_im, -1, -2)))


def _s5_chunk_lanes(u):
    return u.reshape(S5_NCHUNK, S5_CHUNK, BRANCH).transpose(1, 2, 0)


def _s5_token_rows(y):
    return y.transpose(2, 0, 1).reshape(N_TOK, BRANCH)


def kernel(x_prompt, x_sample, cache_diff_k, cache_diff_v, state_s5, state_hgrn, cache_mla_ckv, cache_mla_krope, c, c_ctx, w_mod, b_mod, w_in, w_out, da_lambda, da_norm, s5_a_re, s5_a_im, s5_log_dt, s5_b_re, s5_b_im, s5_c_re, s5_c_im, s5_d, s5_w_glu, hg_lb, hg_norm, mla_q_norm, mla_w_uq, mla_kv_norm, mla_w_ukv, final_norm):
    lb_w = jax.nn.softmax(hg_lb.astype(F32), axis=0)
    lb_all = jnp.cumsum(lb_w, axis=0) - lb_w[0:1]
    c_rows = jnp.concatenate([c_ctx[None], c, jnp.zeros((8 - 1 - N_LAT_SEQ, D_MODEL), F32)], axis=0)
    mods = _modulation(c_rows, w_mod, b_mod)
    da_tabs, mla_tabs = _rope_tables()
    xs = (x_prompt.reshape(N_CTX, D_MODEL), x_sample.reshape(N_LAT, D_MODEL))
    new_k, new_v, new_s5, new_hg, new_ckv, new_kr = [], [], [], [], [], []
    s5_tabs = _s5_tables(s5_a_re, s5_a_im, s5_log_dt, s5_b_re, s5_b_im, s5_c_re, s5_c_im)
    w_abc, w_d = _in_proj_weights(w_in)
    w_out_bf = w_out.astype(BF16)
    for l in range(DEPTH):
        mod = mods[l, :3].reshape(3, 3, D_MODEL)
        z_a, z_b, z_c, z_d, k_new, v_new = _in_proj(xs, mod, w_abc, w_d, l)

        lam_init = 0.8 - 0.6 * math.exp(-0.3 * l)
        kv_lat = _da_latent_kv(z_a, da_tabs,
                               cache_diff_k[:, l].reshape(N_LAT_SEQ, PAST_LEN, BRANCH),
                               cache_diff_v[:, l].reshape(N_LAT_SEQ, PAST_LEN, BRANCH))
        a_out = _da_attention(z_a, da_lambda[l], da_norm[l], lam_init, da_tabs, kv_lat)
        new_k.append(k_new.reshape(N_CTX_SEQ, CTX_LEN, DA_HEADS, 2 * DA_QK))
        new_v.append(v_new.reshape(N_CTX_SEQ, CTX_LEN, DA_HEADS, 2 * DA_QK))

        h0 = state_s5[:, l].transpose(2, 1, 4, 3, 0).reshape(S5_GROUPS, 4, S5_STATE, N_LAT_SEQ)
        h0 = jnp.pad(h0, ((0, 0), (0, 0), (0, 0), (0, 128 - N_LAT_SEQ)))
        y_all, fin = _s5_scan(_s5_chunk_lanes(z_b[:, :BRANCH]).astype(BF16), *s5_tabs, h0, l)
        b_out = _s5_out(_s5_token_rows(y_all), z_b, s5_d[l].reshape(1, BRANCH), s5_w_glu[l].astype(BF16))
        fin = jnp.stack([fin[:, 0:2, :, S5_CTX_SEQ_CH - 1::S5_CTX_SEQ_CH], fin[:, 2:4, :, 0::S5_CTX_SEQ_CH]], axis=1)
        new_s5.append(fin.transpose(4, 1, 0, 3, 2))

        lb = jnp.concatenate([lb_all[l, 0].reshape(HG_HEADS, HG_DK), lb_all[l, 1].reshape(HG_HEADS, HG_DK)],
                             axis=-1).reshape(1, HG_W)
        head_eye = jnp.eye(HG_HEADS, dtype=F32)
        s0 = state_hgrn[:, l].transpose(0, 2, 1, 3, 4).reshape(N_LAT_SEQ, HG_HEADS, HG_HEAD_W, 1, HG_DK)
        s0 = (s0 * head_eye[None, :, None, :, None]).reshape(N_LAT_SEQ, HG_HEADS, HG_HEAD_W, BRANCH)
        s_f, s_b = _hg_states(z_c, lb, s0)
        c_out, s_fin = _hg_main(z_c, s_f, s_b, lb, jnp.tile(hg_norm[l].reshape(1, HG_DK), (1, HG_HEADS)))
        new_hg.append(s_fin.reshape(N_CTX_SEQ, HG_HEADS, 2, HG_DK, HG_DK).transpose(0, 2, 1, 3, 4))

        wq, wk, wv, qn = _mla_weights(mla_w_uq[l], mla_w_ukv[l], mla_q_norm[l])
        q, ckv_n, kr = _mla_prep(z_d, mla_tabs, qn, mla_kv_norm[l].reshape(1, MLA_KV_RANK), wq)
        kr_cache = jnp.pad(cache_mla_krope[:, l], ((0, 0), (0, 0), (MLA_NOPE, 128 - MLA_NOPE - MLA_ROPE)))
        k_ctx, v_ctx, k_lat, v_lat = _mla_kv(ckv_n, kr, cache_mla_ckv[:, l], kr_cache, wk, wv)
        d_out = _mla_attention(z_d, q, k_ctx, v_ctx, k_lat, v_lat)
        new_ckv.append(ckv_n[:N_CTX].reshape(N_CTX_SEQ, CTX_LEN, MLA_KV_RANK))
        new_kr.append(kr[:N_CTX, MLA_NOPE:MLA_NOPE + MLA_ROPE].reshape(N_CTX_SEQ, CTX_LEN, MLA_ROPE))

        xs = _out_proj(a_out, b_out, c_out, d_out, xs, mod, w_out_bf, l,
                       final_norm.reshape(1, D_MODEL), final=(l == DEPTH - 1))
        xs = tuple(xs) if l == DEPTH - 1 else (xs,)
    y_prompt = xs[0].reshape(N_CTX_SEQ, CTX_LEN, D_MODEL)
    y_sample = xs[1].reshape(N_LAT_SEQ, LAT_LEN, D_MODEL)
    st = lambda parts: jnp.stack(parts, axis=1)
    return (y_prompt, y_sample, st(new_k), st(new_v), st(new_s5), st(new_hg), st(new_ckv), st(new_kr))
```

```python
import functools
import math

import numpy as np

import jax
import jax.numpy as jnp
from jax import lax
from jax.experimental import pallas as pl
from jax.experimental.pallas import tpu as pltpu

F32 = jnp.float32
BF16 = jnp.bfloat16

D_MODEL = 1024
DEPTH = 2
N_CTX_SEQ = 16
CTX_LEN = 256
N_LAT_SEQ = 2
LAT_LEN = 2048
PAST_LEN = 256
GRID_W = 64
N_CTX = N_CTX_SEQ * CTX_LEN
N_LAT = N_LAT_SEQ * LAT_LEN
N_TOK = N_CTX + N_LAT
BRANCH = 256
EPS = 1e-6
ROPE_BASE = 10000.0
ROW_TILE = 256
LAT_TILES = LAT_LEN // ROW_TILE
N_TILES = N_TOK // ROW_TILE
CTX_TILES = N_CTX // ROW_TILE
VMEM_LIMIT = 48 * 1024 * 1024
LAT_Q_TILE = 512
LAT_Q_TILES = LAT_LEN // LAT_Q_TILE
BIG_TILE = 512
CTX_SEQ_PER_STEP = 2

DA_HEADS = 4
DA_QK = 32
MLA_HEADS = 4
MLA_NOPE = 64
MLA_ROPE = 32
MLA_Q_RANK = 192
MLA_KV_RANK = 128
S5_GROUPS = 16
S5_CH = 16
S5_STATE = 64
S5_CHUNK = 16
HG_HEADS = 4
HG_DK = 64

W_A = 1024
W_B = 512
W_C = 1536
W_D = 768
W_ABC = W_A + W_B + W_C


def _cparams(*sem):
    return pltpu.CompilerParams(dimension_semantics=sem, vmem_limit_bytes=VMEM_LIMIT)


def _tile_seq(i, tile=ROW_TILE):
    return jnp.where(i < N_CTX // tile, 0, 1 + (i - N_CTX // tile) // (LAT_LEN // tile))


def _silu(x):
    return x * (1.0 / (1.0 + jnp.exp(-x)))


def _dot(a, b):
    return jnp.dot(a, b, preferred_element_type=F32)


def _dot_nt(a, b):
    return lax.dot_general(a, b, (((1,), (1,)), ((), ())), preferred_element_type=F32)


def _mod_body(c_ref, w_ref, b_ref, o_ref):
    c = _silu(c_ref[...]).astype(BF16)
    o_ref[0] = _dot(c, w_ref[0].astype(BF16)) + b_ref[0]


def _modulation(c_rows, w_mod, b_mod):
    tn = 768
    return pl.pallas_call(
        _mod_body,
        grid=(DEPTH, 3 * D_MODEL // tn),
        in_specs=[pl.BlockSpec((8, D_MODEL), lambda l, j: (0, 0)),
                  pl.BlockSpec((1, D_MODEL, tn), lambda l, j: (l, 0, j)),
                  pl.BlockSpec((1, 1, tn), lambda l, j: (l, 0, j))],
        out_specs=pl.BlockSpec((1, 8, tn), lambda l, j: (l, 0, j)),
        out_shape=jax.ShapeDtypeStruct((DEPTH, 8, 3 * D_MODEL), F32),
        compiler_params=_cparams("parallel", "parallel"),
        name="modulation",
    )(c_rows, w_mod, b_mod.reshape(DEPTH, 1, 3 * D_MODEL))


def _split_rows(i, ctx_ref, lat_ref):
    return jnp.where(i < N_CTX // ctx_ref.shape[0], ctx_ref[...], lat_ref[...])


def _ctx_tile_spec(w, tile=ROW_TILE):
    return pl.BlockSpec((tile, w), lambda i: (jnp.minimum(i, N_CTX // tile - 1), 0))


def _lat_tile_spec(w, tile=ROW_TILE):
    return pl.BlockSpec((tile, w), lambda i: (jnp.maximum(i - N_CTX // tile, 0), 0))


def _in_proj_body(*refs, split):
    if split:
        xc_ref, xl_ref, mod_ref, w_ref, wd_ref, oa, ob, oc, od, ok, ov = refs
        x = _split_rows(pl.program_id(0), xc_ref, xl_ref)
    else:
        x_ref, mod_ref, w_ref, wd_ref, oa, ob, oc, od, ok, ov = refs
        x = x_ref[...]
    xn = x * lax.rsqrt(jnp.mean(x * x, axis=-1, keepdims=True) + EPS)
    mod = mod_ref[0]
    h = (xn * (1.0 + mod[1:2]) + mod[0:1]).astype(BF16)
    off = 0
    for o in (oa, ob, oc):
        w = o.shape[-1]
        o[...] = _dot(h, w_ref[0, :, off:off + w])
        off += w
    od[...] = _dot(h, wd_ref[0])

    @pl.when(pl.program_id(0) < CTX_TILES)
    def _():
        ok[...] = oa[:, BRANCH:2 * BRANCH]
        ov[...] = oa[:, 2 * BRANCH:3 * BRANCH]


def _in_proj(xs, mod, w_abc, w_d, l):
    widths = (W_A, W_B, W_C, W_D)
    split = len(xs) == 2
    x_specs = ([_ctx_tile_spec(D_MODEL), _lat_tile_spec(D_MODEL)] if split
               else [pl.BlockSpec((ROW_TILE, D_MODEL), lambda i: (i, 0))])
    return pl.pallas_call(
        functools.partial(_in_proj_body, split=split),
        grid=(N_TILES,),
        in_specs=x_specs + [pl.BlockSpec((1, 3, D_MODEL), lambda i: (_tile_seq(i), 0, 0)),
                            pl.BlockSpec((1, D_MODEL, W_ABC), lambda i: (l, 0, 0)),
                            pl.BlockSpec((1, D_MODEL, W_D), lambda i: (l, 0, 0))],
        out_specs=[pl.BlockSpec((ROW_TILE, w), lambda i: (i, 0)) for w in widths] + [_ctx_tile_spec(BRANCH)] * 2,
        out_shape=[jax.ShapeDtypeStruct((N_TOK, w), F32) for w in widths]
        + [jax.ShapeDtypeStruct((N_CTX, BRANCH), F32)] * 2,
        compiler_params=_cparams("arbitrary"),
        name="in_proj",
    )(*xs, mod, w_abc, w_d)


def _arrange_body(w_ref, pdup_ref, pint_ref, o_ref):
    w = w_ref[0].astype(BF16)

    def perm(x, p_ref):
        return _dot(x, p_ref[...]).astype(BF16)
    o_ref[0] = jnp.concatenate(
        [w[:, 0:1536],
         perm(w[:, 1536:1792], pdup_ref), perm(w[:, 1792:2304], pint_ref), w[:, 2304:2816]], axis=-1)


def _arrange_w_in(w_in, pdup, pint):
    rows = 128
    const = lambda a: pl.BlockSpec(a.shape, lambda l, i: (0, 0))
    return pl.pallas_call(
        _arrange_body,
        grid=(DEPTH, D_MODEL // rows),
        in_specs=[pl.BlockSpec((1, rows, w_in.shape[-1]), lambda l, i: (l, i, 0)), const(pdup), const(pint)],
        out_specs=pl.BlockSpec((1, rows, W_ABC), lambda l, i: (l, i, 0)),
        out_shape=jax.ShapeDtypeStruct((DEPTH, D_MODEL, W_ABC), BF16),
        compiler_params=_cparams("parallel", "parallel"),
        name="arrange_w_in",
    )(w_in, pdup, pint)


def _out_proj_body(*refs, split_in, final):
    ac_ref, al_ref, b_ref, c_ref, dc_ref, dl_ref = refs[:6]
    i = pl.program_id(0)
    if split_in:
        xc_ref, xl_ref, mod_ref, w_ref, fn_ref = refs[6:11]
        x = _split_rows(i, xc_ref, xl_ref)
    else:
        x_ref, mod_ref, w_ref, fn_ref = refs[6:10]
        x = x_ref[...]
    branches = (_split_rows(i, ac_ref, al_ref), b_ref[...], c_ref[...], _split_rows(i, dc_ref, dl_ref))
    acc = None
    for j, r in enumerate(branches):
        t = _dot(r.astype(BF16), w_ref[0, j * BRANCH:(j + 1) * BRANCH, :])
        acc = t if acc is None else acc + t
    x = x + mod_ref[0][2:3] * acc
    if not final:
        refs[-1][...] = x
        return
    y = x * lax.rsqrt(jnp.mean(x * x, axis=-1, keepdims=True) + EPS) * fn_ref[...]
    yc_ref, yl_ref = refs[-2:]

    @pl.when(i < N_CTX // BIG_TILE)
    def _():
        yc_ref[...] = y

    @pl.when(i >= N_CTX // BIG_TILE)
    def _():
        yl_ref[...] = y


def _out_proj(a, b, c, d, xs, mod, w_out, l, final_norm, final):
    br = pl.BlockSpec((BIG_TILE, BRANCH), lambda i: (i, 0))
    pair = [_ctx_tile_spec(BRANCH, BIG_TILE), _lat_tile_spec(BRANCH, BIG_TILE)]
    split_in = len(xs) == 2
    x_specs = ([_ctx_tile_spec(D_MODEL, BIG_TILE), _lat_tile_spec(D_MODEL, BIG_TILE)] if split_in
               else [pl.BlockSpec((BIG_TILE, D_MODEL), lambda i: (i, 0))])
    if final:
        out_specs = [_ctx_tile_spec(D_MODEL, BIG_TILE), _lat_tile_spec(D_MODEL, BIG_TILE)]
        out_shape = [jax.ShapeDtypeStruct((N_CTX, D_MODEL), F32), jax.ShapeDtypeStruct((N_LAT, D_MODEL), F32)]
    else:
        out_specs = pl.BlockSpec((BIG_TILE, D_MODEL), lambda i: (i, 0))
        out_shape = jax.ShapeDtypeStruct((N_TOK, D_MODEL), F32)
    return pl.pallas_call(
        functools.partial(_out_proj_body, split_in=split_in, final=final),
        grid=(N_TOK // BIG_TILE,),
        in_specs=pair + [br, br] + pair + x_specs + [
            pl.BlockSpec((1, 3, D_MODEL), lambda i: (_tile_seq(i, BIG_TILE), 0, 0)),
            pl.BlockSpec((1, D_MODEL, D_MODEL), lambda i: (l, 0, 0)),
            pl.BlockSpec((1, D_MODEL), lambda i: (0, 0))],
        out_specs=out_specs,
        out_shape=out_shape,
        compiler_params=_cparams("arbitrary"),
        name="out_proj",
    )(*a, b, c, *d, *xs, mod, w_out, final_norm)


LOG2E = 1.4426950408889634


def _exp2_rows(s):
    e = jnp.exp2(s - jnp.max(s, axis=-1, keepdims=True))
    return e, jnp.sum(e, axis=-1, keepdims=True)


def _rope(x, cos, sin_lo, sin_hi):
    w = x.shape[-1]
    return x * cos + pltpu.roll(x, w - 8, 1) * sin_lo + pltpu.roll(x, 8, 1) * sin_hi


def _da_kv_body(k_ref, v_ref, cos_ref, slo_ref, shi_ref, ck_ref, cv_ref, ko_ref, vo_ref):
    j = pl.program_id(1)

    @pl.when(j < LAT_TILES)
    def _():
        ko_ref[0] = _rope(k_ref[...], cos_ref[...], slo_ref[...], shi_ref[...]).astype(BF16)
        vo_ref[0] = v_ref[...].astype(BF16)

    @pl.when(j == LAT_TILES)
    def _():
        ko_ref[0] = ck_ref[0].astype(BF16)
        vo_ref[0] = cv_ref[0].astype(BF16)


def _da_latent_kv(z_a, tabs, cache_k, cache_v):
    def rows(col):
        return pl.BlockSpec(
            (ROW_TILE, BRANCH),
            lambda b, j: (CTX_TILES + b * LAT_TILES + jnp.minimum(j, LAT_TILES - 1), col))
    tab = pl.BlockSpec((ROW_TILE, BRANCH), lambda b, j: (jnp.minimum(j, LAT_TILES - 1), 0))
    cache = pl.BlockSpec((1, PAST_LEN, BRANCH), lambda b, j: (b, 0, 0))
    out = pl.BlockSpec((1, ROW_TILE, BRANCH), lambda b, j: (b, j, 0))
    shp = jax.ShapeDtypeStruct((N_LAT_SEQ, LAT_LEN + PAST_LEN, BRANCH), BF16)
    return pl.pallas_call(
        _da_kv_body,
        grid=(N_LAT_SEQ, LAT_TILES + 1),
        in_specs=[rows(1), rows(2), tab, tab, tab, cache, cache],
        out_specs=[out, out],
        out_shape=[shp, shp],
        compiler_params=_cparams("parallel", "parallel"),
        name="da_kv",
    )(z_a, z_a, *tabs, cache_k, cache_v)


def _da_attn_body(lam_ref, ng_ref, q_ref, *rest, rope, lam_init):
    if rope:
        cos_ref, slo_ref, shi_ref, k_ref, v_ref, g_ref, o_ref = rest
        q = _rope(q_ref[...], cos_ref[...], slo_ref[...], shi_ref[...])
        o_ref[...] = _da_attn_tile(lam_ref, ng_ref, q, k_ref[0], v_ref[0], g_ref[...], lam_init)
    else:
        k_ref, v_ref, g_ref, o_ref = rest
        for t in range(q_ref.shape[0] // CTX_LEN):
            r = slice(t * CTX_LEN, (t + 1) * CTX_LEN)
            o_ref[r, :] = _da_attn_tile(lam_ref, ng_ref, q_ref[r, :], k_ref[r, :].astype(BF16),
                                        v_ref[r, :].astype(BF16), g_ref[r, :], lam_init)


def _da_attn_tile(lam_ref, ng_ref, q, k, v, g, lam_init):
    q = q * (DA_QK ** -0.5 * LOG2E)
    lv = lam_ref[...]
    lam = (jnp.exp(jnp.sum(lv[0:1] * lv[1:2], axis=-1, keepdims=True))
           - jnp.exp(jnp.sum(lv[2:3] * lv[3:4], axis=-1, keepdims=True)) + lam_init)
    lane = lax.broadcasted_iota(jnp.int32, (1, BRANCH), 1)
    acc = jnp.zeros(q.shape, F32)
    for h in range(DA_HEADS):
        q1 = jnp.where(lane // DA_QK == 2 * h, q, 0.0).astype(BF16)
        q2 = jnp.where(lane // DA_QK == 2 * h + 1, q, 0.0).astype(BF16)
        e1, l1 = _exp2_rows(_dot_nt(q1, k))
        e2, l2 = _exp2_rows(_dot_nt(q2, k))
        a = (e1 - (lam * l1 / l2) * e2).astype(BF16)
        acc = jnp.where(lane // (2 * DA_QK) == h, _dot(a, v) * (1.0 / l1), acc)
    sq = acc * acc
    ms = jnp.zeros(q.shape, F32)
    for h in range(DA_HEADS):
        hm = lane // (2 * DA_QK) == h
        ms = jnp.where(hm, jnp.sum(jnp.where(hm, sq, 0.0), axis=-1, keepdims=True), ms)
    o = acc * lax.rsqrt(ms * (1.0 / (2 * DA_QK)) + EPS) * (ng_ref[...] * (1.0 - lam_init))
    return o * _silu(g)


def _da_attention(z_a, lam_vec, norm_g, lam_init, tabs, kv_lat):
    ng = jnp.tile(norm_g.reshape(1, 2 * DA_QK), (1, DA_HEADS))
    small = [pl.BlockSpec((4, DA_QK), lambda *_: (0, 0)), pl.BlockSpec((1, BRANCH), lambda *_: (0, 0))]

    rows = CTX_SEQ_PER_STEP * CTX_LEN

    def col(c):
        return pl.BlockSpec((rows, BRANCH), lambda i: (i, c))
    ctx = pl.pallas_call(
        functools.partial(_da_attn_body, rope=False, lam_init=lam_init),
        grid=(N_CTX // rows,),
        in_specs=small + [col(0), col(1), col(2), col(3)],
        out_specs=pl.BlockSpec((rows, BRANCH), lambda i: (i, 0)),
        out_shape=jax.ShapeDtypeStruct((N_CTX, BRANCH), F32),
        compiler_params=_cparams("parallel"),
        name="da_attn_ctx",
    )(lam_vec, ng, z_a, z_a, z_a, z_a)

    def lcol(c):
        return pl.BlockSpec((LAT_Q_TILE, BRANCH), lambda b, j: (N_CTX // LAT_Q_TILE + b * LAT_Q_TILES + j, c))
    tab = pl.BlockSpec((LAT_Q_TILE, BRANCH), lambda b, j: (j, 0))
    kvs = pl.BlockSpec((1, LAT_LEN + PAST_LEN, BRANCH), lambda b, j: (b, 0, 0))
    lat = pl.pallas_call(
        functools.partial(_da_attn_body, rope=True, lam_init=lam_init),
        grid=(N_LAT_SEQ, LAT_Q_TILES),
        in_specs=small + [lcol(0), tab, tab, tab, kvs, kvs, lcol(3)],
        out_specs=pl.BlockSpec((LAT_Q_TILE, BRANCH), lambda b, j: (b * LAT_Q_TILES + j, 0)),
        out_shape=jax.ShapeDtypeStruct((N_LAT, BRANCH), F32),
        compiler_params=_cparams("parallel", "parallel"),
        name="da_attn_lat",
    )(lam_vec, ng, z_a, *tabs, kv_lat[0], kv_lat[1], z_a)
    return ctx, lat


MLA_HEAD_PAD = 128
MLA_QW = MLA_HEADS * MLA_HEAD_PAD


def _mla_prep_body(cq_ref, ckv_ref, kr_ref, ck_t, sk_lo, sk_hi, qn_ref, kvn_ref, wq_ref, q_out, ckv_out, kr_out):
    cq = cq_ref[...]
    ms = jnp.sum(cq * cq, axis=-1, keepdims=True) * (1.0 / MLA_Q_RANK)
    qn = (cq * lax.rsqrt(ms + EPS) * qn_ref[...]).astype(BF16)
    heads = lambda t: jnp.concatenate([t[...]] * MLA_HEADS, axis=-1)
    q = _rope(_dot(qn, wq_ref[...]), heads(ck_t), heads(sk_lo), heads(sk_hi))
    q_out[...] = (q * ((MLA_NOPE + MLA_ROPE) ** -0.5 * LOG2E)).astype(BF16)
    ckv = ckv_ref[...]
    ckv_out[...] = ckv * lax.rsqrt(jnp.mean(ckv * ckv, axis=-1, keepdims=True) + EPS) * kvn_ref[...]
    kr_out[...] = _rope(kr_ref[...], ck_t[...], sk_lo[...], sk_hi[...])


def _mla_prep(z_d, tabs, q_norm_pad, kv_norm, wq):
    ctx_tiles, lat_tiles = N_CTX // BIG_TILE, LAT_LEN // BIG_TILE

    def tab(w):
        return pl.BlockSpec((BIG_TILE, w), lambda i: (jnp.where(i < ctx_tiles, lat_tiles, (i - ctx_tiles) % lat_tiles), 0))

    def col(w, c):
        return pl.BlockSpec((BIG_TILE, w), lambda i: (i, c))

    def const(shape):
        return pl.BlockSpec(shape, lambda i: (0, 0))
    return pl.pallas_call(
        _mla_prep_body,
        grid=(N_TOK // BIG_TILE,),
        in_specs=[col(256, 0), col(128, 2), col(128, 3),
                  tab(128), tab(128), tab(128),
                  const((1, 256)), const((1, 128)), const((256, MLA_QW))],
        out_specs=[col(MLA_QW, 0), col(128, 0), col(128, 0)],
        out_shape=[jax.ShapeDtypeStruct((N_TOK, MLA_QW), BF16),
                   jax.ShapeDtypeStruct((N_TOK, 128), F32),
                   jax.ShapeDtypeStruct((N_TOK, 128), F32)],
        compiler_params=_cparams("parallel"),
        name="mla_prep",
    )(z_d, z_d, z_d, *tabs, q_norm_pad, kv_norm, wq)


def _mla_kv_math(ckv, kr, wk_ref, wv_ref, k_out, v_out):
    c = ckv.astype(BF16)
    k_out[...] = (_dot(c, wk_ref[...]) + jnp.concatenate([kr] * MLA_HEADS, axis=-1)).astype(BF16).reshape(k_out.shape)
    v_out[...] = _dot(c, wv_ref[...]).astype(BF16).reshape(v_out.shape)


def _mla_kv_ctx_body(ckv_ref, kr_ref, wk_ref, wv_ref, k_out, v_out):
    _mla_kv_math(ckv_ref[...], kr_ref[...], wk_ref, wv_ref, k_out, v_out)


def _mla_kv_lat_body(ckv_ref, kr_ref, cckv_ref, ckr_ref, wk_ref, wv_ref, k_out, v_out):
    j = pl.program_id(1)

    @pl.when(j < LAT_TILES)
    def _():
        _mla_kv_math(ckv_ref[...], kr_ref[...], wk_ref, wv_ref, k_out, v_out)

    @pl.when(j == LAT_TILES)
    def _():
        _mla_kv_math(cckv_ref[0], ckr_ref[0], wk_ref, wv_ref, k_out, v_out)


def _mla_kv(ckv, kr, cache_ckv, cache_kr, wk, wv):
    weights = [pl.BlockSpec((128, MLA_QW), lambda *_: (0, 0)), pl.BlockSpec((128, BRANCH), lambda *_: (0, 0))]
    k_ctx, v_ctx = pl.pallas_call(
        _mla_kv_ctx_body,
        grid=(CTX_TILES,),
        in_specs=[pl.BlockSpec((ROW_TILE, 128), lambda i: (i, 0)), pl.BlockSpec((ROW_TILE, 128), lambda i: (i, 0))] + weights,
        out_specs=[pl.BlockSpec((ROW_TILE, MLA_QW), lambda i: (i, 0)),
                   pl.BlockSpec((ROW_TILE, BRANCH), lambda i: (i, 0))],
        out_shape=[jax.ShapeDtypeStruct((N_CTX, MLA_QW), BF16), jax.ShapeDtypeStruct((N_CTX, BRANCH), BF16)],
        compiler_params=_cparams("parallel"),
        name="mla_kv_ctx",
    )(ckv, kr, wk, wv)
    rows = pl.BlockSpec((ROW_TILE, 128), lambda b, j: (CTX_TILES + b * LAT_TILES + jnp.minimum(j, LAT_TILES - 1), 0))
    cache = pl.BlockSpec((1, PAST_LEN, 128), lambda b, j: (b, 0, 0))
    lk = LAT_LEN + PAST_LEN
    k_lat, v_lat = pl.pallas_call(
        _mla_kv_lat_body,
        grid=(N_LAT_SEQ, LAT_TILES + 1),
        in_specs=[rows, rows, cache, cache] + weights,
        out_specs=[pl.BlockSpec((1, ROW_TILE, MLA_QW), lambda b, j: (b, j, 0)),
                   pl.BlockSpec((1, ROW_TILE, BRANCH), lambda b, j: (b, j, 0))],
        out_shape=[jax.ShapeDtypeStruct((N_LAT_SEQ, lk, MLA_QW), BF16), jax.ShapeDtypeStruct((N_LAT_SEQ, lk, BRANCH), BF16)],
        compiler_params=_cparams("parallel", "parallel"),
        name="mla_kv_lat",
    )(ckv, kr, cache_ckv, cache_kr, wk, wv)
    return k_ctx, v_ctx, k_lat, v_lat


def _mla_attn_body(q_ref, k_ref, v_ref, g_ref, o_ref, *, ctx):
    if ctx:
        for t in range(q_ref.shape[0] // CTX_LEN):
            r = slice(t * CTX_LEN, (t + 1) * CTX_LEN)
            o_ref[r, :] = _mla_attn_tile(q_ref[r, :], k_ref[r, :], v_ref[r, :], g_ref[r, :])
    else:
        o_ref[...] = _mla_attn_tile(q_ref[...], k_ref[0], v_ref[0], g_ref[...])


def _mla_attn_tile(q, k, v, g):
    lane = lax.broadcasted_iota(jnp.int32, (1, BRANCH), 1)
    acc = jnp.zeros((q.shape[0], BRANCH), F32)
    for h in range(MLA_HEADS):
        sl = slice(h * MLA_HEAD_PAD, (h + 1) * MLA_HEAD_PAD)
        e, l = _exp2_rows(_dot_nt(q[:, sl], k[:, sl]))
        acc = jnp.where(lane // 64 == h, _dot(e.astype(BF16), v) * (1.0 / l), acc)
    return acc * _silu(g)


def _mla_attention(z_d, q, k_ctx, v_ctx, k_lat, v_lat):
    rows = CTX_SEQ_PER_STEP * CTX_LEN
    ctx = pl.pallas_call(
        functools.partial(_mla_attn_body, ctx=True),
        grid=(N_CTX // rows,),
        in_specs=[pl.BlockSpec((rows, MLA_QW), lambda i: (i, 0)),
                  pl.BlockSpec((rows, MLA_QW), lambda i: (i, 0)),
                  pl.BlockSpec((rows, BRANCH), lambda i: (i, 0)),
                  pl.BlockSpec((rows, BRANCH), lambda i: (i, 2))],
        out_specs=pl.BlockSpec((rows, BRANCH), lambda i: (i, 0)),
        out_shape=jax.ShapeDtypeStruct((N_CTX, BRANCH), F32),
        compiler_params=_cparams("parallel"),
        name="mla_attn_ctx",
    )(q, k_ctx, v_ctx, z_d)
    lk = LAT_LEN + PAST_LEN
    lat = pl.pallas_call(
        functools.partial(_mla_attn_body, ctx=False),
        grid=(N_LAT_SEQ, LAT_Q_TILES),
        in_specs=[pl.BlockSpec((LAT_Q_TILE, MLA_QW), lambda b, j: (N_CTX // LAT_Q_TILE + b * LAT_Q_TILES + j, 0)),
                  pl.BlockSpec((1, lk, MLA_QW), lambda b, j: (b, 0, 0)),
                  pl.BlockSpec((1, lk, BRANCH), lambda b, j: (b, 0, 0)),
                  pl.BlockSpec((LAT_Q_TILE, BRANCH), lambda b, j: (N_CTX // LAT_Q_TILE + b * LAT_Q_TILES + j, 2))],
        out_specs=pl.BlockSpec((LAT_Q_TILE, BRANCH), lambda b, j: (b * LAT_Q_TILES + j, 0)),
        out_shape=jax.ShapeDtypeStruct((N_LAT, BRANCH), F32),
        compiler_params=_cparams("parallel", "parallel"),
        name="mla_attn_lat",
    )(q, k_lat, v_lat, z_d)
    return ctx, lat


S5_TAP = S5_CHUNK * S5_CH
S5_NCHUNK = N_TOK // S5_CHUNK
S5_CTX_CH = N_CTX // S5_CHUNK
S5_CTX_SEQ_CH = CTX_LEN // S5_CHUNK
S5_LAT_SEQ_CH = LAT_LEN // S5_CHUNK
S5_SCAN_STEPS = S5_LAT_SEQ_CH.bit_length() - 1


def _s5_body(x_ref, mt_ref, bst_ref, cot_ref, a_ref, h0_ref, y_ref, fin_ref):
    x = x_ref[...].reshape(S5_TAP, S5_NCHUNK)
    y = _dot(mt_ref[0, 0], x)
    s = _dot(bst_ref[0, 0], x)
    lane = lax.broadcasted_iota(jnp.int32, (1, S5_NCHUNK), 1)
    is_lat = lane >= S5_CTX_CH
    pos_f = jnp.where(is_lat, (lane - S5_CTX_CH) & (S5_LAT_SEQ_CH - 1), lane & (S5_CTX_SEQ_CH - 1))
    pos_b = jnp.where(is_lat, S5_LAT_SEQ_CH - 1, S5_CTX_SEQ_CH - 1) - pos_f
    hin = []
    for d in range(2):
        n = S5_STATE
        sre, sim = s[2 * d * n:(2 * d + 1) * n], s[(2 * d + 1) * n:(2 * d + 2) * n]
        are = jnp.concatenate([a_ref[0, 0, 2 * d]] * (S5_NCHUNK // 128), axis=-1)
        aim = jnp.concatenate([a_ref[0, 0, 2 * d + 1]] * (S5_NCHUNK // 128), axis=-1)
        pos = pos_f if d == 0 else pos_b
        h0r, h0i = jnp.zeros_like(sre), jnp.zeros_like(sre)
        for b in range(N_LAT_SEQ):
            first = S5_CTX_CH + b * S5_LAT_SEQ_CH + (0 if d == 0 else S5_LAT_SEQ_CH - 1)
            h0r = jnp.where(lane == first, h0_ref[0, 2 * d][:, b:b + 1], h0r)
            h0i = jnp.where(lane == first, h0_ref[0, 2 * d + 1][:, b:b + 1], h0i)
        xr = sre + are * h0r - aim * h0i
        xi = sim + are * h0i + aim * h0r
        pr, pi = are, aim
        for j in range(S5_SCAN_STEPS):
            sh = 1 << j
            shift = sh if d == 0 else S5_NCHUNK - sh
            rr, ri = pltpu.roll(xr, shift, 1), pltpu.roll(xi, shift, 1)
            ok = pos >= sh
            xr, xi = (xr + jnp.where(ok, pr * rr - pi * ri, 0.0), xi + jnp.where(ok, pr * ri + pi * rr, 0.0))
            pr, pi = pr * pr - pi * pi, 2.0 * pr * pi
        fin_ref[0, 2 * d] = xr[:, :S5_CTX_CH]
        fin_ref[0, 2 * d + 1] = xi[:, :S5_CTX_CH]
        one = 1 if d == 0 else S5_NCHUNK - 1
        hin.append(jnp.where(pos >= 1, pltpu.roll(xr, one, 1), h0r))
        hin.append(jnp.where(pos >= 1, pltpu.roll(xi, one, 1), h0i))
    y = y + _dot(cot_ref[0, 0], jnp.concatenate(hin, axis=0).astype(BF16))
    y_ref[...] = y.reshape(S5_CHUNK, S5_CH, S5_NCHUNK)


def _s5_scan(x_all, mt, bst, cot, a16, h0, l):
    g = S5_GROUPS
    sq = pl.BlockSpec((1, 1, S5_TAP, S5_TAP), lambda i: (l, i, 0, 0))
    st = pl.BlockSpec((1, 4, S5_STATE, 128), lambda i: (i, 0, 0, 0))
    return pl.pallas_call(
        _s5_body,
        grid=(g,),
        in_specs=[pl.BlockSpec((S5_CHUNK, S5_CH, S5_NCHUNK), lambda i: (0, i, 0)), sq, sq, sq,
                  pl.BlockSpec((1, 1, 4, S5_STATE, 128), lambda i: (l, i, 0, 0, 0)), st],
        out_specs=[pl.BlockSpec((S5_CHUNK, S5_CH, S5_NCHUNK), lambda i: (0, i, 0)),
                   pl.BlockSpec((1, 4, S5_STATE, S5_CTX_CH), lambda i: (i, 0, 0, 0))],
        out_shape=[jax.ShapeDtypeStruct((S5_CHUNK, BRANCH, S5_NCHUNK), F32),
                   jax.ShapeDtypeStruct((g, 4, S5_STATE, S5_CTX_CH), F32)],
        compiler_params=_cparams("parallel"),
        name="s5_scan",
    )(x_all, mt, bst, cot, a16, h0)


def _s5_out_body(y_ref, u_ref, g_ref, d_ref, w_ref, o_ref):
    y = u_ref[...] * d_ref[...] + y_ref[...]
    ge = 0.5 * y * (1.0 + jnp.tanh(0.7978845608028654 * (y + 0.044715 * (y * y * y))))
    gl = _dot(ge.astype(BF16), w_ref[...])
    o_ref[...] = gl[:, :BRANCH] * (1.0 / (1.0 + jnp.exp(-gl[:, BRANCH:]))) * _silu(g_ref[...])


def _s5_out(y_ssm, z_b, d_skip, w_glu):
    tile = 2 * BIG_TILE

    def col(c):
        return pl.BlockSpec((tile, BRANCH), lambda i: (i, c))
    return pl.pallas_call(
        _s5_out_body,
        grid=(N_TOK // tile,),
        in_specs=[col(0), col(0), col(1),
                  pl.BlockSpec((1, BRANCH), lambda i: (0, 0)),
                  pl.BlockSpec((BRANCH, 2 * BRANCH), lambda i: (0, 0))],
        out_specs=col(0),
        out_shape=jax.ShapeDtypeStruct((N_TOK, BRANCH), F32),
        compiler_params=_cparams("parallel"),
        name="s5_out",
    )(y_ssm, z_b, z_b, d_skip, w_glu)


HG_CHUNK = ROW_TILE
HG_W = 2 * HG_HEADS * HG_DK
HG_HEAD_W = 2 * HG_DK
HG_LAT_CHUNKS = LAT_LEN // HG_CHUNK
HG_CHUNKS = N_TOK // HG_CHUNK


def _hg_gates(z, lb):
    e = jnp.exp(-jnp.abs(z))
    r = 1.0 / (1.0 + e)
    sig_pos = jnp.where(z >= 0, r, e * r)
    sig_neg = jnp.where(z >= 0, e * r, r)
    return lb + (1.0 - lb) * sig_pos, (1.0 - lb) * sig_neg


def _bcast_row(x, period, r):
    n, w = x.shape
    if period >= 8:
        x3 = x.reshape(n // period, period, w)
        return jnp.broadcast_to(x3[:, r:r + 1, :], x3.shape).reshape(n, w)
    x3 = x.reshape(n // 8, 8, w)
    sub = lax.broadcasted_iota(jnp.int32, (1, 8, 1), 1)
    out = None
    for j in range(8 // period):
        b = jnp.broadcast_to(x3[:, j * period + r:j * period + r + 1, :], x3.shape)
        out = b if out is None else jnp.where(sub >= j * period, b, out)
    return out.reshape(n, w)


def _hg_scans(f, isb):
    n = f.shape[0]
    row = lax.broadcasted_iota(jnp.int32, (n, 1), 0)
    p, r = f, jnp.ones_like(f)
    levels = []
    h, sh = 1, 0
    while h < n:
        levels.append((h, sh, p, r))
        up = (row >> sh) & 1
        tot_p = jnp.where(isb == 1, _bcast_row(p, 2 * h, h), _bcast_row(p, 2 * h, h - 1))
        tot_r = jnp.where(isb == 1, _bcast_row(p, 2 * h, 0), _bcast_row(p, 2 * h, 2 * h - 1))
        p = p * jnp.where(up != isb, tot_p, 1.0)
        r = r * jnp.where(up == isb, tot_r, 1.0)
        h, sh = 2 * h, sh + 1
    return levels, p, r


def _hg_state_body(zf_ref, zb_ref, vf_ref, vb_ref, lb_ref, s0_ref, sf_out, sb_out, s_scr):
    i = pl.program_id(0)

    @pl.when(i % HG_LAT_CHUNKS == 0)
    def _():
        s_scr[...] = s0_ref[0]

    sf_out[0] = s_scr[:, 0:HG_DK, :]
    sb_out[0] = s_scr[:, HG_DK:, :]
    lane5 = lax.broadcasted_iota(jnp.int32, (1, HG_W), 1)
    isb = (lane5 >> 6) & 1
    z = jnp.where(isb == 1, zb_ref[...], zf_ref[...])
    f, k = _hg_gates(z, lb_ref[...])
    r, ptot = _hg_chunk_decay(f, isb)
    kt = k * r
    lane = lax.broadcasted_iota(jnp.int32, (1, BRANCH), 1)
    vf = vf_ref[...]
    vb = vb_ref[...]
    for hd in range(HG_HEADS):
        sl = slice(hd * HG_HEAD_W, (hd + 1) * HG_HEAD_W)
        kth = kt[:, sl].T.astype(BF16)
        hm = (lane >> 6) == hd
        d_f = _dot(kth, jnp.where(hm, vf, 0.0).astype(BF16))
        d_b = _dot(kth, jnp.where(hm, vb, 0.0).astype(BF16))
        ds = jnp.concatenate([d_f[:HG_DK], d_b[HG_DK:]], axis=0)
        pcol = jnp.broadcast_to(ptot[:, sl], (HG_HEAD_W, HG_HEAD_W)).T[:, 0:1]
        s_scr[hd] = s_scr[hd] * pcol + ds


def _hg_chunk_decay(f, isb):
    n = f.shape[0]
    row = lax.broadcasted_iota(jnp.int32, (n, 1), 0)
    dist = jnp.where(isb == 1, row, n - 1 - row)
    x = f
    sh = 1
    while sh < n:
        src = jnp.where(isb == 1, pltpu.roll(x, sh, 0), pltpu.roll(x, n - sh, 0))
        x = x * jnp.where(dist >= sh, src, 1.0)
        sh *= 2
    total = jnp.where(isb == 1, x[n - 1:n], x[0:1])
    nxt = jnp.where(isb == 1, pltpu.roll(x, 1, 0), pltpu.roll(x, n - 1, 0))
    return jnp.where(dist >= 1, nxt, 1.0), total


HG_LAT_STEPS = N_LAT_SEQ * HG_LAT_CHUNKS


def _hg_lat_rev(i):
    return (i // HG_LAT_CHUNKS) * HG_LAT_CHUNKS + (HG_LAT_CHUNKS - 1 - i % HG_LAT_CHUNKS)


def _hg_states(z_c, lb, s0):
    first = N_CTX_SEQ
    zz_f = pl.BlockSpec((HG_CHUNK, HG_W), lambda i: (first + i, 1))
    zz_b = pl.BlockSpec((HG_CHUNK, HG_W), lambda i: (first + _hg_lat_rev(i), 1))
    v_f = pl.BlockSpec((HG_CHUNK, BRANCH), lambda i: (first + i, 4))
    v_b = pl.BlockSpec((HG_CHUNK, BRANCH), lambda i: (first + _hg_lat_rev(i), 4))
    st = (HG_HEADS, HG_HEAD_W, BRANCH)
    half = (HG_HEADS, HG_DK, BRANCH)
    return pl.pallas_call(
        _hg_state_body,
        grid=(HG_LAT_STEPS,),
        in_specs=[zz_f, zz_b, v_f, v_b,
                  pl.BlockSpec((1, HG_W), lambda i: (0, 0)),
                  pl.BlockSpec((1,) + st, lambda i: (i // HG_LAT_CHUNKS, 0, 0, 0))],
        out_specs=[pl.BlockSpec((1,) + half, lambda i: (i, 0, 0, 0)),
                   pl.BlockSpec((1,) + half, lambda i: (_hg_lat_rev(i), 0, 0, 0))],
        out_shape=[jax.ShapeDtypeStruct((HG_LAT_STEPS,) + half, F32),
                   jax.ShapeDtypeStruct((HG_LAT_STEPS,) + half, F32)],
        scratch_shapes=[pltpu.VMEM(st, F32)],
        compiler_params=_cparams("arbitrary"),
        name="hg_states",
    )(z_c, z_c, z_c, z_c, lb, s0)


def _hg_main_body(qq_ref, zz_ref, v_ref, g_ref, sf_ref, sb_ref, lb_ref, ng_ref, o_ref, fin_ref):
    n = HG_CHUNK
    i = pl.program_id(0)
    qq = qq_ref[...]
    lane5 = lax.broadcasted_iota(jnp.int32, (1, HG_W), 1)
    isb = (lane5 >> 6) & 1
    f, k = _hg_gates(zz_ref[...], lb_ref[...])
    levels, pfull, rfull = _hg_scans(f, isb)
    row = lax.broadcasted_iota(jnp.int32, (n, 1), 0)
    col = lax.broadcasted_iota(jnp.int32, (1, n), 1)
    ops = [(qq.astype(BF16), k.astype(BF16), row == col)]
    for h, sh, p, r in levels:
        up = (row >> sh) & 1
        qt = jnp.where(up != isb, qq * p, 0.0).astype(BF16)
        kt = jnp.where(up == isb, k * r, 0.0).astype(BF16)
        ops.append((qt, kt, (row >> (sh + 1)) == (col >> (sh + 1))))
    qc = (qq * pfull).astype(BF16)
    v = v_ref[...]
    vb = v.astype(BF16)
    lane = lax.broadcasted_iota(jnp.int32, (1, BRANCH), 1)
    latent = i >= N_CTX_SEQ
    acc = jnp.zeros((n, BRANCH), F32)
    for hd in range(HG_HEADS):
        sl = slice(hd * HG_HEAD_W, (hd + 1) * HG_HEAD_W)
        a = jnp.zeros((n, n), F32)
        for qt, kt, mask in ops:
            a = a + jnp.where(mask, _dot_nt(qt[:, sl], kt[:, sl]), 0.0)
        s_in = jnp.concatenate([sf_ref[0, hd], sb_ref[0, hd]], axis=0)
        s_in = jnp.where(latent, s_in, 0.0).astype(BF16)
        o_h = _dot(a.astype(BF16), vb) + _dot(qc[:, sl], s_in)
        acc = jnp.where((lane >> 6) == hd, o_h, acc)
    sq = acc * acc
    ms = jnp.zeros((n, BRANCH), F32)
    for hd in range(HG_HEADS):
        hm = (lane >> 6) == hd
        ms = jnp.where(hm, jnp.sum(jnp.where(hm, sq, 0.0), axis=-1, keepdims=True), ms)
    o_ref[...] = acc * lax.rsqrt(ms * (1.0 / HG_DK) + EPS) * ng_ref[...] * _silu(g_ref[...])

    @pl.when(i < N_CTX_SEQ)
    def _():
        kt_full = k * rfull
        for hd in range(HG_HEADS):
            kth = kt_full[:, hd * HG_HEAD_W:(hd + 1) * HG_HEAD_W].T.astype(BF16)
            ds = _dot(kth, jnp.where((lane >> 6) == hd, v, 0.0).astype(BF16))
            fin_ref[0, hd] = ds[:, hd * HG_DK:(hd + 1) * HG_DK]


def _hg_main(z_c, s_f, s_b, lb, norm_g):
    half = (1, HG_HEADS, HG_DK, BRANCH)
    lat = lambda i: (jnp.maximum(i - N_CTX_SEQ, 0), 0, 0, 0)
    fin = (HG_HEADS, HG_HEAD_W, HG_DK)
    return pl.pallas_call(
        _hg_main_body,
        grid=(HG_CHUNKS,),
        in_specs=[pl.BlockSpec((HG_CHUNK, HG_W), lambda i: (i, 0)),
                  pl.BlockSpec((HG_CHUNK, HG_W), lambda i: (i, 1)),
                  pl.BlockSpec((HG_CHUNK, BRANCH), lambda i: (i, 4)),
                  pl.BlockSpec((HG_CHUNK, BRANCH), lambda i: (i, 5)),
                  pl.BlockSpec(half, lat),
                  pl.BlockSpec(half, lat),
                  pl.BlockSpec((1, HG_W), lambda i: (0, 0)),
                  pl.BlockSpec((1, BRANCH), lambda i: (0, 0))],
        out_specs=[pl.BlockSpec((HG_CHUNK, BRANCH), lambda i: (i, 0)),
                   pl.BlockSpec((1,) + fin, lambda i: (jnp.minimum(i, N_CTX_SEQ - 1), 0, 0, 0))],
        out_shape=[jax.ShapeDtypeStruct((N_TOK, BRANCH), F32),
                   jax.ShapeDtypeStruct((N_CTX_SEQ,) + fin, F32)],
        compiler_params=_cparams("arbitrary"),
        name="hg_main",
    )(z_c, z_c, z_c, z_c, s_f, s_b, lb, norm_g)


def _take_cols(w, plan):
    idx = np.concatenate([p[0] for p in plan]).astype(np.int32)
    sign = np.concatenate([np.broadcast_to(p[1], p[0].shape) for p in plan]).astype(np.float32)
    return jnp.take(w, jnp.asarray(idx), axis=-1) * jnp.asarray(sign)


def _zeros(n):
    return (np.zeros(n, np.int64), 0.0)


_IN_OFF = {}
_off = 0
for _name, _n in (("da_q", 256), ("da_k", 256), ("da_v", 256), ("da_g", 256), ("s5_u", 256), ("s5_g", 256),
                  ("hg_q", 256), ("hg_ff", 256), ("hg_fb", 256), ("hg_i", 256), ("hg_g", 256),
                  ("mla_cq", MLA_Q_RANK), ("mla_ckv", MLA_KV_RANK), ("mla_kr", MLA_ROPE), ("mla_g", 256)):
    _IN_OFF[_name] = np.arange(_off, _off + _n)
    _off += _n


def _perm_matrix(plan, first, k):
    idx = np.concatenate([p[0] for p in plan]) - first
    sign = np.concatenate([np.broadcast_to(p[1], p[0].shape) for p in plan])
    m = np.zeros((k, len(idx)), np.float32)
    m[idx, np.arange(len(idx))] = sign
    return jnp.asarray(m, BF16)


def _in_proj_weights(w_in):
    c = _IN_OFF

    def per_head(x, y):
        return (np.concatenate([c[x].reshape(HG_HEADS, HG_DK), c[y].reshape(HG_HEADS, HG_DK)], axis=1).reshape(-1), 1.0)
    pdup = _perm_matrix([per_head("hg_q", "hg_q")], c["hg_q"][0], 256)
    pint = _perm_matrix([per_head("hg_ff", "hg_fb")], c["hg_ff"][0], 512)
    col = lambda name: w_in[..., c[name][0]:c[name][-1] + 1]
    zero = lambda n: jnp.zeros(w_in.shape[:-1] + (n,), w_in.dtype)
    w_d = jnp.concatenate([col("mla_cq"), zero(256 - MLA_Q_RANK), col("mla_ckv"), zero(MLA_NOPE), col("mla_kr"),
                           zero(128 - MLA_NOPE - MLA_ROPE), col("mla_g")], axis=-1)
    return _arrange_w_in(w_in, pdup, pint), w_d.astype(BF16)


def _rope_tables():
    t = np.arange(LAT_LEN)
    pos = np.stack([t // GRID_W, t % GRID_W], axis=1).astype(np.float32)
    inv_freq = (np.float32(ROPE_BASE) ** (-np.arange(8, dtype=np.float32) / np.float32(8))).astype(np.float32)
    r = np.arange(MLA_ROPE)
    ang = (pos[:, r // 16] * inv_freq[r % 8][None, :]).astype(np.float64)
    cos32, sin32 = np.cos(ang).astype(np.float32), np.sin(ang).astype(np.float32)
    lo = (np.arange(MLA_ROPE) % 16 < 8)[None, :]
    sin_lo32, sin_hi32 = np.where(lo, -sin32, 0.0).astype(np.float32), np.where(lo, 0.0, sin32).astype(np.float32)
    da_tabs = tuple(np.tile(x, (1, 8)) for x in (cos32, sin_lo32, sin_hi32))

    def head(x, fill):
        h = np.concatenate([np.full((LAT_LEN, MLA_NOPE), fill, np.float32), x,
                            np.full((LAT_LEN, MLA_HEAD_PAD - MLA_NOPE - MLA_ROPE), fill, np.float32)], axis=1)
        return np.concatenate([h, np.full((BIG_TILE, MLA_HEAD_PAD), fill, np.float32)], axis=0)
    k_tabs = (head(cos32, 1.0), head(sin_lo32, 0.0), head(sin_hi32, 0.0))
    return tuple(jnp.asarray(x) for x in da_tabs), tuple(jnp.asarray(x) for x in k_tabs)


def _mla_weights(w_uq, w_ukv, q_norm):
    hd = MLA_NOPE + MLA_ROPE
    pad_tail = _zeros(MLA_HEAD_PAD - hd)
    q_plan, k_plan, v_plan = [], [], []
    for h in range(MLA_HEADS):
        nope, rope = np.arange(h * hd, h * hd + MLA_NOPE), np.arange(h * hd + MLA_NOPE, (h + 1) * hd)
        q_plan += [(nope, 1.0), (rope, 1.0), pad_tail]
        k_plan += [(np.arange(h * 2 * MLA_NOPE, h * 2 * MLA_NOPE + MLA_NOPE), 1.0), _zeros(MLA_HEAD_PAD - MLA_NOPE)]
        v_plan += [(np.arange(h * 2 * MLA_NOPE + MLA_NOPE, (h + 1) * 2 * MLA_NOPE), 1.0)]
    pad_rows = lambda x: jnp.pad(x, ((0, 256 - MLA_Q_RANK), (0, 0))).astype(BF16)
    qn = jnp.pad(q_norm, (0, 256 - MLA_Q_RANK)).reshape(1, 256)
    return (pad_rows(_take_cols(w_uq, q_plan)), _take_cols(w_ukv, k_plan).astype(BF16),
            _take_cols(w_ukv, v_plan).astype(BF16), qn)


def _split_bf16(a):
    hi = a.astype(BF16)
    return hi, (a - hi.astype(F32)).astype(BF16)


def _dot_sel(a, sel):
    hi, lo = _split_bf16(a)
    sel = sel.astype(BF16)
    return _dot(hi, sel) + _dot(lo, sel)


def _dot_x3(a, b):
    a_hi, a_lo = _split_bf16(a)
    b_hi, b_lo = _split_bf16(b)
    return _dot(a_hi, b_hi) + _dot(a_hi, b_lo) + _dot(a_lo, b_hi)


def _s5_table_body(xy_ref, bb_ref, c_ref, ct_ref, mt_ref, bst_ref, cot_ref, a_ref):
    n, t, ch = S5_STATE, S5_CHUNK, S5_CH
    wide = 2 * S5_TAP
    xy = xy_ref[0, 0]
    tau_i = lax.broadcasted_iota(jnp.int32, (1, 128), 1)
    tau = tau_i.astype(F32)
    sel_row = lax.broadcasted_iota(jnp.int32, (128, 1), 0)

    def lag(width):
        return lax.broadcasted_iota(jnp.int32, (1, width), 1) >> 4

    def onehot(cond):
        return jnp.where(cond, 1.0, 0.0).astype(F32)
    j = lag(wide)
    e_z = (onehot((j <= t - 1) & (sel_row == t - 1 - j)), onehot((j >= t - 1) & (j <= 2 * t - 2) & (sel_row == j - (t - 1))))
    jc = lag(S5_TAP)
    e_c = (onehot(sel_row == jc + 1), onehot(sel_row == t - jc))
    ch_row = lax.broadcasted_iota(jnp.int32, (ch, 1), 0)
    tile_w = onehot((lax.broadcasted_iota(jnp.int32, (1, wide), 1) & (ch - 1)) == ch_row)
    tile_n = onehot((lax.broadcasted_iota(jnp.int32, (1, S5_TAP), 1) & (ch - 1)) == ch_row)

    z, cot_rows, klong = [], [], None
    for d in range(2):
        x, y = xy[:, 2 * d:2 * d + 1], xy[:, 2 * d + 1:2 * d + 2]
        mag = jnp.exp(jnp.where(tau_i <= t, tau, 0.0) * x)
        ang = jnp.where(tau_i <= t, tau, 0.0) * y
        p_re = jnp.where(tau_i <= t, mag * jnp.cos(ang), 0.0)
        p_im = jnp.where(tau_i <= t, mag * jnp.sin(ang), 0.0)
        a_ref[0, 0, 2 * d] = jnp.broadcast_to(p_re[:, t:t + 1], (n, 128))
        a_ref[0, 0, 2 * d + 1] = jnp.broadcast_to(p_im[:, t:t + 1], (n, 128))
        pz_re, pz_im = _dot_sel(p_re, e_z[d]), _dot_sel(p_im, e_z[d])
        b_re, b_im = _dot_sel(bb_ref[0, 0, 2 * d], tile_w), _dot_sel(bb_ref[0, 0, 2 * d + 1], tile_w)
        z_re, z_im = pz_re * b_re - pz_im * b_im, pz_re * b_im + pz_im * b_re
        z += [z_re, z_im]
        part = _dot_x3(c_ref[0, 0, 2 * d], z_re) - _dot_x3(c_ref[0, 0, 2 * d + 1], z_im)
        klong = part if klong is None else klong + part
        pc_re, pc_im = _dot_sel(p_re, e_c[d]), _dot_sel(p_im, e_c[d])
        c_re, c_im = _dot_sel(ct_ref[0, 0, 2 * d], tile_n), _dot_sel(ct_ref[0, 0, 2 * d + 1], tile_n)
        cot_rows += [c_re * pc_re - c_im * pc_im, -(c_re * pc_im + c_im * pc_re)]
    for tt in range(t):
        off = (t - 1 - tt) * ch
        win = klong if off == 0 else pltpu.roll(klong, wide - off, 1)
        mt_ref[0, 0, tt * ch:(tt + 1) * ch, :] = win[:, :S5_TAP].astype(BF16)
    back = pltpu.roll(z[2], wide - (t - 1) * ch, 1), pltpu.roll(z[3], wide - (t - 1) * ch, 1)
    for k, rows in enumerate((z[0], z[1], back[0], back[1])):
        bst_ref[0, 0, k * n:(k + 1) * n, :] = rows[:, :S5_TAP].astype(BF16)
    cot_ref[0, 0] = jnp.concatenate(cot_rows, axis=0).T.astype(BF16)


def _s5_tables(a_re, a_im, log_dt, b_re, b_im, c_re, c_im):
    nl, g, n, ch = a_re.shape[0], S5_GROUPS, S5_STATE, S5_CH
    step = jnp.exp(log_dt)[..., None]
    mag = jnp.exp(a_re * step)
    ab_re, ab_im = mag * jnp.cos(a_im * step), mag * jnp.sin(a_im * step)
    den = a_re * a_re + a_im * a_im
    f_re = ((ab_re - 1.0) * a_re + ab_im * a_im) / den
    f_im = (ab_im * a_re - (ab_re - 1.0) * a_im) / den
    bb_re = f_re[..., None] * b_re - f_im[..., None] * b_im
    bb_im = f_re[..., None] * b_im + f_im[..., None] * b_re
    by_group = lambda x: jnp.moveaxis(x, 1, 2)
    pair = lambda re, im: jnp.stack([by_group(re), by_group(im)], axis=3).reshape((nl, g, 4) + re.shape[3:])
    xy = jnp.stack([by_group(a_re * step), by_group(a_im * step)], axis=3).reshape(nl, g, 4, n)
    xy = jnp.pad(jnp.swapaxes(xy, 2, 3), ((0, 0), (0, 0), (0, 0), (0, 4)))
    mat = pl.BlockSpec((1, 1, S5_TAP, S5_TAP), lambda l, i: (l, i, 0, 0))
    return pl.pallas_call(
        _s5_table_body,
        grid=(nl, g),
        in_specs=[pl.BlockSpec((1, 1, n, 8), lambda l, i: (l, i, 0, 0)),
                  pl.BlockSpec((1, 1, 4, n, ch), lambda l, i: (l, i, 0, 0, 0)),
                  pl.BlockSpec((1, 1, 4, ch, n), lambda l, i: (l, i, 0, 0, 0)),
                  pl.BlockSpec((1, 1, 4, n, ch), lambda l, i: (l, i, 0, 0, 0))],
        out_specs=[mat, mat, mat, pl.BlockSpec((1, 1, 4, n, 128), lambda l, i: (l, i, 0, 0, 0))],
        out_shape=[jax.ShapeDtypeStruct((nl, g, S5_TAP, S5_TAP), BF16)] * 3
        + [jax.ShapeDtypeStruct((nl, g, 4, n, 128), F32)],
        compiler_params=_cparams("parallel", "parallel"),
        name="s5_tables",
    )(xy, pair(bb_re, bb_im), pair(c_re, c_im), pair(jnp.swapaxes(c_re, -1, -2), jnp.swapaxes(c_im, -1, -2)))


def _s5_chunk_lanes(u):
    return u.reshape(S5_NCHUNK, S5_CHUNK, BRANCH).transpose(1, 2, 0)


def _s5_token_rows(y):
    return y.transpose(2, 0, 1).reshape(N_TOK, BRANCH)


def kernel(x_prompt, x_sample, cache_diff_k, cache_diff_v, state_s5, state_hgrn, cache_mla_ckv, cache_mla_krope, c, c_ctx, w_mod, b_mod, w_in, w_out, da_lambda, da_norm, s5_a_re, s5_a_im, s5_log_dt, s5_b_re, s5_b_im, s5_c_re, s5_c_im, s5_d, s5_w_glu, hg_lb, hg_norm, mla_q_norm, mla_w_uq, mla_kv_norm, mla_w_ukv, final_norm):
    lb_w = jax.nn.softmax(hg_lb.astype(F32), axis=0)
    lb_all = jnp.cumsum(lb_w, axis=0) - lb_w[0:1]
    c_rows = jnp.concatenate([c_ctx[None], c, jnp.zeros((8 - 1 - N_LAT_SEQ, D_MODEL), F32)], axis=0)
    mods = _modulation(c_rows, w_mod, b_mod)
    da_tabs, mla_tabs = _rope_tables()
    xs = (x_prompt.reshape(N_CTX, D_MODEL), x_sample.reshape(N_LAT, D_MODEL))
    new_k, new_v, new_s5, new_hg, new_ckv, new_kr = [], [], [], [], [], []
    s5_tabs = _s5_tables(s5_a_re, s5_a_im, s5_log_dt, s5_b_re, s5_b_im, s5_c_re, s5_c_im)
    w_abc, w_d = _in_proj_weights(w_in)
    w_out_bf = w_out.astype(BF16)
    for l in range(DEPTH):
        mod = mods[l, :3].reshape(3, 3, D_MODEL)
        z_a, z_b, z_c, z_d, k_new, v_new = _in_proj(xs, mod, w_abc, w_d, l)

        lam_init = 0.8 - 0.6 * math.exp(-0.3 * l)
        kv_lat = _da_latent_kv(z_a, da_tabs,
                               cache_diff_k[:, l].reshape(N_LAT_SEQ, PAST_LEN, BRANCH),
                               cache_diff_v[:, l].reshape(N_LAT_SEQ, PAST_LEN, BRANCH))
        a_out = _da_attention(z_a, da_lambda[l], da_norm[l], lam_init, da_tabs, kv_lat)
        new_k.append(k_new.reshape(N_CTX_SEQ, CTX_LEN, DA_HEADS, 2 * DA_QK))
        new_v.append(v_new.reshape(N_CTX_SEQ, CTX_LEN, DA_HEADS, 2 * DA_QK))

        h0 = state_s5[:, l].transpose(2, 1, 4, 3, 0).reshape(S5_GROUPS, 4, S5_STATE, N_LAT_SEQ)
        h0 = jnp.pad(h0, ((0, 0), (0, 0), (0, 0), (0, 128 - N_LAT_SEQ)))
        y_all, fin = _s5_scan(_s5_chunk_lanes(z_b[:, :BRANCH]).astype(BF16), *s5_tabs, h0, l)
        b_out = _s5_out(_s5_token_rows(y_all), z_b, s5_d[l].reshape(1, BRANCH), s5_w_glu[l].astype(BF16))
        fin = jnp.stack([fin[:, 0:2, :, S5_CTX_SEQ_CH - 1::S5_CTX_SEQ_CH], fin[:, 2:4, :, 0::S5_CTX_SEQ_CH]], axis=1)
        new_s5.append(fin.transpose(4, 1, 0, 3, 2))

        lb = jnp.concatenate([lb_all[l, 0].reshape(HG_HEADS, HG_DK), lb_all[l, 1].reshape(HG_HEADS, HG_DK)],
                             axis=-1).reshape(1, HG_W)
        head_eye = jnp.eye(HG_HEADS, dtype=F32)
        s0 = state_hgrn[:, l].transpose(0, 2, 1, 3, 4).reshape(N_LAT_SEQ, HG_HEADS, HG_HEAD_W, 1, HG_DK)
        s0 = (s0 * head_eye[None, :, None, :, None]).reshape(N_LAT_SEQ, HG_HEADS, HG_HEAD_W, BRANCH)
        s_f, s_b = _hg_states(z_c, lb, s0)
        c_out, s_fin = _hg_main(z_c, s_f, s_b, lb, jnp.tile(hg_norm[l].reshape(1, HG_DK), (1, HG_HEADS)))
        new_hg.append(s_fin.reshape(N_CTX_SEQ, HG_HEADS, 2, HG_DK, HG_DK).transpose(0, 2, 1, 3, 4))

        wq, wk, wv, qn = _mla_weights(mla_w_uq[l], mla_w_ukv[l], mla_q_norm[l])
        q, ckv_n, kr = _mla_prep(z_d, mla_tabs, qn, mla_kv_norm[l].reshape(1, MLA_KV_RANK), wq)
        kr_cache = jnp.pad(cache_mla_krope[:, l], ((0, 0), (0, 0), (MLA_NOPE, 128 - MLA_NOPE - MLA_ROPE)))
        k_ctx, v_ctx, k_lat, v_lat = _mla_kv(ckv_n, kr, cache_mla_ckv[:, l], kr_cache, wk, wv)
        d_out = _mla_attention(z_d, q, k_ctx, v_ctx, k_lat, v_lat)
        new_ckv.append(ckv_n[:N_CTX].reshape(N_CTX_SEQ, CTX_LEN, MLA_KV_RANK))
        new_kr.append(kr[:N_CTX, MLA_NOPE:MLA_NOPE + MLA_ROPE].reshape(N_CTX_SEQ, CTX_LEN, MLA_ROPE))

        xs = _out_proj(a_out, b_out, c_out, d_out, xs, mod, w_out_bf, l,
                       final_norm.reshape(1, D_MODEL), final=(l == DEPTH - 1))
        xs = tuple(xs) if l == DEPTH - 1 else (xs,)
    y_prompt = xs[0].reshape(N_CTX_SEQ, CTX_LEN, D_MODEL)
    y_sample = xs[1].reshape(N_LAT_SEQ, LAT_LEN, D_MODEL)
    st = lambda parts: jnp.stack(parts, axis=1)
    return (y_prompt, y_sample, st(new_k), st(new_v), st(new_s5), st(new_hg), st(new_ckv), st(new_kr))
```

```python
import functools
import math

import numpy as np

import jax
import jax.numpy as jnp
from jax import lax
from jax.experimental import pallas as pl
from jax.experimental.pallas import tpu as pltpu

F32 = jnp.float32
BF16 = jnp.bfloat16

D_MODEL = 1024
DEPTH = 2
N_CTX_SEQ = 16
CTX_LEN = 256
N_LAT_SEQ = 2
LAT_LEN = 2048
PAST_LEN = 256
GRID_W = 64
N_CTX = N_CTX_SEQ * CTX_LEN
N_LAT = N_LAT_SEQ * LAT_LEN
N_TOK = N_CTX + N_LAT
BRANCH = 256
EPS = 1e-6
ROPE_BASE = 10000.0
ROW_TILE = 256
LAT_TILES = LAT_LEN // ROW_TILE
N_TILES = N_TOK // ROW_TILE
CTX_TILES = N_CTX // ROW_TILE
VMEM_LIMIT = 48 * 1024 * 1024
IN_PROJ_VMEM_LIMIT = 56 * 1024 * 1024
LAT_Q_TILE = 512
LAT_Q_TILES = LAT_LEN // LAT_Q_TILE
BIG_TILE = 512
CTX_SEQ_PER_STEP = 2

DA_HEADS = 4
DA_QK = 32
MLA_HEADS = 4
MLA_NOPE = 64
MLA_ROPE = 32
MLA_Q_RANK = 192
MLA_KV_RANK = 128
S5_GROUPS = 16
S5_CH = 16
S5_STATE = 64
S5_CHUNK = 16
HG_HEADS = 4
HG_DK = 64

W_A = 1024
W_B = 512
W_C = 1536
W_D = 768
W_ABC = W_A + W_B + W_C


def _cparams(*sem):
    return pltpu.CompilerParams(dimension_semantics=sem, vmem_limit_bytes=VMEM_LIMIT)


def _tile_seq(i, tile=ROW_TILE):
    return jnp.where(i < N_CTX // tile, 0, 1 + (i - N_CTX // tile) // (LAT_LEN // tile))


def _silu(x):
    return x * (1.0 / (1.0 + jnp.exp(-x)))


def _dot(a, b):
    return jnp.dot(a, b, preferred_element_type=F32)


def _dot_nt(a, b):
    return lax.dot_general(a, b, (((1,), (1,)), ((), ())), preferred_element_type=F32)


def _mod_body(c_ref, w_ref, b_ref, o_ref):
    c = _silu(c_ref[...]).astype(BF16)
    o_ref[0] = _dot(c, w_ref[0].astype(BF16)) + b_ref[0]


def _modulation(c_rows, w_mod, b_mod):
    tn = 768
    return pl.pallas_call(
        _mod_body,
        grid=(DEPTH, 3 * D_MODEL // tn),
        in_specs=[pl.BlockSpec((8, D_MODEL), lambda l, j: (0, 0)),
                  pl.BlockSpec((1, D_MODEL, tn), lambda l, j: (l, 0, j)),
                  pl.BlockSpec((1, 1, tn), lambda l, j: (l, 0, j))],
        out_specs=pl.BlockSpec((1, 8, tn), lambda l, j: (l, 0, j)),
        out_shape=jax.ShapeDtypeStruct((DEPTH, 8, 3 * D_MODEL), F32),
        compiler_params=_cparams("parallel", "parallel"),
        name="modulation",
    )(c_rows, w_mod, b_mod.reshape(DEPTH, 1, 3 * D_MODEL))


def _split_rows(i, ctx_ref, lat_ref):
    return jnp.where(i < N_CTX // ctx_ref.shape[0], ctx_ref[...], lat_ref[...])


def _ctx_tile_spec(w, tile=ROW_TILE):
    return pl.BlockSpec((tile, w), lambda i: (jnp.minimum(i, N_CTX // tile - 1), 0))


def _lat_tile_spec(w, tile=ROW_TILE):
    return pl.BlockSpec((tile, w), lambda i: (jnp.maximum(i - N_CTX // tile, 0), 0))


def _in_proj_body(*refs, split):
    if split:
        xc_ref, xl_ref, mod_ref, w_ref, wd_ref, oa, ob, oc, od, ok, ov, ou = refs
        x = _split_rows(pl.program_id(0), xc_ref, xl_ref)
    else:
        x_ref, mod_ref, w_ref, wd_ref, oa, ob, oc, od, ok, ov, ou = refs
        x = x_ref[...]
    xn = x * lax.rsqrt(jnp.mean(x * x, axis=-1, keepdims=True) + EPS)
    mod = mod_ref[0]
    h = (xn * (1.0 + mod[1:2]) + mod[0:1]).astype(BF16)
    off = 0
    for o in (oa, ob, oc):
        w = o.shape[-1]
        o[...] = _dot(h, w_ref[0, :, off:off + w])
        off += w
    od[...] = _dot(h, wd_ref[0])
    ou[...] = ob[:, :BRANCH].astype(BF16)

    @pl.when(pl.program_id(0) < N_CTX // BIG_TILE)
    def _():
        ok[...] = oa[:, BRANCH:2 * BRANCH]
        ov[...] = oa[:, 2 * BRANCH:3 * BRANCH]


def _in_proj(xs, mod, w_abc, w_d, l):
    widths = (W_A, W_B, W_C, W_D)
    split = len(xs) == 2
    x_specs = ([_ctx_tile_spec(D_MODEL, BIG_TILE), _lat_tile_spec(D_MODEL, BIG_TILE)] if split
               else [pl.BlockSpec((BIG_TILE, D_MODEL), lambda i: (i, 0))])
    return pl.pallas_call(
        functools.partial(_in_proj_body, split=split),
        grid=(N_TOK // BIG_TILE,),
        in_specs=x_specs + [pl.BlockSpec((1, 3, D_MODEL), lambda i: (_tile_seq(i, BIG_TILE), 0, 0)),
                            pl.BlockSpec((1, D_MODEL, W_ABC), lambda i: (l, 0, 0)),
                            pl.BlockSpec((1, D_MODEL, W_D), lambda i: (l, 0, 0))],
        out_specs=[pl.BlockSpec((BIG_TILE, w), lambda i: (i, 0)) for w in widths]
        + [_ctx_tile_spec(BRANCH, BIG_TILE)] * 2 + [pl.BlockSpec((BIG_TILE, BRANCH), lambda i: (i, 0))],
        out_shape=[jax.ShapeDtypeStruct((N_TOK, w), F32) for w in widths]
        + [jax.ShapeDtypeStruct((N_CTX, BRANCH), F32)] * 2 + [jax.ShapeDtypeStruct((N_TOK, BRANCH), BF16)],
        compiler_params=pltpu.CompilerParams(dimension_semantics=("arbitrary",), vmem_limit_bytes=IN_PROJ_VMEM_LIMIT),
        name="in_proj",
    )(*xs, mod, w_abc, w_d)


def _arrange_body(w_ref, pdup_ref, pint_ref, o_ref):
    w = w_ref[0].astype(BF16)

    def perm(x, p_ref):
        return _dot(x, p_ref[...]).astype(BF16)
    o_ref[0] = jnp.concatenate(
        [w[:, 0:1536],
         perm(w[:, 1536:1792], pdup_ref), perm(w[:, 1792:2304], pint_ref), w[:, 2304:2816]], axis=-1)


def _arrange_w_in(w_in, pdup, pint):
    rows = 128
    const = lambda a: pl.BlockSpec(a.shape, lambda l, i: (0, 0))
    return pl.pallas_call(
        _arrange_body,
        grid=(DEPTH, D_MODEL // rows),
        in_specs=[pl.BlockSpec((1, rows, w_in.shape[-1]), lambda l, i: (l, i, 0)), const(pdup), const(pint)],
        out_specs=pl.BlockSpec((1, rows, W_ABC), lambda l, i: (l, i, 0)),
        out_shape=jax.ShapeDtypeStruct((DEPTH, D_MODEL, W_ABC), BF16),
        compiler_params=_cparams("parallel", "parallel"),
        name="arrange_w_in",
    )(w_in, pdup, pint)


def _out_proj_body(*refs, split_in, final):
    ac_ref, al_ref, b_ref, c_ref, dc_ref, dl_ref = refs[:6]
    i = pl.program_id(0)
    if split_in:
        xc_ref, xl_ref, mod_ref, w_ref, fn_ref = refs[6:11]
        x = _split_rows(i, xc_ref, xl_ref)
    else:
        x_ref, mod_ref, w_ref, fn_ref = refs[6:10]
        x = x_ref[...]
    branches = (_split_rows(i, ac_ref, al_ref), b_ref[...], c_ref[...], _split_rows(i, dc_ref, dl_ref))
    acc = None
    for j, r in enumerate(branches):
        t = _dot(r.astype(BF16), w_ref[0, j * BRANCH:(j + 1) * BRANCH, :].astype(BF16))
        acc = t if acc is None else acc + t
    x = x + mod_ref[0][2:3] * acc
    if not final:
        refs[-1][...] = x
        return
    y = x * lax.rsqrt(jnp.mean(x * x, axis=-1, keepdims=True) + EPS) * fn_ref[...]
    yc_ref, yl_ref = refs[-2:]

    @pl.when(i < N_CTX // BIG_TILE)
    def _():
        yc_ref[...] = y

    @pl.when(i >= N_CTX // BIG_TILE)
    def _():
        yl_ref[...] = y


def _out_proj(a, b, c, d, xs, mod, w_out, l, final_norm, final):
    br = pl.BlockSpec((BIG_TILE, BRANCH), lambda i: (i, 0))
    pair = [_ctx_tile_spec(BRANCH, BIG_TILE), _lat_tile_spec(BRANCH, BIG_TILE)]
    split_in = len(xs) == 2
    x_specs = ([_ctx_tile_spec(D_MODEL, BIG_TILE), _lat_tile_spec(D_MODEL, BIG_TILE)] if split_in
               else [pl.BlockSpec((BIG_TILE, D_MODEL), lambda i: (i, 0))])
    if final:
        out_specs = [_ctx_tile_spec(D_MODEL, BIG_TILE), _lat_tile_spec(D_MODEL, BIG_TILE)]
        out_shape = [jax.ShapeDtypeStruct((N_CTX, D_MODEL), F32), jax.ShapeDtypeStruct((N_LAT, D_MODEL), F32)]
    else:
        out_specs = pl.BlockSpec((BIG_TILE, D_MODEL), lambda i: (i, 0))
        out_shape = jax.ShapeDtypeStruct((N_TOK, D_MODEL), F32)
    return pl.pallas_call(
        functools.partial(_out_proj_body, split_in=split_in, final=final),
        grid=(N_TOK // BIG_TILE,),
        in_specs=pair + [br, br] + pair + x_specs + [
            pl.BlockSpec((1, 3, D_MODEL), lambda i: (_tile_seq(i, BIG_TILE), 0, 0)),
            pl.BlockSpec((1, D_MODEL, D_MODEL), lambda i: (l, 0, 0)),
            pl.BlockSpec((1, D_MODEL), lambda i: (0, 0))],
        out_specs=out_specs,
        out_shape=out_shape,
        compiler_params=_cparams("arbitrary"),
        name="out_proj",
    )(*a, b, c, *d, *xs, mod, w_out, final_norm)


LOG2E = 1.4426950408889634


def _exp2_rows(s):
    e = jnp.exp2(s - jnp.max(s, axis=-1, keepdims=True))
    return e, jnp.sum(e, axis=-1, keepdims=True)


def _rope(x, cos, sin_lo, sin_hi):
    w = x.shape[-1]
    return x * cos + pltpu.roll(x, w - 8, 1) * sin_lo + pltpu.roll(x, 8, 1) * sin_hi


def _da_kv_body(k_ref, v_ref, cos_ref, slo_ref, shi_ref, ck_ref, cv_ref, ko_ref, vo_ref):
    j = pl.program_id(1)

    @pl.when(j < LAT_TILES)
    def _():
        ko_ref[0] = _rope(k_ref[...], cos_ref[...], slo_ref[...], shi_ref[...]).astype(BF16)
        vo_ref[0] = v_ref[...].astype(BF16)

    @pl.when(j == LAT_TILES)
    def _():
        ko_ref[0] = ck_ref[0].astype(BF16)
        vo_ref[0] = cv_ref[0].astype(BF16)


def _da_latent_kv(z_a, tabs, cache_k, cache_v):
    def rows(col):
        return pl.BlockSpec(
            (ROW_TILE, BRANCH),
            lambda b, j: (CTX_TILES + b * LAT_TILES + jnp.minimum(j, LAT_TILES - 1), col))
    tab = pl.BlockSpec((ROW_TILE, BRANCH), lambda b, j: (jnp.minimum(j, LAT_TILES - 1), 0))
    cache = pl.BlockSpec((1, PAST_LEN, BRANCH), lambda b, j: (b, 0, 0))
    out = pl.BlockSpec((1, ROW_TILE, BRANCH), lambda b, j: (b, j, 0))
    shp = jax.ShapeDtypeStruct((N_LAT_SEQ, LAT_LEN + PAST_LEN, BRANCH), BF16)
    return pl.pallas_call(
        _da_kv_body,
        grid=(N_LAT_SEQ, LAT_TILES + 1),
        in_specs=[rows(1), rows(2), tab, tab, tab, cache, cache],
        out_specs=[out, out],
        out_shape=[shp, shp],
        compiler_params=_cparams("parallel", "parallel"),
        name="da_kv",
    )(z_a, z_a, *tabs, cache_k, cache_v)


def _da_attn_body(lam_ref, ng_ref, q_ref, *rest, rope, lam_init):
    if rope:
        cos_ref, slo_ref, shi_ref, k_ref, v_ref, g_ref, o_ref = rest
        q = _rope(q_ref[...], cos_ref[...], slo_ref[...], shi_ref[...])
        o_ref[...] = _da_attn_tile(lam_ref, ng_ref, q, k_ref[0], v_ref[0], g_ref[...], lam_init)
    else:
        k_ref, v_ref, g_ref, o_ref = rest
        for t in range(q_ref.shape[0] // CTX_LEN):
            r = slice(t * CTX_LEN, (t + 1) * CTX_LEN)
            o_ref[r, :] = _da_attn_tile(lam_ref, ng_ref, q_ref[r, :], k_ref[r, :].astype(BF16),
                                        v_ref[r, :].astype(BF16), g_ref[r, :], lam_init)


def _da_attn_tile(lam_ref, ng_ref, q, k, v, g, lam_init):
    q = q * (DA_QK ** -0.5 * LOG2E)
    lv = lam_ref[...]
    lam = (jnp.exp(jnp.sum(lv[0:1] * lv[1:2], axis=-1, keepdims=True))
           - jnp.exp(jnp.sum(lv[2:3] * lv[3:4], axis=-1, keepdims=True)) + lam_init)
    lane = lax.broadcasted_iota(jnp.int32, (1, BRANCH), 1)
    acc = jnp.zeros(q.shape, F32)
    for h in range(DA_HEADS):
        q1 = jnp.where(lane // DA_QK == 2 * h, q, 0.0).astype(BF16)
        q2 = jnp.where(lane // DA_QK == 2 * h + 1, q, 0.0).astype(BF16)
        e1, l1 = _exp2_rows(_dot_nt(q1, k))
        e2, l2 = _exp2_rows(_dot_nt(q2, k))
        a = (e1 - (lam * l1 / l2) * e2).astype(BF16)
        acc = jnp.where(lane // (2 * DA_QK) == h, _dot(a, v) * (1.0 / l1), acc)
    sq = acc * acc
    ms = jnp.zeros(q.shape, F32)
    for h in range(DA_HEADS):
        hm = lane // (2 * DA_QK) == h
        ms = jnp.where(hm, jnp.sum(jnp.where(hm, sq, 0.0), axis=-1, keepdims=True), ms)
    o = acc * lax.rsqrt(ms * (1.0 / (2 * DA_QK)) + EPS) * (ng_ref[...] * (1.0 - lam_init))
    return o * _silu(g)


def _da_attention(z_a, lam_vec, norm_g, lam_init, tabs, kv_lat):
    ng = jnp.tile(norm_g.reshape(1, 2 * DA_QK), (1, DA_HEADS))
    small = [pl.BlockSpec((4, DA_QK), lambda *_: (0, 0)), pl.BlockSpec((1, BRANCH), lambda *_: (0, 0))]

    rows = CTX_SEQ_PER_STEP * CTX_LEN

    def col(c):
        return pl.BlockSpec((rows, BRANCH), lambda i: (i, c))
    ctx = pl.pallas_call(
        functools.partial(_da_attn_body, rope=False, lam_init=lam_init),
        grid=(N_CTX // rows,),
        in_specs=small + [col(0), col(1), col(2), col(3)],
        out_specs=pl.BlockSpec((rows, BRANCH), lambda i: (i, 0)),
        out_shape=jax.ShapeDtypeStruct((N_CTX, BRANCH), F32),
        compiler_params=_cparams("parallel"),
        name="da_attn_ctx",
    )(lam_vec, ng, z_a, z_a, z_a, z_a)

    def lcol(c):
        return pl.BlockSpec((LAT_Q_TILE, BRANCH), lambda b, j: (N_CTX // LAT_Q_TILE + b * LAT_Q_TILES + j, c))
    tab = pl.BlockSpec((LAT_Q_TILE, BRANCH), lambda b, j: (j, 0))
    kvs = pl.BlockSpec((1, LAT_LEN + PAST_LEN, BRANCH), lambda b, j: (b, 0, 0))
    lat = pl.pallas_call(
        functools.partial(_da_attn_body, rope=True, lam_init=lam_init),
        grid=(N_LAT_SEQ, LAT_Q_TILES),
        in_specs=small + [lcol(0), tab, tab, tab, kvs, kvs, lcol(3)],
        out_specs=pl.BlockSpec((LAT_Q_TILE, BRANCH), lambda b, j: (b * LAT_Q_TILES + j, 0)),
        out_shape=jax.ShapeDtypeStruct((N_LAT, BRANCH), F32),
        compiler_params=_cparams("parallel", "parallel"),
        name="da_attn_lat",
    )(lam_vec, ng, z_a, *tabs, kv_lat[0], kv_lat[1], z_a)
    return ctx, lat


MLA_HEAD_PAD = 128
MLA_QW = MLA_HEADS * MLA_HEAD_PAD


def _mla_prep_body(cq_ref, ckv_ref, kr_ref, ck_t, sk_lo, sk_hi, qn_ref, kvn_ref, wq_ref, q_out, ckv_out, kr_out):
    cq = cq_ref[...]
    ms = jnp.sum(cq * cq, axis=-1, keepdims=True) * (1.0 / MLA_Q_RANK)
    qn = (cq * lax.rsqrt(ms + EPS) * qn_ref[...]).astype(BF16)
    heads = lambda t: jnp.concatenate([t[...]] * MLA_HEADS, axis=-1)
    q = _rope(_dot(qn, wq_ref[...]), heads(ck_t), heads(sk_lo), heads(sk_hi))
    q_out[...] = (q * ((MLA_NOPE + MLA_ROPE) ** -0.5 * LOG2E)).astype(BF16)
    ckv = ckv_ref[...]
    ckv_out[...] = ckv * lax.rsqrt(jnp.mean(ckv * ckv, axis=-1, keepdims=True) + EPS) * kvn_ref[...]
    kr_out[...] = _rope(kr_ref[...], ck_t[...], sk_lo[...], sk_hi[...])


def _mla_prep(z_d, tabs, q_norm_pad, kv_norm, wq):
    ctx_tiles, lat_tiles = N_CTX // BIG_TILE, LAT_LEN // BIG_TILE

    def tab(w):
        return pl.BlockSpec((BIG_TILE, w), lambda i: (jnp.where(i < ctx_tiles, lat_tiles, (i - ctx_tiles) % lat_tiles), 0))

    def col(w, c):
        return pl.BlockSpec((BIG_TILE, w), lambda i: (i, c))

    def const(shape):
        return pl.BlockSpec(shape, lambda i: (0, 0))
    return pl.pallas_call(
        _mla_prep_body,
        grid=(N_TOK // BIG_TILE,),
        in_specs=[col(256, 0), col(128, 2), col(128, 3),
                  tab(128), tab(128), tab(128),
                  const((1, 256)), const((1, 128)), const((256, MLA_QW))],
        out_specs=[col(MLA_QW, 0), col(128, 0), col(128, 0)],
        out_shape=[jax.ShapeDtypeStruct((N_TOK, MLA_QW), BF16),
                   jax.ShapeDtypeStruct((N_TOK, 128), F32),
                   jax.ShapeDtypeStruct((N_TOK, 128), F32)],
        compiler_params=_cparams("parallel"),
        name="mla_prep",
    )(z_d, z_d, z_d, *tabs, q_norm_pad, kv_norm, wq)


def _mla_kv_math(ckv, kr, wk_ref, wv_ref, k_out, v_out):
    c = ckv.astype(BF16)
    k_out[...] = (_dot(c, wk_ref[...]) + jnp.concatenate([kr] * MLA_HEADS, axis=-1)).astype(BF16).reshape(k_out.shape)
    v_out[...] = _dot(c, wv_ref[...]).astype(BF16).reshape(v_out.shape)


def _mla_kv_ctx_body(ckv_ref, kr_ref, wk_ref, wv_ref, k_out, v_out):
    _mla_kv_math(ckv_ref[...], kr_ref[...], wk_ref, wv_ref, k_out, v_out)


def _mla_kv_lat_body(ckv_ref, kr_ref, cckv_ref, ckr_ref, wk_ref, wv_ref, k_out, v_out):
    j = pl.program_id(1)

    @pl.when(j < LAT_TILES)
    def _():
        _mla_kv_math(ckv_ref[...], kr_ref[...], wk_ref, wv_ref, k_out, v_out)

    @pl.when(j == LAT_TILES)
    def _():
        _mla_kv_math(cckv_ref[0], ckr_ref[0], wk_ref, wv_ref, k_out, v_out)


def _mla_kv(ckv, kr, cache_ckv, cache_kr, wk, wv):
    weights = [pl.BlockSpec((128, MLA_QW), lambda *_: (0, 0)), pl.BlockSpec((128, BRANCH), lambda *_: (0, 0))]
    k_ctx, v_ctx = pl.pallas_call(
        _mla_kv_ctx_body,
        grid=(N_CTX // BIG_TILE,),
        in_specs=[pl.BlockSpec((BIG_TILE, 128), lambda i: (i, 0)), pl.BlockSpec((BIG_TILE, 128), lambda i: (i, 0))] + weights,
        out_specs=[pl.BlockSpec((BIG_TILE, MLA_QW), lambda i: (i, 0)),
                   pl.BlockSpec((BIG_TILE, BRANCH), lambda i: (i, 0))],
        out_shape=[jax.ShapeDtypeStruct((N_CTX, MLA_QW), BF16), jax.ShapeDtypeStruct((N_CTX, BRANCH), BF16)],
        compiler_params=_cparams("parallel"),
        name="mla_kv_ctx",
    )(ckv, kr, wk, wv)
    rows = pl.BlockSpec((ROW_TILE, 128), lambda b, j: (CTX_TILES + b * LAT_TILES + jnp.minimum(j, LAT_TILES - 1), 0))
    cache = pl.BlockSpec((1, PAST_LEN, 128), lambda b, j: (b, 0, 0))
    lk = LAT_LEN + PAST_LEN
    k_lat, v_lat = pl.pallas_call(
        _mla_kv_lat_body,
        grid=(N_LAT_SEQ, LAT_TILES + 1),
        in_specs=[rows, rows, cache, cache] + weights,
        out_specs=[pl.BlockSpec((1, ROW_TILE, MLA_QW), lambda b, j: (b, j, 0)),
                   pl.BlockSpec((1, ROW_TILE, BRANCH), lambda b, j: (b, j, 0))],
        out_shape=[jax.ShapeDtypeStruct((N_LAT_SEQ, lk, MLA_QW), BF16), jax.ShapeDtypeStruct((N_LAT_SEQ, lk, BRANCH), BF16)],
        compiler_params=_cparams("parallel", "parallel"),
        name="mla_kv_lat",
    )(ckv, kr, cache_ckv, cache_kr, wk, wv)
    return k_ctx, v_ctx, k_lat, v_lat


def _mla_attn_body(q_ref, k_ref, v_ref, g_ref, o_ref, *, ctx):
    if ctx:
        for t in range(q_ref.shape[0] // CTX_LEN):
            r = slice(t * CTX_LEN, (t + 1) * CTX_LEN)
            o_ref[r, :] = _mla_attn_tile(q_ref[r, :], k_ref[r, :], v_ref[r, :], g_ref[r, :])
    else:
        o_ref[...] = _mla_attn_tile(q_ref[...], k_ref[0], v_ref[0], g_ref[...])


def _mla_attn_tile(q, k, v, g):
    lane = lax.broadcasted_iota(jnp.int32, (1, BRANCH), 1)
    acc = jnp.zeros((q.shape[0], BRANCH), F32)
    for h in range(MLA_HEADS):
        sl = slice(h * MLA_HEAD_PAD, (h + 1) * MLA_HEAD_PAD)
        e, l = _exp2_rows(_dot_nt(q[:, sl], k[:, sl]))
        acc = jnp.where(lane // 64 == h, _dot(e.astype(BF16), v) * (1.0 / l), acc)
    return acc * _silu(g)


def _mla_attention(z_d, q, k_ctx, v_ctx, k_lat, v_lat):
    rows = CTX_SEQ_PER_STEP * CTX_LEN
    ctx = pl.pallas_call(
        functools.partial(_mla_attn_body, ctx=True),
        grid=(N_CTX // rows,),
        in_specs=[pl.BlockSpec((rows, MLA_QW), lambda i: (i, 0)),
                  pl.BlockSpec((rows, MLA_QW), lambda i: (i, 0)),
                  pl.BlockSpec((rows, BRANCH), lambda i: (i, 0)),
                  pl.BlockSpec((rows, BRANCH), lambda i: (i, 2))],
        out_specs=pl.BlockSpec((rows, BRANCH), lambda i: (i, 0)),
        out_shape=jax.ShapeDtypeStruct((N_CTX, BRANCH), F32),
        compiler_params=_cparams("parallel"),
        name="mla_attn_ctx",
    )(q, k_ctx, v_ctx, z_d)
    lk = LAT_LEN + PAST_LEN
    lat = pl.pallas_call(
        functools.partial(_mla_attn_body, ctx=False),
        grid=(N_LAT_SEQ, LAT_Q_TILES),
        in_specs=[pl.BlockSpec((LAT_Q_TILE, MLA_QW), lambda b, j: (N_CTX // LAT_Q_TILE + b * LAT_Q_TILES + j, 0)),
                  pl.BlockSpec((1, lk, MLA_QW), lambda b, j: (b, 0, 0)),
                  pl.BlockSpec((1, lk, BRANCH), lambda b, j: (b, 0, 0)),
                  pl.BlockSpec((LAT_Q_TILE, BRANCH), lambda b, j: (N_CTX // LAT_Q_TILE + b * LAT_Q_TILES + j, 2))],
        out_specs=pl.BlockSpec((LAT_Q_TILE, BRANCH), lambda b, j: (b * LAT_Q_TILES + j, 0)),
        out_shape=jax.ShapeDtypeStruct((N_LAT, BRANCH), F32),
        compiler_params=_cparams("parallel", "parallel"),
        name="mla_attn_lat",
    )(q, k_lat, v_lat, z_d)
    return ctx, lat


S5_TAP = S5_CHUNK * S5_CH
S5_NCHUNK = N_TOK // S5_CHUNK
S5_CTX_CH = N_CTX // S5_CHUNK
S5_CTX_SEQ_CH = CTX_LEN // S5_CHUNK
S5_LAT_SEQ_CH = LAT_LEN // S5_CHUNK
S5_SCAN_STEPS = S5_LAT_SEQ_CH.bit_length() - 1


def _s5_body(x_ref, mt_ref, bst_ref, cot_ref, a_ref, h0_ref, y_ref, fin_ref):
    x = x_ref[...].reshape(S5_TAP, S5_NCHUNK)
    y = _dot(mt_ref[0, 0], x)
    s = _dot(bst_ref[0, 0], x)
    lane = lax.broadcasted_iota(jnp.int32, (1, S5_NCHUNK), 1)
    is_lat = lane >= S5_CTX_CH
    pos_f = jnp.where(is_lat, (lane - S5_CTX_CH) & (S5_LAT_SEQ_CH - 1), lane & (S5_CTX_SEQ_CH - 1))
    pos_b = jnp.where(is_lat, S5_LAT_SEQ_CH - 1, S5_CTX_SEQ_CH - 1) - pos_f
    hin = []
    for d in range(2):
        n = S5_STATE
        sre, sim = s[2 * d * n:(2 * d + 1) * n], s[(2 * d + 1) * n:(2 * d + 2) * n]
        are = jnp.concatenate([a_ref[0, 0, 2 * d]] * (S5_NCHUNK // 128), axis=-1)
        aim = jnp.concatenate([a_ref[0, 0, 2 * d + 1]] * (S5_NCHUNK // 128), axis=-1)
        pos = pos_f if d == 0 else pos_b
        h0r, h0i = jnp.zeros_like(sre), jnp.zeros_like(sre)
        for b in range(N_LAT_SEQ):
            first = S5_CTX_CH + b * S5_LAT_SEQ_CH + (0 if d == 0 else S5_LAT_SEQ_CH - 1)
            h0r = jnp.where(lane == first, h0_ref[0, 2 * d][:, b:b + 1], h0r)
            h0i = jnp.where(lane == first, h0_ref[0, 2 * d + 1][:, b:b + 1], h0i)
        xr = sre + are * h0r - aim * h0i
        xi = sim + are * h0i + aim * h0r
        pr, pi = are, aim
        for j in range(S5_SCAN_STEPS):
            sh = 1 << j
            shift = sh if d == 0 else S5_NCHUNK - sh
            rr, ri = pltpu.roll(xr, shift, 1), pltpu.roll(xi, shift, 1)
            ok = pos >= sh
            xr, xi = (xr + jnp.where(ok, pr * rr - pi * ri, 0.0), xi + jnp.where(ok, pr * ri + pi * rr, 0.0))
            pr, pi = pr * pr - pi * pi, 2.0 * pr * pi
        fin_ref[0, 2 * d] = xr[:, :S5_CTX_CH]
        fin_ref[0, 2 * d + 1] = xi[:, :S5_CTX_CH]
        one = 1 if d == 0 else S5_NCHUNK - 1
        hin.append(jnp.where(pos >= 1, pltpu.roll(xr, one, 1), h0r))
        hin.append(jnp.where(pos >= 1, pltpu.roll(xi, one, 1), h0i))
    y = y + _dot(cot_ref[0, 0], jnp.concatenate(hin, axis=0).astype(BF16))
    y_ref[...] = y.reshape(S5_CHUNK, S5_CH, S5_NCHUNK)


def _s5_scan(x_all, mt, bst, cot, a16, h0, l):
    g = S5_GROUPS
    sq = pl.BlockSpec((1, 1, S5_TAP, S5_TAP), lambda i: (l, i, 0, 0))
    st = pl.BlockSpec((1, 4, S5_STATE, 128), lambda i: (i, 0, 0, 0))
    return pl.pallas_call(
        _s5_body,
        grid=(g,),
        in_specs=[pl.BlockSpec((S5_CHUNK, S5_CH, S5_NCHUNK), lambda i: (0, i, 0)), sq, sq, sq,
                  pl.BlockSpec((1, 1, 4, S5_STATE, 128), lambda i: (l, i, 0, 0, 0)), st],
        out_specs=[pl.BlockSpec((S5_CHUNK, S5_CH, S5_NCHUNK), lambda i: (0, i, 0)),
                   pl.BlockSpec((1, 4, S5_STATE, S5_CTX_CH), lambda i: (i, 0, 0, 0))],
        out_shape=[jax.ShapeDtypeStruct((S5_CHUNK, BRANCH, S5_NCHUNK), F32),
                   jax.ShapeDtypeStruct((g, 4, S5_STATE, S5_CTX_CH), F32)],
        compiler_params=_cparams("parallel"),
        name="s5_scan",
    )(x_all, mt, bst, cot, a16, h0)


def _s5_out_body(y_ref, u_ref, g_ref, d_ref, w_ref, o_ref):
    y = u_ref[...] * d_ref[...] + y_ref[...]
    ge = 0.5 * y * (1.0 + jnp.tanh(0.7978845608028654 * (y + 0.044715 * (y * y * y))))
    gl = _dot(ge.astype(BF16), w_ref[...])
    o_ref[...] = gl[:, :BRANCH] * (1.0 / (1.0 + jnp.exp(-gl[:, BRANCH:]))) * _silu(g_ref[...])


def _s5_out(y_ssm, z_b, d_skip, w_glu):
    tile = 2 * BIG_TILE

    def col(c):
        return pl.BlockSpec((tile, BRANCH), lambda i: (i, c))
    return pl.pallas_call(
        _s5_out_body,
        grid=(N_TOK // tile,),
        in_specs=[col(0), col(0), col(1),
                  pl.BlockSpec((1, BRANCH), lambda i: (0, 0)),
                  pl.BlockSpec((BRANCH, 2 * BRANCH), lambda i: (0, 0))],
        out_specs=col(0),
        out_shape=jax.ShapeDtypeStruct((N_TOK, BRANCH), F32),
        compiler_params=_cparams("parallel"),
        name="s5_out",
    )(y_ssm, z_b, z_b, d_skip, w_glu)


HG_CHUNK = ROW_TILE
HG_W = 2 * HG_HEADS * HG_DK
HG_HEAD_W = 2 * HG_DK
HG_LAT_CHUNKS = LAT_LEN // HG_CHUNK
HG_CHUNKS = N_TOK // HG_CHUNK


def _hg_gates(z, lb):
    e = jnp.exp(-jnp.abs(z))
    r = 1.0 / (1.0 + e)
    sig_pos = jnp.where(z >= 0, r, e * r)
    sig_neg = jnp.where(z >= 0, e * r, r)
    return lb + (1.0 - lb) * sig_pos, (1.0 - lb) * sig_neg


def _bcast_row(x, period, r):
    n, w = x.shape
    if period >= 8:
        x3 = x.reshape(n // period, period, w)
        return jnp.broadcast_to(x3[:, r:r + 1, :], x3.shape).reshape(n, w)
    x3 = x.reshape(n // 8, 8, w)
    sub = lax.broadcasted_iota(jnp.int32, (1, 8, 1), 1)
    out = None
    for j in range(8 // period):
        b = jnp.broadcast_to(x3[:, j * period + r:j * period + r + 1, :], x3.shape)
        out = b if out is None else jnp.where(sub >= j * period, b, out)
    return out.reshape(n, w)


def _hg_scans(f, isb):
    n = f.shape[0]
    row = lax.broadcasted_iota(jnp.int32, (n, 1), 0)
    p, r = f, jnp.ones_like(f)
    levels = []
    h, sh = 1, 0
    while h < n:
        levels.append((h, sh, p, r))
        up = (row >> sh) & 1
        tot_p = jnp.where(isb == 1, _bcast_row(p, 2 * h, h), _bcast_row(p, 2 * h, h - 1))
        tot_r = jnp.where(isb == 1, _bcast_row(p, 2 * h, 0), _bcast_row(p, 2 * h, 2 * h - 1))
        p = p * jnp.where(up != isb, tot_p, 1.0)
        r = r * jnp.where(up == isb, tot_r, 1.0)
        h, sh = 2 * h, sh + 1
    return levels, p, r


def _hg_state_body(zf_ref, zb_ref, vf_ref, vb_ref, lb_ref, s0_ref, sf_out, sb_out, s_scr):
    i = pl.program_id(0)

    @pl.when(i % HG_LAT_CHUNKS == 0)
    def _():
        s_scr[...] = s0_ref[0]

    sf_out[0] = s_scr[:, 0:HG_DK, :]
    sb_out[0] = s_scr[:, HG_DK:, :]
    lane5 = lax.broadcasted_iota(jnp.int32, (1, HG_W), 1)
    isb = (lane5 >> 6) & 1
    z = jnp.where(isb == 1, zb_ref[...], zf_ref[...])
    f, k = _hg_gates(z, lb_ref[...])
    r, ptot = _hg_chunk_decay(f, isb)
    kt = k * r
    lane = lax.broadcasted_iota(jnp.int32, (1, BRANCH), 1)
    vf = vf_ref[...]
    vb = vb_ref[...]
    for hd in range(HG_HEADS):
        sl = slice(hd * HG_HEAD_W, (hd + 1) * HG_HEAD_W)
        kth = kt[:, sl].T.astype(BF16)
        hm = (lane >> 6) == hd
        d_f = _dot(kth, jnp.where(hm, vf, 0.0).astype(BF16))
        d_b = _dot(kth, jnp.where(hm, vb, 0.0).astype(BF16))
        ds = jnp.concatenate([d_f[:HG_DK], d_b[HG_DK:]], axis=0)
        pcol = jnp.broadcast_to(ptot[:, sl], (HG_HEAD_W, HG_HEAD_W)).T[:, 0:1]
        s_scr[hd] = s_scr[hd] * pcol + ds


def _hg_chunk_decay(f, isb):
    n = f.shape[0]
    row = lax.broadcasted_iota(jnp.int32, (n, 1), 0)
    dist = jnp.where(isb == 1, row, n - 1 - row)
    x = f
    sh = 1
    while sh < n:
        src = jnp.where(isb == 1, pltpu.roll(x, sh, 0), pltpu.roll(x, n - sh, 0))
        x = x * jnp.where(dist >= sh, src, 1.0)
        sh *= 2
    total = jnp.where(isb == 1, x[n - 1:n], x[0:1])
    nxt = jnp.where(isb == 1, pltpu.roll(x, 1, 0), pltpu.roll(x, n - 1, 0))
    return jnp.where(dist >= 1, nxt, 1.0), total


HG_LAT_STEPS = N_LAT_SEQ * HG_LAT_CHUNKS


def _hg_lat_rev(i):
    return (i // HG_LAT_CHUNKS) * HG_LAT_CHUNKS + (HG_LAT_CHUNKS - 1 - i % HG_LAT_CHUNKS)


def _hg_states(z_c, lb, s0):
    first = N_CTX_SEQ
    zz_f = pl.BlockSpec((HG_CHUNK, HG_W), lambda i: (first + i, 1))
    zz_b = pl.BlockSpec((HG_CHUNK, HG_W), lambda i: (first + _hg_lat_rev(i), 1))
    v_f = pl.BlockSpec((HG_CHUNK, BRANCH), lambda i: (first + i, 4))
    v_b = pl.BlockSpec((HG_CHUNK, BRANCH), lambda i: (first + _hg_lat_rev(i), 4))
    st = (HG_HEADS, HG_HEAD_W, BRANCH)
    half = (HG_HEADS, HG_DK, BRANCH)
    return pl.pallas_call(
        _hg_state_body,
        grid=(HG_LAT_STEPS,),
        in_specs=[zz_f, zz_b, v_f, v_b,
                  pl.BlockSpec((1, HG_W), lambda i: (0, 0)),
                  pl.BlockSpec((1,) + st, lambda i: (i // HG_LAT_CHUNKS, 0, 0, 0))],
        out_specs=[pl.BlockSpec((1,) + half, lambda i: (i, 0, 0, 0)),
                   pl.BlockSpec((1,) + half, lambda i: (_hg_lat_rev(i), 0, 0, 0))],
        out_shape=[jax.ShapeDtypeStruct((HG_LAT_STEPS,) + half, F32),
                   jax.ShapeDtypeStruct((HG_LAT_STEPS,) + half, F32)],
        scratch_shapes=[pltpu.VMEM(st, F32)],
        compiler_params=_cparams("arbitrary"),
        name="hg_states",
    )(z_c, z_c, z_c, z_c, lb, s0)


def _hg_main_body(qq_ref, zz_ref, v_ref, g_ref, sf_ref, sb_ref, lb_ref, ng_ref, o_ref, fin_ref):
    n = HG_CHUNK
    i = pl.program_id(0)
    qq = qq_ref[...]
    lane5 = lax.broadcasted_iota(jnp.int32, (1, HG_W), 1)
    isb = (lane5 >> 6) & 1
    f, k = _hg_gates(zz_ref[...], lb_ref[...])
    levels, pfull, rfull = _hg_scans(f, isb)
    row = lax.broadcasted_iota(jnp.int32, (n, 1), 0)
    col = lax.broadcasted_iota(jnp.int32, (1, n), 1)
    ops = [(qq.astype(BF16), k.astype(BF16), row == col)]
    for h, sh, p, r in levels:
        up = (row >> sh) & 1
        qt = jnp.where(up != isb, qq * p, 0.0).astype(BF16)
        kt = jnp.where(up == isb, k * r, 0.0).astype(BF16)
        ops.append((qt, kt, (row >> (sh + 1)) == (col >> (sh + 1))))
    qc = (qq * pfull).astype(BF16)
    v = v_ref[...]
    vb = v.astype(BF16)
    lane = lax.broadcasted_iota(jnp.int32, (1, BRANCH), 1)
    latent = i >= N_CTX_SEQ
    acc = jnp.zeros((n, BRANCH), F32)
    for hd in range(HG_HEADS):
        sl = slice(hd * HG_HEAD_W, (hd + 1) * HG_HEAD_W)
        a = jnp.zeros((n, n), F32)
        for qt, kt, mask in ops:
            a = a + jnp.where(mask, _dot_nt(qt[:, sl], kt[:, sl]), 0.0)
        s_in = jnp.concatenate([sf_ref[0, hd], sb_ref[0, hd]], axis=0)
        s_in = jnp.where(latent, s_in, 0.0).astype(BF16)
        o_h = _dot(a.astype(BF16), vb) + _dot(qc[:, sl], s_in)
        acc = jnp.where((lane >> 6) == hd, o_h, acc)
    sq = acc * acc
    ms = jnp.zeros((n, BRANCH), F32)
    for hd in range(HG_HEADS):
        hm = (lane >> 6) == hd
        ms = jnp.where(hm, jnp.sum(jnp.where(hm, sq, 0.0), axis=-1, keepdims=True), ms)
    o_ref[...] = acc * lax.rsqrt(ms * (1.0 / HG_DK) + EPS) * ng_ref[...] * _silu(g_ref[...])

    @pl.when(i < N_CTX_SEQ)
    def _():
        kt_full = k * rfull
        for hd in range(HG_HEADS):
            kth = kt_full[:, hd * HG_HEAD_W:(hd + 1) * HG_HEAD_W].T.astype(BF16)
            ds = _dot(kth, jnp.where((lane >> 6) == hd, v, 0.0).astype(BF16))
            fin_ref[0, hd] = ds[:, hd * HG_DK:(hd + 1) * HG_DK]


def _hg_main(z_c, s_f, s_b, lb, norm_g):
    half = (1, HG_HEADS, HG_DK, BRANCH)
    lat = lambda i: (jnp.maximum(i - N_CTX_SEQ, 0), 0, 0, 0)
    fin = (HG_HEADS, HG_HEAD_W, HG_DK)
    return pl.pallas_call(
        _hg_main_body,
        grid=(HG_CHUNKS,),
        in_specs=[pl.BlockSpec((HG_CHUNK, HG_W), lambda i: (i, 0)),
                  pl.BlockSpec((HG_CHUNK, HG_W), lambda i: (i, 1)),
                  pl.BlockSpec((HG_CHUNK, BRANCH), lambda i: (i, 4)),
                  pl.BlockSpec((HG_CHUNK, BRANCH), lambda i: (i, 5)),
                  pl.BlockSpec(half, lat),
                  pl.BlockSpec(half, lat),
                  pl.BlockSpec((1, HG_W), lambda i: (0, 0)),
                  pl.BlockSpec((1, BRANCH), lambda i: (0, 0))],
        out_specs=[pl.BlockSpec((HG_CHUNK, BRANCH), lambda i: (i, 0)),
                   pl.BlockSpec((1,) + fin, lambda i: (jnp.minimum(i, N_CTX_SEQ - 1), 0, 0, 0))],
        out_shape=[jax.ShapeDtypeStruct((N_TOK, BRANCH), F32),
                   jax.ShapeDtypeStruct((N_CTX_SEQ,) + fin, F32)],
        compiler_params=_cparams("arbitrary"),
        name="hg_main",
    )(z_c, z_c, z_c, z_c, s_f, s_b, lb, norm_g)


def _take_cols(w, plan):
    idx = np.concatenate([p[0] for p in plan]).astype(np.int32)
    sign = np.concatenate([np.broadcast_to(p[1], p[0].shape) for p in plan]).astype(np.float32)
    return jnp.take(w, jnp.asarray(idx), axis=-1) * jnp.asarray(sign)


def _zeros(n):
    return (np.zeros(n, np.int64), 0.0)


_IN_OFF = {}
_off = 0
for _name, _n in (("da_q", 256), ("da_k", 256), ("da_v", 256), ("da_g", 256), ("s5_u", 256), ("s5_g", 256),
                  ("hg_q", 256), ("hg_ff", 256), ("hg_fb", 256), ("hg_i", 256), ("hg_g", 256),
                  ("mla_cq", MLA_Q_RANK), ("mla_ckv", MLA_KV_RANK), ("mla_kr", MLA_ROPE), ("mla_g", 256)):
    _IN_OFF[_name] = np.arange(_off, _off + _n)
    _off += _n


def _perm_matrix(plan, first, k):
    idx = np.concatenate([p[0] for p in plan]) - first
    sign = np.concatenate([np.broadcast_to(p[1], p[0].shape) for p in plan])
    m = np.zeros((k, len(idx)), np.float32)
    m[idx, np.arange(len(idx))] = sign
    return jnp.asarray(m, BF16)


def _in_proj_weights(w_in):
    c = _IN_OFF

    def per_head(x, y):
        return (np.concatenate([c[x].reshape(HG_HEADS, HG_DK), c[y].reshape(HG_HEADS, HG_DK)], axis=1).reshape(-1), 1.0)
    pdup = _perm_matrix([per_head("hg_q", "hg_q")], c["hg_q"][0], 256)
    pint = _perm_matrix([per_head("hg_ff", "hg_fb")], c["hg_ff"][0], 512)
    col = lambda name: w_in[..., c[name][0]:c[name][-1] + 1]
    zero = lambda n: jnp.zeros(w_in.shape[:-1] + (n,), w_in.dtype)
    w_d = jnp.concatenate([col("mla_cq"), zero(256 - MLA_Q_RANK), col("mla_ckv"), zero(MLA_NOPE), col("mla_kr"),
                           zero(128 - MLA_NOPE - MLA_ROPE), col("mla_g")], axis=-1)
    return _arrange_w_in(w_in, pdup, pint), w_d.astype(BF16)


def _rope_tables():
    t = np.arange(LAT_LEN)
    pos = np.stack([t // GRID_W, t % GRID_W], axis=1).astype(np.float32)
    inv_freq = (np.float32(ROPE_BASE) ** (-np.arange(8, dtype=np.float32) / np.float32(8))).astype(np.float32)
    r = np.arange(MLA_ROPE)
    ang = (pos[:, r // 16] * inv_freq[r % 8][None, :]).astype(np.float64)
    cos32, sin32 = np.cos(ang).astype(np.float32), np.sin(ang).astype(np.float32)
    lo = (np.arange(MLA_ROPE) % 16 < 8)[None, :]
    sin_lo32, sin_hi32 = np.where(lo, -sin32, 0.0).astype(np.float32), np.where(lo, 0.0, sin32).astype(np.float32)
    da_tabs = tuple(np.tile(x, (1, 8)) for x in (cos32, sin_lo32, sin_hi32))

    def head(x, fill):
        h = np.concatenate([np.full((LAT_LEN, MLA_NOPE), fill, np.float32), x,
                            np.full((LAT_LEN, MLA_HEAD_PAD - MLA_NOPE - MLA_ROPE), fill, np.float32)], axis=1)
        return np.concatenate([h, np.full((BIG_TILE, MLA_HEAD_PAD), fill, np.float32)], axis=0)
    k_tabs = (head(cos32, 1.0), head(sin_lo32, 0.0), head(sin_hi32, 0.0))
    return tuple(jnp.asarray(x) for x in da_tabs), tuple(jnp.asarray(x) for x in k_tabs)


def _mla_weights(w_uq, w_ukv, q_norm):
    hd = MLA_NOPE + MLA_ROPE
    pad_tail = _zeros(MLA_HEAD_PAD - hd)
    q_plan, k_plan, v_plan = [], [], []
    for h in range(MLA_HEADS):
        nope, rope = np.arange(h * hd, h * hd + MLA_NOPE), np.arange(h * hd + MLA_NOPE, (h + 1) * hd)
        q_plan += [(nope, 1.0), (rope, 1.0), pad_tail]
        k_plan += [(np.arange(h * 2 * MLA_NOPE, h * 2 * MLA_NOPE + MLA_NOPE), 1.0), _zeros(MLA_HEAD_PAD - MLA_NOPE)]
        v_plan += [(np.arange(h * 2 * MLA_NOPE + MLA_NOPE, (h + 1) * 2 * MLA_NOPE), 1.0)]
    pad_rows = lambda x: jnp.pad(x, ((0, 256 - MLA_Q_RANK), (0, 0))).astype(BF16)
    qn = jnp.pad(q_norm, (0, 256 - MLA_Q_RANK)).reshape(1, 256)
    return (pad_rows(_take_cols(w_uq, q_plan)), _take_cols(w_ukv, k_plan).astype(BF16),
            _take_cols(w_ukv, v_plan).astype(BF16), qn)


def _split_bf16(a):
    hi = a.astype(BF16)
    return hi, (a - hi.astype(F32)).astype(BF16)


def _dot_sel(a, sel):
    hi, lo = _split_bf16(a)
    sel = sel.astype(BF16)
    return _dot(hi, sel) + _dot(lo, sel)


def _dot_x3(a, b):
    a_hi, a_lo = _split_bf16(a)
    b_hi, b_lo = _split_bf16(b)
    return _dot(a_hi, b_hi) + _dot(a_hi, b_lo) + _dot(a_lo, b_hi)


def _s5_table_body(xy_ref, bb_ref, c_ref, ct_ref, mt_ref, bst_ref, cot_ref, a_ref):
    n, t, ch = S5_STATE, S5_CHUNK, S5_CH
    wide = 2 * S5_TAP
    xy = xy_ref[0, 0]
    tau_i = lax.broadcasted_iota(jnp.int32, (1, 128), 1)
    tau = tau_i.astype(F32)
    sel_row = lax.broadcasted_iota(jnp.int32, (128, 1), 0)

    def lag(width):
        return lax.broadcasted_iota(jnp.int32, (1, width), 1) >> 4

    def onehot(cond):
        return jnp.where(cond, 1.0, 0.0).astype(F32)
    j = lag(wide)
    e_z = (onehot((j <= t - 1) & (sel_row == t - 1 - j)), onehot((j >= t - 1) & (j <= 2 * t - 2) & (sel_row == j - (t - 1))))
    jc = lag(S5_TAP)
    e_c = (onehot(sel_row == jc + 1), onehot(sel_row == t - jc))
    ch_row = lax.broadcasted_iota(jnp.int32, (ch, 1), 0)
    tile_w = onehot((lax.broadcasted_iota(jnp.int32, (1, wide), 1) & (ch - 1)) == ch_row)
    tile_n = onehot((lax.broadcasted_iota(jnp.int32, (1, S5_TAP), 1) & (ch - 1)) == ch_row)

    z, cot_rows, klong = [], [], None
    for d in range(2):
        x, y = xy[:, 2 * d:2 * d + 1], xy[:, 2 * d + 1:2 * d + 2]
        mag = jnp.exp(jnp.where(tau_i <= t, tau, 0.0) * x)
        ang = jnp.where(tau_i <= t, tau, 0.0) * y
        p_re = jnp.where(tau_i <= t, mag * jnp.cos(ang), 0.0)
        p_im = jnp.where(tau_i <= t, mag * jnp.sin(ang), 0.0)
        a_ref[0, 0, 2 * d] = jnp.broadcast_to(p_re[:, t:t + 1], (n, 128))
        a_ref[0, 0, 2 * d + 1] = jnp.broadcast_to(p_im[:, t:t + 1], (n, 128))
        pz_re, pz_im = _dot_sel(p_re, e_z[d]), _dot_sel(p_im, e_z[d])
        b_re, b_im = _dot_sel(bb_ref[0, 0, 2 * d], tile_w), _dot_sel(bb_ref[0, 0, 2 * d + 1], tile_w)
        z_re, z_im = pz_re * b_re - pz_im * b_im, pz_re * b_im + pz_im * b_re
        z += [z_re, z_im]
        part = _dot_x3(c_ref[0, 0, 2 * d], z_re) - _dot_x3(c_ref[0, 0, 2 * d + 1], z_im)
        klong = part if klong is None else klong + part
        pc_re, pc_im = _dot_sel(p_re, e_c[d]), _dot_sel(p_im, e_c[d])
        c_re, c_im = _dot_sel(ct_ref[0, 0, 2 * d], tile_n), _dot_sel(ct_ref[0, 0, 2 * d + 1], tile_n)
        cot_rows += [c_re * pc_re - c_im * pc_im, -(c_re * pc_im + c_im * pc_re)]
    for tt in range(t):
        off = (t - 1 - tt) * ch
        win = klong if off == 0 else pltpu.roll(klong, wide - off, 1)
        mt_ref[0, 0, tt * ch:(tt + 1) * ch, :] = win[:, :S5_TAP].astype(BF16)
    back = pltpu.roll(z[2], wide - (t - 1) * ch, 1), pltpu.roll(z[3], wide - (t - 1) * ch, 1)
    for k, rows in enumerate((z[0], z[1], back[0], back[1])):
        bst_ref[0, 0, k * n:(k + 1) * n, :] = rows[:, :S5_TAP].astype(BF16)
    cot_ref[0, 0] = jnp.concatenate(cot_rows, axis=0).T.astype(BF16)


def _s5_tables(a_re, a_im, log_dt, b_re, b_im, c_re, c_im):
    nl, g, n, ch = a_re.shape[0], S5_GROUPS, S5_STATE, S5_CH
    step = jnp.exp(log_dt)[..., None]
    mag = jnp.exp(a_re * step)
    ab_re, ab_im = mag * jnp.cos(a_im * step), mag * jnp.sin(a_im * step)
    den = a_re * a_re + a_im * a_im
    f_re = ((ab_re - 1.0) * a_re + ab_im * a_im) / den
    f_im = (ab_im * a_re - (ab_re - 1.0) * a_im) / den
    bb_re = f_re[..., None] * b_re - f_im[..., None] * b_im
    bb_im = f_re[..., None] * b_im + f_im[..., None] * b_re
    by_group = lambda x: jnp.moveaxis(x, 1, 2)
    pair = lambda re, im: jnp.stack([by_group(re), by_group(im)], axis=3).reshape((nl, g, 4) + re.shape[3:])
    xy = jnp.stack([by_group(a_re * step), by_group(a_im * step)], axis=3).reshape(nl, g, 4, n)
    xy = jnp.pad(jnp.swapaxes(xy, 2, 3), ((0, 0), (0, 0), (0, 0), (0, 4)))
    mat = pl.BlockSpec((1, 1, S5_TAP, S5_TAP), lambda l, i: (l, i, 0, 0))
    return pl.pallas_call(
        _s5_table_body,
        grid=(nl, g),
        in_specs=[pl.BlockSpec((1, 1, n, 8), lambda l, i: (l, i, 0, 0)),
                  pl.BlockSpec((1, 1, 4, n, ch), lambda l, i: (l, i, 0, 0, 0)),
                  pl.BlockSpec((1, 1, 4, ch, n), lambda l, i: (l, i, 0, 0, 0)),
                  pl.BlockSpec((1, 1, 4, n, ch), lambda l, i: (l, i, 0, 0, 0))],
        out_specs=[mat, mat, mat, pl.BlockSpec((1, 1, 4, n, 128), lambda l, i: (l, i, 0, 0, 0))],
        out_shape=[jax.ShapeDtypeStruct((nl, g, S5_TAP, S5_TAP), BF16)] * 3
        + [jax.ShapeDtypeStruct((nl, g, 4, n, 128), F32)],
        compiler_params=_cparams("parallel", "parallel"),
        name="s5_tables",
    )(xy, pair(bb_re, bb_im), pair(c_re, c_im), pair(jnp.swapaxes(c_re, -1, -2), jnp.swapaxes(c_im, -1, -2)))


def _s5_chunk_lanes(u):
    return u.reshape(S5_NCHUNK, S5_CHUNK, BRANCH).transpose(1, 2, 0)


def _s5_token_rows(y):
    return y.transpose(2, 0, 1).reshape(N_TOK, BRANCH)


def kernel(x_prompt, x_sample, cache_diff_k, cache_diff_v, state_s5, state_hgrn, cache_mla_ckv, cache_mla_krope, c, c_ctx, w_mod, b_mod, w_in, w_out, da_lambda, da_norm, s5_a_re, s5_a_im, s5_log_dt, s5_b_re, s5_b_im, s5_c_re, s5_c_im, s5_d, s5_w_glu, hg_lb, hg_norm, mla_q_norm, mla_w_uq, mla_kv_norm, mla_w_ukv, final_norm):
    lb_w = jax.nn.softmax(hg_lb.astype(F32), axis=0)
    lb_all = jnp.cumsum(lb_w, axis=0) - lb_w[0:1]
    c_rows = jnp.concatenate([c_ctx[None], c, jnp.zeros((8 - 1 - N_LAT_SEQ, D_MODEL), F32)], axis=0)
    mods = _modulation(c_rows, w_mod, b_mod)
    da_tabs, mla_tabs = _rope_tables()
    xs = (x_prompt.reshape(N_CTX, D_MODEL), x_sample.reshape(N_LAT, D_MODEL))
    new_k, new_v, new_s5, new_hg, new_ckv, new_kr = [], [], [], [], [], []
    s5_tabs = _s5_tables(s5_a_re, s5_a_im, s5_log_dt, s5_b_re, s5_b_im, s5_c_re, s5_c_im)
    w_abc, w_d = _in_proj_weights(w_in)
    for l in range(DEPTH):
        mod = mods[l, :3].reshape(3, 3, D_MODEL)
        z_a, z_b, z_c, z_d, k_new, v_new, u_bf = _in_proj(xs, mod, w_abc, w_d, l)

        lam_init = 0.8 - 0.6 * math.exp(-0.3 * l)
        kv_lat = _da_latent_kv(z_a, da_tabs,
                               cache_diff_k[:, l].reshape(N_LAT_SEQ, PAST_LEN, BRANCH),
                               cache_diff_v[:, l].reshape(N_LAT_SEQ, PAST_LEN, BRANCH))
        a_out = _da_attention(z_a, da_lambda[l], da_norm[l], lam_init, da_tabs, kv_lat)
        new_k.append(k_new.reshape(N_CTX_SEQ, CTX_LEN, DA_HEADS, 2 * DA_QK))
        new_v.append(v_new.reshape(N_CTX_SEQ, CTX_LEN, DA_HEADS, 2 * DA_QK))

        h0 = state_s5[:, l].transpose(2, 1, 4, 3, 0).reshape(S5_GROUPS, 4, S5_STATE, N_LAT_SEQ)
        h0 = jnp.pad(h0, ((0, 0), (0, 0), (0, 0), (0, 128 - N_LAT_SEQ)))
        y_all, fin = _s5_scan(_s5_chunk_lanes(u_bf), *s5_tabs, h0, l)
        b_out = _s5_out(_s5_token_rows(y_all), z_b, s5_d[l].reshape(1, BRANCH), s5_w_glu[l].astype(BF16))
        fin = jnp.stack([fin[:, 0:2, :, S5_CTX_SEQ_CH - 1::S5_CTX_SEQ_CH], fin[:, 2:4, :, 0::S5_CTX_SEQ_CH]], axis=1)
        new_s5.append(fin.transpose(4, 1, 0, 3, 2))

        lb = jnp.concatenate([lb_all[l, 0].reshape(HG_HEADS, HG_DK), lb_all[l, 1].reshape(HG_HEADS, HG_DK)],
                             axis=-1).reshape(1, HG_W)
        head_eye = jnp.eye(HG_HEADS, dtype=F32)
        s0 = state_hgrn[:, l].transpose(0, 2, 1, 3, 4).reshape(N_LAT_SEQ, HG_HEADS, HG_HEAD_W, 1, HG_DK)
        s0 = (s0 * head_eye[None, :, None, :, None]).reshape(N_LAT_SEQ, HG_HEADS, HG_HEAD_W, BRANCH)
        s_f, s_b = _hg_states(z_c, lb, s0)
        c_out, s_fin = _hg_main(z_c, s_f, s_b, lb, jnp.tile(hg_norm[l].reshape(1, HG_DK), (1, HG_HEADS)))
        new_hg.append(s_fin.reshape(N_CTX_SEQ, HG_HEADS, 2, HG_DK, HG_DK).transpose(0, 2, 1, 3, 4))

        wq, wk, wv, qn = _mla_weights(mla_w_uq[l], mla_w_ukv[l], mla_q_norm[l])
        q, ckv_n, kr = _mla_prep(z_d, mla_tabs, qn, mla_kv_norm[l].reshape(1, MLA_KV_RANK), wq)
        kr_cache = jnp.pad(cache_mla_krope[:, l], ((0, 0), (0, 0), (MLA_NOPE, 128 - MLA_NOPE - MLA_ROPE)))
        k_ctx, v_ctx, k_lat, v_lat = _mla_kv(ckv_n, kr, cache_mla_ckv[:, l], kr_cache, wk, wv)
        d_out = _mla_attention(z_d, q, k_ctx, v_ctx, k_lat, v_lat)
        new_ckv.append(ckv_n[:N_CTX].reshape(N_CTX_SEQ, CTX_LEN, MLA_KV_RANK))
        new_kr.append(kr[:N_CTX, MLA_NOPE:MLA_NOPE + MLA_ROPE].reshape(N_CTX_SEQ, CTX_LEN, MLA_ROPE))

        xs = _out_proj(a_out, b_out, c_out, d_out, xs, mod, w_out, l,
                       final_norm.reshape(1, D_MODEL), final=(l == DEPTH - 1))
        xs = tuple(xs) if l == DEPTH - 1 else (xs,)
    y_prompt = xs[0].reshape(N_CTX_SEQ, CTX_LEN, D_MODEL)
    y_sample = xs[1].reshape(N_LAT_SEQ, LAT_LEN, D_MODEL)
    st = lambda parts: jnp.stack(parts, axis=1)
    return (y_prompt, y_sample, st(new_k), st(new_v), st(new_s5), st(new_hg), st(new_ckv), st(new_kr))
```

```python
import functools
import math

import numpy as np

import jax
import jax.numpy as jnp
from jax import lax
from jax.experimental import pallas as pl
from jax.experimental.pallas import tpu as pltpu

F32 = jnp.float32
BF16 = jnp.bfloat16

D_MODEL = 1024
DEPTH = 2
N_CTX_SEQ = 16
CTX_LEN = 256
N_LAT_SEQ = 2
LAT_LEN = 2048
PAST_LEN = 256
GRID_W = 64
N_CTX = N_CTX_SEQ * CTX_LEN
N_LAT = N_LAT_SEQ * LAT_LEN
N_TOK = N_CTX + N_LAT
BRANCH = 256
EPS = 1e-6
ROPE_BASE = 10000.0
ROW_TILE = 256
LAT_TILES = LAT_LEN // ROW_TILE
N_TILES = N_TOK // ROW_TILE
CTX_TILES = N_CTX // ROW_TILE
VMEM_LIMIT = 48 * 1024 * 1024
IN_PROJ_VMEM_LIMIT = 56 * 1024 * 1024
LAT_Q_TILE = 512
LAT_Q_TILES = LAT_LEN // LAT_Q_TILE
BIG_TILE = 512
CTX_SEQ_PER_STEP = 2

DA_HEADS = 4
DA_QK = 32
MLA_HEADS = 4
MLA_NOPE = 64
MLA_ROPE = 32
MLA_Q_RANK = 192
MLA_KV_RANK = 128
S5_GROUPS = 16
S5_CH = 16
S5_STATE = 64
S5_CHUNK = 16
HG_HEADS = 4
HG_DK = 64

W_A = 1024
W_B = 512
W_C = 1536
W_D = 768
W_ALL = W_A + W_B + W_C + W_D


def _cparams(*sem):
    return pltpu.CompilerParams(dimension_semantics=sem, vmem_limit_bytes=VMEM_LIMIT)


def _tile_seq(i, tile=ROW_TILE):
    return jnp.where(i < N_CTX // tile, 0, 1 + (i - N_CTX // tile) // (LAT_LEN // tile))


def _silu(x):
    return x * (1.0 / (1.0 + jnp.exp(-x)))


def _dot(a, b):
    return jnp.dot(a, b, preferred_element_type=F32)


def _dot_nt(a, b):
    return lax.dot_general(a, b, (((1,), (1,)), ((), ())), preferred_element_type=F32)


def _mod_body(c_ref, w_ref, b_ref, o_ref):
    c = _silu(c_ref[...]).astype(BF16)
    o_ref[0] = _dot(c, w_ref[0].astype(BF16)) + b_ref[0]


def _modulation(c_rows, w_mod, b_mod):
    tn = 768
    return pl.pallas_call(
        _mod_body,
        grid=(DEPTH, 3 * D_MODEL // tn),
        in_specs=[pl.BlockSpec((8, D_MODEL), lambda l, j: (0, 0)),
                  pl.BlockSpec((1, D_MODEL, tn), lambda l, j: (l, 0, j)),
                  pl.BlockSpec((1, 1, tn), lambda l, j: (l, 0, j))],
        out_specs=pl.BlockSpec((1, 8, tn), lambda l, j: (l, 0, j)),
        out_shape=jax.ShapeDtypeStruct((DEPTH, 8, 3 * D_MODEL), F32),
        compiler_params=_cparams("parallel", "parallel"),
        name="modulation",
    )(c_rows, w_mod, b_mod.reshape(DEPTH, 1, 3 * D_MODEL))


def _split_rows(i, ctx_ref, lat_ref):
    return jnp.where(i < N_CTX // ctx_ref.shape[0], ctx_ref[...], lat_ref[...])


def _ctx_tile_spec(w, tile=ROW_TILE):
    return pl.BlockSpec((tile, w), lambda i: (jnp.minimum(i, N_CTX // tile - 1), 0))


def _lat_tile_spec(w, tile=ROW_TILE):
    return pl.BlockSpec((tile, w), lambda i: (jnp.maximum(i - N_CTX // tile, 0), 0))


def _in_proj_body(*refs, split):
    if split:
        xc_ref, xl_ref, mod_ref, w_ref, oa, ob, oc, od, ok, ov, ou = refs
        x = _split_rows(pl.program_id(0), xc_ref, xl_ref)
    else:
        x_ref, mod_ref, w_ref, oa, ob, oc, od, ok, ov, ou = refs
        x = x_ref[...]
    xn = x * lax.rsqrt(jnp.mean(x * x, axis=-1, keepdims=True) + EPS)
    mod = mod_ref[0]
    h = (xn * (1.0 + mod[1:2]) + mod[0:1]).astype(BF16)
    off = 0
    for o in (oa, ob, oc, od):
        w = o.shape[-1]
        o[...] = _dot(h, w_ref[0, :, off:off + w])
        off += w
    ou[...] = ob[:, :BRANCH].astype(BF16)

    @pl.when(pl.program_id(0) < N_CTX // BIG_TILE)
    def _():
        ok[...] = oa[:, BRANCH:2 * BRANCH]
        ov[...] = oa[:, 2 * BRANCH:3 * BRANCH]


def _in_proj(xs, mod, w_all, l):
    widths = (W_A, W_B, W_C, W_D)
    split = len(xs) == 2
    x_specs = ([_ctx_tile_spec(D_MODEL, BIG_TILE), _lat_tile_spec(D_MODEL, BIG_TILE)] if split
               else [pl.BlockSpec((BIG_TILE, D_MODEL), lambda i: (i, 0))])
    return pl.pallas_call(
        functools.partial(_in_proj_body, split=split),
        grid=(N_TOK // BIG_TILE,),
        in_specs=x_specs + [pl.BlockSpec((1, 3, D_MODEL), lambda i: (_tile_seq(i, BIG_TILE), 0, 0)),
                            pl.BlockSpec((1, D_MODEL, W_ALL), lambda i: (l, 0, 0))],
        out_specs=[pl.BlockSpec((BIG_TILE, w), lambda i: (i, 0)) for w in widths]
        + [_ctx_tile_spec(BRANCH, BIG_TILE)] * 2 + [pl.BlockSpec((BIG_TILE, BRANCH), lambda i: (i, 0))],
        out_shape=[jax.ShapeDtypeStruct((N_TOK, w), F32) for w in widths]
        + [jax.ShapeDtypeStruct((N_CTX, BRANCH), F32)] * 2 + [jax.ShapeDtypeStruct((N_TOK, BRANCH), BF16)],
        compiler_params=pltpu.CompilerParams(dimension_semantics=("arbitrary",), vmem_limit_bytes=IN_PROJ_VMEM_LIMIT),
        name="in_proj",
    )(*xs, mod, w_all)


def _arrange_body(wt_ref, o_ref):
    w = wt_ref[0]
    c = _IN_OFF
    rows = lambda name: w[c[name][0]:c[name][-1] + 1]
    zero = lambda n: jnp.zeros((n, w.shape[1]), F32)

    def per_head(x, y):
        xs, ys = rows(x), rows(y)
        return [p for h in range(HG_HEADS) for p in (xs[h * HG_DK:(h + 1) * HG_DK], ys[h * HG_DK:(h + 1) * HG_DK])]
    pieces = ([w[0:W_A + W_B]] + per_head("hg_q", "hg_q") + per_head("hg_ff", "hg_fb") + [rows("hg_i"), rows("hg_g")]
              + [rows("mla_cq"), zero(256 - MLA_Q_RANK), rows("mla_ckv"), zero(MLA_NOPE), rows("mla_kr"),
                 zero(128 - MLA_NOPE - MLA_ROPE), rows("mla_g")])
    o_ref[0] = jnp.concatenate(pieces, axis=0).T.astype(BF16)


def _arrange_w_in(w_in):
    lanes = 256
    wt = jnp.swapaxes(w_in, 1, 2)
    return pl.pallas_call(
        _arrange_body,
        grid=(DEPTH, D_MODEL // lanes),
        in_specs=[pl.BlockSpec((1, wt.shape[1], lanes), lambda l, i: (l, 0, i))],
        out_specs=pl.BlockSpec((1, lanes, W_ALL), lambda l, i: (l, i, 0)),
        out_shape=jax.ShapeDtypeStruct((DEPTH, D_MODEL, W_ALL), BF16),
        compiler_params=_cparams("parallel", "parallel"),
        name="arrange_w_in",
    )(wt)


def _out_proj_body(*refs, split_in, final):
    ac_ref, al_ref, b_ref, c_ref, dc_ref, dl_ref = refs[:6]
    i = pl.program_id(0)
    if split_in:
        xc_ref, xl_ref, mod_ref, w_ref, fn_ref = refs[6:11]
        x = _split_rows(i, xc_ref, xl_ref)
    else:
        x_ref, mod_ref, w_ref, fn_ref = refs[6:10]
        x = x_ref[...]
    branches = (_split_rows(i, ac_ref, al_ref), b_ref[...], c_ref[...], _split_rows(i, dc_ref, dl_ref))
    acc = None
    for j, r in enumerate(branches):
        t = _dot(r.astype(BF16), w_ref[0, j * BRANCH:(j + 1) * BRANCH, :].astype(BF16))
        acc = t if acc is None else acc + t
    x = x + mod_ref[0][2:3] * acc
    if not final:
        refs[-1][...] = x
        return
    y = x * lax.rsqrt(jnp.mean(x * x, axis=-1, keepdims=True) + EPS) * fn_ref[...]
    yc_ref, yl_ref = refs[-2:]

    @pl.when(i < N_CTX // BIG_TILE)
    def _():
        yc_ref[...] = y

    @pl.when(i >= N_CTX // BIG_TILE)
    def _():
        yl_ref[...] = y


def _out_proj(a, b, c, d, xs, mod, w_out, l, final_norm, final):
    br = pl.BlockSpec((BIG_TILE, BRANCH), lambda i: (i, 0))
    pair = [_ctx_tile_spec(BRANCH, BIG_TILE), _lat_tile_spec(BRANCH, BIG_TILE)]
    split_in = len(xs) == 2
    x_specs = ([_ctx_tile_spec(D_MODEL, BIG_TILE), _lat_tile_spec(D_MODEL, BIG_TILE)] if split_in
               else [pl.BlockSpec((BIG_TILE, D_MODEL), lambda i: (i, 0))])
    if final:
        out_specs = [_ctx_tile_spec(D_MODEL, BIG_TILE), _lat_tile_spec(D_MODEL, BIG_TILE)]
        out_shape = [jax.ShapeDtypeStruct((N_CTX, D_MODEL), F32), jax.ShapeDtypeStruct((N_LAT, D_MODEL), F32)]
    else:
        out_specs = pl.BlockSpec((BIG_TILE, D_MODEL), lambda i: (i, 0))
        out_shape = jax.ShapeDtypeStruct((N_TOK, D_MODEL), F32)
    return pl.pallas_call(
        functools.partial(_out_proj_body, split_in=split_in, final=final),
        grid=(N_TOK // BIG_TILE,),
        in_specs=pair + [br, br] + pair + x_specs + [
            pl.BlockSpec((1, 3, D_MODEL), lambda i: (_tile_seq(i, BIG_TILE), 0, 0)),
            pl.BlockSpec((1, D_MODEL, D_MODEL), lambda i: (l, 0, 0)),
            pl.BlockSpec((1, D_MODEL), lambda i: (0, 0))],
        out_specs=out_specs,
        out_shape=out_shape,
        compiler_params=_cparams("arbitrary"),
        name="out_proj",
    )(*a, b, c, *d, *xs, mod, w_out, final_norm)


LOG2E = 1.4426950408889634


def _exp2_rows(s):
    e = jnp.exp2(s - jnp.max(s, axis=-1, keepdims=True))
    return e, jnp.sum(e, axis=-1, keepdims=True)


def _rope(x, cos, sin_lo, sin_hi):
    w = x.shape[-1]
    return x * cos + pltpu.roll(x, w - 8, 1) * sin_lo + pltpu.roll(x, 8, 1) * sin_hi


def _da_kv_body(k_ref, v_ref, cos_ref, slo_ref, shi_ref, ck_ref, cv_ref, ko_ref, vo_ref):
    j = pl.program_id(1)

    @pl.when(j < LAT_TILES)
    def _():
        ko_ref[0] = _rope(k_ref[...], cos_ref[...], slo_ref[...], shi_ref[...]).astype(BF16)
        vo_ref[0] = v_ref[...].astype(BF16)

    @pl.when(j == LAT_TILES)
    def _():
        ko_ref[0] = ck_ref[0].astype(BF16)
        vo_ref[0] = cv_ref[0].astype(BF16)


def _da_latent_kv(z_a, tabs, cache_k, cache_v):
    def rows(col):
        return pl.BlockSpec(
            (ROW_TILE, BRANCH),
            lambda b, j: (CTX_TILES + b * LAT_TILES + jnp.minimum(j, LAT_TILES - 1), col))
    tab = pl.BlockSpec((ROW_TILE, BRANCH), lambda b, j: (jnp.minimum(j, LAT_TILES - 1), 0))
    cache = pl.BlockSpec((1, PAST_LEN, BRANCH), lambda b, j: (b, 0, 0))
    out = pl.BlockSpec((1, ROW_TILE, BRANCH), lambda b, j: (b, j, 0))
    shp = jax.ShapeDtypeStruct((N_LAT_SEQ, LAT_LEN + PAST_LEN, BRANCH), BF16)
    return pl.pallas_call(
        _da_kv_body,
        grid=(N_LAT_SEQ, LAT_TILES + 1),
        in_specs=[rows(1), rows(2), tab, tab, tab, cache, cache],
        out_specs=[out, out],
        out_shape=[shp, shp],
        compiler_params=_cparams("parallel", "parallel"),
        name="da_kv",
    )(z_a, z_a, *tabs, cache_k, cache_v)


def _da_attn_body(lam_ref, ng_ref, q_ref, *rest, rope, lam_init):
    if rope:
        cos_ref, slo_ref, shi_ref, k_ref, v_ref, g_ref, o_ref = rest
        q = _rope(q_ref[...], cos_ref[...], slo_ref[...], shi_ref[...])
        o_ref[...] = _da_attn_tile(lam_ref, ng_ref, q, k_ref[0], v_ref[0], g_ref[...], lam_init)
    else:
        k_ref, v_ref, g_ref, o_ref = rest
        for t in range(q_ref.shape[0] // CTX_LEN):
            r = slice(t * CTX_LEN, (t + 1) * CTX_LEN)
            o_ref[r, :] = _da_attn_tile(lam_ref, ng_ref, q_ref[r, :], k_ref[r, :].astype(BF16),
                                        v_ref[r, :].astype(BF16), g_ref[r, :], lam_init)


def _da_attn_tile(lam_ref, ng_ref, q, k, v, g, lam_init):
    q = q * (DA_QK ** -0.5 * LOG2E)
    lv = lam_ref[...]
    lam = (jnp.exp(jnp.sum(lv[0:1] * lv[1:2], axis=-1, keepdims=True))
           - jnp.exp(jnp.sum(lv[2:3] * lv[3:4], axis=-1, keepdims=True)) + lam_init)
    lane = lax.broadcasted_iota(jnp.int32, (1, BRANCH), 1)
    acc = jnp.zeros(q.shape, F32)
    for h in range(DA_HEADS):
        q1 = jnp.where(lane // DA_QK == 2 * h, q, 0.0).astype(BF16)
        q2 = jnp.where(lane // DA_QK == 2 * h + 1, q, 0.0).astype(BF16)
        e1, l1 = _exp2_rows(_dot_nt(q1, k))
        e2, l2 = _exp2_rows(_dot_nt(q2, k))
        a = (e1 - (lam * l1 / l2) * e2).astype(BF16)
        acc = jnp.where(lane // (2 * DA_QK) == h, _dot(a, v) * (1.0 / l1), acc)
    sq = acc * acc
    ms = jnp.zeros(q.shape, F32)
    for h in range(DA_HEADS):
        hm = lane // (2 * DA_QK) == h
        ms = jnp.where(hm, jnp.sum(jnp.where(hm, sq, 0.0), axis=-1, keepdims=True), ms)
    o = acc * lax.rsqrt(ms * (1.0 / (2 * DA_QK)) + EPS) * (ng_ref[...] * (1.0 - lam_init))
    return o * _silu(g)


def _da_attention(z_a, lam_vec, norm_g, lam_init, tabs, kv_lat):
    ng = jnp.tile(norm_g.reshape(1, 2 * DA_QK), (1, DA_HEADS))
    small = [pl.BlockSpec((4, DA_QK), lambda *_: (0, 0)), pl.BlockSpec((1, BRANCH), lambda *_: (0, 0))]

    rows = CTX_SEQ_PER_STEP * CTX_LEN

    def col(c):
        return pl.BlockSpec((rows, BRANCH), lambda i: (i, c))
    ctx = pl.pallas_call(
        functools.partial(_da_attn_body, rope=False, lam_init=lam_init),
        grid=(N_CTX // rows,),
        in_specs=small + [col(0), col(1), col(2), col(3)],
        out_specs=pl.BlockSpec((rows, BRANCH), lambda i: (i, 0)),
        out_shape=jax.ShapeDtypeStruct((N_CTX, BRANCH), F32),
        compiler_params=_cparams("parallel"),
        name="da_attn_ctx",
    )(lam_vec, ng, z_a, z_a, z_a, z_a)

    def lcol(c):
        return pl.BlockSpec((LAT_Q_TILE, BRANCH), lambda b, j: (N_CTX // LAT_Q_TILE + b * LAT_Q_TILES + j, c))
    tab = pl.BlockSpec((LAT_Q_TILE, BRANCH), lambda b, j: (j, 0))
    kvs = pl.BlockSpec((1, LAT_LEN + PAST_LEN, BRANCH), lambda b, j: (b, 0, 0))
    lat = pl.pallas_call(
        functools.partial(_da_attn_body, rope=True, lam_init=lam_init),
        grid=(N_LAT_SEQ, LAT_Q_TILES),
        in_specs=small + [lcol(0), tab, tab, tab, kvs, kvs, lcol(3)],
        out_specs=pl.BlockSpec((LAT_Q_TILE, BRANCH), lambda b, j: (b * LAT_Q_TILES + j, 0)),
        out_shape=jax.ShapeDtypeStruct((N_LAT, BRANCH), F32),
        compiler_params=_cparams("parallel", "parallel"),
        name="da_attn_lat",
    )(lam_vec, ng, z_a, *tabs, kv_lat[0], kv_lat[1], z_a)
    return ctx, lat


MLA_HEAD_PAD = 128
MLA_QW = MLA_HEADS * MLA_HEAD_PAD


def _mla_prep_body(cq_ref, ckv_ref, kr_ref, ck_t, sk_lo, sk_hi, qn_ref, kvn_ref, wq_ref, q_out, ckv_out, kr_out):
    cq = cq_ref[...]
    ms = jnp.sum(cq * cq, axis=-1, keepdims=True) * (1.0 / MLA_Q_RANK)
    qn = (cq * lax.rsqrt(ms + EPS) * qn_ref[...]).astype(BF16)
    heads = lambda t: jnp.concatenate([t[...]] * MLA_HEADS, axis=-1)
    q = _rope(_dot(qn, wq_ref[...]), heads(ck_t), heads(sk_lo), heads(sk_hi))
    q_out[...] = (q * ((MLA_NOPE + MLA_ROPE) ** -0.5 * LOG2E)).astype(BF16)
    ckv = ckv_ref[...]
    ckv_out[...] = ckv * lax.rsqrt(jnp.mean(ckv * ckv, axis=-1, keepdims=True) + EPS) * kvn_ref[...]
    kr_out[...] = _rope(kr_ref[...], ck_t[...], sk_lo[...], sk_hi[...])


def _mla_prep(z_d, tabs, q_norm_pad, kv_norm, wq):
    ctx_tiles, lat_tiles = N_CTX // BIG_TILE, LAT_LEN // BIG_TILE

    def tab(w):
        return pl.BlockSpec((BIG_TILE, w), lambda i: (jnp.where(i < ctx_tiles, lat_tiles, (i - ctx_tiles) % lat_tiles), 0))

    def col(w, c):
        return pl.BlockSpec((BIG_TILE, w), lambda i: (i, c))

    def const(shape):
        return pl.BlockSpec(shape, lambda i: (0, 0))
    return pl.pallas_call(
        _mla_prep_body,
        grid=(N_TOK // BIG_TILE,),
        in_specs=[col(256, 0), col(128, 2), col(128, 3),
                  tab(128), tab(128), tab(128),
                  const((1, 256)), const((1, 128)), const((256, MLA_QW))],
        out_specs=[col(MLA_QW, 0), col(128, 0), col(128, 0)],
        out_shape=[jax.ShapeDtypeStruct((N_TOK, MLA_QW), BF16),
                   jax.ShapeDtypeStruct((N_TOK, 128), F32),
                   jax.ShapeDtypeStruct((N_TOK, 128), F32)],
        compiler_params=_cparams("parallel"),
        name="mla_prep",
    )(z_d, z_d, z_d, *tabs, q_norm_pad, kv_norm, wq)


def _mla_kv_math(ckv, kr, wk_ref, wv_ref, k_out, v_out):
    c = ckv.astype(BF16)
    k_out[...] = (_dot(c, wk_ref[...]) + jnp.concatenate([kr] * MLA_HEADS, axis=-1)).astype(BF16).reshape(k_out.shape)
    v_out[...] = _dot(c, wv_ref[...]).astype(BF16).reshape(v_out.shape)


def _mla_kv_ctx_body(ckv_ref, kr_ref, wk_ref, wv_ref, k_out, v_out):
    _mla_kv_math(ckv_ref[...], kr_ref[...], wk_ref, wv_ref, k_out, v_out)


def _mla_kv_lat_body(ckv_ref, kr_ref, cckv_ref, ckr_ref, wk_ref, wv_ref, k_out, v_out):
    j = pl.program_id(1)

    @pl.when(j < LAT_TILES)
    def _():
        _mla_kv_math(ckv_ref[...], kr_ref[...], wk_ref, wv_ref, k_out, v_out)

    @pl.when(j == LAT_TILES)
    def _():
        _mla_kv_math(cckv_ref[0], ckr_ref[0], wk_ref, wv_ref, k_out, v_out)


def _mla_kv(ckv, kr, cache_ckv, cache_kr, wk, wv):
    weights = [pl.BlockSpec((128, MLA_QW), lambda *_: (0, 0)), pl.BlockSpec((128, BRANCH), lambda *_: (0, 0))]
    k_ctx, v_ctx = pl.pallas_call(
        _mla_kv_ctx_body,
        grid=(N_CTX // BIG_TILE,),
        in_specs=[pl.BlockSpec((BIG_TILE, 128), lambda i: (i, 0)), pl.BlockSpec((BIG_TILE, 128), lambda i: (i, 0))] + weights,
        out_specs=[pl.BlockSpec((BIG_TILE, MLA_QW), lambda i: (i, 0)),
                   pl.BlockSpec((BIG_TILE, BRANCH), lambda i: (i, 0))],
        out_shape=[jax.ShapeDtypeStruct((N_CTX, MLA_QW), BF16), jax.ShapeDtypeStruct((N_CTX, BRANCH), BF16)],
        compiler_params=_cparams("parallel"),
        name="mla_kv_ctx",
    )(ckv, kr, wk, wv)
    rows = pl.BlockSpec((ROW_TILE, 128), lambda b, j: (CTX_TILES + b * LAT_TILES + jnp.minimum(j, LAT_TILES - 1), 0))
    cache = pl.BlockSpec((1, PAST_LEN, 128), lambda b, j: (b, 0, 0))
    lk = LAT_LEN + PAST_LEN
    k_lat, v_lat = pl.pallas_call(
        _mla_kv_lat_body,
        grid=(N_LAT_SEQ, LAT_TILES + 1),
        in_specs=[rows, rows, cache, cache] + weights,
        out_specs=[pl.BlockSpec((1, ROW_TILE, MLA_QW), lambda b, j: (b, j, 0)),
                   pl.BlockSpec((1, ROW_TILE, BRANCH), lambda b, j: (b, j, 0))],
        out_shape=[jax.ShapeDtypeStruct((N_LAT_SEQ, lk, MLA_QW), BF16), jax.ShapeDtypeStruct((N_LAT_SEQ, lk, BRANCH), BF16)],
        compiler_params=_cparams("parallel", "parallel"),
        name="mla_kv_lat",
    )(ckv, kr, cache_ckv, cache_kr, wk, wv)
    return k_ctx, v_ctx, k_lat, v_lat


def _mla_attn_body(q_ref, k_ref, v_ref, g_ref, o_ref, *, ctx):
    if ctx:
        for t in range(q_ref.shape[0] // CTX_LEN):
            r = slice(t * CTX_LEN, (t + 1) * CTX_LEN)
            o_ref[r, :] = _mla_attn_tile(q_ref[r, :], k_ref[r, :], v_ref[r, :], g_ref[r, :])
    else:
        o_ref[...] = _mla_attn_tile(q_ref[...], k_ref[0], v_ref[0], g_ref[...])


def _mla_attn_tile(q, k, v, g):
    lane = lax.broadcasted_iota(jnp.int32, (1, BRANCH), 1)
    acc = jnp.zeros((q.shape[0], BRANCH), F32)
    for h in range(MLA_HEADS):
        sl = slice(h * MLA_HEAD_PAD, (h + 1) * MLA_HEAD_PAD)
        e, l = _exp2_rows(_dot_nt(q[:, sl], k[:, sl]))
        acc = jnp.where(lane // 64 == h, _dot(e.astype(BF16), v) * (1.0 / l), acc)
    return acc * _silu(g)


def _mla_attention(z_d, q, k_ctx, v_ctx, k_lat, v_lat):
    rows = CTX_SEQ_PER_STEP * CTX_LEN
    ctx = pl.pallas_call(
        functools.partial(_mla_attn_body, ctx=True),
        grid=(N_CTX // rows,),
        in_specs=[pl.BlockSpec((rows, MLA_QW), lambda i: (i, 0)),
                  pl.BlockSpec((rows, MLA_QW), lambda i: (i, 0)),
                  pl.BlockSpec((rows, BRANCH), lambda i: (i, 0)),
                  pl.BlockSpec((rows, BRANCH), lambda i: (i, 2))],
        out_specs=pl.BlockSpec((rows, BRANCH), lambda i: (i, 0)),
        out_shape=jax.ShapeDtypeStruct((N_CTX, BRANCH), F32),
        compiler_params=_cparams("parallel"),
        name="mla_attn_ctx",
    )(q, k_ctx, v_ctx, z_d)
    lk = LAT_LEN + PAST_LEN
    lat = pl.pallas_call(
        functools.partial(_mla_attn_body, ctx=False),
        grid=(N_LAT_SEQ, LAT_Q_TILES),
        in_specs=[pl.BlockSpec((LAT_Q_TILE, MLA_QW), lambda b, j: (N_CTX // LAT_Q_TILE + b * LAT_Q_TILES + j, 0)),
                  pl.BlockSpec((1, lk, MLA_QW), lambda b, j: (b, 0, 0)),
                  pl.BlockSpec((1, lk, BRANCH), lambda b, j: (b, 0, 0)),
                  pl.BlockSpec((LAT_Q_TILE, BRANCH), lambda b, j: (N_CTX // LAT_Q_TILE + b * LAT_Q_TILES + j, 2))],
        out_specs=pl.BlockSpec((LAT_Q_TILE, BRANCH), lambda b, j: (b * LAT_Q_TILES + j, 0)),
        out_shape=jax.ShapeDtypeStruct((N_LAT, BRANCH), F32),
        compiler_params=_cparams("parallel", "parallel"),
        name="mla_attn_lat",
    )(q, k_lat, v_lat, z_d)
    return ctx, lat


S5_TAP = S5_CHUNK * S5_CH
S5_NCHUNK = N_TOK // S5_CHUNK
S5_CTX_CH = N_CTX // S5_CHUNK
S5_CTX_SEQ_CH = CTX_LEN // S5_CHUNK
S5_LAT_SEQ_CH = LAT_LEN // S5_CHUNK
S5_SCAN_STEPS = S5_LAT_SEQ_CH.bit_length() - 1


def _s5_body(x_ref, mt_ref, bst_ref, cot_ref, a_ref, h0_ref, y_ref, fin_ref):
    x = x_ref[...].reshape(S5_TAP, S5_NCHUNK)
    y = _dot(mt_ref[0, 0], x)
    s = _dot(bst_ref[0, 0], x)
    lane = lax.broadcasted_iota(jnp.int32, (1, S5_NCHUNK), 1)
    is_lat = lane >= S5_CTX_CH
    pos_f = jnp.where(is_lat, (lane - S5_CTX_CH) & (S5_LAT_SEQ_CH - 1), lane & (S5_CTX_SEQ_CH - 1))
    pos_b = jnp.where(is_lat, S5_LAT_SEQ_CH - 1, S5_CTX_SEQ_CH - 1) - pos_f
    hin = []
    for d in range(2):
        n = S5_STATE
        sre, sim = s[2 * d * n:(2 * d + 1) * n], s[(2 * d + 1) * n:(2 * d + 2) * n]
        are = jnp.concatenate([a_ref[0, 0, 2 * d]] * (S5_NCHUNK // 128), axis=-1)
        aim = jnp.concatenate([a_ref[0, 0, 2 * d + 1]] * (S5_NCHUNK // 128), axis=-1)
        pos = pos_f if d == 0 else pos_b
        h0r, h0i = jnp.zeros_like(sre), jnp.zeros_like(sre)
        for b in range(N_LAT_SEQ):
            first = S5_CTX_CH + b * S5_LAT_SEQ_CH + (0 if d == 0 else S5_LAT_SEQ_CH - 1)
            h0r = jnp.where(lane == first, h0_ref[0, 2 * d][:, b:b + 1], h0r)
            h0i = jnp.where(lane == first, h0_ref[0, 2 * d + 1][:, b:b + 1], h0i)
        xr = sre + are * h0r - aim * h0i
        xi = sim + are * h0i + aim * h0r
        pr, pi = are, aim
        for j in range(S5_SCAN_STEPS):
            sh = 1 << j
            shift = sh if d == 0 else S5_NCHUNK - sh
            rr, ri = pltpu.roll(xr, shift, 1), pltpu.roll(xi, shift, 1)
            ok = pos >= sh
            xr, xi = (xr + jnp.where(ok, pr * rr - pi * ri, 0.0), xi + jnp.where(ok, pr * ri + pi * rr, 0.0))
            pr, pi = pr * pr - pi * pi, 2.0 * pr * pi
        last = lax.broadcasted_iota(jnp.int32, (1, 128), 1) * S5_CTX_SEQ_CH + (S5_CTX_SEQ_CH - 1 if d == 0 else 0)
        pick = jnp.where(lax.broadcasted_iota(jnp.int32, (S5_CTX_CH, 1), 0) == last, 1.0, 0.0)
        fin_ref[0, 2 * d] = _dot_sel(xr[:, :S5_CTX_CH], pick)
        fin_ref[0, 2 * d + 1] = _dot_sel(xi[:, :S5_CTX_CH], pick)
        one = 1 if d == 0 else S5_NCHUNK - 1
        hin.append(jnp.where(pos >= 1, pltpu.roll(xr, one, 1), h0r))
        hin.append(jnp.where(pos >= 1, pltpu.roll(xi, one, 1), h0i))
    y = y + _dot(cot_ref[0, 0], jnp.concatenate(hin, axis=0).astype(BF16))
    y_ref[...] = y.reshape(S5_CHUNK, S5_CH, S5_NCHUNK)


def _s5_scan(x_all, mt, bst, cot, a16, h0, l):
    g = S5_GROUPS
    sq = pl.BlockSpec((1, 1, S5_TAP, S5_TAP), lambda i: (l, i, 0, 0))
    st = pl.BlockSpec((1, 4, S5_STATE, 128), lambda i: (i, 0, 0, 0))
    return pl.pallas_call(
        _s5_body,
        grid=(g,),
        in_specs=[pl.BlockSpec((S5_CHUNK, S5_CH, S5_NCHUNK), lambda i: (0, i, 0)), sq, sq, sq,
                  pl.BlockSpec((1, 1, 4, S5_STATE, 128), lambda i: (l, i, 0, 0, 0)), st],
        out_specs=[pl.BlockSpec((S5_CHUNK, S5_CH, S5_NCHUNK), lambda i: (0, i, 0)),
                   pl.BlockSpec((1, 4, S5_STATE, 128), lambda i: (i, 0, 0, 0))],
        out_shape=[jax.ShapeDtypeStruct((S5_CHUNK, BRANCH, S5_NCHUNK), F32),
                   jax.ShapeDtypeStruct((g, 4, S5_STATE, 128), F32)],
        compiler_params=_cparams("parallel"),
        name="s5_scan",
    )(x_all, mt, bst, cot, a16, h0)


def _s5_out_body(y_ref, u_ref, g_ref, d_ref, w_ref, o_ref):
    y = u_ref[...] * d_ref[...] + y_ref[...]
    ge = 0.5 * y * (1.0 + jnp.tanh(0.7978845608028654 * (y + 0.044715 * (y * y * y))))
    gl = _dot(ge.astype(BF16), w_ref[...])
    o_ref[...] = gl[:, :BRANCH] * (1.0 / (1.0 + jnp.exp(-gl[:, BRANCH:]))) * _silu(g_ref[...])


def _s5_out(y_ssm, z_b, d_skip, w_glu):
    tile = 2 * BIG_TILE

    def col(c):
        return pl.BlockSpec((tile, BRANCH), lambda i: (i, c))
    return pl.pallas_call(
        _s5_out_body,
        grid=(N_TOK // tile,),
        in_specs=[col(0), col(0), col(1),
                  pl.BlockSpec((1, BRANCH), lambda i: (0, 0)),
                  pl.BlockSpec((BRANCH, 2 * BRANCH), lambda i: (0, 0))],
        out_specs=col(0),
        out_shape=jax.ShapeDtypeStruct((N_TOK, BRANCH), F32),
        compiler_params=_cparams("parallel"),
        name="s5_out",
    )(y_ssm, z_b, z_b, d_skip, w_glu)


HG_CHUNK = ROW_TILE
HG_W = 2 * HG_HEADS * HG_DK
HG_HEAD_W = 2 * HG_DK
HG_LAT_CHUNKS = LAT_LEN // HG_CHUNK
HG_CHUNKS = N_TOK // HG_CHUNK


def _hg_gates(z, lb):
    e = jnp.exp(-jnp.abs(z))
    r = 1.0 / (1.0 + e)
    sig_pos = jnp.where(z >= 0, r, e * r)
    sig_neg = jnp.where(z >= 0, e * r, r)
    return lb + (1.0 - lb) * sig_pos, (1.0 - lb) * sig_neg


def _bcast_row(x, period, r):
    n, w = x.shape
    if period >= 8:
        x3 = x.reshape(n // period, period, w)
        return jnp.broadcast_to(x3[:, r:r + 1, :], x3.shape).reshape(n, w)
    x3 = x.reshape(n // 8, 8, w)
    sub = lax.broadcasted_iota(jnp.int32, (1, 8, 1), 1)
    out = None
    for j in range(8 // period):
        b = jnp.broadcast_to(x3[:, j * period + r:j * period + r + 1, :], x3.shape)
        out = b if out is None else jnp.where(sub >= j * period, b, out)
    return out.reshape(n, w)


def _hg_scans(f, isb):
    n = f.shape[0]
    row = lax.broadcasted_iota(jnp.int32, (n, 1), 0)
    p, r = f, jnp.ones_like(f)
    levels = []
    h, sh = 1, 0
    while h < n:
        levels.append((h, sh, p, r))
        up = (row >> sh) & 1
        tot_p = jnp.where(isb == 1, _bcast_row(p, 2 * h, h), _bcast_row(p, 2 * h, h - 1))
        tot_r = jnp.where(isb == 1, _bcast_row(p, 2 * h, 0), _bcast_row(p, 2 * h, 2 * h - 1))
        p = p * jnp.where(up != isb, tot_p, 1.0)
        r = r * jnp.where(up == isb, tot_r, 1.0)
        h, sh = 2 * h, sh + 1
    return levels, p, r


def _hg_state_body(zf_ref, zb_ref, vf_ref, vb_ref, lb_ref, s0_ref, sf_out, sb_out, s_scr):
    i = pl.program_id(0)

    @pl.when(i % HG_LAT_CHUNKS == 0)
    def _():
        s_scr[...] = s0_ref[0]

    sf_out[0] = s_scr[:, 0:HG_DK, :]
    sb_out[0] = s_scr[:, HG_DK:, :]
    lane5 = lax.broadcasted_iota(jnp.int32, (1, HG_W), 1)
    isb = (lane5 >> 6) & 1
    z = jnp.where(isb == 1, zb_ref[...], zf_ref[...])
    f, k = _hg_gates(z, lb_ref[...])
    r, ptot = _hg_chunk_decay(f, isb)
    kt = k * r
    lane = lax.broadcasted_iota(jnp.int32, (1, BRANCH), 1)
    vf = vf_ref[...]
    vb = vb_ref[...]
    for hd in range(HG_HEADS):
        sl = slice(hd * HG_HEAD_W, (hd + 1) * HG_HEAD_W)
        kth = kt[:, sl].T.astype(BF16)
        hm = (lane >> 6) == hd
        d_f = _dot(kth, jnp.where(hm, vf, 0.0).astype(BF16))
        d_b = _dot(kth, jnp.where(hm, vb, 0.0).astype(BF16))
        ds = jnp.concatenate([d_f[:HG_DK], d_b[HG_DK:]], axis=0)
        pcol = jnp.broadcast_to(ptot[:, sl], (HG_HEAD_W, HG_HEAD_W)).T[:, 0:1]
        s_scr[hd] = s_scr[hd] * pcol + ds


def _hg_chunk_decay(f, isb):
    n = f.shape[0]
    row = lax.broadcasted_iota(jnp.int32, (n, 1), 0)
    dist = jnp.where(isb == 1, row, n - 1 - row)
    x = f
    sh = 1
    while sh < n:
        src = jnp.where(isb == 1, pltpu.roll(x, sh, 0), pltpu.roll(x, n - sh, 0))
        x = x * jnp.where(dist >= sh, src, 1.0)
        sh *= 2
    total = jnp.where(isb == 1, x[n - 1:n], x[0:1])
    nxt = jnp.where(isb == 1, pltpu.roll(x, 1, 0), pltpu.roll(x, n - 1, 0))
    return jnp.where(dist >= 1, nxt, 1.0), total


HG_LAT_STEPS = N_LAT_SEQ * HG_LAT_CHUNKS


def _hg_lat_rev(i):
    return (i // HG_LAT_CHUNKS) * HG_LAT_CHUNKS + (HG_LAT_CHUNKS - 1 - i % HG_LAT_CHUNKS)


def _hg_states(z_c, lb, s0):
    first = N_CTX_SEQ
    zz_f = pl.BlockSpec((HG_CHUNK, HG_W), lambda i: (first + i, 1))
    zz_b = pl.BlockSpec((HG_CHUNK, HG_W), lambda i: (first + _hg_lat_rev(i), 1))
    v_f = pl.BlockSpec((HG_CHUNK, BRANCH), lambda i: (first + i, 4))
    v_b = pl.BlockSpec((HG_CHUNK, BRANCH), lambda i: (first + _hg_lat_rev(i), 4))
    st = (HG_HEADS, HG_HEAD_W, BRANCH)
    half = (HG_HEADS, HG_DK, BRANCH)
    return pl.pallas_call(
        _hg_state_body,
        grid=(HG_LAT_STEPS,),
        in_specs=[zz_f, zz_b, v_f, v_b,
                  pl.BlockSpec((1, HG_W), lambda i: (0, 0)),
                  pl.BlockSpec((1,) + st, lambda i: (i // HG_LAT_CHUNKS, 0, 0, 0))],
        out_specs=[pl.BlockSpec((1,) + half, lambda i: (i, 0, 0, 0)),
                   pl.BlockSpec((1,) + half, lambda i: (_hg_lat_rev(i), 0, 0, 0))],
        out_shape=[jax.ShapeDtypeStruct((HG_LAT_STEPS,) + half, F32),
                   jax.ShapeDtypeStruct((HG_LAT_STEPS,) + half, F32)],
        scratch_shapes=[pltpu.VMEM(st, F32)],
        compiler_params=_cparams("arbitrary"),
        name="hg_states",
    )(z_c, z_c, z_c, z_c, lb, s0)


def _hg_main_body(qq_ref, zz_ref, v_ref, g_ref, sf_ref, sb_ref, lb_ref, ng_ref, o_ref, fin_ref):
    n = HG_CHUNK
    i = pl.program_id(0)
    qq = qq_ref[...]
    lane5 = lax.broadcasted_iota(jnp.int32, (1, HG_W), 1)
    isb = (lane5 >> 6) & 1
    f, k = _hg_gates(zz_ref[...], lb_ref[...])
    levels, pfull, rfull = _hg_scans(f, isb)
    row = lax.broadcasted_iota(jnp.int32, (n, 1), 0)
    col = lax.broadcasted_iota(jnp.int32, (1, n), 1)
    ops = [(qq.astype(BF16), k.astype(BF16), row == col)]
    for h, sh, p, r in levels:
        up = (row >> sh) & 1
        qt = jnp.where(up != isb, qq * p, 0.0).astype(BF16)
        kt = jnp.where(up == isb, k * r, 0.0).astype(BF16)
        ops.append((qt, kt, (row >> (sh + 1)) == (col >> (sh + 1))))
    qc = (qq * pfull).astype(BF16)
    v = v_ref[...]
    vb = v.astype(BF16)
    lane = lax.broadcasted_iota(jnp.int32, (1, BRANCH), 1)
    latent = i >= N_CTX_SEQ
    acc = jnp.zeros((n, BRANCH), F32)
    for hd in range(HG_HEADS):
        sl = slice(hd * HG_HEAD_W, (hd + 1) * HG_HEAD_W)
        a = jnp.zeros((n, n), F32)
        for qt, kt, mask in ops:
            a = a + jnp.where(mask, _dot_nt(qt[:, sl], kt[:, sl]), 0.0)
        s_in = jnp.concatenate([sf_ref[0, hd], sb_ref[0, hd]], axis=0)
        s_in = jnp.where(latent, s_in, 0.0).astype(BF16)
        o_h = _dot(a.astype(BF16), vb) + _dot(qc[:, sl], s_in)
        acc = jnp.where((lane >> 6) == hd, o_h, acc)
    sq = acc * acc
    ms = jnp.zeros((n, BRANCH), F32)
    for hd in range(HG_HEADS):
        hm = (lane >> 6) == hd
        ms = jnp.where(hm, jnp.sum(jnp.where(hm, sq, 0.0), axis=-1, keepdims=True), ms)
    o_ref[...] = acc * lax.rsqrt(ms * (1.0 / HG_DK) + EPS) * ng_ref[...] * _silu(g_ref[...])

    @pl.when(i < N_CTX_SEQ)
    def _():
        kt_full = k * rfull
        for hd in range(HG_HEADS):
            kth = kt_full[:, hd * HG_HEAD_W:(hd + 1) * HG_HEAD_W].T.astype(BF16)
            ds = _dot(kth, jnp.where((lane >> 6) == hd, v, 0.0).astype(BF16))
            fin_ref[0, hd] = ds[:, hd * HG_DK:(hd + 1) * HG_DK]


def _hg_main(z_c, s_f, s_b, lb, norm_g):
    half = (1, HG_HEADS, HG_DK, BRANCH)
    lat = lambda i: (jnp.maximum(i - N_CTX_SEQ, 0), 0, 0, 0)
    fin = (HG_HEADS, HG_HEAD_W, HG_DK)
    return pl.pallas_call(
        _hg_main_body,
        grid=(HG_CHUNKS,),
        in_specs=[pl.BlockSpec((HG_CHUNK, HG_W), lambda i: (i, 0)),
                  pl.BlockSpec((HG_CHUNK, HG_W), lambda i: (i, 1)),
                  pl.BlockSpec((HG_CHUNK, BRANCH), lambda i: (i, 4)),
                  pl.BlockSpec((HG_CHUNK, BRANCH), lambda i: (i, 5)),
                  pl.BlockSpec(half, lat),
                  pl.BlockSpec(half, lat),
                  pl.BlockSpec((1, HG_W), lambda i: (0, 0)),
                  pl.BlockSpec((1, BRANCH), lambda i: (0, 0))],
        out_specs=[pl.BlockSpec((HG_CHUNK, BRANCH), lambda i: (i, 0)),
                   pl.BlockSpec((1,) + fin, lambda i: (jnp.minimum(i, N_CTX_SEQ - 1), 0, 0, 0))],
        out_shape=[jax.ShapeDtypeStruct((N_TOK, BRANCH), F32),
                   jax.ShapeDtypeStruct((N_CTX_SEQ,) + fin, F32)],
        compiler_params=_cparams("arbitrary"),
        name="hg_main",
    )(z_c, z_c, z_c, z_c, s_f, s_b, lb, norm_g)


def _take_cols(w, plan):
    idx = np.concatenate([p[0] for p in plan]).astype(np.int32)
    sign = np.concatenate([np.broadcast_to(p[1], p[0].shape) for p in plan]).astype(np.float32)
    return jnp.take(w, jnp.asarray(idx), axis=-1) * jnp.asarray(sign)


def _zeros(n):
    return (np.zeros(n, np.int64), 0.0)


_IN_OFF = {}
_off = 0
for _name, _n in (("da_q", 256), ("da_k", 256), ("da_v", 256), ("da_g", 256), ("s5_u", 256), ("s5_g", 256),
                  ("hg_q", 256), ("hg_ff", 256), ("hg_fb", 256), ("hg_i", 256), ("hg_g", 256),
                  ("mla_cq", MLA_Q_RANK), ("mla_ckv", MLA_KV_RANK), ("mla_kr", MLA_ROPE), ("mla_g", 256)):
    _IN_OFF[_name] = np.arange(_off, _off + _n)
    _off += _n


def _rope_tables():
    t = np.arange(LAT_LEN)
    pos = np.stack([t // GRID_W, t % GRID_W], axis=1).astype(np.float32)
    inv_freq = (np.float32(ROPE_BASE) ** (-np.arange(8, dtype=np.float32) / np.float32(8))).astype(np.float32)
    r = np.arange(MLA_ROPE)
    ang = (pos[:, r // 16] * inv_freq[r % 8][None, :]).astype(np.float64)
    cos32, sin32 = np.cos(ang).astype(np.float32), np.sin(ang).astype(np.float32)
    lo = (np.arange(MLA_ROPE) % 16 < 8)[None, :]
    sin_lo32, sin_hi32 = np.where(lo, -sin32, 0.0).astype(np.float32), np.where(lo, 0.0, sin32).astype(np.float32)
    da_tabs = tuple(np.tile(x, (1, 8)) for x in (cos32, sin_lo32, sin_hi32))

    def head(x, fill):
        h = np.concatenate([np.full((LAT_LEN, MLA_NOPE), fill, np.float32), x,
                            np.full((LAT_LEN, MLA_HEAD_PAD - MLA_NOPE - MLA_ROPE), fill, np.float32)], axis=1)
        return np.concatenate([h, np.full((BIG_TILE, MLA_HEAD_PAD), fill, np.float32)], axis=0)
    k_tabs = (head(cos32, 1.0), head(sin_lo32, 0.0), head(sin_hi32, 0.0))
    return tuple(jnp.asarray(x) for x in da_tabs), tuple(jnp.asarray(x) for x in k_tabs)


def _mla_weights(w_uq, w_ukv, q_norm):
    hd = MLA_NOPE + MLA_ROPE
    pad_tail = _zeros(MLA_HEAD_PAD - hd)
    q_plan, k_plan, v_plan = [], [], []
    for h in range(MLA_HEADS):
        nope, rope = np.arange(h * hd, h * hd + MLA_NOPE), np.arange(h * hd + MLA_NOPE, (h + 1) * hd)
        q_plan += [(nope, 1.0), (rope, 1.0), pad_tail]
        k_plan += [(np.arange(h * 2 * MLA_NOPE, h * 2 * MLA_NOPE + MLA_NOPE), 1.0), _zeros(MLA_HEAD_PAD - MLA_NOPE)]
        v_plan += [(np.arange(h * 2 * MLA_NOPE + MLA_NOPE, (h + 1) * 2 * MLA_NOPE), 1.0)]
    pad_rows = lambda x: jnp.pad(x, ((0, 256 - MLA_Q_RANK), (0, 0))).astype(BF16)
    qn = jnp.pad(q_norm, (0, 256 - MLA_Q_RANK)).reshape(1, 256)
    return (pad_rows(_take_cols(w_uq, q_plan)), _take_cols(w_ukv, k_plan).astype(BF16),
            _take_cols(w_ukv, v_plan).astype(BF16), qn)


def _split_bf16(a):
    hi = a.astype(BF16)
    return hi, (a - hi.astype(F32)).astype(BF16)


def _dot_sel(a, sel):
    hi, lo = _split_bf16(a)
    sel = sel.astype(BF16)
    return _dot(hi, sel) + _dot(lo, sel)


def _dot_x3(a, b):
    a_hi, a_lo = _split_bf16(a)
    b_hi, b_lo = _split_bf16(b)
    return _dot(a_hi, b_hi) + _dot(a_hi, b_lo) + _dot(a_lo, b_hi)


def _s5_table_body(xy_ref, bb_ref, c_ref, ct_ref, mt_ref, bst_ref, cot_ref, a_ref):
    n, t, ch = S5_STATE, S5_CHUNK, S5_CH
    wide = 2 * S5_TAP
    xy = xy_ref[0, 0]
    tau_i = lax.broadcasted_iota(jnp.int32, (1, 128), 1)
    tau = tau_i.astype(F32)
    sel_row = lax.broadcasted_iota(jnp.int32, (128, 1), 0)

    def lag(width):
        return lax.broadcasted_iota(jnp.int32, (1, width), 1) >> 4

    def onehot(cond):
        return jnp.where(cond, 1.0, 0.0).astype(F32)
    j = lag(wide)
    e_z = (onehot((j <= t - 1) & (sel_row == t - 1 - j)), onehot((j >= t - 1) & (j <= 2 * t - 2) & (sel_row == j - (t - 1))))
    jc = lag(S5_TAP)
    e_c = (onehot(sel_row == jc + 1), onehot(sel_row == t - jc))
    ch_row = lax.broadcasted_iota(jnp.int32, (ch, 1), 0)
    tile_w = onehot((lax.broadcasted_iota(jnp.int32, (1, wide), 1) & (ch - 1)) == ch_row)
    tile_n = onehot((lax.broadcasted_iota(jnp.int32, (1, S5_TAP), 1) & (ch - 1)) == ch_row)

    z, cot_rows, klong = [], [], None
    for d in range(2):
        x, y = xy[:, 2 * d:2 * d + 1], xy[:, 2 * d + 1:2 * d + 2]
        mag = jnp.exp(jnp.where(tau_i <= t, tau, 0.0) * x)
        ang = jnp.where(tau_i <= t, tau, 0.0) * y
        p_re = jnp.where(tau_i <= t, mag * jnp.cos(ang), 0.0)
        p_im = jnp.where(tau_i <= t, mag * jnp.sin(ang), 0.0)
        a_ref[0, 0, 2 * d] = jnp.broadcast_to(p_re[:, t:t + 1], (n, 128))
        a_ref[0, 0, 2 * d + 1] = jnp.broadcast_to(p_im[:, t:t + 1], (n, 128))
        pz_re, pz_im = _dot_sel(p_re, e_z[d]), _dot_sel(p_im, e_z[d])
        b_re, b_im = _dot_sel(bb_ref[0, 0, 2 * d], tile_w), _dot_sel(bb_ref[0, 0, 2 * d + 1], tile_w)
        z_re, z_im = pz_re * b_re - pz_im * b_im, pz_re * b_im + pz_im * b_re
        z += [z_re, z_im]
        part = _dot_x3(c_ref[0, 0, 2 * d], z_re) - _dot_x3(c_ref[0, 0, 2 * d + 1], z_im)
        klong = part if klong is None else klong + part
        pc_re, pc_im = _dot_sel(p_re, e_c[d]), _dot_sel(p_im, e_c[d])
        c_re, c_im = _dot_sel(ct_ref[0, 0, 2 * d], tile_n), _dot_sel(ct_ref[0, 0, 2 * d + 1], tile_n)
        cot_rows += [c_re * pc_re - c_im * pc_im, -(c_re * pc_im + c_im * pc_re)]
    for tt in range(t):
        off = (t - 1 - tt) * ch
        win = klong if off == 0 else pltpu.roll(klong, wide - off, 1)
        mt_ref[0, 0, tt * ch:(tt + 1) * ch, :] = win[:, :S5_TAP].astype(BF16)
    back = pltpu.roll(z[2], wide - (t - 1) * ch, 1), pltpu.roll(z[3], wide - (t - 1) * ch, 1)
    for k, rows in enumerate((z[0], z[1], back[0], back[1])):
        bst_ref[0, 0, k * n:(k + 1) * n, :] = rows[:, :S5_TAP].astype(BF16)
    cot_ref[0, 0] = jnp.concatenate(cot_rows, axis=0).T.astype(BF16)


def _s5_tables(a_re, a_im, log_dt, b_re, b_im, c_re, c_im):
    nl, g, n, ch = a_re.shape[0], S5_GROUPS, S5_STATE, S5_CH
    step = jnp.exp(log_dt)[..., None]
    mag = jnp.exp(a_re * step)
    ab_re, ab_im = mag * jnp.cos(a_im * step), mag * jnp.sin(a_im * step)
    den = a_re * a_re + a_im * a_im
    f_re = ((ab_re - 1.0) * a_re + ab_im * a_im) / den
    f_im = (ab_im * a_re - (ab_re - 1.0) * a_im) / den
    bb_re = f_re[..., None] * b_re - f_im[..., None] * b_im
    bb_im = f_re[..., None] * b_im + f_im[..., None] * b_re
    by_group = lambda x: jnp.moveaxis(x, 1, 2)
    pair = lambda re, im: jnp.stack([by_group(re), by_group(im)], axis=3).reshape((nl, g, 4) + re.shape[3:])
    xy = jnp.stack([by_group(a_re * step), by_group(a_im * step)], axis=3).reshape(nl, g, 4, n)
    xy = jnp.pad(jnp.swapaxes(xy, 2, 3), ((0, 0), (0, 0), (0, 0), (0, 4)))
    mat = pl.BlockSpec((1, 1, S5_TAP, S5_TAP), lambda l, i: (l, i, 0, 0))
    return pl.pallas_call(
        _s5_table_body,
        grid=(nl, g),
        in_specs=[pl.BlockSpec((1, 1, n, 8), lambda l, i: (l, i, 0, 0)),
                  pl.BlockSpec((1, 1, 4, n, ch), lambda l, i: (l, i, 0, 0, 0)),
                  pl.BlockSpec((1, 1, 4, ch, n), lambda l, i: (l, i, 0, 0, 0)),
                  pl.BlockSpec((1, 1, 4, n, ch), lambda l, i: (l, i, 0, 0, 0))],
        out_specs=[mat, mat, mat, pl.BlockSpec((1, 1, 4, n, 128), lambda l, i: (l, i, 0, 0, 0))],
        out_shape=[jax.ShapeDtypeStruct((nl, g, S5_TAP, S5_TAP), BF16)] * 3
        + [jax.ShapeDtypeStruct((nl, g, 4, n, 128), F32)],
        compiler_params=_cparams("parallel", "parallel"),
        name="s5_tables",
    )(xy, pair(bb_re, bb_im), pair(c_re, c_im), pair(jnp.swapaxes(c_re, -1, -2), jnp.swapaxes(c_im, -1, -2)))


def _s5_chunk_lanes(u):
    return u.reshape(S5_NCHUNK, S5_CHUNK, BRANCH).transpose(1, 2, 0)


def _s5_token_rows(y):
    return y.transpose(2, 0, 1).reshape(N_TOK, BRANCH)


def kernel(x_prompt, x_sample, cache_diff_k, cache_diff_v, state_s5, state_hgrn, cache_mla_ckv, cache_mla_krope, c, c_ctx, w_mod, b_mod, w_in, w_out, da_lambda, da_norm, s5_a_re, s5_a_im, s5_log_dt, s5_b_re, s5_b_im, s5_c_re, s5_c_im, s5_d, s5_w_glu, hg_lb, hg_norm, mla_q_norm, mla_w_uq, mla_kv_norm, mla_w_ukv, final_norm):
    lb_w = jax.nn.softmax(hg_lb.astype(F32), axis=0)
    lb_all = jnp.cumsum(lb_w, axis=0) - lb_w[0:1]
    c_rows = jnp.concatenate([c_ctx[None], c, jnp.zeros((8 - 1 - N_LAT_SEQ, D_MODEL), F32)], axis=0)
    mods = _modulation(c_rows, w_mod, b_mod)
    da_tabs, mla_tabs = _rope_tables()
    xs = (x_prompt.reshape(N_CTX, D_MODEL), x_sample.reshape(N_LAT, D_MODEL))
    new_k, new_v, new_s5, new_hg, new_ckv, new_kr = [], [], [], [], [], []
    s5_tabs = _s5_tables(s5_a_re, s5_a_im, s5_log_dt, s5_b_re, s5_b_im, s5_c_re, s5_c_im)
    w_all = _arrange_w_in(w_in)
    for l in range(DEPTH):
        mod = mods[l, :3].reshape(3, 3, D_MODEL)
        z_a, z_b, z_c, z_d, k_new, v_new, u_bf = _in_proj(xs, mod, w_all, l)

        lam_init = 0.8 - 0.6 * math.exp(-0.3 * l)
        kv_lat = _da_latent_kv(z_a, da_tabs,
                               cache_diff_k[:, l].reshape(N_LAT_SEQ, PAST_LEN, BRANCH),
                               cache_diff_v[:, l].reshape(N_LAT_SEQ, PAST_LEN, BRANCH))
        a_out = _da_attention(z_a, da_lambda[l], da_norm[l], lam_init, da_tabs, kv_lat)
        new_k.append(k_new.reshape(N_CTX_SEQ, CTX_LEN, DA_HEADS, 2 * DA_QK))
        new_v.append(v_new.reshape(N_CTX_SEQ, CTX_LEN, DA_HEADS, 2 * DA_QK))

        h0 = state_s5[:, l].transpose(2, 1, 4, 3, 0).reshape(S5_GROUPS, 4, S5_STATE, N_LAT_SEQ)
        h0 = jnp.pad(h0, ((0, 0), (0, 0), (0, 0), (0, 128 - N_LAT_SEQ)))
        y_all, fin = _s5_scan(_s5_chunk_lanes(u_bf), *s5_tabs, h0, l)
        b_out = _s5_out(_s5_token_rows(y_all), z_b, s5_d[l].reshape(1, BRANCH), s5_w_glu[l].astype(BF16))
        fin = fin[..., :N_CTX_SEQ].reshape(S5_GROUPS, 2, 2, S5_STATE, N_CTX_SEQ)
        new_s5.append(fin.transpose(4, 1, 0, 3, 2))

        lb = jnp.concatenate([lb_all[l, 0].reshape(HG_HEADS, HG_DK), lb_all[l, 1].reshape(HG_HEADS, HG_DK)],
                             axis=-1).reshape(1, HG_W)
        head_eye = jnp.eye(HG_HEADS, dtype=F32)
        s0 = state_hgrn[:, l].transpose(0, 2, 1, 3, 4).reshape(N_LAT_SEQ, HG_HEADS, HG_HEAD_W, 1, HG_DK)
        s0 = (s0 * head_eye[None, :, None, :, None]).reshape(N_LAT_SEQ, HG_HEADS, HG_HEAD_W, BRANCH)
        s_f, s_b = _hg_states(z_c, lb, s0)
        c_out, s_fin = _hg_main(z_c, s_f, s_b, lb, jnp.tile(hg_norm[l].reshape(1, HG_DK), (1, HG_HEADS)))
        new_hg.append(s_fin.reshape(N_CTX_SEQ, HG_HEADS, 2, HG_DK, HG_DK).transpose(0, 2, 1, 3, 4))

        wq, wk, wv, qn = _mla_weights(mla_w_uq[l], mla_w_ukv[l], mla_q_norm[l])
        q, ckv_n, kr = _mla_prep(z_d, mla_tabs, qn, mla_kv_norm[l].reshape(1, MLA_KV_RANK), wq)
        kr_cache = jnp.pad(cache_mla_krope[:, l], ((0, 0), (0, 0), (MLA_NOPE, 128 - MLA_NOPE - MLA_ROPE)))
        k_ctx, v_ctx, k_lat, v_lat = _mla_kv(ckv_n, kr, cache_mla_ckv[:, l], kr_cache, wk, wv)
        d_out = _mla_attention(z_d, q, k_ctx, v_ctx, k_lat, v_lat)
        new_ckv.append(ckv_n[:N_CTX].reshape(N_CTX_SEQ, CTX_LEN, MLA_KV_RANK))
        new_kr.append(kr[:N_CTX, MLA_NOPE:MLA_NOPE + MLA_ROPE].reshape(N_CTX_SEQ, CTX_LEN, MLA_ROPE))

        xs = _out_proj(a_out, b_out, c_out, d_out, xs, mod, w_out, l,
                       final_norm.reshape(1, D_MODEL), final=(l == DEPTH - 1))
        xs = tuple(xs) if l == DEPTH - 1 else (xs,)
    y_prompt = xs[0].reshape(N_CTX_SEQ, CTX_LEN, D_MODEL)
    y_sample = xs[1].reshape(N_LAT_SEQ, LAT_LEN, D_MODEL)
    st = lambda parts: jnp.stack(parts, axis=1)
    return (y_prompt, y_sample, st(new_k), st(new_v), st(new_s5), st(new_hg), st(new_ckv), st(new_kr))
```

```python
import functools
import math

import numpy as np

import jax
import jax.numpy as jnp
from jax import lax
from jax.experimental import pallas as pl
from jax.experimental.pallas import tpu as pltpu

F32 = jnp.float32
BF16 = jnp.bfloat16

D_MODEL = 1024
DEPTH = 2
N_CTX_SEQ = 16
CTX_LEN = 256
N_LAT_SEQ = 2
LAT_LEN = 2048
PAST_LEN = 256
GRID_W = 64
N_CTX = N_CTX_SEQ * CTX_LEN
N_LAT = N_LAT_SEQ * LAT_LEN
N_TOK = N_CTX + N_LAT
BRANCH = 256
EPS = 1e-6
ROPE_BASE = 10000.0
ROW_TILE = 256
LAT_TILES = LAT_LEN // ROW_TILE
N_TILES = N_TOK // ROW_TILE
CTX_TILES = N_CTX // ROW_TILE
VMEM_LIMIT = 48 * 1024 * 1024
IN_PROJ_VMEM_LIMIT = 56 * 1024 * 1024
LAT_Q_TILE = 512
LAT_Q_TILES = LAT_LEN // LAT_Q_TILE
BIG_TILE = 512
CTX_SEQ_PER_STEP = 2

DA_HEADS = 4
DA_QK = 32
MLA_HEADS = 4
MLA_NOPE = 64
MLA_ROPE = 32
MLA_Q_RANK = 192
MLA_KV_RANK = 128
S5_GROUPS = 16
S5_CH = 16
S5_STATE = 64
S5_CHUNK = 16
HG_HEADS = 4
HG_DK = 64

W_A = 1024
W_B = 512
W_C = 1536
W_D = 768
W_ALL = W_A + W_B + W_C + W_D


def _cparams(*sem):
    return pltpu.CompilerParams(dimension_semantics=sem, vmem_limit_bytes=VMEM_LIMIT)


def _tile_seq(i, tile=ROW_TILE):
    return jnp.where(i < N_CTX // tile, 0, 1 + (i - N_CTX // tile) // (LAT_LEN // tile))


def _silu(x):
    return x * (1.0 / (1.0 + jnp.exp(-x)))


def _dot(a, b):
    return jnp.dot(a, b, preferred_element_type=F32)


def _dot_nt(a, b):
    return lax.dot_general(a, b, (((1,), (1,)), ((), ())), preferred_element_type=F32)


def _mod_body(c_ref, w_ref, b_ref, o_ref):
    c = _silu(c_ref[...]).astype(BF16)
    o_ref[0] = _dot(c, w_ref[0].astype(BF16)) + b_ref[0]


def _modulation(c_rows, w_mod, b_mod):
    tn = 768
    return pl.pallas_call(
        _mod_body,
        grid=(DEPTH, 3 * D_MODEL // tn),
        in_specs=[pl.BlockSpec((8, D_MODEL), lambda l, j: (0, 0)),
                  pl.BlockSpec((1, D_MODEL, tn), lambda l, j: (l, 0, j)),
                  pl.BlockSpec((1, 1, tn), lambda l, j: (l, 0, j))],
        out_specs=pl.BlockSpec((1, 8, tn), lambda l, j: (l, 0, j)),
        out_shape=jax.ShapeDtypeStruct((DEPTH, 8, 3 * D_MODEL), F32),
        compiler_params=_cparams("parallel", "parallel"),
        name="modulation",
    )(c_rows, w_mod, b_mod.reshape(DEPTH, 1, 3 * D_MODEL))


def _split_rows(i, ctx_ref, lat_ref):
    return jnp.where(i < N_CTX // ctx_ref.shape[0], ctx_ref[...], lat_ref[...])


def _ctx_tile_spec(w, tile=ROW_TILE):
    return pl.BlockSpec((tile, w), lambda i: (jnp.minimum(i, N_CTX // tile - 1), 0))


def _lat_tile_spec(w, tile=ROW_TILE):
    return pl.BlockSpec((tile, w), lambda i: (jnp.maximum(i - N_CTX // tile, 0), 0))


def _in_proj_body(*refs, split):
    if split:
        xc_ref, xl_ref, mod_ref, w_ref, oa, ob, oc, od, ok, ov, ou = refs
        x = _split_rows(pl.program_id(0), xc_ref, xl_ref)
    else:
        x_ref, mod_ref, w_ref, oa, ob, oc, od, ok, ov, ou = refs
        x = x_ref[...]
    xn = x * lax.rsqrt(jnp.mean(x * x, axis=-1, keepdims=True) + EPS)
    mod = mod_ref[0]
    h = (xn * (1.0 + mod[1:2]) + mod[0:1]).astype(BF16)
    off = 0
    for o in (oa, ob, oc, od):
        w = o.shape[-1]
        o[...] = _dot(h, w_ref[0, :, off:off + w])
        off += w
    ou[...] = ob[:, :BRANCH].astype(BF16)

    @pl.when(pl.program_id(0) < N_CTX // BIG_TILE)
    def _():
        ok[...] = oa[:, BRANCH:2 * BRANCH]
        ov[...] = oa[:, 2 * BRANCH:3 * BRANCH]


def _in_proj(xs, mod, w_all, l):
    widths = (W_A, W_B, W_C, W_D)
    split = len(xs) == 2
    x_specs = ([_ctx_tile_spec(D_MODEL, BIG_TILE), _lat_tile_spec(D_MODEL, BIG_TILE)] if split
               else [pl.BlockSpec((BIG_TILE, D_MODEL), lambda i: (i, 0))])
    return pl.pallas_call(
        functools.partial(_in_proj_body, split=split),
        grid=(N_TOK // BIG_TILE,),
        in_specs=x_specs + [pl.BlockSpec((1, 3, D_MODEL), lambda i: (_tile_seq(i, BIG_TILE), 0, 0)),
                            pl.BlockSpec((1, D_MODEL, W_ALL), lambda i: (l, 0, 0))],
        out_specs=[pl.BlockSpec((BIG_TILE, w), lambda i: (i, 0)) for w in widths]
        + [_ctx_tile_spec(BRANCH, BIG_TILE)] * 2 + [pl.BlockSpec((BIG_TILE, BRANCH), lambda i: (i, 0))],
        out_shape=[jax.ShapeDtypeStruct((N_TOK, w), F32) for w in widths]
        + [jax.ShapeDtypeStruct((N_CTX, BRANCH), F32)] * 2 + [jax.ShapeDtypeStruct((N_TOK, BRANCH), BF16)],
        compiler_params=pltpu.CompilerParams(dimension_semantics=("arbitrary",), vmem_limit_bytes=IN_PROJ_VMEM_LIMIT),
        name="in_proj",
    )(*xs, mod, w_all)


def _arrange_body(wt_ref, o_ref):
    w = wt_ref[0]
    c = _IN_OFF
    rows = lambda name: w[c[name][0]:c[name][-1] + 1]
    zero = lambda n: jnp.zeros((n, w.shape[1]), F32)

    def per_head(x, y):
        xs, ys = rows(x), rows(y)
        return [p for h in range(HG_HEADS) for p in (xs[h * HG_DK:(h + 1) * HG_DK], ys[h * HG_DK:(h + 1) * HG_DK])]
    pieces = ([w[0:W_A + W_B]] + per_head("hg_q", "hg_q") + per_head("hg_ff", "hg_fb") + [rows("hg_i"), rows("hg_g")]
              + [rows("mla_cq"), zero(256 - MLA_Q_RANK), rows("mla_ckv"), zero(MLA_NOPE), rows("mla_kr"),
                 zero(128 - MLA_NOPE - MLA_ROPE), rows("mla_g")])
    o_ref[0] = jnp.concatenate(pieces, axis=0).T.astype(BF16)


def _arrange_w_in(w_in):
    lanes = 256
    wt = jnp.swapaxes(w_in, 1, 2)
    return pl.pallas_call(
        _arrange_body,
        grid=(DEPTH, D_MODEL // lanes),
        in_specs=[pl.BlockSpec((1, wt.shape[1], lanes), lambda l, i: (l, 0, i))],
        out_specs=pl.BlockSpec((1, lanes, W_ALL), lambda l, i: (l, i, 0)),
        out_shape=jax.ShapeDtypeStruct((DEPTH, D_MODEL, W_ALL), BF16),
        compiler_params=_cparams("parallel", "parallel"),
        name="arrange_w_in",
    )(wt)


def _out_proj_body(*refs, split_in, final):
    ac_ref, al_ref, b_ref, c_ref, dc_ref, dl_ref = refs[:6]
    i = pl.program_id(0)
    if split_in:
        xc_ref, xl_ref, mod_ref, w_ref, fn_ref = refs[6:11]
        x = _split_rows(i, xc_ref, xl_ref)
    else:
        x_ref, mod_ref, w_ref, fn_ref = refs[6:10]
        x = x_ref[...]
    branches = (_split_rows(i, ac_ref, al_ref), b_ref[...], c_ref[...], _split_rows(i, dc_ref, dl_ref))
    acc = None
    for j, r in enumerate(branches):
        t = _dot(r.astype(BF16), w_ref[0, j * BRANCH:(j + 1) * BRANCH, :].astype(BF16))
        acc = t if acc is None else acc + t
    x = x + mod_ref[0][2:3] * acc
    if not final:
        refs[-1][...] = x
        return
    y = x * lax.rsqrt(jnp.mean(x * x, axis=-1, keepdims=True) + EPS) * fn_ref[...]
    yc_ref, yl_ref = refs[-2:]

    @pl.when(i < N_CTX // BIG_TILE)
    def _():
        yc_ref[...] = y

    @pl.when(i >= N_CTX // BIG_TILE)
    def _():
        yl_ref[...] = y


def _out_proj(a, b, c, d, xs, mod, w_out, l, final_norm, final):
    br = pl.BlockSpec((BIG_TILE, BRANCH), lambda i: (i, 0))
    pair = [_ctx_tile_spec(BRANCH, BIG_TILE), _lat_tile_spec(BRANCH, BIG_TILE)]
    split_in = len(xs) == 2
    x_specs = ([_ctx_tile_spec(D_MODEL, BIG_TILE), _lat_tile_spec(D_MODEL, BIG_TILE)] if split_in
               else [pl.BlockSpec((BIG_TILE, D_MODEL), lambda i: (i, 0))])
    if final:
        out_specs = [_ctx_tile_spec(D_MODEL, BIG_TILE), _lat_tile_spec(D_MODEL, BIG_TILE)]
        out_shape = [jax.ShapeDtypeStruct((N_CTX, D_MODEL), F32), jax.ShapeDtypeStruct((N_LAT, D_MODEL), F32)]
    else:
        out_specs = pl.BlockSpec((BIG_TILE, D_MODEL), lambda i: (i, 0))
        out_shape = jax.ShapeDtypeStruct((N_TOK, D_MODEL), F32)
    return pl.pallas_call(
        functools.partial(_out_proj_body, split_in=split_in, final=final),
        grid=(N_TOK // BIG_TILE,),
        in_specs=pair + [br, br] + pair + x_specs + [
            pl.BlockSpec((1, 3, D_MODEL), lambda i: (_tile_seq(i, BIG_TILE), 0, 0)),
            pl.BlockSpec((1, D_MODEL, D_MODEL), lambda i: (l, 0, 0)),
            pl.BlockSpec((1, D_MODEL), lambda i: (0, 0))],
        out_specs=out_specs,
        out_shape=out_shape,
        compiler_params=_cparams("arbitrary"),
        name="out_proj",
    )(*a, b, c, *d, *xs, mod, w_out, final_norm)


LOG2E = 1.4426950408889634


def _exp2_rows(s):
    e = jnp.exp2(s - jnp.max(s, axis=-1, keepdims=True))
    return e, jnp.sum(e, axis=-1, keepdims=True)


def _rope(x, cos, sin_lo, sin_hi):
    w = x.shape[-1]
    return x * cos + pltpu.roll(x, w - 8, 1) * sin_lo + pltpu.roll(x, 8, 1) * sin_hi


def _da_kv_body(k_ref, v_ref, cos_ref, slo_ref, shi_ref, ck_ref, cv_ref, ko_ref, vo_ref):
    j = pl.program_id(1)

    @pl.when(j < LAT_TILES)
    def _():
        ko_ref[0] = _rope(k_ref[...], cos_ref[...], slo_ref[...], shi_ref[...]).astype(BF16)
        vo_ref[0] = v_ref[...].astype(BF16)

    @pl.when(j == LAT_TILES)
    def _():
        ko_ref[0] = ck_ref[0].astype(BF16)
        vo_ref[0] = cv_ref[0].astype(BF16)


def _da_latent_kv(z_a, tabs, cache_k, cache_v):
    def rows(col):
        return pl.BlockSpec(
            (ROW_TILE, BRANCH),
            lambda b, j: (CTX_TILES + b * LAT_TILES + jnp.minimum(j, LAT_TILES - 1), col))
    tab = pl.BlockSpec((ROW_TILE, BRANCH), lambda b, j: (jnp.minimum(j, LAT_TILES - 1), 0))
    cache = pl.BlockSpec((1, PAST_LEN, BRANCH), lambda b, j: (b, 0, 0))
    out = pl.BlockSpec((1, ROW_TILE, BRANCH), lambda b, j: (b, j, 0))
    shp = jax.ShapeDtypeStruct((N_LAT_SEQ, LAT_LEN + PAST_LEN, BRANCH), BF16)
    return pl.pallas_call(
        _da_kv_body,
        grid=(N_LAT_SEQ, LAT_TILES + 1),
        in_specs=[rows(1), rows(2), tab, tab, tab, cache, cache],
        out_specs=[out, out],
        out_shape=[shp, shp],
        compiler_params=_cparams("parallel", "parallel"),
        name="da_kv",
    )(z_a, z_a, *tabs, cache_k, cache_v)


def _da_attn_body(lam_ref, ng_ref, q_ref, *rest, rope, lam_init):
    if rope:
        cos_ref, slo_ref, shi_ref, k_ref, v_ref, g_ref, o_ref = rest
        q = _rope(q_ref[...], cos_ref[...], slo_ref[...], shi_ref[...])
        o_ref[...] = _da_attn_tile(lam_ref, ng_ref, q, k_ref[0], v_ref[0], g_ref[...], lam_init)
    else:
        k_ref, v_ref, g_ref, o_ref = rest
        for t in range(q_ref.shape[0] // CTX_LEN):
            r = slice(t * CTX_LEN, (t + 1) * CTX_LEN)
            o_ref[r, :] = _da_attn_tile(lam_ref, ng_ref, q_ref[r, :], k_ref[r, :].astype(BF16),
                                        v_ref[r, :].astype(BF16), g_ref[r, :], lam_init)


def _da_attn_tile(lam_ref, ng_ref, q, k, v, g, lam_init):
    q = q * (DA_QK ** -0.5 * LOG2E)
    lv = lam_ref[...]
    lam = (jnp.exp(jnp.sum(lv[0:1] * lv[1:2], axis=-1, keepdims=True))
           - jnp.exp(jnp.sum(lv[2:3] * lv[3:4], axis=-1, keepdims=True)) + lam_init)
    lane = lax.broadcasted_iota(jnp.int32, (1, BRANCH), 1)
    acc = jnp.zeros(q.shape, F32)
    for h in range(DA_HEADS):
        q1 = jnp.where(lane // DA_QK == 2 * h, q, 0.0).astype(BF16)
        q2 = jnp.where(lane // DA_QK == 2 * h + 1, q, 0.0).astype(BF16)
        e1, l1 = _exp2_rows(_dot_nt(q1, k))
        e2, l2 = _exp2_rows(_dot_nt(q2, k))
        a = e1.astype(BF16) - (lam * l1 / l2).astype(BF16) * e2.astype(BF16)
        acc = jnp.where(lane // (2 * DA_QK) == h, _dot(a, v) * (1.0 / l1), acc)
    sq = acc * acc
    ms = jnp.zeros(q.shape, F32)
    for h in range(DA_HEADS):
        hm = lane // (2 * DA_QK) == h
        ms = jnp.where(hm, jnp.sum(jnp.where(hm, sq, 0.0), axis=-1, keepdims=True), ms)
    o = acc * lax.rsqrt(ms * (1.0 / (2 * DA_QK)) + EPS) * (ng_ref[...] * (1.0 - lam_init))
    return o * _silu(g)


def _da_attention(z_a, lam_vec, norm_g, lam_init, tabs, kv_lat):
    ng = jnp.tile(norm_g.reshape(1, 2 * DA_QK), (1, DA_HEADS))
    small = [pl.BlockSpec((4, DA_QK), lambda *_: (0, 0)), pl.BlockSpec((1, BRANCH), lambda *_: (0, 0))]

    rows = CTX_SEQ_PER_STEP * CTX_LEN

    def col(c):
        return pl.BlockSpec((rows, BRANCH), lambda i: (i, c))
    ctx = pl.pallas_call(
        functools.partial(_da_attn_body, rope=False, lam_init=lam_init),
        grid=(N_CTX // rows,),
        in_specs=small + [col(0), col(1), col(2), col(3)],
        out_specs=pl.BlockSpec((rows, BRANCH), lambda i: (i, 0)),
        out_shape=jax.ShapeDtypeStruct((N_CTX, BRANCH), F32),
        compiler_params=_cparams("parallel"),
        name="da_attn_ctx",
    )(lam_vec, ng, z_a, z_a, z_a, z_a)

    def lcol(c):
        return pl.BlockSpec((LAT_Q_TILE, BRANCH), lambda b, j: (N_CTX // LAT_Q_TILE + b * LAT_Q_TILES + j, c))
    tab = pl.BlockSpec((LAT_Q_TILE, BRANCH), lambda b, j: (j, 0))
    kvs = pl.BlockSpec((1, LAT_LEN + PAST_LEN, BRANCH), lambda b, j: (b, 0, 0))
    lat = pl.pallas_call(
        functools.partial(_da_attn_body, rope=True, lam_init=lam_init),
        grid=(N_LAT_SEQ, LAT_Q_TILES),
        in_specs=small + [lcol(0), tab, tab, tab, kvs, kvs, lcol(3)],
        out_specs=pl.BlockSpec((LAT_Q_TILE, BRANCH), lambda b, j: (b * LAT_Q_TILES + j, 0)),
        out_shape=jax.ShapeDtypeStruct((N_LAT, BRANCH), F32),
        compiler_params=_cparams("parallel", "parallel"),
        name="da_attn_lat",
    )(lam_vec, ng, z_a, *tabs, kv_lat[0], kv_lat[1], z_a)
    return ctx, lat


MLA_HEAD_PAD = 128
MLA_QW = MLA_HEADS * MLA_HEAD_PAD


def _mla_prep_body(cq_ref, ckv_ref, kr_ref, ck_t, sk_lo, sk_hi, qn_ref, kvn_ref, wq_ref, q_out, ckv_out, kr_out):
    cq = cq_ref[...]
    ms = jnp.sum(cq * cq, axis=-1, keepdims=True) * (1.0 / MLA_Q_RANK)
    qn = (cq * lax.rsqrt(ms + EPS) * qn_ref[...]).astype(BF16)
    heads = lambda t: jnp.concatenate([t[...]] * MLA_HEADS, axis=-1)
    q = _rope(_dot(qn, wq_ref[...]), heads(ck_t), heads(sk_lo), heads(sk_hi))
    q_out[...] = (q * ((MLA_NOPE + MLA_ROPE) ** -0.5 * LOG2E)).astype(BF16)
    ckv = ckv_ref[...]
    ckv_out[...] = ckv * lax.rsqrt(jnp.mean(ckv * ckv, axis=-1, keepdims=True) + EPS) * kvn_ref[...]
    kr_out[...] = _rope(kr_ref[...], ck_t[...], sk_lo[...], sk_hi[...])


def _mla_prep(z_d, tabs, q_norm_pad, kv_norm, wq):
    ctx_tiles, lat_tiles = N_CTX // BIG_TILE, LAT_LEN // BIG_TILE

    def tab(w):
        return pl.BlockSpec((BIG_TILE, w), lambda i: (jnp.where(i < ctx_tiles, lat_tiles, (i - ctx_tiles) % lat_tiles), 0))

    def col(w, c):
        return pl.BlockSpec((BIG_TILE, w), lambda i: (i, c))

    def const(shape):
        return pl.BlockSpec(shape, lambda i: (0, 0))
    return pl.pallas_call(
        _mla_prep_body,
        grid=(N_TOK // BIG_TILE,),
        in_specs=[col(256, 0), col(128, 2), col(128, 3),
                  tab(128), tab(128), tab(128),
                  const((1, 256)), const((1, 128)), const((256, MLA_QW))],
        out_specs=[col(MLA_QW, 0), col(128, 0), col(128, 0)],
        out_shape=[jax.ShapeDtypeStruct((N_TOK, MLA_QW), BF16),
                   jax.ShapeDtypeStruct((N_TOK, 128), F32),
                   jax.ShapeDtypeStruct((N_TOK, 128), F32)],
        compiler_params=_cparams("parallel"),
        name="mla_prep",
    )(z_d, z_d, z_d, *tabs, q_norm_pad, kv_norm, wq)


def _mla_kv_math(ckv, kr, wk_ref, wv_ref, k_out, v_out):
    c = ckv.astype(BF16)
    k_out[...] = (_dot(c, wk_ref[...]) + jnp.concatenate([kr] * MLA_HEADS, axis=-1)).astype(BF16).reshape(k_out.shape)
    v_out[...] = _dot(c, wv_ref[...]).astype(BF16).reshape(v_out.shape)


def _mla_kv_ctx_body(ckv_ref, kr_ref, wk_ref, wv_ref, k_out, v_out):
    _mla_kv_math(ckv_ref[...], kr_ref[...], wk_ref, wv_ref, k_out, v_out)


def _mla_kv_lat_body(ckv_ref, kr_ref, cckv_ref, ckr_ref, wk_ref, wv_ref, k_out, v_out):
    j = pl.program_id(1)

    @pl.when(j < LAT_TILES)
    def _():
        _mla_kv_math(ckv_ref[...], kr_ref[...], wk_ref, wv_ref, k_out, v_out)

    @pl.when(j == LAT_TILES)
    def _():
        _mla_kv_math(cckv_ref[0], ckr_ref[0], wk_ref, wv_ref, k_out, v_out)


def _mla_kv(ckv, kr, cache_ckv, cache_kr, wk, wv):
    weights = [pl.BlockSpec((128, MLA_QW), lambda *_: (0, 0)), pl.BlockSpec((128, BRANCH), lambda *_: (0, 0))]
    k_ctx, v_ctx = pl.pallas_call(
        _mla_kv_ctx_body,
        grid=(N_CTX // BIG_TILE,),
        in_specs=[pl.BlockSpec((BIG_TILE, 128), lambda i: (i, 0)), pl.BlockSpec((BIG_TILE, 128), lambda i: (i, 0))] + weights,
        out_specs=[pl.BlockSpec((BIG_TILE, MLA_QW), lambda i: (i, 0)),
                   pl.BlockSpec((BIG_TILE, BRANCH), lambda i: (i, 0))],
        out_shape=[jax.ShapeDtypeStruct((N_CTX, MLA_QW), BF16), jax.ShapeDtypeStruct((N_CTX, BRANCH), BF16)],
        compiler_params=_cparams("parallel"),
        name="mla_kv_ctx",
    )(ckv, kr, wk, wv)
    rows = pl.BlockSpec((ROW_TILE, 128), lambda b, j: (CTX_TILES + b * LAT_TILES + jnp.minimum(j, LAT_TILES - 1), 0))
    cache = pl.BlockSpec((1, PAST_LEN, 128), lambda b, j: (b, 0, 0))
    lk = LAT_LEN + PAST_LEN
    k_lat, v_lat = pl.pallas_call(
        _mla_kv_lat_body,
        grid=(N_LAT_SEQ, LAT_TILES + 1),
        in_specs=[rows, rows, cache, cache] + weights,
        out_specs=[pl.BlockSpec((1, ROW_TILE, MLA_QW), lambda b, j: (b, j, 0)),
                   pl.BlockSpec((1, ROW_TILE, BRANCH), lambda b, j: (b, j, 0))],
        out_shape=[jax.ShapeDtypeStruct((N_LAT_SEQ, lk, MLA_QW), BF16), jax.ShapeDtypeStruct((N_LAT_SEQ, lk, BRANCH), BF16)],
        compiler_params=_cparams("parallel", "parallel"),
        name="mla_kv_lat",
    )(ckv, kr, cache_ckv, cache_kr, wk, wv)
    return k_ctx, v_ctx, k_lat, v_lat


def _mla_attn_body(q_ref, k_ref, v_ref, g_ref, o_ref, *, ctx):
    if ctx:
        for t in range(q_ref.shape[0] // CTX_LEN):
            r = slice(t * CTX_LEN, (t + 1) * CTX_LEN)
            o_ref[r, :] = _mla_attn_tile(q_ref[r, :], k_ref[r, :], v_ref[r, :], g_ref[r, :])
    else:
        o_ref[...] = _mla_attn_tile(q_ref[...], k_ref[0], v_ref[0], g_ref[...])


def _mla_attn_tile(q, k, v, g):
    lane = lax.broadcasted_iota(jnp.int32, (1, BRANCH), 1)
    acc = jnp.zeros((q.shape[0], BRANCH), F32)
    for h in range(MLA_HEADS):
        sl = slice(h * MLA_HEAD_PAD, (h + 1) * MLA_HEAD_PAD)
        e, l = _exp2_rows(_dot_nt(q[:, sl], k[:, sl]))
        acc = jnp.where(lane // 64 == h, _dot(e.astype(BF16), v) * (1.0 / l), acc)
    return acc * _silu(g)


def _mla_attention(z_d, q, k_ctx, v_ctx, k_lat, v_lat):
    rows = CTX_SEQ_PER_STEP * CTX_LEN
    ctx = pl.pallas_call(
        functools.partial(_mla_attn_body, ctx=True),
        grid=(N_CTX // rows,),
        in_specs=[pl.BlockSpec((rows, MLA_QW), lambda i: (i, 0)),
                  pl.BlockSpec((rows, MLA_QW), lambda i: (i, 0)),
                  pl.BlockSpec((rows, BRANCH), lambda i: (i, 0)),
                  pl.BlockSpec((rows, BRANCH), lambda i: (i, 2))],
        out_specs=pl.BlockSpec((rows, BRANCH), lambda i: (i, 0)),
        out_shape=jax.ShapeDtypeStruct((N_CTX, BRANCH), F32),
        compiler_params=_cparams("parallel"),
        name="mla_attn_ctx",
    )(q, k_ctx, v_ctx, z_d)
    lk = LAT_LEN + PAST_LEN
    lat = pl.pallas_call(
        functools.partial(_mla_attn_body, ctx=False),
        grid=(N_LAT_SEQ, LAT_Q_TILES),
        in_specs=[pl.BlockSpec((LAT_Q_TILE, MLA_QW), lambda b, j: (N_CTX // LAT_Q_TILE + b * LAT_Q_TILES + j, 0)),
                  pl.BlockSpec((1, lk, MLA_QW), lambda b, j: (b, 0, 0)),
                  pl.BlockSpec((1, lk, BRANCH), lambda b, j: (b, 0, 0)),
                  pl.BlockSpec((LAT_Q_TILE, BRANCH), lambda b, j: (N_CTX // LAT_Q_TILE + b * LAT_Q_TILES + j, 2))],
        out_specs=pl.BlockSpec((LAT_Q_TILE, BRANCH), lambda b, j: (b * LAT_Q_TILES + j, 0)),
        out_shape=jax.ShapeDtypeStruct((N_LAT, BRANCH), F32),
        compiler_params=_cparams("parallel", "parallel"),
        name="mla_attn_lat",
    )(q, k_lat, v_lat, z_d)
    return ctx, lat


S5_TAP = S5_CHUNK * S5_CH
S5_NCHUNK = N_TOK // S5_CHUNK
S5_CTX_CH = N_CTX // S5_CHUNK
S5_CTX_SEQ_CH = CTX_LEN // S5_CHUNK
S5_LAT_SEQ_CH = LAT_LEN // S5_CHUNK
S5_SCAN_STEPS = S5_LAT_SEQ_CH.bit_length() - 1


def _s5_body(x_ref, mt_ref, bst_ref, cot_ref, a_ref, h0_ref, y_ref, fin_ref):
    x = x_ref[...].reshape(S5_TAP, S5_NCHUNK)
    y = _dot(mt_ref[0, 0], x)
    s = _dot(bst_ref[0, 0], x)
    lane = lax.broadcasted_iota(jnp.int32, (1, S5_NCHUNK), 1)
    is_lat = lane >= S5_CTX_CH
    pos_f = jnp.where(is_lat, (lane - S5_CTX_CH) & (S5_LAT_SEQ_CH - 1), lane & (S5_CTX_SEQ_CH - 1))
    pos_b = jnp.where(is_lat, S5_LAT_SEQ_CH - 1, S5_CTX_SEQ_CH - 1) - pos_f
    hin = []
    for d in range(2):
        n = S5_STATE
        sre, sim = s[2 * d * n:(2 * d + 1) * n], s[(2 * d + 1) * n:(2 * d + 2) * n]
        are = jnp.concatenate([a_ref[0, 0, 2 * d]] * (S5_NCHUNK // 128), axis=-1)
        aim = jnp.concatenate([a_ref[0, 0, 2 * d + 1]] * (S5_NCHUNK // 128), axis=-1)
        pos = pos_f if d == 0 else pos_b
        h0r, h0i = jnp.zeros_like(sre), jnp.zeros_like(sre)
        for b in range(N_LAT_SEQ):
            first = S5_CTX_CH + b * S5_LAT_SEQ_CH + (0 if d == 0 else S5_LAT_SEQ_CH - 1)
            h0r = jnp.where(lane == first, h0_ref[0, 2 * d][:, b:b + 1], h0r)
            h0i = jnp.where(lane == first, h0_ref[0, 2 * d + 1][:, b:b + 1], h0i)
        xr = sre + are * h0r - aim * h0i
        xi = sim + are * h0i + aim * h0r
        pr, pi = are, aim
        for j in range(S5_SCAN_STEPS):
            sh = 1 << j
            shift = sh if d == 0 else S5_NCHUNK - sh
            rr, ri = pltpu.roll(xr, shift, 1), pltpu.roll(xi, shift, 1)
            ok = pos >= sh
            xr, xi = (xr + jnp.where(ok, pr * rr - pi * ri, 0.0), xi + jnp.where(ok, pr * ri + pi * rr, 0.0))
            pr, pi = pr * pr - pi * pi, 2.0 * pr * pi
        last = lax.broadcasted_iota(jnp.int32, (1, 128), 1) * S5_CTX_SEQ_CH + (S5_CTX_SEQ_CH - 1 if d == 0 else 0)
        pick = jnp.where(lax.broadcasted_iota(jnp.int32, (S5_CTX_CH, 1), 0) == last, 1.0, 0.0)
        fin_ref[0, 2 * d] = _dot_sel(xr[:, :S5_CTX_CH], pick)
        fin_ref[0, 2 * d + 1] = _dot_sel(xi[:, :S5_CTX_CH], pick)
        one = 1 if d == 0 else S5_NCHUNK - 1
        hin.append(jnp.where(pos >= 1, pltpu.roll(xr, one, 1), h0r))
        hin.append(jnp.where(pos >= 1, pltpu.roll(xi, one, 1), h0i))
    y = y + _dot(cot_ref[0, 0], jnp.concatenate(hin, axis=0).astype(BF16))
    y_ref[...] = y.reshape(S5_CHUNK, S5_CH, S5_NCHUNK)


def _s5_scan(x_all, mt, bst, cot, a16, h0, l):
    g = S5_GROUPS
    sq = pl.BlockSpec((1, 1, S5_TAP, S5_TAP), lambda i: (l, i, 0, 0))
    st = pl.BlockSpec((1, 4, S5_STATE, 128), lambda i: (i, 0, 0, 0))
    return pl.pallas_call(
        _s5_body,
        grid=(g,),
        in_specs=[pl.BlockSpec((S5_CHUNK, S5_CH, S5_NCHUNK), lambda i: (0, i, 0)), sq, sq, sq,
                  pl.BlockSpec((1, 1, 4, S5_STATE, 128), lambda i: (l, i, 0, 0, 0)), st],
        out_specs=[pl.BlockSpec((S5_CHUNK, S5_CH, S5_NCHUNK), lambda i: (0, i, 0)),
                   pl.BlockSpec((1, 4, S5_STATE, 128), lambda i: (i, 0, 0, 0))],
        out_shape=[jax.ShapeDtypeStruct((S5_CHUNK, BRANCH, S5_NCHUNK), F32),
                   jax.ShapeDtypeStruct((g, 4, S5_STATE, 128), F32)],
        compiler_params=_cparams("parallel"),
        name="s5_scan",
    )(x_all, mt, bst, cot, a16, h0)


def _s5_out_body(y_ref, u_ref, g_ref, d_ref, w_ref, o_ref):
    y = u_ref[...] * d_ref[...] + y_ref[...]
    ge = 0.5 * y * (1.0 + jnp.tanh(0.7978845608028654 * (y + 0.044715 * (y * y * y))))
    gl = _dot(ge.astype(BF16), w_ref[...])
    o_ref[...] = gl[:, :BRANCH] * (1.0 / (1.0 + jnp.exp(-gl[:, BRANCH:]))) * _silu(g_ref[...])


def _s5_out(y_ssm, z_b, d_skip, w_glu):
    tile = 2 * BIG_TILE

    def col(c):
        return pl.BlockSpec((tile, BRANCH), lambda i: (i, c))
    return pl.pallas_call(
        _s5_out_body,
        grid=(N_TOK // tile,),
        in_specs=[col(0), col(0), col(1),
                  pl.BlockSpec((1, BRANCH), lambda i: (0, 0)),
                  pl.BlockSpec((BRANCH, 2 * BRANCH), lambda i: (0, 0))],
        out_specs=col(0),
        out_shape=jax.ShapeDtypeStruct((N_TOK, BRANCH), F32),
        compiler_params=_cparams("parallel"),
        name="s5_out",
    )(y_ssm, z_b, z_b, d_skip, w_glu)


HG_CHUNK = ROW_TILE
HG_W = 2 * HG_HEADS * HG_DK
HG_HEAD_W = 2 * HG_DK
HG_LAT_CHUNKS = LAT_LEN // HG_CHUNK
HG_CHUNKS = N_TOK // HG_CHUNK


def _hg_gates(z, lb):
    e = jnp.exp(-jnp.abs(z))
    r = 1.0 / (1.0 + e)
    sig_pos = jnp.where(z >= 0, r, e * r)
    sig_neg = jnp.where(z >= 0, e * r, r)
    return lb + (1.0 - lb) * sig_pos, (1.0 - lb) * sig_neg


def _bcast_row(x, period, r):
    n, w = x.shape
    if period >= 8:
        x3 = x.reshape(n // period, period, w)
        return jnp.broadcast_to(x3[:, r:r + 1, :], x3.shape).reshape(n, w)
    x3 = x.reshape(n // 8, 8, w)
    sub = lax.broadcasted_iota(jnp.int32, (1, 8, 1), 1)
    out = None
    for j in range(8 // period):
        b = jnp.broadcast_to(x3[:, j * period + r:j * period + r + 1, :], x3.shape)
        out = b if out is None else jnp.where(sub >= j * period, b, out)
    return out.reshape(n, w)


def _hg_scans(f, isb):
    n = f.shape[0]
    row = lax.broadcasted_iota(jnp.int32, (n, 1), 0)
    p, r = f, jnp.ones_like(f)
    levels = []
    h, sh = 1, 0
    while h < n:
        levels.append((h, sh, p, r))
        up = (row >> sh) & 1
        tot_p = jnp.where(isb == 1, _bcast_row(p, 2 * h, h), _bcast_row(p, 2 * h, h - 1))
        tot_r = jnp.where(isb == 1, _bcast_row(p, 2 * h, 0), _bcast_row(p, 2 * h, 2 * h - 1))
        p = p * jnp.where(up != isb, tot_p, 1.0)
        r = r * jnp.where(up == isb, tot_r, 1.0)
        h, sh = 2 * h, sh + 1
    return levels, p, r


def _hg_state_body(zf_ref, zb_ref, vf_ref, vb_ref, lb_ref, s0_ref, sf_out, sb_out, s_scr):
    i = pl.program_id(0)

    @pl.when(i % HG_LAT_CHUNKS == 0)
    def _():
        s_scr[...] = s0_ref[0]

    sf_out[0] = s_scr[:, 0:HG_DK, :]
    sb_out[0] = s_scr[:, HG_DK:, :]
    lane5 = lax.broadcasted_iota(jnp.int32, (1, HG_W), 1)
    isb = (lane5 >> 6) & 1
    z = jnp.where(isb == 1, zb_ref[...], zf_ref[...])
    f, k = _hg_gates(z, lb_ref[...])
    r, ptot = _hg_chunk_decay(f, isb)
    kt = k * r
    lane = lax.broadcasted_iota(jnp.int32, (1, BRANCH), 1)
    vf = vf_ref[...]
    vb = vb_ref[...]
    for hd in range(HG_HEADS):
        sl = slice(hd * HG_HEAD_W, (hd + 1) * HG_HEAD_W)
        kth = kt[:, sl].T.astype(BF16)
        hm = (lane >> 6) == hd
        d_f = _dot(kth, jnp.where(hm, vf, 0.0).astype(BF16))
        d_b = _dot(kth, jnp.where(hm, vb, 0.0).astype(BF16))
        ds = jnp.concatenate([d_f[:HG_DK], d_b[HG_DK:]], axis=0)
        pcol = jnp.broadcast_to(ptot[:, sl], (HG_HEAD_W, HG_HEAD_W)).T[:, 0:1]
        s_scr[hd] = s_scr[hd] * pcol + ds


def _hg_chunk_decay(f, isb):
    n = f.shape[0]
    row = lax.broadcasted_iota(jnp.int32, (n, 1), 0)
    dist = jnp.where(isb == 1, row, n - 1 - row)
    x = f
    sh = 1
    while sh < n:
        src = jnp.where(isb == 1, pltpu.roll(x, sh, 0), pltpu.roll(x, n - sh, 0))
        x = x * jnp.where(dist >= sh, src, 1.0)
        sh *= 2
    total = jnp.where(isb == 1, x[n - 1:n], x[0:1])
    nxt = jnp.where(isb == 1, pltpu.roll(x, 1, 0), pltpu.roll(x, n - 1, 0))
    return jnp.where(dist >= 1, nxt, 1.0), total


HG_LAT_STEPS = N_LAT_SEQ * HG_LAT_CHUNKS


def _hg_lat_rev(i):
    return (i // HG_LAT_CHUNKS) * HG_LAT_CHUNKS + (HG_LAT_CHUNKS - 1 - i % HG_LAT_CHUNKS)


def _hg_states(z_c, lb, s0):
    first = N_CTX_SEQ
    zz_f = pl.BlockSpec((HG_CHUNK, HG_W), lambda i: (first + i, 1))
    zz_b = pl.BlockSpec((HG_CHUNK, HG_W), lambda i: (first + _hg_lat_rev(i), 1))
    v_f = pl.BlockSpec((HG_CHUNK, BRANCH), lambda i: (first + i, 4))
    v_b = pl.BlockSpec((HG_CHUNK, BRANCH), lambda i: (first + _hg_lat_rev(i), 4))
    st = (HG_HEADS, HG_HEAD_W, BRANCH)
    half = (HG_HEADS, HG_DK, BRANCH)
    return pl.pallas_call(
        _hg_state_body,
        grid=(HG_LAT_STEPS,),
        in_specs=[zz_f, zz_b, v_f, v_b,
                  pl.BlockSpec((1, HG_W), lambda i: (0, 0)),
                  pl.BlockSpec((1,) + st, lambda i: (i // HG_LAT_CHUNKS, 0, 0, 0))],
        out_specs=[pl.BlockSpec((1,) + half, lambda i: (i, 0, 0, 0)),
                   pl.BlockSpec((1,) + half, lambda i: (_hg_lat_rev(i), 0, 0, 0))],
        out_shape=[jax.ShapeDtypeStruct((HG_LAT_STEPS,) + half, F32),
                   jax.ShapeDtypeStruct((HG_LAT_STEPS,) + half, F32)],
        scratch_shapes=[pltpu.VMEM(st, F32)],
        compiler_params=_cparams("arbitrary"),
        name="hg_states",
    )(z_c, z_c, z_c, z_c, lb, s0)


def _hg_main_body(qq_ref, zz_ref, v_ref, g_ref, sf_ref, sb_ref, lb_ref, ng_ref, o_ref, fin_ref):
    n = HG_CHUNK
    i = pl.program_id(0)
    qq = qq_ref[...]
    lane5 = lax.broadcasted_iota(jnp.int32, (1, HG_W), 1)
    isb = (lane5 >> 6) & 1
    f, k = _hg_gates(zz_ref[...], lb_ref[...])
    levels, pfull, rfull = _hg_scans(f, isb)
    row = lax.broadcasted_iota(jnp.int32, (n, 1), 0)
    col = lax.broadcasted_iota(jnp.int32, (1, n), 1)
    qq_b, k_b, zero_b = qq.astype(BF16), k.astype(BF16), jnp.zeros((), BF16)
    ops = [(qq_b, k_b, row == col)]
    for h, sh, p, r in levels:
        up = (row >> sh) & 1
        qt = jnp.where(up != isb, qq_b * p.astype(BF16), zero_b)
        kt = jnp.where(up == isb, k_b * r.astype(BF16), zero_b)
        ops.append((qt, kt, (row >> (sh + 1)) == (col >> (sh + 1))))
    qc = (qq * pfull).astype(BF16)
    v = v_ref[...]
    vb = v.astype(BF16)
    lane = lax.broadcasted_iota(jnp.int32, (1, BRANCH), 1)
    latent = i >= N_CTX_SEQ
    acc = jnp.zeros((n, BRANCH), F32)
    for hd in range(HG_HEADS):
        sl = slice(hd * HG_HEAD_W, (hd + 1) * HG_HEAD_W)
        a = jnp.zeros((n, n), F32)
        for qt, kt, mask in ops:
            a = a + jnp.where(mask, _dot_nt(qt[:, sl], kt[:, sl]), 0.0)
        s_in = jnp.concatenate([sf_ref[0, hd], sb_ref[0, hd]], axis=0)
        s_in = jnp.where(latent, s_in, 0.0).astype(BF16)
        o_h = _dot(a.astype(BF16), vb) + _dot(qc[:, sl], s_in)
        acc = jnp.where((lane >> 6) == hd, o_h, acc)
    sq = acc * acc
    ms = jnp.zeros((n, BRANCH), F32)
    for hd in range(HG_HEADS):
        hm = (lane >> 6) == hd
        ms = jnp.where(hm, jnp.sum(jnp.where(hm, sq, 0.0), axis=-1, keepdims=True), ms)
    o_ref[...] = acc * lax.rsqrt(ms * (1.0 / HG_DK) + EPS) * ng_ref[...] * _silu(g_ref[...])

    @pl.when(i < N_CTX_SEQ)
    def _():
        kt_full = k * rfull
        for hd in range(HG_HEADS):
            kth = kt_full[:, hd * HG_HEAD_W:(hd + 1) * HG_HEAD_W].T.astype(BF16)
            ds = _dot(kth, jnp.where((lane >> 6) == hd, v, 0.0).astype(BF16))
            fin_ref[0, hd] = ds[:, hd * HG_DK:(hd + 1) * HG_DK]


def _hg_main(z_c, s_f, s_b, lb, norm_g):
    half = (1, HG_HEADS, HG_DK, BRANCH)
    lat = lambda i: (jnp.maximum(i - N_CTX_SEQ, 0), 0, 0, 0)
    fin = (HG_HEADS, HG_HEAD_W, HG_DK)
    return pl.pallas_call(
        _hg_main_body,
        grid=(HG_CHUNKS,),
        in_specs=[pl.BlockSpec((HG_CHUNK, HG_W), lambda i: (i, 0)),
                  pl.BlockSpec((HG_CHUNK, HG_W), lambda i: (i, 1)),
                  pl.BlockSpec((HG_CHUNK, BRANCH), lambda i: (i, 4)),
                  pl.BlockSpec((HG_CHUNK, BRANCH), lambda i: (i, 5)),
                  pl.BlockSpec(half, lat),
                  pl.BlockSpec(half, lat),
                  pl.BlockSpec((1, HG_W), lambda i: (0, 0)),
                  pl.BlockSpec((1, BRANCH), lambda i: (0, 0))],
        out_specs=[pl.BlockSpec((HG_CHUNK, BRANCH), lambda i: (i, 0)),
                   pl.BlockSpec((1,) + fin, lambda i: (jnp.minimum(i, N_CTX_SEQ - 1), 0, 0, 0))],
        out_shape=[jax.ShapeDtypeStruct((N_TOK, BRANCH), F32),
                   jax.ShapeDtypeStruct((N_CTX_SEQ,) + fin, F32)],
        compiler_params=_cparams("arbitrary"),
        name="hg_main",
    )(z_c, z_c, z_c, z_c, s_f, s_b, lb, norm_g)


def _take_cols(w, plan):
    idx = np.concatenate([p[0] for p in plan]).astype(np.int32)
    sign = np.concatenate([np.broadcast_to(p[1], p[0].shape) for p in plan]).astype(np.float32)
    return jnp.take(w, jnp.asarray(idx), axis=-1) * jnp.asarray(sign)


def _zeros(n):
    return (np.zeros(n, np.int64), 0.0)


_IN_OFF = {}
_off = 0
for _name, _n in (("da_q", 256), ("da_k", 256), ("da_v", 256), ("da_g", 256), ("s5_u", 256), ("s5_g", 256),
                  ("hg_q", 256), ("hg_ff", 256), ("hg_fb", 256), ("hg_i", 256), ("hg_g", 256),
                  ("mla_cq", MLA_Q_RANK), ("mla_ckv", MLA_KV_RANK), ("mla_kr", MLA_ROPE), ("mla_g", 256)):
    _IN_OFF[_name] = np.arange(_off, _off + _n)
    _off += _n


def _rope_tables():
    t = np.arange(LAT_LEN)
    pos = np.stack([t // GRID_W, t % GRID_W], axis=1).astype(np.float32)
    inv_freq = (np.float32(ROPE_BASE) ** (-np.arange(8, dtype=np.float32) / np.float32(8))).astype(np.float32)
    r = np.arange(MLA_ROPE)
    ang = (pos[:, r // 16] * inv_freq[r % 8][None, :]).astype(np.float64)
    cos32, sin32 = np.cos(ang).astype(np.float32), np.sin(ang).astype(np.float32)
    lo = (np.arange(MLA_ROPE) % 16 < 8)[None, :]
    sin_lo32, sin_hi32 = np.where(lo, -sin32, 0.0).astype(np.float32), np.where(lo, 0.0, sin32).astype(np.float32)
    da_tabs = tuple(np.tile(x, (1, 8)) for x in (cos32, sin_lo32, sin_hi32))

    def head(x, fill):
        h = np.concatenate([np.full((LAT_LEN, MLA_NOPE), fill, np.float32), x,
                            np.full((LAT_LEN, MLA_HEAD_PAD - MLA_NOPE - MLA_ROPE), fill, np.float32)], axis=1)
        return np.concatenate([h, np.full((BIG_TILE, MLA_HEAD_PAD), fill, np.float32)], axis=0)
    k_tabs = (head(cos32, 1.0), head(sin_lo32, 0.0), head(sin_hi32, 0.0))
    return tuple(jnp.asarray(x) for x in da_tabs), tuple(jnp.asarray(x) for x in k_tabs)


def _mla_weights(w_uq, w_ukv, q_norm):
    hd = MLA_NOPE + MLA_ROPE
    pad_tail = _zeros(MLA_HEAD_PAD - hd)
    q_plan, k_plan, v_plan = [], [], []
    for h in range(MLA_HEADS):
        nope, rope = np.arange(h * hd, h * hd + MLA_NOPE), np.arange(h * hd + MLA_NOPE, (h + 1) * hd)
        q_plan += [(nope, 1.0), (rope, 1.0), pad_tail]
        k_plan += [(np.arange(h * 2 * MLA_NOPE, h * 2 * MLA_NOPE + MLA_NOPE), 1.0), _zeros(MLA_HEAD_PAD - MLA_NOPE)]
        v_plan += [(np.arange(h * 2 * MLA_NOPE + MLA_NOPE, (h + 1) * 2 * MLA_NOPE), 1.0)]
    pad_rows = lambda x: jnp.pad(x, ((0, 256 - MLA_Q_RANK), (0, 0))).astype(BF16)
    qn = jnp.pad(q_norm, (0, 256 - MLA_Q_RANK)).reshape(1, 256)
    return (pad_rows(_take_cols(w_uq, q_plan)), _take_cols(w_ukv, k_plan).astype(BF16),
            _take_cols(w_ukv, v_plan).astype(BF16), qn)


def _split_bf16(a):
    hi = a.astype(BF16)
    return hi, (a - hi.astype(F32)).astype(BF16)


def _dot_sel(a, sel):
    hi, lo = _split_bf16(a)
    sel = sel.astype(BF16)
    return _dot(hi, sel) + _dot(lo, sel)


def _dot_x3(a, b):
    a_hi, a_lo = _split_bf16(a)
    b_hi, b_lo = _split_bf16(b)
    return _dot(a_hi, b_hi) + _dot(a_hi, b_lo) + _dot(a_lo, b_hi)


def _s5_table_body(xy_ref, bb_ref, c_ref, ct_ref, mt_ref, bst_ref, cot_ref, a_ref):
    n, t, ch = S5_STATE, S5_CHUNK, S5_CH
    wide = 2 * S5_TAP
    xy = xy_ref[0, 0]
    tau_i = lax.broadcasted_iota(jnp.int32, (1, 128), 1)
    tau = tau_i.astype(F32)
    sel_row = lax.broadcasted_iota(jnp.int32, (128, 1), 0)

    def lag(width):
        return lax.broadcasted_iota(jnp.int32, (1, width), 1) >> 4

    def onehot(cond):
        return jnp.where(cond, 1.0, 0.0).astype(F32)
    j = lag(wide)
    e_z = (onehot((j <= t - 1) & (sel_row == t - 1 - j)), onehot((j >= t - 1) & (j <= 2 * t - 2) & (sel_row == j - (t - 1))))
    jc = lag(S5_TAP)
    e_c = (onehot(sel_row == jc + 1), onehot(sel_row == t - jc))
    ch_row = lax.broadcasted_iota(jnp.int32, (ch, 1), 0)
    tile_w = onehot((lax.broadcasted_iota(jnp.int32, (1, wide), 1) & (ch - 1)) == ch_row)
    tile_n = onehot((lax.broadcasted_iota(jnp.int32, (1, S5_TAP), 1) & (ch - 1)) == ch_row)

    z, cot_rows, klong = [], [], None
    for d in range(2):
        x, y = xy[:, 2 * d:2 * d + 1], xy[:, 2 * d + 1:2 * d + 2]
        mag = jnp.exp(jnp.where(tau_i <= t, tau, 0.0) * x)
        ang = jnp.where(tau_i <= t, tau, 0.0) * y
        p_re = jnp.where(tau_i <= t, mag * jnp.cos(ang), 0.0)
        p_im = jnp.where(tau_i <= t, mag * jnp.sin(ang), 0.0)
        a_ref[0, 0, 2 * d] = jnp.broadcast_to(p_re[:, t:t + 1], (n, 128))
        a_ref[0, 0, 2 * d + 1] = jnp.broadcast_to(p_im[:, t:t + 1], (n, 128))
        pz_re, pz_im = _dot_sel(p_re, e_z[d]), _dot_sel(p_im, e_z[d])
        b_re, b_im = _dot_sel(bb_ref[0, 0, 2 * d], tile_w), _dot_sel(bb_ref[0, 0, 2 * d + 1], tile_w)
        z_re, z_im = pz_re * b_re - pz_im * b_im, pz_re * b_im + pz_im * b_re
        z += [z_re, z_im]
        part = _dot_x3(c_ref[0, 0, 2 * d], z_re) - _dot_x3(c_ref[0, 0, 2 * d + 1], z_im)
        klong = part if klong is None else klong + part
        pc_re, pc_im = _dot_sel(p_re, e_c[d]), _dot_sel(p_im, e_c[d])
        c_re, c_im = _dot_sel(ct_ref[0, 0, 2 * d], tile_n), _dot_sel(ct_ref[0, 0, 2 * d + 1], tile_n)
        cot_rows += [c_re * pc_re - c_im * pc_im, -(c_re * pc_im + c_im * pc_re)]
    for tt in range(t):
        off = (t - 1 - tt) * ch
        win = klong if off == 0 else pltpu.roll(klong, wide - off, 1)
        mt_ref[0, 0, tt * ch:(tt + 1) * ch, :] = win[:, :S5_TAP].astype(BF16)
    back = pltpu.roll(z[2], wide - (t - 1) * ch, 1), pltpu.roll(z[3], wide - (t - 1) * ch, 1)
    for k, rows in enumerate((z[0], z[1], back[0], back[1])):
        bst_ref[0, 0, k * n:(k + 1) * n, :] = rows[:, :S5_TAP].astype(BF16)
    cot_ref[0, 0] = jnp.concatenate(cot_rows, axis=0).T.astype(BF16)


def _s5_tables(a_re, a_im, log_dt, b_re, b_im, c_re, c_im):
    nl, g, n, ch = a_re.shape[0], S5_GROUPS, S5_STATE, S5_CH
    step = jnp.exp(log_dt)[..., None]
    mag = jnp.exp(a_re * step)
    ab_re, ab_im = mag * jnp.cos(a_im * step), mag * jnp.sin(a_im * step)
    den = a_re * a_re + a_im * a_im
    f_re = ((ab_re - 1.0) * a_re + ab_im * a_im) / den
    f_im = (ab_im * a_re - (ab_re - 1.0) * a_im) / den
    bb_re = f_re[..., None] * b_re - f_im[..., None] * b_im
    bb_im = f_re[..., None] * b_im + f_im[..., None] * b_re
    by_group = lambda x: jnp.moveaxis(x, 1, 2)
    pair = lambda re, im: jnp.stack([by_group(re), by_group(im)], axis=3).reshape((nl, g, 4) + re.shape[3:])
    xy = jnp.stack([by_group(a_re * step), by_group(a_im * step)], axis=3).reshape(nl, g, 4, n)
    xy = jnp.pad(jnp.swapaxes(xy, 2, 3), ((0, 0), (0, 0), (0, 0), (0, 4)))
    mat = pl.BlockSpec((1, 1, S5_TAP, S5_TAP), lambda l, i: (l, i, 0, 0))
    return pl.pallas_call(
        _s5_table_body,
        grid=(nl, g),
        in_specs=[pl.BlockSpec((1, 1, n, 8), lambda l, i: (l, i, 0, 0)),
                  pl.BlockSpec((1, 1, 4, n, ch), lambda l, i: (l, i, 0, 0, 0)),
                  pl.BlockSpec((1, 1, 4, ch, n), lambda l, i: (l, i, 0, 0, 0)),
                  pl.BlockSpec((1, 1, 4, n, ch), lambda l, i: (l, i, 0, 0, 0))],
        out_specs=[mat, mat, mat, pl.BlockSpec((1, 1, 4, n, 128), lambda l, i: (l, i, 0, 0, 0))],
        out_shape=[jax.ShapeDtypeStruct((nl, g, S5_TAP, S5_TAP), BF16)] * 3
        + [jax.ShapeDtypeStruct((nl, g, 4, n, 128), F32)],
        compiler_params=_cparams("parallel", "parallel"),
        name="s5_tables",
    )(xy, pair(bb_re, bb_im), pair(c_re, c_im), pair(jnp.swapaxes(c_re, -1, -2), jnp.swapaxes(c_im, -1, -2)))


def _s5_chunk_lanes(u):
    return u.reshape(S5_NCHUNK, S5_CHUNK, BRANCH).transpose(1, 2, 0)


def _s5_token_rows(y):
    return y.transpose(2, 0, 1).reshape(N_TOK, BRANCH)


def kernel(x_prompt, x_sample, cache_diff_k, cache_diff_v, state_s5, state_hgrn, cache_mla_ckv, cache_mla_krope, c, c_ctx, w_mod, b_mod, w_in, w_out, da_lambda, da_norm, s5_a_re, s5_a_im, s5_log_dt, s5_b_re, s5_b_im, s5_c_re, s5_c_im, s5_d, s5_w_glu, hg_lb, hg_norm, mla_q_norm, mla_w_uq, mla_kv_norm, mla_w_ukv, final_norm):
    lb_w = jax.nn.softmax(hg_lb.astype(F32), axis=0)
    lb_all = jnp.cumsum(lb_w, axis=0) - lb_w[0:1]
    c_rows = jnp.concatenate([c_ctx[None], c, jnp.zeros((8 - 1 - N_LAT_SEQ, D_MODEL), F32)], axis=0)
    mods = _modulation(c_rows, w_mod, b_mod)
    da_tabs, mla_tabs = _rope_tables()
    xs = (x_prompt.reshape(N_CTX, D_MODEL), x_sample.reshape(N_LAT, D_MODEL))
    new_k, new_v, new_s5, new_hg, new_ckv, new_kr = [], [], [], [], [], []
    s5_tabs = _s5_tables(s5_a_re, s5_a_im, s5_log_dt, s5_b_re, s5_b_im, s5_c_re, s5_c_im)
    w_all = _arrange_w_in(w_in)
    for l in range(DEPTH):
        mod = mods[l, :3].reshape(3, 3, D_MODEL)
        z_a, z_b, z_c, z_d, k_new, v_new, u_bf = _in_proj(xs, mod, w_all, l)

        lam_init = 0.8 - 0.6 * math.exp(-0.3 * l)
        kv_lat = _da_latent_kv(z_a, da_tabs,
                               cache_diff_k[:, l].reshape(N_LAT_SEQ, PAST_LEN, BRANCH),
                               cache_diff_v[:, l].reshape(N_LAT_SEQ, PAST_LEN, BRANCH))
        a_out = _da_attention(z_a, da_lambda[l], da_norm[l], lam_init, da_tabs, kv_lat)
        new_k.append(k_new.reshape(N_CTX_SEQ, CTX_LEN, DA_HEADS, 2 * DA_QK))
        new_v.append(v_new.reshape(N_CTX_SEQ, CTX_LEN, DA_HEADS, 2 * DA_QK))

        h0 = state_s5[:, l].transpose(2, 1, 4, 3, 0).reshape(S5_GROUPS, 4, S5_STATE, N_LAT_SEQ)
        h0 = jnp.pad(h0, ((0, 0), (0, 0), (0, 0), (0, 128 - N_LAT_SEQ)))
        y_all, fin = _s5_scan(_s5_chunk_lanes(u_bf), *s5_tabs, h0, l)
        b_out = _s5_out(_s5_token_rows(y_all), z_b, s5_d[l].reshape(1, BRANCH), s5_w_glu[l].astype(BF16))
        fin = fin[..., :N_CTX_SEQ].reshape(S5_GROUPS, 2, 2, S5_STATE, N_CTX_SEQ)
        new_s5.append(fin.transpose(4, 1, 0, 3, 2))

        lb = jnp.concatenate([lb_all[l, 0].reshape(HG_HEADS, HG_DK), lb_all[l, 1].reshape(HG_HEADS, HG_DK)],
                             axis=-1).reshape(1, HG_W)
        head_eye = jnp.eye(HG_HEADS, dtype=F32)
        s0 = state_hgrn[:, l].transpose(0, 2, 1, 3, 4).reshape(N_LAT_SEQ, HG_HEADS, HG_HEAD_W, 1, HG_DK)
        s0 = (s0 * head_eye[None, :, None, :, None]).reshape(N_LAT_SEQ, HG_HEADS, HG_HEAD_W, BRANCH)
        s_f, s_b = _hg_states(z_c, lb, s0)
        c_out, s_fin = _hg_main(z_c, s_f, s_b, lb, jnp.tile(hg_norm[l].reshape(1, HG_DK), (1, HG_HEADS)))
        new_hg.append(s_fin.reshape(N_CTX_SEQ, HG_HEADS, 2, HG_DK, HG_DK).transpose(0, 2, 1, 3, 4))

        wq, wk, wv, qn = _mla_weights(mla_w_uq[l], mla_w_ukv[l], mla_q_norm[l])
        q, ckv_n, kr = _mla_prep(z_d, mla_tabs, qn, mla_kv_norm[l].reshape(1, MLA_KV_RANK), wq)
        kr_cache = jnp.pad(cache_mla_krope[:, l], ((0, 0), (0, 0), (MLA_NOPE, 128 - MLA_NOPE - MLA_ROPE)))
        k_ctx, v_ctx, k_lat, v_lat = _mla_kv(ckv_n, kr, cache_mla_ckv[:, l], kr_cache, wk, wv)
        d_out = _mla_attention(z_d, q, k_ctx, v_ctx, k_lat, v_lat)
        new_ckv.append(ckv_n[:N_CTX].reshape(N_CTX_SEQ, CTX_LEN, MLA_KV_RANK))
        new_kr.append(kr[:N_CTX, MLA_NOPE:MLA_NOPE + MLA_ROPE].reshape(N_CTX_SEQ, CTX_LEN, MLA_ROPE))

        xs = _out_proj(a_out, b_out, c_out, d_out, xs, mod, w_out, l,
                       final_norm.reshape(1, D_MODEL), final=(l == DEPTH - 1))
        xs = tuple(xs) if l == DEPTH - 1 else (xs,)
    y_prompt = xs[0].reshape(N_CTX_SEQ, CTX_LEN, D_MODEL)
    y_sample = xs[1].reshape(N_LAT_SEQ, LAT_LEN, D_MODEL)
    st = lambda parts: jnp.stack(parts, axis=1)
    return (y_prompt, y_sample, st(new_k), st(new_v), st(new_s5), st(new_hg), st(new_ckv), st(new_kr))
```

```python
import functools
import math

import numpy as np

import jax
import jax.numpy as jnp
from jax import lax
from jax.experimental import pallas as pl
from jax.experimental.pallas import tpu as pltpu

F32 = jnp.float32
BF16 = jnp.bfloat16

D_MODEL = 1024
DEPTH = 2
N_CTX_SEQ = 16
CTX_LEN = 256
N_LAT_SEQ = 2
LAT_LEN = 2048
PAST_LEN = 256
GRID_W = 64
N_CTX = N_CTX_SEQ * CTX_LEN
N_LAT = N_LAT_SEQ * LAT_LEN
N_TOK = N_CTX + N_LAT
BRANCH = 256
EPS = 1e-6
ROPE_BASE = 10000.0
ROW_TILE = 256
LAT_TILES = LAT_LEN // ROW_TILE
N_TILES = N_TOK // ROW_TILE
CTX_TILES = N_CTX // ROW_TILE
VMEM_LIMIT = 48 * 1024 * 1024
IN_PROJ_VMEM_LIMIT = 56 * 1024 * 1024
LAT_Q_TILE = 512
LAT_Q_TILES = LAT_LEN // LAT_Q_TILE
BIG_TILE = 512
CTX_SEQ_PER_STEP = 2

DA_HEADS = 4
DA_QK = 32
MLA_HEADS = 4
MLA_NOPE = 64
MLA_ROPE = 32
MLA_Q_RANK = 192
MLA_KV_RANK = 128
S5_GROUPS = 16
S5_CH = 16
S5_STATE = 64
S5_CHUNK = 16
HG_HEADS = 4
HG_DK = 64

W_A = 1024
W_B = 512
W_C = 1536
W_D = 768
W_ALL = W_A + W_B + W_C + W_D


def _cparams(*sem):
    return pltpu.CompilerParams(dimension_semantics=sem, vmem_limit_bytes=VMEM_LIMIT)


def _tile_seq(i, tile=ROW_TILE):
    return jnp.where(i < N_CTX // tile, 0, 1 + (i - N_CTX // tile) // (LAT_LEN // tile))


def _silu(x):
    return x * (1.0 / (1.0 + jnp.exp(-x)))


def _dot(a, b):
    return jnp.dot(a, b, preferred_element_type=F32)


def _dot_nt(a, b):
    return lax.dot_general(a, b, (((1,), (1,)), ((), ())), preferred_element_type=F32)


def _mod_body(c_ref, w_ref, b_ref, o_ref):
    c = _silu(c_ref[...]).astype(BF16)
    o_ref[0] = _dot(c, w_ref[0].astype(BF16)) + b_ref[0]


def _modulation(c_rows, w_mod, b_mod):
    tn = 768
    return pl.pallas_call(
        _mod_body,
        grid=(DEPTH, 3 * D_MODEL // tn),
        in_specs=[pl.BlockSpec((8, D_MODEL), lambda l, j: (0, 0)),
                  pl.BlockSpec((1, D_MODEL, tn), lambda l, j: (l, 0, j)),
                  pl.BlockSpec((1, 1, tn), lambda l, j: (l, 0, j))],
        out_specs=pl.BlockSpec((1, 8, tn), lambda l, j: (l, 0, j)),
        out_shape=jax.ShapeDtypeStruct((DEPTH, 8, 3 * D_MODEL), F32),
        compiler_params=_cparams("parallel", "parallel"),
        name="modulation",
    )(c_rows, w_mod, b_mod.reshape(DEPTH, 1, 3 * D_MODEL))


def _split_rows(i, ctx_ref, lat_ref):
    return jnp.where(i < N_CTX // ctx_ref.shape[0], ctx_ref[...], lat_ref[...])


def _ctx_tile_spec(w, tile=ROW_TILE):
    return pl.BlockSpec((tile, w), lambda i: (jnp.minimum(i, N_CTX // tile - 1), 0))


def _lat_tile_spec(w, tile=ROW_TILE):
    return pl.BlockSpec((tile, w), lambda i: (jnp.maximum(i - N_CTX // tile, 0), 0))


def _in_proj_body(*refs, split):
    if split:
        xc_ref, xl_ref, mod_ref, w_ref, oa, ob, oc, od, ok, ov, ou = refs
        x = _split_rows(pl.program_id(0), xc_ref, xl_ref)
    else:
        x_ref, mod_ref, w_ref, oa, ob, oc, od, ok, ov, ou = refs
        x = x_ref[...]
    xn = x * lax.rsqrt(jnp.mean(x * x, axis=-1, keepdims=True) + EPS)
    mod = mod_ref[0]
    h = (xn * (1.0 + mod[1:2]) + mod[0:1]).astype(BF16)
    off = 0
    for o in (oa, ob, oc, od):
        w = o.shape[-1]
        o[...] = _dot(h, w_ref[0, :, off:off + w])
        off += w
    ou[...] = ob[:, :BRANCH].astype(BF16)

    @pl.when(pl.program_id(0) < N_CTX // BIG_TILE)
    def _():
        for t in range(BIG_TILE // CTX_LEN):
            r = slice(t * CTX_LEN, (t + 1) * CTX_LEN)
            ok[t] = oa[r, BRANCH:2 * BRANCH].T
            ov[t] = oa[r, 2 * BRANCH:3 * BRANCH].T


def _in_proj(xs, mod, w_all, l):
    widths = (W_A, W_B, W_C, W_D)
    split = len(xs) == 2
    x_specs = ([_ctx_tile_spec(D_MODEL, BIG_TILE), _lat_tile_spec(D_MODEL, BIG_TILE)] if split
               else [pl.BlockSpec((BIG_TILE, D_MODEL), lambda i: (i, 0))])
    return pl.pallas_call(
        functools.partial(_in_proj_body, split=split),
        grid=(N_TOK // BIG_TILE,),
        in_specs=x_specs + [pl.BlockSpec((1, 3, D_MODEL), lambda i: (_tile_seq(i, BIG_TILE), 0, 0)),
                            pl.BlockSpec((1, D_MODEL, W_ALL), lambda i: (l, 0, 0))],
        out_specs=[pl.BlockSpec((BIG_TILE, w), lambda i: (i, 0)) for w in widths]
        + [pl.BlockSpec((BIG_TILE // CTX_LEN, BRANCH, CTX_LEN), lambda i: (jnp.minimum(i, N_CTX // BIG_TILE - 1), 0, 0))] * 2
        + [pl.BlockSpec((BIG_TILE, BRANCH), lambda i: (i, 0))],
        out_shape=[jax.ShapeDtypeStruct((N_TOK, w), F32) for w in widths]
        + [jax.ShapeDtypeStruct((N_CTX_SEQ, BRANCH, CTX_LEN), F32)] * 2 + [jax.ShapeDtypeStruct((N_TOK, BRANCH), BF16)],
        compiler_params=pltpu.CompilerParams(dimension_semantics=("arbitrary",), vmem_limit_bytes=IN_PROJ_VMEM_LIMIT),
        name="in_proj",
    )(*xs, mod, w_all)


def _arrange_body(wt_ref, o_ref):
    w = wt_ref[0]
    c = _IN_OFF
    rows = lambda name: w[c[name][0]:c[name][-1] + 1]
    zero = lambda n: jnp.zeros((n, w.shape[1]), F32)

    def per_head(x, y):
        xs, ys = rows(x), rows(y)
        return [p for h in range(HG_HEADS) for p in (xs[h * HG_DK:(h + 1) * HG_DK], ys[h * HG_DK:(h + 1) * HG_DK])]
    pieces = ([w[0:W_A + W_B]] + per_head("hg_q", "hg_q") + per_head("hg_ff", "hg_fb") + [rows("hg_i"), rows("hg_g")]
              + [rows("mla_cq"), zero(256 - MLA_Q_RANK), rows("mla_ckv"), zero(MLA_NOPE), rows("mla_kr"),
                 zero(128 - MLA_NOPE - MLA_ROPE), rows("mla_g")])
    o_ref[0] = jnp.concatenate(pieces, axis=0).T.astype(BF16)


def _arrange_w_in(w_in):
    lanes = 256
    wt = jnp.swapaxes(w_in, 1, 2)
    return pl.pallas_call(
        _arrange_body,
        grid=(DEPTH, D_MODEL // lanes),
        in_specs=[pl.BlockSpec((1, wt.shape[1], lanes), lambda l, i: (l, 0, i))],
        out_specs=pl.BlockSpec((1, lanes, W_ALL), lambda l, i: (l, i, 0)),
        out_shape=jax.ShapeDtypeStruct((DEPTH, D_MODEL, W_ALL), BF16),
        compiler_params=_cparams("parallel", "parallel"),
        name="arrange_w_in",
    )(wt)


def _out_proj_body(*refs, split_in, final):
    ac_ref, al_ref, y_ref, u_ref, g_ref, dsk_ref, wglu_ref, c_ref, dc_ref, dl_ref = refs[:10]
    i = pl.program_id(0)
    if split_in:
        xc_ref, xl_ref, mod_ref, w_ref, fn_ref = refs[10:15]
        x = _split_rows(i, xc_ref, xl_ref)
    else:
        x_ref, mod_ref, w_ref, fn_ref = refs[10:14]
        x = x_ref[...]
    b_out = _s5_gated(y_ref[...], u_ref[...], g_ref[...], dsk_ref[...], wglu_ref[...])
    branches = (_split_rows(i, ac_ref, al_ref), b_out, c_ref[...], _split_rows(i, dc_ref, dl_ref))
    acc = None
    for j, r in enumerate(branches):
        t = _dot(r.astype(BF16), w_ref[0, j * BRANCH:(j + 1) * BRANCH, :].astype(BF16))
        acc = t if acc is None else acc + t
    x = x + mod_ref[0][2:3] * acc
    if not final:
        refs[-1][...] = x
        return
    y = x * lax.rsqrt(jnp.mean(x * x, axis=-1, keepdims=True) + EPS) * fn_ref[...]
    yc_ref, yl_ref = refs[-2:]

    @pl.when(i < N_CTX // BIG_TILE)
    def _():
        yc_ref[...] = y

    @pl.when(i >= N_CTX // BIG_TILE)
    def _():
        yl_ref[...] = y


def _out_proj(a, s5, c, d, xs, mod, w_out, l, final_norm, final):
    y_ssm, z_b, d_skip, w_glu = s5
    br = pl.BlockSpec((BIG_TILE, BRANCH), lambda i: (i, 0))
    s5_specs = [br, br, pl.BlockSpec((BIG_TILE, BRANCH), lambda i: (i, 1)),
                pl.BlockSpec((1, BRANCH), lambda i: (0, 0)), pl.BlockSpec((BRANCH, 2 * BRANCH), lambda i: (0, 0))]
    pair = [_ctx_tile_spec(BRANCH, BIG_TILE), _lat_tile_spec(BRANCH, BIG_TILE)]
    split_in = len(xs) == 2
    x_specs = ([_ctx_tile_spec(D_MODEL, BIG_TILE), _lat_tile_spec(D_MODEL, BIG_TILE)] if split_in
               else [pl.BlockSpec((BIG_TILE, D_MODEL), lambda i: (i, 0))])
    if final:
        out_specs = [_ctx_tile_spec(D_MODEL, BIG_TILE), _lat_tile_spec(D_MODEL, BIG_TILE)]
        out_shape = [jax.ShapeDtypeStruct((N_CTX, D_MODEL), F32), jax.ShapeDtypeStruct((N_LAT, D_MODEL), F32)]
    else:
        out_specs = pl.BlockSpec((BIG_TILE, D_MODEL), lambda i: (i, 0))
        out_shape = jax.ShapeDtypeStruct((N_TOK, D_MODEL), F32)
    return pl.pallas_call(
        functools.partial(_out_proj_body, split_in=split_in, final=final),
        grid=(N_TOK // BIG_TILE,),
        in_specs=pair + s5_specs + [br] + pair + x_specs + [
            pl.BlockSpec((1, 3, D_MODEL), lambda i: (_tile_seq(i, BIG_TILE), 0, 0)),
            pl.BlockSpec((1, D_MODEL, D_MODEL), lambda i: (l, 0, 0)),
            pl.BlockSpec((1, D_MODEL), lambda i: (0, 0))],
        out_specs=out_specs,
        out_shape=out_shape,
        compiler_params=_cparams("arbitrary"),
        name="out_proj",
    )(*a, y_ssm, z_b, z_b, d_skip, w_glu, c, *d, *xs, mod, w_out, final_norm)


LOG2E = 1.4426950408889634


def _exp2_rows(s):
    e = jnp.exp2(s - jnp.max(s, axis=-1, keepdims=True))
    return e, jnp.sum(e, axis=-1, keepdims=True)


def _rope(x, cos, sin_lo, sin_hi):
    w = x.shape[-1]
    return x * cos + pltpu.roll(x, w - 8, 1) * sin_lo + pltpu.roll(x, 8, 1) * sin_hi


def _da_kv_body(k_ref, v_ref, cos_ref, slo_ref, shi_ref, ck_ref, cv_ref, ko_ref, vo_ref):
    j = pl.program_id(1)

    @pl.when(j < LAT_TILES)
    def _():
        ko_ref[0] = _rope(k_ref[...], cos_ref[...], slo_ref[...], shi_ref[...]).astype(BF16)
        vo_ref[0] = v_ref[...].astype(BF16)

    @pl.when(j == LAT_TILES)
    def _():
        ko_ref[0] = ck_ref[0].astype(BF16)
        vo_ref[0] = cv_ref[0].astype(BF16)


def _da_latent_kv(z_a, tabs, cache_k, cache_v):
    def rows(col):
        return pl.BlockSpec(
            (ROW_TILE, BRANCH),
            lambda b, j: (CTX_TILES + b * LAT_TILES + jnp.minimum(j, LAT_TILES - 1), col))
    tab = pl.BlockSpec((ROW_TILE, BRANCH), lambda b, j: (jnp.minimum(j, LAT_TILES - 1), 0))
    cache = pl.BlockSpec((1, PAST_LEN, BRANCH), lambda b, j: (b, 0, 0))
    out = pl.BlockSpec((1, ROW_TILE, BRANCH), lambda b, j: (b, j, 0))
    shp = jax.ShapeDtypeStruct((N_LAT_SEQ, LAT_LEN + PAST_LEN, BRANCH), BF16)
    return pl.pallas_call(
        _da_kv_body,
        grid=(N_LAT_SEQ, LAT_TILES + 1),
        in_specs=[rows(1), rows(2), tab, tab, tab, cache, cache],
        out_specs=[out, out],
        out_shape=[shp, shp],
        compiler_params=_cparams("parallel", "parallel"),
        name="da_kv",
    )(z_a, z_a, *tabs, cache_k, cache_v)


def _da_attn_body(lam_ref, ng_ref, q_ref, *rest, rope, lam_init):
    if rope:
        cos_ref, slo_ref, shi_ref, k_ref, v_ref, g_ref, o_ref = rest
        q = _rope(q_ref[...], cos_ref[...], slo_ref[...], shi_ref[...])
        o_ref[...] = _da_attn_tile(lam_ref, ng_ref, q, k_ref[0], v_ref[0], g_ref[...], lam_init)
    else:
        k_ref, v_ref, g_ref, o_ref = rest
        for t in range(q_ref.shape[0] // CTX_LEN):
            r = slice(t * CTX_LEN, (t + 1) * CTX_LEN)
            o_ref[r, :] = _da_attn_tile(lam_ref, ng_ref, q_ref[r, :], k_ref[r, :].astype(BF16),
                                        v_ref[r, :].astype(BF16), g_ref[r, :], lam_init)


def _da_attn_tile(lam_ref, ng_ref, q, k, v, g, lam_init):
    q = q * (DA_QK ** -0.5 * LOG2E)
    lv = lam_ref[...]
    lam = (jnp.exp(jnp.sum(lv[0:1] * lv[1:2], axis=-1, keepdims=True))
           - jnp.exp(jnp.sum(lv[2:3] * lv[3:4], axis=-1, keepdims=True)) + lam_init)
    lane = lax.broadcasted_iota(jnp.int32, (1, BRANCH), 1)
    acc = jnp.zeros(q.shape, F32)
    for h in range(DA_HEADS):
        q1 = jnp.where(lane // DA_QK == 2 * h, q, 0.0).astype(BF16)
        q2 = jnp.where(lane // DA_QK == 2 * h + 1, q, 0.0).astype(BF16)
        e1, l1 = _exp2_rows(_dot_nt(q1, k))
        e2, l2 = _exp2_rows(_dot_nt(q2, k))
        a = (e1 - (lam * l1 / l2) * e2).astype(BF16)
        acc = jnp.where(lane // (2 * DA_QK) == h, _dot(a, v) * (1.0 / l1), acc)
    sq = acc * acc
    ms = jnp.zeros(q.shape, F32)
    for h in range(DA_HEADS):
        hm = lane // (2 * DA_QK) == h
        ms = jnp.where(hm, jnp.sum(jnp.where(hm, sq, 0.0), axis=-1, keepdims=True), ms)
    o = acc * lax.rsqrt(ms * (1.0 / (2 * DA_QK)) + EPS) * (ng_ref[...] * (1.0 - lam_init))
    return o * _silu(g)


def _da_attention(z_a, lam_vec, norm_g, lam_init, tabs, kv_lat):
    ng = jnp.tile(norm_g.reshape(1, 2 * DA_QK), (1, DA_HEADS))
    small = [pl.BlockSpec((4, DA_QK), lambda *_: (0, 0)), pl.BlockSpec((1, BRANCH), lambda *_: (0, 0))]

    rows = CTX_SEQ_PER_STEP * CTX_LEN

    def col(c):
        return pl.BlockSpec((rows, BRANCH), lambda i: (i, c))
    ctx = pl.pallas_call(
        functools.partial(_da_attn_body, rope=False, lam_init=lam_init),
        grid=(N_CTX // rows,),
        in_specs=small + [col(0), col(1), col(2), col(3)],
        out_specs=pl.BlockSpec((rows, BRANCH), lambda i: (i, 0)),
        out_shape=jax.ShapeDtypeStruct((N_CTX, BRANCH), F32),
        compiler_params=_cparams("parallel"),
        name="da_attn_ctx",
    )(lam_vec, ng, z_a, z_a, z_a, z_a)

    def lcol(c):
        return pl.BlockSpec((LAT_Q_TILE, BRANCH), lambda b, j: (N_CTX // LAT_Q_TILE + b * LAT_Q_TILES + j, c))
    tab = pl.BlockSpec((LAT_Q_TILE, BRANCH), lambda b, j: (j, 0))
    kvs = pl.BlockSpec((1, LAT_LEN + PAST_LEN, BRANCH), lambda b, j: (b, 0, 0))
    lat = pl.pallas_call(
        functools.partial(_da_attn_body, rope=True, lam_init=lam_init),
        grid=(N_LAT_SEQ, LAT_Q_TILES),
        in_specs=small + [lcol(0), tab, tab, tab, kvs, kvs, lcol(3)],
        out_specs=pl.BlockSpec((LAT_Q_TILE, BRANCH), lambda b, j: (b * LAT_Q_TILES + j, 0)),
        out_shape=jax.ShapeDtypeStruct((N_LAT, BRANCH), F32),
        compiler_params=_cparams("parallel", "parallel"),
        name="da_attn_lat",
    )(lam_vec, ng, z_a, *tabs, kv_lat[0], kv_lat[1], z_a)
    return ctx, lat


MLA_HEAD_PAD = 128
MLA_QW = MLA_HEADS * MLA_HEAD_PAD


def _mla_prep_body(cq_ref, ckv_ref, kr_ref, ck_t, sk_lo, sk_hi, qn_ref, kvn_ref, wq_ref, q_out, ckv_out, kr_out):
    cq = cq_ref[...]
    ms = jnp.sum(cq * cq, axis=-1, keepdims=True) * (1.0 / MLA_Q_RANK)
    qn = (cq * lax.rsqrt(ms + EPS) * qn_ref[...]).astype(BF16)
    heads = lambda t: jnp.concatenate([t[...]] * MLA_HEADS, axis=-1)
    q = _rope(_dot(qn, wq_ref[...]), heads(ck_t), heads(sk_lo), heads(sk_hi))
    q_out[...] = (q * ((MLA_NOPE + MLA_ROPE) ** -0.5 * LOG2E)).astype(BF16)
    ckv = ckv_ref[...]
    ckv_out[...] = ckv * lax.rsqrt(jnp.mean(ckv * ckv, axis=-1, keepdims=True) + EPS) * kvn_ref[...]
    kr_out[...] = _rope(kr_ref[...], ck_t[...], sk_lo[...], sk_hi[...])


def _mla_prep(z_d, tabs, q_norm_pad, kv_norm, wq):
    ctx_tiles, lat_tiles = N_CTX // BIG_TILE, LAT_LEN // BIG_TILE

    def tab(w):
        return pl.BlockSpec((BIG_TILE, w), lambda i: (jnp.where(i < ctx_tiles, lat_tiles, (i - ctx_tiles) % lat_tiles), 0))

    def col(w, c):
        return pl.BlockSpec((BIG_TILE, w), lambda i: (i, c))

    def const(shape):
        return pl.BlockSpec(shape, lambda i: (0, 0))
    return pl.pallas_call(
        _mla_prep_body,
        grid=(N_TOK // BIG_TILE,),
        in_specs=[col(256, 0), col(128, 2), col(128, 3),
                  tab(128), tab(128), tab(128),
                  const((1, 256)), const((1, 128)), const((256, MLA_QW))],
        out_specs=[col(MLA_QW, 0), col(128, 0), col(128, 0)],
        out_shape=[jax.ShapeDtypeStruct((N_TOK, MLA_QW), BF16),
                   jax.ShapeDtypeStruct((N_TOK, 128), F32),
                   jax.ShapeDtypeStruct((N_TOK, 128), F32)],
        compiler_params=_cparams("parallel"),
        name="mla_prep",
    )(z_d, z_d, z_d, *tabs, q_norm_pad, kv_norm, wq)


def _mla_kv_math(ckv, kr, wk_ref, wv_ref, k_out, v_out):
    c = ckv.astype(BF16)
    k_out[...] = (_dot(c, wk_ref[...]) + jnp.concatenate([kr] * MLA_HEADS, axis=-1)).astype(BF16).reshape(k_out.shape)
    v_out[...] = _dot(c, wv_ref[...]).astype(BF16).reshape(v_out.shape)


def _mla_kv_ctx_body(ckv_ref, kr_ref, wk_ref, wv_ref, k_out, v_out):
    _mla_kv_math(ckv_ref[...], kr_ref[...], wk_ref, wv_ref, k_out, v_out)


def _mla_kv_lat_body(ckv_ref, kr_ref, cckv_ref, ckr_ref, wk_ref, wv_ref, k_out, v_out):
    j = pl.program_id(1)

    @pl.when(j < LAT_TILES)
    def _():
        _mla_kv_math(ckv_ref[...], kr_ref[...], wk_ref, wv_ref, k_out, v_out)

    @pl.when(j == LAT_TILES)
    def _():
        _mla_kv_math(cckv_ref[0], ckr_ref[0], wk_ref, wv_ref, k_out, v_out)


def _mla_kv(ckv, kr, cache_ckv, cache_kr, wk, wv):
    weights = [pl.BlockSpec((128, MLA_QW), lambda *_: (0, 0)), pl.BlockSpec((128, BRANCH), lambda *_: (0, 0))]
    k_ctx, v_ctx = pl.pallas_call(
        _mla_kv_ctx_body,
        grid=(N_CTX // BIG_TILE,),
        in_specs=[pl.BlockSpec((BIG_TILE, 128), lambda i: (i, 0)), pl.BlockSpec((BIG_TILE, 128), lambda i: (i, 0))] + weights,
        out_specs=[pl.BlockSpec((BIG_TILE, MLA_QW), lambda i: (i, 0)),
                   pl.BlockSpec((BIG_TILE, BRANCH), lambda i: (i, 0))],
        out_shape=[jax.ShapeDtypeStruct((N_CTX, MLA_QW), BF16), jax.ShapeDtypeStruct((N_CTX, BRANCH), BF16)],
        compiler_params=_cparams("parallel"),
        name="mla_kv_ctx",
    )(ckv, kr, wk, wv)
    rows = pl.BlockSpec((ROW_TILE, 128), lambda b, j: (CTX_TILES + b * LAT_TILES + jnp.minimum(j, LAT_TILES - 1), 0))
    cache = pl.BlockSpec((1, PAST_LEN, 128), lambda b, j: (b, 0, 0))
    lk = LAT_LEN + PAST_LEN
    k_lat, v_lat = pl.pallas_call(
        _mla_kv_lat_body,
        grid=(N_LAT_SEQ, LAT_TILES + 1),
        in_specs=[rows, rows, cache, cache] + weights,
        out_specs=[pl.BlockSpec((1, ROW_TILE, MLA_QW), lambda b, j: (b, j, 0)),
                   pl.BlockSpec((1, ROW_TILE, BRANCH), lambda b, j: (b, j, 0))],
        out_shape=[jax.ShapeDtypeStruct((N_LAT_SEQ, lk, MLA_QW), BF16), jax.ShapeDtypeStruct((N_LAT_SEQ, lk, BRANCH), BF16)],
        compiler_params=_cparams("parallel", "parallel"),
        name="mla_kv_lat",
    )(ckv, kr, cache_ckv, cache_kr, wk, wv)
    return k_ctx, v_ctx, k_lat, v_lat


def _mla_attn_body(q_ref, k_ref, v_ref, g_ref, o_ref, *, ctx):
    if ctx:
        for t in range(q_ref.shape[0] // CTX_LEN):
            r = slice(t * CTX_LEN, (t + 1) * CTX_LEN)
            o_ref[r, :] = _mla_attn_tile(q_ref[r, :], k_ref[r, :], v_ref[r, :], g_ref[r, :])
    else:
        o_ref[...] = _mla_attn_tile(q_ref[...], k_ref[0], v_ref[0], g_ref[...])


def _mla_attn_tile(q, k, v, g):
    lane = lax.broadcasted_iota(jnp.int32, (1, BRANCH), 1)
    acc = jnp.zeros((q.shape[0], BRANCH), F32)
    for h in range(MLA_HEADS):
        sl = slice(h * MLA_HEAD_PAD, (h + 1) * MLA_HEAD_PAD)
        e, l = _exp2_rows(_dot_nt(q[:, sl], k[:, sl]))
        acc = jnp.where(lane // 64 == h, _dot(e.astype(BF16), v) * (1.0 / l), acc)
    return acc * _silu(g)


def _mla_attention(z_d, q, k_ctx, v_ctx, k_lat, v_lat):
    rows = CTX_SEQ_PER_STEP * CTX_LEN
    ctx = pl.pallas_call(
        functools.partial(_mla_attn_body, ctx=True),
        grid=(N_CTX // rows,),
        in_specs=[pl.BlockSpec((rows, MLA_QW), lambda i: (i, 0)),
                  pl.BlockSpec((rows, MLA_QW), lambda i: (i, 0)),
                  pl.BlockSpec((rows, BRANCH), lambda i: (i, 0)),
                  pl.BlockSpec((rows, BRANCH), lambda i: (i, 2))],
        out_specs=pl.BlockSpec((rows, BRANCH), lambda i: (i, 0)),
        out_shape=jax.ShapeDtypeStruct((N_CTX, BRANCH), F32),
        compiler_params=_cparams("parallel"),
        name="mla_attn_ctx",
    )(q, k_ctx, v_ctx, z_d)
    lk = LAT_LEN + PAST_LEN
    lat = pl.pallas_call(
        functools.partial(_mla_attn_body, ctx=False),
        grid=(N_LAT_SEQ, LAT_Q_TILES),
        in_specs=[pl.BlockSpec((LAT_Q_TILE, MLA_QW), lambda b, j: (N_CTX // LAT_Q_TILE + b * LAT_Q_TILES + j, 0)),
                  pl.BlockSpec((1, lk, MLA_QW), lambda b, j: (b, 0, 0)),
                  pl.BlockSpec((1, lk, BRANCH), lambda b, j: (b, 0, 0)),
                  pl.BlockSpec((LAT_Q_TILE, BRANCH), lambda b, j: (N_CTX // LAT_Q_TILE + b * LAT_Q_TILES + j, 2))],
        out_specs=pl.BlockSpec((LAT_Q_TILE, BRANCH), lambda b, j: (b * LAT_Q_TILES + j, 0)),
        out_shape=jax.ShapeDtypeStruct((N_LAT, BRANCH), F32),
        compiler_params=_cparams("parallel", "parallel"),
        name="mla_attn_lat",
    )(q, k_lat, v_lat, z_d)
    return ctx, lat


S5_TAP = S5_CHUNK * S5_CH
S5_NCHUNK = N_TOK // S5_CHUNK
S5_CTX_CH = N_CTX // S5_CHUNK
S5_CTX_SEQ_CH = CTX_LEN // S5_CHUNK
S5_LAT_SEQ_CH = LAT_LEN // S5_CHUNK
S5_SCAN_STEPS = S5_LAT_SEQ_CH.bit_length() - 1


def _s5_body(x_ref, mt_ref, bst_ref, cot_ref, a_ref, h0_ref, y_ref, fin_ref):
    x = x_ref[...].reshape(S5_TAP, S5_NCHUNK)
    y = _dot(mt_ref[0, 0], x)
    s = _dot(bst_ref[0, 0], x)
    lane = lax.broadcasted_iota(jnp.int32, (1, S5_NCHUNK), 1)
    is_lat = lane >= S5_CTX_CH
    pos_f = jnp.where(is_lat, (lane - S5_CTX_CH) & (S5_LAT_SEQ_CH - 1), lane & (S5_CTX_SEQ_CH - 1))
    pos_b = jnp.where(is_lat, S5_LAT_SEQ_CH - 1, S5_CTX_SEQ_CH - 1) - pos_f
    hin = []
    for d in range(2):
        n = S5_STATE
        sre, sim = s[2 * d * n:(2 * d + 1) * n], s[(2 * d + 1) * n:(2 * d + 2) * n]
        are = jnp.concatenate([a_ref[0, 0, 2 * d]] * (S5_NCHUNK // 128), axis=-1)
        aim = jnp.concatenate([a_ref[0, 0, 2 * d + 1]] * (S5_NCHUNK // 128), axis=-1)
        pos = pos_f if d == 0 else pos_b
        h0r, h0i = jnp.zeros_like(sre), jnp.zeros_like(sre)
        for b in range(N_LAT_SEQ):
            first = S5_CTX_CH + b * S5_LAT_SEQ_CH + (0 if d == 0 else S5_LAT_SEQ_CH - 1)
            h0r = jnp.where(lane == first, h0_ref[0, 2 * d][:, b:b + 1], h0r)
            h0i = jnp.where(lane == first, h0_ref[0, 2 * d + 1][:, b:b + 1], h0i)
        xr = sre + are * h0r - aim * h0i
        xi = sim + are * h0i + aim * h0r
        pr, pi = are, aim
        for j in range(S5_SCAN_STEPS):
            sh = 1 << j
            shift = sh if d == 0 else S5_NCHUNK - sh
            rr, ri = pltpu.roll(xr, shift, 1), pltpu.roll(xi, shift, 1)
            ok = pos >= sh
            xr, xi = (xr + jnp.where(ok, pr * rr - pi * ri, 0.0), xi + jnp.where(ok, pr * ri + pi * rr, 0.0))
            pr, pi = pr * pr - pi * pi, 2.0 * pr * pi
        last = lax.broadcasted_iota(jnp.int32, (1, 128), 1) * S5_CTX_SEQ_CH + (S5_CTX_SEQ_CH - 1 if d == 0 else 0)
        pick = jnp.where(lax.broadcasted_iota(jnp.int32, (S5_CTX_CH, 1), 0) == last, 1.0, 0.0)
        fin_ref[0, 2 * d] = _dot_sel(xr[:, :S5_CTX_CH], pick)
        fin_ref[0, 2 * d + 1] = _dot_sel(xi[:, :S5_CTX_CH], pick)
        one = 1 if d == 0 else S5_NCHUNK - 1
        hin.append(jnp.where(pos >= 1, pltpu.roll(xr, one, 1), h0r))
        hin.append(jnp.where(pos >= 1, pltpu.roll(xi, one, 1), h0i))
    y = y + _dot(cot_ref[0, 0], jnp.concatenate(hin, axis=0).astype(BF16))
    y_ref[...] = y.reshape(S5_CHUNK, S5_CH, S5_NCHUNK)


def _s5_scan(x_all, mt, bst, cot, a16, h0, l):
    g = S5_GROUPS
    sq = pl.BlockSpec((1, 1, S5_TAP, S5_TAP), lambda i: (l, i, 0, 0))
    st = pl.BlockSpec((1, 4, S5_STATE, 128), lambda i: (i, 0, 0, 0))
    return pl.pallas_call(
        _s5_body,
        grid=(g,),
        in_specs=[pl.BlockSpec((S5_CHUNK, S5_CH, S5_NCHUNK), lambda i: (0, i, 0)), sq, sq, sq,
                  pl.BlockSpec((1, 1, 4, S5_STATE, 128), lambda i: (l, i, 0, 0, 0)), st],
        out_specs=[pl.BlockSpec((S5_CHUNK, S5_CH, S5_NCHUNK), lambda i: (0, i, 0)),
                   pl.BlockSpec((1, 4, S5_STATE, 128), lambda i: (i, 0, 0, 0))],
        out_shape=[jax.ShapeDtypeStruct((S5_CHUNK, BRANCH, S5_NCHUNK), F32),
                   jax.ShapeDtypeStruct((g, 4, S5_STATE, 128), F32)],
        compiler_params=_cparams("parallel"),
        name="s5_scan",
    )(x_all, mt, bst, cot, a16, h0)


def _s5_gated(y_ssm, u, g, d_skip, w_glu):
    y = u * d_skip + y_ssm
    ge = 0.5 * y * (1.0 + jnp.tanh(0.7978845608028654 * (y + 0.044715 * (y * y * y))))
    gl = _dot(ge.astype(BF16), w_glu)
    return gl[:, :BRANCH] * (1.0 / (1.0 + jnp.exp(-gl[:, BRANCH:]))) * _silu(g)


HG_CHUNK = ROW_TILE
HG_W = 2 * HG_HEADS * HG_DK
HG_HEAD_W = 2 * HG_DK
HG_LAT_CHUNKS = LAT_LEN // HG_CHUNK
HG_CHUNKS = N_TOK // HG_CHUNK


def _hg_gates(z, lb):
    e = jnp.exp(-jnp.abs(z))
    r = 1.0 / (1.0 + e)
    sig_pos = jnp.where(z >= 0, r, e * r)
    sig_neg = jnp.where(z >= 0, e * r, r)
    return lb + (1.0 - lb) * sig_pos, (1.0 - lb) * sig_neg


def _bcast_row(x, period, r):
    n, w = x.shape
    if period >= 8:
        x3 = x.reshape(n // period, period, w)
        return jnp.broadcast_to(x3[:, r:r + 1, :], x3.shape).reshape(n, w)
    x3 = x.reshape(n // 8, 8, w)
    sub = lax.broadcasted_iota(jnp.int32, (1, 8, 1), 1)
    out = None
    for j in range(8 // period):
        b = jnp.broadcast_to(x3[:, j * period + r:j * period + r + 1, :], x3.shape)
        out = b if out is None else jnp.where(sub >= j * period, b, out)
    return out.reshape(n, w)


def _hg_scans(f, isb):
    n = f.shape[0]
    row = lax.broadcasted_iota(jnp.int32, (n, 1), 0)
    p, r = f, jnp.ones_like(f)
    levels = []
    h, sh = 1, 0
    while h < n:
        levels.append((h, sh, p, r))
        up = (row >> sh) & 1
        tot_p = jnp.where(isb == 1, _bcast_row(p, 2 * h, h), _bcast_row(p, 2 * h, h - 1))
        tot_r = jnp.where(isb == 1, _bcast_row(p, 2 * h, 0), _bcast_row(p, 2 * h, 2 * h - 1))
        p = p * jnp.where(up != isb, tot_p, 1.0)
        r = r * jnp.where(up == isb, tot_r, 1.0)
        h, sh = 2 * h, sh + 1
    return levels, p, r


def _hg_state_body(zf_ref, zb_ref, vf_ref, vb_ref, lb_ref, s0_ref, sf_out, sb_out, s_scr):
    i = pl.program_id(0)

    @pl.when(i % HG_LAT_CHUNKS == 0)
    def _():
        s_scr[...] = s0_ref[0]

    sf_out[0] = s_scr[:, 0:HG_DK, :]
    sb_out[0] = s_scr[:, HG_DK:, :]
    lane5 = lax.broadcasted_iota(jnp.int32, (1, HG_W), 1)
    isb = (lane5 >> 6) & 1
    z = jnp.where(isb == 1, zb_ref[...], zf_ref[...])
    f, k = _hg_gates(z, lb_ref[...])
    r, ptot = _hg_chunk_decay(f, isb)
    kt = k * r
    lane = lax.broadcasted_iota(jnp.int32, (1, BRANCH), 1)
    vf = vf_ref[...]
    vb = vb_ref[...]
    for hd in range(HG_HEADS):
        sl = slice(hd * HG_HEAD_W, (hd + 1) * HG_HEAD_W)
        kth = kt[:, sl].T.astype(BF16)
        hm = (lane >> 6) == hd
        d_f = _dot(kth, jnp.where(hm, vf, 0.0).astype(BF16))
        d_b = _dot(kth, jnp.where(hm, vb, 0.0).astype(BF16))
        ds = jnp.concatenate([d_f[:HG_DK], d_b[HG_DK:]], axis=0)
        pcol = jnp.broadcast_to(ptot[:, sl], (HG_HEAD_W, HG_HEAD_W)).T[:, 0:1]
        s_scr[hd] = s_scr[hd] * pcol + ds


def _hg_chunk_decay(f, isb):
    n = f.shape[0]
    row = lax.broadcasted_iota(jnp.int32, (n, 1), 0)
    dist = jnp.where(isb == 1, row, n - 1 - row)
    x = f
    sh = 1
    while sh < n:
        src = jnp.where(isb == 1, pltpu.roll(x, sh, 0), pltpu.roll(x, n - sh, 0))
        x = x * jnp.where(dist >= sh, src, 1.0)
        sh *= 2
    total = jnp.where(isb == 1, x[n - 1:n], x[0:1])
    nxt = jnp.where(isb == 1, pltpu.roll(x, 1, 0), pltpu.roll(x, n - 1, 0))
    return jnp.where(dist >= 1, nxt, 1.0), total


HG_LAT_STEPS = N_LAT_SEQ * HG_LAT_CHUNKS


def _hg_lat_rev(i):
    return (i // HG_LAT_CHUNKS) * HG_LAT_CHUNKS + (HG_LAT_CHUNKS - 1 - i % HG_LAT_CHUNKS)


def _hg_states(z_c, lb, s0):
    first = N_CTX_SEQ
    zz_f = pl.BlockSpec((HG_CHUNK, HG_W), lambda i: (first + i, 1))
    zz_b = pl.BlockSpec((HG_CHUNK, HG_W), lambda i: (first + _hg_lat_rev(i), 1))
    v_f = pl.BlockSpec((HG_CHUNK, BRANCH), lambda i: (first + i, 4))
    v_b = pl.BlockSpec((HG_CHUNK, BRANCH), lambda i: (first + _hg_lat_rev(i), 4))
    st = (HG_HEADS, HG_HEAD_W, BRANCH)
    half = (HG_HEADS, HG_DK, BRANCH)
    return pl.pallas_call(
        _hg_state_body,
        grid=(HG_LAT_STEPS,),
        in_specs=[zz_f, zz_b, v_f, v_b,
                  pl.BlockSpec((1, HG_W), lambda i: (0, 0)),
                  pl.BlockSpec((1,) + st, lambda i: (i // HG_LAT_CHUNKS, 0, 0, 0))],
        out_specs=[pl.BlockSpec((1,) + half, lambda i: (i, 0, 0, 0)),
                   pl.BlockSpec((1,) + half, lambda i: (_hg_lat_rev(i), 0, 0, 0))],
        out_shape=[jax.ShapeDtypeStruct((HG_LAT_STEPS,) + half, F32),
                   jax.ShapeDtypeStruct((HG_LAT_STEPS,) + half, F32)],
        scratch_shapes=[pltpu.VMEM(st, F32)],
        compiler_params=_cparams("arbitrary"),
        name="hg_states",
    )(z_c, z_c, z_c, z_c, lb, s0)


def _hg_main_body(qq_ref, zz_ref, v_ref, g_ref, sf_ref, sb_ref, lb_ref, ng_ref, o_ref, fin_ref):
    n = HG_CHUNK
    i = pl.program_id(0)
    qq = qq_ref[...]
    lane5 = lax.broadcasted_iota(jnp.int32, (1, HG_W), 1)
    isb = (lane5 >> 6) & 1
    f, k = _hg_gates(zz_ref[...], lb_ref[...])
    levels, pfull, rfull = _hg_scans(f, isb)
    row = lax.broadcasted_iota(jnp.int32, (n, 1), 0)
    col = lax.broadcasted_iota(jnp.int32, (1, n), 1)
    ops = [(qq.astype(BF16), k.astype(BF16), row == col)]
    for h, sh, p, r in levels:
        up = (row >> sh) & 1
        qt = jnp.where(up != isb, qq * p, 0.0).astype(BF16)
        kt = jnp.where(up == isb, k * r, 0.0).astype(BF16)
        ops.append((qt, kt, (row >> (sh + 1)) == (col >> (sh + 1))))
    qc = (qq * pfull).astype(BF16)
    v = v_ref[...]
    vb = v.astype(BF16)
    lane = lax.broadcasted_iota(jnp.int32, (1, BRANCH), 1)
    latent = i >= N_CTX_SEQ
    acc = jnp.zeros((n, BRANCH), F32)
    for hd in range(HG_HEADS):
        sl = slice(hd * HG_HEAD_W, (hd + 1) * HG_HEAD_W)
        a = jnp.zeros((n, n), F32)
        for qt, kt, mask in ops:
            a = a + jnp.where(mask, _dot_nt(qt[:, sl], kt[:, sl]), 0.0)
        s_in = jnp.concatenate([sf_ref[0, hd], sb_ref[0, hd]], axis=0)
        s_in = jnp.where(latent, s_in, 0.0).astype(BF16)
        o_h = _dot(a.astype(BF16), vb) + _dot(qc[:, sl], s_in)
        acc = jnp.where((lane >> 6) == hd, o_h, acc)
    sq = acc * acc
    ms = jnp.zeros((n, BRANCH), F32)
    for hd in range(HG_HEADS):
        hm = (lane >> 6) == hd
        ms = jnp.where(hm, jnp.sum(jnp.where(hm, sq, 0.0), axis=-1, keepdims=True), ms)
    o_ref[...] = acc * lax.rsqrt(ms * (1.0 / HG_DK) + EPS) * ng_ref[...] * _silu(g_ref[...])

    @pl.when(i < N_CTX_SEQ)
    def _():
        kt_full = k * rfull
        for hd in range(HG_HEADS):
            kth = kt_full[:, hd * HG_HEAD_W:(hd + 1) * HG_HEAD_W].T.astype(BF16)
            ds = _dot(kth, jnp.where((lane >> 6) == hd, v, 0.0).astype(BF16))
            fin_ref[0, hd] = ds[:, hd * HG_DK:(hd + 1) * HG_DK]


def _hg_main(z_c, s_f, s_b, lb, norm_g):
    half = (1, HG_HEADS, HG_DK, BRANCH)
    lat = lambda i: (jnp.maximum(i - N_CTX_SEQ, 0), 0, 0, 0)
    fin = (HG_HEADS, HG_HEAD_W, HG_DK)
    return pl.pallas_call(
        _hg_main_body,
        grid=(HG_CHUNKS,),
        in_specs=[pl.BlockSpec((HG_CHUNK, HG_W), lambda i: (i, 0)),
                  pl.BlockSpec((HG_CHUNK, HG_W), lambda i: (i, 1)),
                  pl.BlockSpec((HG_CHUNK, BRANCH), lambda i: (i, 4)),
                  pl.BlockSpec((HG_CHUNK, BRANCH), lambda i: (i, 5)),
                  pl.BlockSpec(half, lat),
                  pl.BlockSpec(half, lat),
                  pl.BlockSpec((1, HG_W), lambda i: (0, 0)),
                  pl.BlockSpec((1, BRANCH), lambda i: (0, 0))],
        out_specs=[pl.BlockSpec((HG_CHUNK, BRANCH), lambda i: (i, 0)),
                   pl.BlockSpec((1,) + fin, lambda i: (jnp.minimum(i, N_CTX_SEQ - 1), 0, 0, 0))],
        out_shape=[jax.ShapeDtypeStruct((N_TOK, BRANCH), F32),
                   jax.ShapeDtypeStruct((N_CTX_SEQ,) + fin, F32)],
        compiler_params=_cparams("arbitrary"),
        name="hg_main",
    )(z_c, z_c, z_c, z_c, s_f, s_b, lb, norm_g)


def _take_cols(w, plan):
    idx = np.concatenate([p[0] for p in plan]).astype(np.int32)
    sign = np.concatenate([np.broadcast_to(p[1], p[0].shape) for p in plan]).astype(np.float32)
    return jnp.take(w, jnp.asarray(idx), axis=-1) * jnp.asarray(sign)


def _zeros(n):
    return (np.zeros(n, np.int64), 0.0)


_IN_OFF = {}
_off = 0
for _name, _n in (("da_q", 256), ("da_k", 256), ("da_v", 256), ("da_g", 256), ("s5_u", 256), ("s5_g", 256),
                  ("hg_q", 256), ("hg_ff", 256), ("hg_fb", 256), ("hg_i", 256), ("hg_g", 256),
                  ("mla_cq", MLA_Q_RANK), ("mla_ckv", MLA_KV_RANK), ("mla_kr", MLA_ROPE), ("mla_g", 256)):
    _IN_OFF[_name] = np.arange(_off, _off + _n)
    _off += _n


def _rope_tables():
    t = np.arange(LAT_LEN)
    pos = np.stack([t // GRID_W, t % GRID_W], axis=1).astype(np.float32)
    inv_freq = (np.float32(ROPE_BASE) ** (-np.arange(8, dtype=np.float32) / np.float32(8))).astype(np.float32)
    r = np.arange(MLA_ROPE)
    ang = (pos[:, r // 16] * inv_freq[r % 8][None, :]).astype(np.float64)
    cos32, sin32 = np.cos(ang).astype(np.float32), np.sin(ang).astype(np.float32)
    lo = (np.arange(MLA_ROPE) % 16 < 8)[None, :]
    sin_lo32, sin_hi32 = np.where(lo, -sin32, 0.0).astype(np.float32), np.where(lo, 0.0, sin32).astype(np.float32)
    da_tabs = tuple(np.tile(x, (1, 8)) for x in (cos32, sin_lo32, sin_hi32))

    def head(x, fill):
        h = np.concatenate([np.full((LAT_LEN, MLA_NOPE), fill, np.float32), x,
                            np.full((LAT_LEN, MLA_HEAD_PAD - MLA_NOPE - MLA_ROPE), fill, np.float32)], axis=1)
        return np.concatenate([h, np.full((BIG_TILE, MLA_HEAD_PAD), fill, np.float32)], axis=0)
    k_tabs = (head(cos32, 1.0), head(sin_lo32, 0.0), head(sin_hi32, 0.0))
    return tuple(jnp.asarray(x) for x in da_tabs), tuple(jnp.asarray(x) for x in k_tabs)


def _mla_weights(w_uq, w_ukv, q_norm):
    hd = MLA_NOPE + MLA_ROPE
    pad_tail = _zeros(MLA_HEAD_PAD - hd)
    q_plan, k_plan, v_plan = [], [], []
    for h in range(MLA_HEADS):
        nope, rope = np.arange(h * hd, h * hd + MLA_NOPE), np.arange(h * hd + MLA_NOPE, (h + 1) * hd)
        q_plan += [(nope, 1.0), (rope, 1.0), pad_tail]
        k_plan += [(np.arange(h * 2 * MLA_NOPE, h * 2 * MLA_NOPE + MLA_NOPE), 1.0), _zeros(MLA_HEAD_PAD - MLA_NOPE)]
        v_plan += [(np.arange(h * 2 * MLA_NOPE + MLA_NOPE, (h + 1) * 2 * MLA_NOPE), 1.0)]
    pad_rows = lambda x: jnp.pad(x, ((0, 256 - MLA_Q_RANK), (0, 0))).astype(BF16)
    qn = jnp.pad(q_norm, (0, 256 - MLA_Q_RANK)).reshape(1, 256)
    return (pad_rows(_take_cols(w_uq, q_plan)), _take_cols(w_ukv, k_plan).astype(BF16),
            _take_cols(w_ukv, v_plan).astype(BF16), qn)


def _split_bf16(a):
    hi = a.astype(BF16)
    return hi, (a - hi.astype(F32)).astype(BF16)


def _dot_sel(a, sel):
    hi, lo = _split_bf16(a)
    sel = sel.astype(BF16)
    return _dot(hi, sel) + _dot(lo, sel)


def _dot_x3(a, b):
    a_hi, a_lo = _split_bf16(a)
    b_hi, b_lo = _split_bf16(b)
    return _dot(a_hi, b_hi) + _dot(a_hi, b_lo) + _dot(a_lo, b_hi)


def _s5_table_body(xy_ref, bb_ref, c_ref, ct_ref, mt_ref, bst_ref, cot_ref, a_ref):
    n, t, ch = S5_STATE, S5_CHUNK, S5_CH
    wide = 2 * S5_TAP
    xy = xy_ref[0, 0]
    tau_i = lax.broadcasted_iota(jnp.int32, (1, 128), 1)
    tau = tau_i.astype(F32)
    sel_row = lax.broadcasted_iota(jnp.int32, (128, 1), 0)

    def lag(width):
        return lax.broadcasted_iota(jnp.int32, (1, width), 1) >> 4

    def onehot(cond):
        return jnp.where(cond, 1.0, 0.0).astype(F32)
    j = lag(wide)
    e_z = (onehot((j <= t - 1) & (sel_row == t - 1 - j)), onehot((j >= t - 1) & (j <= 2 * t - 2) & (sel_row == j - (t - 1))))
    jc = lag(S5_TAP)
    e_c = (onehot(sel_row == jc + 1), onehot(sel_row == t - jc))
    ch_row = lax.broadcasted_iota(jnp.int32, (ch, 1), 0)
    tile_w = onehot((lax.broadcasted_iota(jnp.int32, (1, wide), 1) & (ch - 1)) == ch_row)
    tile_n = onehot((lax.broadcasted_iota(jnp.int32, (1, S5_TAP), 1) & (ch - 1)) == ch_row)

    z, cot_rows, klong = [], [], None
    for d in range(2):
        x, y = xy[:, 2 * d:2 * d + 1], xy[:, 2 * d + 1:2 * d + 2]
        mag = jnp.exp(jnp.where(tau_i <= t, tau, 0.0) * x)
        ang = jnp.where(tau_i <= t, tau, 0.0) * y
        p_re = jnp.where(tau_i <= t, mag * jnp.cos(ang), 0.0)
        p_im = jnp.where(tau_i <= t, mag * jnp.sin(ang), 0.0)
        a_ref[0, 0, 2 * d] = jnp.broadcast_to(p_re[:, t:t + 1], (n, 128))
        a_ref[0, 0, 2 * d + 1] = jnp.broadcast_to(p_im[:, t:t + 1], (n, 128))
        pz_re, pz_im = _dot_sel(p_re, e_z[d]), _dot_sel(p_im, e_z[d])
        b_re, b_im = _dot_sel(bb_ref[0, 0, 2 * d], tile_w), _dot_sel(bb_ref[0, 0, 2 * d + 1], tile_w)
        z_re, z_im = pz_re * b_re - pz_im * b_im, pz_re * b_im + pz_im * b_re
        z += [z_re, z_im]
        part = _dot_x3(c_ref[0, 0, 2 * d], z_re) - _dot_x3(c_ref[0, 0, 2 * d + 1], z_im)
        klong = part if klong is None else klong + part
        pc_re, pc_im = _dot_sel(p_re, e_c[d]), _dot_sel(p_im, e_c[d])
        c_re, c_im = _dot_sel(ct_ref[0, 0, 2 * d], tile_n), _dot_sel(ct_ref[0, 0, 2 * d + 1], tile_n)
        cot_rows += [c_re * pc_re - c_im * pc_im, -(c_re * pc_im + c_im * pc_re)]
    for tt in range(t):
        off = (t - 1 - tt) * ch
        win = klong if off == 0 else pltpu.roll(klong, wide - off, 1)
        mt_ref[0, 0, tt * ch:(tt + 1) * ch, :] = win[:, :S5_TAP].astype(BF16)
    back = pltpu.roll(z[2], wide - (t - 1) * ch, 1), pltpu.roll(z[3], wide - (t - 1) * ch, 1)
    for k, rows in enumerate((z[0], z[1], back[0], back[1])):
        bst_ref[0, 0, k * n:(k + 1) * n, :] = rows[:, :S5_TAP].astype(BF16)
    cot_ref[0, 0] = jnp.concatenate(cot_rows, axis=0).T.astype(BF16)


def _s5_tables(a_re, a_im, log_dt, b_re, b_im, c_re, c_im):
    nl, g, n, ch = a_re.shape[0], S5_GROUPS, S5_STATE, S5_CH
    step = jnp.exp(log_dt)[..., None]
    mag = jnp.exp(a_re * step)
    ab_re, ab_im = mag * jnp.cos(a_im * step), mag * jnp.sin(a_im * step)
    den = a_re * a_re + a_im * a_im
    f_re = ((ab_re - 1.0) * a_re + ab_im * a_im) / den
    f_im = (ab_im * a_re - (ab_re - 1.0) * a_im) / den
    bb_re = f_re[..., None] * b_re - f_im[..., None] * b_im
    bb_im = f_re[..., None] * b_im + f_im[..., None] * b_re
    by_group = lambda x: jnp.moveaxis(x, 1, 2)
    pair = lambda re, im: jnp.stack([by_group(re), by_group(im)], axis=3).reshape((nl, g, 4) + re.shape[3:])
    xy = jnp.stack([by_group(a_re * step), by_group(a_im * step)], axis=3).reshape(nl, g, 4, n)
    xy = jnp.pad(jnp.swapaxes(xy, 2, 3), ((0, 0), (0, 0), (0, 0), (0, 4)))
    mat = pl.BlockSpec((1, 1, S5_TAP, S5_TAP), lambda l, i: (l, i, 0, 0))
    return pl.pallas_call(
        _s5_table_body,
        grid=(nl, g),
        in_specs=[pl.BlockSpec((1, 1, n, 8), lambda l, i: (l, i, 0, 0)),
                  pl.BlockSpec((1, 1, 4, n, ch), lambda l, i: (l, i, 0, 0, 0)),
                  pl.BlockSpec((1, 1, 4, ch, n), lambda l, i: (l, i, 0, 0, 0)),
                  pl.BlockSpec((1, 1, 4, n, ch), lambda l, i: (l, i, 0, 0, 0))],
        out_specs=[mat, mat, mat, pl.BlockSpec((1, 1, 4, n, 128), lambda l, i: (l, i, 0, 0, 0))],
        out_shape=[jax.ShapeDtypeStruct((nl, g, S5_TAP, S5_TAP), BF16)] * 3
        + [jax.ShapeDtypeStruct((nl, g, 4, n, 128), F32)],
        compiler_params=_cparams("parallel", "parallel"),
        name="s5_tables",
    )(xy, pair(bb_re, bb_im), pair(c_re, c_im), pair(jnp.swapaxes(c_re, -1, -2), jnp.swapaxes(c_im, -1, -2)))


def _s5_chunk_lanes(u):
    return u.reshape(S5_NCHUNK, S5_CHUNK, BRANCH).transpose(1, 2, 0)


def _s5_token_rows(y):
    return y.transpose(2, 0, 1).reshape(N_TOK, BRANCH)


def kernel(x_prompt, x_sample, cache_diff_k, cache_diff_v, state_s5, state_hgrn, cache_mla_ckv, cache_mla_krope, c, c_ctx, w_mod, b_mod, w_in, w_out, da_lambda, da_norm, s5_a_re, s5_a_im, s5_log_dt, s5_b_re, s5_b_im, s5_c_re, s5_c_im, s5_d, s5_w_glu, hg_lb, hg_norm, mla_q_norm, mla_w_uq, mla_kv_norm, mla_w_ukv, final_norm):
    lb_w = jax.nn.softmax(hg_lb.astype(F32), axis=0)
    lb_all = jnp.cumsum(lb_w, axis=0) - lb_w[0:1]
    c_rows = jnp.concatenate([c_ctx[None], c, jnp.zeros((8 - 1 - N_LAT_SEQ, D_MODEL), F32)], axis=0)
    mods = _modulation(c_rows, w_mod, b_mod)
    da_tabs, mla_tabs = _rope_tables()
    xs = (x_prompt.reshape(N_CTX, D_MODEL), x_sample.reshape(N_LAT, D_MODEL))
    new_k, new_v, new_s5, new_hg, new_ckv, new_kr = [], [], [], [], [], []
    s5_tabs = _s5_tables(s5_a_re, s5_a_im, s5_log_dt, s5_b_re, s5_b_im, s5_c_re, s5_c_im)
    w_all = _arrange_w_in(w_in)
    for l in range(DEPTH):
        mod = mods[l, :3].reshape(3, 3, D_MODEL)
        z_a, z_b, z_c, z_d, k_new, v_new, u_bf = _in_proj(xs, mod, w_all, l)

        lam_init = 0.8 - 0.6 * math.exp(-0.3 * l)
        kv_lat = _da_latent_kv(z_a, da_tabs,
                               cache_diff_k[:, l].reshape(N_LAT_SEQ, PAST_LEN, BRANCH),
                               cache_diff_v[:, l].reshape(N_LAT_SEQ, PAST_LEN, BRANCH))
        a_out = _da_attention(z_a, da_lambda[l], da_norm[l], lam_init, da_tabs, kv_lat)
        new_k.append(k_new)
        new_v.append(v_new)

        h0 = state_s5[:, l].transpose(2, 1, 4, 3, 0).reshape(S5_GROUPS, 4, S5_STATE, N_LAT_SEQ)
        h0 = jnp.pad(h0, ((0, 0), (0, 0), (0, 0), (0, 128 - N_LAT_SEQ)))
        y_all, fin = _s5_scan(_s5_chunk_lanes(u_bf), *s5_tabs, h0, l)
        b_out = (_s5_token_rows(y_all), z_b, s5_d[l].reshape(1, BRANCH), s5_w_glu[l].astype(BF16))
        fin = fin[..., :N_CTX_SEQ].reshape(S5_GROUPS, 2, 2, S5_STATE, N_CTX_SEQ)
        new_s5.append(fin.transpose(4, 1, 0, 3, 2))

        lb = jnp.concatenate([lb_all[l, 0].reshape(HG_HEADS, HG_DK), lb_all[l, 1].reshape(HG_HEADS, HG_DK)],
                             axis=-1).reshape(1, HG_W)
        head_eye = jnp.eye(HG_HEADS, dtype=F32)
        s0 = state_hgrn[:, l].transpose(0, 2, 1, 3, 4).reshape(N_LAT_SEQ, HG_HEADS, HG_HEAD_W, 1, HG_DK)
        s0 = (s0 * head_eye[None, :, None, :, None]).reshape(N_LAT_SEQ, HG_HEADS, HG_HEAD_W, BRANCH)
        s_f, s_b = _hg_states(z_c, lb, s0)
        c_out, s_fin = _hg_main(z_c, s_f, s_b, lb, jnp.tile(hg_norm[l].reshape(1, HG_DK), (1, HG_HEADS)))
        new_hg.append(s_fin.reshape(N_CTX_SEQ, HG_HEADS, 2, HG_DK, HG_DK).transpose(0, 2, 1, 3, 4))

        wq, wk, wv, qn = _mla_weights(mla_w_uq[l], mla_w_ukv[l], mla_q_norm[l])
        q, ckv_n, kr = _mla_prep(z_d, mla_tabs, qn, mla_kv_norm[l].reshape(1, MLA_KV_RANK), wq)
        kr_cache = jnp.pad(cache_mla_krope[:, l], ((0, 0), (0, 0), (MLA_NOPE, 128 - MLA_NOPE - MLA_ROPE)))
        k_ctx, v_ctx, k_lat, v_lat = _mla_kv(ckv_n, kr, cache_mla_ckv[:, l], kr_cache, wk, wv)
        d_out = _mla_attention(z_d, q, k_ctx, v_ctx, k_lat, v_lat)
        new_ckv.append(ckv_n[:N_CTX].reshape(N_CTX_SEQ, CTX_LEN, MLA_KV_RANK))
        new_kr.append(kr[:N_CTX, MLA_NOPE:MLA_NOPE + MLA_ROPE].reshape(N_CTX_SEQ, CTX_LEN, MLA_ROPE))

        xs = _out_proj(a_out, b_out, c_out, d_out, xs, mod, w_out, l,
                       final_norm.reshape(1, D_MODEL), final=(l == DEPTH - 1))
        xs = tuple(xs) if l == DEPTH - 1 else (xs,)
    y_prompt = xs[0].reshape(N_CTX_SEQ, CTX_LEN, D_MODEL)
    y_sample = xs[1].reshape(N_LAT_SEQ, LAT_LEN, D_MODEL)
    st = lambda parts: jnp.stack(parts, axis=1)
    heads_last = lambda kv: kv.reshape(N_CTX_SEQ, DEPTH, DA_HEADS, 2 * DA_QK, CTX_LEN).transpose(0, 1, 4, 2, 3)
    return (y_prompt, y_sample, heads_last(st(new_k)), heads_last(st(new_v)), st(new_s5), st(new_hg), st(new_ckv), st(new_kr))
```

```python
import functools
import math

import numpy as np

import jax
import jax.numpy as jnp
from jax import lax
from jax.experimental import pallas as pl
from jax.experimental.pallas import tpu as pltpu

F32 = jnp.float32
BF16 = jnp.bfloat16

D_MODEL = 1024
DEPTH = 2
N_CTX_SEQ = 16
CTX_LEN = 256
N_LAT_SEQ = 2
LAT_LEN = 2048
PAST_LEN = 256
GRID_W = 64
N_CTX = N_CTX_SEQ * CTX_LEN
N_LAT = N_LAT_SEQ * LAT_LEN
N_TOK = N_CTX + N_LAT
BRANCH = 256
EPS = 1e-6
ROPE_BASE = 10000.0
ROW_TILE = 256
LAT_TILES = LAT_LEN // ROW_TILE
N_TILES = N_TOK // ROW_TILE
CTX_TILES = N_CTX // ROW_TILE
VMEM_LIMIT = 48 * 1024 * 1024
IN_PROJ_VMEM_LIMIT = 56 * 1024 * 1024
LAT_Q_TILE = 512
LAT_Q_TILES = LAT_LEN // LAT_Q_TILE
BIG_TILE = 512
CTX_SEQ_PER_STEP = 2

DA_HEADS = 4
DA_QK = 32
MLA_HEADS = 4
MLA_NOPE = 64
MLA_ROPE = 32
MLA_Q_RANK = 192
MLA_KV_RANK = 128
S5_GROUPS = 16
S5_CH = 16
S5_STATE = 64
S5_CHUNK = 16
HG_HEADS = 4
HG_DK = 64

W_A = 1024
W_B = 512
W_C = 1536
W_D = 768
W_ALL = W_A + W_B + W_C + W_D


def _cparams(*sem):
    return pltpu.CompilerParams(dimension_semantics=sem, vmem_limit_bytes=VMEM_LIMIT)


def _tile_seq(i, tile=ROW_TILE):
    return jnp.where(i < N_CTX // tile, 0, 1 + (i - N_CTX // tile) // (LAT_LEN // tile))


def _silu(x):
    return x * (1.0 / (1.0 + jnp.exp(-x)))


def _dot(a, b):
    return jnp.dot(a, b, preferred_element_type=F32)


def _dot_nt(a, b):
    return lax.dot_general(a, b, (((1,), (1,)), ((), ())), preferred_element_type=F32)


def _mod_body(c_ref, w_ref, b_ref, o_ref):
    c = _silu(c_ref[...]).astype(BF16)
    o_ref[0] = _dot(c, w_ref[0].astype(BF16)) + b_ref[0]


def _modulation(c_rows, w_mod, b_mod):
    tn = 768
    return pl.pallas_call(
        _mod_body,
        grid=(DEPTH, 3 * D_MODEL // tn),
        in_specs=[pl.BlockSpec((8, D_MODEL), lambda l, j: (0, 0)),
                  pl.BlockSpec((1, D_MODEL, tn), lambda l, j: (l, 0, j)),
                  pl.BlockSpec((1, 1, tn), lambda l, j: (l, 0, j))],
        out_specs=pl.BlockSpec((1, 8, tn), lambda l, j: (l, 0, j)),
        out_shape=jax.ShapeDtypeStruct((DEPTH, 8, 3 * D_MODEL), F32),
        compiler_params=_cparams("parallel", "parallel"),
        name="modulation",
    )(c_rows, w_mod, b_mod.reshape(DEPTH, 1, 3 * D_MODEL))


def _split_rows(i, ctx_ref, lat_ref):
    return jnp.where(i < N_CTX // ctx_ref.shape[0], ctx_ref[...], lat_ref[...])


def _ctx_tile_spec(w, tile=ROW_TILE):
    return pl.BlockSpec((tile, w), lambda i: (jnp.minimum(i, N_CTX // tile - 1), 0))


def _lat_tile_spec(w, tile=ROW_TILE):
    return pl.BlockSpec((tile, w), lambda i: (jnp.maximum(i - N_CTX // tile, 0), 0))


def _in_proj_body(*refs, split):
    if split:
        xc_ref, xl_ref, mod_ref, w_ref, oa, ob, oc, od, ok, ov, ou = refs
        x = _split_rows(pl.program_id(0), xc_ref, xl_ref)
    else:
        x_ref, mod_ref, w_ref, oa, ob, oc, od, ok, ov, ou = refs
        x = x_ref[...]
    xn = x * lax.rsqrt(jnp.mean(x * x, axis=-1, keepdims=True) + EPS)
    mod = mod_ref[0]
    h = (xn * (1.0 + mod[1:2]) + mod[0:1]).astype(BF16)
    off = 0
    for o in (oa, ob, oc, od):
        w = o.shape[-1]
        o[...] = _dot(h, w_ref[0, :, off:off + w])
        off += w
    ou[...] = ob[:, :BRANCH].astype(BF16)

    @pl.when(pl.program_id(0) < N_CTX // BIG_TILE)
    def _():
        for t in range(BIG_TILE // CTX_LEN):
            r = slice(t * CTX_LEN, (t + 1) * CTX_LEN)
            ok[t] = oa[r, BRANCH:2 * BRANCH].T
            ov[t] = oa[r, 2 * BRANCH:3 * BRANCH].T


def _in_proj(xs, mod, w_all, l):
    widths = (W_A, W_B, W_C, W_D)
    split = len(xs) == 2
    x_specs = ([_ctx_tile_spec(D_MODEL, BIG_TILE), _lat_tile_spec(D_MODEL, BIG_TILE)] if split
               else [pl.BlockSpec((BIG_TILE, D_MODEL), lambda i: (i, 0))])
    return pl.pallas_call(
        functools.partial(_in_proj_body, split=split),
        grid=(N_TOK // BIG_TILE,),
        in_specs=x_specs + [pl.BlockSpec((1, 3, D_MODEL), lambda i: (_tile_seq(i, BIG_TILE), 0, 0)),
                            pl.BlockSpec((1, D_MODEL, W_ALL), lambda i: (l, 0, 0))],
        out_specs=[pl.BlockSpec((BIG_TILE, w), lambda i: (i, 0)) for w in widths]
        + [pl.BlockSpec((BIG_TILE // CTX_LEN, BRANCH, CTX_LEN), lambda i: (jnp.minimum(i, N_CTX // BIG_TILE - 1), 0, 0))] * 2
        + [pl.BlockSpec((BIG_TILE, BRANCH), lambda i: (i, 0))],
        out_shape=[jax.ShapeDtypeStruct((N_TOK, w), F32) for w in widths]
        + [jax.ShapeDtypeStruct((N_CTX_SEQ, BRANCH, CTX_LEN), F32)] * 2 + [jax.ShapeDtypeStruct((N_TOK, BRANCH), BF16)],
        compiler_params=pltpu.CompilerParams(dimension_semantics=("arbitrary",), vmem_limit_bytes=IN_PROJ_VMEM_LIMIT),
        name="in_proj",
    )(*xs, mod, w_all)


def _arrange_body(wt_ref, o_ref):
    w = wt_ref[0]
    c = _IN_OFF
    rows = lambda name: w[c[name][0]:c[name][-1] + 1]
    zero = lambda n: jnp.zeros((n, w.shape[1]), F32)

    def per_head(x, y):
        xs, ys = rows(x), rows(y)
        return [p for h in range(HG_HEADS) for p in (xs[h * HG_DK:(h + 1) * HG_DK], ys[h * HG_DK:(h + 1) * HG_DK])]
    pieces = ([w[0:W_A + W_B]] + per_head("hg_q", "hg_q") + per_head("hg_ff", "hg_fb") + [rows("hg_i"), rows("hg_g")]
              + [rows("mla_cq"), zero(256 - MLA_Q_RANK), rows("mla_ckv"), zero(MLA_NOPE), rows("mla_kr"),
                 zero(128 - MLA_NOPE - MLA_ROPE), rows("mla_g")])
    o_ref[0] = jnp.concatenate(pieces, axis=0).T.astype(BF16)


def _arrange_w_in(w_in):
    lanes = 256
    wt = jnp.swapaxes(w_in, 1, 2)
    return pl.pallas_call(
        _arrange_body,
        grid=(DEPTH, D_MODEL // lanes),
        in_specs=[pl.BlockSpec((1, wt.shape[1], lanes), lambda l, i: (l, 0, i))],
        out_specs=pl.BlockSpec((1, lanes, W_ALL), lambda l, i: (l, i, 0)),
        out_shape=jax.ShapeDtypeStruct((DEPTH, D_MODEL, W_ALL), BF16),
        compiler_params=_cparams("parallel", "parallel"),
        name="arrange_w_in",
    )(wt)


def _out_proj_body(*refs, split_in, final):
    ac_ref, al_ref, y_ref, u_ref, g_ref, dsk_ref, wglu_ref, c_ref, dc_ref, dl_ref = refs[:10]
    i = pl.program_id(0)
    if split_in:
        xc_ref, xl_ref, mod_ref, w_ref, fn_ref = refs[10:15]
        x = _split_rows(i, xc_ref, xl_ref)
    else:
        x_ref, mod_ref, w_ref, fn_ref = refs[10:14]
        x = x_ref[...]
    b_out = _s5_gated(y_ref[...], u_ref[...], g_ref[...], dsk_ref[...], wglu_ref[...])
    branches = (_split_rows(i, ac_ref, al_ref), b_out, c_ref[...], _split_rows(i, dc_ref, dl_ref))
    acc = None
    for j, r in enumerate(branches):
        t = _dot(r.astype(BF16), w_ref[0, j * BRANCH:(j + 1) * BRANCH, :].astype(BF16))
        acc = t if acc is None else acc + t
    x = x + mod_ref[0][2:3] * acc
    if not final:
        refs[-1][...] = x
        return
    y = x * lax.rsqrt(jnp.mean(x * x, axis=-1, keepdims=True) + EPS) * fn_ref[...]
    yc_ref, yl_ref = refs[-2:]

    @pl.when(i < N_CTX // BIG_TILE)
    def _():
        yc_ref[...] = y

    @pl.when(i >= N_CTX // BIG_TILE)
    def _():
        yl_ref[...] = y


def _out_proj(a, s5, c, d, xs, mod, w_out, l, final_norm, final):
    y_ssm, z_b, d_skip, w_glu = s5
    br = pl.BlockSpec((BIG_TILE, BRANCH), lambda i: (i, 0))
    s5_specs = [br, br, pl.BlockSpec((BIG_TILE, BRANCH), lambda i: (i, 1)),
                pl.BlockSpec((1, BRANCH), lambda i: (0, 0)), pl.BlockSpec((BRANCH, 2 * BRANCH), lambda i: (0, 0))]
    pair = [_ctx_tile_spec(BRANCH, BIG_TILE), _lat_tile_spec(BRANCH, BIG_TILE)]
    split_in = len(xs) == 2
    x_specs = ([_ctx_tile_spec(D_MODEL, BIG_TILE), _lat_tile_spec(D_MODEL, BIG_TILE)] if split_in
               else [pl.BlockSpec((BIG_TILE, D_MODEL), lambda i: (i, 0))])
    if final:
        out_specs = [_ctx_tile_spec(D_MODEL, BIG_TILE), _lat_tile_spec(D_MODEL, BIG_TILE)]
        out_shape = [jax.ShapeDtypeStruct((N_CTX, D_MODEL), F32), jax.ShapeDtypeStruct((N_LAT, D_MODEL), F32)]
    else:
        out_specs = pl.BlockSpec((BIG_TILE, D_MODEL), lambda i: (i, 0))
        out_shape = jax.ShapeDtypeStruct((N_TOK, D_MODEL), F32)
    return pl.pallas_call(
        functools.partial(_out_proj_body, split_in=split_in, final=final),
        grid=(N_TOK // BIG_TILE,),
        in_specs=pair + s5_specs + [br] + pair + x_specs + [
            pl.BlockSpec((1, 3, D_MODEL), lambda i: (_tile_seq(i, BIG_TILE), 0, 0)),
            pl.BlockSpec((1, D_MODEL, D_MODEL), lambda i: (l, 0, 0)),
            pl.BlockSpec((1, D_MODEL), lambda i: (0, 0))],
        out_specs=out_specs,
        out_shape=out_shape,
        compiler_params=_cparams("arbitrary"),
        name="out_proj",
    )(*a, y_ssm, z_b, z_b, d_skip, w_glu, c, *d, *xs, mod, w_out, final_norm)


LOG2E = 1.4426950408889634


def _exp2_rows(s):
    e = jnp.exp2(s - jnp.max(s, axis=-1, keepdims=True))
    return e, jnp.sum(e, axis=-1, keepdims=True)


def _rope(x, cos, sin_lo, sin_hi):
    w = x.shape[-1]
    return x * cos + pltpu.roll(x, w - 8, 1) * sin_lo + pltpu.roll(x, 8, 1) * sin_hi


def _da_kv_body(k_ref, v_ref, cos_ref, slo_ref, shi_ref, ck_ref, cv_ref, ko_ref, vo_ref):
    j = pl.program_id(1)

    @pl.when(j < LAT_TILES)
    def _():
        ko_ref[0] = _rope(k_ref[...], cos_ref[...], slo_ref[...], shi_ref[...]).astype(BF16)
        vo_ref[0] = v_ref[...].astype(BF16)

    @pl.when(j == LAT_TILES)
    def _():
        ko_ref[0] = ck_ref[0].astype(BF16)
        vo_ref[0] = cv_ref[0].astype(BF16)


def _da_latent_kv(z_a, tabs, cache_k, cache_v):
    def rows(col):
        return pl.BlockSpec(
            (ROW_TILE, BRANCH),
            lambda b, j: (CTX_TILES + b * LAT_TILES + jnp.minimum(j, LAT_TILES - 1), col))
    tab = pl.BlockSpec((ROW_TILE, BRANCH), lambda b, j: (jnp.minimum(j, LAT_TILES - 1), 0))
    cache = pl.BlockSpec((1, PAST_LEN, BRANCH), lambda b, j: (b, 0, 0))
    out = pl.BlockSpec((1, ROW_TILE, BRANCH), lambda b, j: (b, j, 0))
    shp = jax.ShapeDtypeStruct((N_LAT_SEQ, LAT_LEN + PAST_LEN, BRANCH), BF16)
    return pl.pallas_call(
        _da_kv_body,
        grid=(N_LAT_SEQ, LAT_TILES + 1),
        in_specs=[rows(1), rows(2), tab, tab, tab, cache, cache],
        out_specs=[out, out],
        out_shape=[shp, shp],
        compiler_params=_cparams("parallel", "parallel"),
        name="da_kv",
    )(z_a, z_a, *tabs, cache_k, cache_v)


def _da_attn_body(lam_ref, ng_ref, q_ref, *rest, rope, lam_init):
    if rope:
        cos_ref, slo_ref, shi_ref, k_ref, v_ref, g_ref, o_ref = rest
        q = _rope(q_ref[...], cos_ref[...], slo_ref[...], shi_ref[...])
        o_ref[...] = _da_attn_tile(lam_ref, ng_ref, q, k_ref[0], v_ref[0], g_ref[...], lam_init)
    else:
        k_ref, v_ref, g_ref, o_ref = rest
        for t in range(q_ref.shape[0] // CTX_LEN):
            r = slice(t * CTX_LEN, (t + 1) * CTX_LEN)
            o_ref[r, :] = _da_attn_tile(lam_ref, ng_ref, q_ref[r, :], k_ref[r, :].astype(BF16),
                                        v_ref[r, :].astype(BF16), g_ref[r, :], lam_init)


def _da_attn_tile(lam_ref, ng_ref, q, k, v, g, lam_init):
    q = q * (DA_QK ** -0.5 * LOG2E)
    lv = lam_ref[...]
    lam = (jnp.exp(jnp.sum(lv[0:1] * lv[1:2], axis=-1, keepdims=True))
           - jnp.exp(jnp.sum(lv[2:3] * lv[3:4], axis=-1, keepdims=True)) + lam_init)
    lane = lax.broadcasted_iota(jnp.int32, (1, BRANCH), 1)
    acc = jnp.zeros(q.shape, F32)
    for h in range(DA_HEADS):
        q1 = jnp.where(lane // DA_QK == 2 * h, q, 0.0).astype(BF16)
        q2 = jnp.where(lane // DA_QK == 2 * h + 1, q, 0.0).astype(BF16)
        s12 = _dot_nt(jnp.concatenate([q1, q2], axis=0), k)
        e1, l1 = _exp2_rows(s12[:q.shape[0]])
        e2, l2 = _exp2_rows(s12[q.shape[0]:])
        a = (e1 - (lam * l1 / l2) * e2).astype(BF16)
        acc = jnp.where(lane // (2 * DA_QK) == h, _dot(a, v) * (1.0 / l1), acc)
    sq = acc * acc
    ms = jnp.zeros(q.shape, F32)
    for h in range(DA_HEADS):
        hm = lane // (2 * DA_QK) == h
        ms = jnp.where(hm, jnp.sum(jnp.where(hm, sq, 0.0), axis=-1, keepdims=True), ms)
    o = acc * lax.rsqrt(ms * (1.0 / (2 * DA_QK)) + EPS) * (ng_ref[...] * (1.0 - lam_init))
    return o * _silu(g)


def _da_attention(z_a, lam_vec, norm_g, lam_init, tabs, kv_lat):
    ng = jnp.tile(norm_g.reshape(1, 2 * DA_QK), (1, DA_HEADS))
    small = [pl.BlockSpec((4, DA_QK), lambda *_: (0, 0)), pl.BlockSpec((1, BRANCH), lambda *_: (0, 0))]

    rows = CTX_SEQ_PER_STEP * CTX_LEN

    def col(c):
        return pl.BlockSpec((rows, BRANCH), lambda i: (i, c))
    ctx = pl.pallas_call(
        functools.partial(_da_attn_body, rope=False, lam_init=lam_init),
        grid=(N_CTX // rows,),
        in_specs=small + [col(0), col(1), col(2), col(3)],
        out_specs=pl.BlockSpec((rows, BRANCH), lambda i: (i, 0)),
        out_shape=jax.ShapeDtypeStruct((N_CTX, BRANCH), F32),
        compiler_params=_cparams("parallel"),
        name="da_attn_ctx",
    )(lam_vec, ng, z_a, z_a, z_a, z_a)

    def lcol(c):
        return pl.BlockSpec((LAT_Q_TILE, BRANCH), lambda b, j: (N_CTX // LAT_Q_TILE + b * LAT_Q_TILES + j, c))
    tab = pl.BlockSpec((LAT_Q_TILE, BRANCH), lambda b, j: (j, 0))
    kvs = pl.BlockSpec((1, LAT_LEN + PAST_LEN, BRANCH), lambda b, j: (b, 0, 0))
    lat = pl.pallas_call(
        functools.partial(_da_attn_body, rope=True, lam_init=lam_init),
        grid=(N_LAT_SEQ, LAT_Q_TILES),
        in_specs=small + [lcol(0), tab, tab, tab, kvs, kvs, lcol(3)],
        out_specs=pl.BlockSpec((LAT_Q_TILE, BRANCH), lambda b, j: (b * LAT_Q_TILES + j, 0)),
        out_shape=jax.ShapeDtypeStruct((N_LAT, BRANCH), F32),
        compiler_params=_cparams("parallel", "parallel"),
        name="da_attn_lat",
    )(lam_vec, ng, z_a, *tabs, kv_lat[0], kv_lat[1], z_a)
    return ctx, lat


MLA_HEAD_PAD = 128
MLA_QW = MLA_HEADS * MLA_HEAD_PAD


def _mla_prep_body(cq_ref, ckv_ref, kr_ref, ck_t, sk_lo, sk_hi, qn_ref, kvn_ref, wq_ref, q_out, ckv_out, kr_out):
    cq = cq_ref[...]
    ms = jnp.sum(cq * cq, axis=-1, keepdims=True) * (1.0 / MLA_Q_RANK)
    qn = (cq * lax.rsqrt(ms + EPS) * qn_ref[...]).astype(BF16)
    heads = lambda t: jnp.concatenate([t[...]] * MLA_HEADS, axis=-1)
    q = _rope(_dot(qn, wq_ref[...]), heads(ck_t), heads(sk_lo), heads(sk_hi))
    q_out[...] = (q * ((MLA_NOPE + MLA_ROPE) ** -0.5 * LOG2E)).astype(BF16)
    ckv = ckv_ref[...]
    ckv_out[...] = ckv * lax.rsqrt(jnp.mean(ckv * ckv, axis=-1, keepdims=True) + EPS) * kvn_ref[...]
    kr_out[...] = _rope(kr_ref[...], ck_t[...], sk_lo[...], sk_hi[...])


def _mla_prep(z_d, tabs, q_norm_pad, kv_norm, wq):
    ctx_tiles, lat_tiles = N_CTX // BIG_TILE, LAT_LEN // BIG_TILE

    def tab(w):
        return pl.BlockSpec((BIG_TILE, w), lambda i: (jnp.where(i < ctx_tiles, lat_tiles, (i - ctx_tiles) % lat_tiles), 0))

    def col(w, c):
        return pl.BlockSpec((BIG_TILE, w), lambda i: (i, c))

    def const(shape):
        return pl.BlockSpec(shape, lambda i: (0, 0))
    return pl.pallas_call(
        _mla_prep_body,
        grid=(N_TOK // BIG_TILE,),
        in_specs=[col(256, 0), col(128, 2), col(128, 3),
                  tab(128), tab(128), tab(128),
                  const((1, 256)), const((1, 128)), const((256, MLA_QW))],
        out_specs=[col(MLA_QW, 0), col(128, 0), col(128, 0)],
        out_shape=[jax.ShapeDtypeStruct((N_TOK, MLA_QW), BF16),
                   jax.ShapeDtypeStruct((N_TOK, 128), F32),
                   jax.ShapeDtypeStruct((N_TOK, 128), F32)],
        compiler_params=_cparams("parallel"),
        name="mla_prep",
    )(z_d, z_d, z_d, *tabs, q_norm_pad, kv_norm, wq)


def _mla_kv_math(ckv, kr, wk_ref, wv_ref, k_out, v_out):
    c = ckv.astype(BF16)
    k_out[...] = (_dot(c, wk_ref[...]) + jnp.concatenate([kr] * MLA_HEADS, axis=-1)).astype(BF16).reshape(k_out.shape)
    v_out[...] = _dot(c, wv_ref[...]).astype(BF16).reshape(v_out.shape)


def _mla_kv_ctx_body(ckv_ref, kr_ref, wk_ref, wv_ref, k_out, v_out):
    _mla_kv_math(ckv_ref[...], kr_ref[...], wk_ref, wv_ref, k_out, v_out)


def _mla_kv_lat_body(ckv_ref, kr_ref, cckv_ref, ckr_ref, wk_ref, wv_ref, k_out, v_out):
    j = pl.program_id(1)

    @pl.when(j < LAT_TILES)
    def _():
        _mla_kv_math(ckv_ref[...], kr_ref[...], wk_ref, wv_ref, k_out, v_out)

    @pl.when(j == LAT_TILES)
    def _():
        _mla_kv_math(cckv_ref[0], ckr_ref[0], wk_ref, wv_ref, k_out, v_out)


def _mla_kv(ckv, kr, cache_ckv, cache_kr, wk, wv):
    weights = [pl.BlockSpec((128, MLA_QW), lambda *_: (0, 0)), pl.BlockSpec((128, BRANCH), lambda *_: (0, 0))]
    k_ctx, v_ctx = pl.pallas_call(
        _mla_kv_ctx_body,
        grid=(N_CTX // BIG_TILE,),
        in_specs=[pl.BlockSpec((BIG_TILE, 128), lambda i: (i, 0)), pl.BlockSpec((BIG_TILE, 128), lambda i: (i, 0))] + weights,
        out_specs=[pl.BlockSpec((BIG_TILE, MLA_QW), lambda i: (i, 0)),
                   pl.BlockSpec((BIG_TILE, BRANCH), lambda i: (i, 0))],
        out_shape=[jax.ShapeDtypeStruct((N_CTX, MLA_QW), BF16), jax.ShapeDtypeStruct((N_CTX, BRANCH), BF16)],
        compiler_params=_cparams("parallel"),
        name="mla_kv_ctx",
    )(ckv, kr, wk, wv)
    rows = pl.BlockSpec((ROW_TILE, 128), lambda b, j: (CTX_TILES + b * LAT_TILES + jnp.minimum(j, LAT_TILES - 1), 0))
    cache = pl.BlockSpec((1, PAST_LEN, 128), lambda b, j: (b, 0, 0))
    lk = LAT_LEN + PAST_LEN
    k_lat, v_lat = pl.pallas_call(
        _mla_kv_lat_body,
        grid=(N_LAT_SEQ, LAT_TILES + 1),
        in_specs=[rows, rows, cache, cache] + weights,
        out_specs=[pl.BlockSpec((1, ROW_TILE, MLA_QW), lambda b, j: (b, j, 0)),
                   pl.BlockSpec((1, ROW_TILE, BRANCH), lambda b, j: (b, j, 0))],
        out_shape=[jax.ShapeDtypeStruct((N_LAT_SEQ, lk, MLA_QW), BF16), jax.ShapeDtypeStruct((N_LAT_SEQ, lk, BRANCH), BF16)],
        compiler_params=_cparams("parallel", "parallel"),
        name="mla_kv_lat",
    )(ckv, kr, cache_ckv, cache_kr, wk, wv)
    return k_ctx, v_ctx, k_lat, v_lat


def _mla_attn_body(q_ref, k_ref, v_ref, g_ref, o_ref, *, ctx):
    if ctx:
        for t in range(q_ref.shape[0] // CTX_LEN):
            r = slice(t * CTX_LEN, (t + 1) * CTX_LEN)
            o_ref[r, :] = _mla_attn_tile(q_ref[r, :], k_ref[r, :], v_ref[r, :], g_ref[r, :])
    else:
        o_ref[...] = _mla_attn_tile(q_ref[...], k_ref[0], v_ref[0], g_ref[...])


def _mla_attn_tile(q, k, v, g):
    lane = lax.broadcasted_iota(jnp.int32, (1, BRANCH), 1)
    acc = jnp.zeros((q.shape[0], BRANCH), F32)
    for h in range(MLA_HEADS):
        sl = slice(h * MLA_HEAD_PAD, (h + 1) * MLA_HEAD_PAD)
        e, l = _exp2_rows(_dot_nt(q[:, sl], k[:, sl]))
        acc = jnp.where(lane // 64 == h, _dot(e.astype(BF16), v) * (1.0 / l), acc)
    return acc * _silu(g)


def _mla_attention(z_d, q, k_ctx, v_ctx, k_lat, v_lat):
    rows = CTX_SEQ_PER_STEP * CTX_LEN
    ctx = pl.pallas_call(
        functools.partial(_mla_attn_body, ctx=True),
        grid=(N_CTX // rows,),
        in_specs=[pl.BlockSpec((rows, MLA_QW), lambda i: (i, 0)),
                  pl.BlockSpec((rows, MLA_QW), lambda i: (i, 0)),
                  pl.BlockSpec((rows, BRANCH), lambda i: (i, 0)),
                  pl.BlockSpec((rows, BRANCH), lambda i: (i, 2))],
        out_specs=pl.BlockSpec((rows, BRANCH), lambda i: (i, 0)),
        out_shape=jax.ShapeDtypeStruct((N_CTX, BRANCH), F32),
        compiler_params=_cparams("parallel"),
        name="mla_attn_ctx",
    )(q, k_ctx, v_ctx, z_d)
    lk = LAT_LEN + PAST_LEN
    lat = pl.pallas_call(
        functools.partial(_mla_attn_body, ctx=False),
        grid=(N_LAT_SEQ, LAT_Q_TILES),
        in_specs=[pl.BlockSpec((LAT_Q_TILE, MLA_QW), lambda b, j: (N_CTX // LAT_Q_TILE + b * LAT_Q_TILES + j, 0)),
                  pl.BlockSpec((1, lk, MLA_QW), lambda b, j: (b, 0, 0)),
                  pl.BlockSpec((1, lk, BRANCH), lambda b, j: (b, 0, 0)),
                  pl.BlockSpec((LAT_Q_TILE, BRANCH), lambda b, j: (N_CTX // LAT_Q_TILE + b * LAT_Q_TILES + j, 2))],
        out_specs=pl.BlockSpec((LAT_Q_TILE, BRANCH), lambda b, j: (b * LAT_Q_TILES + j, 0)),
        out_shape=jax.ShapeDtypeStruct((N_LAT, BRANCH), F32),
        compiler_params=_cparams("parallel", "parallel"),
        name="mla_attn_lat",
    )(q, k_lat, v_lat, z_d)
    return ctx, lat


S5_TAP = S5_CHUNK * S5_CH
S5_NCHUNK = N_TOK // S5_CHUNK
S5_CTX_CH = N_CTX // S5_CHUNK
S5_CTX_SEQ_CH = CTX_LEN // S5_CHUNK
S5_LAT_SEQ_CH = LAT_LEN // S5_CHUNK
S5_SCAN_STEPS = S5_LAT_SEQ_CH.bit_length() - 1
S5_TABLE_GROUPS = 4


def _s5_body(x_ref, mt_ref, bst_ref, cot_ref, a_ref, h0_ref, y_ref, fin_ref):
    x = x_ref[...].reshape(S5_TAP, S5_NCHUNK)
    y = _dot(mt_ref[0, 0], x)
    s = _dot(bst_ref[0, 0], x)
    lane = lax.broadcasted_iota(jnp.int32, (1, S5_NCHUNK), 1)
    is_lat = lane >= S5_CTX_CH
    pos_f = jnp.where(is_lat, (lane - S5_CTX_CH) & (S5_LAT_SEQ_CH - 1), lane & (S5_CTX_SEQ_CH - 1))
    pos_b = jnp.where(is_lat, S5_LAT_SEQ_CH - 1, S5_CTX_SEQ_CH - 1) - pos_f
    hin = []
    for d in range(2):
        n = S5_STATE
        sre, sim = s[2 * d * n:(2 * d + 1) * n], s[(2 * d + 1) * n:(2 * d + 2) * n]
        are = jnp.concatenate([a_ref[0, 0, 2 * d]] * (S5_NCHUNK // 128), axis=-1)
        aim = jnp.concatenate([a_ref[0, 0, 2 * d + 1]] * (S5_NCHUNK // 128), axis=-1)
        pos = pos_f if d == 0 else pos_b
        h0r, h0i = jnp.zeros_like(sre), jnp.zeros_like(sre)
        for b in range(N_LAT_SEQ):
            first = S5_CTX_CH + b * S5_LAT_SEQ_CH + (0 if d == 0 else S5_LAT_SEQ_CH - 1)
            h0r = jnp.where(lane == first, h0_ref[0, 2 * d][:, b:b + 1], h0r)
            h0i = jnp.where(lane == first, h0_ref[0, 2 * d + 1][:, b:b + 1], h0i)
        xr = sre + are * h0r - aim * h0i
        xi = sim + are * h0i + aim * h0r
        pr, pi = are, aim
        for j in range(S5_SCAN_STEPS):
            sh = 1 << j
            shift = sh if d == 0 else S5_NCHUNK - sh
            rr, ri = pltpu.roll(xr, shift, 1), pltpu.roll(xi, shift, 1)
            ok = pos >= sh
            xr, xi = (xr + jnp.where(ok, pr * rr - pi * ri, 0.0), xi + jnp.where(ok, pr * ri + pi * rr, 0.0))
            pr, pi = pr * pr - pi * pi, 2.0 * pr * pi
        last = lax.broadcasted_iota(jnp.int32, (1, 128), 1) * S5_CTX_SEQ_CH + (S5_CTX_SEQ_CH - 1 if d == 0 else 0)
        pick = jnp.where(lax.broadcasted_iota(jnp.int32, (S5_CTX_CH, 1), 0) == last, 1.0, 0.0)
        fin_ref[0, 2 * d] = _dot_sel(xr[:, :S5_CTX_CH], pick)
        fin_ref[0, 2 * d + 1] = _dot_sel(xi[:, :S5_CTX_CH], pick)
        one = 1 if d == 0 else S5_NCHUNK - 1
        hin.append(jnp.where(pos >= 1, pltpu.roll(xr, one, 1), h0r))
        hin.append(jnp.where(pos >= 1, pltpu.roll(xi, one, 1), h0i))
    y = y + _dot(cot_ref[0, 0], jnp.concatenate(hin, axis=0).astype(BF16))
    y_ref[...] = y.reshape(S5_CHUNK, S5_CH, S5_NCHUNK)


def _s5_scan(x_all, mt, bst, cot, a16, h0, l):
    g = S5_GROUPS
    sq = pl.BlockSpec((1, 1, S5_TAP, S5_TAP), lambda i: (l, i, 0, 0))
    st = pl.BlockSpec((1, 4, S5_STATE, 128), lambda i: (i, 0, 0, 0))
    return pl.pallas_call(
        _s5_body,
        grid=(g,),
        in_specs=[pl.BlockSpec((S5_CHUNK, S5_CH, S5_NCHUNK), lambda i: (0, i, 0)), sq, sq, sq,
                  pl.BlockSpec((1, 1, 4, S5_STATE, 128), lambda i: (l, i, 0, 0, 0)), st],
        out_specs=[pl.BlockSpec((S5_CHUNK, S5_CH, S5_NCHUNK), lambda i: (0, i, 0)),
                   pl.BlockSpec((1, 4, S5_STATE, 128), lambda i: (i, 0, 0, 0))],
        out_shape=[jax.ShapeDtypeStruct((S5_CHUNK, BRANCH, S5_NCHUNK), F32),
                   jax.ShapeDtypeStruct((g, 4, S5_STATE, 128), F32)],
        compiler_params=_cparams("parallel"),
        name="s5_scan",
    )(x_all, mt, bst, cot, a16, h0)


def _s5_gated(y_ssm, u, g, d_skip, w_glu):
    y = u * d_skip + y_ssm
    ge = 0.5 * y * (1.0 + jnp.tanh(0.7978845608028654 * (y + 0.044715 * (y * y * y))))
    gl = _dot(ge.astype(BF16), w_glu)
    return gl[:, :BRANCH] * (1.0 / (1.0 + jnp.exp(-gl[:, BRANCH:]))) * _silu(g)


HG_CHUNK = ROW_TILE
HG_W = 2 * HG_HEADS * HG_DK
HG_HEAD_W = 2 * HG_DK
HG_LAT_CHUNKS = LAT_LEN // HG_CHUNK
HG_CHUNKS = N_TOK // HG_CHUNK


def _hg_gates(z, lb):
    e = jnp.exp(-jnp.abs(z))
    r = 1.0 / (1.0 + e)
    sig_pos = jnp.where(z >= 0, r, e * r)
    sig_neg = jnp.where(z >= 0, e * r, r)
    return lb + (1.0 - lb) * sig_pos, (1.0 - lb) * sig_neg


def _bcast_row(x, period, r):
    n, w = x.shape
    if period >= 8:
        x3 = x.reshape(n // period, period, w)
        return jnp.broadcast_to(x3[:, r:r + 1, :], x3.shape).reshape(n, w)
    x3 = x.reshape(n // 8, 8, w)
    sub = lax.broadcasted_iota(jnp.int32, (1, 8, 1), 1)
    out = None
    for j in range(8 // period):
        b = jnp.broadcast_to(x3[:, j * period + r:j * period + r + 1, :], x3.shape)
        out = b if out is None else jnp.where(sub >= j * period, b, out)
    return out.reshape(n, w)


def _hg_scans(f, isb):
    n = f.shape[0]
    row = lax.broadcasted_iota(jnp.int32, (n, 1), 0)
    p, r = f, jnp.ones_like(f)
    levels = []
    h, sh = 1, 0
    while h < n:
        levels.append((h, sh, p, r))
        up = (row >> sh) & 1
        tot_p = jnp.where(isb == 1, _bcast_row(p, 2 * h, h), _bcast_row(p, 2 * h, h - 1))
        tot_r = jnp.where(isb == 1, _bcast_row(p, 2 * h, 0), _bcast_row(p, 2 * h, 2 * h - 1))
        p = p * jnp.where(up != isb, tot_p, 1.0)
        r = r * jnp.where(up == isb, tot_r, 1.0)
        h, sh = 2 * h, sh + 1
    return levels, p, r


def _hg_state_body(zf_ref, zb_ref, vf_ref, vb_ref, lb_ref, s0_ref, sf_out, sb_out, s_scr):
    i = pl.program_id(0)

    @pl.when(i % HG_LAT_CHUNKS == 0)
    def _():
        s_scr[...] = s0_ref[0]

    sf_out[0] = s_scr[:, 0:HG_DK, :]
    sb_out[0] = s_scr[:, HG_DK:, :]
    lane5 = lax.broadcasted_iota(jnp.int32, (1, HG_W), 1)
    isb = (lane5 >> 6) & 1
    z = jnp.where(isb == 1, zb_ref[...], zf_ref[...])
    f, k = _hg_gates(z, lb_ref[...])
    r, ptot = _hg_chunk_decay(f, isb)
    kt = k * r
    lane = lax.broadcasted_iota(jnp.int32, (1, BRANCH), 1)
    vf = vf_ref[...]
    vb = vb_ref[...]
    for hd in range(HG_HEADS):
        sl = slice(hd * HG_HEAD_W, (hd + 1) * HG_HEAD_W)
        kth = kt[:, sl].T.astype(BF16)
        hm = (lane >> 6) == hd
        d_f = _dot(kth, jnp.where(hm, vf, 0.0).astype(BF16))
        d_b = _dot(kth, jnp.where(hm, vb, 0.0).astype(BF16))
        ds = jnp.concatenate([d_f[:HG_DK], d_b[HG_DK:]], axis=0)
        pcol = jnp.broadcast_to(ptot[:, sl], (HG_HEAD_W, HG_HEAD_W)).T[:, 0:1]
        s_scr[hd] = s_scr[hd] * pcol + ds


def _hg_chunk_decay(f, isb):
    n = f.shape[0]
    row = lax.broadcasted_iota(jnp.int32, (n, 1), 0)
    dist = jnp.where(isb == 1, row, n - 1 - row)
    x = f
    sh = 1
    while sh < n:
        src = jnp.where(isb == 1, pltpu.roll(x, sh, 0), pltpu.roll(x, n - sh, 0))
        x = x * jnp.where(dist >= sh, src, 1.0)
        sh *= 2
    total = jnp.where(isb == 1, x[n - 1:n], x[0:1])
    nxt = jnp.where(isb == 1, pltpu.roll(x, 1, 0), pltpu.roll(x, n - 1, 0))
    return jnp.where(dist >= 1, nxt, 1.0), total


HG_LAT_STEPS = N_LAT_SEQ * HG_LAT_CHUNKS


def _hg_lat_rev(i):
    return (i // HG_LAT_CHUNKS) * HG_LAT_CHUNKS + (HG_LAT_CHUNKS - 1 - i % HG_LAT_CHUNKS)


def _hg_states(z_c, lb, s0):
    first = N_CTX_SEQ
    zz_f = pl.BlockSpec((HG_CHUNK, HG_W), lambda i: (first + i, 1))
    zz_b = pl.BlockSpec((HG_CHUNK, HG_W), lambda i: (first + _hg_lat_rev(i), 1))
    v_f = pl.BlockSpec((HG_CHUNK, BRANCH), lambda i: (first + i, 4))
    v_b = pl.BlockSpec((HG_CHUNK, BRANCH), lambda i: (first + _hg_lat_rev(i), 4))
    st = (HG_HEADS, HG_HEAD_W, BRANCH)
    half = (HG_HEADS, HG_DK, BRANCH)
    return pl.pallas_call(
        _hg_state_body,
        grid=(HG_LAT_STEPS,),
        in_specs=[zz_f, zz_b, v_f, v_b,
                  pl.BlockSpec((1, HG_W), lambda i: (0, 0)),
                  pl.BlockSpec((1,) + st, lambda i: (i // HG_LAT_CHUNKS, 0, 0, 0))],
        out_specs=[pl.BlockSpec((1,) + half, lambda i: (i, 0, 0, 0)),
                   pl.BlockSpec((1,) + half, lambda i: (_hg_lat_rev(i), 0, 0, 0))],
        out_shape=[jax.ShapeDtypeStruct((HG_LAT_STEPS,) + half, F32),
                   jax.ShapeDtypeStruct((HG_LAT_STEPS,) + half, F32)],
        scratch_shapes=[pltpu.VMEM(st, F32)],
        compiler_params=_cparams("arbitrary"),
        name="hg_states",
    )(z_c, z_c, z_c, z_c, lb, s0)


def _hg_main_body(qq_ref, zz_ref, v_ref, g_ref, sf_ref, sb_ref, lb_ref, ng_ref, o_ref, fin_ref):
    n = HG_CHUNK
    i = pl.program_id(0)
    qq = qq_ref[...]
    lane5 = lax.broadcasted_iota(jnp.int32, (1, HG_W), 1)
    isb = (lane5 >> 6) & 1
    f, k = _hg_gates(zz_ref[...], lb_ref[...])
    levels, pfull, rfull = _hg_scans(f, isb)
    row = lax.broadcasted_iota(jnp.int32, (n, 1), 0)
    col = lax.broadcasted_iota(jnp.int32, (1, n), 1)
    ops = [(qq.astype(BF16), k.astype(BF16), row == col)]
    for h, sh, p, r in levels:
        up = (row >> sh) & 1
        qt = jnp.where(up != isb, qq * p, 0.0).astype(BF16)
        kt = jnp.where(up == isb, k * r, 0.0).astype(BF16)
        ops.append((qt, kt, (row >> (sh + 1)) == (col >> (sh + 1))))
    qc = (qq * pfull).astype(BF16)
    v = v_ref[...]
    vb = v.astype(BF16)
    lane = lax.broadcasted_iota(jnp.int32, (1, BRANCH), 1)
    latent = i >= N_CTX_SEQ
    acc = jnp.zeros((n, BRANCH), F32)
    for hd in range(HG_HEADS):
        sl = slice(hd * HG_HEAD_W, (hd + 1) * HG_HEAD_W)
        a = jnp.zeros((n, n), F32)
        for qt, kt, mask in ops:
            a = a + jnp.where(mask, _dot_nt(qt[:, sl], kt[:, sl]), 0.0)
        s_in = jnp.concatenate([sf_ref[0, hd], sb_ref[0, hd]], axis=0)
        s_in = jnp.where(latent, s_in, 0.0).astype(BF16)
        o_h = _dot(a.astype(BF16), vb) + _dot(qc[:, sl], s_in)
        acc = jnp.where((lane >> 6) == hd, o_h, acc)
    sq = acc * acc
    ms = jnp.zeros((n, BRANCH), F32)
    for hd in range(HG_HEADS):
        hm = (lane >> 6) == hd
        ms = jnp.where(hm, jnp.sum(jnp.where(hm, sq, 0.0), axis=-1, keepdims=True), ms)
    o_ref[...] = acc * lax.rsqrt(ms * (1.0 / HG_DK) + EPS) * ng_ref[...] * _silu(g_ref[...])

    @pl.when(i < N_CTX_SEQ)
    def _():
        kt_full = k * rfull
        for hd in range(HG_HEADS):
            kth = kt_full[:, hd * HG_HEAD_W:(hd + 1) * HG_HEAD_W].T.astype(BF16)
            ds = _dot(kth, jnp.where((lane >> 6) == hd, v, 0.0).astype(BF16))
            fin_ref[0, hd] = ds[:, hd * HG_DK:(hd + 1) * HG_DK]


def _hg_main(z_c, s_f, s_b, lb, norm_g):
    half = (1, HG_HEADS, HG_DK, BRANCH)
    lat = lambda i: (jnp.maximum(i - N_CTX_SEQ, 0), 0, 0, 0)
    fin = (HG_HEADS, HG_HEAD_W, HG_DK)
    return pl.pallas_call(
        _hg_main_body,
        grid=(HG_CHUNKS,),
        in_specs=[pl.BlockSpec((HG_CHUNK, HG_W), lambda i: (i, 0)),
                  pl.BlockSpec((HG_CHUNK, HG_W), lambda i: (i, 1)),
                  pl.BlockSpec((HG_CHUNK, BRANCH), lambda i: (i, 4)),
                  pl.BlockSpec((HG_CHUNK, BRANCH), lambda i: (i, 5)),
                  pl.BlockSpec(half, lat),
                  pl.BlockSpec(half, lat),
                  pl.BlockSpec((1, HG_W), lambda i: (0, 0)),
                  pl.BlockSpec((1, BRANCH), lambda i: (0, 0))],
        out_specs=[pl.BlockSpec((HG_CHUNK, BRANCH), lambda i: (i, 0)),
                   pl.BlockSpec((1,) + fin, lambda i: (jnp.minimum(i, N_CTX_SEQ - 1), 0, 0, 0))],
        out_shape=[jax.ShapeDtypeStruct((N_TOK, BRANCH), F32),
                   jax.ShapeDtypeStruct((N_CTX_SEQ,) + fin, F32)],
        compiler_params=_cparams("arbitrary"),
        name="hg_main",
    )(z_c, z_c, z_c, z_c, s_f, s_b, lb, norm_g)


def _take_cols(w, plan):
    idx = np.concatenate([p[0] for p in plan]).astype(np.int32)
    sign = np.concatenate([np.broadcast_to(p[1], p[0].shape) for p in plan]).astype(np.float32)
    return jnp.take(w, jnp.asarray(idx), axis=-1) * jnp.asarray(sign)


def _zeros(n):
    return (np.zeros(n, np.int64), 0.0)


_IN_OFF = {}
_off = 0
for _name, _n in (("da_q", 256), ("da_k", 256), ("da_v", 256), ("da_g", 256), ("s5_u", 256), ("s5_g", 256),
                  ("hg_q", 256), ("hg_ff", 256), ("hg_fb", 256), ("hg_i", 256), ("hg_g", 256),
                  ("mla_cq", MLA_Q_RANK), ("mla_ckv", MLA_KV_RANK), ("mla_kr", MLA_ROPE), ("mla_g", 256)):
    _IN_OFF[_name] = np.arange(_off, _off + _n)
    _off += _n


def _rope_tables():
    t = np.arange(LAT_LEN)
    pos = np.stack([t // GRID_W, t % GRID_W], axis=1).astype(np.float32)
    inv_freq = (np.float32(ROPE_BASE) ** (-np.arange(8, dtype=np.float32) / np.float32(8))).astype(np.float32)
    r = np.arange(MLA_ROPE)
    ang = (pos[:, r // 16] * inv_freq[r % 8][None, :]).astype(np.float64)
    cos32, sin32 = np.cos(ang).astype(np.float32), np.sin(ang).astype(np.float32)
    lo = (np.arange(MLA_ROPE) % 16 < 8)[None, :]
    sin_lo32, sin_hi32 = np.where(lo, -sin32, 0.0).astype(np.float32), np.where(lo, 0.0, sin32).astype(np.float32)
    da_tabs = tuple(np.tile(x, (1, 8)) for x in (cos32, sin_lo32, sin_hi32))

    def head(x, fill):
        h = np.concatenate([np.full((LAT_LEN, MLA_NOPE), fill, np.float32), x,
                            np.full((LAT_LEN, MLA_HEAD_PAD - MLA_NOPE - MLA_ROPE), fill, np.float32)], axis=1)
        return np.concatenate([h, np.full((BIG_TILE, MLA_HEAD_PAD), fill, np.float32)], axis=0)
    k_tabs = (head(cos32, 1.0), head(sin_lo32, 0.0), head(sin_hi32, 0.0))
    return tuple(jnp.asarray(x) for x in da_tabs), tuple(jnp.asarray(x) for x in k_tabs)


def _mla_weights(w_uq, w_ukv, q_norm):
    hd = MLA_NOPE + MLA_ROPE
    pad_tail = _zeros(MLA_HEAD_PAD - hd)
    q_plan, k_plan, v_plan = [], [], []
    for h in range(MLA_HEADS):
        nope, rope = np.arange(h * hd, h * hd + MLA_NOPE), np.arange(h * hd + MLA_NOPE, (h + 1) * hd)
        q_plan += [(nope, 1.0), (rope, 1.0), pad_tail]
        k_plan += [(np.arange(h * 2 * MLA_NOPE, h * 2 * MLA_NOPE + MLA_NOPE), 1.0), _zeros(MLA_HEAD_PAD - MLA_NOPE)]
        v_plan += [(np.arange(h * 2 * MLA_NOPE + MLA_NOPE, (h + 1) * 2 * MLA_NOPE), 1.0)]
    pad_rows = lambda x: jnp.pad(x, ((0, 256 - MLA_Q_RANK), (0, 0))).astype(BF16)
    qn = jnp.pad(q_norm, (0, 256 - MLA_Q_RANK)).reshape(1, 256)
    return (pad_rows(_take_cols(w_uq, q_plan)), _take_cols(w_ukv, k_plan).astype(BF16),
            _take_cols(w_ukv, v_plan).astype(BF16), qn)


def _split_bf16(a):
    hi = a.astype(BF16)
    return hi, (a - hi.astype(F32)).astype(BF16)


def _dot_sel(a, sel):
    hi, lo = _split_bf16(a)
    sel = sel.astype(BF16)
    return _dot(hi, sel) + _dot(lo, sel)


def _dot_x3(a, b):
    a_hi, a_lo = _split_bf16(a)
    b_hi, b_lo = _split_bf16(b)
    return _dot(a_hi, b_hi) + _dot(a_hi, b_lo) + _dot(a_lo, b_hi)


def _s5_table_body(xy_ref, bb_ref, c_ref, ct_ref, mt_ref, bst_ref, cot_ref, a_ref):
    n, t, ch = S5_STATE, S5_CHUNK, S5_CH
    wide = 2 * S5_TAP
    tau_i = lax.broadcasted_iota(jnp.int32, (1, 128), 1)
    tau = tau_i.astype(F32)
    sel_row = lax.broadcasted_iota(jnp.int32, (128, 1), 0)

    def lag(width):
        return lax.broadcasted_iota(jnp.int32, (1, width), 1) >> 4

    def onehot(cond):
        return jnp.where(cond, 1.0, 0.0).astype(F32)
    j = lag(wide)
    e_z = (onehot((j <= t - 1) & (sel_row == t - 1 - j)), onehot((j >= t - 1) & (j <= 2 * t - 2) & (sel_row == j - (t - 1))))
    jc = lag(S5_TAP)
    e_c = (onehot(sel_row == jc + 1), onehot(sel_row == t - jc))
    ch_row = lax.broadcasted_iota(jnp.int32, (ch, 1), 0)
    tile_w = onehot((lax.broadcasted_iota(jnp.int32, (1, wide), 1) & (ch - 1)) == ch_row)
    tile_n = onehot((lax.broadcasted_iota(jnp.int32, (1, S5_TAP), 1) & (ch - 1)) == ch_row)

    for gi in range(S5_TABLE_GROUPS):
        xy = xy_ref[0, gi]
        z, cot_rows, klong = [], [], None
        for d in range(2):
            x, y = xy[:, 2 * d:2 * d + 1], xy[:, 2 * d + 1:2 * d + 2]
            mag = jnp.exp(jnp.where(tau_i <= t, tau, 0.0) * x)
            ang = jnp.where(tau_i <= t, tau, 0.0) * y
            p_re = jnp.where(tau_i <= t, mag * jnp.cos(ang), 0.0)
            p_im = jnp.where(tau_i <= t, mag * jnp.sin(ang), 0.0)
            a_ref[0, gi, 2 * d] = jnp.broadcast_to(p_re[:, t:t + 1], (n, 128))
            a_ref[0, gi, 2 * d + 1] = jnp.broadcast_to(p_im[:, t:t + 1], (n, 128))
            pz_re, pz_im = _dot_sel(p_re, e_z[d]), _dot_sel(p_im, e_z[d])
            b_re, b_im = _dot_sel(bb_ref[0, gi, 2 * d], tile_w), _dot_sel(bb_ref[0, gi, 2 * d + 1], tile_w)
            z_re, z_im = pz_re * b_re - pz_im * b_im, pz_re * b_im + pz_im * b_re
            z += [z_re, z_im]
            part = _dot_x3(c_ref[0, gi, 2 * d], z_re) - _dot_x3(c_ref[0, gi, 2 * d + 1], z_im)
            klong = part if klong is None else klong + part
            pc_re, pc_im = _dot_sel(p_re, e_c[d]), _dot_sel(p_im, e_c[d])
            c_re, c_im = _dot_sel(ct_ref[0, gi, 2 * d], tile_n), _dot_sel(ct_ref[0, gi, 2 * d + 1], tile_n)
            cot_rows += [c_re * pc_re - c_im * pc_im, -(c_re * pc_im + c_im * pc_re)]
        for tt in range(t):
            off = (t - 1 - tt) * ch
            win = klong if off == 0 else pltpu.roll(klong, wide - off, 1)
            mt_ref[0, gi, tt * ch:(tt + 1) * ch, :] = win[:, :S5_TAP].astype(BF16)
        back = pltpu.roll(z[2], wide - (t - 1) * ch, 1), pltpu.roll(z[3], wide - (t - 1) * ch, 1)
        for k, rows in enumerate((z[0], z[1], back[0], back[1])):
            bst_ref[0, gi, k * n:(k + 1) * n, :] = rows[:, :S5_TAP].astype(BF16)
        cot_ref[0, gi] = jnp.concatenate(cot_rows, axis=0).T.astype(BF16)


def _s5_tables(a_re, a_im, log_dt, b_re, b_im, c_re, c_im):
    nl, g, n, ch = a_re.shape[0], S5_GROUPS, S5_STATE, S5_CH
    step = jnp.exp(log_dt)[..., None]
    mag = jnp.exp(a_re * step)
    ab_re, ab_im = mag * jnp.cos(a_im * step), mag * jnp.sin(a_im * step)
    den = a_re * a_re + a_im * a_im
    f_re = ((ab_re - 1.0) * a_re + ab_im * a_im) / den
    f_im = (ab_im * a_re - (ab_re - 1.0) * a_im) / den
    bb_re = f_re[..., None] * b_re - f_im[..., None] * b_im
    bb_im = f_re[..., None] * b_im + f_im[..., None] * b_re
    by_group = lambda x: jnp.moveaxis(x, 1, 2)
    pair = lambda re, im: jnp.stack([by_group(re), by_group(im)], axis=3).reshape((nl, g, 4) + re.shape[3:])
    xy = jnp.stack([by_group(a_re * step), by_group(a_im * step)], axis=3).reshape(nl, g, 4, n)
    xy = jnp.pad(jnp.swapaxes(xy, 2, 3), ((0, 0), (0, 0), (0, 0), (0, 4)))
    gs = S5_TABLE_GROUPS
    mat = pl.BlockSpec((1, gs, S5_TAP, S5_TAP), lambda l, i: (l, i, 0, 0))
    return pl.pallas_call(
        _s5_table_body,
        grid=(nl, g // gs),
        in_specs=[pl.BlockSpec((1, gs, n, 8), lambda l, i: (l, i, 0, 0)),
                  pl.BlockSpec((1, gs, 4, n, ch), lambda l, i: (l, i, 0, 0, 0)),
                  pl.BlockSpec((1, gs, 4, ch, n), lambda l, i: (l, i, 0, 0, 0)),
                  pl.BlockSpec((1, gs, 4, n, ch), lambda l, i: (l, i, 0, 0, 0))],
        out_specs=[mat, mat, mat, pl.BlockSpec((1, gs, 4, n, 128), lambda l, i: (l, i, 0, 0, 0))],
        out_shape=[jax.ShapeDtypeStruct((nl, g, S5_TAP, S5_TAP), BF16)] * 3
        + [jax.ShapeDtypeStruct((nl, g, 4, n, 128), F32)],
        compiler_params=_cparams("parallel", "parallel"),
        name="s5_tables",
    )(xy, pair(bb_re, bb_im), pair(c_re, c_im), pair(jnp.swapaxes(c_re, -1, -2), jnp.swapaxes(c_im, -1, -2)))


def _s5_chunk_lanes(u):
    return u.reshape(S5_NCHUNK, S5_CHUNK, BRANCH).transpose(1, 2, 0)


def _s5_token_rows(y):
    return y.transpose(2, 0, 1).reshape(N_TOK, BRANCH)


def kernel(x_prompt, x_sample, cache_diff_k, cache_diff_v, state_s5, state_hgrn, cache_mla_ckv, cache_mla_krope, c, c_ctx, w_mod, b_mod, w_in, w_out, da_lambda, da_norm, s5_a_re, s5_a_im, s5_log_dt, s5_b_re, s5_b_im, s5_c_re, s5_c_im, s5_d, s5_w_glu, hg_lb, hg_norm, mla_q_norm, mla_w_uq, mla_kv_norm, mla_w_ukv, final_norm):
    lb_w = jax.nn.softmax(hg_lb.astype(F32), axis=0)
    lb_all = jnp.cumsum(lb_w, axis=0) - lb_w[0:1]
    c_rows = jnp.concatenate([c_ctx[None], c, jnp.zeros((8 - 1 - N_LAT_SEQ, D_MODEL), F32)], axis=0)
    mods = _modulation(c_rows, w_mod, b_mod)
    da_tabs, mla_tabs = _rope_tables()
    xs = (x_prompt.reshape(N_CTX, D_MODEL), x_sample.reshape(N_LAT, D_MODEL))
    new_k, new_v, new_s5, new_hg, new_ckv, new_kr = [], [], [], [], [], []
    s5_tabs = _s5_tables(s5_a_re, s5_a_im, s5_log_dt, s5_b_re, s5_b_im, s5_c_re, s5_c_im)
    w_all = _arrange_w_in(w_in)
    for l in range(DEPTH):
        mod = mods[l, :3].reshape(3, 3, D_MODEL)
        z_a, z_b, z_c, z_d, k_new, v_new, u_bf = _in_proj(xs, mod, w_all, l)

        lam_init = 0.8 - 0.6 * math.exp(-0.3 * l)
        kv_lat = _da_latent_kv(z_a, da_tabs,
                               cache_diff_k[:, l].reshape(N_LAT_SEQ, PAST_LEN, BRANCH),
                               cache_diff_v[:, l].reshape(N_LAT_SEQ, PAST_LEN, BRANCH))
        a_out = _da_attention(z_a, da_lambda[l], da_norm[l], lam_init, da_tabs, kv_lat)
        new_k.append(k_new)
        new_v.append(v_new)

        h0 = state_s5[:, l].transpose(2, 1, 4, 3, 0).reshape(S5_GROUPS, 4, S5_STATE, N_LAT_SEQ)
        h0 = jnp.pad(h0, ((0, 0), (0, 0), (0, 0), (0, 128 - N_LAT_SEQ)))
        y_all, fin = _s5_scan(_s5_chunk_lanes(u_bf), *s5_tabs, h0, l)
        b_out = (_s5_token_rows(y_all), z_b, s5_d[l].reshape(1, BRANCH), s5_w_glu[l].astype(BF16))
        fin = fin[..., :N_CTX_SEQ].reshape(S5_GROUPS, 2, 2, S5_STATE, N_CTX_SEQ)
        new_s5.append(fin.transpose(4, 1, 0, 3, 2))

        lb = jnp.concatenate([lb_all[l, 0].reshape(HG_HEADS, HG_DK), lb_all[l, 1].reshape(HG_HEADS, HG_DK)],
                             axis=-1).reshape(1, HG_W)
        head_eye = jnp.eye(HG_HEADS, dtype=F32)
        s0 = state_hgrn[:, l].transpose(0, 2, 1, 3, 4).reshape(N_LAT_SEQ, HG_HEADS, HG_HEAD_W, 1, HG_DK)
        s0 = (s0 * head_eye[None, :, None, :, None]).reshape(N_LAT_SEQ, HG_HEADS, HG_HEAD_W, BRANCH)
        s_f, s_b = _hg_states(z_c, lb, s0)
        c_out, s_fin = _hg_main(z_c, s_f, s_b, lb, jnp.tile(hg_norm[l].reshape(1, HG_DK), (1, HG_HEADS)))
        new_hg.append(s_fin.reshape(N_CTX_SEQ, HG_HEADS, 2, HG_DK, HG_DK).transpose(0, 2, 1, 3, 4))

        wq, wk, wv, qn = _mla_weights(mla_w_uq[l], mla_w_ukv[l], mla_q_norm[l])
        q, ckv_n, kr = _mla_prep(z_d, mla_tabs, qn, mla_kv_norm[l].reshape(1, MLA_KV_RANK), wq)
        kr_cache = jnp.pad(cache_mla_krope[:, l], ((0, 0), (0, 0), (MLA_NOPE, 128 - MLA_NOPE - MLA_ROPE)))
        k_ctx, v_ctx, k_lat, v_lat = _mla_kv(ckv_n, kr, cache_mla_ckv[:, l], kr_cache, wk, wv)
        d_out = _mla_attention(z_d, q, k_ctx, v_ctx, k_lat, v_lat)
        new_ckv.append(ckv_n[:N_CTX].reshape(N_CTX_SEQ, CTX_LEN, MLA_KV_RANK))
        new_kr.append(kr[:N_CTX, MLA_NOPE:MLA_NOPE + MLA_ROPE].reshape(N_CTX_SEQ, CTX_LEN, MLA_ROPE))

        xs = _out_proj(a_out, b_out, c_out, d_out, xs, mod, w_out, l,
                       final_norm.reshape(1, D_MODEL), final=(l == DEPTH - 1))
        xs = tuple(xs) if l == DEPTH - 1 else (xs,)
    y_prompt = xs[0].reshape(N_CTX_SEQ, CTX_LEN, D_MODEL)
    y_sample = xs[1].reshape(N_LAT_SEQ, LAT_LEN, D_MODEL)
    st = lambda parts: jnp.stack(parts, axis=1)
    heads_last = lambda kv: kv.reshape(N_CTX_SEQ, DEPTH, DA_HEADS, 2 * DA_QK, CTX_LEN).transpose(0, 1, 4, 2, 3)
    return (y_prompt, y_sample, heads_last(st(new_k)), heads_last(st(new_v)), st(new_s5), st(new_hg), st(new_ckv), st(new_kr))
```

```python
import functools
import math

import numpy as np

import jax
import jax.numpy as jnp
from jax import lax
from jax.experimental import pallas as pl
from jax.experimental.pallas import tpu as pltpu

F32 = jnp.float32
BF16 = jnp.bfloat16

D_MODEL = 1024
DEPTH = 2
N_CTX_SEQ = 16
CTX_LEN = 256
N_LAT_SEQ = 2
LAT_LEN = 2048
PAST_LEN = 256
GRID_W = 64
N_CTX = N_CTX_SEQ * CTX_LEN
N_LAT = N_LAT_SEQ * LAT_LEN
N_TOK = N_CTX + N_LAT
BRANCH = 256
EPS = 1e-6
ROPE_BASE = 10000.0
ROW_TILE = 256
LAT_TILES = LAT_LEN // ROW_TILE
N_TILES = N_TOK // ROW_TILE
CTX_TILES = N_CTX // ROW_TILE
VMEM_LIMIT = 48 * 1024 * 1024
IN_PROJ_VMEM_LIMIT = 56 * 1024 * 1024
LAT_Q_TILE = 512
LAT_Q_TILES = LAT_LEN // LAT_Q_TILE
BIG_TILE = 512
CTX_SEQ_PER_STEP = 2

DA_HEADS = 4
DA_QK = 32
MLA_HEADS = 4
MLA_NOPE = 64
MLA_ROPE = 32
MLA_Q_RANK = 192
MLA_KV_RANK = 128
S5_GROUPS = 16
S5_CH = 16
S5_STATE = 64
S5_CHUNK = 16
HG_HEADS = 4
HG_DK = 64

W_A = 1024
W_B = 512
W_C = 1536
W_D = 768
W_ALL = W_A + W_B + W_C + W_D


def _cparams(*sem):
    return pltpu.CompilerParams(dimension_semantics=sem, vmem_limit_bytes=VMEM_LIMIT)


def _tile_seq(i, tile=ROW_TILE):
    return jnp.where(i < N_CTX // tile, 0, 1 + (i - N_CTX // tile) // (LAT_LEN // tile))


def _silu(x):
    return x * (1.0 / (1.0 + jnp.exp(-x)))


def _dot(a, b):
    return jnp.dot(a, b, preferred_element_type=F32)


def _dot_nt(a, b):
    return lax.dot_general(a, b, (((1,), (1,)), ((), ())), preferred_element_type=F32)


def _mod_body(c_ref, w_ref, b_ref, o_ref):
    c = _silu(c_ref[...]).astype(BF16)
    o_ref[0] = _dot(c, w_ref[0].astype(BF16)) + b_ref[0]


def _modulation(c_rows, w_mod, b_mod):
    tn = 768
    return pl.pallas_call(
        _mod_body,
        grid=(DEPTH, 3 * D_MODEL // tn),
        in_specs=[pl.BlockSpec((8, D_MODEL), lambda l, j: (0, 0)),
                  pl.BlockSpec((1, D_MODEL, tn), lambda l, j: (l, 0, j)),
                  pl.BlockSpec((1, 1, tn), lambda l, j: (l, 0, j))],
        out_specs=pl.BlockSpec((1, 8, tn), lambda l, j: (l, 0, j)),
        out_shape=jax.ShapeDtypeStruct((DEPTH, 8, 3 * D_MODEL), F32),
        compiler_params=_cparams("parallel", "parallel"),
        name="modulation",
    )(c_rows, w_mod, b_mod.reshape(DEPTH, 1, 3 * D_MODEL))


def _split_rows(i, ctx_ref, lat_ref):
    return jnp.where(i < N_CTX // ctx_ref.shape[0], ctx_ref[...], lat_ref[...])


def _ctx_tile_spec(w, tile=ROW_TILE):
    return pl.BlockSpec((tile, w), lambda i: (jnp.minimum(i, N_CTX // tile - 1), 0))


def _lat_tile_spec(w, tile=ROW_TILE):
    return pl.BlockSpec((tile, w), lambda i: (jnp.maximum(i - N_CTX // tile, 0), 0))


def _in_proj_body(*refs, split):
    if split:
        xc_ref, xl_ref, mod_ref, w_ref, oa, ob, oc, od, ok, ov, ou = refs
        x = _split_rows(pl.program_id(0), xc_ref, xl_ref)
    else:
        x_ref, mod_ref, w_ref, oa, ob, oc, od, ok, ov, ou = refs
        x = x_ref[...]
    xn = x * lax.rsqrt(jnp.mean(x * x, axis=-1, keepdims=True) + EPS)
    mod = mod_ref[0]
    h = (xn * (1.0 + mod[1:2]) + mod[0:1]).astype(BF16)
    off = 0
    for o in (oa, ob, oc, od):
        w = o.shape[-1]
        o[...] = _dot(h, w_ref[0, :, off:off + w])
        off += w
    ou[...] = ob[:, :BRANCH].astype(BF16)

    @pl.when(pl.program_id(0) < N_CTX // BIG_TILE)
    def _():
        for t in range(BIG_TILE // CTX_LEN):
            r = slice(t * CTX_LEN, (t + 1) * CTX_LEN)
            ok[t] = oa[r, BRANCH:2 * BRANCH].T
            ov[t] = oa[r, 2 * BRANCH:3 * BRANCH].T


def _in_proj(xs, mod, w_all, l):
    widths = (W_A, W_B, W_C, W_D)
    split = len(xs) == 2
    x_specs = ([_ctx_tile_spec(D_MODEL, BIG_TILE), _lat_tile_spec(D_MODEL, BIG_TILE)] if split
               else [pl.BlockSpec((BIG_TILE, D_MODEL), lambda i: (i, 0))])
    return pl.pallas_call(
        functools.partial(_in_proj_body, split=split),
        grid=(N_TOK // BIG_TILE,),
        in_specs=x_specs + [pl.BlockSpec((1, 3, D_MODEL), lambda i: (_tile_seq(i, BIG_TILE), 0, 0)),
                            pl.BlockSpec((1, D_MODEL, W_ALL), lambda i: (l, 0, 0))],
        out_specs=[pl.BlockSpec((BIG_TILE, w), lambda i: (i, 0)) for w in widths]
        + [pl.BlockSpec((BIG_TILE // CTX_LEN, BRANCH, CTX_LEN), lambda i: (jnp.minimum(i, N_CTX // BIG_TILE - 1), 0, 0))] * 2
        + [pl.BlockSpec((BIG_TILE, BRANCH), lambda i: (i, 0))],
        out_shape=[jax.ShapeDtypeStruct((N_TOK, w), F32) for w in widths]
        + [jax.ShapeDtypeStruct((N_CTX_SEQ, BRANCH, CTX_LEN), F32)] * 2 + [jax.ShapeDtypeStruct((N_TOK, BRANCH), BF16)],
        compiler_params=pltpu.CompilerParams(dimension_semantics=("arbitrary",), vmem_limit_bytes=IN_PROJ_VMEM_LIMIT),
        name="in_proj",
    )(*xs, mod, w_all)


def _arrange_body(wt_ref, o_ref):
    w = wt_ref[0]
    c = _IN_OFF
    rows = lambda name: w[c[name][0]:c[name][-1] + 1]
    zero = lambda n: jnp.zeros((n, w.shape[1]), F32)

    def per_head(x, y):
        xs, ys = rows(x), rows(y)
        return [p for h in range(HG_HEADS) for p in (xs[h * HG_DK:(h + 1) * HG_DK], ys[h * HG_DK:(h + 1) * HG_DK])]
    pieces = ([w[0:W_A + W_B]] + per_head("hg_q", "hg_q") + per_head("hg_ff", "hg_fb") + [rows("hg_i"), rows("hg_g")]
              + [rows("mla_cq"), zero(256 - MLA_Q_RANK), rows("mla_ckv"), zero(MLA_NOPE), rows("mla_kr"),
                 zero(128 - MLA_NOPE - MLA_ROPE), rows("mla_g")])
    o_ref[0] = jnp.concatenate(pieces, axis=0).T.astype(BF16)


def _arrange_w_in(w_in):
    lanes = 256
    wt = jnp.swapaxes(w_in, 1, 2)
    return pl.pallas_call(
        _arrange_body,
        grid=(DEPTH, D_MODEL // lanes),
        in_specs=[pl.BlockSpec((1, wt.shape[1], lanes), lambda l, i: (l, 0, i))],
        out_specs=pl.BlockSpec((1, lanes, W_ALL), lambda l, i: (l, i, 0)),
        out_shape=jax.ShapeDtypeStruct((DEPTH, D_MODEL, W_ALL), BF16),
        compiler_params=_cparams("parallel", "parallel"),
        name="arrange_w_in",
    )(wt)


def _out_proj_body(*refs, split_in, final):
    ac_ref, al_ref, y_ref, u_ref, g_ref, dsk_ref, wglu_ref, c_ref, dc_ref, dl_ref = refs[:10]
    i = pl.program_id(0)
    if split_in:
        xc_ref, xl_ref, mod_ref, w_ref, fn_ref = refs[10:15]
        x = _split_rows(i, xc_ref, xl_ref)
    else:
        x_ref, mod_ref, w_ref, fn_ref = refs[10:14]
        x = x_ref[...]
    b_out = _s5_gated(y_ref[...], u_ref[...], g_ref[...], dsk_ref[...], wglu_ref[...])
    branches = (_split_rows(i, ac_ref, al_ref), b_out, c_ref[...], _split_rows(i, dc_ref, dl_ref))
    acc = None
    for j, r in enumerate(branches):
        t = _dot(r.astype(BF16), w_ref[0, j * BRANCH:(j + 1) * BRANCH, :].astype(BF16))
        acc = t if acc is None else acc + t
    x = x + mod_ref[0][2:3] * acc
    if not final:
        refs[-1][...] = x
        return
    y = x * lax.rsqrt(jnp.mean(x * x, axis=-1, keepdims=True) + EPS) * fn_ref[...]
    yc_ref, yl_ref = refs[-2:]

    @pl.when(i < N_CTX // BIG_TILE)
    def _():
        yc_ref[...] = y

    @pl.when(i >= N_CTX // BIG_TILE)
    def _():
        yl_ref[...] = y


def _out_proj(a, s5, c, d, xs, mod, w_out, l, final_norm, final):
    y_ssm, z_b, d_skip, w_glu = s5
    br = pl.BlockSpec((BIG_TILE, BRANCH), lambda i: (i, 0))
    s5_specs = [br, br, pl.BlockSpec((BIG_TILE, BRANCH), lambda i: (i, 1)),
                pl.BlockSpec((1, BRANCH), lambda i: (0, 0)), pl.BlockSpec((BRANCH, 2 * BRANCH), lambda i: (0, 0))]
    pair = [_ctx_tile_spec(BRANCH, BIG_TILE), _lat_tile_spec(BRANCH, BIG_TILE)]
    split_in = len(xs) == 2
    x_specs = ([_ctx_tile_spec(D_MODEL, BIG_TILE), _lat_tile_spec(D_MODEL, BIG_TILE)] if split_in
               else [pl.BlockSpec((BIG_TILE, D_MODEL), lambda i: (i, 0))])
    if final:
        out_specs = [_ctx_tile_spec(D_MODEL, BIG_TILE), _lat_tile_spec(D_MODEL, BIG_TILE)]
        out_shape = [jax.ShapeDtypeStruct((N_CTX, D_MODEL), F32), jax.ShapeDtypeStruct((N_LAT, D_MODEL), F32)]
    else:
        out_specs = pl.BlockSpec((BIG_TILE, D_MODEL), lambda i: (i, 0))
        out_shape = jax.ShapeDtypeStruct((N_TOK, D_MODEL), F32)
    return pl.pallas_call(
        functools.partial(_out_proj_body, split_in=split_in, final=final),
        grid=(N_TOK // BIG_TILE,),
        in_specs=pair + s5_specs + [br] + pair + x_specs + [
            pl.BlockSpec((1, 3, D_MODEL), lambda i: (_tile_seq(i, BIG_TILE), 0, 0)),
            pl.BlockSpec((1, D_MODEL, D_MODEL), lambda i: (l, 0, 0)),
            pl.BlockSpec((1, D_MODEL), lambda i: (0, 0))],
        out_specs=out_specs,
        out_shape=out_shape,
        compiler_params=_cparams("arbitrary"),
        name="out_proj",
    )(*a, y_ssm, z_b, z_b, d_skip, w_glu, c, *d, *xs, mod, w_out, final_norm)


LOG2E = 1.4426950408889634


def _exp2_rows(s):
    e = jnp.exp2(s - jnp.max(s, axis=-1, keepdims=True))
    return e, jnp.sum(e, axis=-1, keepdims=True)


def _rope(x, cos, sin_lo, sin_hi):
    w = x.shape[-1]
    return x * cos + pltpu.roll(x, w - 8, 1) * sin_lo + pltpu.roll(x, 8, 1) * sin_hi


def _da_kv_body(k_ref, v_ref, cos_ref, slo_ref, shi_ref, ck_ref, cv_ref, ko_ref, vo_ref):
    j = pl.program_id(1)

    @pl.when(j < LAT_TILES)
    def _():
        ko_ref[0] = _rope(k_ref[...], cos_ref[...], slo_ref[...], shi_ref[...]).astype(BF16)
        vo_ref[0] = v_ref[...].astype(BF16)

    @pl.when(j == LAT_TILES)
    def _():
        ko_ref[0] = ck_ref[0].astype(BF16)
        vo_ref[0] = cv_ref[0].astype(BF16)


def _da_latent_kv(z_a, tabs, cache_k, cache_v):
    def rows(col):
        return pl.BlockSpec(
            (ROW_TILE, BRANCH),
            lambda b, j: (CTX_TILES + b * LAT_TILES + jnp.minimum(j, LAT_TILES - 1), col))
    tab = pl.BlockSpec((ROW_TILE, BRANCH), lambda b, j: (jnp.minimum(j, LAT_TILES - 1), 0))
    cache = pl.BlockSpec((1, PAST_LEN, BRANCH), lambda b, j: (b, 0, 0))
    out = pl.BlockSpec((1, ROW_TILE, BRANCH), lambda b, j: (b, j, 0))
    shp = jax.ShapeDtypeStruct((N_LAT_SEQ, LAT_LEN + PAST_LEN, BRANCH), BF16)
    return pl.pallas_call(
        _da_kv_body,
        grid=(N_LAT_SEQ, LAT_TILES + 1),
        in_specs=[rows(1), rows(2), tab, tab, tab, cache, cache],
        out_specs=[out, out],
        out_shape=[shp, shp],
        compiler_params=_cparams("parallel", "parallel"),
        name="da_kv",
    )(z_a, z_a, *tabs, cache_k, cache_v)


def _da_attn_body(lam_ref, ng_ref, q_ref, *rest, rope, lam_init):
    if rope:
        cos_ref, slo_ref, shi_ref, k_ref, v_ref, g_ref, o_ref = rest
        q = _rope(q_ref[...], cos_ref[...], slo_ref[...], shi_ref[...])
        o_ref[...] = _da_attn_tile(lam_ref, ng_ref, q, k_ref[0], v_ref[0], g_ref[...], lam_init)
    else:
        k_ref, v_ref, g_ref, o_ref = rest
        for t in range(q_ref.shape[0] // CTX_LEN):
            r = slice(t * CTX_LEN, (t + 1) * CTX_LEN)
            o_ref[r, :] = _da_attn_tile(lam_ref, ng_ref, q_ref[r, :], k_ref[r, :].astype(BF16),
                                        v_ref[r, :].astype(BF16), g_ref[r, :], lam_init)


def _da_attn_tile(lam_ref, ng_ref, q, k, v, g, lam_init):
    q = q * (DA_QK ** -0.5 * LOG2E)
    lv = lam_ref[...]
    lam = (jnp.exp(jnp.sum(lv[0:1] * lv[1:2], axis=-1, keepdims=True))
           - jnp.exp(jnp.sum(lv[2:3] * lv[3:4], axis=-1, keepdims=True)) + lam_init)
    lane = lax.broadcasted_iota(jnp.int32, (1, BRANCH), 1)
    acc = jnp.zeros(q.shape, F32)
    for h in range(DA_HEADS):
        q1 = jnp.where(lane // DA_QK == 2 * h, q, 0.0).astype(BF16)
        q2 = jnp.where(lane // DA_QK == 2 * h + 1, q, 0.0).astype(BF16)
        e1, l1 = _exp2_rows(_dot_nt(q1, k))
        e2, l2 = _exp2_rows(_dot_nt(q2, k))
        a = (e1 - (lam * l1 / l2) * e2).astype(BF16)
        acc = jnp.where(lane // (2 * DA_QK) == h, _dot(a, v) * (1.0 / l1), acc)
    sq = acc * acc
    ms = jnp.zeros(q.shape, F32)
    for h in range(DA_HEADS):
        hm = lane // (2 * DA_QK) == h
        ms = jnp.where(hm, jnp.sum(jnp.where(hm, sq, 0.0), axis=-1, keepdims=True), ms)
    o = acc * lax.rsqrt(ms * (1.0 / (2 * DA_QK)) + EPS) * (ng_ref[...] * (1.0 - lam_init))
    return o * _silu(g)


def _da_attention(z_a, lam_vec, norm_g, lam_init, tabs, kv_lat):
    ng = jnp.tile(norm_g.reshape(1, 2 * DA_QK), (1, DA_HEADS))
    small = [pl.BlockSpec((4, DA_QK), lambda *_: (0, 0)), pl.BlockSpec((1, BRANCH), lambda *_: (0, 0))]

    rows = CTX_SEQ_PER_STEP * CTX_LEN

    def col(c):
        return pl.BlockSpec((rows, BRANCH), lambda i: (i, c))
    ctx = pl.pallas_call(
        functools.partial(_da_attn_body, rope=False, lam_init=lam_init),
        grid=(N_CTX // rows,),
        in_specs=small + [col(0), col(1), col(2), col(3)],
        out_specs=pl.BlockSpec((rows, BRANCH), lambda i: (i, 0)),
        out_shape=jax.ShapeDtypeStruct((N_CTX, BRANCH), F32),
        compiler_params=_cparams("parallel"),
        name="da_attn_ctx",
    )(lam_vec, ng, z_a, z_a, z_a, z_a)

    def lcol(c):
        return pl.BlockSpec((LAT_Q_TILE, BRANCH), lambda b, j: (N_CTX // LAT_Q_TILE + b * LAT_Q_TILES + j, c))
    tab = pl.BlockSpec((LAT_Q_TILE, BRANCH), lambda b, j: (j, 0))
    kvs = pl.BlockSpec((1, LAT_LEN + PAST_LEN, BRANCH), lambda b, j: (b, 0, 0))
    lat = pl.pallas_call(
        functools.partial(_da_attn_body, rope=True, lam_init=lam_init),
        grid=(N_LAT_SEQ, LAT_Q_TILES),
        in_specs=small + [lcol(0), tab, tab, tab, kvs, kvs, lcol(3)],
        out_specs=pl.BlockSpec((LAT_Q_TILE, BRANCH), lambda b, j: (b * LAT_Q_TILES + j, 0)),
        out_shape=jax.ShapeDtypeStruct((N_LAT, BRANCH), F32),
        compiler_params=_cparams("parallel", "parallel"),
        name="da_attn_lat",
    )(lam_vec, ng, z_a, *tabs, kv_lat[0], kv_lat[1], z_a)
    return ctx, lat


MLA_HEAD_PAD = 128
MLA_QW = MLA_HEADS * MLA_HEAD_PAD


def _mla_prep_body(cq_ref, ckv_ref, kr_ref, ck_t, sk_lo, sk_hi, qn_ref, kvn_ref, wq_ref, q_out, ckv_out, kr_out):
    cq = cq_ref[...]
    ms = jnp.sum(cq * cq, axis=-1, keepdims=True) * (1.0 / MLA_Q_RANK)
    qn = (cq * lax.rsqrt(ms + EPS) * qn_ref[...]).astype(BF16)
    heads = lambda t: jnp.concatenate([t[...]] * MLA_HEADS, axis=-1)
    q = _rope(_dot(qn, wq_ref[...]), heads(ck_t), heads(sk_lo), heads(sk_hi))
    q_out[...] = (q * ((MLA_NOPE + MLA_ROPE) ** -0.5 * LOG2E)).astype(BF16)
    ckv = ckv_ref[...]
    ckv_out[...] = ckv * lax.rsqrt(jnp.mean(ckv * ckv, axis=-1, keepdims=True) + EPS) * kvn_ref[...]
    kr_out[...] = _rope(kr_ref[...], ck_t[...], sk_lo[...], sk_hi[...])


def _mla_prep(z_d, tabs, q_norm_pad, kv_norm, wq):
    ctx_tiles, lat_tiles = N_CTX // BIG_TILE, LAT_LEN // BIG_TILE

    def tab(w):
        return pl.BlockSpec((BIG_TILE, w), lambda i: (jnp.where(i < ctx_tiles, lat_tiles, (i - ctx_tiles) % lat_tiles), 0))

    def col(w, c):
        return pl.BlockSpec((BIG_TILE, w), lambda i: (i, c))

    def const(shape):
        return pl.BlockSpec(shape, lambda i: (0, 0))
    return pl.pallas_call(
        _mla_prep_body,
        grid=(N_TOK // BIG_TILE,),
        in_specs=[col(256, 0), col(128, 2), col(128, 3),
                  tab(128), tab(128), tab(128),
                  const((1, 256)), const((1, 128)), const((256, MLA_QW))],
        out_specs=[col(MLA_QW, 0), col(128, 0), col(128, 0)],
        out_shape=[jax.ShapeDtypeStruct((N_TOK, MLA_QW), BF16),
                   jax.ShapeDtypeStruct((N_TOK, 128), F32),
                   jax.ShapeDtypeStruct((N_TOK, 128), F32)],
        compiler_params=_cparams("parallel"),
        name="mla_prep",
    )(z_d, z_d, z_d, *tabs, q_norm_pad, kv_norm, wq)


def _mla_kv_math(ckv, kr, wk_ref, wv_ref, k_out, v_out):
    c = ckv.astype(BF16)
    k_out[...] = (_dot(c, wk_ref[...]) + jnp.concatenate([kr] * MLA_HEADS, axis=-1)).astype(BF16).reshape(k_out.shape)
    v_out[...] = _dot(c, wv_ref[...]).astype(BF16).reshape(v_out.shape)


def _mla_kv_ctx_body(ckv_ref, kr_ref, wk_ref, wv_ref, k_out, v_out):
    _mla_kv_math(ckv_ref[...], kr_ref[...], wk_ref, wv_ref, k_out, v_out)


def _mla_kv_lat_body(ckv_ref, kr_ref, cckv_ref, ckr_ref, wk_ref, wv_ref, k_out, v_out):
    j = pl.program_id(1)

    @pl.when(j < LAT_TILES)
    def _():
        _mla_kv_math(ckv_ref[...], kr_ref[...], wk_ref, wv_ref, k_out, v_out)

    @pl.when(j == LAT_TILES)
    def _():
        _mla_kv_math(cckv_ref[0], ckr_ref[0], wk_ref, wv_ref, k_out, v_out)


def _mla_kv(ckv, kr, cache_ckv, cache_kr, wk, wv):
    weights = [pl.BlockSpec((128, MLA_QW), lambda *_: (0, 0)), pl.BlockSpec((128, BRANCH), lambda *_: (0, 0))]
    k_ctx, v_ctx = pl.pallas_call(
        _mla_kv_ctx_body,
        grid=(N_CTX // BIG_TILE,),
        in_specs=[pl.BlockSpec((BIG_TILE, 128), lambda i: (i, 0)), pl.BlockSpec((BIG_TILE, 128), lambda i: (i, 0))] + weights,
        out_specs=[pl.BlockSpec((BIG_TILE, MLA_QW), lambda i: (i, 0)),
                   pl.BlockSpec((BIG_TILE, BRANCH), lambda i: (i, 0))],
        out_shape=[jax.ShapeDtypeStruct((N_CTX, MLA_QW), BF16), jax.ShapeDtypeStruct((N_CTX, BRANCH), BF16)],
        compiler_params=_cparams("parallel"),
        name="mla_kv_ctx",
    )(ckv, kr, wk, wv)
    rows = pl.BlockSpec((ROW_TILE, 128), lambda b, j: (CTX_TILES + b * LAT_TILES + jnp.minimum(j, LAT_TILES - 1), 0))
    cache = pl.BlockSpec((1, PAST_LEN, 128), lambda b, j: (b, 0, 0))
    lk = LAT_LEN + PAST_LEN
    k_lat, v_lat = pl.pallas_call(
        _mla_kv_lat_body,
        grid=(N_LAT_SEQ, LAT_TILES + 1),
        in_specs=[rows, rows, cache, cache] + weights,
        out_specs=[pl.BlockSpec((1, ROW_TILE, MLA_QW), lambda b, j: (b, j, 0)),
                   pl.BlockSpec((1, ROW_TILE, BRANCH), lambda b, j: (b, j, 0))],
        out_shape=[jax.ShapeDtypeStruct((N_LAT_SEQ, lk, MLA_QW), BF16), jax.ShapeDtypeStruct((N_LAT_SEQ, lk, BRANCH), BF16)],
        compiler_params=_cparams("parallel", "parallel"),
        name="mla_kv_lat",
    )(ckv, kr, cache_ckv, cache_kr, wk, wv)
    return k_ctx, v_ctx, k_lat, v_lat


def _mla_attn_body(q_ref, k_ref, v_ref, g_ref, o_ref, *, ctx):
    if ctx:
        for t in range(q_ref.shape[0] // CTX_LEN):
            r = slice(t * CTX_LEN, (t + 1) * CTX_LEN)
            o_ref[r, :] = _mla_attn_tile(q_ref[r, :], k_ref[r, :], v_ref[r, :], g_ref[r, :])
    else:
        o_ref[...] = _mla_attn_tile(q_ref[...], k_ref[0], v_ref[0], g_ref[...])


def _mla_attn_tile(q, k, v, g):
    lane = lax.broadcasted_iota(jnp.int32, (1, BRANCH), 1)
    acc = jnp.zeros((q.shape[0], BRANCH), F32)
    for h in range(MLA_HEADS):
        sl = slice(h * MLA_HEAD_PAD, (h + 1) * MLA_HEAD_PAD)
        e, l = _exp2_rows(_dot_nt(q[:, sl], k[:, sl]))
        acc = jnp.where(lane // 64 == h, _dot(e.astype(BF16), v) * (1.0 / l), acc)
    return acc * _silu(g)


def _mla_attention(z_d, q, k_ctx, v_ctx, k_lat, v_lat):
    rows = CTX_SEQ_PER_STEP * CTX_LEN
    ctx = pl.pallas_call(
        functools.partial(_mla_attn_body, ctx=True),
        grid=(N_CTX // rows,),
        in_specs=[pl.BlockSpec((rows, MLA_QW), lambda i: (i, 0)),
                  pl.BlockSpec((rows, MLA_QW), lambda i: (i, 0)),
                  pl.BlockSpec((rows, BRANCH), lambda i: (i, 0)),
                  pl.BlockSpec((rows, BRANCH), lambda i: (i, 2))],
        out_specs=pl.BlockSpec((rows, BRANCH), lambda i: (i, 0)),
        out_shape=jax.ShapeDtypeStruct((N_CTX, BRANCH), F32),
        compiler_params=_cparams("parallel"),
        name="mla_attn_ctx",
    )(q, k_ctx, v_ctx, z_d)
    lk = LAT_LEN + PAST_LEN
    lat = pl.pallas_call(
        functools.partial(_mla_attn_body, ctx=False),
        grid=(N_LAT_SEQ, LAT_Q_TILES),
        in_specs=[pl.BlockSpec((LAT_Q_TILE, MLA_QW), lambda b, j: (N_CTX // LAT_Q_TILE + b * LAT_Q_TILES + j, 0)),
                  pl.BlockSpec((1, lk, MLA_QW), lambda b, j: (b, 0, 0)),
                  pl.BlockSpec((1, lk, BRANCH), lambda b, j: (b, 0, 0)),
                  pl.BlockSpec((LAT_Q_TILE, BRANCH), lambda b, j: (N_CTX // LAT_Q_TILE + b * LAT_Q_TILES + j, 2))],
        out_specs=pl.BlockSpec((LAT_Q_TILE, BRANCH), lambda b, j: (b * LAT_Q_TILES + j, 0)),
        out_shape=jax.ShapeDtypeStruct((N_LAT, BRANCH), F32),
        compiler_params=_cparams("parallel", "parallel"),
        name="mla_attn_lat",
    )(q, k_lat, v_lat, z_d)
    return ctx, lat


S5_TAP = S5_CHUNK * S5_CH
S5_NCHUNK = N_TOK // S5_CHUNK
S5_CTX_CH = N_CTX // S5_CHUNK
S5_CTX_SEQ_CH = CTX_LEN // S5_CHUNK
S5_LAT_SEQ_CH = LAT_LEN // S5_CHUNK
S5_SCAN_STEPS = S5_LAT_SEQ_CH.bit_length() - 1
S5_TABLE_GROUPS = 4


def _s5_body(x_ref, mt_ref, bst_ref, cot_ref, a_ref, h0_ref, y_ref, fin_ref):
    x = x_ref[...].reshape(S5_TAP, S5_NCHUNK)
    y = _dot(mt_ref[0, 0], x)
    s = _dot(bst_ref[0, 0], x)
    lane = lax.broadcasted_iota(jnp.int32, (1, S5_NCHUNK), 1)
    is_lat = lane >= S5_CTX_CH
    pos_f = jnp.where(is_lat, (lane - S5_CTX_CH) & (S5_LAT_SEQ_CH - 1), lane & (S5_CTX_SEQ_CH - 1))
    pos_b = jnp.where(is_lat, S5_LAT_SEQ_CH - 1, S5_CTX_SEQ_CH - 1) - pos_f
    hin = []
    for d in range(2):
        n = S5_STATE
        sre, sim = s[2 * d * n:(2 * d + 1) * n], s[(2 * d + 1) * n:(2 * d + 2) * n]
        are = jnp.concatenate([a_ref[0, 0, 2 * d]] * (S5_NCHUNK // 128), axis=-1)
        aim = jnp.concatenate([a_ref[0, 0, 2 * d + 1]] * (S5_NCHUNK // 128), axis=-1)
        pos = pos_f if d == 0 else pos_b
        h0r, h0i = jnp.zeros_like(sre), jnp.zeros_like(sre)
        for b in range(N_LAT_SEQ):
            first = S5_CTX_CH + b * S5_LAT_SEQ_CH + (0 if d == 0 else S5_LAT_SEQ_CH - 1)
            h0r = jnp.where(lane == first, h0_ref[0, 2 * d][:, b:b + 1], h0r)
            h0i = jnp.where(lane == first, h0_ref[0, 2 * d + 1][:, b:b + 1], h0i)
        xr = sre + are * h0r - aim * h0i
        xi = sim + are * h0i + aim * h0r
        pr, pi = are, aim
        for j in range(S5_SCAN_STEPS):
            sh = 1 << j
            shift = sh if d == 0 else S5_NCHUNK - sh
            rr, ri = pltpu.roll(xr, shift, 1), pltpu.roll(xi, shift, 1)
            ok = pos >= sh
            xr, xi = (xr + jnp.where(ok, pr * rr - pi * ri, 0.0), xi + jnp.where(ok, pr * ri + pi * rr, 0.0))
            pr, pi = pr * pr - pi * pi, 2.0 * pr * pi
        last = lax.broadcasted_iota(jnp.int32, (1, 128), 1) * S5_CTX_SEQ_CH + (S5_CTX_SEQ_CH - 1 if d == 0 else 0)
        pick = jnp.where(lax.broadcasted_iota(jnp.int32, (S5_CTX_CH, 1), 0) == last, 1.0, 0.0)
        fin_ref[0, 2 * d] = _dot_sel(xr[:, :S5_CTX_CH], pick)
        fin_ref[0, 2 * d + 1] = _dot_sel(xi[:, :S5_CTX_CH], pick)
        one = 1 if d == 0 else S5_NCHUNK - 1
        hin.append(jnp.where(pos >= 1, pltpu.roll(xr, one, 1), h0r))
        hin.append(jnp.where(pos >= 1, pltpu.roll(xi, one, 1), h0i))
    y = y + _dot(cot_ref[0, 0], jnp.concatenate(hin, axis=0).astype(BF16))
    y_ref[...] = y.reshape(S5_CHUNK, S5_CH, S5_NCHUNK)


def _s5_scan(x_all, mt, bst, cot, a16, h0, l):
    g = S5_GROUPS
    sq = pl.BlockSpec((1, 1, S5_TAP, S5_TAP), lambda i: (l, i, 0, 0))
    st = pl.BlockSpec((1, 4, S5_STATE, 128), lambda i: (i, 0, 0, 0))
    return pl.pallas_call(
        _s5_body,
        grid=(g,),
        in_specs=[pl.BlockSpec((S5_CHUNK, S5_CH, S5_NCHUNK), lambda i: (0, i, 0)), sq, sq, sq,
                  pl.BlockSpec((1, 1, 4, S5_STATE, 128), lambda i: (l, i, 0, 0, 0)), st],
        out_specs=[pl.BlockSpec((S5_CHUNK, S5_CH, S5_NCHUNK), lambda i: (0, i, 0)),
                   pl.BlockSpec((1, 4, S5_STATE, 128), lambda i: (i, 0, 0, 0))],
        out_shape=[jax.ShapeDtypeStruct((S5_CHUNK, BRANCH, S5_NCHUNK), F32),
                   jax.ShapeDtypeStruct((g, 4, S5_STATE, 128), F32)],
        compiler_params=_cparams("parallel"),
        name="s5_scan",
    )(x_all, mt, bst, cot, a16, h0)


def _s5_gated(y_ssm, u, g, d_skip, w_glu):
    y = u * d_skip + y_ssm
    ge = 0.5 * y * (1.0 + jnp.tanh(0.7978845608028654 * (y + 0.044715 * (y * y * y))))
    gl = _dot(ge.astype(BF16), w_glu)
    return gl[:, :BRANCH] * (1.0 / (1.0 + jnp.exp(-gl[:, BRANCH:]))) * _silu(g)


HG_CHUNK = ROW_TILE
HG_W = 2 * HG_HEADS * HG_DK
HG_HEAD_W = 2 * HG_DK
HG_LAT_CHUNKS = LAT_LEN // HG_CHUNK
HG_CHUNKS = N_TOK // HG_CHUNK


def _hg_gates(z, lb):
    e = jnp.exp(-jnp.abs(z))
    r = 1.0 / (1.0 + e)
    sig_pos = jnp.where(z >= 0, r, e * r)
    sig_neg = jnp.where(z >= 0, e * r, r)
    return lb + (1.0 - lb) * sig_pos, (1.0 - lb) * sig_neg


def _bcast_row(x, period, r):
    n, w = x.shape
    if period >= 8:
        x3 = x.reshape(n // period, period, w)
        return jnp.broadcast_to(x3[:, r:r + 1, :], x3.shape).reshape(n, w)
    x3 = x.reshape(n // 8, 8, w)
    sub = lax.broadcasted_iota(jnp.int32, (1, 8, 1), 1)
    out = None
    for j in range(8 // period):
        b = jnp.broadcast_to(x3[:, j * period + r:j * period + r + 1, :], x3.shape)
        out = b if out is None else jnp.where(sub >= j * period, b, out)
    return out.reshape(n, w)


def _hg_scans(f, isb):
    n = f.shape[0]
    row = lax.broadcasted_iota(jnp.int32, (n, 1), 0)
    p, r = f, jnp.ones_like(f)
    levels = []
    h, sh = 1, 0
    while h < n:
        levels.append((h, sh, p, r))
        up = (row >> sh) & 1
        tot_p = jnp.where(isb == 1, _bcast_row(p, 2 * h, h), _bcast_row(p, 2 * h, h - 1))
        tot_r = jnp.where(isb == 1, _bcast_row(p, 2 * h, 0), _bcast_row(p, 2 * h, 2 * h - 1))
        p = p * jnp.where(up != isb, tot_p, 1.0)
        r = r * jnp.where(up == isb, tot_r, 1.0)
        h, sh = 2 * h, sh + 1
    return levels, p, r


def _hg_state_body(zf_ref, zb_ref, vf_ref, vb_ref, lb_ref, s0_ref, sf_out, sb_out, s_scr):
    i = pl.program_id(0)

    @pl.when(i % HG_LAT_CHUNKS == 0)
    def _():
        s_scr[...] = s0_ref[0]

    sf_out[0] = s_scr[:, 0:HG_DK, :]
    sb_out[0] = s_scr[:, HG_DK:, :]
    lane5 = lax.broadcasted_iota(jnp.int32, (1, HG_W), 1)
    isb = (lane5 >> 6) & 1
    z = jnp.where(isb == 1, zb_ref[...], zf_ref[...])
    f, k = _hg_gates(z, lb_ref[...])
    r, ptot = _hg_chunk_decay(f, isb)
    kt = k * r
    lane = lax.broadcasted_iota(jnp.int32, (1, BRANCH), 1)
    vf = vf_ref[...]
    vb = vb_ref[...]
    for hd in range(HG_HEADS):
        sl = slice(hd * HG_HEAD_W, (hd + 1) * HG_HEAD_W)
        kth = kt[:, sl].T.astype(BF16)
        hm = (lane >> 6) == hd
        d_f = _dot(kth, jnp.where(hm, vf, 0.0).astype(BF16))
        d_b = _dot(kth, jnp.where(hm, vb, 0.0).astype(BF16))
        ds = jnp.concatenate([d_f[:HG_DK], d_b[HG_DK:]], axis=0)
        pcol = jnp.broadcast_to(ptot[:, sl], (HG_HEAD_W, HG_HEAD_W)).T[:, 0:1]
        s_scr[hd] = s_scr[hd] * pcol + ds


def _hg_chunk_decay(f, isb):
    n = f.shape[0]
    row = lax.broadcasted_iota(jnp.int32, (n, 1), 0)
    dist = jnp.where(isb == 1, row, n - 1 - row)
    x = f
    sh = 1
    while sh < n:
        src = jnp.where(isb == 1, pltpu.roll(x, sh, 0), pltpu.roll(x, n - sh, 0))
        x = x * jnp.where(dist >= sh, src, 1.0)
        sh *= 2
    total = jnp.where(isb == 1, x[n - 1:n], x[0:1])
    nxt = jnp.where(isb == 1, pltpu.roll(x, 1, 0), pltpu.roll(x, n - 1, 0))
    return jnp.where(dist >= 1, nxt, 1.0), total


HG_LAT_STEPS = N_LAT_SEQ * HG_LAT_CHUNKS


def _hg_lat_rev(i):
    return (i // HG_LAT_CHUNKS) * HG_LAT_CHUNKS + (HG_LAT_CHUNKS - 1 - i % HG_LAT_CHUNKS)


def _hg_states(z_c, lb, s0):
    first = N_CTX_SEQ
    zz_f = pl.BlockSpec((HG_CHUNK, HG_W), lambda i: (first + i, 1))
    zz_b = pl.BlockSpec((HG_CHUNK, HG_W), lambda i: (first + _hg_lat_rev(i), 1))
    v_f = pl.BlockSpec((HG_CHUNK, BRANCH), lambda i: (first + i, 4))
    v_b = pl.BlockSpec((HG_CHUNK, BRANCH), lambda i: (first + _hg_lat_rev(i), 4))
    st = (HG_HEADS, HG_HEAD_W, BRANCH)
    half = (HG_HEADS, HG_DK, BRANCH)
    return pl.pallas_call(
        _hg_state_body,
        grid=(HG_LAT_STEPS,),
        in_specs=[zz_f, zz_b, v_f, v_b,
                  pl.BlockSpec((1, HG_W), lambda i: (0, 0)),
                  pl.BlockSpec((1,) + st, lambda i: (i // HG_LAT_CHUNKS, 0, 0, 0))],
        out_specs=[pl.BlockSpec((1,) + half, lambda i: (i, 0, 0, 0)),
                   pl.BlockSpec((1,) + half, lambda i: (_hg_lat_rev(i), 0, 0, 0))],
        out_shape=[jax.ShapeDtypeStruct((HG_LAT_STEPS,) + half, F32),
                   jax.ShapeDtypeStruct((HG_LAT_STEPS,) + half, F32)],
        scratch_shapes=[pltpu.VMEM(st, F32)],
        compiler_params=_cparams("arbitrary"),
        name="hg_states",
    )(z_c, z_c, z_c, z_c, lb, s0)


def _hg_main_body(qq_ref, zz_ref, v_ref, g_ref, sf_ref, sb_ref, lb_ref, ng_ref, o_ref, fin_ref):
    n = HG_CHUNK
    i = pl.program_id(0)
    qq = qq_ref[...]
    lane5 = lax.broadcasted_iota(jnp.int32, (1, HG_W), 1)
    isb = (lane5 >> 6) & 1
    f, k = _hg_gates(zz_ref[...], lb_ref[...])
    levels, pfull, rfull = _hg_scans(f, isb)
    row = lax.broadcasted_iota(jnp.int32, (n, 1), 0)
    col = lax.broadcasted_iota(jnp.int32, (1, n), 1)
    ops = [(qq.astype(BF16), k.astype(BF16), row == col)]
    for h, sh, p, r in levels:
        up = (row >> sh) & 1
        qt = jnp.where(up != isb, qq * p, 0.0).astype(BF16)
        kt = jnp.where(up == isb, k * r, 0.0).astype(BF16)
        ops.append((qt, kt, (row >> (sh + 1)) == (col >> (sh + 1))))
    qc = (qq * pfull).astype(BF16)
    v = v_ref[...]
    vb = v.astype(BF16)
    lane = lax.broadcasted_iota(jnp.int32, (1, BRANCH), 1)
    latent = i >= N_CTX_SEQ
    acc = jnp.zeros((n, BRANCH), F32)
    for hd in range(HG_HEADS):
        sl = slice(hd * HG_HEAD_W, (hd + 1) * HG_HEAD_W)
        a = jnp.zeros((n, n), F32)
        for qt, kt, mask in ops:
            a = a + jnp.where(mask, _dot_nt(qt[:, sl], kt[:, sl]), 0.0)
        s_in = jnp.concatenate([sf_ref[0, hd], sb_ref[0, hd]], axis=0)
        s_in = jnp.where(latent, s_in, 0.0).astype(BF16)
        o_h = _dot(a.astype(BF16), vb) + _dot(qc[:, sl], s_in)
        acc = jnp.where((lane >> 6) == hd, o_h, acc)
    sq = acc * acc
    ms = jnp.zeros((n, BRANCH), F32)
    for hd in range(HG_HEADS):
        hm = (lane >> 6) == hd
        ms = jnp.where(hm, jnp.sum(jnp.where(hm, sq, 0.0), axis=-1, keepdims=True), ms)
    o_ref[...] = acc * lax.rsqrt(ms * (1.0 / HG_DK) + EPS) * ng_ref[...] * _silu(g_ref[...])

    @pl.when(i < N_CTX_SEQ)
    def _():
        kt_full = k * rfull
        for hd in range(HG_HEADS):
            kth = kt_full[:, hd * HG_HEAD_W:(hd + 1) * HG_HEAD_W].T.astype(BF16)
            ds = _dot(kth, jnp.where((lane >> 6) == hd, v, 0.0).astype(BF16))
            fin_ref[0, hd] = ds[:, hd * HG_DK:(hd + 1) * HG_DK]


def _hg_main(z_c, s_f, s_b, lb, norm_g):
    half = (1, HG_HEADS, HG_DK, BRANCH)
    lat = lambda i: (jnp.maximum(i - N_CTX_SEQ, 0), 0, 0, 0)
    fin = (HG_HEADS, HG_HEAD_W, HG_DK)
    return pl.pallas_call(
        _hg_main_body,
        grid=(HG_CHUNKS,),
        in_specs=[pl.BlockSpec((HG_CHUNK, HG_W), lambda i: (i, 0)),
                  pl.BlockSpec((HG_CHUNK, HG_W), lambda i: (i, 1)),
                  pl.BlockSpec((HG_CHUNK, BRANCH), lambda i: (i, 4)),
                  pl.BlockSpec((HG_CHUNK, BRANCH), lambda i: (i, 5)),
                  pl.BlockSpec(half, lat),
                  pl.BlockSpec(half, lat),
                  pl.BlockSpec((1, HG_W), lambda i: (0, 0)),
                  pl.BlockSpec((1, BRANCH), lambda i: (0, 0))],
        out_specs=[pl.BlockSpec((HG_CHUNK, BRANCH), lambda i: (i, 0)),
                   pl.BlockSpec((1,) + fin, lambda i: (jnp.minimum(i, N_CTX_SEQ - 1), 0, 0, 0))],
        out_shape=[jax.ShapeDtypeStruct((N_TOK, BRANCH), F32),
                   jax.ShapeDtypeStruct((N_CTX_SEQ,) + fin, F32)],
        compiler_params=_cparams("arbitrary"),
        name="hg_main",
    )(z_c, z_c, z_c, z_c, s_f, s_b, lb, norm_g)


def _take_cols(w, plan):
    idx = np.concatenate([p[0] for p in plan]).astype(np.int32)
    sign = np.concatenate([np.broadcast_to(p[1], p[0].shape) for p in plan]).astype(np.float32)
    return jnp.take(w, jnp.asarray(idx), axis=-1) * jnp.asarray(sign)


def _zeros(n):
    return (np.zeros(n, np.int64), 0.0)


_IN_OFF = {}
_off = 0
for _name, _n in (("da_q", 256), ("da_k", 256), ("da_v", 256), ("da_g", 256), ("s5_u", 256), ("s5_g", 256),
                  ("hg_q", 256), ("hg_ff", 256), ("hg_fb", 256), ("hg_i", 256), ("hg_g", 256),
                  ("mla_cq", MLA_Q_RANK), ("mla_ckv", MLA_KV_RANK), ("mla_kr", MLA_ROPE), ("mla_g", 256)):
    _IN_OFF[_name] = np.arange(_off, _off + _n)
    _off += _n


def _rope_tables():
    t = np.arange(LAT_LEN)
    pos = np.stack([t // GRID_W, t % GRID_W], axis=1).astype(np.float32)
    inv_freq = (np.float32(ROPE_BASE) ** (-np.arange(8, dtype=np.float32) / np.float32(8))).astype(np.float32)
    r = np.arange(MLA_ROPE)
    ang = (pos[:, r // 16] * inv_freq[r % 8][None, :]).astype(np.float64)
    cos32, sin32 = np.cos(ang).astype(np.float32), np.sin(ang).astype(np.float32)
    lo = (np.arange(MLA_ROPE) % 16 < 8)[None, :]
    sin_lo32, sin_hi32 = np.where(lo, -sin32, 0.0).astype(np.float32), np.where(lo, 0.0, sin32).astype(np.float32)
    da_tabs = tuple(np.tile(x, (1, 8)) for x in (cos32, sin_lo32, sin_hi32))

    def head(x, fill):
        h = np.concatenate([np.full((LAT_LEN, MLA_NOPE), fill, np.float32), x,
                            np.full((LAT_LEN, MLA_HEAD_PAD - MLA_NOPE - MLA_ROPE), fill, np.float32)], axis=1)
        return np.concatenate([h, np.full((BIG_TILE, MLA_HEAD_PAD), fill, np.float32)], axis=0)
    k_tabs = (head(cos32, 1.0), head(sin_lo32, 0.0), head(sin_hi32, 0.0))
    return tuple(jnp.asarray(x) for x in da_tabs), tuple(jnp.asarray(x) for x in k_tabs)


def _mla_weights(w_uq, w_ukv, q_norm):
    hd = MLA_NOPE + MLA_ROPE
    pad_tail = _zeros(MLA_HEAD_PAD - hd)
    q_plan, k_plan, v_plan = [], [], []
    for h in range(MLA_HEADS):
        nope, rope = np.arange(h * hd, h * hd + MLA_NOPE), np.arange(h * hd + MLA_NOPE, (h + 1) * hd)
        q_plan += [(nope, 1.0), (rope, 1.0), pad_tail]
        k_plan += [(np.arange(h * 2 * MLA_NOPE, h * 2 * MLA_NOPE + MLA_NOPE), 1.0), _zeros(MLA_HEAD_PAD - MLA_NOPE)]
        v_plan += [(np.arange(h * 2 * MLA_NOPE + MLA_NOPE, (h + 1) * 2 * MLA_NOPE), 1.0)]
    pad_rows = lambda x: jnp.pad(x, ((0, 256 - MLA_Q_RANK), (0, 0))).astype(BF16)
    qn = jnp.pad(q_norm, (0, 256 - MLA_Q_RANK)).reshape(1, 256)
    return (pad_rows(_take_cols(w_uq, q_plan)), _take_cols(w_ukv, k_plan).astype(BF16),
            _take_cols(w_ukv, v_plan).astype(BF16), qn)


def _split_bf16(a):
    hi = a.astype(BF16)
    return hi, (a - hi.astype(F32)).astype(BF16)


def _dot_sel(a, sel):
    hi, lo = _split_bf16(a)
    sel = sel.astype(BF16)
    return _dot(hi, sel) + _dot(lo, sel)


def _dot_x3(a, b):
    a_hi, a_lo = _split_bf16(a)
    b_hi, b_lo = _split_bf16(b)
    return _dot(a_hi, b_hi) + _dot(a_hi, b_lo) + _dot(a_lo, b_hi)


def _s5_table_body(xy_ref, bb_ref, c_ref, ct_ref, mt_ref, bst_ref, cot_ref, a_ref):
    n, t, ch = S5_STATE, S5_CHUNK, S5_CH
    wide = 2 * S5_TAP
    tau_i = lax.broadcasted_iota(jnp.int32, (1, 128), 1)
    tau = tau_i.astype(F32)
    sel_row = lax.broadcasted_iota(jnp.int32, (128, 1), 0)

    def lag(width):
        return lax.broadcasted_iota(jnp.int32, (1, width), 1) >> 4

    def onehot(cond):
        return jnp.where(cond, 1.0, 0.0).astype(F32)
    j = lag(wide)
    e_z = (onehot((j <= t - 1) & (sel_row == t - 1 - j)), onehot((j >= t - 1) & (j <= 2 * t - 2) & (sel_row == j - (t - 1))))
    jc = lag(S5_TAP)
    e_c = (onehot(sel_row == jc + 1), onehot(sel_row == t - jc))
    ch_row = lax.broadcasted_iota(jnp.int32, (ch, 1), 0)
    tile_w = onehot((lax.broadcasted_iota(jnp.int32, (1, wide), 1) & (ch - 1)) == ch_row)
    tile_n = onehot((lax.broadcasted_iota(jnp.int32, (1, S5_TAP), 1) & (ch - 1)) == ch_row)

    for gi in range(S5_TABLE_GROUPS):
        xy = xy_ref[0, gi]
        z, cot_rows, klong = [], [], None
        for d in range(2):
            x, y = xy[:, 2 * d:2 * d + 1], xy[:, 2 * d + 1:2 * d + 2]
            mag = jnp.exp(jnp.where(tau_i <= t, tau, 0.0) * x)
            ang = jnp.where(tau_i <= t, tau, 0.0) * y
            p_re = jnp.where(tau_i <= t, mag * jnp.cos(ang), 0.0)
            p_im = jnp.where(tau_i <= t, mag * jnp.sin(ang), 0.0)
            a_ref[0, gi, 2 * d] = jnp.broadcast_to(p_re[:, t:t + 1], (n, 128))
            a_ref[0, gi, 2 * d + 1] = jnp.broadcast_to(p_im[:, t:t + 1], (n, 128))
            pz_re, pz_im = _dot_sel(p_re, e_z[d]), _dot_sel(p_im, e_z[d])
            b_re, b_im = _dot_sel(bb_ref[0, gi, 2 * d], tile_w), _dot_sel(bb_ref[0, gi, 2 * d + 1], tile_w)
            z_re, z_im = pz_re * b_re - pz_im * b_im, pz_re * b_im + pz_im * b_re
            z += [z_re, z_im]
            part = _dot_x3(c_ref[0, gi, 2 * d], z_re) - _dot_x3(c_ref[0, gi, 2 * d + 1], z_im)
            klong = part if klong is None else klong + part
            pc_re, pc_im = _dot_sel(p_re, e_c[d]), _dot_sel(p_im, e_c[d])
            c_re, c_im = _dot_sel(ct_ref[0, gi, 2 * d], tile_n), _dot_sel(ct_ref[0, gi, 2 * d + 1], tile_n)
            cot_rows += [c_re * pc_re - c_im * pc_im, -(c_re * pc_im + c_im * pc_re)]
        for tt in range(t):
            off = (t - 1 - tt) * ch
            win = klong if off == 0 else pltpu.roll(klong, wide - off, 1)
            mt_ref[0, gi, tt * ch:(tt + 1) * ch, :] = win[:, :S5_TAP].astype(BF16)
        back = pltpu.roll(z[2], wide - (t - 1) * ch, 1), pltpu.roll(z[3], wide - (t - 1) * ch, 1)
        for k, rows in enumerate((z[0], z[1], back[0], back[1])):
            bst_ref[0, gi, k * n:(k + 1) * n, :] = rows[:, :S5_TAP].astype(BF16)
        cot_ref[0, gi] = jnp.concatenate(cot_rows, axis=0).T.astype(BF16)


def _s5_tables(a_re, a_im, log_dt, b_re, b_im, c_re, c_im):
    nl, g, n, ch = a_re.shape[0], S5_GROUPS, S5_STATE, S5_CH
    step = jnp.exp(log_dt)[..., None]
    mag = jnp.exp(a_re * step)
    ab_re, ab_im = mag * jnp.cos(a_im * step), mag * jnp.sin(a_im * step)
    den = a_re * a_re + a_im * a_im
    f_re = ((ab_re - 1.0) * a_re + ab_im * a_im) / den
    f_im = (ab_im * a_re - (ab_re - 1.0) * a_im) / den
    bb_re = f_re[..., None] * b_re - f_im[..., None] * b_im
    bb_im = f_re[..., None] * b_im + f_im[..., None] * b_re
    by_group = lambda x: jnp.moveaxis(x, 1, 2)
    pair = lambda re, im: jnp.stack([by_group(re), by_group(im)], axis=3).reshape((nl, g, 4) + re.shape[3:])
    xy = jnp.stack([by_group(a_re * step), by_group(a_im * step)], axis=3).reshape(nl, g, 4, n)
    xy = jnp.pad(jnp.swapaxes(xy, 2, 3), ((0, 0), (0, 0), (0, 0), (0, 4)))
    gs = S5_TABLE_GROUPS
    mat = pl.BlockSpec((1, gs, S5_TAP, S5_TAP), lambda l, i: (l, i, 0, 0))
    return pl.pallas_call(
        _s5_table_body,
        grid=(nl, g // gs),
        in_specs=[pl.BlockSpec((1, gs, n, 8), lambda l, i: (l, i, 0, 0)),
                  pl.BlockSpec((1, gs, 4, n, ch), lambda l, i: (l, i, 0, 0, 0)),
                  pl.BlockSpec((1, gs, 4, ch, n), lambda l, i: (l, i, 0, 0, 0)),
                  pl.BlockSpec((1, gs, 4, n, ch), lambda l, i: (l, i, 0, 0, 0))],
        out_specs=[mat, mat, mat, pl.BlockSpec((1, gs, 4, n, 128), lambda l, i: (l, i, 0, 0, 0))],
        out_shape=[jax.ShapeDtypeStruct((nl, g, S5_TAP, S5_TAP), BF16)] * 3
        + [jax.ShapeDtypeStruct((nl, g, 4, n, 128), F32)],
        compiler_params=_cparams("parallel", "parallel"),
        name="s5_tables",
    )(xy, pair(bb_re, bb_im), pair(c_re, c_im), pair(jnp.swapaxes(c_re, -1, -2), jnp.swapaxes(c_im, -1, -2)))


def _s5_chunk_lanes(u):
    return u.reshape(S5_NCHUNK, S5_CHUNK, BRANCH).transpose(1, 2, 0)


def _s5_token_rows(y):
    return y.transpose(2, 0, 1).reshape(N_TOK, BRANCH)


def kernel(x_prompt, x_sample, cache_diff_k, cache_diff_v, state_s5, state_hgrn, cache_mla_ckv, cache_mla_krope, c, c_ctx, w_mod, b_mod, w_in, w_out, da_lambda, da_norm, s5_a_re, s5_a_im, s5_log_dt, s5_b_re, s5_b_im, s5_c_re, s5_c_im, s5_d, s5_w_glu, hg_lb, hg_norm, mla_q_norm, mla_w_uq, mla_kv_norm, mla_w_ukv, final_norm):
    lb_w = jax.nn.softmax(hg_lb.astype(F32), axis=0)
    lb_all = jnp.cumsum(lb_w, axis=0) - lb_w[0:1]
    c_rows = jnp.concatenate([c_ctx[None], c, jnp.zeros((8 - 1 - N_LAT_SEQ, D_MODEL), F32)], axis=0)
    mods = _modulation(c_rows, w_mod, b_mod)
    da_tabs, mla_tabs = _rope_tables()
    xs = (x_prompt.reshape(N_CTX, D_MODEL), x_sample.reshape(N_LAT, D_MODEL))
    new_k, new_v, new_s5, new_hg, new_ckv, new_kr = [], [], [], [], [], []
    s5_tabs = _s5_tables(s5_a_re, s5_a_im, s5_log_dt, s5_b_re, s5_b_im, s5_c_re, s5_c_im)
    w_all = _arrange_w_in(w_in)
    for l in range(DEPTH):
        mod = mods[l, :3].reshape(3, 3, D_MODEL)
        z_a, z_b, z_c, z_d, k_new, v_new, u_bf = _in_proj(xs, mod, w_all, l)

        lam_init = 0.8 - 0.6 * math.exp(-0.3 * l)
        kv_lat = _da_latent_kv(z_a, da_tabs,
                               cache_diff_k[:, l].reshape(N_LAT_SEQ, PAST_LEN, BRANCH),
                               cache_diff_v[:, l].reshape(N_LAT_SEQ, PAST_LEN, BRANCH))
        a_out = _da_attention(z_a, da_lambda[l], da_norm[l], lam_init, da_tabs, kv_lat)
        new_k.append(k_new)
        new_v.append(v_new)

        h0 = state_s5[:, l].transpose(2, 1, 4, 3, 0).reshape(S5_GROUPS, 4, S5_STATE, N_LAT_SEQ)
        h0 = jnp.pad(h0, ((0, 0), (0, 0), (0, 0), (0, 128 - N_LAT_SEQ)))
        y_all, fin = _s5_scan(_s5_chunk_lanes(u_bf), *s5_tabs, h0, l)
        b_out = (_s5_token_rows(y_all), z_b, s5_d[l].reshape(1, BRANCH), s5_w_glu[l].astype(BF16))
        fin = fin[..., :N_CTX_SEQ].reshape(S5_GROUPS, 2, 2, S5_STATE, N_CTX_SEQ)
        new_s5.append(fin.transpose(4, 1, 0, 3, 2))

        lb = jnp.concatenate([lb_all[l, 0].reshape(HG_HEADS, HG_DK), lb_all[l, 1].reshape(HG_HEADS, HG_DK)],
                             axis=-1).reshape(1, HG_W)
        head_eye = jnp.eye(HG_HEADS, dtype=F32)
        s0 = state_hgrn[:, l].transpose(0, 2, 1, 3, 4).reshape(N_LAT_SEQ, HG_HEADS, HG_HEAD_W, 1, HG_DK)
        s0 = (s0 * head_eye[None, :, None, :, None]).reshape(N_LAT_SEQ, HG_HEADS, HG_HEAD_W, BRANCH)
        s_f, s_b = _hg_states(z_c, lb, s0)
        c_out, s_fin = _hg_main(z_c, s_f, s_b, lb, jnp.tile(hg_norm[l].reshape(1, HG_DK), (1, HG_HEADS)))
        new_hg.append(s_fin.reshape(N_CTX_SEQ, HG_HEADS, 2, HG_DK, HG_DK).transpose(0, 2, 1, 3, 4))

        wq, wk, wv, qn = _mla_weights(mla_w_uq[l], mla_w_ukv[l], mla_q_norm[l])
        q, ckv_n, kr = _mla_prep(z_d, mla_tabs, qn, mla_kv_norm[l].reshape(1, MLA_KV_RANK), wq)
        kr_cache = jnp.pad(cache_mla_krope[:, l], ((0, 0), (0, 0), (MLA_NOPE, 128 - MLA_NOPE - MLA_ROPE)))
        k_ctx, v_ctx, k_lat, v_lat = _mla_kv(ckv_n, kr, cache_mla_ckv[:, l], kr_cache, wk, wv)
        d_out = _mla_attention(z_d, q, k_ctx, v_ctx, k_lat, v_lat)
        new_ckv.append(ckv_n[:N_CTX].reshape(N_CTX_SEQ, CTX_LEN, MLA_KV_RANK))
        new_kr.append(kr[:N_CTX, MLA_NOPE:MLA_NOPE + MLA_ROPE].reshape(N_CTX_SEQ, CTX_LEN, MLA_ROPE))

        xs = _out_proj(a_out, b_out, c_out, d_out, xs, mod, w_out, l,
                       final_norm.reshape(1, D_MODEL), final=(l == DEPTH - 1))
        xs = tuple(xs) if l == DEPTH - 1 else (xs,)
    y_prompt = xs[0].reshape(N_CTX_SEQ, CTX_LEN, D_MODEL)
    y_sample = xs[1].reshape(N_LAT_SEQ, LAT_LEN, D_MODEL)
    st = lambda parts: jnp.stack(parts, axis=1)
    heads_last = lambda kv: kv.reshape(N_CTX_SEQ, DEPTH, DA_HEADS, 2 * DA_QK, CTX_LEN).transpose(0, 1, 4, 2, 3)
    return (y_prompt, y_sample, heads_last(st(new_k)), heads_last(st(new_v)), st(new_s5), st(new_hg), st(new_ckv), st(new_kr))
```

```python
import functools
import math

import numpy as np

import jax
import jax.numpy as jnp
from jax import lax
from jax.experimental import pallas as pl
from jax.experimental.pallas import tpu as pltpu

F32 = jnp.float32
BF16 = jnp.bfloat16

D_MODEL = 1024
DEPTH = 2
N_CTX_SEQ = 16
CTX_LEN = 256
N_LAT_SEQ = 2
LAT_LEN = 2048
PAST_LEN = 256
GRID_W = 64
N_CTX = N_CTX_SEQ * CTX_LEN
N_LAT = N_LAT_SEQ * LAT_LEN
N_TOK = N_CTX + N_LAT
BRANCH = 256
EPS = 1e-6
ROPE_BASE = 10000.0
ROW_TILE = 256
LAT_TILES = LAT_LEN // ROW_TILE
N_TILES = N_TOK // ROW_TILE
CTX_TILES = N_CTX // ROW_TILE
VMEM_LIMIT = 48 * 1024 * 1024
IN_PROJ_VMEM_LIMIT = 56 * 1024 * 1024
LAT_Q_TILE = 512
LAT_Q_TILES = LAT_LEN // LAT_Q_TILE
BIG_TILE = 512
CTX_SEQ_PER_STEP = 2

DA_HEADS = 4
DA_QK = 32
MLA_HEADS = 4
MLA_NOPE = 64
MLA_ROPE = 32
MLA_Q_RANK = 192
MLA_KV_RANK = 128
S5_GROUPS = 16
S5_CH = 16
S5_STATE = 64
S5_CHUNK = 16
HG_HEADS = 4
HG_DK = 64

W_A = 1024
W_B = 512
W_C = 1536
W_D = 768
W_ALL = W_A + W_B + W_C + W_D


def _cparams(*sem):
    return pltpu.CompilerParams(dimension_semantics=sem, vmem_limit_bytes=VMEM_LIMIT)


def _tile_seq(i, tile=ROW_TILE):
    return jnp.where(i < N_CTX // tile, 0, 1 + (i - N_CTX // tile) // (LAT_LEN // tile))


def _silu(x):
    return x * (1.0 / (1.0 + jnp.exp(-x)))


def _dot(a, b):
    return jnp.dot(a, b, preferred_element_type=F32)


def _dot_nt(a, b):
    return lax.dot_general(a, b, (((1,), (1,)), ((), ())), preferred_element_type=F32)


def _mod_body(c_ref, w_ref, b_ref, o_ref):
    c = _silu(c_ref[...]).astype(BF16)
    o_ref[0] = _dot(c, w_ref[0].astype(BF16)) + b_ref[0]


def _modulation(c_rows, w_mod, b_mod):
    tn = 768
    return pl.pallas_call(
        _mod_body,
        grid=(DEPTH, 3 * D_MODEL // tn),
        in_specs=[pl.BlockSpec((8, D_MODEL), lambda l, j: (0, 0)),
                  pl.BlockSpec((1, D_MODEL, tn), lambda l, j: (l, 0, j)),
                  pl.BlockSpec((1, 1, tn), lambda l, j: (l, 0, j))],
        out_specs=pl.BlockSpec((1, 8, tn), lambda l, j: (l, 0, j)),
        out_shape=jax.ShapeDtypeStruct((DEPTH, 8, 3 * D_MODEL), F32),
        compiler_params=_cparams("parallel", "parallel"),
        name="modulation",
    )(c_rows, w_mod, b_mod.reshape(DEPTH, 1, 3 * D_MODEL))


def _split_rows(i, ctx_ref, lat_ref):
    return jnp.where(i < N_CTX // ctx_ref.shape[0], ctx_ref[...], lat_ref[...])


def _ctx_tile_spec(w, tile=ROW_TILE):
    return pl.BlockSpec((tile, w), lambda i: (jnp.minimum(i, N_CTX // tile - 1), 0))


def _lat_tile_spec(w, tile=ROW_TILE):
    return pl.BlockSpec((tile, w), lambda i: (jnp.maximum(i - N_CTX // tile, 0), 0))


def _in_proj_body(*refs, split):
    if split:
        xc_ref, xl_ref, mod_ref, w_ref, oa, ob, oc, od, ok, ov, ou = refs
        x = _split_rows(pl.program_id(0), xc_ref, xl_ref)
    else:
        x_ref, mod_ref, w_ref, oa, ob, oc, od, ok, ov, ou = refs
        x = x_ref[...]
    xn = x * lax.rsqrt(jnp.mean(x * x, axis=-1, keepdims=True) + EPS)
    mod = mod_ref[0]
    h = (xn * (1.0 + mod[1:2]) + mod[0:1]).astype(BF16)
    off = 0
    for o in (oa, ob, oc, od):
        w = o.shape[-1]
        o[...] = _dot(h, w_ref[0, :, off:off + w])
        off += w
    ou[...] = ob[:, :BRANCH].astype(BF16)

    @pl.when(pl.program_id(0) < N_CTX // BIG_TILE)
    def _():
        for t in range(BIG_TILE // CTX_LEN):
            r = slice(t * CTX_LEN, (t + 1) * CTX_LEN)
            ok[t] = oa[r, BRANCH:2 * BRANCH].T
            ov[t] = oa[r, 2 * BRANCH:3 * BRANCH].T


def _in_proj(xs, mod, w_all, l):
    widths = (W_A, W_B, W_C, W_D)
    split = len(xs) == 2
    x_specs = ([_ctx_tile_spec(D_MODEL, BIG_TILE), _lat_tile_spec(D_MODEL, BIG_TILE)] if split
               else [pl.BlockSpec((BIG_TILE, D_MODEL), lambda i: (i, 0))])
    return pl.pallas_call(
        functools.partial(_in_proj_body, split=split),
        grid=(N_TOK // BIG_TILE,),
        in_specs=x_specs + [pl.BlockSpec((1, 3, D_MODEL), lambda i: (_tile_seq(i, BIG_TILE), 0, 0)),
                            pl.BlockSpec((1, D_MODEL, W_ALL), lambda i: (l, 0, 0))],
        out_specs=[pl.BlockSpec((BIG_TILE, w), lambda i: (i, 0)) for w in widths]
        + [pl.BlockSpec((BIG_TILE // CTX_LEN, BRANCH, CTX_LEN), lambda i: (jnp.minimum(i, N_CTX // BIG_TILE - 1), 0, 0))] * 2
        + [pl.BlockSpec((BIG_TILE, BRANCH), lambda i: (i, 0))],
        out_shape=[jax.ShapeDtypeStruct((N_TOK, w), F32) for w in widths]
        + [jax.ShapeDtypeStruct((N_CTX_SEQ, BRANCH, CTX_LEN), F32)] * 2 + [jax.ShapeDtypeStruct((N_TOK, BRANCH), BF16)],
        compiler_params=pltpu.CompilerParams(dimension_semantics=("arbitrary",), vmem_limit_bytes=IN_PROJ_VMEM_LIMIT),
        name="in_proj",
    )(*xs, mod, w_all)


def _arrange_body(wt_ref, o_ref):
    w = wt_ref[0]
    c = _IN_OFF
    rows = lambda name: w[c[name][0]:c[name][-1] + 1]
    zero = lambda n: jnp.zeros((n, w.shape[1]), F32)

    def per_head(x, y):
        xs, ys = rows(x), rows(y)
        return [p for h in range(HG_HEADS) for p in (xs[h * HG_DK:(h + 1) * HG_DK], ys[h * HG_DK:(h + 1) * HG_DK])]
    pieces = ([w[0:W_A + W_B]] + per_head("hg_q", "hg_q") + per_head("hg_ff", "hg_fb") + [rows("hg_i"), rows("hg_g")]
              + [rows("mla_cq"), zero(256 - MLA_Q_RANK), rows("mla_ckv"), zero(MLA_NOPE), rows("mla_kr"),
                 zero(128 - MLA_NOPE - MLA_ROPE), rows("mla_g")])
    o_ref[0] = jnp.concatenate(pieces, axis=0).T.astype(BF16)


def _arrange_w_in(w_in):
    lanes = 256
    wt = jnp.swapaxes(w_in, 1, 2)
    return pl.pallas_call(
        _arrange_body,
        grid=(DEPTH, D_MODEL // lanes),
        in_specs=[pl.BlockSpec((1, wt.shape[1], lanes), lambda l, i: (l, 0, i))],
        out_specs=pl.BlockSpec((1, lanes, W_ALL), lambda l, i: (l, i, 0)),
        out_shape=jax.ShapeDtypeStruct((DEPTH, D_MODEL, W_ALL), BF16),
        compiler_params=_cparams("parallel", "parallel"),
        name="arrange_w_in",
    )(wt)


def _out_proj_body(*refs, split_in, final):
    ac_ref, al_ref, y_ref, u_ref, g_ref, dsk_ref, wglu_ref, c_ref, dc_ref, dl_ref = refs[:10]
    i = pl.program_id(0)
    if split_in:
        xc_ref, xl_ref, mod_ref, w_ref, fn_ref = refs[10:15]
        x = _split_rows(i, xc_ref, xl_ref)
    else:
        x_ref, mod_ref, w_ref, fn_ref = refs[10:14]
        x = x_ref[...]
    b_out = _s5_gated(y_ref[...], u_ref[...], g_ref[...], dsk_ref[...], wglu_ref[...])
    branches = (_split_rows(i, ac_ref, al_ref), b_out, c_ref[...], _split_rows(i, dc_ref, dl_ref))
    acc = None
    for j, r in enumerate(branches):
        t = _dot(r.astype(BF16), w_ref[0, j * BRANCH:(j + 1) * BRANCH, :].astype(BF16))
        acc = t if acc is None else acc + t
    x = x + mod_ref[0][2:3] * acc
    if not final:
        refs[-1][...] = x
        return
    y = x * lax.rsqrt(jnp.mean(x * x, axis=-1, keepdims=True) + EPS) * fn_ref[...]
    yc_ref, yl_ref = refs[-2:]

    @pl.when(i < N_CTX // BIG_TILE)
    def _():
        yc_ref[...] = y

    @pl.when(i >= N_CTX // BIG_TILE)
    def _():
        yl_ref[...] = y


def _out_proj(a, s5, c, d, xs, mod, w_out, l, final_norm, final):
    y_ssm, z_b, d_skip, w_glu = s5
    br = pl.BlockSpec((BIG_TILE, BRANCH), lambda i: (i, 0))
    s5_specs = [br, br, pl.BlockSpec((BIG_TILE, BRANCH), lambda i: (i, 1)),
                pl.BlockSpec((1, BRANCH), lambda i: (0, 0)), pl.BlockSpec((BRANCH, 2 * BRANCH), lambda i: (0, 0))]
    pair = [_ctx_tile_spec(BRANCH, BIG_TILE), _lat_tile_spec(BRANCH, BIG_TILE)]
    split_in = len(xs) == 2
    x_specs = ([_ctx_tile_spec(D_MODEL, BIG_TILE), _lat_tile_spec(D_MODEL, BIG_TILE)] if split_in
               else [pl.BlockSpec((BIG_TILE, D_MODEL), lambda i: (i, 0))])
    if final:
        out_specs = [_ctx_tile_spec(D_MODEL, BIG_TILE), _lat_tile_spec(D_MODEL, BIG_TILE)]
        out_shape = [jax.ShapeDtypeStruct((N_CTX, D_MODEL), F32), jax.ShapeDtypeStruct((N_LAT, D_MODEL), F32)]
    else:
        out_specs = pl.BlockSpec((BIG_TILE, D_MODEL), lambda i: (i, 0))
        out_shape = jax.ShapeDtypeStruct((N_TOK, D_MODEL), F32)
    return pl.pallas_call(
        functools.partial(_out_proj_body, split_in=split_in, final=final),
        grid=(N_TOK // BIG_TILE,),
        in_specs=pair + s5_specs + [br] + pair + x_specs + [
            pl.BlockSpec((1, 3, D_MODEL), lambda i: (_tile_seq(i, BIG_TILE), 0, 0)),
            pl.BlockSpec((1, D_MODEL, D_MODEL), lambda i: (l, 0, 0)),
            pl.BlockSpec((1, D_MODEL), lambda i: (0, 0))],
        out_specs=out_specs,
        out_shape=out_shape,
        compiler_params=_cparams("arbitrary"),
        name="out_proj",
    )(*a, y_ssm, z_b, z_b, d_skip, w_glu, c, *d, *xs, mod, w_out, final_norm)


LOG2E = 1.4426950408889634


def _exp2_rows(s):
    e = jnp.exp2(s - jnp.max(s, axis=-1, keepdims=True))
    return e, jnp.sum(e, axis=-1, keepdims=True)


def _rope(x, cos, sin_lo, sin_hi):
    w = x.shape[-1]
    return x * cos + pltpu.roll(x, w - 8, 1) * sin_lo + pltpu.roll(x, 8, 1) * sin_hi


def _da_kv_body(k_ref, v_ref, cos_ref, slo_ref, shi_ref, ck_ref, cv_ref, ko_ref, vo_ref):
    j = pl.program_id(1)

    @pl.when(j < LAT_TILES)
    def _():
        ko_ref[0] = _rope(k_ref[...], cos_ref[...], slo_ref[...], shi_ref[...]).astype(BF16)
        vo_ref[0] = v_ref[...].astype(BF16)

    @pl.when(j == LAT_TILES)
    def _():
        ko_ref[0] = ck_ref[0].astype(BF16)
        vo_ref[0] = cv_ref[0].astype(BF16)


def _da_latent_kv(z_a, tabs, cache_k, cache_v):
    def rows(col):
        return pl.BlockSpec(
            (ROW_TILE, BRANCH),
            lambda b, j: (CTX_TILES + b * LAT_TILES + jnp.minimum(j, LAT_TILES - 1), col))
    tab = pl.BlockSpec((ROW_TILE, BRANCH), lambda b, j: (jnp.minimum(j, LAT_TILES - 1), 0))
    cache = pl.BlockSpec((1, PAST_LEN, BRANCH), lambda b, j: (b, 0, 0))
    out = pl.BlockSpec((1, ROW_TILE, BRANCH), lambda b, j: (b, j, 0))
    shp = jax.ShapeDtypeStruct((N_LAT_SEQ, LAT_LEN + PAST_LEN, BRANCH), BF16)
    return pl.pallas_call(
        _da_kv_body,
        grid=(N_LAT_SEQ, LAT_TILES + 1),
        in_specs=[rows(1), rows(2), tab, tab, tab, cache, cache],
        out_specs=[out, out],
        out_shape=[shp, shp],
        compiler_params=_cparams("parallel", "parallel"),
        name="da_kv",
    )(z_a, z_a, *tabs, cache_k, cache_v)


def _da_attn_body(lam_ref, ng_ref, q_ref, *rest, rope, lam_init):
    if rope:
        cos_ref, slo_ref, shi_ref, k_ref, v_ref, g_ref, o_ref = rest
        q = _rope(q_ref[...], cos_ref[...], slo_ref[...], shi_ref[...])
        o_ref[...] = _da_attn_tile(lam_ref, ng_ref, q, k_ref[0], v_ref[0], g_ref[...], lam_init, True)
    else:
        k_ref, v_ref, g_ref, o_ref = rest
        for t in range(q_ref.shape[0] // CTX_LEN):
            r = slice(t * CTX_LEN, (t + 1) * CTX_LEN)
            o_ref[r, :] = _da_attn_tile(lam_ref, ng_ref, q_ref[r, :], k_ref[r, :].astype(BF16),
                                        v_ref[r, :].astype(BF16), g_ref[r, :], lam_init, False)


def _da_attn_tile(lam_ref, ng_ref, q, k, v, g, lam_init, scale_after):
    q = q * (DA_QK ** -0.5 * LOG2E)
    lv = lam_ref[...]
    lam = (jnp.exp(jnp.sum(lv[0:1] * lv[1:2], axis=-1, keepdims=True))
           - jnp.exp(jnp.sum(lv[2:3] * lv[3:4], axis=-1, keepdims=True)) + lam_init)
    lane = lax.broadcasted_iota(jnp.int32, (1, BRANCH), 1)
    acc = jnp.zeros(q.shape, F32)
    for h in range(DA_HEADS):
        q1 = jnp.where(lane // DA_QK == 2 * h, q, 0.0).astype(BF16)
        q2 = jnp.where(lane // DA_QK == 2 * h + 1, q, 0.0).astype(BF16)
        e1, l1 = _exp2_rows(_dot_nt(q1, k))
        e2, l2 = _exp2_rows(_dot_nt(q2, k))
        if scale_after:
            a = (e1 - (lam * l1 / l2) * e2).astype(BF16)
            pv = _dot(a, v) * (1.0 / l1)
        else:
            pv = _dot((e1 * (1.0 / l1) - lam * (e2 * (1.0 / l2))).astype(BF16), v)
        acc = jnp.where(lane // (2 * DA_QK) == h, pv, acc)
    sq = acc * acc
    ms = jnp.zeros(q.shape, F32)
    for h in range(DA_HEADS):
        hm = lane // (2 * DA_QK) == h
        ms = jnp.where(hm, jnp.sum(jnp.where(hm, sq, 0.0), axis=-1, keepdims=True), ms)
    o = acc * lax.rsqrt(ms * (1.0 / (2 * DA_QK)) + EPS) * (ng_ref[...] * (1.0 - lam_init))
    return o * _silu(g)


def _da_attention(z_a, lam_vec, norm_g, lam_init, tabs, kv_lat):
    ng = jnp.tile(norm_g.reshape(1, 2 * DA_QK), (1, DA_HEADS))
    small = [pl.BlockSpec((4, DA_QK), lambda *_: (0, 0)), pl.BlockSpec((1, BRANCH), lambda *_: (0, 0))]

    rows = CTX_SEQ_PER_STEP * CTX_LEN

    def col(c):
        return pl.BlockSpec((rows, BRANCH), lambda i: (i, c))
    ctx = pl.pallas_call(
        functools.partial(_da_attn_body, rope=False, lam_init=lam_init),
        grid=(N_CTX // rows,),
        in_specs=small + [col(0), col(1), col(2), col(3)],
        out_specs=pl.BlockSpec((rows, BRANCH), lambda i: (i, 0)),
        out_shape=jax.ShapeDtypeStruct((N_CTX, BRANCH), F32),
        compiler_params=_cparams("parallel"),
        name="da_attn_ctx",
    )(lam_vec, ng, z_a, z_a, z_a, z_a)

    def lcol(c):
        return pl.BlockSpec((LAT_Q_TILE, BRANCH), lambda b, j: (N_CTX // LAT_Q_TILE + b * LAT_Q_TILES + j, c))
    tab = pl.BlockSpec((LAT_Q_TILE, BRANCH), lambda b, j: (j, 0))
    kvs = pl.BlockSpec((1, LAT_LEN + PAST_LEN, BRANCH), lambda b, j: (b, 0, 0))
    lat = pl.pallas_call(
        functools.partial(_da_attn_body, rope=True, lam_init=lam_init),
        grid=(N_LAT_SEQ, LAT_Q_TILES),
        in_specs=small + [lcol(0), tab, tab, tab, kvs, kvs, lcol(3)],
        out_specs=pl.BlockSpec((LAT_Q_TILE, BRANCH), lambda b, j: (b * LAT_Q_TILES + j, 0)),
        out_shape=jax.ShapeDtypeStruct((N_LAT, BRANCH), F32),
        compiler_params=_cparams("parallel", "parallel"),
        name="da_attn_lat",
    )(lam_vec, ng, z_a, *tabs, kv_lat[0], kv_lat[1], z_a)
    return ctx, lat


MLA_HEAD_PAD = 128
MLA_QW = MLA_HEADS * MLA_HEAD_PAD


def _mla_prep_body(cq_ref, ckv_ref, kr_ref, ck_t, sk_lo, sk_hi, qn_ref, kvn_ref, wq_ref, q_out, ckv_out, kr_out):
    cq = cq_ref[...]
    ms = jnp.sum(cq * cq, axis=-1, keepdims=True) * (1.0 / MLA_Q_RANK)
    qn = (cq * lax.rsqrt(ms + EPS) * qn_ref[...]).astype(BF16)
    heads = lambda t: jnp.concatenate([t[...]] * MLA_HEADS, axis=-1)
    q = _rope(_dot(qn, wq_ref[...]), heads(ck_t), heads(sk_lo), heads(sk_hi))
    q_out[...] = (q * ((MLA_NOPE + MLA_ROPE) ** -0.5 * LOG2E)).astype(BF16)
    ckv = ckv_ref[...]
    ckv_out[...] = ckv * lax.rsqrt(jnp.mean(ckv * ckv, axis=-1, keepdims=True) + EPS) * kvn_ref[...]
    kr_out[...] = _rope(kr_ref[...], ck_t[...], sk_lo[...], sk_hi[...])


def _mla_prep(z_d, tabs, q_norm_pad, kv_norm, wq):
    ctx_tiles, lat_tiles = N_CTX // BIG_TILE, LAT_LEN // BIG_TILE

    def tab(w):
        return pl.BlockSpec((BIG_TILE, w), lambda i: (jnp.where(i < ctx_tiles, lat_tiles, (i - ctx_tiles) % lat_tiles), 0))

    def col(w, c):
        return pl.BlockSpec((BIG_TILE, w), lambda i: (i, c))

    def const(shape):
        return pl.BlockSpec(shape, lambda i: (0, 0))
    return pl.pallas_call(
        _mla_prep_body,
        grid=(N_TOK // BIG_TILE,),
        in_specs=[col(256, 0), col(128, 2), col(128, 3),
                  tab(128), tab(128), tab(128),
                  const((1, 256)), const((1, 128)), const((256, MLA_QW))],
        out_specs=[col(MLA_QW, 0), col(128, 0), col(128, 0)],
        out_shape=[jax.ShapeDtypeStruct((N_TOK, MLA_QW), BF16),
                   jax.ShapeDtypeStruct((N_TOK, 128), F32),
                   jax.ShapeDtypeStruct((N_TOK, 128), F32)],
        compiler_params=_cparams("parallel"),
        name="mla_prep",
    )(z_d, z_d, z_d, *tabs, q_norm_pad, kv_norm, wq)


def _mla_kv_math(ckv, kr, wk_ref, wv_ref, k_out, v_out):
    c = ckv.astype(BF16)
    k_out[...] = (_dot(c, wk_ref[...]) + jnp.concatenate([kr] * MLA_HEADS, axis=-1)).astype(BF16).reshape(k_out.shape)
    v_out[...] = _dot(c, wv_ref[...]).astype(BF16).reshape(v_out.shape)


def _mla_kv_ctx_body(ckv_ref, kr_ref, wk_ref, wv_ref, k_out, v_out):
    _mla_kv_math(ckv_ref[...], kr_ref[...], wk_ref, wv_ref, k_out, v_out)


def _mla_kv_lat_body(ckv_ref, kr_ref, cckv_ref, ckr_ref, wk_ref, wv_ref, k_out, v_out):
    j = pl.program_id(1)

    @pl.when(j < LAT_TILES)
    def _():
        _mla_kv_math(ckv_ref[...], kr_ref[...], wk_ref, wv_ref, k_out, v_out)

    @pl.when(j == LAT_TILES)
    def _():
        _mla_kv_math(cckv_ref[0], ckr_ref[0], wk_ref, wv_ref, k_out, v_out)


def _mla_kv(ckv, kr, cache_ckv, cache_kr, wk, wv):
    weights = [pl.BlockSpec((128, MLA_QW), lambda *_: (0, 0)), pl.BlockSpec((128, BRANCH), lambda *_: (0, 0))]
    k_ctx, v_ctx = pl.pallas_call(
        _mla_kv_ctx_body,
        grid=(N_CTX // BIG_TILE,),
        in_specs=[pl.BlockSpec((BIG_TILE, 128), lambda i: (i, 0)), pl.BlockSpec((BIG_TILE, 128), lambda i: (i, 0))] + weights,
        out_specs=[pl.BlockSpec((BIG_TILE, MLA_QW), lambda i: (i, 0)),
                   pl.BlockSpec((BIG_TILE, BRANCH), lambda i: (i, 0))],
        out_shape=[jax.ShapeDtypeStruct((N_CTX, MLA_QW), BF16), jax.ShapeDtypeStruct((N_CTX, BRANCH), BF16)],
        compiler_params=_cparams("parallel"),
        name="mla_kv_ctx",
    )(ckv, kr, wk, wv)
    rows = pl.BlockSpec((ROW_TILE, 128), lambda b, j: (CTX_TILES + b * LAT_TILES + jnp.minimum(j, LAT_TILES - 1), 0))
    cache = pl.BlockSpec((1, PAST_LEN, 128), lambda b, j: (b, 0, 0))
    lk = LAT_LEN + PAST_LEN
    k_lat, v_lat = pl.pallas_call(
        _mla_kv_lat_body,
        grid=(N_LAT_SEQ, LAT_TILES + 1),
        in_specs=[rows, rows, cache, cache] + weights,
        out_specs=[pl.BlockSpec((1, ROW_TILE, MLA_QW), lambda b, j: (b, j, 0)),
                   pl.BlockSpec((1, ROW_TILE, BRANCH), lambda b, j: (b, j, 0))],
        out_shape=[jax.ShapeDtypeStruct((N_LAT_SEQ, lk, MLA_QW), BF16), jax.ShapeDtypeStruct((N_LAT_SEQ, lk, BRANCH), BF16)],
        compiler_params=_cparams("parallel", "parallel"),
        name="mla_kv_lat",
    )(ckv, kr, cache_ckv, cache_kr, wk, wv)
    return k_ctx, v_ctx, k_lat, v_lat


def _mla_attn_body(q_ref, k_ref, v_ref, g_ref, o_ref, *, ctx):
    if ctx:
        for t in range(q_ref.shape[0] // CTX_LEN):
            r = slice(t * CTX_LEN, (t + 1) * CTX_LEN)
            o_ref[r, :] = _mla_attn_tile(q_ref[r, :], k_ref[r, :], v_ref[r, :], g_ref[r, :])
    else:
        o_ref[...] = _mla_attn_tile(q_ref[...], k_ref[0], v_ref[0], g_ref[...])


def _mla_attn_tile(q, k, v, g):
    lane = lax.broadcasted_iota(jnp.int32, (1, BRANCH), 1)
    acc = jnp.zeros((q.shape[0], BRANCH), F32)
    for h in range(MLA_HEADS):
        sl = slice(h * MLA_HEAD_PAD, (h + 1) * MLA_HEAD_PAD)
        e, l = _exp2_rows(_dot_nt(q[:, sl], k[:, sl]))
        acc = jnp.where(lane // 64 == h, _dot(e.astype(BF16), v) * (1.0 / l), acc)
    return acc * _silu(g)


def _mla_attention(z_d, q, k_ctx, v_ctx, k_lat, v_lat):
    rows = CTX_SEQ_PER_STEP * CTX_LEN
    ctx = pl.pallas_call(
        functools.partial(_mla_attn_body, ctx=True),
        grid=(N_CTX // rows,),
        in_specs=[pl.BlockSpec((rows, MLA_QW), lambda i: (i, 0)),
                  pl.BlockSpec((rows, MLA_QW), lambda i: (i, 0)),
                  pl.BlockSpec((rows, BRANCH), lambda i: (i, 0)),
                  pl.BlockSpec((rows, BRANCH), lambda i: (i, 2))],
        out_specs=pl.BlockSpec((rows, BRANCH), lambda i: (i, 0)),
        out_shape=jax.ShapeDtypeStruct((N_CTX, BRANCH), F32),
        compiler_params=_cparams("parallel"),
        name="mla_attn_ctx",
    )(q, k_ctx, v_ctx, z_d)
    lk = LAT_LEN + PAST_LEN
    lat = pl.pallas_call(
        functools.partial(_mla_attn_body, ctx=False),
        grid=(N_LAT_SEQ, LAT_Q_TILES),
        in_specs=[pl.BlockSpec((LAT_Q_TILE, MLA_QW), lambda b, j: (N_CTX // LAT_Q_TILE + b * LAT_Q_TILES + j, 0)),
                  pl.BlockSpec((1, lk, MLA_QW), lambda b, j: (b, 0, 0)),
                  pl.BlockSpec((1, lk, BRANCH), lambda b, j: (b, 0, 0)),
                  pl.BlockSpec((LAT_Q_TILE, BRANCH), lambda b, j: (N_CTX // LAT_Q_TILE + b * LAT_Q_TILES + j, 2))],
        out_specs=pl.BlockSpec((LAT_Q_TILE, BRANCH), lambda b, j: (b * LAT_Q_TILES + j, 0)),
        out_shape=jax.ShapeDtypeStruct((N_LAT, BRANCH), F32),
        compiler_params=_cparams("parallel", "parallel"),
        name="mla_attn_lat",
    )(q, k_lat, v_lat, z_d)
    return ctx, lat


S5_TAP = S5_CHUNK * S5_CH
S5_NCHUNK = N_TOK // S5_CHUNK
S5_CTX_CH = N_CTX // S5_CHUNK
S5_CTX_SEQ_CH = CTX_LEN // S5_CHUNK
S5_LAT_SEQ_CH = LAT_LEN // S5_CHUNK
S5_SCAN_STEPS = S5_LAT_SEQ_CH.bit_length() - 1
S5_TABLE_GROUPS = 8


def _s5_body(x_ref, mt_ref, bst_ref, cot_ref, a_ref, h0_ref, y_ref, fin_ref):
    x = x_ref[...].reshape(S5_TAP, S5_NCHUNK)
    y = _dot(mt_ref[0, 0], x)
    s = _dot(bst_ref[0, 0], x)
    lane = lax.broadcasted_iota(jnp.int32, (1, S5_NCHUNK), 1)
    is_lat = lane >= S5_CTX_CH
    pos_f = jnp.where(is_lat, (lane - S5_CTX_CH) & (S5_LAT_SEQ_CH - 1), lane & (S5_CTX_SEQ_CH - 1))
    pos_b = jnp.where(is_lat, S5_LAT_SEQ_CH - 1, S5_CTX_SEQ_CH - 1) - pos_f
    hin = []
    for d in range(2):
        n = S5_STATE
        sre, sim = s[2 * d * n:(2 * d + 1) * n], s[(2 * d + 1) * n:(2 * d + 2) * n]
        are = jnp.concatenate([a_ref[0, 0, 2 * d]] * (S5_NCHUNK // 128), axis=-1)
        aim = jnp.concatenate([a_ref[0, 0, 2 * d + 1]] * (S5_NCHUNK // 128), axis=-1)
        pos = pos_f if d == 0 else pos_b
        h0r, h0i = jnp.zeros_like(sre), jnp.zeros_like(sre)
        for b in range(N_LAT_SEQ):
            first = S5_CTX_CH + b * S5_LAT_SEQ_CH + (0 if d == 0 else S5_LAT_SEQ_CH - 1)
            h0r = jnp.where(lane == first, h0_ref[0, 2 * d][:, b:b + 1], h0r)
            h0i = jnp.where(lane == first, h0_ref[0, 2 * d + 1][:, b:b + 1], h0i)
        xr = sre + are * h0r - aim * h0i
        xi = sim + are * h0i + aim * h0r
        pr, pi = are, aim
        for j in range(S5_SCAN_STEPS):
            sh = 1 << j
            shift = sh if d == 0 else S5_NCHUNK - sh
            rr, ri = pltpu.roll(xr, shift, 1), pltpu.roll(xi, shift, 1)
            ok = pos >= sh
            xr, xi = (xr + jnp.where(ok, pr * rr - pi * ri, 0.0), xi + jnp.where(ok, pr * ri + pi * rr, 0.0))
            pr, pi = pr * pr - pi * pi, 2.0 * pr * pi
        last = lax.broadcasted_iota(jnp.int32, (1, 128), 1) * S5_CTX_SEQ_CH + (S5_CTX_SEQ_CH - 1 if d == 0 else 0)
        pick = jnp.where(lax.broadcasted_iota(jnp.int32, (S5_CTX_CH, 1), 0) == last, 1.0, 0.0)
        fin_ref[0, 2 * d] = _dot_sel(xr[:, :S5_CTX_CH], pick)
        fin_ref[0, 2 * d + 1] = _dot_sel(xi[:, :S5_CTX_CH], pick)
        one = 1 if d == 0 else S5_NCHUNK - 1
        hin.append(jnp.where(pos >= 1, pltpu.roll(xr, one, 1), h0r))
        hin.append(jnp.where(pos >= 1, pltpu.roll(xi, one, 1), h0i))
    y = y + _dot(cot_ref[0, 0], jnp.concatenate(hin, axis=0).astype(BF16))
    y_ref[...] = y.reshape(S5_CHUNK, S5_CH, S5_NCHUNK)


def _s5_scan(x_all, mt, bst, cot, a16, h0, l):
    g = S5_GROUPS
    sq = pl.BlockSpec((1, 1, S5_TAP, S5_TAP), lambda i: (l, i, 0, 0))
    st = pl.BlockSpec((1, 4, S5_STATE, 128), lambda i: (i, 0, 0, 0))
    return pl.pallas_call(
        _s5_body,
        grid=(g,),
        in_specs=[pl.BlockSpec((S5_CHUNK, S5_CH, S5_NCHUNK), lambda i: (0, i, 0)), sq, sq, sq,
                  pl.BlockSpec((1, 1, 4, S5_STATE, 128), lambda i: (l, i, 0, 0, 0)), st],
        out_specs=[pl.BlockSpec((S5_CHUNK, S5_CH, S5_NCHUNK), lambda i: (0, i, 0)),
                   pl.BlockSpec((1, 4, S5_STATE, 128), lambda i: (i, 0, 0, 0))],
        out_shape=[jax.ShapeDtypeStruct((S5_CHUNK, BRANCH, S5_NCHUNK), F32),
                   jax.ShapeDtypeStruct((g, 4, S5_STATE, 128), F32)],
        compiler_params=_cparams("parallel"),
        name="s5_scan",
    )(x_all, mt, bst, cot, a16, h0)


def _s5_gated(y_ssm, u, g, d_skip, w_glu):
    y = u * d_skip + y_ssm
    ge = 0.5 * y * (1.0 + jnp.tanh(0.7978845608028654 * (y + 0.044715 * (y * y * y))))
    gl = _dot(ge.astype(BF16), w_glu)
    return gl[:, :BRANCH] * (1.0 / (1.0 + jnp.exp(-gl[:, BRANCH:]))) * _silu(g)


HG_CHUNK = ROW_TILE
HG_W = 2 * HG_HEADS * HG_DK
HG_HEAD_W = 2 * HG_DK
HG_LAT_CHUNKS = LAT_LEN // HG_CHUNK
HG_CHUNKS = N_TOK // HG_CHUNK


def _hg_gates(z, lb):
    e = jnp.exp(-jnp.abs(z))
    r = 1.0 / (1.0 + e)
    sig_pos = jnp.where(z >= 0, r, e * r)
    sig_neg = jnp.where(z >= 0, e * r, r)
    return lb + (1.0 - lb) * sig_pos, (1.0 - lb) * sig_neg


def _bcast_row(x, period, r):
    n, w = x.shape
    if period >= 8:
        x3 = x.reshape(n // period, period, w)
        return jnp.broadcast_to(x3[:, r:r + 1, :], x3.shape).reshape(n, w)
    x3 = x.reshape(n // 8, 8, w)
    sub = lax.broadcasted_iota(jnp.int32, (1, 8, 1), 1)
    out = None
    for j in range(8 // period):
        b = jnp.broadcast_to(x3[:, j * period + r:j * period + r + 1, :], x3.shape)
        out = b if out is None else jnp.where(sub >= j * period, b, out)
    return out.reshape(n, w)


def _hg_scans(f, isb):
    n = f.shape[0]
    row = lax.broadcasted_iota(jnp.int32, (n, 1), 0)
    p, r = f, jnp.ones_like(f)
    levels = []
    h, sh = 1, 0
    while h < n:
        levels.append((h, sh, p, r))
        up = (row >> sh) & 1
        tot_p = jnp.where(isb == 1, _bcast_row(p, 2 * h, h), _bcast_row(p, 2 * h, h - 1))
        tot_r = jnp.where(isb == 1, _bcast_row(p, 2 * h, 0), _bcast_row(p, 2 * h, 2 * h - 1))
        p = p * jnp.where(up != isb, tot_p, 1.0)
        r = r * jnp.where(up == isb, tot_r, 1.0)
        h, sh = 2 * h, sh + 1
    return levels, p, r


def _hg_state_body(zf_ref, zb_ref, vf_ref, vb_ref, lb_ref, s0_ref, sf_out, sb_out, s_scr):
    i = pl.program_id(0)

    @pl.when(i % HG_LAT_CHUNKS == 0)
    def _():
        s_scr[...] = s0_ref[0]

    sf_out[0] = s_scr[:, 0:HG_DK, :]
    sb_out[0] = s_scr[:, HG_DK:, :]
    lane5 = lax.broadcasted_iota(jnp.int32, (1, HG_W), 1)
    isb = (lane5 >> 6) & 1
    z = jnp.where(isb == 1, zb_ref[...], zf_ref[...])
    f, k = _hg_gates(z, lb_ref[...])
    r, ptot = _hg_chunk_decay(f, isb)
    kt = k * r
    lane = lax.broadcasted_iota(jnp.int32, (1, BRANCH), 1)
    vf = vf_ref[...]
    vb = vb_ref[...]
    for hd in range(HG_HEADS):
        sl = slice(hd * HG_HEAD_W, (hd + 1) * HG_HEAD_W)
        kth = kt[:, sl].T.astype(BF16)
        hm = (lane >> 6) == hd
        d_f = _dot(kth, jnp.where(hm, vf, 0.0).astype(BF16))
        d_b = _dot(kth, jnp.where(hm, vb, 0.0).astype(BF16))
        ds = jnp.concatenate([d_f[:HG_DK], d_b[HG_DK:]], axis=0)
        pcol = jnp.broadcast_to(ptot[:, sl], (HG_HEAD_W, HG_HEAD_W)).T[:, 0:1]
        s_scr[hd] = s_scr[hd] * pcol + ds


def _hg_chunk_decay(f, isb):
    n = f.shape[0]
    row = lax.broadcasted_iota(jnp.int32, (n, 1), 0)
    dist = jnp.where(isb == 1, row, n - 1 - row)
    x = f
    sh = 1
    while sh < n:
        src = jnp.where(isb == 1, pltpu.roll(x, sh, 0), pltpu.roll(x, n - sh, 0))
        x = x * jnp.where(dist >= sh, src, 1.0)
        sh *= 2
    total = jnp.where(isb == 1, x[n - 1:n], x[0:1])
    nxt = jnp.where(isb == 1, pltpu.roll(x, 1, 0), pltpu.roll(x, n - 1, 0))
    return jnp.where(dist >= 1, nxt, 1.0), total


HG_LAT_STEPS = N_LAT_SEQ * HG_LAT_CHUNKS


def _hg_lat_rev(i):
    return (i // HG_LAT_CHUNKS) * HG_LAT_CHUNKS + (HG_LAT_CHUNKS - 1 - i % HG_LAT_CHUNKS)


def _hg_states(z_c, lb, s0):
    first = N_CTX_SEQ
    zz_f = pl.BlockSpec((HG_CHUNK, HG_W), lambda i: (first + i, 1))
    zz_b = pl.BlockSpec((HG_CHUNK, HG_W), lambda i: (first + _hg_lat_rev(i), 1))
    v_f = pl.BlockSpec((HG_CHUNK, BRANCH), lambda i: (first + i, 4))
    v_b = pl.BlockSpec((HG_CHUNK, BRANCH), lambda i: (first + _hg_lat_rev(i), 4))
    st = (HG_HEADS, HG_HEAD_W, BRANCH)
    half = (HG_HEADS, HG_DK, BRANCH)
    return pl.pallas_call(
        _hg_state_body,
        grid=(HG_LAT_STEPS,),
        in_specs=[zz_f, zz_b, v_f, v_b,
                  pl.BlockSpec((1, HG_W), lambda i: (0, 0)),
                  pl.BlockSpec((1,) + st, lambda i: (i // HG_LAT_CHUNKS, 0, 0, 0))],
        out_specs=[pl.BlockSpec((1,) + half, lambda i: (i, 0, 0, 0)),
                   pl.BlockSpec((1,) + half, lambda i: (_hg_lat_rev(i), 0, 0, 0))],
        out_shape=[jax.ShapeDtypeStruct((HG_LAT_STEPS,) + half, F32),
                   jax.ShapeDtypeStruct((HG_LAT_STEPS,) + half, F32)],
        scratch_shapes=[pltpu.VMEM(st, F32)],
        compiler_params=_cparams("arbitrary"),
        name="hg_states",
    )(z_c, z_c, z_c, z_c, lb, s0)


def _hg_main_body(qq_ref, zz_ref, v_ref, g_ref, sf_ref, sb_ref, lb_ref, ng_ref, o_ref, fin_ref):
    n = HG_CHUNK
    i = pl.program_id(0)
    qq = qq_ref[...]
    lane5 = lax.broadcasted_iota(jnp.int32, (1, HG_W), 1)
    isb = (lane5 >> 6) & 1
    f, k = _hg_gates(zz_ref[...], lb_ref[...])
    levels, pfull, rfull = _hg_scans(f, isb)
    row = lax.broadcasted_iota(jnp.int32, (n, 1), 0)
    col = lax.broadcasted_iota(jnp.int32, (1, n), 1)
    ops = [(qq.astype(BF16), k.astype(BF16), row == col)]
    for h, sh, p, r in levels:
        up = (row >> sh) & 1
        qt = jnp.where(up != isb, qq * p, 0.0).astype(BF16)
        kt = jnp.where(up == isb, k * r, 0.0).astype(BF16)
        ops.append((qt, kt, (row >> (sh + 1)) == (col >> (sh + 1))))
    qc = (qq * pfull).astype(BF16)
    v = v_ref[...]
    vb = v.astype(BF16)
    lane = lax.broadcasted_iota(jnp.int32, (1, BRANCH), 1)
    latent = i >= N_CTX_SEQ
    acc = jnp.zeros((n, BRANCH), F32)
    for hd in range(HG_HEADS):
        sl = slice(hd * HG_HEAD_W, (hd + 1) * HG_HEAD_W)
        a = jnp.zeros((n, n), F32)
        for qt, kt, mask in ops:
            a = a + jnp.where(mask, _dot_nt(qt[:, sl], kt[:, sl]), 0.0)
        s_in = jnp.concatenate([sf_ref[0, hd], sb_ref[0, hd]], axis=0)
        s_in = jnp.where(latent, s_in, 0.0).astype(BF16)
        o_h = _dot(a.astype(BF16), vb) + _dot(qc[:, sl], s_in)
        acc = jnp.where((lane >> 6) == hd, o_h, acc)
    sq = acc * acc
    ms = jnp.zeros((n, BRANCH), F32)
    for hd in range(HG_HEADS):
        hm = (lane >> 6) == hd
        ms = jnp.where(hm, jnp.sum(jnp.where(hm, sq, 0.0), axis=-1, keepdims=True), ms)
    o_ref[...] = acc * lax.rsqrt(ms * (1.0 / HG_DK) + EPS) * ng_ref[...] * _silu(g_ref[...])

    @pl.when(i < N_CTX_SEQ)
    def _():
        kt_full = k * rfull
        for hd in range(HG_HEADS):
            kth = kt_full[:, hd * HG_HEAD_W:(hd + 1) * HG_HEAD_W].T.astype(BF16)
            ds = _dot(kth, jnp.where((lane >> 6) == hd, v, 0.0).astype(BF16))
            fin_ref[0, hd] = ds[:, hd * HG_DK:(hd + 1) * HG_DK]


def _hg_main(z_c, s_f, s_b, lb, norm_g):
    half = (1, HG_HEADS, HG_DK, BRANCH)
    lat = lambda i: (jnp.maximum(i - N_CTX_SEQ, 0), 0, 0, 0)
    fin = (HG_HEADS, HG_HEAD_W, HG_DK)
    return pl.pallas_call(
        _hg_main_body,
        grid=(HG_CHUNKS,),
        in_specs=[pl.BlockSpec((HG_CHUNK, HG_W), lambda i: (i, 0)),
                  pl.BlockSpec((HG_CHUNK, HG_W), lambda i: (i, 1)),
                  pl.BlockSpec((HG_CHUNK, BRANCH), lambda i: (i, 4)),
                  pl.BlockSpec((HG_CHUNK, BRANCH), lambda i: (i, 5)),
                  pl.BlockSpec(half, lat),
                  pl.BlockSpec(half, lat),
                  pl.BlockSpec((1, HG_W), lambda i: (0, 0)),
                  pl.BlockSpec((1, BRANCH), lambda i: (0, 0))],
        out_specs=[pl.BlockSpec((HG_CHUNK, BRANCH), lambda i: (i, 0)),
                   pl.BlockSpec((1,) + fin, lambda i: (jnp.minimum(i, N_CTX_SEQ - 1), 0, 0, 0))],
        out_shape=[jax.ShapeDtypeStruct((N_TOK, BRANCH), F32),
                   jax.ShapeDtypeStruct((N_CTX_SEQ,) + fin, F32)],
        compiler_params=_cparams("arbitrary"),
        name="hg_main",
    )(z_c, z_c, z_c, z_c, s_f, s_b, lb, norm_g)


def _take_cols(w, plan):
    idx = np.concatenate([p[0] for p in plan]).astype(np.int32)
    sign = np.concatenate([np.broadcast_to(p[1], p[0].shape) for p in plan]).astype(np.float32)
    return jnp.take(w, jnp.asarray(idx), axis=-1) * jnp.asarray(sign)


def _zeros(n):
    return (np.zeros(n, np.int64), 0.0)


_IN_OFF = {}
_off = 0
for _name, _n in (("da_q", 256), ("da_k", 256), ("da_v", 256), ("da_g", 256), ("s5_u", 256), ("s5_g", 256),
                  ("hg_q", 256), ("hg_ff", 256), ("hg_fb", 256), ("hg_i", 256), ("hg_g", 256),
                  ("mla_cq", MLA_Q_RANK), ("mla_ckv", MLA_KV_RANK), ("mla_kr", MLA_ROPE), ("mla_g", 256)):
    _IN_OFF[_name] = np.arange(_off, _off + _n)
    _off += _n


def _rope_tables():
    t = np.arange(LAT_LEN)
    pos = np.stack([t // GRID_W, t % GRID_W], axis=1).astype(np.float32)
    inv_freq = (np.float32(ROPE_BASE) ** (-np.arange(8, dtype=np.float32) / np.float32(8))).astype(np.float32)
    r = np.arange(MLA_ROPE)
    ang = (pos[:, r // 16] * inv_freq[r % 8][None, :]).astype(np.float64)
    cos32, sin32 = np.cos(ang).astype(np.float32), np.sin(ang).astype(np.float32)
    lo = (np.arange(MLA_ROPE) % 16 < 8)[None, :]
    sin_lo32, sin_hi32 = np.where(lo, -sin32, 0.0).astype(np.float32), np.where(lo, 0.0, sin32).astype(np.float32)
    da_tabs = tuple(np.tile(x, (1, 8)) for x in (cos32, sin_lo32, sin_hi32))

    def head(x, fill):
        h = np.concatenate([np.full((LAT_LEN, MLA_NOPE), fill, np.float32), x,
                            np.full((LAT_LEN, MLA_HEAD_PAD - MLA_NOPE - MLA_ROPE), fill, np.float32)], axis=1)
        return np.concatenate([h, np.full((BIG_TILE, MLA_HEAD_PAD), fill, np.float32)], axis=0)
    k_tabs = (head(cos32, 1.0), head(sin_lo32, 0.0), head(sin_hi32, 0.0))
    return tuple(jnp.asarray(x) for x in da_tabs), tuple(jnp.asarray(x) for x in k_tabs)


def _mla_weights(w_uq, w_ukv, q_norm):
    hd = MLA_NOPE + MLA_ROPE
    pad_tail = _zeros(MLA_HEAD_PAD - hd)
    q_plan, k_plan, v_plan = [], [], []
    for h in range(MLA_HEADS):
        nope, rope = np.arange(h * hd, h * hd + MLA_NOPE), np.arange(h * hd + MLA_NOPE, (h + 1) * hd)
        q_plan += [(nope, 1.0), (rope, 1.0), pad_tail]
        k_plan += [(np.arange(h * 2 * MLA_NOPE, h * 2 * MLA_NOPE + MLA_NOPE), 1.0), _zeros(MLA_HEAD_PAD - MLA_NOPE)]
        v_plan += [(np.arange(h * 2 * MLA_NOPE + MLA_NOPE, (h + 1) * 2 * MLA_NOPE), 1.0)]
    pad_rows = lambda x: jnp.pad(x, ((0, 256 - MLA_Q_RANK), (0, 0))).astype(BF16)
    qn = jnp.pad(q_norm, (0, 256 - MLA_Q_RANK)).reshape(1, 256)
    return (pad_rows(_take_cols(w_uq, q_plan)), _take_cols(w_ukv, k_plan).astype(BF16),
            _take_cols(w_ukv, v_plan).astype(BF16), qn)


def _split_bf16(a):
    hi = a.astype(BF16)
    return hi, (a - hi.astype(F32)).astype(BF16)


def _dot_sel(a, sel):
    hi, lo = _split_bf16(a)
    sel = sel.astype(BF16)
    return _dot(hi, sel) + _dot(lo, sel)


def _dot_x3(a, b):
    a_hi, a_lo = _split_bf16(a)
    b_hi, b_lo = _split_bf16(b)
    return _dot(a_hi, b_hi) + _dot(a_hi, b_lo) + _dot(a_lo, b_hi)


def _s5_table_body(xy_ref, bb_ref, c_ref, ct_ref, mt_ref, bst_ref, cot_ref, a_ref):
    n, t, ch = S5_STATE, S5_CHUNK, S5_CH
    wide = 2 * S5_TAP
    tau_i = lax.broadcasted_iota(jnp.int32, (1, 128), 1)
    tau = tau_i.astype(F32)
    sel_row = lax.broadcasted_iota(jnp.int32, (128, 1), 0)

    def lag(width):
        return lax.broadcasted_iota(jnp.int32, (1, width), 1) >> 4

    def onehot(cond):
        return jnp.where(cond, 1.0, 0.0).astype(F32)
    j = lag(wide)
    e_z = (onehot((j <= t - 1) & (sel_row == t - 1 - j)), onehot((j >= t - 1) & (j <= 2 * t - 2) & (sel_row == j - (t - 1))))
    jc = lag(S5_TAP)
    e_c = (onehot(sel_row == jc + 1), onehot(sel_row == t - jc))
    ch_row = lax.broadcasted_iota(jnp.int32, (ch, 1), 0)
    tile_w = onehot((lax.broadcasted_iota(jnp.int32, (1, wide), 1) & (ch - 1)) == ch_row)
    tile_n = onehot((lax.broadcasted_iota(jnp.int32, (1, S5_TAP), 1) & (ch - 1)) == ch_row)

    for gi in range(S5_TABLE_GROUPS):
        xy = xy_ref[0, gi]
        z, cot_rows, klong = [], [], None
        for d in range(2):
            x, y = xy[:, 2 * d:2 * d + 1], xy[:, 2 * d + 1:2 * d + 2]
            mag = jnp.exp(jnp.where(tau_i <= t, tau, 0.0) * x)
            ang = jnp.where(tau_i <= t, tau, 0.0) * y
            p_re = jnp.where(tau_i <= t, mag * jnp.cos(ang), 0.0)
            p_im = jnp.where(tau_i <= t, mag * jnp.sin(ang), 0.0)
            a_ref[0, gi, 2 * d] = jnp.broadcast_to(p_re[:, t:t + 1], (n, 128))
            a_ref[0, gi, 2 * d + 1] = jnp.broadcast_to(p_im[:, t:t + 1], (n, 128))
            pz_re, pz_im = _dot_sel(p_re, e_z[d]), _dot_sel(p_im, e_z[d])
            b_re, b_im = _dot_sel(bb_ref[0, gi, 2 * d], tile_w), _dot_sel(bb_ref[0, gi, 2 * d + 1], tile_w)
            z_re, z_im = pz_re * b_re - pz_im * b_im, pz_re * b_im + pz_im * b_re
            z += [z_re, z_im]
            part = _dot_x3(c_ref[0, gi, 2 * d], z_re) - _dot_x3(c_ref[0, gi, 2 * d + 1], z_im)
            klong = part if klong is None else klong + part
            pc_re, pc_im = _dot_sel(p_re, e_c[d]), _dot_sel(p_im, e_c[d])
            c_re, c_im = _dot_sel(ct_ref[0, gi, 2 * d], tile_n), _dot_sel(ct_ref[0, gi, 2 * d + 1], tile_n)
            cot_rows += [c_re * pc_re - c_im * pc_im, -(c_re * pc_im + c_im * pc_re)]
        for tt in range(t):
            off = (t - 1 - tt) * ch
            win = klong if off == 0 else pltpu.roll(klong, wide - off, 1)
            mt_ref[0, gi, tt * ch:(tt + 1) * ch, :] = win[:, :S5_TAP].astype(BF16)
        back = pltpu.roll(z[2], wide - (t - 1) * ch, 1), pltpu.roll(z[3], wide - (t - 1) * ch, 1)
        for k, rows in enumerate((z[0], z[1], back[0], back[1])):
            bst_ref[0, gi, k * n:(k + 1) * n, :] = rows[:, :S5_TAP].astype(BF16)
        cot_ref[0, gi] = jnp.concatenate(cot_rows, axis=0).T.astype(BF16)


def _s5_tables(a_re, a_im, log_dt, b_re, b_im, c_re, c_im):
    nl, g, n, ch = a_re.shape[0], S5_GROUPS, S5_STATE, S5_CH
    step = jnp.exp(log_dt)[..., None]
    mag = jnp.exp(a_re * step)
    ab_re, ab_im = mag * jnp.cos(a_im * step), mag * jnp.sin(a_im * step)
    den = a_re * a_re + a_im * a_im
    f_re = ((ab_re - 1.0) * a_re + ab_im * a_im) / den
    f_im = (ab_im * a_re - (ab_re - 1.0) * a_im) / den
    bb_re = f_re[..., None] * b_re - f_im[..., None] * b_im
    bb_im = f_re[..., None] * b_im + f_im[..., None] * b_re
    by_group = lambda x: jnp.moveaxis(x, 1, 2)
    pair = lambda re, im: jnp.stack([by_group(re), by_group(im)], axis=3).reshape((nl, g, 4) + re.shape[3:])
    xy = jnp.stack([by_group(a_re * step), by_group(a_im * step)], axis=3).reshape(nl, g, 4, n)
    xy = jnp.pad(jnp.swapaxes(xy, 2, 3), ((0, 0), (0, 0), (0, 0), (0, 4)))
    gs = S5_TABLE_GROUPS
    mat = pl.BlockSpec((1, gs, S5_TAP, S5_TAP), lambda l, i: (l, i, 0, 0))
    return pl.pallas_call(
        _s5_table_body,
        grid=(nl, g // gs),
        in_specs=[pl.BlockSpec((1, gs, n, 8), lambda l, i: (l, i, 0, 0)),
                  pl.BlockSpec((1, gs, 4, n, ch), lambda l, i: (l, i, 0, 0, 0)),
                  pl.BlockSpec((1, gs, 4, ch, n), lambda l, i: (l, i, 0, 0, 0)),
                  pl.BlockSpec((1, gs, 4, n, ch), lambda l, i: (l, i, 0, 0, 0))],
        out_specs=[mat, mat, mat, pl.BlockSpec((1, gs, 4, n, 128), lambda l, i: (l, i, 0, 0, 0))],
        out_shape=[jax.ShapeDtypeStruct((nl, g, S5_TAP, S5_TAP), BF16)] * 3
        + [jax.ShapeDtypeStruct((nl, g, 4, n, 128), F32)],
        compiler_params=_cparams("parallel", "parallel"),
        name="s5_tables",
    )(xy, pair(bb_re, bb_im), pair(c_re, c_im), pair(jnp.swapaxes(c_re, -1, -2), jnp.swapaxes(c_im, -1, -2)))


def _s5_chunk_lanes(u):
    return u.reshape(S5_NCHUNK, S5_CHUNK, BRANCH).transpose(1, 2, 0)


def _s5_token_rows(y):
    return y.transpose(2, 0, 1).reshape(N_TOK, BRANCH)


def kernel(x_prompt, x_sample, cache_diff_k, cache_diff_v, state_s5, state_hgrn, cache_mla_ckv, cache_mla_krope, c, c_ctx, w_mod, b_mod, w_in, w_out, da_lambda, da_norm, s5_a_re, s5_a_im, s5_log_dt, s5_b_re, s5_b_im, s5_c_re, s5_c_im, s5_d, s5_w_glu, hg_lb, hg_norm, mla_q_norm, mla_w_uq, mla_kv_norm, mla_w_ukv, final_norm):
    lb_w = jax.nn.softmax(hg_lb.astype(F32), axis=0)
    lb_all = jnp.cumsum(lb_w, axis=0) - lb_w[0:1]
    c_rows = jnp.concatenate([c_ctx[None], c, jnp.zeros((8 - 1 - N_LAT_SEQ, D_MODEL), F32)], axis=0)
    mods = _modulation(c_rows, w_mod, b_mod)
    da_tabs, mla_tabs = _rope_tables()
    xs = (x_prompt.reshape(N_CTX, D_MODEL), x_sample.reshape(N_LAT, D_MODEL))
    new_k, new_v, new_s5, new_hg, new_ckv, new_kr = [], [], [], [], [], []
    s5_tabs = _s5_tables(s5_a_re, s5_a_im, s5_log_dt, s5_b_re, s5_b_im, s5_c_re, s5_c_im)
    w_all = _arrange_w_in(w_in)
    for l in range(DEPTH):
        mod = mods[l, :3].reshape(3, 3, D_MODEL)
        z_a, z_b, z_c, z_d, k_new, v_new, u_bf = _in_proj(xs, mod, w_all, l)

        lam_init = 0.8 - 0.6 * math.exp(-0.3 * l)
        kv_lat = _da_latent_kv(z_a, da_tabs,
                               cache_diff_k[:, l].reshape(N_LAT_SEQ, PAST_LEN, BRANCH),
                               cache_diff_v[:, l].reshape(N_LAT_SEQ, PAST_LEN, BRANCH))
        a_out = _da_attention(z_a, da_lambda[l], da_norm[l], lam_init, da_tabs, kv_lat)
        new_k.append(k_new)
        new_v.append(v_new)

        h0 = state_s5[:, l].transpose(2, 1, 4, 3, 0).reshape(S5_GROUPS, 4, S5_STATE, N_LAT_SEQ)
        h0 = jnp.pad(h0, ((0, 0), (0, 0), (0, 0), (0, 128 - N_LAT_SEQ)))
        y_all, fin = _s5_scan(_s5_chunk_lanes(u_bf), *s5_tabs, h0, l)
        b_out = (_s5_token_rows(y_all), z_b, s5_d[l].reshape(1, BRANCH), s5_w_glu[l].astype(BF16))
        fin = fin[..., :N_CTX_SEQ].reshape(S5_GROUPS, 2, 2, S5_STATE, N_CTX_SEQ)
        new_s5.append(fin.transpose(4, 1, 0, 3, 2))

        lb = jnp.concatenate([lb_all[l, 0].reshape(HG_HEADS, HG_DK), lb_all[l, 1].reshape(HG_HEADS, HG_DK)],
                             axis=-1).reshape(1, HG_W)
        head_eye = jnp.eye(HG_HEADS, dtype=F32)
        s0 = state_hgrn[:, l].transpose(0, 2, 1, 3, 4).reshape(N_LAT_SEQ, HG_HEADS, HG_HEAD_W, 1, HG_DK)
        s0 = (s0 * head_eye[None, :, None, :, None]).reshape(N_LAT_SEQ, HG_HEADS, HG_HEAD_W, BRANCH)
        s_f, s_b = _hg_states(z_c, lb, s0)
        c_out, s_fin = _hg_main(z_c, s_f, s_b, lb, jnp.tile(hg_norm[l].reshape(1, HG_DK), (1, HG_HEADS)))
        new_hg.append(s_fin.reshape(N_CTX_SEQ, HG_HEADS, 2, HG_DK, HG_DK).transpose(0, 2, 1, 3, 4))

        wq, wk, wv, qn = _mla_weights(mla_w_uq[l], mla_w_ukv[l], mla_q_norm[l])
        q, ckv_n, kr = _mla_prep(z_d, mla_tabs, qn, mla_kv_norm[l].reshape(1, MLA_KV_RANK), wq)
        kr_cache = jnp.pad(cache_mla_krope[:, l], ((0, 0), (0, 0), (MLA_NOPE, 128 - MLA_NOPE - MLA_ROPE)))
        k_ctx, v_ctx, k_lat, v_lat = _mla_kv(ckv_n, kr, cache_mla_ckv[:, l], kr_cache, wk, wv)
        d_out = _mla_attention(z_d, q, k_ctx, v_ctx, k_lat, v_lat)
        new_ckv.append(ckv_n[:N_CTX].reshape(N_CTX_SEQ, CTX_LEN, MLA_KV_RANK))
        new_kr.append(kr[:N_CTX, MLA_NOPE:MLA_NOPE + MLA_ROPE].reshape(N_CTX_SEQ, CTX_LEN, MLA_ROPE))

        xs = _out_proj(a_out, b_out, c_out, d_out, xs, mod, w_out, l,
                       final_norm.reshape(1, D_MODEL), final=(l == DEPTH - 1))
        xs = tuple(xs) if l == DEPTH - 1 else (xs,)
    y_prompt = xs[0].reshape(N_CTX_SEQ, CTX_LEN, D_MODEL)
    y_sample = xs[1].reshape(N_LAT_SEQ, LAT_LEN, D_MODEL)
    st = lambda parts: jnp.stack(parts, axis=1)
    heads_last = lambda kv: kv.reshape(N_CTX_SEQ, DEPTH, DA_HEADS, 2 * DA_QK, CTX_LEN).transpose(0, 1, 4, 2, 3)
    return (y_prompt, y_sample, heads_last(st(new_k)), heads_last(st(new_v)), st(new_s5), st(new_hg), st(new_ckv), st(new_kr))
```

```python
import functools
import math

import numpy as np

import jax
import jax.numpy as jnp
from jax import lax
from jax.experimental import pallas as pl
from jax.experimental.pallas import tpu as pltpu

F32 = jnp.float32
BF16 = jnp.bfloat16

D_MODEL = 1024
DEPTH = 2
N_CTX_SEQ = 16
CTX_LEN = 256
N_LAT_SEQ = 2
LAT_LEN = 2048
PAST_LEN = 256
GRID_W = 64
N_CTX = N_CTX_SEQ * CTX_LEN
N_LAT = N_LAT_SEQ * LAT_LEN
N_TOK = N_CTX + N_LAT
BRANCH = 256
EPS = 1e-6
ROPE_BASE = 10000.0
ROW_TILE = 256
LAT_TILES = LAT_LEN // ROW_TILE
N_TILES = N_TOK // ROW_TILE
CTX_TILES = N_CTX // ROW_TILE
VMEM_LIMIT = 48 * 1024 * 1024
IN_PROJ_VMEM_LIMIT = 56 * 1024 * 1024
LAT_Q_TILE = 512
LAT_Q_TILES = LAT_LEN // LAT_Q_TILE
BIG_TILE = 512
CTX_SEQ_PER_STEP = 2

DA_HEADS = 4
DA_QK = 32
MLA_HEADS = 4
MLA_NOPE = 64
MLA_ROPE = 32
MLA_Q_RANK = 192
MLA_KV_RANK = 128
S5_GROUPS = 16
S5_CH = 16
S5_STATE = 64
S5_CHUNK = 16
HG_HEADS = 4
HG_DK = 64

W_A = 1024
W_B = 512
W_C = 1536
W_D = 768
W_ALL = W_A + W_B + W_C + W_D


def _cparams(*sem):
    return pltpu.CompilerParams(dimension_semantics=sem, vmem_limit_bytes=VMEM_LIMIT)


def _tile_seq(i, tile=ROW_TILE):
    return jnp.where(i < N_CTX // tile, 0, 1 + (i - N_CTX // tile) // (LAT_LEN // tile))


def _silu(x):
    return x * (1.0 / (1.0 + jnp.exp(-x)))


def _dot(a, b):
    return jnp.dot(a, b, preferred_element_type=F32)


def _dot_nt(a, b):
    return lax.dot_general(a, b, (((1,), (1,)), ((), ())), preferred_element_type=F32)


def _mod_body(c_ref, w_ref, b_ref, o_ref):
    c = _silu(c_ref[...]).astype(BF16)
    o_ref[0] = _dot(c, w_ref[0].astype(BF16)) + b_ref[0]


def _modulation(c_rows, w_mod, b_mod):
    tn = 768
    return pl.pallas_call(
        _mod_body,
        grid=(DEPTH, 3 * D_MODEL // tn),
        in_specs=[pl.BlockSpec((8, D_MODEL), lambda l, j: (0, 0)),
                  pl.BlockSpec((1, D_MODEL, tn), lambda l, j: (l, 0, j)),
                  pl.BlockSpec((1, 1, tn), lambda l, j: (l, 0, j))],
        out_specs=pl.BlockSpec((1, 8, tn), lambda l, j: (l, 0, j)),
        out_shape=jax.ShapeDtypeStruct((DEPTH, 8, 3 * D_MODEL), F32),
        compiler_params=_cparams("parallel", "parallel"),
        name="modulation",
    )(c_rows, w_mod, b_mod.reshape(DEPTH, 1, 3 * D_MODEL))


def _split_rows(i, ctx_ref, lat_ref):
    return jnp.where(i < N_CTX // ctx_ref.shape[0], ctx_ref[...], lat_ref[...])


def _ctx_tile_spec(w, tile=ROW_TILE):
    return pl.BlockSpec((tile, w), lambda i: (jnp.minimum(i, N_CTX // tile - 1), 0))


def _lat_tile_spec(w, tile=ROW_TILE):
    return pl.BlockSpec((tile, w), lambda i: (jnp.maximum(i - N_CTX // tile, 0), 0))


def _in_proj_body(*refs, split):
    if split:
        xc_ref, xl_ref, mod_ref, w_ref, oa, ob, oc, od, ok, ov, ou = refs
        x = _split_rows(pl.program_id(0), xc_ref, xl_ref)
    else:
        x_ref, mod_ref, w_ref, oa, ob, oc, od, ok, ov, ou = refs
        x = x_ref[...]
    xn = x * lax.rsqrt(jnp.mean(x * x, axis=-1, keepdims=True) + EPS)
    mod = mod_ref[0]
    h = (xn * (1.0 + mod[1:2]) + mod[0:1]).astype(BF16)
    off = 0
    for o in (oa, ob, oc, od):
        w = o.shape[-1]
        o[...] = _dot(h, w_ref[0, :, off:off + w])
        off += w
    ou[...] = ob[:, :BRANCH].astype(BF16)

    @pl.when(pl.program_id(0) < N_CTX // BIG_TILE)
    def _():
        for t in range(BIG_TILE // CTX_LEN):
            r = slice(t * CTX_LEN, (t + 1) * CTX_LEN)
            ok[t] = oa[r, BRANCH:2 * BRANCH].T
            ov[t] = oa[r, 2 * BRANCH:3 * BRANCH].T


def _in_proj(xs, mod, w_all, l):
    widths = (W_A, W_B, W_C, W_D)
    split = len(xs) == 2
    x_specs = ([_ctx_tile_spec(D_MODEL, BIG_TILE), _lat_tile_spec(D_MODEL, BIG_TILE)] if split
               else [pl.BlockSpec((BIG_TILE, D_MODEL), lambda i: (i, 0))])
    return pl.pallas_call(
        functools.partial(_in_proj_body, split=split),
        grid=(N_TOK // BIG_TILE,),
        in_specs=x_specs + [pl.BlockSpec((1, 3, D_MODEL), lambda i: (_tile_seq(i, BIG_TILE), 0, 0)),
                            pl.BlockSpec((1, D_MODEL, W_ALL), lambda i: (l, 0, 0))],
        out_specs=[pl.BlockSpec((BIG_TILE, w), lambda i: (i, 0)) for w in widths]
        + [pl.BlockSpec((BIG_TILE // CTX_LEN, BRANCH, CTX_LEN), lambda i: (jnp.minimum(i, N_CTX // BIG_TILE - 1), 0, 0))] * 2
        + [pl.BlockSpec((BIG_TILE, BRANCH), lambda i: (i, 0))],
        out_shape=[jax.ShapeDtypeStruct((N_TOK, w), F32) for w in widths]
        + [jax.ShapeDtypeStruct((N_CTX_SEQ, BRANCH, CTX_LEN), F32)] * 2 + [jax.ShapeDtypeStruct((N_TOK, BRANCH), BF16)],
        compiler_params=pltpu.CompilerParams(dimension_semantics=("arbitrary",), vmem_limit_bytes=IN_PROJ_VMEM_LIMIT),
        name="in_proj",
    )(*xs, mod, w_all)


def _arrange_body(wt_ref, o_ref):
    w = wt_ref[0]
    c = _IN_OFF
    rows = lambda name: w[c[name][0]:c[name][-1] + 1]
    zero = lambda n: jnp.zeros((n, w.shape[1]), F32)

    def per_head(x, y):
        xs, ys = rows(x), rows(y)
        return [p for h in range(HG_HEADS) for p in (xs[h * HG_DK:(h + 1) * HG_DK], ys[h * HG_DK:(h + 1) * HG_DK])]
    pieces = ([w[0:W_A + W_B]] + per_head("hg_q", "hg_q") + per_head("hg_ff", "hg_fb") + [rows("hg_i"), rows("hg_g")]
              + [rows("mla_cq"), zero(256 - MLA_Q_RANK), rows("mla_ckv"), zero(MLA_NOPE), rows("mla_kr"),
                 zero(128 - MLA_NOPE - MLA_ROPE), rows("mla_g")])
    o_ref[0] = jnp.concatenate(pieces, axis=0).T.astype(BF16)


def _arrange_w_in(w_in):
    lanes = 256
    wt = jnp.swapaxes(w_in, 1, 2)
    return pl.pallas_call(
        _arrange_body,
        grid=(DEPTH, D_MODEL // lanes),
        in_specs=[pl.BlockSpec((1, wt.shape[1], lanes), lambda l, i: (l, 0, i))],
        out_specs=pl.BlockSpec((1, lanes, W_ALL), lambda l, i: (l, i, 0)),
        out_shape=jax.ShapeDtypeStruct((DEPTH, D_MODEL, W_ALL), BF16),
        compiler_params=_cparams("parallel", "parallel"),
        name="arrange_w_in",
    )(wt)


def _out_proj_body(*refs, split_in, final):
    ac_ref, al_ref, y_ref, u_ref, g_ref, dsk_ref, wglu_ref, c_ref, dc_ref, dl_ref = refs[:10]
    i = pl.program_id(0)
    if split_in:
        xc_ref, xl_ref, mod_ref, w_ref, fn_ref = refs[10:15]
        x = _split_rows(i, xc_ref, xl_ref)
    else:
        x_ref, mod_ref, w_ref, fn_ref = refs[10:14]
        x = x_ref[...]
    b_out = _s5_gated(y_ref[...], u_ref[...], g_ref[...], dsk_ref[...], wglu_ref[...])
    branches = (_split_rows(i, ac_ref, al_ref), b_out, c_ref[...], _split_rows(i, dc_ref, dl_ref))
    acc = None
    for j, r in enumerate(branches):
        t = _dot(r.astype(BF16), w_ref[0, j * BRANCH:(j + 1) * BRANCH, :].astype(BF16))
        acc = t if acc is None else acc + t
    x = x + mod_ref[0][2:3] * acc
    if not final:
        refs[-1][...] = x
        return
    y = x * lax.rsqrt(jnp.mean(x * x, axis=-1, keepdims=True) + EPS) * fn_ref[...]
    yc_ref, yl_ref = refs[-2:]

    @pl.when(i < N_CTX // BIG_TILE)
    def _():
        yc_ref[...] = y

    @pl.when(i >= N_CTX // BIG_TILE)
    def _():
        yl_ref[...] = y


def _out_proj(a, s5, c, d, xs, mod, w_out, l, final_norm, final):
    y_ssm, z_b, d_skip, w_glu = s5
    br = pl.BlockSpec((BIG_TILE, BRANCH), lambda i: (i, 0))
    s5_specs = [br, br, pl.BlockSpec((BIG_TILE, BRANCH), lambda i: (i, 1)),
                pl.BlockSpec((1, BRANCH), lambda i: (0, 0)), pl.BlockSpec((BRANCH, 2 * BRANCH), lambda i: (0, 0))]
    pair = [_ctx_tile_spec(BRANCH, BIG_TILE), _lat_tile_spec(BRANCH, BIG_TILE)]
    split_in = len(xs) == 2
    x_specs = ([_ctx_tile_spec(D_MODEL, BIG_TILE), _lat_tile_spec(D_MODEL, BIG_TILE)] if split_in
               else [pl.BlockSpec((BIG_TILE, D_MODEL), lambda i: (i, 0))])
    if final:
        out_specs = [_ctx_tile_spec(D_MODEL, BIG_TILE), _lat_tile_spec(D_MODEL, BIG_TILE)]
        out_shape = [jax.ShapeDtypeStruct((N_CTX, D_MODEL), F32), jax.ShapeDtypeStruct((N_LAT, D_MODEL), F32)]
    else:
        out_specs = pl.BlockSpec((BIG_TILE, D_MODEL), lambda i: (i, 0))
        out_shape = jax.ShapeDtypeStruct((N_TOK, D_MODEL), F32)
    return pl.pallas_call(
        functools.partial(_out_proj_body, split_in=split_in, final=final),
        grid=(N_TOK // BIG_TILE,),
        in_specs=pair + s5_specs + [br] + pair + x_specs + [
            pl.BlockSpec((1, 3, D_MODEL), lambda i: (_tile_seq(i, BIG_TILE), 0, 0)),
            pl.BlockSpec((1, D_MODEL, D_MODEL), lambda i: (l, 0, 0)),
            pl.BlockSpec((1, D_MODEL), lambda i: (0, 0))],
        out_specs=out_specs,
        out_shape=out_shape,
        compiler_params=_cparams("arbitrary"),
        name="out_proj",
    )(*a, y_ssm, z_b, z_b, d_skip, w_glu, c, *d, *xs, mod, w_out, final_norm)


LOG2E = 1.4426950408889634


def _exp2_rows(s):
    e = jnp.exp2(s - jnp.max(s, axis=-1, keepdims=True))
    return e, jnp.sum(e, axis=-1, keepdims=True)


def _rope(x, cos, sin_lo, sin_hi):
    w = x.shape[-1]
    return x * cos + pltpu.roll(x, w - 8, 1) * sin_lo + pltpu.roll(x, 8, 1) * sin_hi


def _da_kv_body(k_ref, v_ref, cos_ref, slo_ref, shi_ref, ck_ref, cv_ref, ko_ref, vo_ref):
    j = pl.program_id(1)

    @pl.when(j < LAT_TILES)
    def _():
        ko_ref[0] = _rope(k_ref[...], cos_ref[...], slo_ref[...], shi_ref[...]).astype(BF16)
        vo_ref[0] = v_ref[...].astype(BF16)

    @pl.when(j == LAT_TILES)
    def _():
        ko_ref[0] = ck_ref[0].astype(BF16)
        vo_ref[0] = cv_ref[0].astype(BF16)


def _da_latent_kv(z_a, tabs, cache_k, cache_v):
    def rows(col):
        return pl.BlockSpec(
            (ROW_TILE, BRANCH),
            lambda b, j: (CTX_TILES + b * LAT_TILES + jnp.minimum(j, LAT_TILES - 1), col))
    tab = pl.BlockSpec((ROW_TILE, BRANCH), lambda b, j: (jnp.minimum(j, LAT_TILES - 1), 0))
    cache = pl.BlockSpec((1, PAST_LEN, BRANCH), lambda b, j: (b, 0, 0))
    out = pl.BlockSpec((1, ROW_TILE, BRANCH), lambda b, j: (b, j, 0))
    shp = jax.ShapeDtypeStruct((N_LAT_SEQ, LAT_LEN + PAST_LEN, BRANCH), BF16)
    return pl.pallas_call(
        _da_kv_body,
        grid=(N_LAT_SEQ, LAT_TILES + 1),
        in_specs=[rows(1), rows(2), tab, tab, tab, cache, cache],
        out_specs=[out, out],
        out_shape=[shp, shp],
        compiler_params=_cparams("parallel", "parallel"),
        name="da_kv",
    )(z_a, z_a, *tabs, cache_k, cache_v)


def _da_attn_body(lam_ref, ng_ref, q_ref, *rest, rope, lam_init):
    if rope:
        cos_ref, slo_ref, shi_ref, k_ref, v_ref, g_ref, o_ref = rest
        q = _rope(q_ref[...], cos_ref[...], slo_ref[...], shi_ref[...])
        o_ref[...] = _da_attn_tile(lam_ref, ng_ref, q, k_ref[0], v_ref[0], g_ref[...], lam_init, True)
    else:
        k_ref, v_ref, g_ref, o_ref = rest
        for t in range(q_ref.shape[0] // CTX_LEN):
            r = slice(t * CTX_LEN, (t + 1) * CTX_LEN)
            o_ref[r, :] = _da_attn_tile(lam_ref, ng_ref, q_ref[r, :], k_ref[r, :].astype(BF16),
                                        v_ref[r, :].astype(BF16), g_ref[r, :], lam_init, False)


def _da_attn_tile(lam_ref, ng_ref, q, k, v, g, lam_init, scale_after):
    q = q * (DA_QK ** -0.5 * LOG2E)
    lv = lam_ref[...]
    lam = (jnp.exp(jnp.sum(lv[0:1] * lv[1:2], axis=-1, keepdims=True))
           - jnp.exp(jnp.sum(lv[2:3] * lv[3:4], axis=-1, keepdims=True)) + lam_init)
    lane = lax.broadcasted_iota(jnp.int32, (1, BRANCH), 1)
    acc = jnp.zeros(q.shape, F32)
    for h in range(DA_HEADS):
        q1 = jnp.where(lane // DA_QK == 2 * h, q, 0.0).astype(BF16)
        q2 = jnp.where(lane // DA_QK == 2 * h + 1, q, 0.0).astype(BF16)
        e1, l1 = _exp2_rows(_dot_nt(q1, k))
        e2, l2 = _exp2_rows(_dot_nt(q2, k))
        if scale_after:
            a = (e1 - (lam * l1 / l2) * e2).astype(BF16)
            pv = _dot(a, v) * (1.0 / l1)
        else:
            pv = _dot((e1 * (1.0 / l1) - lam * (e2 * (1.0 / l2))).astype(BF16), v)
        acc = jnp.where(lane // (2 * DA_QK) == h, pv, acc)
    sq = acc * acc
    ms = jnp.zeros(q.shape, F32)
    for h in range(DA_HEADS):
        hm = lane // (2 * DA_QK) == h
        ms = jnp.where(hm, jnp.sum(jnp.where(hm, sq, 0.0), axis=-1, keepdims=True), ms)
    o = acc * lax.rsqrt(ms * (1.0 / (2 * DA_QK)) + EPS) * (ng_ref[...] * (1.0 - lam_init))
    return o * _silu(g)


def _da_head_body(lam_ref, ng_ref, q_ref, cos_ref, slo_ref, shi_ref, k_ref, v_ref, g_ref, o_ref, acc_ref, *, lam_init):
    h = pl.program_id(2)
    q = _rope(q_ref[...], cos_ref[...], slo_ref[...], shi_ref[...]) * (DA_QK ** -0.5 * LOG2E)
    k, v = k_ref[0], v_ref[0]
    lv = lam_ref[...]
    lam = (jnp.exp(jnp.sum(lv[0:1] * lv[1:2], axis=-1, keepdims=True))
           - jnp.exp(jnp.sum(lv[2:3] * lv[3:4], axis=-1, keepdims=True)) + lam_init)
    lane = lax.broadcasted_iota(jnp.int32, (1, BRANCH), 1)
    q1 = jnp.where(lane // DA_QK == 2 * h, q, 0.0).astype(BF16)
    q2 = jnp.where(lane // DA_QK == 2 * h + 1, q, 0.0).astype(BF16)
    e1, l1 = _exp2_rows(_dot_nt(q1, k))
    e2, l2 = _exp2_rows(_dot_nt(q2, k))
    a = (e1 - (lam * l1 / l2) * e2).astype(BF16)
    pv = _dot(a, v) * (1.0 / l1)

    @pl.when(h == 0)
    def _():
        acc_ref[...] = jnp.zeros(acc_ref.shape, F32)

    acc_ref[...] = jnp.where(lane // (2 * DA_QK) == h, pv, acc_ref[...])

    @pl.when(h == DA_HEADS - 1)
    def _():
        acc = acc_ref[...]
        sq = acc * acc
        ms = jnp.zeros(acc.shape, F32)
        for hh in range(DA_HEADS):
            hm = lane // (2 * DA_QK) == hh
            ms = jnp.where(hm, jnp.sum(jnp.where(hm, sq, 0.0), axis=-1, keepdims=True), ms)
        o = acc * lax.rsqrt(ms * (1.0 / (2 * DA_QK)) + EPS) * (ng_ref[...] * (1.0 - lam_init))
        o_ref[...] = o * _silu(g_ref[...])


def _da_attention(z_a, lam_vec, norm_g, lam_init, tabs, kv_lat):
    ng = jnp.tile(norm_g.reshape(1, 2 * DA_QK), (1, DA_HEADS))
    small = [pl.BlockSpec((4, DA_QK), lambda *_: (0, 0)), pl.BlockSpec((1, BRANCH), lambda *_: (0, 0))]

    rows = CTX_SEQ_PER_STEP * CTX_LEN

    def col(c):
        return pl.BlockSpec((rows, BRANCH), lambda i: (i, c))
    ctx = pl.pallas_call(
        functools.partial(_da_attn_body, rope=False, lam_init=lam_init),
        grid=(N_CTX // rows,),
        in_specs=small + [col(0), col(1), col(2), col(3)],
        out_specs=pl.BlockSpec((rows, BRANCH), lambda i: (i, 0)),
        out_shape=jax.ShapeDtypeStruct((N_CTX, BRANCH), F32),
        compiler_params=_cparams("parallel"),
        name="da_attn_ctx",
    )(lam_vec, ng, z_a, z_a, z_a, z_a)

    def lcol(c):
        return pl.BlockSpec((LAT_Q_TILE, BRANCH), lambda b, j, h: (N_CTX // LAT_Q_TILE + b * LAT_Q_TILES + j, c))
    tab = pl.BlockSpec((LAT_Q_TILE, BRANCH), lambda b, j, h: (j, 0))
    kvs = pl.BlockSpec((1, LAT_LEN + PAST_LEN, BRANCH), lambda b, j, h: (b, 0, 0))
    lat = pl.pallas_call(
        functools.partial(_da_head_body, lam_init=lam_init),
        grid=(N_LAT_SEQ, LAT_Q_TILES, DA_HEADS),
        in_specs=small + [lcol(0), tab, tab, tab, kvs, kvs, lcol(3)],
        out_specs=pl.BlockSpec((LAT_Q_TILE, BRANCH), lambda b, j, h: (b * LAT_Q_TILES + j, 0)),
        out_shape=jax.ShapeDtypeStruct((N_LAT, BRANCH), F32),
        scratch_shapes=[pltpu.VMEM((LAT_Q_TILE, BRANCH), F32)],
        compiler_params=_cparams("parallel", "parallel", "arbitrary"),
        name="da_attn_lat",
    )(lam_vec, ng, z_a, *tabs, kv_lat[0], kv_lat[1], z_a)
    return ctx, lat


MLA_HEAD_PAD = 128
MLA_QW = MLA_HEADS * MLA_HEAD_PAD


def _mla_prep_body(cq_ref, ckv_ref, kr_ref, ck_t, sk_lo, sk_hi, qn_ref, kvn_ref, wq_ref, q_out, ckv_out, kr_out):
    cq = cq_ref[...]
    ms = jnp.sum(cq * cq, axis=-1, keepdims=True) * (1.0 / MLA_Q_RANK)
    qn = (cq * lax.rsqrt(ms + EPS) * qn_ref[...]).astype(BF16)
    heads = lambda t: jnp.concatenate([t[...]] * MLA_HEADS, axis=-1)
    q = _rope(_dot(qn, wq_ref[...]), heads(ck_t), heads(sk_lo), heads(sk_hi))
    q_out[...] = (q * ((MLA_NOPE + MLA_ROPE) ** -0.5 * LOG2E)).astype(BF16)
    ckv = ckv_ref[...]
    ckv_out[...] = ckv * lax.rsqrt(jnp.mean(ckv * ckv, axis=-1, keepdims=True) + EPS) * kvn_ref[...]
    kr_out[...] = _rope(kr_ref[...], ck_t[...], sk_lo[...], sk_hi[...])


def _mla_prep(z_d, tabs, q_norm_pad, kv_norm, wq):
    ctx_tiles, lat_tiles = N_CTX // BIG_TILE, LAT_LEN // BIG_TILE

    def tab(w):
        return pl.BlockSpec((BIG_TILE, w), lambda i: (jnp.where(i < ctx_tiles, lat_tiles, (i - ctx_tiles) % lat_tiles), 0))

    def col(w, c):
        return pl.BlockSpec((BIG_TILE, w), lambda i: (i, c))

    def const(shape):
        return pl.BlockSpec(shape, lambda i: (0, 0))
    return pl.pallas_call(
        _mla_prep_body,
        grid=(N_TOK // BIG_TILE,),
        in_specs=[col(256, 0), col(128, 2), col(128, 3),
                  tab(128), tab(128), tab(128),
                  const((1, 256)), const((1, 128)), const((256, MLA_QW))],
        out_specs=[col(MLA_QW, 0), col(128, 0), col(128, 0)],
        out_shape=[jax.ShapeDtypeStruct((N_TOK, MLA_QW), BF16),
                   jax.ShapeDtypeStruct((N_TOK, 128), F32),
                   jax.ShapeDtypeStruct((N_TOK, 128), F32)],
        compiler_params=_cparams("parallel"),
        name="mla_prep",
    )(z_d, z_d, z_d, *tabs, q_norm_pad, kv_norm, wq)


def _mla_kv_math(ckv, kr, wk_ref, wv_ref, k_out, v_out):
    c = ckv.astype(BF16)
    k_out[...] = (_dot(c, wk_ref[...]) + jnp.concatenate([kr] * MLA_HEADS, axis=-1)).astype(BF16).reshape(k_out.shape)
    v_out[...] = _dot(c, wv_ref[...]).astype(BF16).reshape(v_out.shape)


def _mla_kv_ctx_body(ckv_ref, kr_ref, wk_ref, wv_ref, k_out, v_out):
    _mla_kv_math(ckv_ref[...], kr_ref[...], wk_ref, wv_ref, k_out, v_out)


def _mla_kv_lat_body(ckv_ref, kr_ref, cckv_ref, ckr_ref, wk_ref, wv_ref, k_out, v_out):
    j = pl.program_id(1)

    @pl.when(j < LAT_TILES)
    def _():
        _mla_kv_math(ckv_ref[...], kr_ref[...], wk_ref, wv_ref, k_out, v_out)

    @pl.when(j == LAT_TILES)
    def _():
        _mla_kv_math(cckv_ref[0], ckr_ref[0], wk_ref, wv_ref, k_out, v_out)


def _mla_kv(ckv, kr, cache_ckv, cache_kr, wk, wv):
    weights = [pl.BlockSpec((128, MLA_QW), lambda *_: (0, 0)), pl.BlockSpec((128, BRANCH), lambda *_: (0, 0))]
    k_ctx, v_ctx = pl.pallas_call(
        _mla_kv_ctx_body,
        grid=(N_CTX // BIG_TILE,),
        in_specs=[pl.BlockSpec((BIG_TILE, 128), lambda i: (i, 0)), pl.BlockSpec((BIG_TILE, 128), lambda i: (i, 0))] + weights,
        out_specs=[pl.BlockSpec((BIG_TILE, MLA_QW), lambda i: (i, 0)),
                   pl.BlockSpec((BIG_TILE, BRANCH), lambda i: (i, 0))],
        out_shape=[jax.ShapeDtypeStruct((N_CTX, MLA_QW), BF16), jax.ShapeDtypeStruct((N_CTX, BRANCH), BF16)],
        compiler_params=_cparams("parallel"),
        name="mla_kv_ctx",
    )(ckv, kr, wk, wv)
    rows = pl.BlockSpec((ROW_TILE, 128), lambda b, j: (CTX_TILES + b * LAT_TILES + jnp.minimum(j, LAT_TILES - 1), 0))
    cache = pl.BlockSpec((1, PAST_LEN, 128), lambda b, j: (b, 0, 0))
    lk = LAT_LEN + PAST_LEN
    k_lat, v_lat = pl.pallas_call(
        _mla_kv_lat_body,
        grid=(N_LAT_SEQ, LAT_TILES + 1),
        in_specs=[rows, rows, cache, cache] + weights,
        out_specs=[pl.BlockSpec((1, ROW_TILE, MLA_QW), lambda b, j: (b, j, 0)),
                   pl.BlockSpec((1, ROW_TILE, BRANCH), lambda b, j: (b, j, 0))],
        out_shape=[jax.ShapeDtypeStruct((N_LAT_SEQ, lk, MLA_QW), BF16), jax.ShapeDtypeStruct((N_LAT_SEQ, lk, BRANCH), BF16)],
        compiler_params=_cparams("parallel", "parallel"),
        name="mla_kv_lat",
    )(ckv, kr, cache_ckv, cache_kr, wk, wv)
    return k_ctx, v_ctx, k_lat, v_lat


def _mla_attn_body(q_ref, k_ref, v_ref, g_ref, o_ref, *, ctx):
    if ctx:
        for t in range(q_ref.shape[0] // CTX_LEN):
            r = slice(t * CTX_LEN, (t + 1) * CTX_LEN)
            o_ref[r, :] = _mla_attn_tile(q_ref[r, :], k_ref[r, :], v_ref[r, :], g_ref[r, :])
    else:
        o_ref[...] = _mla_attn_tile(q_ref[...], k_ref[0], v_ref[0], g_ref[...])


def _mla_attn_tile(q, k, v, g):
    lane = lax.broadcasted_iota(jnp.int32, (1, BRANCH), 1)
    acc = jnp.zeros((q.shape[0], BRANCH), F32)
    for h in range(MLA_HEADS):
        sl = slice(h * MLA_HEAD_PAD, (h + 1) * MLA_HEAD_PAD)
        e, l = _exp2_rows(_dot_nt(q[:, sl], k[:, sl]))
        acc = jnp.where(lane // 64 == h, _dot(e.astype(BF16), v) * (1.0 / l), acc)
    return acc * _silu(g)


def _mla_attention(z_d, q, k_ctx, v_ctx, k_lat, v_lat):
    rows = CTX_SEQ_PER_STEP * CTX_LEN
    ctx = pl.pallas_call(
        functools.partial(_mla_attn_body, ctx=True),
        grid=(N_CTX // rows,),
        in_specs=[pl.BlockSpec((rows, MLA_QW), lambda i: (i, 0)),
                  pl.BlockSpec((rows, MLA_QW), lambda i: (i, 0)),
                  pl.BlockSpec((rows, BRANCH), lambda i: (i, 0)),
                  pl.BlockSpec((rows, BRANCH), lambda i: (i, 2))],
        out_specs=pl.BlockSpec((rows, BRANCH), lambda i: (i, 0)),
        out_shape=jax.ShapeDtypeStruct((N_CTX, BRANCH), F32),
        compiler_params=_cparams("parallel"),
        name="mla_attn_ctx",
    )(q, k_ctx, v_ctx, z_d)
    lk = LAT_LEN + PAST_LEN
    lat = pl.pallas_call(
        functools.partial(_mla_attn_body, ctx=False),
        grid=(N_LAT_SEQ, LAT_Q_TILES),
        in_specs=[pl.BlockSpec((LAT_Q_TILE, MLA_QW), lambda b, j: (N_CTX // LAT_Q_TILE + b * LAT_Q_TILES + j, 0)),
                  pl.BlockSpec((1, lk, MLA_QW), lambda b, j: (b, 0, 0)),
                  pl.BlockSpec((1, lk, BRANCH), lambda b, j: (b, 0, 0)),
                  pl.BlockSpec((LAT_Q_TILE, BRANCH), lambda b, j: (N_CTX // LAT_Q_TILE + b * LAT_Q_TILES + j, 2))],
        out_specs=pl.BlockSpec((LAT_Q_TILE, BRANCH), lambda b, j: (b * LAT_Q_TILES + j, 0)),
        out_shape=jax.ShapeDtypeStruct((N_LAT, BRANCH), F32),
        compiler_params=_cparams("parallel", "parallel"),
        name="mla_attn_lat",
    )(q, k_lat, v_lat, z_d)
    return ctx, lat


S5_TAP = S5_CHUNK * S5_CH
S5_NCHUNK = N_TOK // S5_CHUNK
S5_CTX_CH = N_CTX // S5_CHUNK
S5_CTX_SEQ_CH = CTX_LEN // S5_CHUNK
S5_LAT_SEQ_CH = LAT_LEN // S5_CHUNK
S5_SCAN_STEPS = S5_LAT_SEQ_CH.bit_length() - 1
S5_TABLE_GROUPS = 8


def _s5_body(x_ref, mt_ref, bst_ref, cot_ref, a_ref, h0_ref, y_ref, fin_ref):
    x = x_ref[...].reshape(S5_TAP, S5_NCHUNK)
    y = _dot(mt_ref[0, 0], x)
    s = _dot(bst_ref[0, 0], x)
    lane = lax.broadcasted_iota(jnp.int32, (1, S5_NCHUNK), 1)
    is_lat = lane >= S5_CTX_CH
    pos_f = jnp.where(is_lat, (lane - S5_CTX_CH) & (S5_LAT_SEQ_CH - 1), lane & (S5_CTX_SEQ_CH - 1))
    pos_b = jnp.where(is_lat, S5_LAT_SEQ_CH - 1, S5_CTX_SEQ_CH - 1) - pos_f
    hin = []
    for d in range(2):
        n = S5_STATE
        sre, sim = s[2 * d * n:(2 * d + 1) * n], s[(2 * d + 1) * n:(2 * d + 2) * n]
        are = jnp.concatenate([a_ref[0, 0, 2 * d]] * (S5_NCHUNK // 128), axis=-1)
        aim = jnp.concatenate([a_ref[0, 0, 2 * d + 1]] * (S5_NCHUNK // 128), axis=-1)
        pos = pos_f if d == 0 else pos_b
        h0r, h0i = jnp.zeros_like(sre), jnp.zeros_like(sre)
        for b in range(N_LAT_SEQ):
            first = S5_CTX_CH + b * S5_LAT_SEQ_CH + (0 if d == 0 else S5_LAT_SEQ_CH - 1)
            h0r = jnp.where(lane == first, h0_ref[0, 2 * d][:, b:b + 1], h0r)
            h0i = jnp.where(lane == first, h0_ref[0, 2 * d + 1][:, b:b + 1], h0i)
        xr = sre + are * h0r - aim * h0i
        xi = sim + are * h0i + aim * h0r
        pr, pi = are, aim
        for j in range(S5_SCAN_STEPS):
            sh = 1 << j
            shift = sh if d == 0 else S5_NCHUNK - sh
            rr, ri = pltpu.roll(xr, shift, 1), pltpu.roll(xi, shift, 1)
            ok = pos >= sh
            xr, xi = (xr + jnp.where(ok, pr * rr - pi * ri, 0.0), xi + jnp.where(ok, pr * ri + pi * rr, 0.0))
            pr, pi = pr * pr - pi * pi, 2.0 * pr * pi
        last = lax.broadcasted_iota(jnp.int32, (1, 128), 1) * S5_CTX_SEQ_CH + (S5_CTX_SEQ_CH - 1 if d == 0 else 0)
        pick = jnp.where(lax.broadcasted_iota(jnp.int32, (S5_CTX_CH, 1), 0) == last, 1.0, 0.0)
        fin_ref[0, 2 * d] = _dot_sel(xr[:, :S5_CTX_CH], pick)
        fin_ref[0, 2 * d + 1] = _dot_sel(xi[:, :S5_CTX_CH], pick)
        one = 1 if d == 0 else S5_NCHUNK - 1
        hin.append(jnp.where(pos >= 1, pltpu.roll(xr, one, 1), h0r))
        hin.append(jnp.where(pos >= 1, pltpu.roll(xi, one, 1), h0i))
    y = y + _dot(cot_ref[0, 0], jnp.concatenate(hin, axis=0).astype(BF16))
    y_ref[...] = y.reshape(S5_CHUNK, S5_CH, S5_NCHUNK)


def _s5_scan(x_all, mt, bst, cot, a16, h0, l):
    g = S5_GROUPS
    sq = pl.BlockSpec((1, 1, S5_TAP, S5_TAP), lambda i: (l, i, 0, 0))
    st = pl.BlockSpec((1, 4, S5_STATE, 128), lambda i: (i, 0, 0, 0))
    return pl.pallas_call(
        _s5_body,
        grid=(g,),
        in_specs=[pl.BlockSpec((S5_CHUNK, S5_CH, S5_NCHUNK), lambda i: (0, i, 0)), sq, sq, sq,
                  pl.BlockSpec((1, 1, 4, S5_STATE, 128), lambda i: (l, i, 0, 0, 0)), st],
        out_specs=[pl.BlockSpec((S5_CHUNK, S5_CH, S5_NCHUNK), lambda i: (0, i, 0)),
                   pl.BlockSpec((1, 4, S5_STATE, 128), lambda i: (i, 0, 0, 0))],
        out_shape=[jax.ShapeDtypeStruct((S5_CHUNK, BRANCH, S5_NCHUNK), F32),
                   jax.ShapeDtypeStruct((g, 4, S5_STATE, 128), F32)],
        compiler_params=_cparams("parallel"),
        name="s5_scan",
    )(x_all, mt, bst, cot, a16, h0)


def _s5_gated(y_ssm, u, g, d_skip, w_glu):
    y = u * d_skip + y_ssm
    ge = 0.5 * y * (1.0 + jnp.tanh(0.7978845608028654 * (y + 0.044715 * (y * y * y))))
    gl = _dot(ge.astype(BF16), w_glu)
    return gl[:, :BRANCH] * (1.0 / (1.0 + jnp.exp(-gl[:, BRANCH:]))) * _silu(g)


HG_CHUNK = ROW_TILE
HG_W = 2 * HG_HEADS * HG_DK
HG_HEAD_W = 2 * HG_DK
HG_LAT_CHUNKS = LAT_LEN // HG_CHUNK
HG_CHUNKS = N_TOK // HG_CHUNK


def _hg_gates(z, lb):
    e = jnp.exp(-jnp.abs(z))
    r = 1.0 / (1.0 + e)
    sig_pos = jnp.where(z >= 0, r, e * r)
    sig_neg = jnp.where(z >= 0, e * r, r)
    return lb + (1.0 - lb) * sig_pos, (1.0 - lb) * sig_neg


def _bcast_row(x, period, r):
    n, w = x.shape
    if period >= 8:
        x3 = x.reshape(n // period, period, w)
        return jnp.broadcast_to(x3[:, r:r + 1, :], x3.shape).reshape(n, w)
    x3 = x.reshape(n // 8, 8, w)
    sub = lax.broadcasted_iota(jnp.int32, (1, 8, 1), 1)
    out = None
    for j in range(8 // period):
        b = jnp.broadcast_to(x3[:, j * period + r:j * period + r + 1, :], x3.shape)
        out = b if out is None else jnp.where(sub >= j * period, b, out)
    return out.reshape(n, w)


def _hg_scans(f, isb):
    n = f.shape[0]
    row = lax.broadcasted_iota(jnp.int32, (n, 1), 0)
    p, r = f, jnp.ones_like(f)
    levels = []
    h, sh = 1, 0
    while h < n:
        levels.append((h, sh, p, r))
        up = (row >> sh) & 1
        tot_p = jnp.where(isb == 1, _bcast_row(p, 2 * h, h), _bcast_row(p, 2 * h, h - 1))
        tot_r = jnp.where(isb == 1, _bcast_row(p, 2 * h, 0), _bcast_row(p, 2 * h, 2 * h - 1))
        p = p * jnp.where(up != isb, tot_p, 1.0)
        r = r * jnp.where(up == isb, tot_r, 1.0)
        h, sh = 2 * h, sh + 1
    return levels, p, r


def _hg_state_body(zf_ref, zb_ref, vf_ref, vb_ref, lb_ref, s0_ref, sf_out, sb_out, s_scr):
    i = pl.program_id(0)

    @pl.when(i % HG_LAT_CHUNKS == 0)
    def _():
        s_scr[...] = s0_ref[0]

    sf_out[0] = s_scr[:, 0:HG_DK, :]
    sb_out[0] = s_scr[:, HG_DK:, :]
    lane5 = lax.broadcasted_iota(jnp.int32, (1, HG_W), 1)
    isb = (lane5 >> 6) & 1
    z = jnp.where(isb == 1, zb_ref[...], zf_ref[...])
    f, k = _hg_gates(z, lb_ref[...])
    r, ptot = _hg_chunk_decay(f, isb)
    kt = k * r
    lane = lax.broadcasted_iota(jnp.int32, (1, BRANCH), 1)
    vf = vf_ref[...]
    vb = vb_ref[...]
    for hd in range(HG_HEADS):
        sl = slice(hd * HG_HEAD_W, (hd + 1) * HG_HEAD_W)
        kth = kt[:, sl].T.astype(BF16)
        hm = (lane >> 6) == hd
        d_f = _dot(kth, jnp.where(hm, vf, 0.0).astype(BF16))
        d_b = _dot(kth, jnp.where(hm, vb, 0.0).astype(BF16))
        ds = jnp.concatenate([d_f[:HG_DK], d_b[HG_DK:]], axis=0)
        pcol = jnp.broadcast_to(ptot[:, sl], (HG_HEAD_W, HG_HEAD_W)).T[:, 0:1]
        s_scr[hd] = s_scr[hd] * pcol + ds


def _hg_chunk_decay(f, isb):
    n = f.shape[0]
    row = lax.broadcasted_iota(jnp.int32, (n, 1), 0)
    dist = jnp.where(isb == 1, row, n - 1 - row)
    x = f
    sh = 1
    while sh < n:
        src = jnp.where(isb == 1, pltpu.roll(x, sh, 0), pltpu.roll(x, n - sh, 0))
        x = x * jnp.where(dist >= sh, src, 1.0)
        sh *= 2
    total = jnp.where(isb == 1, x[n - 1:n], x[0:1])
    nxt = jnp.where(isb == 1, pltpu.roll(x, 1, 0), pltpu.roll(x, n - 1, 0))
    return jnp.where(dist >= 1, nxt, 1.0), total


HG_LAT_STEPS = N_LAT_SEQ * HG_LAT_CHUNKS


def _hg_lat_rev(i):
    return (i // HG_LAT_CHUNKS) * HG_LAT_CHUNKS + (HG_LAT_CHUNKS - 1 - i % HG_LAT_CHUNKS)


def _hg_states(z_c, lb, s0):
    first = N_CTX_SEQ
    zz_f = pl.BlockSpec((HG_CHUNK, HG_W), lambda i: (first + i, 1))
    zz_b = pl.BlockSpec((HG_CHUNK, HG_W), lambda i: (first + _hg_lat_rev(i), 1))
    v_f = pl.BlockSpec((HG_CHUNK, BRANCH), lambda i: (first + i, 4))
    v_b = pl.BlockSpec((HG_CHUNK, BRANCH), lambda i: (first + _hg_lat_rev(i), 4))
    st = (HG_HEADS, HG_HEAD_W, BRANCH)
    half = (HG_HEADS, HG_DK, BRANCH)
    return pl.pallas_call(
        _hg_state_body,
        grid=(HG_LAT_STEPS,),
        in_specs=[zz_f, zz_b, v_f, v_b,
                  pl.BlockSpec((1, HG_W), lambda i: (0, 0)),
                  pl.BlockSpec((1,) + st, lambda i: (i // HG_LAT_CHUNKS, 0, 0, 0))],
        out_specs=[pl.BlockSpec((1,) + half, lambda i: (i, 0, 0, 0)),
                   pl.BlockSpec((1,) + half, lambda i: (_hg_lat_rev(i), 0, 0, 0))],
        out_shape=[jax.ShapeDtypeStruct((HG_LAT_STEPS,) + half, F32),
                   jax.ShapeDtypeStruct((HG_LAT_STEPS,) + half, F32)],
        scratch_shapes=[pltpu.VMEM(st, F32)],
        compiler_params=_cparams("arbitrary"),
        name="hg_states",
    )(z_c, z_c, z_c, z_c, lb, s0)


def _hg_main_body(qq_ref, zz_ref, v_ref, g_ref, sf_ref, sb_ref, lb_ref, ng_ref, o_ref, fin_ref):
    n = HG_CHUNK
    i = pl.program_id(0)
    qq = qq_ref[...]
    lane5 = lax.broadcasted_iota(jnp.int32, (1, HG_W), 1)
    isb = (lane5 >> 6) & 1
    f, k = _hg_gates(zz_ref[...], lb_ref[...])
    levels, pfull, rfull = _hg_scans(f, isb)
    row = lax.broadcasted_iota(jnp.int32, (n, 1), 0)
    col = lax.broadcasted_iota(jnp.int32, (1, n), 1)
    ops = [(qq.astype(BF16), k.astype(BF16), row == col)]
    for h, sh, p, r in levels:
        up = (row >> sh) & 1
        qt = jnp.where(up != isb, qq * p, 0.0).astype(BF16)
        kt = jnp.where(up == isb, k * r, 0.0).astype(BF16)
        ops.append((qt, kt, (row >> (sh + 1)) == (col >> (sh + 1))))
    qc = (qq * pfull).astype(BF16)
    v = v_ref[...]
    vb = v.astype(BF16)
    lane = lax.broadcasted_iota(jnp.int32, (1, BRANCH), 1)
    latent = i >= N_CTX_SEQ
    acc = jnp.zeros((n, BRANCH), F32)
    for hd in range(HG_HEADS):
        sl = slice(hd * HG_HEAD_W, (hd + 1) * HG_HEAD_W)
        a = jnp.zeros((n, n), F32)
        for qt, kt, mask in ops:
            a = a + jnp.where(mask, _dot_nt(qt[:, sl], kt[:, sl]), 0.0)
        s_in = jnp.concatenate([sf_ref[0, hd], sb_ref[0, hd]], axis=0)
        s_in = jnp.where(latent, s_in, 0.0).astype(BF16)
        o_h = _dot(a.astype(BF16), vb) + _dot(qc[:, sl], s_in)
        acc = jnp.where((lane >> 6) == hd, o_h, acc)
    sq = acc * acc
    ms = jnp.zeros((n, BRANCH), F32)
    for hd in range(HG_HEADS):
        hm = (lane >> 6) == hd
        ms = jnp.where(hm, jnp.sum(jnp.where(hm, sq, 0.0), axis=-1, keepdims=True), ms)
    o_ref[...] = acc * lax.rsqrt(ms * (1.0 / HG_DK) + EPS) * ng_ref[...] * _silu(g_ref[...])

    @pl.when(i < N_CTX_SEQ)
    def _():
        kt_full = k * rfull
        for hd in range(HG_HEADS):
            kth = kt_full[:, hd * HG_HEAD_W:(hd + 1) * HG_HEAD_W].T.astype(BF16)
            ds = _dot(kth, jnp.where((lane >> 6) == hd, v, 0.0).astype(BF16))
            fin_ref[0, hd] = ds[:, hd * HG_DK:(hd + 1) * HG_DK]


def _hg_main(z_c, s_f, s_b, lb, norm_g):
    half = (1, HG_HEADS, HG_DK, BRANCH)
    lat = lambda i: (jnp.maximum(i - N_CTX_SEQ, 0), 0, 0, 0)
    fin = (HG_HEADS, HG_HEAD_W, HG_DK)
    return pl.pallas_call(
        _hg_main_body,
        grid=(HG_CHUNKS,),
        in_specs=[pl.BlockSpec((HG_CHUNK, HG_W), lambda i: (i, 0)),
                  pl.BlockSpec((HG_CHUNK, HG_W), lambda i: (i, 1)),
                  pl.BlockSpec((HG_CHUNK, BRANCH), lambda i: (i, 4)),
                  pl.BlockSpec((HG_CHUNK, BRANCH), lambda i: (i, 5)),
                  pl.BlockSpec(half, lat),
                  pl.BlockSpec(half, lat),
                  pl.BlockSpec((1, HG_W), lambda i: (0, 0)),
                  pl.BlockSpec((1, BRANCH), lambda i: (0, 0))],
        out_specs=[pl.BlockSpec((HG_CHUNK, BRANCH), lambda i: (i, 0)),
                   pl.BlockSpec((1,) + fin, lambda i: (jnp.minimum(i, N_CTX_SEQ - 1), 0, 0, 0))],
        out_shape=[jax.ShapeDtypeStruct((N_TOK, BRANCH), F32),
                   jax.ShapeDtypeStruct((N_CTX_SEQ,) + fin, F32)],
        compiler_params=_cparams("arbitrary"),
        name="hg_main",
    )(z_c, z_c, z_c, z_c, s_f, s_b, lb, norm_g)


def _take_cols(w, plan):
    idx = np.concatenate([p[0] for p in plan]).astype(np.int32)
    sign = np.concatenate([np.broadcast_to(p[1], p[0].shape) for p in plan]).astype(np.float32)
    return jnp.take(w, jnp.asarray(idx), axis=-1) * jnp.asarray(sign)


def _zeros(n):
    return (np.zeros(n, np.int64), 0.0)


_IN_OFF = {}
_off = 0
for _name, _n in (("da_q", 256), ("da_k", 256), ("da_v", 256), ("da_g", 256), ("s5_u", 256), ("s5_g", 256),
                  ("hg_q", 256), ("hg_ff", 256), ("hg_fb", 256), ("hg_i", 256), ("hg_g", 256),
                  ("mla_cq", MLA_Q_RANK), ("mla_ckv", MLA_KV_RANK), ("mla_kr", MLA_ROPE), ("mla_g", 256)):
    _IN_OFF[_name] = np.arange(_off, _off + _n)
    _off += _n


def _rope_tables():
    t = np.arange(LAT_LEN)
    pos = np.stack([t // GRID_W, t % GRID_W], axis=1).astype(np.float32)
    inv_freq = (np.float32(ROPE_BASE) ** (-np.arange(8, dtype=np.float32) / np.float32(8))).astype(np.float32)
    r = np.arange(MLA_ROPE)
    ang = (pos[:, r // 16] * inv_freq[r % 8][None, :]).astype(np.float64)
    cos32, sin32 = np.cos(ang).astype(np.float32), np.sin(ang).astype(np.float32)
    lo = (np.arange(MLA_ROPE) % 16 < 8)[None, :]
    sin_lo32, sin_hi32 = np.where(lo, -sin32, 0.0).astype(np.float32), np.where(lo, 0.0, sin32).astype(np.float32)
    da_tabs = tuple(np.tile(x, (1, 8)) for x in (cos32, sin_lo32, sin_hi32))

    def head(x, fill):
        h = np.concatenate([np.full((LAT_LEN, MLA_NOPE), fill, np.float32), x,
                            np.full((LAT_LEN, MLA_HEAD_PAD - MLA_NOPE - MLA_ROPE), fill, np.float32)], axis=1)
        return np.concatenate([h, np.full((BIG_TILE, MLA_HEAD_PAD), fill, np.float32)], axis=0)
    k_tabs = (head(cos32, 1.0), head(sin_lo32, 0.0), head(sin_hi32, 0.0))
    return tuple(jnp.asarray(x) for x in da_tabs), tuple(jnp.asarray(x) for x in k_tabs)


def _mla_weights(w_uq, w_ukv, q_norm):
    hd = MLA_NOPE + MLA_ROPE
    pad_tail = _zeros(MLA_HEAD_PAD - hd)
    q_plan, k_plan, v_plan = [], [], []
    for h in range(MLA_HEADS):
        nope, rope = np.arange(h * hd, h * hd + MLA_NOPE), np.arange(h * hd + MLA_NOPE, (h + 1) * hd)
        q_plan += [(nope, 1.0), (rope, 1.0), pad_tail]
        k_plan += [(np.arange(h * 2 * MLA_NOPE, h * 2 * MLA_NOPE + MLA_NOPE), 1.0), _zeros(MLA_HEAD_PAD - MLA_NOPE)]
        v_plan += [(np.arange(h * 2 * MLA_NOPE + MLA_NOPE, (h + 1) * 2 * MLA_NOPE), 1.0)]
    pad_rows = lambda x: jnp.pad(x, ((0, 256 - MLA_Q_RANK), (0, 0))).astype(BF16)
    qn = jnp.pad(q_norm, (0, 256 - MLA_Q_RANK)).reshape(1, 256)
    return (pad_rows(_take_cols(w_uq, q_plan)), _take_cols(w_ukv, k_plan).astype(BF16),
            _take_cols(w_ukv, v_plan).astype(BF16), qn)


def _split_bf16(a):
    hi = a.astype(BF16)
    return hi, (a - hi.astype(F32)).astype(BF16)


def _dot_sel(a, sel):
    hi, lo = _split_bf16(a)
    sel = sel.astype(BF16)
    return _dot(hi, sel) + _dot(lo, sel)


def _dot_x3(a, b):
    a_hi, a_lo = _split_bf16(a)
    b_hi, b_lo = _split_bf16(b)
    return _dot(a_hi, b_hi) + _dot(a_hi, b_lo) + _dot(a_lo, b_hi)


def _s5_table_body(xy_ref, bb_ref, c_ref, ct_ref, mt_ref, bst_ref, cot_ref, a_ref):
    n, t, ch = S5_STATE, S5_CHUNK, S5_CH
    wide = 2 * S5_TAP
    tau_i = lax.broadcasted_iota(jnp.int32, (1, 128), 1)
    tau = tau_i.astype(F32)
    sel_row = lax.broadcasted_iota(jnp.int32, (128, 1), 0)

    def lag(width):
        return lax.broadcasted_iota(jnp.int32, (1, width), 1) >> 4

    def onehot(cond):
        return jnp.where(cond, 1.0, 0.0).astype(F32)
    j = lag(wide)
    e_z = (onehot((j <= t - 1) & (sel_row == t - 1 - j)), onehot((j >= t - 1) & (j <= 2 * t - 2) & (sel_row == j - (t - 1))))
    jc = lag(S5_TAP)
    e_c = (onehot(sel_row == jc + 1), onehot(sel_row == t - jc))
    ch_row = lax.broadcasted_iota(jnp.int32, (ch, 1), 0)
    tile_w = onehot((lax.broadcasted_iota(jnp.int32, (1, wide), 1) & (ch - 1)) == ch_row)
    tile_n = onehot((lax.broadcasted_iota(jnp.int32, (1, S5_TAP), 1) & (ch - 1)) == ch_row)

    for gi in range(S5_TABLE_GROUPS):
        xy = xy_ref[0, gi]
        z, cot_rows, klong = [], [], None
        for d in range(2):
            x, y = xy[:, 2 * d:2 * d + 1], xy[:, 2 * d + 1:2 * d + 2]
            mag = jnp.exp(jnp.where(tau_i <= t, tau, 0.0) * x)
            ang = jnp.where(tau_i <= t, tau, 0.0) * y
            p_re = jnp.where(tau_i <= t, mag * jnp.cos(ang), 0.0)
            p_im = jnp.where(tau_i <= t, mag * jnp.sin(ang), 0.0)
            a_ref[0, gi, 2 * d] = jnp.broadcast_to(p_re[:, t:t + 1], (n, 128))
            a_ref[0, gi, 2 * d + 1] = jnp.broadcast_to(p_im[:, t:t + 1], (n, 128))
            pz_re, pz_im = _dot_sel(p_re, e_z[d]), _dot_sel(p_im, e_z[d])
            b_re, b_im = _dot_sel(bb_ref[0, gi, 2 * d], tile_w), _dot_sel(bb_ref[0, gi, 2 * d + 1], tile_w)
            z_re, z_im = pz_re * b_re - pz_im * b_im, pz_re * b_im + pz_im * b_re
            z += [z_re, z_im]
            part = _dot_x3(c_ref[0, gi, 2 * d], z_re) - _dot_x3(c_ref[0, gi, 2 * d + 1], z_im)
            klong = part if klong is None else klong + part
            pc_re, pc_im = _dot_sel(p_re, e_c[d]), _dot_sel(p_im, e_c[d])
            c_re, c_im = _dot_sel(ct_ref[0, gi, 2 * d], tile_n), _dot_sel(ct_ref[0, gi, 2 * d + 1], tile_n)
            cot_rows += [c_re * pc_re - c_im * pc_im, -(c_re * pc_im + c_im * pc_re)]
        for tt in range(t):
            off = (t - 1 - tt) * ch
            win = klong if off == 0 else pltpu.roll(klong, wide - off, 1)
            mt_ref[0, gi, tt * ch:(tt + 1) * ch, :] = win[:, :S5_TAP].astype(BF16)
        back = pltpu.roll(z[2], wide - (t - 1) * ch, 1), pltpu.roll(z[3], wide - (t - 1) * ch, 1)
        for k, rows in enumerate((z[0], z[1], back[0], back[1])):
            bst_ref[0, gi, k * n:(k + 1) * n, :] = rows[:, :S5_TAP].astype(BF16)
        cot_ref[0, gi] = jnp.concatenate(cot_rows, axis=0).T.astype(BF16)


def _s5_tables(a_re, a_im, log_dt, b_re, b_im, c_re, c_im):
    nl, g, n, ch = a_re.shape[0], S5_GROUPS, S5_STATE, S5_CH
    step = jnp.exp(log_dt)[..., None]
    mag = jnp.exp(a_re * step)
    ab_re, ab_im = mag * jnp.cos(a_im * step), mag * jnp.sin(a_im * step)
    den = a_re * a_re + a_im * a_im
    f_re = ((ab_re - 1.0) * a_re + ab_im * a_im) / den
    f_im = (ab_im * a_re - (ab_re - 1.0) * a_im) / den
    bb_re = f_re[..., None] * b_re - f_im[..., None] * b_im
    bb_im = f_re[..., None] * b_im + f_im[..., None] * b_re
    by_group = lambda x: jnp.moveaxis(x, 1, 2)
    pair = lambda re, im: jnp.stack([by_group(re), by_group(im)], axis=3).reshape((nl, g, 4) + re.shape[3:])
    xy = jnp.stack([by_group(a_re * step), by_group(a_im * step)], axis=3).reshape(nl, g, 4, n)
    xy = jnp.pad(jnp.swapaxes(xy, 2, 3), ((0, 0), (0, 0), (0, 0), (0, 4)))
    gs = S5_TABLE_GROUPS
    mat = pl.BlockSpec((1, gs, S5_TAP, S5_TAP), lambda l, i: (l, i, 0, 0))
    return pl.pallas_call(
        _s5_table_body,
        grid=(nl, g // gs),
        in_specs=[pl.BlockSpec((1, gs, n, 8), lambda l, i: (l, i, 0, 0)),
                  pl.BlockSpec((1, gs, 4, n, ch), lambda l, i: (l, i, 0, 0, 0)),
                  pl.BlockSpec((1, gs, 4, ch, n), lambda l, i: (l, i, 0, 0, 0)),
                  pl.BlockSpec((1, gs, 4, n, ch), lambda l, i: (l, i, 0, 0, 0))],
        out_specs=[mat, mat, mat, pl.BlockSpec((1, gs, 4, n, 128), lambda l, i: (l, i, 0, 0, 0))],
        out_shape=[jax.ShapeDtypeStruct((nl, g, S5_TAP, S5_TAP), BF16)] * 3
        + [jax.ShapeDtypeStruct((nl, g, 4, n, 128), F32)],
        compiler_params=_cparams("parallel", "parallel"),
        name="s5_tables",
    )(xy, pair(bb_re, bb_im), pair(c_re, c_im), pair(jnp.swapaxes(c_re, -1, -2), jnp.swapaxes(c_im, -1, -2)))


def _s5_chunk_lanes(u):
    return u.reshape(S5_NCHUNK, S5_CHUNK, BRANCH).transpose(1, 2, 0)


def _s5_token_rows(y):
    return y.transpose(2, 0, 1).reshape(N_TOK, BRANCH)


def kernel(x_prompt, x_sample, cache_diff_k, cache_diff_v, state_s5, state_hgrn, cache_mla_ckv, cache_mla_krope, c, c_ctx, w_mod, b_mod, w_in, w_out, da_lambda, da_norm, s5_a_re, s5_a_im, s5_log_dt, s5_b_re, s5_b_im, s5_c_re, s5_c_im, s5_d, s5_w_glu, hg_lb, hg_norm, mla_q_norm, mla_w_uq, mla_kv_norm, mla_w_ukv, final_norm):
    lb_w = jax.nn.softmax(hg_lb.astype(F32), axis=0)
    lb_all = jnp.cumsum(lb_w, axis=0) - lb_w[0:1]
    c_rows = jnp.concatenate([c_ctx[None], c, jnp.zeros((8 - 1 - N_LAT_SEQ, D_MODEL), F32)], axis=0)
    mods = _modulation(c_rows, w_mod, b_mod)
    da_tabs, mla_tabs = _rope_tables()
    xs = (x_prompt.reshape(N_CTX, D_MODEL), x_sample.reshape(N_LAT, D_MODEL))
    new_k, new_v, new_s5, new_hg, new_ckv, new_kr = [], [], [], [], [], []
    s5_tabs = _s5_tables(s5_a_re, s5_a_im, s5_log_dt, s5_b_re, s5_b_im, s5_c_re, s5_c_im)
    w_all = _arrange_w_in(w_in)
    for l in range(DEPTH):
        mod = mods[l, :3].reshape(3, 3, D_MODEL)
        z_a, z_b, z_c, z_d, k_new, v_new, u_bf = _in_proj(xs, mod, w_all, l)

        lam_init = 0.8 - 0.6 * math.exp(-0.3 * l)
        kv_lat = _da_latent_kv(z_a, da_tabs,
                               cache_diff_k[:, l].reshape(N_LAT_SEQ, PAST_LEN, BRANCH),
                               cache_diff_v[:, l].reshape(N_LAT_SEQ, PAST_LEN, BRANCH))
        a_out = _da_attention(z_a, da_lambda[l], da_norm[l], lam_init, da_tabs, kv_lat)
        new_k.append(k_new)
        new_v.append(v_new)

        h0 = state_s5[:, l].transpose(2, 1, 4, 3, 0).reshape(S5_GROUPS, 4, S5_STATE, N_LAT_SEQ)
        h0 = jnp.pad(h0, ((0, 0), (0, 0), (0, 0), (0, 128 - N_LAT_SEQ)))
        y_all, fin = _s5_scan(_s5_chunk_lanes(u_bf), *s5_tabs, h0, l)
        b_out = (_s5_token_rows(y_all), z_b, s5_d[l].reshape(1, BRANCH), s5_w_glu[l].astype(BF16))
        fin = fin[..., :N_CTX_SEQ].reshape(S5_GROUPS, 2, 2, S5_STATE, N_CTX_SEQ)
        new_s5.append(fin.transpose(4, 1, 0, 3, 2))

        lb = jnp.concatenate([lb_all[l, 0].reshape(HG_HEADS, HG_DK), lb_all[l, 1].reshape(HG_HEADS, HG_DK)],
                             axis=-1).reshape(1, HG_W)
        head_eye = jnp.eye(HG_HEADS, dtype=F32)
        s0 = state_hgrn[:, l].transpose(0, 2, 1, 3, 4).reshape(N_LAT_SEQ, HG_HEADS, HG_HEAD_W, 1, HG_DK)
        s0 = (s0 * head_eye[None, :, None, :, None]).reshape(N_LAT_SEQ, HG_HEADS, HG_HEAD_W, BRANCH)
        s_f, s_b = _hg_states(z_c, lb, s0)
        c_out, s_fin = _hg_main(z_c, s_f, s_b, lb, jnp.tile(hg_norm[l].reshape(1, HG_DK), (1, HG_HEADS)))
        new_hg.append(s_fin.reshape(N_CTX_SEQ, HG_HEADS, 2, HG_DK, HG_DK).transpose(0, 2, 1, 3, 4))

        wq, wk, wv, qn = _mla_weights(mla_w_uq[l], mla_w_ukv[l], mla_q_norm[l])
        q, ckv_n, kr = _mla_prep(z_d, mla_tabs, qn, mla_kv_norm[l].reshape(1, MLA_KV_RANK), wq)
        kr_cache = jnp.pad(cache_mla_krope[:, l], ((0, 0), (0, 0), (MLA_NOPE, 128 - MLA_NOPE - MLA_ROPE)))
        k_ctx, v_ctx, k_lat, v_lat = _mla_kv(ckv_n, kr, cache_mla_ckv[:, l], kr_cache, wk, wv)
        d_out = _mla_attention(z_d, q, k_ctx, v_ctx, k_lat, v_lat)
        new_ckv.append(ckv_n[:N_CTX].reshape(N_CTX_SEQ, CTX_LEN, MLA_KV_RANK))
        new_kr.append(kr[:N_CTX, MLA_NOPE:MLA_NOPE + MLA_ROPE].reshape(N_CTX_SEQ, CTX_LEN, MLA_ROPE))

        xs = _out_proj(a_out, b_out, c_out, d_out, xs, mod, w_out, l,
                       final_norm.reshape(1, D_MODEL), final=(l == DEPTH - 1))
        xs = tuple(xs) if l == DEPTH - 1 else (xs,)
    y_prompt = xs[0].reshape(N_CTX_SEQ, CTX_LEN, D_MODEL)
    y_sample = xs[1].reshape(N_LAT_SEQ, LAT_LEN, D_MODEL)
    st = lambda parts: jnp.stack(parts, axis=1)
    heads_last = lambda kv: kv.reshape(N_CTX_SEQ, DEPTH, DA_HEADS, 2 * DA_QK, CTX_LEN).transpose(0, 1, 4, 2, 3)
    return (y_prompt, y_sample, heads_last(st(new_k)), heads_last(st(new_v)), st(new_s5), st(new_hg), st(new_ckv), st(new_kr))
```

```python
import functools
import math

import numpy as np

import jax
import jax.numpy as jnp
from jax import lax
from jax.experimental import pallas as pl
from jax.experimental.pallas import tpu as pltpu

F32 = jnp.float32
BF16 = jnp.bfloat16

D_MODEL = 1024
DEPTH = 2
N_CTX_SEQ = 16
CTX_LEN = 256
N_LAT_SEQ = 2
LAT_LEN = 2048
PAST_LEN = 256
GRID_W = 64
N_CTX = N_CTX_SEQ * CTX_LEN
N_LAT = N_LAT_SEQ * LAT_LEN
N_TOK = N_CTX + N_LAT
BRANCH = 256
EPS = 1e-6
ROPE_BASE = 10000.0
ROW_TILE = 256
LAT_TILES = LAT_LEN // ROW_TILE
N_TILES = N_TOK // ROW_TILE
CTX_TILES = N_CTX // ROW_TILE
VMEM_LIMIT = 48 * 1024 * 1024
IN_PROJ_VMEM_LIMIT = 56 * 1024 * 1024
LAT_Q_TILE = 512
LAT_Q_TILES = LAT_LEN // LAT_Q_TILE
BIG_TILE = 512
CTX_SEQ_PER_STEP = 2

DA_HEADS = 4
DA_QK = 32
MLA_HEADS = 4
MLA_NOPE = 64
MLA_ROPE = 32
MLA_Q_RANK = 192
MLA_KV_RANK = 128
S5_GROUPS = 16
S5_CH = 16
S5_STATE = 64
S5_CHUNK = 16
HG_HEADS = 4
HG_DK = 64

W_A = 1024
W_B = 512
W_C = 1536
W_D = 768
W_ALL = W_A + W_B + W_C + W_D


def _cparams(*sem):
    return pltpu.CompilerParams(dimension_semantics=sem, vmem_limit_bytes=VMEM_LIMIT)


def _tile_seq(i, tile=ROW_TILE):
    return jnp.where(i < N_CTX // tile, 0, 1 + (i - N_CTX // tile) // (LAT_LEN // tile))


def _silu(x):
    return x * (1.0 / (1.0 + jnp.exp(-x)))


def _dot(a, b):
    return jnp.dot(a, b, preferred_element_type=F32)


def _dot_nt(a, b):
    return lax.dot_general(a, b, (((1,), (1,)), ((), ())), preferred_element_type=F32)


def _mod_body(c_ref, w_ref, b_ref, o_ref):
    c = _silu(c_ref[...]).astype(BF16)
    o_ref[0] = _dot(c, w_ref[0].astype(BF16)) + b_ref[0]


def _modulation(c_rows, w_mod, b_mod):
    tn = 768
    return pl.pallas_call(
        _mod_body,
        grid=(DEPTH, 3 * D_MODEL // tn),
        in_specs=[pl.BlockSpec((8, D_MODEL), lambda l, j: (0, 0)),
                  pl.BlockSpec((1, D_MODEL, tn), lambda l, j: (l, 0, j)),
                  pl.BlockSpec((1, 1, tn), lambda l, j: (l, 0, j))],
        out_specs=pl.BlockSpec((1, 8, tn), lambda l, j: (l, 0, j)),
        out_shape=jax.ShapeDtypeStruct((DEPTH, 8, 3 * D_MODEL), F32),
        compiler_params=_cparams("parallel", "parallel"),
        name="modulation",
    )(c_rows, w_mod, b_mod.reshape(DEPTH, 1, 3 * D_MODEL))


def _split_rows(i, ctx_ref, lat_ref):
    return jnp.where(i < N_CTX // ctx_ref.shape[0], ctx_ref[...], lat_ref[...])


def _ctx_tile_spec(w, tile=ROW_TILE):
    return pl.BlockSpec((tile, w), lambda i: (jnp.minimum(i, N_CTX // tile - 1), 0))


def _lat_tile_spec(w, tile=ROW_TILE):
    return pl.BlockSpec((tile, w), lambda i: (jnp.maximum(i - N_CTX // tile, 0), 0))


def _in_proj_body(*refs, split):
    if split:
        xc_ref, xl_ref, mod_ref, w_ref, oa, ob, oc, od, ok, ov, ou = refs
        x = _split_rows(pl.program_id(0), xc_ref, xl_ref)
    else:
        x_ref, mod_ref, w_ref, oa, ob, oc, od, ok, ov, ou = refs
        x = x_ref[...]
    xn = x * lax.rsqrt(jnp.mean(x * x, axis=-1, keepdims=True) + EPS)
    mod = mod_ref[0]
    h = (xn * (1.0 + mod[1:2]) + mod[0:1]).astype(BF16)
    off = 0
    for o in (oa, ob, oc, od):
        w = o.shape[-1]
        o[...] = _dot(h, w_ref[0, :, off:off + w])
        off += w
    ou[...] = ob[:, :BRANCH].astype(BF16)

    @pl.when(pl.program_id(0) < N_CTX // BIG_TILE)
    def _():
        for t in range(BIG_TILE // CTX_LEN):
            r = slice(t * CTX_LEN, (t + 1) * CTX_LEN)
            ok[t] = oa[r, BRANCH:2 * BRANCH].T
            ov[t] = oa[r, 2 * BRANCH:3 * BRANCH].T


def _in_proj(xs, mod, w_all, l):
    widths = (W_A, W_B, W_C, W_D)
    split = len(xs) == 2
    x_specs = ([_ctx_tile_spec(D_MODEL, BIG_TILE), _lat_tile_spec(D_MODEL, BIG_TILE)] if split
               else [pl.BlockSpec((BIG_TILE, D_MODEL), lambda i: (i, 0))])
    return pl.pallas_call(
        functools.partial(_in_proj_body, split=split),
        grid=(N_TOK // BIG_TILE,),
        in_specs=x_specs + [pl.BlockSpec((1, 3, D_MODEL), lambda i: (_tile_seq(i, BIG_TILE), 0, 0)),
                            pl.BlockSpec((1, D_MODEL, W_ALL), lambda i: (l, 0, 0))],
        out_specs=[pl.BlockSpec((BIG_TILE, w), lambda i: (i, 0)) for w in widths]
        + [pl.BlockSpec((BIG_TILE // CTX_LEN, BRANCH, CTX_LEN), lambda i: (jnp.minimum(i, N_CTX // BIG_TILE - 1), 0, 0))] * 2
        + [pl.BlockSpec((BIG_TILE, BRANCH), lambda i: (i, 0))],
        out_shape=[jax.ShapeDtypeStruct((N_TOK, w), F32) for w in widths]
        + [jax.ShapeDtypeStruct((N_CTX_SEQ, BRANCH, CTX_LEN), F32)] * 2 + [jax.ShapeDtypeStruct((N_TOK, BRANCH), BF16)],
        compiler_params=pltpu.CompilerParams(dimension_semantics=("arbitrary",), vmem_limit_bytes=IN_PROJ_VMEM_LIMIT),
        name="in_proj",
    )(*xs, mod, w_all)


def _arrange_body(wt_ref, o_ref):
    w = wt_ref[0]
    c = _IN_OFF
    rows = lambda name: w[c[name][0]:c[name][-1] + 1]
    zero = lambda n: jnp.zeros((n, w.shape[1]), F32)

    def per_head(x, y):
        xs, ys = rows(x), rows(y)
        return [p for h in range(HG_HEADS) for p in (xs[h * HG_DK:(h + 1) * HG_DK], ys[h * HG_DK:(h + 1) * HG_DK])]
    pieces = ([w[0:W_A + W_B]] + per_head("hg_q", "hg_q") + per_head("hg_ff", "hg_fb") + [rows("hg_i"), rows("hg_g")]
              + [rows("mla_cq"), zero(256 - MLA_Q_RANK), rows("mla_ckv"), zero(MLA_NOPE), rows("mla_kr"),
                 zero(128 - MLA_NOPE - MLA_ROPE), rows("mla_g")])
    o_ref[0] = jnp.concatenate(pieces, axis=0).T.astype(BF16)


def _arrange_w_in(w_in):
    lanes = 256
    wt = jnp.swapaxes(w_in, 1, 2)
    return pl.pallas_call(
        _arrange_body,
        grid=(DEPTH, D_MODEL // lanes),
        in_specs=[pl.BlockSpec((1, wt.shape[1], lanes), lambda l, i: (l, 0, i))],
        out_specs=pl.BlockSpec((1, lanes, W_ALL), lambda l, i: (l, i, 0)),
        out_shape=jax.ShapeDtypeStruct((DEPTH, D_MODEL, W_ALL), BF16),
        compiler_params=_cparams("parallel", "parallel"),
        name="arrange_w_in",
    )(wt)


def _out_proj_body(*refs, split_in, final):
    ac_ref, al_ref, y_ref, u_ref, g_ref, dsk_ref, wglu_ref, c_ref, dc_ref, dl_ref = refs[:10]
    i = pl.program_id(0)
    if split_in:
        xc_ref, xl_ref, mod_ref, w_ref, fn_ref = refs[10:15]
        x = _split_rows(i, xc_ref, xl_ref)
    else:
        x_ref, mod_ref, w_ref, fn_ref = refs[10:14]
        x = x_ref[...]
    b_out = _s5_gated(y_ref[...], u_ref[...], g_ref[...], dsk_ref[...], wglu_ref[...])
    branches = (_split_rows(i, ac_ref, al_ref), b_out, c_ref[...], _split_rows(i, dc_ref, dl_ref))
    acc = None
    for j, r in enumerate(branches):
        t = _dot(r.astype(BF16), w_ref[0, j * BRANCH:(j + 1) * BRANCH, :].astype(BF16))
        acc = t if acc is None else acc + t
    x = x + mod_ref[0][2:3] * acc
    if not final:
        refs[-1][...] = x
        return
    y = x * lax.rsqrt(jnp.mean(x * x, axis=-1, keepdims=True) + EPS) * fn_ref[...]
    yc_ref, yl_ref = refs[-2:]

    @pl.when(i < N_CTX // BIG_TILE)
    def _():
        yc_ref[...] = y

    @pl.when(i >= N_CTX // BIG_TILE)
    def _():
        yl_ref[...] = y


def _out_proj(a, s5, c, d, xs, mod, w_out, l, final_norm, final):
    y_ssm, z_b, d_skip, w_glu = s5
    br = pl.BlockSpec((BIG_TILE, BRANCH), lambda i: (i, 0))
    s5_specs = [br, br, pl.BlockSpec((BIG_TILE, BRANCH), lambda i: (i, 1)),
                pl.BlockSpec((1, BRANCH), lambda i: (0, 0)), pl.BlockSpec((BRANCH, 2 * BRANCH), lambda i: (0, 0))]
    pair = [_ctx_tile_spec(BRANCH, BIG_TILE), _lat_tile_spec(BRANCH, BIG_TILE)]
    split_in = len(xs) == 2
    x_specs = ([_ctx_tile_spec(D_MODEL, BIG_TILE), _lat_tile_spec(D_MODEL, BIG_TILE)] if split_in
               else [pl.BlockSpec((BIG_TILE, D_MODEL), lambda i: (i, 0))])
    if final:
        out_specs = [_ctx_tile_spec(D_MODEL, BIG_TILE), _lat_tile_spec(D_MODEL, BIG_TILE)]
        out_shape = [jax.ShapeDtypeStruct((N_CTX, D_MODEL), F32), jax.ShapeDtypeStruct((N_LAT, D_MODEL), F32)]
    else:
        out_specs = pl.BlockSpec((BIG_TILE, D_MODEL), lambda i: (i, 0))
        out_shape = jax.ShapeDtypeStruct((N_TOK, D_MODEL), F32)
    return pl.pallas_call(
        functools.partial(_out_proj_body, split_in=split_in, final=final),
        grid=(N_TOK // BIG_TILE,),
        in_specs=pair + s5_specs + [br] + pair + x_specs + [
            pl.BlockSpec((1, 3, D_MODEL), lambda i: (_tile_seq(i, BIG_TILE), 0, 0)),
            pl.BlockSpec((1, D_MODEL, D_MODEL), lambda i: (l, 0, 0)),
            pl.BlockSpec((1, D_MODEL), lambda i: (0, 0))],
        out_specs=out_specs,
        out_shape=out_shape,
        compiler_params=_cparams("arbitrary"),
        name="out_proj",
    )(*a, y_ssm, z_b, z_b, d_skip, w_glu, c, *d, *xs, mod, w_out, final_norm)


LOG2E = 1.4426950408889634


def _exp2_rows(s):
    e = jnp.exp2(s - jnp.max(s, axis=-1, keepdims=True))
    return e, jnp.sum(e, axis=-1, keepdims=True)


def _rope(x, cos, sin_lo, sin_hi):
    w = x.shape[-1]
    return x * cos + pltpu.roll(x, w - 8, 1) * sin_lo + pltpu.roll(x, 8, 1) * sin_hi


def _da_kv_body(k_ref, v_ref, cos_ref, slo_ref, shi_ref, ck_ref, cv_ref, ko_ref, vo_ref):
    j = pl.program_id(1)

    @pl.when(j < LAT_TILES)
    def _():
        ko_ref[0] = _rope(k_ref[...], cos_ref[...], slo_ref[...], shi_ref[...]).astype(BF16)
        vo_ref[0] = v_ref[...].astype(BF16)

    @pl.when(j == LAT_TILES)
    def _():
        ko_ref[0] = ck_ref[0].astype(BF16)
        vo_ref[0] = cv_ref[0].astype(BF16)


def _da_latent_kv(z_a, tabs, cache_k, cache_v):
    def rows(col):
        return pl.BlockSpec(
            (ROW_TILE, BRANCH),
            lambda b, j: (CTX_TILES + b * LAT_TILES + jnp.minimum(j, LAT_TILES - 1), col))
    tab = pl.BlockSpec((ROW_TILE, BRANCH), lambda b, j: (jnp.minimum(j, LAT_TILES - 1), 0))
    cache = pl.BlockSpec((1, PAST_LEN, BRANCH), lambda b, j: (b, 0, 0))
    out = pl.BlockSpec((1, ROW_TILE, BRANCH), lambda b, j: (b, j, 0))
    shp = jax.ShapeDtypeStruct((N_LAT_SEQ, LAT_LEN + PAST_LEN, BRANCH), BF16)
    return pl.pallas_call(
        _da_kv_body,
        grid=(N_LAT_SEQ, LAT_TILES + 1),
        in_specs=[rows(1), rows(2), tab, tab, tab, cache, cache],
        out_specs=[out, out],
        out_shape=[shp, shp],
        compiler_params=_cparams("parallel", "parallel"),
        name="da_kv",
    )(z_a, z_a, *tabs, cache_k, cache_v)


def _da_attn_body(lam_ref, ng_ref, q_ref, *rest, rope, lam_init):
    if rope:
        cos_ref, slo_ref, shi_ref, k_ref, v_ref, g_ref, o_ref = rest
        q = _rope(q_ref[...], cos_ref[...], slo_ref[...], shi_ref[...])
        o_ref[...] = _da_attn_tile(lam_ref, ng_ref, q, k_ref[0], v_ref[0], g_ref[...], lam_init, True)
    else:
        k_ref, v_ref, g_ref, o_ref = rest
        for t in range(q_ref.shape[0] // CTX_LEN):
            r = slice(t * CTX_LEN, (t + 1) * CTX_LEN)
            o_ref[r, :] = _da_attn_tile(lam_ref, ng_ref, q_ref[r, :], k_ref[r, :].astype(BF16),
                                        v_ref[r, :].astype(BF16), g_ref[r, :], lam_init, False)


def _da_attn_tile(lam_ref, ng_ref, q, k, v, g, lam_init, scale_after):
    q = q * (DA_QK ** -0.5 * LOG2E)
    lv = lam_ref[...]
    lam = (jnp.exp(jnp.sum(lv[0:1] * lv[1:2], axis=-1, keepdims=True))
           - jnp.exp(jnp.sum(lv[2:3] * lv[3:4], axis=-1, keepdims=True)) + lam_init)
    lane = lax.broadcasted_iota(jnp.int32, (1, BRANCH), 1)
    acc = jnp.zeros(q.shape, F32)
    for h in range(DA_HEADS):
        q1 = jnp.where(lane // DA_QK == 2 * h, q, 0.0).astype(BF16)
        q2 = jnp.where(lane // DA_QK == 2 * h + 1, q, 0.0).astype(BF16)
        e1, l1 = _exp2_rows(_dot_nt(q1, k))
        e2, l2 = _exp2_rows(_dot_nt(q2, k))
        if scale_after:
            a = (e1 - (lam * l1 / l2) * e2).astype(BF16)
            pv = _dot(a, v) * (1.0 / l1)
        else:
            pv = _dot((e1 * (1.0 / l1) - lam * (e2 * (1.0 / l2))).astype(BF16), v)
        acc = jnp.where(lane // (2 * DA_QK) == h, pv, acc)
    sq = acc * acc
    ms = jnp.zeros(q.shape, F32)
    for h in range(DA_HEADS):
        hm = lane // (2 * DA_QK) == h
        ms = jnp.where(hm, jnp.sum(jnp.where(hm, sq, 0.0), axis=-1, keepdims=True), ms)
    o = acc * lax.rsqrt(ms * (1.0 / (2 * DA_QK)) + EPS) * (ng_ref[...] * (1.0 - lam_init))
    return o * _silu(g)


def _da_attention(z_a, lam_vec, norm_g, lam_init, tabs, kv_lat):
    ng = jnp.tile(norm_g.reshape(1, 2 * DA_QK), (1, DA_HEADS))
    small = [pl.BlockSpec((4, DA_QK), lambda *_: (0, 0)), pl.BlockSpec((1, BRANCH), lambda *_: (0, 0))]

    rows = CTX_SEQ_PER_STEP * CTX_LEN

    def col(c):
        return pl.BlockSpec((rows, BRANCH), lambda i: (i, c))
    ctx = pl.pallas_call(
        functools.partial(_da_attn_body, rope=False, lam_init=lam_init),
        grid=(N_CTX // rows,),
        in_specs=small + [col(0), col(1), col(2), col(3)],
        out_specs=pl.BlockSpec((rows, BRANCH), lambda i: (i, 0)),
        out_shape=jax.ShapeDtypeStruct((N_CTX, BRANCH), F32),
        compiler_params=_cparams("parallel"),
        name="da_attn_ctx",
    )(lam_vec, ng, z_a, z_a, z_a, z_a)

    def lcol(c):
        return pl.BlockSpec((LAT_Q_TILE, BRANCH), lambda b, j: (N_CTX // LAT_Q_TILE + b * LAT_Q_TILES + j, c))
    tab = pl.BlockSpec((LAT_Q_TILE, BRANCH), lambda b, j: (j, 0))
    kvs = pl.BlockSpec((1, LAT_LEN + PAST_LEN, BRANCH), lambda b, j: (b, 0, 0))
    lat = pl.pallas_call(
        functools.partial(_da_attn_body, rope=True, lam_init=lam_init),
        grid=(N_LAT_SEQ, LAT_Q_TILES),
        in_specs=small + [lcol(0), tab, tab, tab, kvs, kvs, lcol(3)],
        out_specs=pl.BlockSpec((LAT_Q_TILE, BRANCH), lambda b, j: (b * LAT_Q_TILES + j, 0)),
        out_shape=jax.ShapeDtypeStruct((N_LAT, BRANCH), F32),
        compiler_params=_cparams("parallel", "parallel"),
        name="da_attn_lat",
    )(lam_vec, ng, z_a, *tabs, kv_lat[0], kv_lat[1], z_a)
    return ctx, lat


MLA_HEAD_PAD = 128
MLA_QW = MLA_HEADS * MLA_HEAD_PAD


def _mla_prep_body(cq_ref, ckv_ref, kr_ref, ck_t, sk_lo, sk_hi, qn_ref, kvn_ref, wq_ref, q_out, ckv_out, kr_out):
    cq = cq_ref[...]
    ms = jnp.sum(cq * cq, axis=-1, keepdims=True) * (1.0 / MLA_Q_RANK)
    qn = (cq * lax.rsqrt(ms + EPS) * qn_ref[...]).astype(BF16)
    heads = lambda t: jnp.concatenate([t[...]] * MLA_HEADS, axis=-1)
    q = _rope(_dot(qn, wq_ref[...]), heads(ck_t), heads(sk_lo), heads(sk_hi))
    q_out[...] = (q * ((MLA_NOPE + MLA_ROPE) ** -0.5 * LOG2E)).astype(BF16)
    ckv = ckv_ref[...]
    ckv_out[...] = ckv * lax.rsqrt(jnp.mean(ckv * ckv, axis=-1, keepdims=True) + EPS) * kvn_ref[...]
    kr_out[...] = _rope(kr_ref[...], ck_t[...], sk_lo[...], sk_hi[...])


def _mla_prep(z_d, tabs, q_norm_pad, kv_norm, wq):
    ctx_tiles, lat_tiles = N_CTX // BIG_TILE, LAT_LEN // BIG_TILE

    def tab(w):
        return pl.BlockSpec((BIG_TILE, w), lambda i: (jnp.where(i < ctx_tiles, lat_tiles, (i - ctx_tiles) % lat_tiles), 0))

    def col(w, c):
        return pl.BlockSpec((BIG_TILE, w), lambda i: (i, c))

    def const(shape):
        return pl.BlockSpec(shape, lambda i: (0, 0))
    return pl.pallas_call(
        _mla_prep_body,
        grid=(N_TOK // BIG_TILE,),
        in_specs=[col(256, 0), col(128, 2), col(128, 3),
                  tab(128), tab(128), tab(128),
                  const((1, 256)), const((1, 128)), const((256, MLA_QW))],
        out_specs=[col(MLA_QW, 0), col(128, 0), col(128, 0)],
        out_shape=[jax.ShapeDtypeStruct((N_TOK, MLA_QW), BF16),
                   jax.ShapeDtypeStruct((N_TOK, 128), F32),
                   jax.ShapeDtypeStruct((N_TOK, 128), F32)],
        compiler_params=_cparams("parallel"),
        name="mla_prep",
    )(z_d, z_d, z_d, *tabs, q_norm_pad, kv_norm, wq)


def _mla_kv_math(ckv, kr, wk_ref, wv_ref, k_out, v_out):
    c = ckv.astype(BF16)
    k_out[...] = (_dot(c, wk_ref[...]) + jnp.concatenate([kr] * MLA_HEADS, axis=-1)).astype(BF16).reshape(k_out.shape)
    v_out[...] = _dot(c, wv_ref[...]).astype(BF16).reshape(v_out.shape)


def _mla_kv_ctx_body(ckv_ref, kr_ref, wk_ref, wv_ref, k_out, v_out):
    _mla_kv_math(ckv_ref[...], kr_ref[...], wk_ref, wv_ref, k_out, v_out)


def _mla_kv_lat_body(ckv_ref, kr_ref, cckv_ref, ckr_ref, wk_ref, wv_ref, k_out, v_out):
    j = pl.program_id(1)

    @pl.when(j < LAT_TILES)
    def _():
        _mla_kv_math(ckv_ref[...], kr_ref[...], wk_ref, wv_ref, k_out, v_out)

    @pl.when(j == LAT_TILES)
    def _():
        _mla_kv_math(cckv_ref[0], ckr_ref[0], wk_ref, wv_ref, k_out, v_out)


def _mla_kv(ckv, kr, cache_ckv, cache_kr, wk, wv):
    weights = [pl.BlockSpec((128, MLA_QW), lambda *_: (0, 0)), pl.BlockSpec((128, BRANCH), lambda *_: (0, 0))]
    k_ctx, v_ctx = pl.pallas_call(
        _mla_kv_ctx_body,
        grid=(N_CTX // BIG_TILE,),
        in_specs=[pl.BlockSpec((BIG_TILE, 128), lambda i: (i, 0)), pl.BlockSpec((BIG_TILE, 128), lambda i: (i, 0))] + weights,
        out_specs=[pl.BlockSpec((BIG_TILE, MLA_QW), lambda i: (i, 0)),
                   pl.BlockSpec((BIG_TILE, BRANCH), lambda i: (i, 0))],
        out_shape=[jax.ShapeDtypeStruct((N_CTX, MLA_QW), BF16), jax.ShapeDtypeStruct((N_CTX, BRANCH), BF16)],
        compiler_params=_cparams("parallel"),
        name="mla_kv_ctx",
    )(ckv, kr, wk, wv)
    rows = pl.BlockSpec((ROW_TILE, 128), lambda b, j: (CTX_TILES + b * LAT_TILES + jnp.minimum(j, LAT_TILES - 1), 0))
    cache = pl.BlockSpec((1, PAST_LEN, 128), lambda b, j: (b, 0, 0))
    lk = LAT_LEN + PAST_LEN
    k_lat, v_lat = pl.pallas_call(
        _mla_kv_lat_body,
        grid=(N_LAT_SEQ, LAT_TILES + 1),
        in_specs=[rows, rows, cache, cache] + weights,
        out_specs=[pl.BlockSpec((1, ROW_TILE, MLA_QW), lambda b, j: (b, j, 0)),
                   pl.BlockSpec((1, ROW_TILE, BRANCH), lambda b, j: (b, j, 0))],
        out_shape=[jax.ShapeDtypeStruct((N_LAT_SEQ, lk, MLA_QW), BF16), jax.ShapeDtypeStruct((N_LAT_SEQ, lk, BRANCH), BF16)],
        compiler_params=_cparams("parallel", "parallel"),
        name="mla_kv_lat",
    )(ckv, kr, cache_ckv, cache_kr, wk, wv)
    return k_ctx, v_ctx, k_lat, v_lat


def _mla_attn_body(q_ref, k_ref, v_ref, g_ref, o_ref, *, ctx):
    if ctx:
        for t in range(q_ref.shape[0] // CTX_LEN):
            r = slice(t * CTX_LEN, (t + 1) * CTX_LEN)
            o_ref[r, :] = _mla_attn_tile(q_ref[r, :], k_ref[r, :], v_ref[r, :], g_ref[r, :])
    else:
        o_ref[...] = _mla_attn_tile(q_ref[...], k_ref[0], v_ref[0], g_ref[...])


def _mla_attn_tile(q, k, v, g):
    lane = lax.broadcasted_iota(jnp.int32, (1, BRANCH), 1)
    acc = jnp.zeros((q.shape[0], BRANCH), F32)
    for h in range(MLA_HEADS):
        sl = slice(h * MLA_HEAD_PAD, (h + 1) * MLA_HEAD_PAD)
        e, l = _exp2_rows(_dot_nt(q[:, sl], k[:, sl]))
        acc = jnp.where(lane // 64 == h, _dot(e.astype(BF16), v) * (1.0 / l), acc)
    return acc * _silu(g)


def _mla_attention(z_d, q, k_ctx, v_ctx, k_lat, v_lat):
    rows = CTX_SEQ_PER_STEP * CTX_LEN
    ctx = pl.pallas_call(
        functools.partial(_mla_attn_body, ctx=True),
        grid=(N_CTX // rows,),
        in_specs=[pl.BlockSpec((rows, MLA_QW), lambda i: (i, 0)),
                  pl.BlockSpec((rows, MLA_QW), lambda i: (i, 0)),
                  pl.BlockSpec((rows, BRANCH), lambda i: (i, 0)),
                  pl.BlockSpec((rows, BRANCH), lambda i: (i, 2))],
        out_specs=pl.BlockSpec((rows, BRANCH), lambda i: (i, 0)),
        out_shape=jax.ShapeDtypeStruct((N_CTX, BRANCH), F32),
        compiler_params=_cparams("parallel"),
        name="mla_attn_ctx",
    )(q, k_ctx, v_ctx, z_d)
    lk = LAT_LEN + PAST_LEN
    lat = pl.pallas_call(
        functools.partial(_mla_attn_body, ctx=False),
        grid=(N_LAT_SEQ, LAT_Q_TILES),
        in_specs=[pl.BlockSpec((LAT_Q_TILE, MLA_QW), lambda b, j: (N_CTX // LAT_Q_TILE + b * LAT_Q_TILES + j, 0)),
                  pl.BlockSpec((1, lk, MLA_QW), lambda b, j: (b, 0, 0)),
                  pl.BlockSpec((1, lk, BRANCH), lambda b, j: (b, 0, 0)),
                  pl.BlockSpec((LAT_Q_TILE, BRANCH), lambda b, j: (N_CTX // LAT_Q_TILE + b * LAT_Q_TILES + j, 2))],
        out_specs=pl.BlockSpec((LAT_Q_TILE, BRANCH), lambda b, j: (b * LAT_Q_TILES + j, 0)),
        out_shape=jax.ShapeDtypeStruct((N_LAT, BRANCH), F32),
        compiler_params=_cparams("parallel", "parallel"),
        name="mla_attn_lat",
    )(q, k_lat, v_lat, z_d)
    return ctx, lat


S5_TAP = S5_CHUNK * S5_CH
S5_NCHUNK = N_TOK // S5_CHUNK
S5_CTX_CH = N_CTX // S5_CHUNK
S5_CTX_SEQ_CH = CTX_LEN // S5_CHUNK
S5_LAT_SEQ_CH = LAT_LEN // S5_CHUNK
S5_SCAN_STEPS = S5_LAT_SEQ_CH.bit_length() - 1
S5_TABLE_GROUPS = 8


def _s5_body(x_ref, mt_ref, bst_ref, cot_ref, a_ref, h0_ref, y_ref, fin_ref):
    x = x_ref[...].reshape(S5_TAP, S5_NCHUNK)
    y = _dot(mt_ref[0, 0], x)
    s = _dot(bst_ref[0, 0], x)
    lane = lax.broadcasted_iota(jnp.int32, (1, S5_NCHUNK), 1)
    is_lat = lane >= S5_CTX_CH
    pos_f = jnp.where(is_lat, (lane - S5_CTX_CH) & (S5_LAT_SEQ_CH - 1), lane & (S5_CTX_SEQ_CH - 1))
    pos_b = jnp.where(is_lat, S5_LAT_SEQ_CH - 1, S5_CTX_SEQ_CH - 1) - pos_f
    hin = []
    for d in range(2):
        n = S5_STATE
        sre, sim = s[2 * d * n:(2 * d + 1) * n], s[(2 * d + 1) * n:(2 * d + 2) * n]
        are = jnp.concatenate([a_ref[0, 0, 2 * d]] * (S5_NCHUNK // 128), axis=-1)
        aim = jnp.concatenate([a_ref[0, 0, 2 * d + 1]] * (S5_NCHUNK // 128), axis=-1)
        pos = pos_f if d == 0 else pos_b
        h0r, h0i = jnp.zeros_like(sre), jnp.zeros_like(sre)
        for b in range(N_LAT_SEQ):
            first = S5_CTX_CH + b * S5_LAT_SEQ_CH + (0 if d == 0 else S5_LAT_SEQ_CH - 1)
            h0r = jnp.where(lane == first, h0_ref[0, 2 * d][:, b:b + 1], h0r)
            h0i = jnp.where(lane == first, h0_ref[0, 2 * d + 1][:, b:b + 1], h0i)
        xr = sre + are * h0r - aim * h0i
        xi = sim + are * h0i + aim * h0r
        pr, pi = are, aim
        for j in range(S5_SCAN_STEPS):
            sh = 1 << j
            shift = sh if d == 0 else S5_NCHUNK - sh
            rr, ri = pltpu.roll(xr, shift, 1), pltpu.roll(xi, shift, 1)
            ok = pos >= sh
            xr, xi = (xr + jnp.where(ok, pr * rr - pi * ri, 0.0), xi + jnp.where(ok, pr * ri + pi * rr, 0.0))
            pr, pi = pr * pr - pi * pi, 2.0 * pr * pi
        last = lax.broadcasted_iota(jnp.int32, (1, 128), 1) * S5_CTX_SEQ_CH + (S5_CTX_SEQ_CH - 1 if d == 0 else 0)
        pick = jnp.where(lax.broadcasted_iota(jnp.int32, (S5_CTX_CH, 1), 0) == last, 1.0, 0.0)
        fin_ref[0, 2 * d] = _dot_sel(xr[:, :S5_CTX_CH], pick)
        fin_ref[0, 2 * d + 1] = _dot_sel(xi[:, :S5_CTX_CH], pick)
        one = 1 if d == 0 else S5_NCHUNK - 1
        hin.append(jnp.where(pos >= 1, pltpu.roll(xr, one, 1), h0r))
        hin.append(jnp.where(pos >= 1, pltpu.roll(xi, one, 1), h0i))
    y = y + _dot(cot_ref[0, 0], jnp.concatenate(hin, axis=0).astype(BF16))
    y_ref[...] = y.reshape(S5_CHUNK, S5_CH, S5_NCHUNK)


def _s5_scan(x_all, mt, bst, cot, a16, h0, l):
    g = S5_GROUPS
    sq = pl.BlockSpec((1, 1, S5_TAP, S5_TAP), lambda i: (l, i, 0, 0))
    st = pl.BlockSpec((1, 4, S5_STATE, 128), lambda i: (i, 0, 0, 0))
    return pl.pallas_call(
        _s5_body,
        grid=(g,),
        in_specs=[pl.BlockSpec((S5_CHUNK, S5_CH, S5_NCHUNK), lambda i: (0, i, 0)), sq, sq, sq,
                  pl.BlockSpec((1, 1, 4, S5_STATE, 128), lambda i: (l, i, 0, 0, 0)), st],
        out_specs=[pl.BlockSpec((S5_CHUNK, S5_CH, S5_NCHUNK), lambda i: (0, i, 0)),
                   pl.BlockSpec((1, 4, S5_STATE, 128), lambda i: (i, 0, 0, 0))],
        out_shape=[jax.ShapeDtypeStruct((S5_CHUNK, BRANCH, S5_NCHUNK), F32),
                   jax.ShapeDtypeStruct((g, 4, S5_STATE, 128), F32)],
        compiler_params=_cparams("parallel"),
        name="s5_scan",
    )(x_all, mt, bst, cot, a16, h0)


def _s5_gated(y_ssm, u, g, d_skip, w_glu):
    y = u * d_skip + y_ssm
    ge = 0.5 * y * (1.0 + jnp.tanh(0.7978845608028654 * (y + 0.044715 * (y * y * y))))
    gl = _dot(ge.astype(BF16), w_glu)
    return gl[:, :BRANCH] * (1.0 / (1.0 + jnp.exp(-gl[:, BRANCH:]))) * _silu(g)


HG_CHUNK = ROW_TILE
HG_W = 2 * HG_HEADS * HG_DK
HG_HEAD_W = 2 * HG_DK
HG_LAT_CHUNKS = LAT_LEN // HG_CHUNK
HG_CHUNKS = N_TOK // HG_CHUNK


def _hg_gates(z, lb):
    e = jnp.exp(-jnp.abs(z))
    r = 1.0 / (1.0 + e)
    sig_pos = jnp.where(z >= 0, r, e * r)
    sig_neg = jnp.where(z >= 0, e * r, r)
    return lb + (1.0 - lb) * sig_pos, (1.0 - lb) * sig_neg


def _bcast_row(x, period, r):
    n, w = x.shape
    if period >= 8:
        x3 = x.reshape(n // period, period, w)
        return jnp.broadcast_to(x3[:, r:r + 1, :], x3.shape).reshape(n, w)
    x3 = x.reshape(n // 8, 8, w)
    sub = lax.broadcasted_iota(jnp.int32, (1, 8, 1), 1)
    out = None
    for j in range(8 // period):
        b = jnp.broadcast_to(x3[:, j * period + r:j * period + r + 1, :], x3.shape)
        out = b if out is None else jnp.where(sub >= j * period, b, out)
    return out.reshape(n, w)


def _hg_scans(f, isb):
    n = f.shape[0]
    row = lax.broadcasted_iota(jnp.int32, (n, 1), 0)
    p, r = f, jnp.ones_like(f)
    levels = []
    h, sh = 1, 0
    while h < n:
        levels.append((h, sh, p, r))
        up = (row >> sh) & 1
        tot_p = jnp.where(isb == 1, _bcast_row(p, 2 * h, h), _bcast_row(p, 2 * h, h - 1))
        tot_r = jnp.where(isb == 1, _bcast_row(p, 2 * h, 0), _bcast_row(p, 2 * h, 2 * h - 1))
        p = p * jnp.where(up != isb, tot_p, 1.0)
        r = r * jnp.where(up == isb, tot_r, 1.0)
        h, sh = 2 * h, sh + 1
    return levels, p, r


def _hg_state_body(zf_ref, zb_ref, vf_ref, vb_ref, lb_ref, s0_ref, sf_out, sb_out, s_scr):
    i = pl.program_id(0)

    @pl.when(i % HG_LAT_CHUNKS == 0)
    def _():
        s_scr[...] = s0_ref[0]

    sf_out[0] = s_scr[:, 0:HG_DK, :]
    sb_out[0] = s_scr[:, HG_DK:, :]
    lane5 = lax.broadcasted_iota(jnp.int32, (1, HG_W), 1)
    isb = (lane5 >> 6) & 1
    z = jnp.where(isb == 1, zb_ref[...], zf_ref[...])
    f, k = _hg_gates(z, lb_ref[...])
    r, ptot = _hg_chunk_decay(f, isb)
    kt = k * r
    lane = lax.broadcasted_iota(jnp.int32, (1, BRANCH), 1)
    vf = vf_ref[...]
    vb = vb_ref[...]
    for hd in range(HG_HEADS):
        sl = slice(hd * HG_HEAD_W, (hd + 1) * HG_HEAD_W)
        kth = kt[:, sl].T.astype(BF16)
        hm = (lane >> 6) == hd
        d_f = _dot(kth, jnp.where(hm, vf, 0.0).astype(BF16))
        d_b = _dot(kth, jnp.where(hm, vb, 0.0).astype(BF16))
        ds = jnp.concatenate([d_f[:HG_DK], d_b[HG_DK:]], axis=0)
        pcol = jnp.broadcast_to(ptot[:, sl], (HG_HEAD_W, HG_HEAD_W)).T[:, 0:1]
        s_scr[hd] = s_scr[hd] * pcol + ds


def _hg_chunk_decay(f, isb):
    n = f.shape[0]
    row = lax.broadcasted_iota(jnp.int32, (n, 1), 0)
    dist = jnp.where(isb == 1, row, n - 1 - row)
    x = f
    sh = 1
    while sh < n:
        src = jnp.where(isb == 1, pltpu.roll(x, sh, 0), pltpu.roll(x, n - sh, 0))
        x = x * jnp.where(dist >= sh, src, 1.0)
        sh *= 2
    total = jnp.where(isb == 1, x[n - 1:n], x[0:1])
    nxt = jnp.where(isb == 1, pltpu.roll(x, 1, 0), pltpu.roll(x, n - 1, 0))
    return jnp.where(dist >= 1, nxt, 1.0), total


HG_LAT_STEPS = N_LAT_SEQ * HG_LAT_CHUNKS


def _hg_lat_rev(i):
    return (i // HG_LAT_CHUNKS) * HG_LAT_CHUNKS + (HG_LAT_CHUNKS - 1 - i % HG_LAT_CHUNKS)


def _hg_states(z_c, lb, s0):
    first = N_CTX_SEQ
    zz_f = pl.BlockSpec((HG_CHUNK, HG_W), lambda i: (first + i, 1))
    zz_b = pl.BlockSpec((HG_CHUNK, HG_W), lambda i: (first + _hg_lat_rev(i), 1))
    v_f = pl.BlockSpec((HG_CHUNK, BRANCH), lambda i: (first + i, 4))
    v_b = pl.BlockSpec((HG_CHUNK, BRANCH), lambda i: (first + _hg_lat_rev(i), 4))
    st = (HG_HEADS, HG_HEAD_W, BRANCH)
    half = (HG_HEADS, HG_DK, BRANCH)
    return pl.pallas_call(
        _hg_state_body,
        grid=(HG_LAT_STEPS,),
        in_specs=[zz_f, zz_b, v_f, v_b,
                  pl.BlockSpec((1, HG_W), lambda i: (0, 0)),
                  pl.BlockSpec((1,) + st, lambda i: (i // HG_LAT_CHUNKS, 0, 0, 0))],
        out_specs=[pl.BlockSpec((1,) + half, lambda i: (i, 0, 0, 0)),
                   pl.BlockSpec((1,) + half, lambda i: (_hg_lat_rev(i), 0, 0, 0))],
        out_shape=[jax.ShapeDtypeStruct((HG_LAT_STEPS,) + half, F32),
                   jax.ShapeDtypeStruct((HG_LAT_STEPS,) + half, F32)],
        scratch_shapes=[pltpu.VMEM(st, F32)],
        compiler_params=_cparams("arbitrary"),
        name="hg_states",
    )(z_c, z_c, z_c, z_c, lb, s0)


def _hg_main_body(qq_ref, zz_ref, v_ref, g_ref, sf_ref, sb_ref, lb_ref, ng_ref, o_ref, fin_ref):
    n = HG_CHUNK
    i = pl.program_id(0)
    qq = qq_ref[...]
    lane5 = lax.broadcasted_iota(jnp.int32, (1, HG_W), 1)
    isb = (lane5 >> 6) & 1
    f, k = _hg_gates(zz_ref[...], lb_ref[...])
    levels, pfull, rfull = _hg_scans(f, isb)
    row = lax.broadcasted_iota(jnp.int32, (n, 1), 0)
    col = lax.broadcasted_iota(jnp.int32, (1, n), 1)
    ops = [(qq.astype(BF16), k.astype(BF16), 0)]
    for h, sh, p, r in levels:
        up = (row >> sh) & 1
        qt = jnp.where(up != isb, qq * p, 0.0).astype(BF16)
        kt = jnp.where(up == isb, k * r, 0.0).astype(BF16)
        ops.append((qt, kt, sh + 1))
    qc = (qq * pfull).astype(BF16)
    v = v_ref[...]
    vb = v.astype(BF16)
    lane = lax.broadcasted_iota(jnp.int32, (1, BRANCH), 1)
    latent = i >= N_CTX_SEQ
    acc = jnp.zeros((n, BRANCH), F32)
    for hd in range(HG_HEADS):
        sl = slice(hd * HG_HEAD_W, (hd + 1) * HG_HEAD_W)
        s_in = jnp.concatenate([sf_ref[0, hd], sb_ref[0, hd]], axis=0)
        s_in = jnp.where(latent, s_in, 0.0).astype(BF16)
        halves = []
        for r0 in range(0, n, n // 2):
            rs = slice(r0, r0 + n // 2)
            a = jnp.zeros((n // 2, n), F32)
            for qt, kt, bs in ops:
                a = a + jnp.where((row[rs] >> bs) == (col >> bs), _dot_nt(qt[rs, sl], kt[:, sl]), 0.0)
            halves.append(_dot(a.astype(BF16), vb) + _dot(qc[rs, sl], s_in))
        o_h = jnp.concatenate(halves, axis=0)
        acc = jnp.where((lane >> 6) == hd, o_h, acc)
    sq = acc * acc
    ms = jnp.zeros((n, BRANCH), F32)
    for hd in range(HG_HEADS):
        hm = (lane >> 6) == hd
        ms = jnp.where(hm, jnp.sum(jnp.where(hm, sq, 0.0), axis=-1, keepdims=True), ms)
    o_ref[...] = acc * lax.rsqrt(ms * (1.0 / HG_DK) + EPS) * ng_ref[...] * _silu(g_ref[...])

    @pl.when(i < N_CTX_SEQ)
    def _():
        kt_full = k * rfull
        for hd in range(HG_HEADS):
            kth = kt_full[:, hd * HG_HEAD_W:(hd + 1) * HG_HEAD_W].T.astype(BF16)
            ds = _dot(kth, jnp.where((lane >> 6) == hd, v, 0.0).astype(BF16))
            fin_ref[0, hd] = ds[:, hd * HG_DK:(hd + 1) * HG_DK]


def _hg_main(z_c, s_f, s_b, lb, norm_g):
    half = (1, HG_HEADS, HG_DK, BRANCH)
    lat = lambda i: (jnp.maximum(i - N_CTX_SEQ, 0), 0, 0, 0)
    fin = (HG_HEADS, HG_HEAD_W, HG_DK)
    return pl.pallas_call(
        _hg_main_body,
        grid=(HG_CHUNKS,),
        in_specs=[pl.BlockSpec((HG_CHUNK, HG_W), lambda i: (i, 0)),
                  pl.BlockSpec((HG_CHUNK, HG_W), lambda i: (i, 1)),
                  pl.BlockSpec((HG_CHUNK, BRANCH), lambda i: (i, 4)),
                  pl.BlockSpec((HG_CHUNK, BRANCH), lambda i: (i, 5)),
                  pl.BlockSpec(half, lat),
                  pl.BlockSpec(half, lat),
                  pl.BlockSpec((1, HG_W), lambda i: (0, 0)),
                  pl.BlockSpec((1, BRANCH), lambda i: (0, 0))],
        out_specs=[pl.BlockSpec((HG_CHUNK, BRANCH), lambda i: (i, 0)),
                   pl.BlockSpec((1,) + fin, lambda i: (jnp.minimum(i, N_CTX_SEQ - 1), 0, 0, 0))],
        out_shape=[jax.ShapeDtypeStruct((N_TOK, BRANCH), F32),
                   jax.ShapeDtypeStruct((N_CTX_SEQ,) + fin, F32)],
        compiler_params=_cparams("arbitrary"),
        name="hg_main",
    )(z_c, z_c, z_c, z_c, s_f, s_b, lb, norm_g)


def _take_cols(w, plan):
    idx = np.concatenate([p[0] for p in plan]).astype(np.int32)
    sign = np.concatenate([np.broadcast_to(p[1], p[0].shape) for p in plan]).astype(np.float32)
    return jnp.take(w, jnp.asarray(idx), axis=-1) * jnp.asarray(sign)


def _zeros(n):
    return (np.zeros(n, np.int64), 0.0)


_IN_OFF = {}
_off = 0
for _name, _n in (("da_q", 256), ("da_k", 256), ("da_v", 256), ("da_g", 256), ("s5_u", 256), ("s5_g", 256),
                  ("hg_q", 256), ("hg_ff", 256), ("hg_fb", 256), ("hg_i", 256), ("hg_g", 256),
                  ("mla_cq", MLA_Q_RANK), ("mla_ckv", MLA_KV_RANK), ("mla_kr", MLA_ROPE), ("mla_g", 256)):
    _IN_OFF[_name] = np.arange(_off, _off + _n)
    _off += _n


def _rope_tables():
    t = np.arange(LAT_LEN)
    pos = np.stack([t // GRID_W, t % GRID_W], axis=1).astype(np.float32)
    inv_freq = (np.float32(ROPE_BASE) ** (-np.arange(8, dtype=np.float32) / np.float32(8))).astype(np.float32)
    r = np.arange(MLA_ROPE)
    ang = (pos[:, r // 16] * inv_freq[r % 8][None, :]).astype(np.float64)
    cos32, sin32 = np.cos(ang).astype(np.float32), np.sin(ang).astype(np.float32)
    lo = (np.arange(MLA_ROPE) % 16 < 8)[None, :]
    sin_lo32, sin_hi32 = np.where(lo, -sin32, 0.0).astype(np.float32), np.where(lo, 0.0, sin32).astype(np.float32)
    da_tabs = tuple(np.tile(x, (1, 8)) for x in (cos32, sin_lo32, sin_hi32))

    def head(x, fill):
        h = np.concatenate([np.full((LAT_LEN, MLA_NOPE), fill, np.float32), x,
                            np.full((LAT_LEN, MLA_HEAD_PAD - MLA_NOPE - MLA_ROPE), fill, np.float32)], axis=1)
        return np.concatenate([h, np.full((BIG_TILE, MLA_HEAD_PAD), fill, np.float32)], axis=0)
    k_tabs = (head(cos32, 1.0), head(sin_lo32, 0.0), head(sin_hi32, 0.0))
    return tuple(jnp.asarray(x) for x in da_tabs), tuple(jnp.asarray(x) for x in k_tabs)


def _mla_weights(w_uq, w_ukv, q_norm):
    hd = MLA_NOPE + MLA_ROPE
    pad_tail = _zeros(MLA_HEAD_PAD - hd)
    q_plan, k_plan, v_plan = [], [], []
    for h in range(MLA_HEADS):
        nope, rope = np.arange(h * hd, h * hd + MLA_NOPE), np.arange(h * hd + MLA_NOPE, (h + 1) * hd)
        q_plan += [(nope, 1.0), (rope, 1.0), pad_tail]
        k_plan += [(np.arange(h * 2 * MLA_NOPE, h * 2 * MLA_NOPE + MLA_NOPE), 1.0), _zeros(MLA_HEAD_PAD - MLA_NOPE)]
        v_plan += [(np.arange(h * 2 * MLA_NOPE + MLA_NOPE, (h + 1) * 2 * MLA_NOPE), 1.0)]
    pad_rows = lambda x: jnp.pad(x, ((0, 256 - MLA_Q_RANK), (0, 0))).astype(BF16)
    qn = jnp.pad(q_norm, (0, 256 - MLA_Q_RANK)).reshape(1, 256)
    return (pad_rows(_take_cols(w_uq, q_plan)), _take_cols(w_ukv, k_plan).astype(BF16),
            _take_cols(w_ukv, v_plan).astype(BF16), qn)


def _split_bf16(a):
    hi = a.astype(BF16)
    return hi, (a - hi.astype(F32)).astype(BF16)


def _dot_sel(a, sel):
    hi, lo = _split_bf16(a)
    sel = sel.astype(BF16)
    return _dot(hi, sel) + _dot(lo, sel)


def _dot_x3(a, b):
    a_hi, a_lo = _split_bf16(a)
    b_hi, b_lo = _split_bf16(b)
    return _dot(a_hi, b_hi) + _dot(a_hi, b_lo) + _dot(a_lo, b_hi)


def _s5_table_body(xy_ref, bb_ref, c_ref, ct_ref, mt_ref, bst_ref, cot_ref, a_ref):
    n, t, ch = S5_STATE, S5_CHUNK, S5_CH
    wide = 2 * S5_TAP
    tau_i = lax.broadcasted_iota(jnp.int32, (1, 128), 1)
    tau = tau_i.astype(F32)
    sel_row = lax.broadcasted_iota(jnp.int32, (128, 1), 0)

    def lag(width):
        return lax.broadcasted_iota(jnp.int32, (1, width), 1) >> 4

    def onehot(cond):
        return jnp.where(cond, 1.0, 0.0).astype(F32)
    j = lag(wide)
    e_z = (onehot((j <= t - 1) & (sel_row == t - 1 - j)), onehot((j >= t - 1) & (j <= 2 * t - 2) & (sel_row == j - (t - 1))))
    jc = lag(S5_TAP)
    e_c = (onehot(sel_row == jc + 1), onehot(sel_row == t - jc))
    ch_row = lax.broadcasted_iota(jnp.int32, (ch, 1), 0)
    tile_w = onehot((lax.broadcasted_iota(jnp.int32, (1, wide), 1) & (ch - 1)) == ch_row)
    tile_n = onehot((lax.broadcasted_iota(jnp.int32, (1, S5_TAP), 1) & (ch - 1)) == ch_row)

    for gi in range(S5_TABLE_GROUPS):
        xy = xy_ref[0, gi]
        z, cot_rows, klong = [], [], None
        for d in range(2):
            x, y = xy[:, 2 * d:2 * d + 1], xy[:, 2 * d + 1:2 * d + 2]
            mag = jnp.exp(jnp.where(tau_i <= t, tau, 0.0) * x)
            ang = jnp.where(tau_i <= t, tau, 0.0) * y
            p_re = jnp.where(tau_i <= t, mag * jnp.cos(ang), 0.0)
            p_im = jnp.where(tau_i <= t, mag * jnp.sin(ang), 0.0)
            a_ref[0, gi, 2 * d] = jnp.broadcast_to(p_re[:, t:t + 1], (n, 128))
            a_ref[0, gi, 2 * d + 1] = jnp.broadcast_to(p_im[:, t:t + 1], (n, 128))
            pz_re, pz_im = _dot_sel(p_re, e_z[d]), _dot_sel(p_im, e_z[d])
            b_re, b_im = _dot_sel(bb_ref[0, gi, 2 * d], tile_w), _dot_sel(bb_ref[0, gi, 2 * d + 1], tile_w)
            z_re, z_im = pz_re * b_re - pz_im * b_im, pz_re * b_im + pz_im * b_re
            z += [z_re, z_im]
            part = _dot_x3(c_ref[0, gi, 2 * d], z_re) - _dot_x3(c_ref[0, gi, 2 * d + 1], z_im)
            klong = part if klong is None else klong + part
            pc_re, pc_im = _dot_sel(p_re, e_c[d]), _dot_sel(p_im, e_c[d])
            c_re, c_im = _dot_sel(ct_ref[0, gi, 2 * d], tile_n), _dot_sel(ct_ref[0, gi, 2 * d + 1], tile_n)
            cot_rows += [c_re * pc_re - c_im * pc_im, -(c_re * pc_im + c_im * pc_re)]
        for tt in range(t):
            off = (t - 1 - tt) * ch
            win = klong if off == 0 else pltpu.roll(klong, wide - off, 1)
            mt_ref[0, gi, tt * ch:(tt + 1) * ch, :] = win[:, :S5_TAP].astype(BF16)
        back = pltpu.roll(z[2], wide - (t - 1) * ch, 1), pltpu.roll(z[3], wide - (t - 1) * ch, 1)
        for k, rows in enumerate((z[0], z[1], back[0], back[1])):
            bst_ref[0, gi, k * n:(k + 1) * n, :] = rows[:, :S5_TAP].astype(BF16)
        cot_ref[0, gi] = jnp.concatenate(cot_rows, axis=0).T.astype(BF16)


def _s5_tables(a_re, a_im, log_dt, b_re, b_im, c_re, c_im):
    nl, g, n, ch = a_re.shape[0], S5_GROUPS, S5_STATE, S5_CH
    step = jnp.exp(log_dt)[..., None]
    mag = jnp.exp(a_re * step)
    ab_re, ab_im = mag * jnp.cos(a_im * step), mag * jnp.sin(a_im * step)
    den = a_re * a_re + a_im * a_im
    f_re = ((ab_re - 1.0) * a_re + ab_im * a_im) / den
    f_im = (ab_im * a_re - (ab_re - 1.0) * a_im) / den
    bb_re = f_re[..., None] * b_re - f_im[..., None] * b_im
    bb_im = f_re[..., None] * b_im + f_im[..., None] * b_re
    by_group = lambda x: jnp.moveaxis(x, 1, 2)
    pair = lambda re, im: jnp.stack([by_group(re), by_group(im)], axis=3).reshape((nl, g, 4) + re.shape[3:])
    xy = jnp.stack([by_group(a_re * step), by_group(a_im * step)], axis=3).reshape(nl, g, 4, n)
    xy = jnp.pad(jnp.swapaxes(xy, 2, 3), ((0, 0), (0, 0), (0, 0), (0, 4)))
    gs = S5_TABLE_GROUPS
    mat = pl.BlockSpec((1, gs, S5_TAP, S5_TAP), lambda l, i: (l, i, 0, 0))
    return pl.pallas_call(
        _s5_table_body,
        grid=(nl, g // gs),
        in_specs=[pl.BlockSpec((1, gs, n, 8), lambda l, i: (l, i, 0, 0)),
                  pl.BlockSpec((1, gs, 4, n, ch), lambda l, i: (l, i, 0, 0, 0)),
                  pl.BlockSpec((1, gs, 4, ch, n), lambda l, i: (l, i, 0, 0, 0)),
                  pl.BlockSpec((1, gs, 4, n, ch), lambda l, i: (l, i, 0, 0, 0))],
        out_specs=[mat, mat, mat, pl.BlockSpec((1, gs, 4, n, 128), lambda l, i: (l, i, 0, 0, 0))],
        out_shape=[jax.ShapeDtypeStruct((nl, g, S5_TAP, S5_TAP), BF16)] * 3
        + [jax.ShapeDtypeStruct((nl, g, 4, n, 128), F32)],
        compiler_params=_cparams("parallel", "parallel"),
        name="s5_tables",
    )(xy, pair(bb_re, bb_im), pair(c_re, c_im), pair(jnp.swapaxes(c_re, -1, -2), jnp.swapaxes(c_im, -1, -2)))


def _s5_chunk_lanes(u):
    return u.reshape(S5_NCHUNK, S5_CHUNK, BRANCH).transpose(1, 2, 0)


def _s5_token_rows(y):
    return y.transpose(2, 0, 1).reshape(N_TOK, BRANCH)


def kernel(x_prompt, x_sample, cache_diff_k, cache_diff_v, state_s5, state_hgrn, cache_mla_ckv, cache_mla_krope, c, c_ctx, w_mod, b_mod, w_in, w_out, da_lambda, da_norm, s5_a_re, s5_a_im, s5_log_dt, s5_b_re, s5_b_im, s5_c_re, s5_c_im, s5_d, s5_w_glu, hg_lb, hg_norm, mla_q_norm, mla_w_uq, mla_kv_norm, mla_w_ukv, final_norm):
    lb_w = jax.nn.softmax(hg_lb.astype(F32), axis=0)
    lb_all = jnp.cumsum(lb_w, axis=0) - lb_w[0:1]
    c_rows = jnp.concatenate([c_ctx[None], c, jnp.zeros((8 - 1 - N_LAT_SEQ, D_MODEL), F32)], axis=0)
    mods = _modulation(c_rows, w_mod, b_mod)
    da_tabs, mla_tabs = _rope_tables()
    xs = (x_prompt.reshape(N_CTX, D_MODEL), x_sample.reshape(N_LAT, D_MODEL))
    new_k, new_v, new_s5, new_hg, new_ckv, new_kr = [], [], [], [], [], []
    s5_tabs = _s5_tables(s5_a_re, s5_a_im, s5_log_dt, s5_b_re, s5_b_im, s5_c_re, s5_c_im)
    w_all = _arrange_w_in(w_in)
    for l in range(DEPTH):
        mod = mods[l, :3].reshape(3, 3, D_MODEL)
        z_a, z_b, z_c, z_d, k_new, v_new, u_bf = _in_proj(xs, mod, w_all, l)

        lam_init = 0.8 - 0.6 * math.exp(-0.3 * l)
        kv_lat = _da_latent_kv(z_a, da_tabs,
                               cache_diff_k[:, l].reshape(N_LAT_SEQ, PAST_LEN, BRANCH),
                               cache_diff_v[:, l].reshape(N_LAT_SEQ, PAST_LEN, BRANCH))
        a_out = _da_attention(z_a, da_lambda[l], da_norm[l], lam_init, da_tabs, kv_lat)
        new_k.append(k_new)
        new_v.append(v_new)

        h0 = state_s5[:, l].transpose(2, 1, 4, 3, 0).reshape(S5_GROUPS, 4, S5_STATE, N_LAT_SEQ)
        h0 = jnp.pad(h0, ((0, 0), (0, 0), (0, 0), (0, 128 - N_LAT_SEQ)))
        y_all, fin = _s5_scan(_s5_chunk_lanes(u_bf), *s5_tabs, h0, l)
        b_out = (_s5_token_rows(y_all), z_b, s5_d[l].reshape(1, BRANCH), s5_w_glu[l].astype(BF16))
        fin = fin[..., :N_CTX_SEQ].reshape(S5_GROUPS, 2, 2, S5_STATE, N_CTX_SEQ)
        new_s5.append(fin.transpose(4, 1, 0, 3, 2))

        lb = jnp.concatenate([lb_all[l, 0].reshape(HG_HEADS, HG_DK), lb_all[l, 1].reshape(HG_HEADS, HG_DK)],
                             axis=-1).reshape(1, HG_W)
        head_eye = jnp.eye(HG_HEADS, dtype=F32)
        s0 = state_hgrn[:, l].transpose(0, 2, 1, 3, 4).reshape(N_LAT_SEQ, HG_HEADS, HG_HEAD_W, 1, HG_DK)
        s0 = (s0 * head_eye[None, :, None, :, None]).reshape(N_LAT_SEQ, HG_HEADS, HG_HEAD_W, BRANCH)
        s_f, s_b = _hg_states(z_c, lb, s0)
        c_out, s_fin = _hg_main(z_c, s_f, s_b, lb, jnp.tile(hg_norm[l].reshape(1, HG_DK), (1, HG_HEADS)))
        new_hg.append(s_fin.reshape(N_CTX_SEQ, HG_HEADS, 2, HG_DK, HG_DK).transpose(0, 2, 1, 3, 4))

        wq, wk, wv, qn = _mla_weights(mla_w_uq[l], mla_w_ukv[l], mla_q_norm[l])
        q, ckv_n, kr = _mla_prep(z_d, mla_tabs, qn, mla_kv_norm[l].reshape(1, MLA_KV_RANK), wq)
        kr_cache = jnp.pad(cache_mla_krope[:, l], ((0, 0), (0, 0), (MLA_NOPE, 128 - MLA_NOPE - MLA_ROPE)))
        k_ctx, v_ctx, k_lat, v_lat = _mla_kv(ckv_n, kr, cache_mla_ckv[:, l], kr_cache, wk, wv)
        d_out = _mla_attention(z_d, q, k_ctx, v_ctx, k_lat, v_lat)
        new_ckv.append(ckv_n[:N_CTX].reshape(N_CTX_SEQ, CTX_LEN, MLA_KV_RANK))
        new_kr.append(kr[:N_CTX, MLA_NOPE:MLA_NOPE + MLA_ROPE].reshape(N_CTX_SEQ, CTX_LEN, MLA_ROPE))

        xs = _out_proj(a_out, b_out, c_out, d_out, xs, mod, w_out, l,
                       final_norm.reshape(1, D_MODEL), final=(l == DEPTH - 1))
        xs = tuple(xs) if l == DEPTH - 1 else (xs,)
    y_prompt = xs[0].reshape(N_CTX_SEQ, CTX_LEN, D_MODEL)
    y_sample = xs[1].reshape(N_LAT_SEQ, LAT_LEN, D_MODEL)
    st = lambda parts: jnp.stack(parts, axis=1)
    heads_last = lambda kv: kv.reshape(N_CTX_SEQ, DEPTH, DA_HEADS, 2 * DA_QK, CTX_LEN).transpose(0, 1, 4, 2, 3)
    return (y_prompt, y_sample, heads_last(st(new_k)), heads_last(st(new_v)), st(new_s5), st(new_hg), st(new_ckv), st(new_kr))
```
